```python
import jax, jax.numpy as jnp
from jax import lax
import numpy as np

D_MODEL = 1024
BATCH = 8
SEQ = 4096
DEPTH = 4

CHUNK = 64
N_MIXERS = 3
CONV_A_WIDTH = 31
CONV_C_WIDTH = 3
SGU_BLOCK = 128
SGU_HEADS = 8
SGU_WIDTH = 2 * D_MODEL
SGU_GROUP = SGU_WIDTH // SGU_HEADS
D_FF = -(-8 * D_MODEL // (3 * 256)) * 256
D_PLE = 256
ALPHA = (2 * DEPTH) ** 0.25
BETA = (8 * DEPTH) ** -0.25
LN_EPS = 1e-5
N_A = (DEPTH + 2) // 3
N_B = (DEPTH + 1) // 3
N_C = DEPTH // 3

kernel_name = "interleaved_conv_sgu_shortconv_deepnorm_trunk"


def layer_norm(x, g, b):
    xf = x.astype(jnp.float32)
    mu = jnp.mean(xf, axis=-1, keepdims=True)
    xc = xf - mu
    var = jnp.mean(xc * xc, axis=-1, keepdims=True)
    y = xc * lax.rsqrt(var + LN_EPS) * g.astype(jnp.float32) + b.astype(jnp.float32)
    return y.astype(x.dtype)


def rms_norm(x, g):
    xf = x.astype(jnp.float32)
    y = xf * lax.rsqrt(jnp.mean(xf * xf, axis=-1, keepdims=True) + LN_EPS) * g.astype(jnp.float32)
    return y.astype(x.dtype)


def causal_depthwise_conv(x, w):
    k, c = w.shape
    return lax.conv_general_dilated(
        x, w[:, None, :].astype(x.dtype), window_strides=(1,), padding=[(k - 1, 0)],
        dimension_numbers=("NWC", "WIO", "NWC"), feature_group_count=c)


def mixer_conformer_conv(x, w_pw1, b_pw1, w_dw, b_dw, ln_g, ln_b, w_pw2):
    h = x @ w_pw1 + b_pw1
    a, g = jnp.split(h, 2, axis=-1)
    h = a * jax.nn.sigmoid(g)
    h = causal_depthwise_conv(h, w_dw) + b_dw
    h = jax.nn.silu(layer_norm(h, ln_g, ln_b))
    return h @ w_pw2


def mixer_sgu(x, w_in, b_in, ln_g, ln_b, w_s, b_s, w_out):
    bsz, seq, _ = x.shape
    z = jax.nn.gelu(x @ w_in + b_in)
    u, v = jnp.split(z, 2, axis=-1)
    v = layer_norm(v, ln_g, ln_b)
    v = v.reshape(bsz, seq // SGU_BLOCK, SGU_BLOCK, SGU_HEADS, SGU_GROUP)
    pos = jnp.arange(SGU_BLOCK) // CHUNK
    mask = pos[:, None] >= pos[None, :]
    w_m = jnp.where(mask[None], w_s, jnp.zeros_like(w_s))
    f = jnp.einsum("hts,bnshg->bnthg", w_m, v) + jnp.transpose(b_s)[None, None, :, :, None]
    return (u * f.reshape(bsz, seq, SGU_WIDTH)) @ w_out


def mixer_short_conv(x, w_in, w_conv, w_out):
    bg, cg, h = jnp.split(x @ w_in, 3, axis=-1)
    y = causal_depthwise_conv(cg * h, w_conv)
    return (bg * y) @ w_out


def swiglu(x, w_gate, w_up, w_down):
    return (jax.nn.silu(x @ w_gate) * (x @ w_up)) @ w_down


def _fwd_setup_inputs(seed: int = 0) -> dict:
    key = jax.random.key(seed)
    ks = jax.random.split(key, 32)
    f32 = jnp.float32

    def nrm(k, shape, scale):
        return jax.random.normal(k, shape, f32) * scale

    D, E = D_MODEL, SGU_WIDTH
    return {
        "x": nrm(ks[0], (BATCH, SEQ, D), 1.0),
        "p": nrm(ks[1], (DEPTH, BATCH, SEQ, D_PLE), 1.0),
        "a_w_pw1": nrm(ks[2], (N_A, D, 2 * D), D ** -0.5),
        "a_b_pw1": nrm(ks[3], (N_A, 2 * D), 0.02),
        "a_w_dw": nrm(ks[4], (N_A, CONV_A_WIDTH, D), CONV_A_WIDTH ** -0.5),
        "a_b_dw": nrm(ks[5], (N_A, D), 0.02),
        "a_ln_g": 1.0 + nrm(ks[6], (N_A, D), 0.02),
        "a_ln_b": nrm(ks[7], (N_A, D), 0.02),
        "a_w_pw2": nrm(ks[8], (N_A, D, D), BETA * D ** -0.5),
        "b_w_in": nrm(ks[9], (N_B, D, 2 * E), D ** -0.5),
        "b_b_in": nrm(ks[10], (N_B, 2 * E), 0.02),
        "b_ln_g": 1.0 + nrm(ks[11], (N_B, E), 0.02),
        "b_ln_b": nrm(ks[12], (N_B, E), 0.02),
        "b_w_s": nrm(ks[13], (N_B, SGU_HEADS, SGU_BLOCK, SGU_BLOCK), SGU_BLOCK ** -0.5),
        "b_b_s": 1.0 + nrm(ks[14], (N_B, SGU_HEADS, SGU_BLOCK), 0.02),
        "b_w_out": nrm(ks[15], (N_B, E, D), BETA * E ** -0.5),
        "c_w_in": nrm(ks[16], (N_C, D, 3 * D), D ** -0.5),
        "c_w_conv": nrm(ks[17], (N_C, CONV_C_WIDTH, D), CONV_C_WIDTH ** -0.5),
        "c_w_out": nrm(ks[18], (N_C, D, D), BETA * D ** -0.5),
        "ln1_g": 1.0 + nrm(ks[19], (DEPTH, D), 0.02),
        "ln1_b": nrm(ks[20], (DEPTH, D), 0.02),
        "ln2_g": 1.0 + nrm(ks[21], (DEPTH, D), 0.02),
        "ln2_b": nrm(ks[22], (DEPTH, D), 0.02),
        "ffn_w_gate": nrm(ks[23], (DEPTH, D, D_FF), D ** -0.5),
        "ffn_w_up": nrm(ks[24], (DEPTH, D, D_FF), D ** -0.5),
        "ffn_w_down": nrm(ks[25], (DEPTH, D_FF, D), BETA * D_FF ** -0.5),
        "ple_w_gate": nrm(ks[26], (DEPTH, D, D), D ** -0.5),
        "ple_w_proj": nrm(ks[27], (DEPTH, D_PLE, D), D_PLE ** -0.5),
        "ple_norm_g": 1.0 + nrm(ks[28], (DEPTH, D), 0.02),
    }


def _fwd_reference(x, p,
              a_w_pw1, a_b_pw1, a_w_dw, a_b_dw, a_ln_g, a_ln_b, a_w_pw2,
              b_w_in, b_b_in, b_ln_g, b_ln_b, b_w_s, b_b_s, b_w_out,
              c_w_in, c_w_conv, c_w_out,
              ln1_g, ln1_b, ln2_g, ln2_b,
              ffn_w_gate, ffn_w_up, ffn_w_down,
              ple_w_gate, ple_w_proj, ple_norm_g):
    for i in range(DEPTH):
        m, j = i % N_MIXERS, i // N_MIXERS
        if m == 0:
            h = mixer_conformer_conv(x, a_w_pw1[j], a_b_pw1[j], a_w_dw[j], a_b_dw[j],
                                     a_ln_g[j], a_ln_b[j], a_w_pw2[j])
        elif m == 1:
            h = mixer_sgu(x, b_w_in[j], b_b_in[j], b_ln_g[j], b_ln_b[j],
                          b_w_s[j], b_b_s[j], b_w_out[j])
        else:
            h = mixer_short_conv(x, c_w_in[j], c_w_conv[j], c_w_out[j])
        x = layer_norm(ALPHA * x + h, ln1_g[i], ln1_b[i])
        x = layer_norm(ALPHA * x + swiglu(x, ffn_w_gate[i], ffn_w_up[i], ffn_w_down[i]),
                       ln2_g[i], ln2_b[i])
        gate = jax.nn.sigmoid(x @ ple_w_gate[i])
        x = x + gate * rms_norm(p[i] @ ple_w_proj[i], ple_norm_g[i])
    return x


import jax as _jax
import jax.numpy as _jnp

TWIN_FORMAT = 'train_step'
FWD_PARAMS = ['x', 'p', 'a_w_pw1', 'a_b_pw1', 'a_w_dw', 'a_b_dw', 'a_ln_g', 'a_ln_b', 'a_w_pw2', 'b_w_in', 'b_b_in', 'b_ln_g', 'b_ln_b', 'b_w_s', 'b_b_s', 'b_w_out', 'c_w_in', 'c_w_conv', 'c_w_out', 'ln1_g', 'ln1_b', 'ln2_g', 'ln2_b', 'ffn_w_gate', 'ffn_w_up', 'ffn_w_down', 'ple_w_gate', 'ple_w_proj', 'ple_norm_g']
TWIN_WEIGHTS = ['a_w_pw1', 'a_b_pw1', 'a_w_dw', 'a_b_dw', 'a_ln_g', 'a_ln_b', 'a_w_pw2', 'b_w_in', 'b_b_in', 'b_ln_g', 'b_ln_b', 'b_w_s', 'b_b_s', 'b_w_out', 'c_w_in', 'c_w_conv', 'c_w_out', 'ln1_g', 'ln1_b', 'ln2_g', 'ln2_b', 'ffn_w_gate', 'ffn_w_up', 'ffn_w_down', 'ple_w_gate', 'ple_w_proj', 'ple_norm_g']
TWIN_DIFF_INPUT = 'x'
TWIN_INPUTS = ['x', 'p', 'a_w_pw1', 'a_b_pw1', 'a_w_dw', 'a_b_dw', 'a_ln_g', 'a_ln_b', 'a_w_pw2', 'b_w_in', 'b_b_in', 'b_ln_g', 'b_ln_b', 'b_w_s', 'b_b_s', 'b_w_out', 'c_w_in', 'c_w_conv', 'c_w_out', 'ln1_g', 'ln1_b', 'ln2_g', 'ln2_b', 'ffn_w_gate', 'ffn_w_up', 'ffn_w_down', 'ple_w_gate', 'ple_w_proj', 'ple_norm_g', 'loss_target', 'm_a_w_pw1', 'm_a_b_pw1', 'm_a_w_dw', 'm_a_b_dw', 'm_a_ln_g', 'm_a_ln_b', 'm_a_w_pw2', 'm_b_w_in', 'm_b_b_in', 'm_b_ln_g', 'm_b_ln_b', 'm_b_w_s', 'm_b_b_s', 'm_b_w_out', 'm_c_w_in', 'm_c_w_conv', 'm_c_w_out', 'm_ln1_g', 'm_ln1_b', 'm_ln2_g', 'm_ln2_b', 'm_ffn_w_gate', 'm_ffn_w_up', 'm_ffn_w_down', 'm_ple_w_gate', 'm_ple_w_proj', 'm_ple_norm_g', 'v_a_w_pw1', 'v_a_b_pw1', 'v_a_w_dw', 'v_a_b_dw', 'v_a_ln_g', 'v_a_ln_b', 'v_a_w_pw2', 'v_b_w_in', 'v_b_b_in', 'v_b_ln_g', 'v_b_ln_b', 'v_b_w_s', 'v_b_b_s', 'v_b_w_out', 'v_c_w_in', 'v_c_w_conv', 'v_c_w_out', 'v_ln1_g', 'v_ln1_b', 'v_ln2_g', 'v_ln2_b', 'v_ffn_w_gate', 'v_ffn_w_up', 'v_ffn_w_down', 'v_ple_w_gate', 'v_ple_w_proj', 'v_ple_norm_g']
TWIN_OUTPUTS = ['loss', 'grad_x', 'grad_a_w_pw1', 'grad_a_b_pw1', 'grad_a_w_dw', 'grad_a_b_dw', 'grad_a_ln_g', 'grad_a_ln_b', 'grad_a_w_pw2', 'grad_b_w_in', 'grad_b_b_in', 'grad_b_ln_g', 'grad_b_ln_b', 'grad_b_w_s', 'grad_b_b_s', 'grad_b_w_out', 'grad_c_w_in', 'grad_c_w_conv', 'grad_c_w_out', 'grad_ln1_g', 'grad_ln1_b', 'grad_ln2_g', 'grad_ln2_b', 'grad_ffn_w_gate', 'grad_ffn_w_up', 'grad_ffn_w_down', 'grad_ple_w_gate', 'grad_ple_w_proj', 'grad_ple_norm_g', 'delta_a_w_pw1', 'delta_a_b_pw1', 'delta_a_w_dw', 'delta_a_b_dw', 'delta_a_ln_g', 'delta_a_ln_b', 'delta_a_w_pw2', 'delta_b_w_in', 'delta_b_b_in', 'delta_b_ln_g', 'delta_b_ln_b', 'delta_b_w_s', 'delta_b_b_s', 'delta_b_w_out', 'delta_c_w_in', 'delta_c_w_conv', 'delta_c_w_out', 'delta_ln1_g', 'delta_ln1_b', 'delta_ln2_g', 'delta_ln2_b', 'delta_ffn_w_gate', 'delta_ffn_w_up', 'delta_ffn_w_down', 'delta_ple_w_gate', 'delta_ple_w_proj', 'delta_ple_norm_g', 'new_m_a_w_pw1', 'new_m_a_b_pw1', 'new_m_a_w_dw', 'new_m_a_b_dw', 'new_m_a_ln_g', 'new_m_a_ln_b', 'new_m_a_w_pw2', 'new_m_b_w_in', 'new_m_b_b_in', 'new_m_b_ln_g', 'new_m_b_ln_b', 'new_m_b_w_s', 'new_m_b_b_s', 'new_m_b_w_out', 'new_m_c_w_in', 'new_m_c_w_conv', 'new_m_c_w_out', 'new_m_ln1_g', 'new_m_ln1_b', 'new_m_ln2_g', 'new_m_ln2_b', 'new_m_ffn_w_gate', 'new_m_ffn_w_up', 'new_m_ffn_w_down', 'new_m_ple_w_gate', 'new_m_ple_w_proj', 'new_m_ple_norm_g', 'new_v_a_w_pw1', 'new_v_a_b_pw1', 'new_v_a_w_dw', 'new_v_a_b_dw', 'new_v_a_ln_g', 'new_v_a_ln_b', 'new_v_a_w_pw2', 'new_v_b_w_in', 'new_v_b_b_in', 'new_v_b_ln_g', 'new_v_b_ln_b', 'new_v_b_w_s', 'new_v_b_b_s', 'new_v_b_w_out', 'new_v_c_w_in', 'new_v_c_w_conv', 'new_v_c_w_out', 'new_v_ln1_g', 'new_v_ln1_b', 'new_v_ln2_g', 'new_v_ln2_b', 'new_v_ffn_w_gate', 'new_v_ffn_w_up', 'new_v_ffn_w_down', 'new_v_ple_w_gate', 'new_v_ple_w_proj', 'new_v_ple_norm_g']
TWIN_LEAF_KINDS = {'loss': 'loss', 'grad_x': 'grad_x', 'grad_a_w_pw1': 'grad_w', 'grad_a_b_pw1': 'grad_w', 'grad_a_w_dw': 'grad_w', 'grad_a_b_dw': 'grad_w', 'grad_a_ln_g': 'grad_w', 'grad_a_ln_b': 'grad_w', 'grad_a_w_pw2': 'grad_w', 'grad_b_w_in': 'grad_w', 'grad_b_b_in': 'grad_w', 'grad_b_ln_g': 'grad_w', 'grad_b_ln_b': 'grad_w', 'grad_b_w_s': 'grad_w', 'grad_b_b_s': 'grad_w', 'grad_b_w_out': 'grad_w', 'grad_c_w_in': 'grad_w', 'grad_c_w_conv': 'grad_w', 'grad_c_w_out': 'grad_w', 'grad_ln1_g': 'grad_w', 'grad_ln1_b': 'grad_w', 'grad_ln2_g': 'grad_w', 'grad_ln2_b': 'grad_w', 'grad_ffn_w_gate': 'grad_w', 'grad_ffn_w_up': 'grad_w', 'grad_ffn_w_down': 'grad_w', 'grad_ple_w_gate': 'grad_w', 'grad_ple_w_proj': 'grad_w', 'grad_ple_norm_g': 'grad_w', 'delta_a_w_pw1': 'delta_w', 'delta_a_b_pw1': 'delta_w', 'delta_a_w_dw': 'delta_w', 'delta_a_b_dw': 'delta_w', 'delta_a_ln_g': 'delta_w', 'delta_a_ln_b': 'delta_w', 'delta_a_w_pw2': 'delta_w', 'delta_b_w_in': 'delta_w', 'delta_b_b_in': 'delta_w', 'delta_b_ln_g': 'delta_w', 'delta_b_ln_b': 'delta_w', 'delta_b_w_s': 'delta_w', 'delta_b_b_s': 'delta_w', 'delta_b_w_out': 'delta_w', 'delta_c_w_in': 'delta_w', 'delta_c_w_conv': 'delta_w', 'delta_c_w_out': 'delta_w', 'delta_ln1_g': 'delta_w', 'delta_ln1_b': 'delta_w', 'delta_ln2_g': 'delta_w', 'delta_ln2_b': 'delta_w', 'delta_ffn_w_gate': 'delta_w', 'delta_ffn_w_up': 'delta_w', 'delta_ffn_w_down': 'delta_w', 'delta_ple_w_gate': 'delta_w', 'delta_ple_w_proj': 'delta_w', 'delta_ple_norm_g': 'delta_w', 'new_m_a_w_pw1': 'new_m', 'new_m_a_b_pw1': 'new_m', 'new_m_a_w_dw': 'new_m', 'new_m_a_b_dw': 'new_m', 'new_m_a_ln_g': 'new_m', 'new_m_a_ln_b': 'new_m', 'new_m_a_w_pw2': 'new_m', 'new_m_b_w_in': 'new_m', 'new_m_b_b_in': 'new_m', 'new_m_b_ln_g': 'new_m', 'new_m_b_ln_b': 'new_m', 'new_m_b_w_s': 'new_m', 'new_m_b_b_s': 'new_m', 'new_m_b_w_out': 'new_m', 'new_m_c_w_in': 'new_m', 'new_m_c_w_conv': 'new_m', 'new_m_c_w_out': 'new_m', 'new_m_ln1_g': 'new_m', 'new_m_ln1_b': 'new_m', 'new_m_ln2_g': 'new_m', 'new_m_ln2_b': 'new_m', 'new_m_ffn_w_gate': 'new_m', 'new_m_ffn_w_up': 'new_m', 'new_m_ffn_w_down': 'new_m', 'new_m_ple_w_gate': 'new_m', 'new_m_ple_w_proj': 'new_m', 'new_m_ple_norm_g': 'new_m', 'new_v_a_w_pw1': 'new_v', 'new_v_a_b_pw1': 'new_v', 'new_v_a_w_dw': 'new_v', 'new_v_a_b_dw': 'new_v', 'new_v_a_ln_g': 'new_v', 'new_v_a_ln_b': 'new_v', 'new_v_a_w_pw2': 'new_v', 'new_v_b_w_in': 'new_v', 'new_v_b_b_in': 'new_v', 'new_v_b_ln_g': 'new_v', 'new_v_b_ln_b': 'new_v', 'new_v_b_w_s': 'new_v', 'new_v_b_b_s': 'new_v', 'new_v_b_w_out': 'new_v', 'new_v_c_w_in': 'new_v', 'new_v_c_w_conv': 'new_v', 'new_v_c_w_out': 'new_v', 'new_v_ln1_g': 'new_v', 'new_v_ln1_b': 'new_v', 'new_v_ln2_g': 'new_v', 'new_v_ln2_b': 'new_v', 'new_v_ffn_w_gate': 'new_v', 'new_v_ffn_w_up': 'new_v', 'new_v_ffn_w_down': 'new_v', 'new_v_ple_w_gate': 'new_v', 'new_v_ple_w_proj': 'new_v', 'new_v_ple_norm_g': 'new_v'}


def _forward(args):
    return _fwd_reference(*[args[k] for k in FWD_PARAMS])


def _output_shape():
    def fwd():
        inp = _fwd_setup_inputs(0)
        return _fwd_reference(*[inp[k] for k in FWD_PARAMS])
    out = _jax.eval_shape(fwd)
    return out.shape, out.dtype

N_MICROBATCH = 1
ADAM_LR = 0.001
ADAM_B1 = 0.9
ADAM_B2 = 0.999
ADAM_EPS = 1e-08
ADAM_WD = 0.01
ADAM_STEP = 10
PER_EXAMPLE_BATCH_AXIS = {'x': 0, 'p': 1, 'loss_target': 0}
SHARED_INPUTS = []
_WEIGHT_DTYPES = {'a_w_pw1': _jnp.float32, 'a_b_pw1': _jnp.float32, 'a_w_dw': _jnp.float32, 'a_b_dw': _jnp.float32, 'a_ln_g': _jnp.float32, 'a_ln_b': _jnp.float32, 'a_w_pw2': _jnp.float32, 'b_w_in': _jnp.float32, 'b_b_in': _jnp.float32, 'b_ln_g': _jnp.float32, 'b_ln_b': _jnp.float32, 'b_w_s': _jnp.float32, 'b_b_s': _jnp.float32, 'b_w_out': _jnp.float32, 'c_w_in': _jnp.float32, 'c_w_conv': _jnp.float32, 'c_w_out': _jnp.float32, 'ln1_g': _jnp.float32, 'ln1_b': _jnp.float32, 'ln2_g': _jnp.float32, 'ln2_b': _jnp.float32, 'ffn_w_gate': _jnp.float32, 'ffn_w_up': _jnp.float32, 'ffn_w_down': _jnp.float32, 'ple_w_gate': _jnp.float32, 'ple_w_proj': _jnp.float32, 'ple_norm_g': _jnp.float32}
MOMENT_SCALE = {'a_w_pw1': 2.359615e-02, 'a_b_pw1': 2.002268e-01, 'a_w_dw': 4.180169e-02, 'a_b_dw': 4.909865e-01, 'a_ln_g': 2.100207e-01, 'a_ln_b': 3.076087e-01, 'a_w_pw2': 2.913492e-01, 'b_w_in': 2.726679e-02, 'b_b_in': 1.237643e-01, 'b_ln_g': 1.913393e-02, 'b_ln_b': 2.098461e-02, 'b_w_s': 2.715083e-02, 'b_b_s': 3.567320e-02, 'b_w_out': 3.845314e-01, 'c_w_in': 6.025804e-02, 'c_w_conv': 6.010893e-02, 'c_w_out': 1.434834e-01, 'ln1_g': 8.219061e-01, 'ln1_b': 2.721855e+00, 'ln2_g': 1.658837e+01, 'ln2_b': 3.062842e+00, 'ffn_w_gate': 1.838338e-02, 'ffn_w_up': 1.848987e-02, 'ffn_w_down': 7.268003e-02, 'ple_w_gate': 1.273794e-01, 'ple_w_proj': 1.121718e-01, 'ple_norm_g': 4.725614e+00}


def _to_microbatches(a, axis):
    t = _jnp.moveaxis(a, axis, 0)
    t = t.reshape((N_MICROBATCH, t.shape[0] // N_MICROBATCH) + t.shape[1:])
    return _jnp.moveaxis(t, 1, axis + 1)


def setup_inputs(seed: int = 0) -> dict:
    inp = _fwd_setup_inputs(seed)
    key = _jax.random.fold_in(_jax.random.key(seed), 7919)
    shape, _ = _output_shape()
    out = dict(inp)
    out["loss_target"] = _jax.random.normal(_jax.random.fold_in(key, 0), shape, _jnp.float32)
    for i, name in enumerate(TWIN_WEIGHTS):
        w = inp[name].astype(_jnp.float32)
        if MOMENT_SCALE is None:
            s = _jnp.sqrt(_jnp.mean(_jnp.square(w)) + 1e-30)
        else:
            s = MOMENT_SCALE[name]
        km, kv = _jax.random.split(_jax.random.fold_in(key, i + 1))
        out[name] = w
        out["m_" + name] = s * _jax.random.normal(km, w.shape, _jnp.float32)
        out["v_" + name] = (s * s) * _jax.random.uniform(kv, w.shape, _jnp.float32, 0.5, 1.5)
    if N_MICROBATCH > 1:
        for name, axis in PER_EXAMPLE_BATCH_AXIS.items():
            out[name] = _to_microbatches(out[name], axis)
    return {'x': out['x'], 'p': out['p'], 'a_w_pw1': out['a_w_pw1'], 'a_b_pw1': out['a_b_pw1'], 'a_w_dw': out['a_w_dw'], 'a_b_dw': out['a_b_dw'], 'a_ln_g': out['a_ln_g'], 'a_ln_b': out['a_ln_b'], 'a_w_pw2': out['a_w_pw2'], 'b_w_in': out['b_w_in'], 'b_b_in': out['b_b_in'], 'b_ln_g': out['b_ln_g'], 'b_ln_b': out['b_ln_b'], 'b_w_s': out['b_w_s'], 'b_b_s': out['b_b_s'], 'b_w_out': out['b_w_out'], 'c_w_in': out['c_w_in'], 'c_w_conv': out['c_w_conv'], 'c_w_out': out['c_w_out'], 'ln1_g': out['ln1_g'], 'ln1_b': out['ln1_b'], 'ln2_g': out['ln2_g'], 'ln2_b': out['ln2_b'], 'ffn_w_gate': out['ffn_w_gate'], 'ffn_w_up': out['ffn_w_up'], 'ffn_w_down': out['ffn_w_down'], 'ple_w_gate': out['ple_w_gate'], 'ple_w_proj': out['ple_w_proj'], 'ple_norm_g': out['ple_norm_g'], 'loss_target': out['loss_target'], 'm_a_w_pw1': out['m_a_w_pw1'], 'm_a_b_pw1': out['m_a_b_pw1'], 'm_a_w_dw': out['m_a_w_dw'], 'm_a_b_dw': out['m_a_b_dw'], 'm_a_ln_g': out['m_a_ln_g'], 'm_a_ln_b': out['m_a_ln_b'], 'm_a_w_pw2': out['m_a_w_pw2'], 'm_b_w_in': out['m_b_w_in'], 'm_b_b_in': out['m_b_b_in'], 'm_b_ln_g': out['m_b_ln_g'], 'm_b_ln_b': out['m_b_ln_b'], 'm_b_w_s': out['m_b_w_s'], 'm_b_b_s': out['m_b_b_s'], 'm_b_w_out': out['m_b_w_out'], 'm_c_w_in': out['m_c_w_in'], 'm_c_w_conv': out['m_c_w_conv'], 'm_c_w_out': out['m_c_w_out'], 'm_ln1_g': out['m_ln1_g'], 'm_ln1_b': out['m_ln1_b'], 'm_ln2_g': out['m_ln2_g'], 'm_ln2_b': out['m_ln2_b'], 'm_ffn_w_gate': out['m_ffn_w_gate'], 'm_ffn_w_up': out['m_ffn_w_up'], 'm_ffn_w_down': out['m_ffn_w_down'], 'm_ple_w_gate': out['m_ple_w_gate'], 'm_ple_w_proj': out['m_ple_w_proj'], 'm_ple_norm_g': out['m_ple_norm_g'], 'v_a_w_pw1': out['v_a_w_pw1'], 'v_a_b_pw1': out['v_a_b_pw1'], 'v_a_w_dw': out['v_a_w_dw'], 'v_a_b_dw': out['v_a_b_dw'], 'v_a_ln_g': out['v_a_ln_g'], 'v_a_ln_b': out['v_a_ln_b'], 'v_a_w_pw2': out['v_a_w_pw2'], 'v_b_w_in': out['v_b_w_in'], 'v_b_b_in': out['v_b_b_in'], 'v_b_ln_g': out['v_b_ln_g'], 'v_b_ln_b': out['v_b_ln_b'], 'v_b_w_s': out['v_b_w_s'], 'v_b_b_s': out['v_b_b_s'], 'v_b_w_out': out['v_b_w_out'], 'v_c_w_in': out['v_c_w_in'], 'v_c_w_conv': out['v_c_w_conv'], 'v_c_w_out': out['v_c_w_out'], 'v_ln1_g': out['v_ln1_g'], 'v_ln1_b': out['v_ln1_b'], 'v_ln2_g': out['v_ln2_g'], 'v_ln2_b': out['v_ln2_b'], 'v_ffn_w_gate': out['v_ffn_w_gate'], 'v_ffn_w_up': out['v_ffn_w_up'], 'v_ffn_w_down': out['v_ffn_w_down'], 'v_ple_w_gate': out['v_ple_w_gate'], 'v_ple_w_proj': out['v_ple_w_proj'], 'v_ple_norm_g': out['v_ple_norm_g']}


def _loss(weights, diff, rest, loss_target):
    with _jax.named_scope("forward"):
        args = {**rest, TWIN_DIFF_INPUT: diff, **{k: w.astype(_WEIGHT_DTYPES[k]) for k, w in weights.items()}}
        y = _forward(args)
    with _jax.named_scope("loss_head"):
        err = _jnp.square(y.astype(_jnp.float32) - loss_target)
        return 0.5 * _jnp.sum(_jnp.mean(err, axis=-1)) if err.ndim else 0.5 * err


def _adamw(w, g, m, v):
    m = ADAM_B1 * m + (1.0 - ADAM_B1) * g
    v = ADAM_B2 * v + (1.0 - ADAM_B2) * _jnp.square(g)
    m_hat = m / (1.0 - ADAM_B1 ** ADAM_STEP)
    v_hat = v / (1.0 - ADAM_B2 ** ADAM_STEP)
    delta = -ADAM_LR * (m_hat / (_jnp.sqrt(v_hat) + ADAM_EPS) + ADAM_WD * w)
    return delta, m, v


def reference(x, p, a_w_pw1, a_b_pw1, a_w_dw, a_b_dw, a_ln_g, a_ln_b, a_w_pw2, b_w_in, b_b_in, b_ln_g, b_ln_b, b_w_s, b_b_s, b_w_out, c_w_in, c_w_conv, c_w_out, ln1_g, ln1_b, ln2_g, ln2_b, ffn_w_gate, ffn_w_up, ffn_w_down, ple_w_gate, ple_w_proj, ple_norm_g, loss_target, m_a_w_pw1, m_a_b_pw1, m_a_w_dw, m_a_b_dw, m_a_ln_g, m_a_ln_b, m_a_w_pw2, m_b_w_in, m_b_b_in, m_b_ln_g, m_b_ln_b, m_b_w_s, m_b_b_s, m_b_w_out, m_c_w_in, m_c_w_conv, m_c_w_out, m_ln1_g, m_ln1_b, m_ln2_g, m_ln2_b, m_ffn_w_gate, m_ffn_w_up, m_ffn_w_down, m_ple_w_gate, m_ple_w_proj, m_ple_norm_g, v_a_w_pw1, v_a_b_pw1, v_a_w_dw, v_a_b_dw, v_a_ln_g, v_a_ln_b, v_a_w_pw2, v_b_w_in, v_b_b_in, v_b_ln_g, v_b_ln_b, v_b_w_s, v_b_b_s, v_b_w_out, v_c_w_in, v_c_w_conv, v_c_w_out, v_ln1_g, v_ln1_b, v_ln2_g, v_ln2_b, v_ffn_w_gate, v_ffn_w_up, v_ffn_w_down, v_ple_w_gate, v_ple_w_proj, v_ple_norm_g):
    given = dict(x=x, p=p, a_w_pw1=a_w_pw1, a_b_pw1=a_b_pw1, a_w_dw=a_w_dw, a_b_dw=a_b_dw, a_ln_g=a_ln_g, a_ln_b=a_ln_b, a_w_pw2=a_w_pw2, b_w_in=b_w_in, b_b_in=b_b_in, b_ln_g=b_ln_g, b_ln_b=b_ln_b, b_w_s=b_w_s, b_b_s=b_b_s, b_w_out=b_w_out, c_w_in=c_w_in, c_w_conv=c_w_conv, c_w_out=c_w_out, ln1_g=ln1_g, ln1_b=ln1_b, ln2_g=ln2_g, ln2_b=ln2_b, ffn_w_gate=ffn_w_gate, ffn_w_up=ffn_w_up, ffn_w_down=ffn_w_down, ple_w_gate=ple_w_gate, ple_w_proj=ple_w_proj, ple_norm_g=ple_norm_g, loss_target=loss_target, m_a_w_pw1=m_a_w_pw1, m_a_b_pw1=m_a_b_pw1, m_a_w_dw=m_a_w_dw, m_a_b_dw=m_a_b_dw, m_a_ln_g=m_a_ln_g, m_a_ln_b=m_a_ln_b, m_a_w_pw2=m_a_w_pw2, m_b_w_in=m_b_w_in, m_b_b_in=m_b_b_in, m_b_ln_g=m_b_ln_g, m_b_ln_b=m_b_ln_b, m_b_w_s=m_b_w_s, m_b_b_s=m_b_b_s, m_b_w_out=m_b_w_out, m_c_w_in=m_c_w_in, m_c_w_conv=m_c_w_conv, m_c_w_out=m_c_w_out, m_ln1_g=m_ln1_g, m_ln1_b=m_ln1_b, m_ln2_g=m_ln2_g, m_ln2_b=m_ln2_b, m_ffn_w_gate=m_ffn_w_gate, m_ffn_w_up=m_ffn_w_up, m_ffn_w_down=m_ffn_w_down, m_ple_w_gate=m_ple_w_gate, m_ple_w_proj=m_ple_w_proj, m_ple_norm_g=m_ple_norm_g, v_a_w_pw1=v_a_w_pw1, v_a_b_pw1=v_a_b_pw1, v_a_w_dw=v_a_w_dw, v_a_b_dw=v_a_b_dw, v_a_ln_g=v_a_ln_g, v_a_ln_b=v_a_ln_b, v_a_w_pw2=v_a_w_pw2, v_b_w_in=v_b_w_in, v_b_b_in=v_b_b_in, v_b_ln_g=v_b_ln_g, v_b_ln_b=v_b_ln_b, v_b_w_s=v_b_w_s, v_b_b_s=v_b_b_s, v_b_w_out=v_b_w_out, v_c_w_in=v_c_w_in, v_c_w_conv=v_c_w_conv, v_c_w_out=v_c_w_out, v_ln1_g=v_ln1_g, v_ln1_b=v_ln1_b, v_ln2_g=v_ln2_g, v_ln2_b=v_ln2_b, v_ffn_w_gate=v_ffn_w_gate, v_ffn_w_up=v_ffn_w_up, v_ffn_w_down=v_ffn_w_down, v_ple_w_gate=v_ple_w_gate, v_ple_w_proj=v_ple_w_proj, v_ple_norm_g=v_ple_norm_g)
    weights = {n: given[n] for n in TWIN_WEIGHTS}
    shared = {n: given[n] for n in SHARED_INPUTS}
    per_example = {n: given[n] for n in ['x', 'p']}
    grad_fn = _jax.value_and_grad(_loss, argnums=(0, 1))

    def one_microbatch(ex, loss_target):
        ex = dict(ex)
        diff = ex.pop(TWIN_DIFF_INPUT)
        return grad_fn(weights, diff, {**shared, **ex}, loss_target)

    if N_MICROBATCH == 1:
        loss, (grad_w, grad_x) = one_microbatch(per_example, given["loss_target"])
    else:
        def body(carry, xs):
            loss_sum, grad_sum = carry
            l_k, (gw_k, gx_k) = one_microbatch(xs[0], xs[1])
            with _jax.named_scope("update"):
                return (loss_sum + l_k, _jax.tree.map(_jnp.add, grad_sum, gw_k)), gx_k

        init = (_jnp.zeros((), _jnp.float32), _jax.tree.map(_jnp.zeros_like, weights))
        (loss, grad_w), grad_x = _jax.lax.scan(body, init, (per_example, given["loss_target"]))
    with _jax.named_scope("update"):
        delta_w, new_m, new_v = {}, {}, {}
        for n in TWIN_WEIGHTS:
            delta_w[n], new_m[n], new_v[n] = _adamw(weights[n], grad_w[n], given["m_" + n], given["v_" + n])
    return (loss, grad_x, *[grad_w[n] for n in TWIN_WEIGHTS], *[delta_w[n] for n in TWIN_WEIGHTS],
            *[new_m[n] for n in TWIN_WEIGHTS], *[new_v[n] for n in TWIN_WEIGHTS])
```

```python
import jax
import jax.numpy as jnp
from jax import lax
from jax.experimental import pallas as pl
from jax.experimental.pallas import tpu as pltpu

F32, BF16 = jnp.float32, jnp.bfloat16
S = 4096
D = 1024
E = 2048
FF = 2816
FQ = FF // 4
NQ = 4
DEPTH = 4
ALPHA = (2 * DEPTH) ** 0.25
LN_EPS = 1e-5
CONV_A, CONV_C = 31, 3
HALO_A, HALO_C = 32, 8
SGU_T, SGU_H, SGU_G, SGU_CHUNK = 128, 8, 256, 64
VMEM_LIMIT = 56 * 1024 * 1024
MESH = pl.DeviceIdType.MESH
ADAM_LR, ADAM_B1, ADAM_B2, ADAM_EPS, ADAM_WD, ADAM_STEP = 0.001, 0.9, 0.999, 1e-08, 0.01, 10
GELU_C, GELU_A = 0.7978845608028654, 0.044715

BIG = ["a_w_pw1", "a_w_pw2", "b_w_in", "b_w_out", "c_w_in", "c_w_out",
       "ffn_w_gate", "ffn_w_up", "ffn_w_down", "ple_w_gate", "ple_w_proj"]
SMALL_SHARDED = ["a_b_pw1", "a_w_dw", "a_b_dw", "a_ln_g", "a_ln_b", "c_w_conv"]
SMALL_REPL = ["b_b_in", "b_ln_g", "b_ln_b", "b_w_s", "b_b_s", "ln1_g", "ln1_b", "ln2_g", "ln2_b", "ple_norm_g"]
WEIGHTS = ["a_w_pw1", "a_b_pw1", "a_w_dw", "a_b_dw", "a_ln_g", "a_ln_b", "a_w_pw2", "b_w_in", "b_b_in", "b_ln_g",
           "b_ln_b", "b_w_s", "b_b_s", "b_w_out", "c_w_in", "c_w_conv", "c_w_out", "ln1_g", "ln1_b", "ln2_g",
           "ln2_b", "ffn_w_gate", "ffn_w_up", "ffn_w_down", "ple_w_gate", "ple_w_proj", "ple_norm_g"]


def _call(name, body, grid, in_specs, out_specs, out_shape, scratch=(), aliases=None):
    return pl.pallas_call(
        body, name=name, grid=grid, in_specs=in_specs, out_specs=out_specs, out_shape=out_shape,
        scratch_shapes=list(scratch), input_output_aliases=aliases or {},
        compiler_params=pltpu.CompilerParams(dimension_semantics=("arbitrary",) * len(grid),
                                             vmem_limit_bytes=VMEM_LIMIT))


def _sds(shape, dtype=F32):
    return jax.ShapeDtypeStruct(shape, dtype)


def _row(tm, c):
    return pl.BlockSpec((tm, c), lambda i: (i, 0))


def _grow(g, tm, c):
    return pl.BlockSpec((g, tm, c), lambda i: (0, i, 0))


def _const(shape):
    nd = len(shape)
    return pl.BlockSpec(shape, lambda i: (0,) * nd, pipeline_mode=pl.Buffered(1))


def _wspec(w, l):
    return pl.BlockSpec((NQ, None, w.shape[2], w.shape[3]), lambda i: (0, l, 0, 0), pipeline_mode=pl.Buffered(1))


def _prev(tm, hb, c):
    return pl.BlockSpec((hb, c), lambda i: (jnp.maximum(i * (tm // hb) - 1, 0), 0))


def _next(tm, hb, c):
    return pl.BlockSpec((hb, c), lambda i: (jnp.minimum((i + 1) * (tm // hb), S // hb - 1), 0))


def _acc(r, c):
    return pl.BlockSpec((r, c), lambda i: (0, 0))


def _sig(x):
    return 1.0 / (1.0 + jnp.exp(-x))


def _ln(z, g, b):
    mu = jnp.mean(z, axis=-1, keepdims=True)
    zc = z - mu
    rstd = lax.rsqrt(jnp.mean(zc * zc, axis=-1, keepdims=True) + LN_EPS)
    xhat = zc * rstd
    return xhat * g + b, xhat, rstd


def _ln_bwd(dyg, xhat, rstd):
    return rstd * (dyg - jnp.mean(dyg, axis=-1, keepdims=True) - xhat * jnp.mean(dyg * xhat, axis=-1, keepdims=True))


def _mm(a, w):
    return jnp.dot(a.astype(BF16), w, preferred_element_type=F32)


def _mmt(a, w):
    return lax.dot_general(a.astype(BF16), w, (((1,), (1,)), ((), ())), preferred_element_type=F32)


def _colsum(x):
    return jnp.sum(x, axis=0, keepdims=True)


def _gelu(x):
    t = jnp.tanh(GELU_C * (x + GELU_A * x * x * x))
    return 0.5 * x * (1.0 + t), t


def _gelu_grad(x, t):
    return 0.5 * (1.0 + t) + 0.5 * x * (1.0 - t * t) * GELU_C * (1.0 + 3.0 * GELU_A * x * x)


def _silu_grad(a, sg):
    return sg * (1.0 + a * (1.0 - sg))


def _sgu_masks():
    r = lax.broadcasted_iota(jnp.int32, (SGU_T, SGU_T), 0) // SGU_CHUNK
    c = lax.broadcasted_iota(jnp.int32, (SGU_T, SGU_T), 1) // SGU_CHUNK
    return r >= c, c >= r


def _fill_halo(buf, lo, n, halo_val_fn, is_edge):
    @pl.when(is_edge)
    def _():
        buf[lo:lo + n, :] = jnp.zeros((n, buf.shape[1]), F32)

    @pl.when(jnp.logical_not(is_edge))
    def _():
        buf[lo:lo + n, :] = halo_val_fn()


def _fwd_a1(x0, w1, b1, l):
    tm = 512

    def body(x_ref, w_ref, b_ref, h_ref, glu_ref):
        xb = x_ref[...].astype(BF16)
        for q in range(NQ):
            sl = slice(q * 512, (q + 1) * 512)
            h_ref[:, sl] = jnp.dot(xb, w_ref[q], preferred_element_type=F32) + b_ref[:, sl]
        glu_ref[...] = h_ref[:, :D] * _sig(h_ref[:, D:])

    return _call(f"fwd_a1_{l}", body, (S // tm,), [_row(tm, D), _wspec(w1, l), _const((1, 2 * D))],
                 [_row(tm, 2 * D), _row(tm, D)], [_sds((S, 2 * D)), _sds((S, D))])(x0, w1, b1)


def _fwd_a2(glu, x0, wdw, bdw, lg, lb, w2, l):
    tm = 256

    def body(g_ref, gp_ref, x_ref, wdw_ref, bdw_ref, lg_ref, lb_ref, w2_ref, z_ref, cv_ref, buf):
        i = pl.program_id(0)
        _fill_halo(buf, 0, HALO_A, lambda: gp_ref[...], i == 0)
        buf[HALO_A:HALO_A + tm, :] = g_ref[...]
        acc = jnp.zeros((tm, D), F32) + bdw_ref[...]
        for k in range(CONV_A):
            acc = acc + wdw_ref[k:k + 1, :] * buf[pl.ds(HALO_A - (CONV_A - 1) + k, tm), :]
        cv_ref[...] = acc
        n, _, _ = _ln(acc, lg_ref[...], lb_ref[...])
        sb = (n * _sig(n)).astype(BF16)
        hm = _mm(sb[:, 0:256], w2_ref[0])
        for q in range(1, NQ):
            hm = hm + _mm(sb[:, q * 256:(q + 1) * 256], w2_ref[q])
        z_ref[...] = ALPHA * x_ref[...] + hm

    return _call(f"fwd_a2_{l}", body, (S // tm,),
                 [_row(tm, D), _prev(tm, HALO_A, D), _row(tm, D), _const((32, D)), _const((1, D)), _const((1, D)),
                  _const((1, D)), _wspec(w2, l)],
                 [_row(tm, D), _row(tm, D)], [_sds((S, D)), _sds((S, D))],
                 scratch=[pltpu.VMEM((HALO_A + tm, D), F32)])(glu, glu, x0, wdw, bdw, lg, lb, w2)


def _fwd_b(x0, win, b_in, lg, lb, ws, bsx, wout):
    tm = 256

    def body(x_ref, win_ref, bin_ref, lg_ref, lb_ref, ws_ref, bsx_ref, wout_ref, z_ref, h_ref, f_scr):
        xb = x_ref[...].astype(BF16)
        for q in range(NQ):
            sl = slice(q * 1024, (q + 1) * 1024)
            h_ref[:, sl] = jnp.dot(xb, win_ref[q], preferred_element_type=F32) + bin_ref[:, sl]
        u, _ = _gelu(h_ref[:, :E])
        v, _ = _gelu(h_ref[:, E:])
        vn, _, _ = _ln(v, lg_ref[...], lb_ref[...])
        vnb = vn.astype(BF16)
        mask, _ = _sgu_masks()
        for hd in range(SGU_H):
            wm = jnp.where(mask, ws_ref[hd], 0.0).astype(BF16)
            cs = slice(hd * SGU_G, (hd + 1) * SGU_G)
            for n in range(tm // SGU_T):
                rs = slice(n * SGU_T, (n + 1) * SGU_T)
                f_scr[rs, cs] = jnp.dot(wm, vnb[rs, cs], preferred_element_type=F32) + bsx_ref[hd]
        mb = (u * f_scr[...]).astype(BF16)
        out = _mm(mb[:, 0:512], wout_ref[0])
        for q in range(1, NQ):
            out = out + _mm(mb[:, q * 512:(q + 1) * 512], wout_ref[q])
        z_ref[...] = ALPHA * x_ref[...] + out

    return _call("fwd_b", body, (S // tm,),
                 [_row(tm, D), _wspec(win, 0), _const((1, 2 * E)), _const((1, E)), _const((1, E)),
                  _const((SGU_H, SGU_T, SGU_T)), _const((SGU_H, SGU_T, SGU_G)), _wspec(wout, 0)],
                 [_row(tm, D), _row(tm, 2 * E)], [_sds((S, D)), _sds((S, 2 * E))],
                 scratch=[pltpu.VMEM((tm, E), F32)])(x0, win, b_in, lg, lb, ws, bsx, wout)


def _fwd_c1(x0, win):
    tm = 512

    def body(x_ref, w_ref, hc_ref):
        xb = x_ref[...].astype(BF16)
        for q in range(NQ):
            hc_ref[:, q * 768:(q + 1) * 768] = jnp.dot(xb, w_ref[q], preferred_element_type=F32)

    return _call("fwd_c1", body, (S // tm,), [_row(tm, D), _wspec(win, 0)], _row(tm, 3 * D),
                 _sds((S, 3 * D)))(x0, win)


def _short_conv(buf, hc_ref, hcp_ref, wc_ref, tm, i):
    _fill_halo(buf, 0, HALO_C, lambda: hcp_ref[:, D:2 * D] * hcp_ref[:, 2 * D:], i == 0)
    buf[HALO_C:HALO_C + tm, :] = hc_ref[:, D:2 * D] * hc_ref[:, 2 * D:]
    y = wc_ref[0:1, :] * buf[pl.ds(HALO_C - 2, tm), :]
    for k in range(1, CONV_C):
        y = y + wc_ref[k:k + 1, :] * buf[pl.ds(HALO_C - 2 + k, tm), :]
    return y


def _fwd_c2(hc, x0, wc, wout):
    tm = 256

    def body(hc_ref, hcp_ref, x_ref, wc_ref, wout_ref, z_ref, buf):
        y = _short_conv(buf, hc_ref, hcp_ref, wc_ref, tm, pl.program_id(0))
        mb = (hc_ref[:, :D] * y).astype(BF16)
        out = _mm(mb[:, 0:256], wout_ref[0])
        for q in range(1, NQ):
            out = out + _mm(mb[:, q * 256:(q + 1) * 256], wout_ref[q])
        z_ref[...] = ALPHA * x_ref[...] + out

    return _call("fwd_c2", body, (S // tm,),
                 [_row(tm, 3 * D), _prev(tm, HALO_C, 3 * D), _row(tm, D), _const((8, D)), _wspec(wout, 0)],
                 _row(tm, D), _sds((S, D)), scratch=[pltpu.VMEM((HALO_C + tm, D), F32)])(hc, hc, x0, wc, wout)


def _fwd_ffn(z1, lg, lb, wg, wu, wd, l):
    tm = 512

    def body(z_ref, lg_ref, lb_ref, wg_ref, wu_ref, wd_ref, o_ref):
        x1, _, _ = _ln(z_ref[...], lg_ref[...], lb_ref[...])
        xb = x1.astype(BF16)
        f = None
        for q in range(NQ):
            a = jnp.dot(xb, wg_ref[q], preferred_element_type=F32)
            u = jnp.dot(xb, wu_ref[q], preferred_element_type=F32)
            t = _mm(a * _sig(a) * u, wd_ref[q])
            f = t if f is None else f + t
        o_ref[...] = ALPHA * x1 + f

    return _call(f"fwd_ffn_{l}", body, (S // tm,),
                 [_row(tm, D), _const((1, D)), _const((1, D)), _wspec(wg, l), _wspec(wu, l), _wspec(wd, l)],
                 _row(tm, D), _sds((S, D)))(z1, lg, lb, wg, wu, wd)


def _ple_parts(z2, p, lg, lb, wg_ref, wp_ref, pg):
    x2, xhat, rstd = _ln(z2, lg, lb)
    xb = x2.astype(BF16)
    gp = _mm(xb[:, 0:256], wg_ref[0])
    for q in range(1, NQ):
        gp = gp + _mm(xb[:, q * 256:(q + 1) * 256], wg_ref[q])
    gate = _sig(gp)
    pb = p.astype(BF16)
    qp = jnp.concatenate([jnp.dot(pb, wp_ref[q], preferred_element_type=F32) for q in range(NQ)], axis=1)
    rs = lax.rsqrt(jnp.mean(qp * qp, axis=-1, keepdims=True) + LN_EPS)
    qn = qp * rs
    return x2, xhat, rstd, xb, gate, qn, rs, qn * pg


def _fwd_ple(z2, p, lg, lb, wg, wp, pg, l):
    tm = 512

    def body(z_ref, p_ref, lg_ref, lb_ref, wg_ref, wp_ref, pg_ref, o_ref):
        x2, _, _, _, gate, _, _, r = _ple_parts(z_ref[...], p_ref[...], lg_ref[...], lb_ref[...], wg_ref, wp_ref,
                                                pg_ref[...])
        o_ref[...] = x2 + gate * r

    return _call(f"fwd_ple_{l}", body, (S // tm,),
                 [_row(tm, D), _row(tm, 256), _const((1, D)), _const((1, D)), _wspec(wg, l), _wspec(wp, l),
                  _const((1, D))],
                 _row(tm, D), _sds((S, D)))(z2, p, lg, lb, wg, wp, pg)


def _loss_head(y, target):
    tm = 512

    def body(y_ref, t_ref, dy_ref, acc_ref):
        @pl.when(pl.program_id(0) == 0)
        def _():
            acc_ref[...] = jnp.zeros_like(acc_ref)

        e = y_ref[...] - t_ref[...]
        dy_ref[...] = e * (1.0 / D)
        acc_ref[0:1, :] += _colsum(e * e)

    return _call("loss_head", body, (S // tm,), [_row(tm, D), _row(tm, D)], [_row(tm, D), _acc(8, D)],
                 [_sds((S, D)), _sds((8, D))])(y, target)


def _zero_first(*refs):
    @pl.when(pl.program_id(0) == 0)
    def _():
        for r in refs:
            r[...] = jnp.zeros_like(r)


def _bwd_ple(g, z2, p, lg, lb, wg, wp, pg, l):
    tm = 256

    def body(g_ref, z_ref, p_ref, lg_ref, lb_ref, wg_ref, wp_ref, pg_ref, dz_ref, xb_ref, dgp_ref, dqp_ref, acc_ref):
        _zero_first(acc_ref)
        gin = g_ref[...]
        lgv, pgv = lg_ref[...], pg_ref[...]
        _, xhat, rstd, xb, gate, qn, rs, r = _ple_parts(z_ref[...], p_ref[...], lgv, lb_ref[...], wg_ref, wp_ref, pgv)
        xb_ref[...] = xb
        dgpb = (gin * r * gate * (1.0 - gate)).astype(BF16)
        dgp_ref[...] = dgpb
        dx2 = gin + jnp.concatenate([_mmt(dgpb, wg_ref[q]) for q in range(NQ)], axis=1)
        dr = gin * gate
        acc_ref[0:1, :] += _colsum(dr * qn)
        t = dr * pgv
        dqp_ref[...] = (rs * (t - qn * jnp.mean(t * qn, axis=-1, keepdims=True))).astype(BF16)
        acc_ref[1:2, :] += _colsum(dx2 * xhat)
        acc_ref[2:3, :] += _colsum(dx2)
        dz_ref[...] = _ln_bwd(dx2 * lgv, xhat, rstd)

    return _call(f"bwd_ple_{l}", body, (S // tm,),
                 [_row(tm, D), _row(tm, D), _row(tm, 256), _const((1, D)), _const((1, D)), _wspec(wg, l),
                  _wspec(wp, l), _const((1, D))],
                 [_row(tm, D), _row(tm, D), _row(tm, D), _row(tm, D), _acc(8, D)],
                 [_sds((S, D)), _sds((S, D), BF16), _sds((S, D), BF16), _sds((S, D), BF16), _sds((8, D))]
                 )(g, z2, p, lg, lb, wg, wp, pg)


def _bwd_ffn(dz2, z1, lg, lb, wg, wu, wd, l):
    tm = 256

    def body(dz2_ref, z_ref, lg_ref, lb_ref, wg_ref, wu_ref, wd_ref, dz1_ref, xb_ref, hm_ref, da_ref, du_ref, acc_ref):
        _zero_first(acc_ref)
        dz2v = dz2_ref[...]
        dzb = dz2v.astype(BF16)
        lgv = lg_ref[...]
        x1, xhat, rstd = _ln(z_ref[...], lgv, lb_ref[...])
        xb = x1.astype(BF16)
        xb_ref[...] = xb
        dx1 = ALPHA * dz2v
        for q in range(NQ):
            a = jnp.dot(xb, wg_ref[q], preferred_element_type=F32)
            u = jnp.dot(xb, wu_ref[q], preferred_element_type=F32)
            sg = _sig(a)
            s = a * sg
            hm_ref[q] = (s * u).astype(BF16)
            dhm = _mmt(dzb, wd_ref[q])
            dub = (dhm * s).astype(BF16)
            dab = (dhm * u * _silu_grad(a, sg)).astype(BF16)
            da_ref[q] = dab
            du_ref[q] = dub
            dx1 = dx1 + _mmt(dab, wg_ref[q]) + _mmt(dub, wu_ref[q])
        acc_ref[0:1, :] += _colsum(dx1 * xhat)
        acc_ref[1:2, :] += _colsum(dx1)
        dz1_ref[...] = _ln_bwd(dx1 * lgv, xhat, rstd)

    return _call(f"bwd_ffn_{l}", body, (S // tm,),
                 [_row(tm, D), _row(tm, D), _const((1, D)), _const((1, D)), _wspec(wg, l), _wspec(wu, l),
                  _wspec(wd, l)],
                 [_row(tm, D), _row(tm, D), _grow(NQ, tm, FQ), _grow(NQ, tm, FQ), _grow(NQ, tm, FQ), _acc(8, D)],
                 [_sds((S, D)), _sds((S, D), BF16), _sds((NQ, S, FQ), BF16), _sds((NQ, S, FQ), BF16),
                  _sds((NQ, S, FQ), BF16), _sds((8, D))])(dz2, z1, lg, lb, wg, wu, wd)


def _bwd_a2(dz1, cv, lg, lb, w2, l):
    tm = 512

    def body(dz_ref, cv_ref, lg_ref, lb_ref, w2_ref, dcv_ref, sb_ref, acc_ref):
        _zero_first(acc_ref)
        lgv = lg_ref[...]
        n, xhat, rstd = _ln(cv_ref[...], lgv, lb_ref[...])
        sg = _sig(n)
        sb_ref[...] = (n * sg).astype(BF16)
        dzb = dz_ref[...].astype(BF16)
        ds = jnp.concatenate([_mmt(dzb, w2_ref[q]) for q in range(NQ)], axis=1)
        dn = ds * _silu_grad(n, sg)
        acc_ref[0:1, :] += _colsum(dn * xhat)
        acc_ref[1:2, :] += _colsum(dn)
        dcv = _ln_bwd(dn * lgv, xhat, rstd)
        acc_ref[2:3, :] += _colsum(dcv)
        dcv_ref[...] = dcv

    return _call(f"bwd_a2_{l}", body, (S // tm,),
                 [_row(tm, D), _row(tm, D), _const((1, D)), _const((1, D)), _wspec(w2, l)],
                 [_row(tm, D), _row(tm, D), _acc(8, D)],
                 [_sds((S, D)), _sds((S, D), BF16), _sds((8, D))])(dz1, cv, lg, lb, w2)


def _bwd_conv_a(dcv, glu, wdw, l):
    tm = 256
    nb = S // tm

    def body(d_ref, dn_ref, g_ref, gp_ref, w_ref, dglu_ref, dw_ref, bufd, bufx):
        i = pl.program_id(0)
        _zero_first(dw_ref)
        bufd[0:tm, :] = d_ref[...]
        _fill_halo(bufd, tm, HALO_A, lambda: dn_ref[...], i == nb - 1)
        _fill_halo(bufx, 0, HALO_A, lambda: gp_ref[...], i == 0)
        bufx[HALO_A:HALO_A + tm, :] = g_ref[...]
        acc = w_ref[0:1, :] * bufd[pl.ds(CONV_A - 1, tm), :]
        for k in range(1, CONV_A):
            acc = acc + w_ref[k:k + 1, :] * bufd[pl.ds(CONV_A - 1 - k, tm), :]
        dglu_ref[...] = acc
        dv = d_ref[...]
        for k in range(CONV_A):
            dw_ref[k:k + 1, :] += _colsum(dv * bufx[pl.ds(HALO_A - (CONV_A - 1) + k, tm), :])

    return _call(f"bwd_conv_a_{l}", body, (nb,),
                 [_row(tm, D), _next(tm, HALO_A, D), _row(tm, D), _prev(tm, HALO_A, D), _const((32, D))],
                 [_row(tm, D), _acc(32, D)], [_sds((S, D)), _sds((32, D))],
                 scratch=[pltpu.VMEM((tm + HALO_A, D), F32), pltpu.VMEM((HALO_A + tm, D), F32)]
                 )(dcv, dcv, glu, glu, wdw)


def _bwd_a1(dglu, h, dz1, w1, l):
    tm = 256

    def body(dg_ref, h_ref, dz_ref, w_ref, dx_ref, dh_ref, acc_ref):
        _zero_first(acc_ref)
        a, g = h_ref[:, :D], h_ref[:, D:]
        sg = _sig(g)
        dgl = dg_ref[...]
        da = dgl * sg
        dg = dgl * a * sg * (1.0 - sg)
        acc_ref[0:1, 0:D] += _colsum(da)
        acc_ref[0:1, D:2 * D] += _colsum(dg)
        dh_ref[:, 0:D] = da.astype(BF16)
        dh_ref[:, D:2 * D] = dg.astype(BF16)
        dx = ALPHA * dz_ref[...]
        for q in range(NQ):
            dx = dx + _mmt(dh_ref[:, q * 512:(q + 1) * 512], w_ref[q])
        dx_ref[...] = dx

    return _call(f"bwd_a1_{l}", body, (S // tm,),
                 [_row(tm, D), _row(tm, 2 * D), _row(tm, D), _wspec(w1, l)],
                 [_row(tm, D), _row(tm, 2 * D), _acc(8, 2 * D)],
                 [_sds((S, D)), _sds((S, 2 * D), BF16), _sds((8, 2 * D))])(dglu, h, dz1, w1)


def _bwd_c2(dz1, hc, wc, wout):
    tm = 256

    def body(dz_ref, hc_ref, hcp_ref, wc_ref, wout_ref, dy_ref, dbg_ref, mb_ref, buf):
        y = _short_conv(buf, hc_ref, hcp_ref, wc_ref, tm, pl.program_id(0))
        dzb = dz_ref[...].astype(BF16)
        dm = jnp.concatenate([_mmt(dzb, wout_ref[q]) for q in range(NQ)], axis=1)
        bg = hc_ref[:, :D]
        mb_ref[...] = (bg * y).astype(BF16)
        dbg_ref[...] = (dm * y).astype(BF16)
        dy_ref[...] = dm * bg

    return _call("bwd_c2", body, (S // tm,),
                 [_row(tm, D), _row(tm, 3 * D), _prev(tm, HALO_C, 3 * D), _const((8, D)), _wspec(wout, 0)],
                 [_row(tm, D), _row(tm, D), _row(tm, D)],
                 [_sds((S, D)), _sds((S, D), BF16), _sds((S, D), BF16)],
                 scratch=[pltpu.VMEM((HALO_C + tm, D), F32)])(dz1, hc, hc, wc, wout)


def _bwd_c1(dy, hc, dbg, dz1, wc, win):
    tm = 256
    nb = S // tm

    def body(d_ref, dn_ref, hc_ref, hcp_ref, dbg_ref, dz_ref, wc_ref, win_ref, dx_ref, dhc_ref, dwc_ref, bufd, bufq):
        i = pl.program_id(0)
        _zero_first(dwc_ref)
        bufd[0:tm, :] = d_ref[...]
        _fill_halo(bufd, tm, HALO_C, lambda: dn_ref[...], i == nb - 1)
        _fill_halo(bufq, 0, HALO_C, lambda: hcp_ref[:, D:2 * D] * hcp_ref[:, 2 * D:], i == 0)
        bufq[HALO_C:HALO_C + tm, :] = hc_ref[:, D:2 * D] * hc_ref[:, 2 * D:]
        dq = wc_ref[0:1, :] * bufd[pl.ds(CONV_C - 1, tm), :]
        for k in range(1, CONV_C):
            dq = dq + wc_ref[k:k + 1, :] * bufd[pl.ds(CONV_C - 1 - k, tm), :]
        dv = d_ref[...]
        for k in range(CONV_C):
            dwc_ref[k:k + 1, :] += _colsum(dv * bufq[pl.ds(HALO_C - (CONV_C - 1) + k, tm), :])
        dhc_ref[:, 0:D] = dbg_ref[...]
        dhc_ref[:, D:2 * D] = (dq * hc_ref[:, 2 * D:]).astype(BF16)
        dhc_ref[:, 2 * D:3 * D] = (dq * hc_ref[:, D:2 * D]).astype(BF16)
        dx = ALPHA * dz_ref[...]
        for q in range(NQ):
            dx = dx + _mmt(dhc_ref[:, q * 768:(q + 1) * 768], win_ref[q])
        dx_ref[...] = dx

    return _call("bwd_c1", body, (nb,),
                 [_row(tm, D), _next(tm, HALO_C, D), _row(tm, 3 * D), _prev(tm, HALO_C, 3 * D), _row(tm, D),
                  _row(tm, D), _const((8, D)), _wspec(win, 0)],
                 [_row(tm, D), _row(tm, 3 * D), _acc(8, D)],
                 [_sds((S, D)), _sds((S, 3 * D), BF16), _sds((8, D))],
                 scratch=[pltpu.VMEM((tm + HALO_C, D), F32), pltpu.VMEM((HALO_C + tm, D), F32)]
                 )(dy, dy, hc, hc, dbg, dz1, wc, win)


def _bwd_b(dz1, h, lg, lb, win, wout, ws, wst, bsx):
    tm = 128
    nb = S // tm

    def body(dz_ref, h_ref, lg_ref, lb_ref, win_ref, wout_ref, ws_ref, wst_ref, bsx_ref,
             dx_ref, dh_ref, mb_ref, acc_ref, dws_ref, dbs_ref, f_scr, dvn_scr):
        _zero_first(acc_ref, dws_ref, dbs_ref)
        lgv = lg_ref[...]
        hu, hv = h_ref[:, :E], h_ref[:, E:]
        u, tu = _gelu(hu)
        v, tv = _gelu(hv)
        vn, xhat, rstd = _ln(v, lgv, lb_ref[...])
        vnb = vn.astype(BF16)
        dzb = dz_ref[...].astype(BF16)
        dm = jnp.concatenate([_mmt(dzb, wout_ref[q]) for q in range(NQ)], axis=1)
        mask, mask_t = _sgu_masks()
        for hd in range(SGU_H):
            wm = jnp.where(mask, ws_ref[hd], 0.0).astype(BF16)
            cs = slice(hd * SGU_G, (hd + 1) * SGU_G)
            for n in range(tm // SGU_T):
                rs = slice(n * SGU_T, (n + 1) * SGU_T)
                f_scr[rs, cs] = jnp.dot(wm, vnb[rs, cs], preferred_element_type=F32) + bsx_ref[hd]
        f = f_scr[...]
        mb_ref[...] = (u * f).astype(BF16)
        du = dm * f
        df = dm * u
        dfb = df.astype(BF16)
        for hd in range(SGU_H):
            wmt = jnp.where(mask_t, wst_ref[hd], 0.0).astype(BF16)
            cs = slice(hd * SGU_G, (hd + 1) * SGU_G)
            for n in range(tm // SGU_T):
                rs = slice(n * SGU_T, (n + 1) * SGU_T)
                dvn_scr[rs, cs] = jnp.dot(wmt, dfb[rs, cs], preferred_element_type=F32)
                dws_ref[hd] += lax.dot_general(dfb[rs, cs], vnb[rs, cs], (((1,), (1,)), ((), ())),
                                               preferred_element_type=F32)
                dbs_ref[hd] += df[rs, cs]
        dvn = dvn_scr[...]
        acc_ref[1:2, 0:E] += _colsum(dvn * xhat)
        acc_ref[2:3, 0:E] += _colsum(dvn)
        dv = _ln_bwd(dvn * lgv, xhat, rstd)
        dhu = du * _gelu_grad(hu, tu)
        dhv = dv * _gelu_grad(hv, tv)
        acc_ref[0:1, 0:E] += _colsum(dhu)
        acc_ref[0:1, E:2 * E] += _colsum(dhv)
        dh_ref[:, 0:E] = dhu.astype(BF16)
        dh_ref[:, E:2 * E] = dhv.astype(BF16)
        dx = ALPHA * dz_ref[...]
        for q in range(NQ):
            dx = dx + _mmt(dh_ref[:, q * 1024:(q + 1) * 1024], win_ref[q])
        dx_ref[...] = dx

        @pl.when(pl.program_id(0) == nb - 1)
        def _():
            for hd in range(SGU_H):
                dws_ref[hd] = jnp.where(mask, dws_ref[hd], 0.0)

    c3 = lambda a, b, c: pl.BlockSpec((a, b, c), lambda i: (0, 0, 0))
    return _call("bwd_b", body, (nb,),
                 [_row(tm, D), _row(tm, 2 * E), _const((1, E)), _const((1, E)), _wspec(win, 0), _wspec(wout, 0),
                  _const((SGU_H, SGU_T, SGU_T)), _const((SGU_H, SGU_T, SGU_T)), _const((SGU_H, SGU_T, SGU_G))],
                 [_row(tm, D), _row(tm, 2 * E), _row(tm, E), _acc(8, 2 * E), c3(SGU_H, SGU_T, SGU_T),
                  c3(SGU_H, SGU_T, SGU_G)],
                 [_sds((S, D)), _sds((S, 2 * E), BF16), _sds((S, E), BF16), _sds((8, 2 * E)),
                  _sds((SGU_H, SGU_T, SGU_T)), _sds((SGU_H, SGU_T, SGU_G))],
                 scratch=[pltpu.VMEM((tm, E), F32), pltpu.VMEM((tm, E), F32)]
                 )(dz1, h, lg, lb, win, wout, ws, wst, bsx)


def _mm_tn(name, a, amode, b, bmode, k, n, nl, l, buf):
    ts = min(1024, S)

    def spec(mode, w):
        if mode == "1":
            return pl.BlockSpec((ts, w), lambda g, s: (s, 0))
        if mode == "c":
            return pl.BlockSpec((ts, w), lambda g, s: (s, g))
        return pl.BlockSpec((None, ts, w), lambda g, s: (g, s, 0))

    def body(a_ref, b_ref, *rest):
        o_ref = rest[-1]

        @pl.when(pl.program_id(1) == 0)
        def _():
            o_ref[...] = jnp.zeros_like(o_ref)

        o_ref[...] += lax.dot_general(a_ref[...].astype(BF16), b_ref[...].astype(BF16), (((0,), (0,)), ((), ())),
                                      preferred_element_type=F32)

    in_specs = [spec(amode, k), spec(bmode, n)]
    args = [a, b]
    aliases = {}
    if buf is not None:
        in_specs.append(pl.BlockSpec(memory_space=pl.ANY))
        args.append(buf)
        aliases = {2: 0}
    return _call(name, body, (NQ, S // ts), in_specs,
                 pl.BlockSpec((None, None, k, n), lambda g, s: (l, g, 0, 0)), _sds((nl, NQ, k, n)),
                 aliases=aliases)(*args)


def _row_block(k, cap=256):
    return max(t for t in range(16, min(k, cap) + 1, 16) if k % t == 0)


def _cast_bf16(w):
    nl, k, n = w.shape
    tb = _row_block(k)

    def body(w_ref, o_ref):
        o_ref[...] = w_ref[...].astype(BF16)

    spec = pl.BlockSpec((None, tb, n), lambda l, i: (l, i, 0))
    return _call("cast_bf16", body, (nl, k // tb), [spec], spec, _sds(w.shape, BF16))(w)


def _adam(name, w, m, v, gc):
    nl, k, n = w.shape
    nc = gc.shape[0]
    tb = _row_block(k)

    def body(w_ref, m_ref, v_ref, g_ref, go_ref, d_ref, mo_ref, vo_ref):
        g = g_ref[0].astype(F32)
        for c in range(1, nc):
            g = g + g_ref[c].astype(F32)
        m2 = ADAM_B1 * m_ref[...] + (1.0 - ADAM_B1) * g
        v2 = ADAM_B2 * v_ref[...] + (1.0 - ADAM_B2) * (g * g)
        m_hat = m2 / (1.0 - ADAM_B1 ** ADAM_STEP)
        v_hat = v2 / (1.0 - ADAM_B2 ** ADAM_STEP)
        go_ref[...] = g
        d_ref[...] = -ADAM_LR * (m_hat / (jnp.sqrt(v_hat) + ADAM_EPS) + ADAM_WD * w_ref[...])
        mo_ref[...] = m2
        vo_ref[...] = v2

    spec = pl.BlockSpec((None, tb, n), lambda l, i: (l, i, 0))
    gspec = pl.BlockSpec((nc, None, tb, n), lambda l, i: (0, l, i, 0))
    return _call(name, body, (nl, k // tb), [spec, spec, spec, gspec], [spec] * 4, [_sds(w.shape)] * 4)(w, m, v, gc)


def _sum8(g8):
    r = g8.shape[1]

    def body(g_ref, o_ref):
        acc = g_ref[0]
        for d in range(1, 8):
            acc = acc + g_ref[d]
        o_ref[...] = acc

    return _call("sum8", body, (1,), [pl.BlockSpec((8, r, 128), lambda i: (0, 0, 0))],
                 pl.BlockSpec((r, 128), lambda i: (0, 0)), _sds((r, 128)))(g8)


def _place():
    x, y, c = lax.axis_index("x"), lax.axis_index("y"), lax.axis_index("c")
    return x, y, c, 2 * x + y, (x, y, 1 - c), [(1 - x, y), (x, 1 - y), (1 - x, 1 - y)]


def _exchange(n, local_src, remote_src, dst):
    def run(send, recv, lsem):
        x, y, c, q, sib, chips = _place()
        me = (x, y, c)

        def rcopy(w, k, qq, cc, to, src=None):
            return pltpu.make_async_remote_copy(
                src_ref=dst(w, qq, cc) if src is None else src, dst_ref=dst(w, qq, cc),
                send_sem=send.at[7 * w + k], recv_sem=recv.at[7 * w + k], device_id=to, device_id_type=MESH)

        mine, sent = [], []
        for w in range(n):
            m = pltpu.make_async_copy(local_src(w), dst(w, q, c), lsem.at[w])
            m.start()
            mine.append(m)
            first = [rcopy(w, 0, q, c, sib, local_src(w))]
            first += [rcopy(w, 1 + j, q, c, (cx, cy, c), remote_src(w, 2 * cx + cy)) for j, (cx, cy) in enumerate(chips)]
            for cp in first:
                cp.start()
            sent += first
        for w in range(n):
            for j, (cx, cy) in enumerate(chips):
                rcopy(w, 1 + j, 2 * cx + cy, c, me).wait_recv()
                fwd = rcopy(w, 4 + j, 2 * cx + cy, c, sib)
                fwd.start()
                sent.append(fwd)
        for w in range(n):
            rcopy(w, 0, q, 1 - c, me).wait_recv()
            for j, (cx, cy) in enumerate(chips):
                rcopy(w, 4 + j, 2 * cx + cy, 1 - c, me).wait_recv()
        for cp in sent:
            cp.wait_send()
        for m in mine:
            m.wait()

    return run


def _comm_call(name, body, args, out_shape, n_sem):
    any_spec = pl.BlockSpec(memory_space=pl.ANY)
    return pl.pallas_call(
        body, name=name, in_specs=[any_spec] * len(args), out_specs=[any_spec] * len(out_shape), out_shape=out_shape,
        scratch_shapes=[pltpu.SemaphoreType.DMA((7 * n_sem,)), pltpu.SemaphoreType.DMA((7 * n_sem,)),
                        pltpu.SemaphoreType.DMA((n_sem,))])(*args)


def _gather_weights(shards):
    n = len(shards)
    kh = [s.shape[1] // 2 for s in shards]

    def body(*refs):
        ins, outs = refs[:n], refs[n:2 * n]
        c = lax.axis_index("c")
        src = lambda w: ins[w].at[:, pl.ds(c * kh[w], kh[w]), :]
        _exchange(n, src, lambda w, q: src(w), lambda w, q, cc: outs[w].at[q, :, pl.ds(cc * kh[w], kh[w]), :])(
            *refs[2 * n:])

    return _comm_call("gather_weights", body, shards, [_sds((NQ,) + s.shape, BF16) for s in shards], n)


def _scatter_partials(parts):
    n = len(parts)

    def body(*refs):
        ins, outs = refs[:n], refs[n:2 * n]
        q = 2 * lax.axis_index("x") + lax.axis_index("y")
        _exchange(n, lambda w: ins[w].at[:, q], lambda w, qq: ins[w].at[:, qq],
                  lambda w, qq, cc: outs[w].at[qq, :, cc])(*refs[2 * n:])

    return _comm_call("scatter_partials", body, parts,
                      [_sds((NQ, p.shape[0], 2) + p.shape[2:], BF16) for p in parts], n)


def _swap_halves(dws):
    n = len(dws)
    kh = [d.shape[2] // 2 for d in dws]

    def body(*refs):
        ins, outs = refs[:n], refs[n:2 * n]
        send, recv = refs[2 * n:]
        x, y, c, _, sib, _ = _place()
        copies = [pltpu.make_async_remote_copy(
            src_ref=ins[w].at[:, :, pl.ds((1 - c) * kh[w], kh[w]), :], dst_ref=outs[w],
            send_sem=send.at[w], recv_sem=recv.at[w], device_id=sib, device_id_type=MESH) for w in range(n)]
        for cp in copies:
            cp.start()
        for cp in copies:
            cp.wait()

    any_spec = pl.BlockSpec(memory_space=pl.ANY)
    return pl.pallas_call(
        body, name="swap_halves", in_specs=[any_spec] * n, out_specs=[any_spec] * n,
        out_shape=[_sds(d.shape[:2] + (kh[w],) + d.shape[3:]) for w, d in enumerate(dws)],
        scratch_shapes=[pltpu.SemaphoreType.DMA((n,)), pltpu.SemaphoreType.DMA((n,))])(*dws)


def _add_halves(dw, got, cidx):
    nl, _, k, n = dw.shape
    kh = k // 2

    def body(c_ref, a_ref, b_ref, o_ref):
        o_ref[...] = (a_ref[...] + b_ref[...]).astype(BF16)

    grid_spec = pltpu.PrefetchScalarGridSpec(
        num_scalar_prefetch=1, grid=(nl, NQ),
        in_specs=[pl.BlockSpec((None, None, None, kh, n), lambda l, q, c_ref: (l, q, c_ref[0], 0, 0)),
                  pl.BlockSpec((None, None, kh, n), lambda l, q, c_ref: (l, q, 0, 0))],
        out_specs=pl.BlockSpec((None, None, kh, n), lambda l, q, c_ref: (l, q, 0, 0)))
    return pl.pallas_call(
        body, name="add_halves", grid_spec=grid_spec, out_shape=_sds((nl, NQ, kh, n), BF16),
        compiler_params=pltpu.CompilerParams(dimension_semantics=("arbitrary", "arbitrary"),
                                             vmem_limit_bytes=VMEM_LIMIT))(cidx, dw.reshape(nl, NQ, 2, kh, n), got)


def _gather8(v):
    r = v.shape[0]

    def body(v_ref, o_ref, send, recv, lsem):
        _exchange(1, lambda w: v_ref, lambda w, q: v_ref, lambda w, q, cc: o_ref.at[2 * q + cc])(send, recv, lsem)

    vm = pl.BlockSpec(memory_space=pltpu.VMEM)
    return pl.pallas_call(
        body, name="gather8", in_specs=[vm], out_specs=vm, out_shape=_sds((8, r, 128)),
        scratch_shapes=[pltpu.SemaphoreType.DMA((7,)), pltpu.SemaphoreType.DMA((7,)), pltpu.SemaphoreType.DMA((1,))],
        compiler_params=pltpu.CompilerParams(vmem_limit_bytes=VMEM_LIMIT))(v)


PACK = 16 * 128


def _pack(arrays):
    parts = []
    for a in arrays:
        flat = a.reshape(-1)
        parts.append(jnp.pad(flat, (0, (-flat.shape[0]) % PACK)))
    return jnp.concatenate(parts).reshape(-1, 128)


def _unpack(packed, shapes):
    flat = packed.reshape(-1)
    out, off = [], 0
    for shp in shapes:
        size = 1
        for d in shp:
            size *= d
        out.append(flat[off:off + size].reshape(shp))
        off += size + (-size) % PACK
    return out


def kernel(x, p, a_w_pw1, a_b_pw1, a_w_dw, a_b_dw, a_ln_g, a_ln_b, a_w_pw2, b_w_in, b_b_in, b_ln_g, b_ln_b, b_w_s, b_b_s, b_w_out, c_w_in, c_w_conv, c_w_out, ln1_g, ln1_b, ln2_g, ln2_b, ffn_w_gate, ffn_w_up, ffn_w_down, ple_w_gate, ple_w_proj, ple_norm_g, loss_target, m_a_w_pw1, m_a_b_pw1, m_a_w_dw, m_a_b_dw, m_a_ln_g, m_a_ln_b, m_a_w_pw2, m_b_w_in, m_b_b_in, m_b_ln_g, m_b_ln_b, m_b_w_s, m_b_b_s, m_b_w_out, m_c_w_in, m_c_w_conv, m_c_w_out, m_ln1_g, m_ln1_b, m_ln2_g, m_ln2_b, m_ffn_w_gate, m_ffn_w_up, m_ffn_w_down, m_ple_w_gate, m_ple_w_proj, m_ple_norm_g, v_a_w_pw1, v_a_b_pw1, v_a_w_dw, v_a_b_dw, v_a_ln_g, v_a_ln_b, v_a_w_pw2, v_b_w_in, v_b_b_in, v_b_ln_g, v_b_ln_b, v_b_w_s, v_b_b_s, v_b_w_out, v_c_w_in, v_c_w_conv, v_c_w_out, v_ln1_g, v_ln1_b, v_ln2_g, v_ln2_b, v_ffn_w_gate, v_ffn_w_up, v_ffn_w_down, v_ple_w_gate, v_ple_w_proj, v_ple_norm_g):
    args = dict(locals())
    wts = {k: args[k] for k in WEIGHTS}
    mom = {k: args["m_" + k] for k in WEIGHTS}
    var = {k: args["v_" + k] for k in WEIGHTS}
    q_idx = 2 * lax.axis_index("x") + lax.axis_index("y")
    c_idx = lax.axis_index("c").astype(jnp.int32).reshape(1)

    gw = dict(zip(BIG, _gather_weights([_cast_bf16(wts[k]) for k in BIG])))
    shard_shapes = [wts[k].shape for k in SMALL_SHARDED]
    small8 = _gather8(_pack([wts[k] for k in SMALL_SHARDED]))
    per_chip = [_unpack(small8[2 * qq], shard_shapes) for qq in range(NQ)]
    full = {k: jnp.concatenate([per_chip[qq][i] for qq in range(NQ)], axis=-1) for i, k in enumerate(SMALL_SHARDED)}
    for k in SMALL_REPL:
        full[k] = wts[k]

    def vec(name, l):
        return full[name][l][None, :]

    def conv_w(name, l, rows):
        w = full[name][l]
        return jnp.pad(w, ((0, rows - w.shape[0]), (0, 0)))

    ws = full["b_w_s"][0]
    wst = jnp.transpose(ws, (0, 2, 1))
    bsx = jnp.broadcast_to(full["b_b_s"][0][:, :, None], (SGU_H, SGU_T, SGU_G))

    x0s, z1s, z2s, saved = [], [], [], []
    cur = x[0]
    for i in range(DEPTH):
        mix, j = i % 3, i // 3
        x0s.append(cur)
        if mix == 0:
            h, glu = _fwd_a1(cur, gw["a_w_pw1"], vec("a_b_pw1", j), j)
            z1, cv = _fwd_a2(glu, cur, conv_w("a_w_dw", j, 32), vec("a_b_dw", j), vec("a_ln_g", j), vec("a_ln_b", j),
                             gw["a_w_pw2"], j)
            saved.append((h, glu, cv))
        elif mix == 1:
            z1, h = _fwd_b(cur, gw["b_w_in"], vec("b_b_in", 0), vec("b_ln_g", 0), vec("b_ln_b", 0), ws, bsx,
                           gw["b_w_out"])
            saved.append((h,))
        else:
            hc = _fwd_c1(cur, gw["c_w_in"])
            z1 = _fwd_c2(hc, cur, conv_w("c_w_conv", 0, 8), gw["c_w_out"])
            saved.append((hc,))
        z2 = _fwd_ffn(z1, vec("ln1_g", i), vec("ln1_b", i), gw["ffn_w_gate"], gw["ffn_w_up"], gw["ffn_w_down"], i)
        cur = _fwd_ple(z2, p[i, 0], vec("ln2_g", i), vec("ln2_b", i), gw["ple_w_gate"], gw["ple_w_proj"],
                       vec("ple_norm_g", i), i)
        z1s.append(z1)
        z2s.append(z2)

    g, loss_acc = _loss_head(cur, loss_target[0])
    loss = lax.psum(0.5 / D * jnp.sum(loss_acc[0]), ("x", "y", "c"))

    dw = {k: None for k in BIG}
    sg = {}

    def wgrad(name, l, a, amode, b, bmode):
        nl, k, n = wts[name].shape
        dw[name] = _mm_tn(f"dw_{name}_{l}", a, amode, b, bmode, k, n, nl, l, dw[name])

    for i in reversed(range(DEPTH)):
        mix, j = i % 3, i // 3
        dz2, x2b, dgp, dqp, acc = _bwd_ple(g, z2s[i], p[i, 0], vec("ln2_g", i), vec("ln2_b", i), gw["ple_w_gate"],
                                           gw["ple_w_proj"], vec("ple_norm_g", i), i)
        sg["ple_norm_g", i], sg["ln2_g", i], sg["ln2_b", i] = acc[0], acc[1], acc[2]
        wgrad("ple_w_gate", i, x2b, "c", dgp, "1")
        wgrad("ple_w_proj", i, p[i, 0], "1", dqp, "c")
        dz1, x1b, hm, da, du, acc = _bwd_ffn(dz2, z1s[i], vec("ln1_g", i), vec("ln1_b", i), gw["ffn_w_gate"],
                                             gw["ffn_w_up"], gw["ffn_w_down"], i)
        sg["ln1_g", i], sg["ln1_b", i] = acc[0], acc[1]
        wgrad("ffn_w_gate", i, x1b, "1", da, "g")
        wgrad("ffn_w_up", i, x1b, "1", du, "g")
        wgrad("ffn_w_down", i, hm, "g", dz2, "1")
        x0 = x0s[i]
        if mix == 0:
            h, glu, cv = saved[i]
            dcv, sb, acc = _bwd_a2(dz1, cv, vec("a_ln_g", j), vec("a_ln_b", j), gw["a_w_pw2"], j)
            sg["a_ln_g", j], sg["a_ln_b", j], sg["a_b_dw", j] = acc[0], acc[1], acc[2]
            wgrad("a_w_pw2", j, sb, "c", dz1, "1")
            dglu, dwdw = _bwd_conv_a(dcv, glu, conv_w("a_w_dw", j, 32), j)
            sg["a_w_dw", j] = dwdw[:CONV_A]
            g, dh, acc = _bwd_a1(dglu, h, dz1, gw["a_w_pw1"], j)
            sg["a_b_pw1", j] = acc[0]
            wgrad("a_w_pw1", j, x0, "1", dh, "c")
        elif mix == 1:
            (h,) = saved[i]
            g, dh, mb, acc, dws, dbs = _bwd_b(dz1, h, vec("b_ln_g", 0), vec("b_ln_b", 0), gw["b_w_in"], gw["b_w_out"],
                                              ws, wst, bsx)
            sg["b_b_in", 0], sg["b_ln_g", 0], sg["b_ln_b", 0] = acc[0], acc[1, :E], acc[2, :E]
            sg["b_w_s", 0], sg["b_b_s", 0] = dws, jnp.sum(dbs, axis=-1)
            wgrad("b_w_out", 0, mb, "c", dz1, "1")
            wgrad("b_w_in", 0, x0, "1", dh, "c")
        else:
            (hc,) = saved[i]
            wc = conv_w("c_w_conv", 0, 8)
            dy, dbg, mb = _bwd_c2(dz1, hc, wc, gw["c_w_out"])
            wgrad("c_w_out", 0, mb, "c", dz1, "1")
            g, dhc, dwc = _bwd_c1(dy, hc, dbg, dz1, wc, gw["c_w_in"])
            sg["c_w_conv", 0] = dwc[:CONV_C]
            wgrad("c_w_in", 0, x0, "1", dhc, "c")
    grad_x = g[None]

    dws_all = [dw[k] for k in BIG]
    got = _swap_halves(dws_all)
    parts = [_add_halves(d, r, c_idx) for d, r in zip(dws_all, got)]
    contribs = _scatter_partials(parts)
    res = {}
    for k, gc in zip(BIG, contribs):
        nl, kq, n = wts[k].shape
        res[k] = _adam(f"adam_{k}", wts[k], mom[k], var[k], gc.reshape(NQ, nl, kq, n))

    small = SMALL_SHARDED + SMALL_REPL
    gfull = {k: jnp.stack([sg[k, l] for l in range(full[k].shape[0])]) for k in small}
    gsum = _unpack(_sum8(_gather8(_pack([gfull[k] for k in small]))), [full[k].shape for k in small])
    gmine = []
    for k, gs in zip(small, gsum):
        if k in SMALL_SHARDED:
            wdt = wts[k].shape[-1]
            gs = lax.dynamic_slice_in_dim(gs, q_idx * wdt, wdt, axis=gs.ndim - 1)
        gmine.append(gs)
    packed = [_pack(t)[None] for t in ([wts[k] for k in small], [mom[k] for k in small], [var[k] for k in small])]
    outs = _adam("adam_small", packed[0], packed[1], packed[2], _pack(gmine)[None, None])
    unpacked = [_unpack(o[0], [wts[k].shape for k in small]) for o in outs]
    for i, k in enumerate(small):
        res[k] = tuple(u[i] for u in unpacked)

    return (loss, grad_x, *[res[k][0] for k in WEIGHTS], *[res[k][1] for k in WEIGHTS],
            *[res[k][2] for k in WEIGHTS], *[res[k][3] for k in WEIGHTS])
```

```python
import jax
import jax.numpy as jnp
from jax import lax
from jax.experimental import pallas as pl
from jax.experimental.pallas import tpu as pltpu

F32, BF16 = jnp.float32, jnp.bfloat16
S = 4096
D = 1024
E = 2048
FF = 2816
FQ = FF // 4
NQ = 4
DEPTH = 4
ALPHA = (2 * DEPTH) ** 0.25
LN_EPS = 1e-5
CONV_A, CONV_C = 31, 3
HALO_A, HALO_C = 32, 8
SGU_T, SGU_H, SGU_G, SGU_CHUNK = 128, 8, 256, 64
VMEM_LIMIT = 56 * 1024 * 1024
MESH = pl.DeviceIdType.MESH
ADAM_LR, ADAM_B1, ADAM_B2, ADAM_EPS, ADAM_WD, ADAM_STEP = 0.001, 0.9, 0.999, 1e-08, 0.01, 10
GELU_C, GELU_A = 0.7978845608028654, 0.044715

BIG = ["a_w_pw1", "a_w_pw2", "b_w_in", "b_w_out", "c_w_in", "c_w_out",
       "ffn_w_gate", "ffn_w_up", "ffn_w_down", "ple_w_gate", "ple_w_proj"]
SMALL_SHARDED = ["a_b_pw1", "a_w_dw", "a_b_dw", "a_ln_g", "a_ln_b", "c_w_conv"]
SMALL_REPL = ["b_b_in", "b_ln_g", "b_ln_b", "b_w_s", "b_b_s", "ln1_g", "ln1_b", "ln2_g", "ln2_b", "ple_norm_g"]
WEIGHTS = ["a_w_pw1", "a_b_pw1", "a_w_dw", "a_b_dw", "a_ln_g", "a_ln_b", "a_w_pw2", "b_w_in", "b_b_in", "b_ln_g",
           "b_ln_b", "b_w_s", "b_b_s", "b_w_out", "c_w_in", "c_w_conv", "c_w_out", "ln1_g", "ln1_b", "ln2_g",
           "ln2_b", "ffn_w_gate", "ffn_w_up", "ffn_w_down", "ple_w_gate", "ple_w_proj", "ple_norm_g"]


def _call(name, body, grid, in_specs, out_specs, out_shape, scratch=(), aliases=None, hosts=()):
    params = pltpu.CompilerParams(dimension_semantics=("arbitrary",) * len(grid), vmem_limit_bytes=VMEM_LIMIT)
    if not hosts:
        return pl.pallas_call(
            body, name=name, grid=grid, in_specs=in_specs, out_specs=out_specs, out_shape=out_shape,
            scratch_shapes=list(scratch), input_output_aliases=aliases or {}, compiler_params=params)
    assert len(grid) == 1 and not aliases
    single = not isinstance(out_shape, (list, tuple))
    own_shapes = [out_shape] if single else list(out_shape)
    own_specs = [out_specs] if single else list(out_specs)
    n_in, n_out, n_scr = len(in_specs), len(own_shapes), len(scratch)
    h_in = [len(h.arrays) for h in hosts]
    h_out = [len(h.out_shapes) for h in hosts]
    h_sem = [len(h.sems) for h in hosts]

    def split(refs, counts):
        out, off = [], 0
        for cnt in counts:
            out.append(refs[off:off + cnt])
            off += cnt
        return out

    def wrapped(*refs):
        ins, hin, outs, hout, scr, hsem = split(refs, [n_in, sum(h_in), n_out, sum(h_out), n_scr, sum(h_sem)])
        per_host = list(zip(hosts, split(hin, h_in), split(hout, h_out), split(hsem, h_sem)))

        @pl.when(pl.program_id(0) == 0)
        def _():
            for h, a, o, s in per_host:
                h.start(a, o, s)

        body(*ins, *outs, *scr)

        @pl.when(pl.program_id(0) == grid[0] - 1)
        def _():
            for h, a, o, s in per_host:
                h.finish(a, o, s)

    any_spec = pl.BlockSpec(memory_space=pl.ANY)
    call = pl.pallas_call(
        wrapped, name=name, grid=grid, in_specs=list(in_specs) + [any_spec] * sum(h_in),
        out_specs=own_specs + [any_spec] * sum(h_out),
        out_shape=own_shapes + [s for h in hosts for s in h.out_shapes],
        scratch_shapes=list(scratch) + [s for h in hosts for s in h.sems], compiler_params=params)

    def run(*args):
        res = call(*args, *[a for h in hosts for a in h.arrays])
        own = res[0] if single else list(res[:n_out])
        return own, split(list(res[n_out:]), h_out)

    return run


def _sds(shape, dtype=F32):
    return jax.ShapeDtypeStruct(shape, dtype)


def _row(tm, c):
    return pl.BlockSpec((tm, c), lambda i: (i, 0))


def _grow(g, tm, c):
    return pl.BlockSpec((g, tm, c), lambda i: (0, i, 0))


def _const(shape):
    nd = len(shape)
    return pl.BlockSpec(shape, lambda i: (0,) * nd, pipeline_mode=pl.Buffered(1))


def _wspec(w):
    return pl.BlockSpec((NQ, None, w.shape[2], w.shape[3]), lambda i: (0, 0, 0, 0), pipeline_mode=pl.Buffered(1))


def _prev(tm, hb, c):
    return pl.BlockSpec((hb, c), lambda i: (jnp.maximum(i * (tm // hb) - 1, 0), 0))


def _next(tm, hb, c):
    return pl.BlockSpec((hb, c), lambda i: (jnp.minimum((i + 1) * (tm // hb), S // hb - 1), 0))


def _acc(r, c):
    return pl.BlockSpec((r, c), lambda i: (0, 0))


def _sig(x):
    return 1.0 / (1.0 + jnp.exp(-x))


def _ln(z, g, b):
    mu = jnp.mean(z, axis=-1, keepdims=True)
    zc = z - mu
    rstd = lax.rsqrt(jnp.mean(zc * zc, axis=-1, keepdims=True) + LN_EPS)
    xhat = zc * rstd
    return xhat * g + b, xhat, rstd


def _ln_bwd(dyg, xhat, rstd):
    return rstd * (dyg - jnp.mean(dyg, axis=-1, keepdims=True) - xhat * jnp.mean(dyg * xhat, axis=-1, keepdims=True))


def _mm(a, w):
    return jnp.dot(a.astype(BF16), w, preferred_element_type=F32)


def _mmt(a, w):
    return lax.dot_general(a.astype(BF16), w, (((1,), (1,)), ((), ())), preferred_element_type=F32)


def _colsum(x):
    return jnp.sum(x, axis=0, keepdims=True)


def _gelu(x):
    t = jnp.tanh(GELU_C * (x + GELU_A * x * x * x))
    return 0.5 * x * (1.0 + t), t


def _gelu_grad(x, t):
    return 0.5 * (1.0 + t) + 0.5 * x * (1.0 - t * t) * GELU_C * (1.0 + 3.0 * GELU_A * x * x)


def _silu_grad(a, sg):
    return sg * (1.0 + a * (1.0 - sg))


def _sgu_masks():
    r = lax.broadcasted_iota(jnp.int32, (SGU_T, SGU_T), 0) // SGU_CHUNK
    c = lax.broadcasted_iota(jnp.int32, (SGU_T, SGU_T), 1) // SGU_CHUNK
    return r >= c, c >= r


def _fill_halo(buf, lo, n, halo_val_fn, is_edge):
    @pl.when(is_edge)
    def _():
        buf[lo:lo + n, :] = jnp.zeros((n, buf.shape[1]), F32)

    @pl.when(jnp.logical_not(is_edge))
    def _():
        buf[lo:lo + n, :] = halo_val_fn()


def _fwd_a1(x0, w1, b1, l):
    tm = 512

    def body(x_ref, w_ref, b_ref, h_ref, glu_ref):
        xb = x_ref[...].astype(BF16)
        for q in range(NQ):
            sl = slice(q * 512, (q + 1) * 512)
            h_ref[:, sl] = jnp.dot(xb, w_ref[q], preferred_element_type=F32) + b_ref[:, sl]
        glu_ref[...] = h_ref[:, :D] * _sig(h_ref[:, D:])

    return _call(f"fwd_a1_{l}", body, (S // tm,), [_row(tm, D), _wspec(w1), _const((1, 2 * D))],
                 [_row(tm, 2 * D), _row(tm, D)], [_sds((S, 2 * D)), _sds((S, D))])(x0, w1, b1)


def _fwd_a2(glu, x0, wdw, bdw, lg, lb, w2, l, hosts=()):
    tm = 256

    def body(g_ref, gp_ref, x_ref, wdw_ref, bdw_ref, lg_ref, lb_ref, w2_ref, z_ref, cv_ref, buf):
        i = pl.program_id(0)
        _fill_halo(buf, 0, HALO_A, lambda: gp_ref[...], i == 0)
        buf[HALO_A:HALO_A + tm, :] = g_ref[...]
        acc = jnp.zeros((tm, D), F32) + bdw_ref[...]
        for k in range(CONV_A):
            acc = acc + wdw_ref[k:k + 1, :] * buf[pl.ds(HALO_A - (CONV_A - 1) + k, tm), :]
        cv_ref[...] = acc
        n, _, _ = _ln(acc, lg_ref[...], lb_ref[...])
        sb = (n * _sig(n)).astype(BF16)
        hm = _mm(sb[:, 0:256], w2_ref[0])
        for q in range(1, NQ):
            hm = hm + _mm(sb[:, q * 256:(q + 1) * 256], w2_ref[q])
        z_ref[...] = ALPHA * x_ref[...] + hm

    return _call(f"fwd_a2_{l}", body, (S // tm,),
                 [_row(tm, D), _prev(tm, HALO_A, D), _row(tm, D), _const((32, D)), _const((1, D)), _const((1, D)),
                  _const((1, D)), _wspec(w2)],
                 [_row(tm, D), _row(tm, D)], [_sds((S, D)), _sds((S, D))],
                 scratch=[pltpu.VMEM((HALO_A + tm, D), F32)], hosts=hosts)(glu, glu, x0, wdw, bdw, lg, lb, w2)


def _fwd_b(x0, win, b_in, lg, lb, ws, bsx, wout, hosts=()):
    tm = 256

    def body(x_ref, win_ref, bin_ref, lg_ref, lb_ref, ws_ref, bsx_ref, wout_ref, z_ref, h_ref, f_scr):
        xb = x_ref[...].astype(BF16)
        for q in range(NQ):
            sl = slice(q * 1024, (q + 1) * 1024)
            h_ref[:, sl] = jnp.dot(xb, win_ref[q], preferred_element_type=F32) + bin_ref[:, sl]
        u, _ = _gelu(h_ref[:, :E])
        v, _ = _gelu(h_ref[:, E:])
        vn, _, _ = _ln(v, lg_ref[...], lb_ref[...])
        vnb = vn.astype(BF16)
        mask, _ = _sgu_masks()
        for hd in range(SGU_H):
            wm = jnp.where(mask, ws_ref[hd], 0.0).astype(BF16)
            cs = slice(hd * SGU_G, (hd + 1) * SGU_G)
            for n in range(tm // SGU_T):
                rs = slice(n * SGU_T, (n + 1) * SGU_T)
                f_scr[rs, cs] = jnp.dot(wm, vnb[rs, cs], preferred_element_type=F32) + bsx_ref[hd]
        mb = (u * f_scr[...]).astype(BF16)
        out = _mm(mb[:, 0:512], wout_ref[0])
        for q in range(1, NQ):
            out = out + _mm(mb[:, q * 512:(q + 1) * 512], wout_ref[q])
        z_ref[...] = ALPHA * x_ref[...] + out

    return _call("fwd_b", body, (S // tm,),
                 [_row(tm, D), _wspec(win), _const((1, 2 * E)), _const((1, E)), _const((1, E)),
                  _const((SGU_H, SGU_T, SGU_T)), _const((SGU_H, SGU_T, SGU_G)), _wspec(wout)],
                 [_row(tm, D), _row(tm, 2 * E)], [_sds((S, D)), _sds((S, 2 * E))],
                 scratch=[pltpu.VMEM((tm, E), F32)], hosts=hosts)(x0, win, b_in, lg, lb, ws, bsx, wout)


def _fwd_c1(x0, win, hosts=()):
    tm = 512

    def body(x_ref, w_ref, hc_ref):
        xb = x_ref[...].astype(BF16)
        for q in range(NQ):
            hc_ref[:, q * 768:(q + 1) * 768] = jnp.dot(xb, w_ref[q], preferred_element_type=F32)

    return _call("fwd_c1", body, (S // tm,), [_row(tm, D), _wspec(win)], _row(tm, 3 * D),
                 _sds((S, 3 * D)), hosts=hosts)(x0, win)


def _short_conv(buf, hc_ref, hcp_ref, wc_ref, tm, i):
    _fill_halo(buf, 0, HALO_C, lambda: hcp_ref[:, D:2 * D] * hcp_ref[:, 2 * D:], i == 0)
    buf[HALO_C:HALO_C + tm, :] = hc_ref[:, D:2 * D] * hc_ref[:, 2 * D:]
    y = wc_ref[0:1, :] * buf[pl.ds(HALO_C - 2, tm), :]
    for k in range(1, CONV_C):
        y = y + wc_ref[k:k + 1, :] * buf[pl.ds(HALO_C - 2 + k, tm), :]
    return y


def _fwd_c2(hc, x0, wc, wout, hosts=()):
    tm = 256

    def body(hc_ref, hcp_ref, x_ref, wc_ref, wout_ref, z_ref, buf):
        y = _short_conv(buf, hc_ref, hcp_ref, wc_ref, tm, pl.program_id(0))
        mb = (hc_ref[:, :D] * y).astype(BF16)
        out = _mm(mb[:, 0:256], wout_ref[0])
        for q in range(1, NQ):
            out = out + _mm(mb[:, q * 256:(q + 1) * 256], wout_ref[q])
        z_ref[...] = ALPHA * x_ref[...] + out

    return _call("fwd_c2", body, (S // tm,),
                 [_row(tm, 3 * D), _prev(tm, HALO_C, 3 * D), _row(tm, D), _const((8, D)), _wspec(wout)],
                 _row(tm, D), _sds((S, D)), scratch=[pltpu.VMEM((HALO_C + tm, D), F32)], hosts=hosts
                 )(hc, hc, x0, wc, wout)


def _fwd_ffn(z1, lg, lb, wg, wu, wd, l, hosts=()):
    tm = 512

    def body(z_ref, lg_ref, lb_ref, wg_ref, wu_ref, wd_ref, o_ref):
        x1, _, _ = _ln(z_ref[...], lg_ref[...], lb_ref[...])
        xb = x1.astype(BF16)
        f = None
        for q in range(NQ):
            a = jnp.dot(xb, wg_ref[q], preferred_element_type=F32)
            u = jnp.dot(xb, wu_ref[q], preferred_element_type=F32)
            t = _mm(a * _sig(a) * u, wd_ref[q])
            f = t if f is None else f + t
        o_ref[...] = ALPHA * x1 + f

    return _call(f"fwd_ffn_{l}", body, (S // tm,),
                 [_row(tm, D), _const((1, D)), _const((1, D)), _wspec(wg), _wspec(wu), _wspec(wd)],
                 _row(tm, D), _sds((S, D)), hosts=hosts)(z1, lg, lb, wg, wu, wd)


def _ple_parts(z2, p, lg, lb, wg_ref, wp_ref, pg):
    x2, xhat, rstd = _ln(z2, lg, lb)
    xb = x2.astype(BF16)
    gp = _mm(xb[:, 0:256], wg_ref[0])
    for q in range(1, NQ):
        gp = gp + _mm(xb[:, q * 256:(q + 1) * 256], wg_ref[q])
    gate = _sig(gp)
    pb = p.astype(BF16)
    qp = jnp.concatenate([jnp.dot(pb, wp_ref[q], preferred_element_type=F32) for q in range(NQ)], axis=1)
    rs = lax.rsqrt(jnp.mean(qp * qp, axis=-1, keepdims=True) + LN_EPS)
    qn = qp * rs
    return x2, xhat, rstd, xb, gate, qn, rs, qn * pg


def _fwd_ple(z2, p, lg, lb, wg, wp, pg, l, hosts=()):
    tm = 512

    def body(z_ref, p_ref, lg_ref, lb_ref, wg_ref, wp_ref, pg_ref, o_ref):
        x2, _, _, _, gate, _, _, r = _ple_parts(z_ref[...], p_ref[...], lg_ref[...], lb_ref[...], wg_ref, wp_ref,
                                                pg_ref[...])
        o_ref[...] = x2 + gate * r

    return _call(f"fwd_ple_{l}", body, (S // tm,),
                 [_row(tm, D), _row(tm, 256), _const((1, D)), _const((1, D)), _wspec(wg), _wspec(wp),
                  _const((1, D))],
                 _row(tm, D), _sds((S, D)), hosts=hosts)(z2, p, lg, lb, wg, wp, pg)


def _loss_head(y, target):
    tm = 512

    def body(y_ref, t_ref, dy_ref, acc_ref):
        @pl.when(pl.program_id(0) == 0)
        def _():
            acc_ref[...] = jnp.zeros_like(acc_ref)

        e = y_ref[...] - t_ref[...]
        dy_ref[...] = e * (1.0 / D)
        acc_ref[0:1, :] += _colsum(e * e)

    return _call("loss_head", body, (S // tm,), [_row(tm, D), _row(tm, D)], [_row(tm, D), _acc(8, D)],
                 [_sds((S, D)), _sds((8, D))])(y, target)


def _zero_first(*refs):
    @pl.when(pl.program_id(0) == 0)
    def _():
        for r in refs:
            r[...] = jnp.zeros_like(r)


def _bwd_ple(g, z2, p, lg, lb, wg, wp, pg, l, hosts=()):
    tm = 256

    def body(g_ref, z_ref, p_ref, lg_ref, lb_ref, wg_ref, wp_ref, pg_ref, dz_ref, xb_ref, dgp_ref, dqp_ref, acc_ref):
        _zero_first(acc_ref)
        gin = g_ref[...]
        lgv, pgv = lg_ref[...], pg_ref[...]
        _, xhat, rstd, xb, gate, qn, rs, r = _ple_parts(z_ref[...], p_ref[...], lgv, lb_ref[...], wg_ref, wp_ref, pgv)
        xb_ref[...] = xb
        dgpb = (gin * r * gate * (1.0 - gate)).astype(BF16)
        dgp_ref[...] = dgpb
        dx2 = gin + jnp.concatenate([_mmt(dgpb, wg_ref[q]) for q in range(NQ)], axis=1)
        dr = gin * gate
        acc_ref[0:1, :] += _colsum(dr * qn)
        t = dr * pgv
        dqp_ref[...] = (rs * (t - qn * jnp.mean(t * qn, axis=-1, keepdims=True))).astype(BF16)
        acc_ref[1:2, :] += _colsum(dx2 * xhat)
        acc_ref[2:3, :] += _colsum(dx2)
        dz_ref[...] = _ln_bwd(dx2 * lgv, xhat, rstd)

    return _call(f"bwd_ple_{l}", body, (S // tm,),
                 [_row(tm, D), _row(tm, D), _row(tm, 256), _const((1, D)), _const((1, D)), _wspec(wg),
                  _wspec(wp), _const((1, D))],
                 [_row(tm, D), _row(tm, D), _row(tm, D), _row(tm, D), _acc(8, D)],
                 [_sds((S, D)), _sds((S, D), BF16), _sds((S, D), BF16), _sds((S, D), BF16), _sds((8, D))],
                 hosts=hosts)(g, z2, p, lg, lb, wg, wp, pg)


def _bwd_ffn(dz2, z1, lg, lb, wg, wu, wd, l, hosts=()):
    tm = 256

    def body(dz2_ref, z_ref, lg_ref, lb_ref, wg_ref, wu_ref, wd_ref, dz1_ref, xb_ref, hm_ref, da_ref, du_ref, acc_ref):
        _zero_first(acc_ref)
        dz2v = dz2_ref[...]
        dzb = dz2v.astype(BF16)
        lgv = lg_ref[...]
        x1, xhat, rstd = _ln(z_ref[...], lgv, lb_ref[...])
        xb = x1.astype(BF16)
        xb_ref[...] = xb
        dx1 = ALPHA * dz2v
        for q in range(NQ):
            a = jnp.dot(xb, wg_ref[q], preferred_element_type=F32)
            u = jnp.dot(xb, wu_ref[q], preferred_element_type=F32)
            sg = _sig(a)
            s = a * sg
            hm_ref[q] = (s * u).astype(BF16)
            dhm = _mmt(dzb, wd_ref[q])
            dub = (dhm * s).astype(BF16)
            dab = (dhm * u * _silu_grad(a, sg)).astype(BF16)
            da_ref[q] = dab
            du_ref[q] = dub
            dx1 = dx1 + _mmt(dab, wg_ref[q]) + _mmt(dub, wu_ref[q])
        acc_ref[0:1, :] += _colsum(dx1 * xhat)
        acc_ref[1:2, :] += _colsum(dx1)
        dz1_ref[...] = _ln_bwd(dx1 * lgv, xhat, rstd)

    return _call(f"bwd_ffn_{l}", body, (S // tm,),
                 [_row(tm, D), _row(tm, D), _const((1, D)), _const((1, D)), _wspec(wg), _wspec(wu),
                  _wspec(wd)],
                 [_row(tm, D), _row(tm, D), _grow(NQ, tm, FQ), _grow(NQ, tm, FQ), _grow(NQ, tm, FQ), _acc(8, D)],
                 [_sds((S, D)), _sds((S, D), BF16), _sds((NQ, S, FQ), BF16), _sds((NQ, S, FQ), BF16),
                  _sds((NQ, S, FQ), BF16), _sds((8, D))], hosts=hosts)(dz2, z1, lg, lb, wg, wu, wd)


def _bwd_a2(dz1, cv, lg, lb, w2, l, hosts=()):
    tm = 512

    def body(dz_ref, cv_ref, lg_ref, lb_ref, w2_ref, dcv_ref, sb_ref, acc_ref):
        _zero_first(acc_ref)
        lgv = lg_ref[...]
        n, xhat, rstd = _ln(cv_ref[...], lgv, lb_ref[...])
        sg = _sig(n)
        sb_ref[...] = (n * sg).astype(BF16)
        dzb = dz_ref[...].astype(BF16)
        ds = jnp.concatenate([_mmt(dzb, w2_ref[q]) for q in range(NQ)], axis=1)
        dn = ds * _silu_grad(n, sg)
        acc_ref[0:1, :] += _colsum(dn * xhat)
        acc_ref[1:2, :] += _colsum(dn)
        dcv = _ln_bwd(dn * lgv, xhat, rstd)
        acc_ref[2:3, :] += _colsum(dcv)
        dcv_ref[...] = dcv

    return _call(f"bwd_a2_{l}", body, (S // tm,),
                 [_row(tm, D), _row(tm, D), _const((1, D)), _const((1, D)), _wspec(w2)],
                 [_row(tm, D), _row(tm, D), _acc(8, D)],
                 [_sds((S, D)), _sds((S, D), BF16), _sds((8, D))], hosts=hosts)(dz1, cv, lg, lb, w2)


def _bwd_conv_a(dcv, glu, wdw, l, hosts=()):
    tm = 256
    nb = S // tm

    def body(d_ref, dn_ref, g_ref, gp_ref, w_ref, dglu_ref, dw_ref, bufd, bufx):
        i = pl.program_id(0)
        _zero_first(dw_ref)
        bufd[0:tm, :] = d_ref[...]
        _fill_halo(bufd, tm, HALO_A, lambda: dn_ref[...], i == nb - 1)
        _fill_halo(bufx, 0, HALO_A, lambda: gp_ref[...], i == 0)
        bufx[HALO_A:HALO_A + tm, :] = g_ref[...]
        acc = w_ref[0:1, :] * bufd[pl.ds(CONV_A - 1, tm), :]
        for k in range(1, CONV_A):
            acc = acc + w_ref[k:k + 1, :] * bufd[pl.ds(CONV_A - 1 - k, tm), :]
        dglu_ref[...] = acc
        dv = d_ref[...]
        for k in range(CONV_A):
            dw_ref[k:k + 1, :] += _colsum(dv * bufx[pl.ds(HALO_A - (CONV_A - 1) + k, tm), :])

    return _call(f"bwd_conv_a_{l}", body, (nb,),
                 [_row(tm, D), _next(tm, HALO_A, D), _row(tm, D), _prev(tm, HALO_A, D), _const((32, D))],
                 [_row(tm, D), _acc(32, D)], [_sds((S, D)), _sds((32, D))],
                 scratch=[pltpu.VMEM((tm + HALO_A, D), F32), pltpu.VMEM((HALO_A + tm, D), F32)], hosts=hosts
                 )(dcv, dcv, glu, glu, wdw)


def _bwd_a1(dglu, h, dz1, w1, l):
    tm = 256

    def body(dg_ref, h_ref, dz_ref, w_ref, dx_ref, dh_ref, acc_ref):
        _zero_first(acc_ref)
        a, g = h_ref[:, :D], h_ref[:, D:]
        sg = _sig(g)
        dgl = dg_ref[...]
        da = dgl * sg
        dg = dgl * a * sg * (1.0 - sg)
        acc_ref[0:1, 0:D] += _colsum(da)
        acc_ref[0:1, D:2 * D] += _colsum(dg)
        dh_ref[:, 0:D] = da.astype(BF16)
        dh_ref[:, D:2 * D] = dg.astype(BF16)
        dx = ALPHA * dz_ref[...]
        for q in range(NQ):
            dx = dx + _mmt(dh_ref[:, q * 512:(q + 1) * 512], w_ref[q])
        dx_ref[...] = dx

    return _call(f"bwd_a1_{l}", body, (S // tm,),
                 [_row(tm, D), _row(tm, 2 * D), _row(tm, D), _wspec(w1)],
                 [_row(tm, D), _row(tm, 2 * D), _acc(8, 2 * D)],
                 [_sds((S, D)), _sds((S, 2 * D), BF16), _sds((8, 2 * D))])(dglu, h, dz1, w1)


def _bwd_c2(dz1, hc, wc, wout):
    tm = 256

    def body(dz_ref, hc_ref, hcp_ref, wc_ref, wout_ref, dy_ref, dbg_ref, mb_ref, buf):
        y = _short_conv(buf, hc_ref, hcp_ref, wc_ref, tm, pl.program_id(0))
        dzb = dz_ref[...].astype(BF16)
        dm = jnp.concatenate([_mmt(dzb, wout_ref[q]) for q in range(NQ)], axis=1)
        bg = hc_ref[:, :D]
        mb_ref[...] = (bg * y).astype(BF16)
        dbg_ref[...] = (dm * y).astype(BF16)
        dy_ref[...] = dm * bg

    return _call("bwd_c2", body, (S // tm,),
                 [_row(tm, D), _row(tm, 3 * D), _prev(tm, HALO_C, 3 * D), _const((8, D)), _wspec(wout)],
                 [_row(tm, D), _row(tm, D), _row(tm, D)],
                 [_sds((S, D)), _sds((S, D), BF16), _sds((S, D), BF16)],
                 scratch=[pltpu.VMEM((HALO_C + tm, D), F32)])(dz1, hc, hc, wc, wout)


def _bwd_c1(dy, hc, dbg, dz1, wc, win):
    tm = 256
    nb = S // tm

    def body(d_ref, dn_ref, hc_ref, hcp_ref, dbg_ref, dz_ref, wc_ref, win_ref, dx_ref, dhc_ref, dwc_ref, bufd, bufq):
        i = pl.program_id(0)
        _zero_first(dwc_ref)
        bufd[0:tm, :] = d_ref[...]
        _fill_halo(bufd, tm, HALO_C, lambda: dn_ref[...], i == nb - 1)
        _fill_halo(bufq, 0, HALO_C, lambda: hcp_ref[:, D:2 * D] * hcp_ref[:, 2 * D:], i == 0)
        bufq[HALO_C:HALO_C + tm, :] = hc_ref[:, D:2 * D] * hc_ref[:, 2 * D:]
        dq = wc_ref[0:1, :] * bufd[pl.ds(CONV_C - 1, tm), :]
        for k in range(1, CONV_C):
            dq = dq + wc_ref[k:k + 1, :] * bufd[pl.ds(CONV_C - 1 - k, tm), :]
        dv = d_ref[...]
        for k in range(CONV_C):
            dwc_ref[k:k + 1, :] += _colsum(dv * bufq[pl.ds(HALO_C - (CONV_C - 1) + k, tm), :])
        dhc_ref[:, 0:D] = dbg_ref[...]
        dhc_ref[:, D:2 * D] = (dq * hc_ref[:, 2 * D:]).astype(BF16)
        dhc_ref[:, 2 * D:3 * D] = (dq * hc_ref[:, D:2 * D]).astype(BF16)
        dx = ALPHA * dz_ref[...]
        for q in range(NQ):
            dx = dx + _mmt(dhc_ref[:, q * 768:(q + 1) * 768], win_ref[q])
        dx_ref[...] = dx

    return _call("bwd_c1", body, (nb,),
                 [_row(tm, D), _next(tm, HALO_C, D), _row(tm, 3 * D), _prev(tm, HALO_C, 3 * D), _row(tm, D),
                  _row(tm, D), _const((8, D)), _wspec(win)],
                 [_row(tm, D), _row(tm, 3 * D), _acc(8, D)],
                 [_sds((S, D)), _sds((S, 3 * D), BF16), _sds((8, D))],
                 scratch=[pltpu.VMEM((tm + HALO_C, D), F32), pltpu.VMEM((HALO_C + tm, D), F32)]
                 )(dy, dy, hc, hc, dbg, dz1, wc, win)


def _bwd_b(dz1, h, lg, lb, win, wout, ws, wst, bsx):
    tm = 128
    nb = S // tm

    def body(dz_ref, h_ref, lg_ref, lb_ref, win_ref, wout_ref, ws_ref, wst_ref, bsx_ref,
             dx_ref, dh_ref, mb_ref, acc_ref, dws_ref, dbs_ref, f_scr, dvn_scr):
        _zero_first(acc_ref, dws_ref, dbs_ref)
        lgv = lg_ref[...]
        hu, hv = h_ref[:, :E], h_ref[:, E:]
        u, tu = _gelu(hu)
        v, tv = _gelu(hv)
        vn, xhat, rstd = _ln(v, lgv, lb_ref[...])
        vnb = vn.astype(BF16)
        dzb = dz_ref[...].astype(BF16)
        dm = jnp.concatenate([_mmt(dzb, wout_ref[q]) for q in range(NQ)], axis=1)
        mask, mask_t = _sgu_masks()
        for hd in range(SGU_H):
            wm = jnp.where(mask, ws_ref[hd], 0.0).astype(BF16)
            cs = slice(hd * SGU_G, (hd + 1) * SGU_G)
            for n in range(tm // SGU_T):
                rs = slice(n * SGU_T, (n + 1) * SGU_T)
                f_scr[rs, cs] = jnp.dot(wm, vnb[rs, cs], preferred_element_type=F32) + bsx_ref[hd]
        f = f_scr[...]
        mb_ref[...] = (u * f).astype(BF16)
        du = dm * f
        df = dm * u
        dfb = df.astype(BF16)
        for hd in range(SGU_H):
            wmt = jnp.where(mask_t, wst_ref[hd], 0.0).astype(BF16)
            cs = slice(hd * SGU_G, (hd + 1) * SGU_G)
            for n in range(tm // SGU_T):
                rs = slice(n * SGU_T, (n + 1) * SGU_T)
                dvn_scr[rs, cs] = jnp.dot(wmt, dfb[rs, cs], preferred_element_type=F32)
                dws_ref[hd] += lax.dot_general(dfb[rs, cs], vnb[rs, cs], (((1,), (1,)), ((), ())),
                                               preferred_element_type=F32)
                dbs_ref[hd] += df[rs, cs]
        dvn = dvn_scr[...]
        acc_ref[1:2, 0:E] += _colsum(dvn * xhat)
        acc_ref[2:3, 0:E] += _colsum(dvn)
        dv = _ln_bwd(dvn * lgv, xhat, rstd)
        dhu = du * _gelu_grad(hu, tu)
        dhv = dv * _gelu_grad(hv, tv)
        acc_ref[0:1, 0:E] += _colsum(dhu)
        acc_ref[0:1, E:2 * E] += _colsum(dhv)
        dh_ref[:, 0:E] = dhu.astype(BF16)
        dh_ref[:, E:2 * E] = dhv.astype(BF16)
        dx = ALPHA * dz_ref[...]
        for q in range(NQ):
            dx = dx + _mmt(dh_ref[:, q * 1024:(q + 1) * 1024], win_ref[q])
        dx_ref[...] = dx

        @pl.when(pl.program_id(0) == nb - 1)
        def _():
            for hd in range(SGU_H):
                dws_ref[hd] = jnp.where(mask, dws_ref[hd], 0.0)

    c3 = lambda a, b, c: pl.BlockSpec((a, b, c), lambda i: (0, 0, 0))
    return _call("bwd_b", body, (nb,),
                 [_row(tm, D), _row(tm, 2 * E), _const((1, E)), _const((1, E)), _wspec(win), _wspec(wout),
                  _const((SGU_H, SGU_T, SGU_T)), _const((SGU_H, SGU_T, SGU_T)), _const((SGU_H, SGU_T, SGU_G))],
                 [_row(tm, D), _row(tm, 2 * E), _row(tm, E), _acc(8, 2 * E), c3(SGU_H, SGU_T, SGU_T),
                  c3(SGU_H, SGU_T, SGU_G)],
                 [_sds((S, D)), _sds((S, 2 * E), BF16), _sds((S, E), BF16), _sds((8, 2 * E)),
                  _sds((SGU_H, SGU_T, SGU_T)), _sds((SGU_H, SGU_T, SGU_G))],
                 scratch=[pltpu.VMEM((tm, E), F32), pltpu.VMEM((tm, E), F32)]
                 )(dz1, h, lg, lb, win, wout, ws, wst, bsx)


def _mm_tn(name, a, amode, b, bmode, k, n, nl, l, buf):
    ts = min(1024, S)

    def spec(mode, w):
        if mode == "1":
            return pl.BlockSpec((ts, w), lambda g, s: (s, 0))
        if mode == "c":
            return pl.BlockSpec((ts, w), lambda g, s: (s, g))
        return pl.BlockSpec((None, ts, w), lambda g, s: (g, s, 0))

    def body(a_ref, b_ref, *rest):
        o_ref = rest[-1]

        @pl.when(pl.program_id(1) == 0)
        def _():
            o_ref[...] = jnp.zeros_like(o_ref)

        o_ref[...] += lax.dot_general(a_ref[...].astype(BF16), b_ref[...].astype(BF16), (((0,), (0,)), ((), ())),
                                      preferred_element_type=F32)

    in_specs = [spec(amode, k), spec(bmode, n)]
    args = [a, b]
    aliases = {}
    if buf is not None:
        in_specs.append(pl.BlockSpec(memory_space=pl.ANY))
        args.append(buf)
        aliases = {2: 0}
    return _call(name, body, (NQ, S // ts), in_specs,
                 pl.BlockSpec((None, None, k, n), lambda g, s: (l, g, 0, 0)), _sds((nl, NQ, k, n)),
                 aliases=aliases)(*args)


def _row_block(k, cap=256):
    return max(t for t in range(16, min(k, cap) + 1, 16) if k % t == 0)


def _cast_bf16(w):
    nl, k, n = w.shape
    tb = _row_block(k)

    def body(w_ref, o_ref):
        o_ref[...] = w_ref[...].astype(BF16)

    spec = pl.BlockSpec((None, tb, n), lambda l, i: (l, i, 0))
    return _call("cast_bf16", body, (nl, k // tb), [spec], spec, _sds(w.shape, BF16))(w)


def _adam(name, w, m, v, gc, l, prev):
    nl, k, n = w.shape
    nc = gc.shape[0]
    tb = _row_block(k)

    def body(w_ref, m_ref, v_ref, g_ref, *rest):
        go_ref, d_ref, mo_ref, vo_ref = rest[-4:]
        g = g_ref[0].astype(F32)
        for c in range(1, nc):
            g = g + g_ref[c].astype(F32)
        m2 = ADAM_B1 * m_ref[...] + (1.0 - ADAM_B1) * g
        v2 = ADAM_B2 * v_ref[...] + (1.0 - ADAM_B2) * (g * g)
        m_hat = m2 / (1.0 - ADAM_B1 ** ADAM_STEP)
        v_hat = v2 / (1.0 - ADAM_B2 ** ADAM_STEP)
        go_ref[...] = g
        d_ref[...] = -ADAM_LR * (m_hat / (jnp.sqrt(v_hat) + ADAM_EPS) + ADAM_WD * w_ref[...])
        mo_ref[...] = m2
        vo_ref[...] = v2

    spec = pl.BlockSpec((None, tb, n), lambda i: (l, i, 0))
    gspec = pl.BlockSpec((nc, None, tb, n), lambda i: (0, 0, i, 0))
    in_specs, args, aliases = [spec, spec, spec, gspec], [w, m, v, gc], {}
    if prev is not None:
        in_specs += [pl.BlockSpec(memory_space=pl.ANY)] * 4
        args += list(prev)
        aliases = {4 + j: j for j in range(4)}
    return _call(name, body, (k // tb,), in_specs, [spec] * 4, [_sds(w.shape)] * 4, aliases=aliases)(*args)


def _sum8(g8):
    r = g8.shape[1]

    def body(g_ref, o_ref):
        acc = g_ref[0]
        for d in range(1, 8):
            acc = acc + g_ref[d]
        o_ref[...] = acc

    return _call("sum8", body, (1,), [pl.BlockSpec((8, r, 128), lambda i: (0, 0, 0))],
                 pl.BlockSpec((r, 128), lambda i: (0, 0)), _sds((r, 128)))(g8)


def _place():
    x, y, c = lax.axis_index("x"), lax.axis_index("y"), lax.axis_index("c")
    return x, y, c, 2 * x + y, (x, y, 1 - c), [(1 - x, y), (x, 1 - y), (1 - x, 1 - y)]


class _Exchange:
    def __init__(self, arrays, out_shapes):
        self.arrays, self.out_shapes = list(arrays), list(out_shapes)
        n = len(self.arrays)
        self.sems = [pltpu.SemaphoreType.DMA((7 * n,)), pltpu.SemaphoreType.DMA((7 * n,)),
                     pltpu.SemaphoreType.DMA((n,))]

    def _copies(self, ins, outs, sems):
        send, recv, lsem = sems
        local_src, remote_src, dst = self.maps(ins, outs)
        x, y, c, q, sib, chips = _place()

        def rcopy(w, k, qq, cc, to, src=None):
            return pltpu.make_async_remote_copy(
                src_ref=dst(w, qq, cc) if src is None else src, dst_ref=dst(w, qq, cc),
                send_sem=send.at[7 * w + k], recv_sem=recv.at[7 * w + k], device_id=to, device_id_type=MESH)

        def mine(w):
            return pltpu.make_async_copy(local_src(w), dst(w, q, c), lsem.at[w])

        def first(w):
            return [rcopy(w, 0, q, c, sib, local_src(w))] + [
                rcopy(w, 1 + j, q, c, (cx, cy, c), remote_src(w, 2 * cx + cy)) for j, (cx, cy) in enumerate(chips)]

        return rcopy, mine, first, (x, y, c), q, c, sib, chips

    def start(self, ins, outs, sems):
        _, mine, first, *_ = self._copies(ins, outs, sems)
        for w in range(len(self.arrays)):
            mine(w).start()
            for cp in first(w):
                cp.start()

    def finish(self, ins, outs, sems):
        rcopy, mine, first, me, q, c, sib, chips = self._copies(ins, outs, sems)
        n = len(self.arrays)
        for w in range(n):
            for j, (cx, cy) in enumerate(chips):
                rcopy(w, 1 + j, 2 * cx + cy, c, me).wait_recv()
                rcopy(w, 4 + j, 2 * cx + cy, c, sib).start()
        for w in range(n):
            rcopy(w, 0, q, 1 - c, me).wait_recv()
            for j, (cx, cy) in enumerate(chips):
                rcopy(w, 4 + j, 2 * cx + cy, 1 - c, me).wait_recv()
        for w in range(n):
            for cp in first(w):
                cp.wait_send()
            for j, (cx, cy) in enumerate(chips):
                rcopy(w, 4 + j, 2 * cx + cy, c, sib).wait_send()
            mine(w).wait()


class _GatherWeights(_Exchange):
    def __init__(self, items):
        self.layers = [l for _, l in items]
        self.kh = [s.shape[1] // 2 for s, _ in items]
        super().__init__([s for s, _ in items], [_sds((NQ, 1) + s.shape[1:], BF16) for s, _ in items])

    def maps(self, ins, outs):
        c = lax.axis_index("c")
        src = lambda w: ins[w].at[pl.ds(self.layers[w], 1), pl.ds(c * self.kh[w], self.kh[w]), :]
        return src, lambda w, q: src(w), lambda w, q, cc: outs[w].at[q, :, pl.ds(cc * self.kh[w], self.kh[w]), :]


class _ScatterPartials(_Exchange):
    def __init__(self, parts):
        super().__init__(parts, [_sds((NQ, 1, 2) + p.shape[2:], BF16) for p in parts])

    def maps(self, ins, outs):
        q = 2 * lax.axis_index("x") + lax.axis_index("y")
        return (lambda w: ins[w].at[:, q]), (lambda w, qq: ins[w].at[:, qq]), (lambda w, qq, cc: outs[w].at[qq, :, cc])


class _Gather8(_Exchange):
    def __init__(self, v):
        super().__init__([v], [_sds((8,) + v.shape)])

    def maps(self, ins, outs):
        return (lambda w: ins[0]), (lambda w, q: ins[0]), (lambda w, q, cc: outs[0].at[2 * q + cc])


class _SwapHalves:
    def __init__(self, dws):
        self.arrays = list(dws)
        self.kh = [d.shape[2] // 2 for d in dws]
        self.out_shapes = [_sds(d.shape[:2] + (kh,) + d.shape[3:]) for d, kh in zip(dws, self.kh)]
        self.sems = [pltpu.SemaphoreType.DMA((len(dws),)), pltpu.SemaphoreType.DMA((len(dws),))]

    def _copies(self, ins, outs, sems):
        send, recv = sems
        _, _, c, _, sib, _ = _place()
        return [pltpu.make_async_remote_copy(
            src_ref=ins[w].at[:, :, pl.ds((1 - c) * self.kh[w], self.kh[w]), :], dst_ref=outs[w],
            send_sem=send.at[w], recv_sem=recv.at[w], device_id=sib, device_id_type=MESH)
            for w in range(len(self.arrays))]

    def start(self, ins, outs, sems):
        for cp in self._copies(ins, outs, sems):
            cp.start()

    def finish(self, ins, outs, sems):
        for cp in self._copies(ins, outs, sems):
            cp.wait()


def _comm_only(name, host):
    n_in, n_out = len(host.arrays), len(host.out_shapes)

    def body(*refs):
        ins, outs, sems = refs[:n_in], refs[n_in:n_in + n_out], refs[n_in + n_out:]
        host.start(ins, outs, sems)
        host.finish(ins, outs, sems)

    any_spec = pl.BlockSpec(memory_space=pl.ANY)
    return pl.pallas_call(body, name=name, in_specs=[any_spec] * n_in, out_specs=[any_spec] * n_out,
                          out_shape=host.out_shapes, scratch_shapes=host.sems)(*host.arrays)


def _add_halves(dw, got, cidx):
    nl, _, k, n = dw.shape
    kh = k // 2

    def body(c_ref, a_ref, b_ref, o_ref):
        o_ref[...] = (a_ref[...] + b_ref[...]).astype(BF16)

    grid_spec = pltpu.PrefetchScalarGridSpec(
        num_scalar_prefetch=1, grid=(nl, NQ),
        in_specs=[pl.BlockSpec((None, None, None, kh, n), lambda l, q, c_ref: (l, q, c_ref[0], 0, 0)),
                  pl.BlockSpec((None, None, kh, n), lambda l, q, c_ref: (l, q, 0, 0))],
        out_specs=pl.BlockSpec((None, None, kh, n), lambda l, q, c_ref: (l, q, 0, 0)))
    return pl.pallas_call(
        body, name="add_halves", grid_spec=grid_spec, out_shape=_sds((nl, NQ, kh, n), BF16),
        compiler_params=pltpu.CompilerParams(dimension_semantics=("arbitrary", "arbitrary"),
                                             vmem_limit_bytes=VMEM_LIMIT))(cidx, dw.reshape(nl, NQ, 2, kh, n), got)


def _gather8(name, v):
    return _comm_only(name, _Gather8(v))[0]


PACK = 16 * 128


def _pack(arrays):
    parts = []
    for a in arrays:
        flat = a.reshape(-1)
        parts.append(jnp.pad(flat, (0, (-flat.shape[0]) % PACK)))
    return jnp.concatenate(parts).reshape(-1, 128)


def _unpack(packed, shapes):
    flat = packed.reshape(-1)
    out, off = [], 0
    for shp in shapes:
        size = 1
        for d in shp:
            size *= d
        out.append(flat[off:off + size].reshape(shp))
        off += size + (-size) % PACK
    return out


def kernel(x, p, a_w_pw1, a_b_pw1, a_w_dw, a_b_dw, a_ln_g, a_ln_b, a_w_pw2, b_w_in, b_b_in, b_ln_g, b_ln_b, b_w_s, b_b_s, b_w_out, c_w_in, c_w_conv, c_w_out, ln1_g, ln1_b, ln2_g, ln2_b, ffn_w_gate, ffn_w_up, ffn_w_down, ple_w_gate, ple_w_proj, ple_norm_g, loss_target, m_a_w_pw1, m_a_b_pw1, m_a_w_dw, m_a_b_dw, m_a_ln_g, m_a_ln_b, m_a_w_pw2, m_b_w_in, m_b_b_in, m_b_ln_g, m_b_ln_b, m_b_w_s, m_b_b_s, m_b_w_out, m_c_w_in, m_c_w_conv, m_c_w_out, m_ln1_g, m_ln1_b, m_ln2_g, m_ln2_b, m_ffn_w_gate, m_ffn_w_up, m_ffn_w_down, m_ple_w_gate, m_ple_w_proj, m_ple_norm_g, v_a_w_pw1, v_a_b_pw1, v_a_w_dw, v_a_b_dw, v_a_ln_g, v_a_ln_b, v_a_w_pw2, v_b_w_in, v_b_b_in, v_b_ln_g, v_b_ln_b, v_b_w_s, v_b_b_s, v_b_w_out, v_c_w_in, v_c_w_conv, v_c_w_out, v_ln1_g, v_ln1_b, v_ln2_g, v_ln2_b, v_ffn_w_gate, v_ffn_w_up, v_ffn_w_down, v_ple_w_gate, v_ple_w_proj, v_ple_norm_g):
    args = dict(locals())
    wts = {k: args[k] for k in WEIGHTS}
    mom = {k: args["m_" + k] for k in WEIGHTS}
    var = {k: args["v_" + k] for k in WEIGHTS}
    q_idx = 2 * lax.axis_index("x") + lax.axis_index("y")
    c_idx = lax.axis_index("c").astype(jnp.int32).reshape(1)

    wb = {k: _cast_bf16(wts[k]) for k in BIG}
    mixw = [[("a_w_pw1", 0), ("a_w_pw2", 0)], [("b_w_in", 0), ("b_w_out", 0)], [("c_w_in", 0), ("c_w_out", 0)],
            [("a_w_pw1", 1), ("a_w_pw2", 1)]]
    ffnw = [[("ffn_w_gate", l), ("ffn_w_up", l), ("ffn_w_down", l)] for l in range(DEPTH)]
    plew = [[("ple_w_gate", l), ("ple_w_proj", l)] for l in range(DEPTH)]
    fwd_plan = {("a2", 0): ffnw[0], ("ffn", 0): mixw[1], ("ple", 0): plew[1],
                ("b", 1): ffnw[1], ("ffn", 1): mixw[2], ("ple", 1): plew[2],
                ("c1", 2): ffnw[2][:2], ("c2", 2): ffnw[2][2:], ("ffn", 2): mixw[3], ("ple", 2): plew[3],
                ("a2", 3): ffnw[3]}
    gw = {}

    def gather(keys):
        return _GatherWeights([(wb[name], l) for name, l in keys])

    def hosted(tag, fn, *fargs):
        keys = fwd_plan.get(tag)
        if not keys:
            return fn(*fargs)
        own, (got,) = fn(*fargs, hosts=[gather(keys)])
        gw.update(zip(keys, got))
        return own

    first_keys = mixw[0] + plew[0]
    gw.update(zip(first_keys, _comm_only("gather_first", gather(first_keys))))
    shard_shapes = [wts[k].shape for k in SMALL_SHARDED]
    small8 = _gather8("gather_small", _pack([wts[k] for k in SMALL_SHARDED]))
    per_chip = [_unpack(small8[2 * qq], shard_shapes) for qq in range(NQ)]
    full = {k: jnp.concatenate([per_chip[qq][i] for qq in range(NQ)], axis=-1) for i, k in enumerate(SMALL_SHARDED)}
    for k in SMALL_REPL:
        full[k] = wts[k]

    def vec(name, l):
        return full[name][l][None, :]

    def conv_w(name, l, rows):
        w = full[name][l]
        return jnp.pad(w, ((0, rows - w.shape[0]), (0, 0)))

    ws = full["b_w_s"][0]
    wst = jnp.transpose(ws, (0, 2, 1))
    bsx = jnp.broadcast_to(full["b_b_s"][0][:, :, None], (SGU_H, SGU_T, SGU_G))

    x0s, z1s, z2s, saved = [], [], [], []
    cur = x[0]
    for i in range(DEPTH):
        mix, j = i % 3, i // 3
        x0s.append(cur)
        if mix == 0:
            h, glu = _fwd_a1(cur, gw["a_w_pw1", j], vec("a_b_pw1", j), i)
            z1, cv = hosted(("a2", i), _fwd_a2, glu, cur, conv_w("a_w_dw", j, 32), vec("a_b_dw", j), vec("a_ln_g", j),
                            vec("a_ln_b", j), gw["a_w_pw2", j], i)
            saved.append((h, glu, cv))
        elif mix == 1:
            z1, h = hosted(("b", i), _fwd_b, cur, gw["b_w_in", 0], vec("b_b_in", 0), vec("b_ln_g", 0),
                           vec("b_ln_b", 0), ws, bsx, gw["b_w_out", 0])
            saved.append((h,))
        else:
            hc = hosted(("c1", i), _fwd_c1, cur, gw["c_w_in", 0])
            z1 = hosted(("c2", i), _fwd_c2, hc, cur, conv_w("c_w_conv", 0, 8), gw["c_w_out", 0])
            saved.append((hc,))
        z2 = hosted(("ffn", i), _fwd_ffn, z1, vec("ln1_g", i), vec("ln1_b", i), gw["ffn_w_gate", i],
                    gw["ffn_w_up", i], gw["ffn_w_down", i], i)
        cur = hosted(("ple", i), _fwd_ple, z2, p[i, 0], vec("ln2_g", i), vec("ln2_b", i), gw["ple_w_gate", i],
                     gw["ple_w_proj", i], vec("ple_norm_g", i), i)
        z1s.append(z1)
        z2s.append(z2)

    g, loss_acc = _loss_head(cur, loss_target[0])
    loss = lax.psum(0.5 / D * jnp.sum(loss_acc[0]), ("x", "y", "c"))

    dws = {}
    sg = {}
    res = {k: None for k in BIG}

    def wgrad(name, l, a, amode, b, bmode):
        _, k, n = wts[name].shape
        dws[name, l] = _mm_tn(f"dw_{name}_{l}", a, amode, b, bmode, k, n, 1, 0, None)

    def swap(keys):
        return _SwapHalves([dws[k] for k in keys])

    def add_halves(keys, got):
        return [_add_halves(dws[k], r, c_idx) for k, r in zip(keys, got)]

    def update(keys, contribs):
        for (name, l), gc in zip(keys, contribs):
            _, kq, n = wts[name].shape
            res[name] = _adam(f"adam_{name}_{l}", wts[name], mom[name], var[name], gc.reshape(NQ, 1, kq, n), l,
                              res[name])

    pending = None
    for i in reversed(range(DEPTH)):
        mix, j = i % 3, i // 3
        ple_args = (g, z2s[i], p[i, 0], vec("ln2_g", i), vec("ln2_b", i), gw["ple_w_gate", i], gw["ple_w_proj", i],
                    vec("ple_norm_g", i), i)
        if pending:
            (dz2, x2b, dgp, dqp, acc), (got,) = _bwd_ple(*ple_args, hosts=[swap(pending)])
            parts = add_halves(pending, got)
        else:
            dz2, x2b, dgp, dqp, acc = _bwd_ple(*ple_args)
        sg["ple_norm_g", i], sg["ln2_g", i], sg["ln2_b", i] = acc[0], acc[1], acc[2]
        wgrad("ple_w_gate", i, x2b, "c", dgp, "1")
        wgrad("ple_w_proj", i, p[i, 0], "1", dqp, "c")
        ffn_args = (dz2, z1s[i], vec("ln1_g", i), vec("ln1_b", i), gw["ffn_w_gate", i], gw["ffn_w_up", i],
                    gw["ffn_w_down", i], i)
        if pending:
            (dz1, x1b, hm, da, du, acc), (contribs,) = _bwd_ffn(*ffn_args, hosts=[_ScatterPartials(parts)])
            update(pending, contribs)
        else:
            dz1, x1b, hm, da, du, acc = _bwd_ffn(*ffn_args)
        sg["ln1_g", i], sg["ln1_b", i] = acc[0], acc[1]
        wgrad("ffn_w_gate", i, x1b, "1", da, "g")
        wgrad("ffn_w_up", i, x1b, "1", du, "g")
        wgrad("ffn_w_down", i, hm, "g", dz2, "1")
        x0 = x0s[i]
        if mix == 0:
            h, glu, cv = saved[i]
            a2_args = (dz1, cv, vec("a_ln_g", j), vec("a_ln_b", j), gw["a_w_pw2", j], i)
            conv_args = (glu, conv_w("a_w_dw", j, 32), i)
            if i == 0:
                early = ffnw[0] + plew[0]
                (dcv, sb, acc), (got,) = _bwd_a2(*a2_args, hosts=[swap(early)])
                parts = add_halves(early, got)
                (dglu, dwdw), (contribs,) = _bwd_conv_a(dcv, *conv_args, hosts=[_ScatterPartials(parts)])
                update(early, contribs)
            else:
                dcv, sb, acc = _bwd_a2(*a2_args)
                dglu, dwdw = _bwd_conv_a(dcv, *conv_args)
            sg["a_ln_g", j], sg["a_ln_b", j], sg["a_b_dw", j] = acc[0], acc[1], acc[2]
            wgrad("a_w_pw2", j, sb, "c", dz1, "1")
            sg["a_w_dw", j] = dwdw[:CONV_A]
            g, dh, acc = _bwd_a1(dglu, h, dz1, gw["a_w_pw1", j], i)
            sg["a_b_pw1", j] = acc[0]
            wgrad("a_w_pw1", j, x0, "1", dh, "c")
        elif mix == 1:
            (h,) = saved[i]
            g, dh, mb, acc, dw_s, db_s = _bwd_b(dz1, h, vec("b_ln_g", 0), vec("b_ln_b", 0), gw["b_w_in", 0],
                                                gw["b_w_out", 0], ws, wst, bsx)
            sg["b_b_in", 0], sg["b_ln_g", 0], sg["b_ln_b", 0] = acc[0], acc[1, :E], acc[2, :E]
            sg["b_w_s", 0], sg["b_b_s", 0] = dw_s, jnp.sum(db_s, axis=-1)
            wgrad("b_w_out", 0, mb, "c", dz1, "1")
            wgrad("b_w_in", 0, x0, "1", dh, "c")
        else:
            (hc,) = saved[i]
            wc = conv_w("c_w_conv", 0, 8)
            dy, dbg, mb = _bwd_c2(dz1, hc, wc, gw["c_w_out", 0])
            wgrad("c_w_out", 0, mb, "c", dz1, "1")
            g, dhc, dwc = _bwd_c1(dy, hc, dbg, dz1, wc, gw["c_w_in", 0])
            sg["c_w_conv", 0] = dwc[:CONV_C]
            wgrad("c_w_in", 0, x0, "1", dhc, "c")
        pending = mixw[i] + ffnw[i] + plew[i] if i > 0 else mixw[0]
    grad_x = g[None]
    parts = add_halves(pending, _comm_only("swap_last", swap(pending)))
    update(pending, _comm_only("scatter_last", _ScatterPartials(parts)))

    small = SMALL_SHARDED + SMALL_REPL
    gfull = {k: jnp.stack([sg[k, l] for l in range(full[k].shape[0])]) for k in small}
    gsum = _unpack(_sum8(_gather8("gather_small_grads", _pack([gfull[k] for k in small]))),
                   [full[k].shape for k in small])
    gmine = []
    for k, gs in zip(small, gsum):
        if k in SMALL_SHARDED:
            wdt = wts[k].shape[-1]
            gs = lax.dynamic_slice_in_dim(gs, q_idx * wdt, wdt, axis=gs.ndim - 1)
        gmine.append(gs)
    packed = [_pack(t)[None] for t in ([wts[k] for k in small], [mom[k] for k in small], [var[k] for k in small])]
    outs = _adam("adam_small", packed[0], packed[1], packed[2], _pack(gmine)[None, None], 0, None)
    unpacked = [_unpack(o[0], [wts[k].shape for k in small]) for o in outs]
    for i, k in enumerate(small):
        res[k] = tuple(u[i] for u in unpacked)

    return (loss, grad_x, *[res[k][0] for k in WEIGHTS], *[res[k][1] for k in WEIGHTS],
            *[res[k][2] for k in WEIGHTS], *[res[k][3] for k in WEIGHTS])
```

```python
import jax
import jax.numpy as jnp
from jax import lax
from jax.experimental import pallas as pl
from jax.experimental.pallas import tpu as pltpu

F32, BF16 = jnp.float32, jnp.bfloat16
S = 4096
D = 1024
E = 2048
FF = 2816
FQ = FF // 4
NQ = 4
DEPTH = 4
ALPHA = (2 * DEPTH) ** 0.25
LN_EPS = 1e-5
CONV_A, CONV_C = 31, 3
HALO_A, HALO_C = 32, 8
SGU_T, SGU_H, SGU_G, SGU_CHUNK = 128, 8, 256, 64
VMEM_LIMIT = 56 * 1024 * 1024
MESH = pl.DeviceIdType.MESH
ADAM_LR, ADAM_B1, ADAM_B2, ADAM_EPS, ADAM_WD, ADAM_STEP = 0.001, 0.9, 0.999, 1e-08, 0.01, 10
GELU_C, GELU_A = 0.7978845608028654, 0.044715

BIG = ["a_w_pw1", "a_w_pw2", "b_w_in", "b_w_out", "c_w_in", "c_w_out",
       "ffn_w_gate", "ffn_w_up", "ffn_w_down", "ple_w_gate", "ple_w_proj"]
TRANSPOSED = ["ffn_w_gate", "ffn_w_up"]
SMALL_SHARDED = ["a_b_pw1", "a_w_dw", "a_b_dw", "a_ln_g", "a_ln_b", "c_w_conv"]
SMALL_REPL = ["b_b_in", "b_ln_g", "b_ln_b", "b_w_s", "b_b_s", "ln1_g", "ln1_b", "ln2_g", "ln2_b", "ple_norm_g"]
WEIGHTS = ["a_w_pw1", "a_b_pw1", "a_w_dw", "a_b_dw", "a_ln_g", "a_ln_b", "a_w_pw2", "b_w_in", "b_b_in", "b_ln_g",
           "b_ln_b", "b_w_s", "b_b_s", "b_w_out", "c_w_in", "c_w_conv", "c_w_out", "ln1_g", "ln1_b", "ln2_g",
           "ln2_b", "ffn_w_gate", "ffn_w_up", "ffn_w_down", "ple_w_gate", "ple_w_proj", "ple_norm_g"]


def _call(name, body, grid, in_specs, out_specs, out_shape, scratch=(), aliases=None, hosts=()):
    params = pltpu.CompilerParams(dimension_semantics=("arbitrary",) * len(grid), vmem_limit_bytes=VMEM_LIMIT)
    if not hosts:
        return pl.pallas_call(
            body, name=name, grid=grid, in_specs=in_specs, out_specs=out_specs, out_shape=out_shape,
            scratch_shapes=list(scratch), input_output_aliases=aliases or {}, compiler_params=params)
    assert len(grid) == 1 and not aliases
    single = not isinstance(out_shape, (list, tuple))
    own_shapes = [out_shape] if single else list(out_shape)
    own_specs = [out_specs] if single else list(out_specs)
    n_in, n_out, n_scr = len(in_specs), len(own_shapes), len(scratch)
    h_in = [len(h.arrays) for h in hosts]
    h_out = [len(h.out_shapes) for h in hosts]
    h_sem = [len(h.sems) for h in hosts]

    def split(refs, counts):
        out, off = [], 0
        for cnt in counts:
            out.append(refs[off:off + cnt])
            off += cnt
        return out

    def wrapped(*refs):
        ins, hin, outs, hout, scr, hsem = split(refs, [n_in, sum(h_in), n_out, sum(h_out), n_scr, sum(h_sem)])
        per_host = list(zip(hosts, split(hin, h_in), split(hout, h_out), split(hsem, h_sem)))

        @pl.when(pl.program_id(0) == 0)
        def _():
            for h, a, o, s in per_host:
                h.start(a, o, s)

        body(*ins, *outs, *scr)

        @pl.when(pl.program_id(0) == grid[0] - 1)
        def _():
            for h, a, o, s in per_host:
                h.finish(a, o, s)

    any_spec = pl.BlockSpec(memory_space=pl.ANY)
    call = pl.pallas_call(
        wrapped, name=name, grid=grid, in_specs=list(in_specs) + [any_spec] * sum(h_in),
        out_specs=own_specs + [any_spec] * sum(h_out),
        out_shape=own_shapes + [s for h in hosts for s in h.out_shapes],
        scratch_shapes=list(scratch) + [s for h in hosts for s in h.sems], compiler_params=params)

    def run(*args):
        res = call(*args, *[a for h in hosts for a in h.arrays])
        own = res[0] if single else list(res[:n_out])
        return own, split(list(res[n_out:]), h_out)

    return run


def _sds(shape, dtype=F32):
    return jax.ShapeDtypeStruct(shape, dtype)


def _row(tm, c):
    return pl.BlockSpec((tm, c), lambda i: (i, 0))


def _grow(g, tm, c):
    return pl.BlockSpec((g, tm, c), lambda i: (0, i, 0))


def _const(shape):
    nd = len(shape)
    return pl.BlockSpec(shape, lambda i: (0,) * nd, pipeline_mode=pl.Buffered(1))


def _wspec(w):
    return pl.BlockSpec((NQ, None, w.shape[2], w.shape[3]), lambda i: (0, 0, 0, 0), pipeline_mode=pl.Buffered(1))


def _prev(tm, hb, c):
    return pl.BlockSpec((hb, c), lambda i: (jnp.maximum(i * (tm // hb) - 1, 0), 0))


def _next(tm, hb, c):
    return pl.BlockSpec((hb, c), lambda i: (jnp.minimum((i + 1) * (tm // hb), S // hb - 1), 0))


def _acc(r, c):
    return pl.BlockSpec((r, c), lambda i: (0, 0))


def _sig(x):
    return 1.0 / (1.0 + jnp.exp(-x))


def _ln(z, g, b):
    mu = jnp.mean(z, axis=-1, keepdims=True)
    zc = z - mu
    rstd = lax.rsqrt(jnp.mean(zc * zc, axis=-1, keepdims=True) + LN_EPS)
    xhat = zc * rstd
    return xhat * g + b, xhat, rstd


def _ln_bwd(dyg, xhat, rstd):
    return rstd * (dyg - jnp.mean(dyg, axis=-1, keepdims=True) - xhat * jnp.mean(dyg * xhat, axis=-1, keepdims=True))


def _mm(a, w):
    return jnp.dot(a.astype(BF16), w, preferred_element_type=F32)


def _mmt(a, w):
    return lax.dot_general(a.astype(BF16), w, (((1,), (1,)), ((), ())), preferred_element_type=F32)


def _colsum(x):
    return jnp.sum(x, axis=0, keepdims=True)


def _gelu(x):
    t = jnp.tanh(GELU_C * (x + GELU_A * x * x * x))
    return 0.5 * x * (1.0 + t), t


def _gelu_grad(x, t):
    return 0.5 * (1.0 + t) + 0.5 * x * (1.0 - t * t) * GELU_C * (1.0 + 3.0 * GELU_A * x * x)


def _silu_grad(a, sg):
    return sg * (1.0 + a * (1.0 - sg))


def _sgu_masks():
    r = lax.broadcasted_iota(jnp.int32, (SGU_T, SGU_T), 0) // SGU_CHUNK
    c = lax.broadcasted_iota(jnp.int32, (SGU_T, SGU_T), 1) // SGU_CHUNK
    return r >= c, c >= r


def _fill_halo(buf, lo, n, halo_val_fn, is_edge):
    @pl.when(is_edge)
    def _():
        buf[lo:lo + n, :] = jnp.zeros((n, buf.shape[1]), F32)

    @pl.when(jnp.logical_not(is_edge))
    def _():
        buf[lo:lo + n, :] = halo_val_fn()


SUB, LANE = 8, 128
ROWS_AT_ONCE = 16


def _shift_copies(buf, sh):
    rows = sh.shape[1]
    for s in range(1, SUB):
        sh[s - 1, :, :] = buf[pl.ds(s, rows), :]


def _tap(buf, sh, o, base, lanes):
    m, s = divmod(o, SUB)
    rows = pl.ds(pl.multiple_of(base + m * SUB, SUB), SUB)
    return buf[rows, lanes] if s == 0 else sh[s - 1, rows, lanes]


def _conv_rows(out_ref, w_ref, bias_ref, offsets, buf, sh, tm):
    for cb in range(D // LANE):
        lanes = slice(cb * LANE, (cb + 1) * LANE)
        wv = [jnp.broadcast_to(w_ref[k:k + 1, lanes], (SUB, LANE)) for k in range(len(offsets))]
        bias = None if bias_ref is None else jnp.broadcast_to(bias_ref[:, lanes], (SUB, LANE))

        def body(jb, carry):
            accs = [bias] * ROWS_AT_ONCE
            for k, o in enumerate(offsets):
                for jj in range(ROWS_AT_ONCE):
                    t = wv[k] * _tap(buf, sh, o, (jb * ROWS_AT_ONCE + jj) * SUB, lanes)
                    accs[jj] = t if accs[jj] is None else accs[jj] + t
            for jj in range(ROWS_AT_ONCE):
                out_ref[pl.ds(pl.multiple_of((jb * ROWS_AT_ONCE + jj) * SUB, SUB), SUB), lanes] = accs[jj]
            return carry

        lax.fori_loop(0, tm // (SUB * ROWS_AT_ONCE), body, 0)


def _conv_wgrad(dw_ref, d_ref, offsets, buf, sh, tm):
    for cb in range(D // LANE):
        lanes = slice(cb * LANE, (cb + 1) * LANE)

        def body(j, accs):
            base = j * SUB
            d = d_ref[pl.ds(pl.multiple_of(base, SUB), SUB), lanes]
            return tuple(acc + d * _tap(buf, sh, o, base, lanes) for acc, o in zip(accs, offsets))

        accs = lax.fori_loop(0, tm // SUB, body, tuple(jnp.zeros((SUB, LANE), F32) for _ in offsets), unroll=4)
        for k, acc in enumerate(accs):
            dw_ref[k:k + 1, lanes] += jnp.sum(acc, axis=0, keepdims=True)


def _fwd_a1(x0, w1, b1, l):
    tm = 512

    def body(x_ref, w_ref, b_ref, h_ref, glu_ref):
        xb = x_ref[...].astype(BF16)
        for q in range(NQ):
            sl = slice(q * 512, (q + 1) * 512)
            h_ref[:, sl] = jnp.dot(xb, w_ref[q], preferred_element_type=F32) + b_ref[:, sl]
        glu_ref[...] = h_ref[:, :D] * _sig(h_ref[:, D:])

    return _call(f"fwd_a1_{l}", body, (S // tm,), [_row(tm, D), _wspec(w1), _const((1, 2 * D))],
                 [_row(tm, 2 * D), _row(tm, D)], [_sds((S, 2 * D)), _sds((S, D))])(x0, w1, b1)


def _fwd_a2(glu, x0, wdw, bdw, lg, lb, w2, l, hosts=()):
    tm = 256

    def body(g_ref, gp_ref, x_ref, wdw_ref, bdw_ref, lg_ref, lb_ref, w2_ref, z_ref, cv_ref, buf, sh):
        i = pl.program_id(0)
        _fill_halo(buf, 0, HALO_A, lambda: gp_ref[...], i == 0)
        buf[HALO_A:HALO_A + tm, :] = g_ref[...]
        _shift_copies(buf, sh)
        _conv_rows(cv_ref, wdw_ref, bdw_ref, [HALO_A - (CONV_A - 1) + k for k in range(CONV_A)], buf, sh, tm)
        n, _, _ = _ln(cv_ref[...], lg_ref[...], lb_ref[...])
        sb = (n * _sig(n)).astype(BF16)
        hm = _mm(sb[:, 0:256], w2_ref[0])
        for q in range(1, NQ):
            hm = hm + _mm(sb[:, q * 256:(q + 1) * 256], w2_ref[q])
        z_ref[...] = ALPHA * x_ref[...] + hm

    return _call(f"fwd_a2_{l}", body, (S // tm,),
                 [_row(tm, D), _prev(tm, HALO_A, D), _row(tm, D), _const((32, D)), _const((1, D)), _const((1, D)),
                  _const((1, D)), _wspec(w2)],
                 [_row(tm, D), _row(tm, D)], [_sds((S, D)), _sds((S, D))],
                 scratch=[pltpu.VMEM((HALO_A + tm, D), F32), pltpu.VMEM((SUB - 1, HALO_A + tm - SUB, D), F32)],
                 hosts=hosts)(glu, glu, x0, wdw, bdw, lg, lb, w2)


def _fwd_b(x0, win, b_in, lg, lb, ws, bsx, wout, hosts=()):
    tm = 256

    def body(x_ref, win_ref, bin_ref, lg_ref, lb_ref, ws_ref, bsx_ref, wout_ref, z_ref, h_ref, f_scr):
        xb = x_ref[...].astype(BF16)
        for q in range(NQ):
            sl = slice(q * 1024, (q + 1) * 1024)
            h_ref[:, sl] = jnp.dot(xb, win_ref[q], preferred_element_type=F32) + bin_ref[:, sl]
        u, _ = _gelu(h_ref[:, :E])
        v, _ = _gelu(h_ref[:, E:])
        vn, _, _ = _ln(v, lg_ref[...], lb_ref[...])
        vnb = vn.astype(BF16)
        mask, _ = _sgu_masks()
        for hd in range(SGU_H):
            wm = jnp.where(mask, ws_ref[hd], 0.0).astype(BF16)
            cs = slice(hd * SGU_G, (hd + 1) * SGU_G)
            for n in range(tm // SGU_T):
                rs = slice(n * SGU_T, (n + 1) * SGU_T)
                f_scr[rs, cs] = jnp.dot(wm, vnb[rs, cs], preferred_element_type=F32) + bsx_ref[hd]
        mb = (u * f_scr[...]).astype(BF16)
        out = _mm(mb[:, 0:512], wout_ref[0])
        for q in range(1, NQ):
            out = out + _mm(mb[:, q * 512:(q + 1) * 512], wout_ref[q])
        z_ref[...] = ALPHA * x_ref[...] + out

    return _call("fwd_b", body, (S // tm,),
                 [_row(tm, D), _wspec(win), _const((1, 2 * E)), _const((1, E)), _const((1, E)),
                  _const((SGU_H, SGU_T, SGU_T)), _const((SGU_H, SGU_T, SGU_G)), _wspec(wout)],
                 [_row(tm, D), _row(tm, 2 * E)], [_sds((S, D)), _sds((S, 2 * E))],
                 scratch=[pltpu.VMEM((tm, E), F32)], hosts=hosts)(x0, win, b_in, lg, lb, ws, bsx, wout)


def _fwd_c1(x0, win, hosts=()):
    tm = 512

    def body(x_ref, w_ref, hc_ref):
        xb = x_ref[...].astype(BF16)
        for q in range(NQ):
            hc_ref[:, q * 768:(q + 1) * 768] = jnp.dot(xb, w_ref[q], preferred_element_type=F32)

    return _call("fwd_c1", body, (S // tm,), [_row(tm, D), _wspec(win)], _row(tm, 3 * D),
                 _sds((S, 3 * D)), hosts=hosts)(x0, win)


def _short_conv(buf, hc_ref, hcp_ref, wc_ref, tm, i):
    _fill_halo(buf, 0, HALO_C, lambda: hcp_ref[:, D:2 * D] * hcp_ref[:, 2 * D:], i == 0)
    buf[HALO_C:HALO_C + tm, :] = hc_ref[:, D:2 * D] * hc_ref[:, 2 * D:]
    y = wc_ref[0:1, :] * buf[pl.ds(HALO_C - 2, tm), :]
    for k in range(1, CONV_C):
        y = y + wc_ref[k:k + 1, :] * buf[pl.ds(HALO_C - 2 + k, tm), :]
    return y


def _fwd_c2(hc, x0, wc, wout, hosts=()):
    tm = 256

    def body(hc_ref, hcp_ref, x_ref, wc_ref, wout_ref, z_ref, buf):
        y = _short_conv(buf, hc_ref, hcp_ref, wc_ref, tm, pl.program_id(0))
        mb = (hc_ref[:, :D] * y).astype(BF16)
        out = _mm(mb[:, 0:256], wout_ref[0])
        for q in range(1, NQ):
            out = out + _mm(mb[:, q * 256:(q + 1) * 256], wout_ref[q])
        z_ref[...] = ALPHA * x_ref[...] + out

    return _call("fwd_c2", body, (S // tm,),
                 [_row(tm, 3 * D), _prev(tm, HALO_C, 3 * D), _row(tm, D), _const((8, D)), _wspec(wout)],
                 _row(tm, D), _sds((S, D)), scratch=[pltpu.VMEM((HALO_C + tm, D), F32)], hosts=hosts
                 )(hc, hc, x0, wc, wout)


def _fwd_ffn(z1, lg, lb, wg, wu, wd, l, hosts=()):
    tm = 512

    def body(z_ref, lg_ref, lb_ref, wg_ref, wu_ref, wd_ref, o_ref):
        x1, _, _ = _ln(z_ref[...], lg_ref[...], lb_ref[...])
        xb = x1.astype(BF16)
        f = None
        for q in range(NQ):
            a = _mmt(xb, wg_ref[q])
            u = _mmt(xb, wu_ref[q])
            t = _mm(a * _sig(a) * u, wd_ref[q])
            f = t if f is None else f + t
        o_ref[...] = ALPHA * x1 + f

    return _call(f"fwd_ffn_{l}", body, (S // tm,),
                 [_row(tm, D), _const((1, D)), _const((1, D)), _wspec(wg), _wspec(wu), _wspec(wd)],
                 _row(tm, D), _sds((S, D)), hosts=hosts)(z1, lg, lb, wg, wu, wd)


def _ple_parts(z2, p, lg, lb, wg_ref, wp_ref, pg):
    x2, xhat, rstd = _ln(z2, lg, lb)
    xb = x2.astype(BF16)
    gp = _mm(xb[:, 0:256], wg_ref[0])
    for q in range(1, NQ):
        gp = gp + _mm(xb[:, q * 256:(q + 1) * 256], wg_ref[q])
    gate = _sig(gp)
    pb = p.astype(BF16)
    qp = jnp.concatenate([jnp.dot(pb, wp_ref[q], preferred_element_type=F32) for q in range(NQ)], axis=1)
    rs = lax.rsqrt(jnp.mean(qp * qp, axis=-1, keepdims=True) + LN_EPS)
    qn = qp * rs
    return x2, xhat, rstd, xb, gate, qn, rs, qn * pg


def _fwd_ple(z2, p, lg, lb, wg, wp, pg, l, hosts=()):
    tm = 512

    def body(z_ref, p_ref, lg_ref, lb_ref, wg_ref, wp_ref, pg_ref, o_ref):
        x2, _, _, _, gate, _, _, r = _ple_parts(z_ref[...], p_ref[...], lg_ref[...], lb_ref[...], wg_ref, wp_ref,
                                                pg_ref[...])
        o_ref[...] = x2 + gate * r

    return _call(f"fwd_ple_{l}", body, (S // tm,),
                 [_row(tm, D), _row(tm, 256), _const((1, D)), _const((1, D)), _wspec(wg), _wspec(wp),
                  _const((1, D))],
                 _row(tm, D), _sds((S, D)), hosts=hosts)(z2, p, lg, lb, wg, wp, pg)


def _loss_head(y, target):
    tm = 512

    def body(y_ref, t_ref, dy_ref, acc_ref):
        @pl.when(pl.program_id(0) == 0)
        def _():
            acc_ref[...] = jnp.zeros_like(acc_ref)

        e = y_ref[...] - t_ref[...]
        dy_ref[...] = e * (1.0 / D)
        acc_ref[0:1, :] += _colsum(e * e)

    return _call("loss_head", body, (S // tm,), [_row(tm, D), _row(tm, D)], [_row(tm, D), _acc(8, D)],
                 [_sds((S, D)), _sds((8, D))])(y, target)


def _zero_first(*refs):
    @pl.when(pl.program_id(0) == 0)
    def _():
        for r in refs:
            r[...] = jnp.zeros_like(r)


def _bwd_ple(g, z2, p, lg, lb, wg, wp, pg, l, hosts=()):
    tm = 256

    def body(g_ref, z_ref, p_ref, lg_ref, lb_ref, wg_ref, wp_ref, pg_ref, dz_ref, xb_ref, dgp_ref, dqp_ref, acc_ref):
        _zero_first(acc_ref)
        gin = g_ref[...]
        lgv, pgv = lg_ref[...], pg_ref[...]
        _, xhat, rstd, xb, gate, qn, rs, r = _ple_parts(z_ref[...], p_ref[...], lgv, lb_ref[...], wg_ref, wp_ref, pgv)
        xb_ref[...] = xb
        dgpb = (gin * r * gate * (1.0 - gate)).astype(BF16)
        dgp_ref[...] = dgpb
        dx2 = gin + jnp.concatenate([_mmt(dgpb, wg_ref[q]) for q in range(NQ)], axis=1)
        dr = gin * gate
        acc_ref[0:1, :] += _colsum(dr * qn)
        t = dr * pgv
        dqp_ref[...] = (rs * (t - qn * jnp.mean(t * qn, axis=-1, keepdims=True))).astype(BF16)
        acc_ref[1:2, :] += _colsum(dx2 * xhat)
        acc_ref[2:3, :] += _colsum(dx2)
        dz_ref[...] = _ln_bwd(dx2 * lgv, xhat, rstd)

    return _call(f"bwd_ple_{l}", body, (S // tm,),
                 [_row(tm, D), _row(tm, D), _row(tm, 256), _const((1, D)), _const((1, D)), _wspec(wg),
                  _wspec(wp), _const((1, D))],
                 [_row(tm, D), _row(tm, D), _row(tm, D), _row(tm, D), _acc(8, D)],
                 [_sds((S, D)), _sds((S, D), BF16), _sds((S, D), BF16), _sds((S, D), BF16), _sds((8, D))],
                 hosts=hosts)(g, z2, p, lg, lb, wg, wp, pg)


def _bwd_ffn(dz2, z1, lg, lb, wg, wu, wd, l, hosts=()):
    tm = 256

    def body(dz2_ref, z_ref, lg_ref, lb_ref, wg_ref, wu_ref, wd_ref, dz1_ref, xb_ref, hm_ref, da_ref, du_ref, acc_ref):
        _zero_first(acc_ref)
        dz2v = dz2_ref[...]
        dzb = dz2v.astype(BF16)
        lgv = lg_ref[...]
        x1, xhat, rstd = _ln(z_ref[...], lgv, lb_ref[...])
        xb = x1.astype(BF16)
        xb_ref[...] = xb
        dx1 = ALPHA * dz2v
        for q in range(NQ):
            a = _mmt(xb, wg_ref[q])
            u = _mmt(xb, wu_ref[q])
            sg = _sig(a)
            s = a * sg
            hm_ref[q] = (s * u).astype(BF16)
            dhm = _mmt(dzb, wd_ref[q])
            dub = (dhm * s).astype(BF16)
            dab = (dhm * u * _silu_grad(a, sg)).astype(BF16)
            da_ref[q] = dab
            du_ref[q] = dub
            dx1 = dx1 + _mm(dab, wg_ref[q]) + _mm(dub, wu_ref[q])
        acc_ref[0:1, :] += _colsum(dx1 * xhat)
        acc_ref[1:2, :] += _colsum(dx1)
        dz1_ref[...] = _ln_bwd(dx1 * lgv, xhat, rstd)

    return _call(f"bwd_ffn_{l}", body, (S // tm,),
                 [_row(tm, D), _row(tm, D), _const((1, D)), _const((1, D)), _wspec(wg), _wspec(wu),
                  _wspec(wd)],
                 [_row(tm, D), _row(tm, D), _grow(NQ, tm, FQ), _grow(NQ, tm, FQ), _grow(NQ, tm, FQ), _acc(8, D)],
                 [_sds((S, D)), _sds((S, D), BF16), _sds((NQ, S, FQ), BF16), _sds((NQ, S, FQ), BF16),
                  _sds((NQ, S, FQ), BF16), _sds((8, D))], hosts=hosts)(dz2, z1, lg, lb, wg, wu, wd)


def _bwd_a2(dz1, cv, lg, lb, w2, l, hosts=()):
    tm = 512

    def body(dz_ref, cv_ref, lg_ref, lb_ref, w2_ref, dcv_ref, sb_ref, acc_ref):
        _zero_first(acc_ref)
        lgv = lg_ref[...]
        n, xhat, rstd = _ln(cv_ref[...], lgv, lb_ref[...])
        sg = _sig(n)
        sb_ref[...] = (n * sg).astype(BF16)
        dzb = dz_ref[...].astype(BF16)
        ds = jnp.concatenate([_mmt(dzb, w2_ref[q]) for q in range(NQ)], axis=1)
        dn = ds * _silu_grad(n, sg)
        acc_ref[0:1, :] += _colsum(dn * xhat)
        acc_ref[1:2, :] += _colsum(dn)
        dcv = _ln_bwd(dn * lgv, xhat, rstd)
        acc_ref[2:3, :] += _colsum(dcv)
        dcv_ref[...] = dcv

    return _call(f"bwd_a2_{l}", body, (S // tm,),
                 [_row(tm, D), _row(tm, D), _const((1, D)), _const((1, D)), _wspec(w2)],
                 [_row(tm, D), _row(tm, D), _acc(8, D)],
                 [_sds((S, D)), _sds((S, D), BF16), _sds((8, D))], hosts=hosts)(dz1, cv, lg, lb, w2)


def _bwd_conv_a(dcv, glu, wdw, l, hosts=()):
    tm = 256
    nb = S // tm

    def body(d_ref, dn_ref, g_ref, gp_ref, w_ref, dglu_ref, dw_ref, bufd, bufx, sh):
        i = pl.program_id(0)
        _zero_first(dw_ref)
        bufd[0:tm, :] = d_ref[...]
        _fill_halo(bufd, tm, HALO_A, lambda: dn_ref[...], i == nb - 1)
        _fill_halo(bufx, 0, HALO_A, lambda: gp_ref[...], i == 0)
        bufx[HALO_A:HALO_A + tm, :] = g_ref[...]
        _shift_copies(bufd, sh)
        _conv_rows(dglu_ref, w_ref, None, [CONV_A - 1 - k for k in range(CONV_A)], bufd, sh, tm)
        _shift_copies(bufx, sh)
        _conv_wgrad(dw_ref, d_ref, [HALO_A - (CONV_A - 1) + k for k in range(CONV_A)], bufx, sh, tm)

    return _call(f"bwd_conv_a_{l}", body, (nb,),
                 [_row(tm, D), _next(tm, HALO_A, D), _row(tm, D), _prev(tm, HALO_A, D), _const((32, D))],
                 [_row(tm, D), _acc(32, D)], [_sds((S, D)), _sds((32, D))],
                 scratch=[pltpu.VMEM((tm + HALO_A, D), F32), pltpu.VMEM((HALO_A + tm, D), F32),
                          pltpu.VMEM((SUB - 1, HALO_A + tm - SUB, D), F32)], hosts=hosts)(dcv, dcv, glu, glu, wdw)


def _bwd_a1(dglu, h, dz1, w1, l):
    tm = 256

    def body(dg_ref, h_ref, dz_ref, w_ref, dx_ref, dh_ref, acc_ref):
        _zero_first(acc_ref)
        a, g = h_ref[:, :D], h_ref[:, D:]
        sg = _sig(g)
        dgl = dg_ref[...]
        da = dgl * sg
        dg = dgl * a * sg * (1.0 - sg)
        acc_ref[0:1, 0:D] += _colsum(da)
        acc_ref[0:1, D:2 * D] += _colsum(dg)
        dh_ref[:, 0:D] = da.astype(BF16)
        dh_ref[:, D:2 * D] = dg.astype(BF16)
        dx = ALPHA * dz_ref[...]
        for q in range(NQ):
            dx = dx + _mmt(dh_ref[:, q * 512:(q + 1) * 512], w_ref[q])
        dx_ref[...] = dx

    return _call(f"bwd_a1_{l}", body, (S // tm,),
                 [_row(tm, D), _row(tm, 2 * D), _row(tm, D), _wspec(w1)],
                 [_row(tm, D), _row(tm, 2 * D), _acc(8, 2 * D)],
                 [_sds((S, D)), _sds((S, 2 * D), BF16), _sds((8, 2 * D))])(dglu, h, dz1, w1)


def _bwd_c2(dz1, hc, wc, wout):
    tm = 256

    def body(dz_ref, hc_ref, hcp_ref, wc_ref, wout_ref, dy_ref, dbg_ref, mb_ref, buf):
        y = _short_conv(buf, hc_ref, hcp_ref, wc_ref, tm, pl.program_id(0))
        dzb = dz_ref[...].astype(BF16)
        dm = jnp.concatenate([_mmt(dzb, wout_ref[q]) for q in range(NQ)], axis=1)
        bg = hc_ref[:, :D]
        mb_ref[...] = (bg * y).astype(BF16)
        dbg_ref[...] = (dm * y).astype(BF16)
        dy_ref[...] = dm * bg

    return _call("bwd_c2", body, (S // tm,),
                 [_row(tm, D), _row(tm, 3 * D), _prev(tm, HALO_C, 3 * D), _const((8, D)), _wspec(wout)],
                 [_row(tm, D), _row(tm, D), _row(tm, D)],
                 [_sds((S, D)), _sds((S, D), BF16), _sds((S, D), BF16)],
                 scratch=[pltpu.VMEM((HALO_C + tm, D), F32)])(dz1, hc, hc, wc, wout)


def _bwd_c1(dy, hc, dbg, dz1, wc, win):
    tm = 256
    nb = S // tm

    def body(d_ref, dn_ref, hc_ref, hcp_ref, dbg_ref, dz_ref, wc_ref, win_ref, dx_ref, dhc_ref, dwc_ref, bufd, bufq):
        i = pl.program_id(0)
        _zero_first(dwc_ref)
        bufd[0:tm, :] = d_ref[...]
        _fill_halo(bufd, tm, HALO_C, lambda: dn_ref[...], i == nb - 1)
        _fill_halo(bufq, 0, HALO_C, lambda: hcp_ref[:, D:2 * D] * hcp_ref[:, 2 * D:], i == 0)
        bufq[HALO_C:HALO_C + tm, :] = hc_ref[:, D:2 * D] * hc_ref[:, 2 * D:]
        dq = wc_ref[0:1, :] * bufd[pl.ds(CONV_C - 1, tm), :]
        for k in range(1, CONV_C):
            dq = dq + wc_ref[k:k + 1, :] * bufd[pl.ds(CONV_C - 1 - k, tm), :]
        dv = d_ref[...]
        for k in range(CONV_C):
            dwc_ref[k:k + 1, :] += _colsum(dv * bufq[pl.ds(HALO_C - (CONV_C - 1) + k, tm), :])
        dhc_ref[:, 0:D] = dbg_ref[...]
        dhc_ref[:, D:2 * D] = (dq * hc_ref[:, 2 * D:]).astype(BF16)
        dhc_ref[:, 2 * D:3 * D] = (dq * hc_ref[:, D:2 * D]).astype(BF16)
        dx = ALPHA * dz_ref[...]
        for q in range(NQ):
            dx = dx + _mmt(dhc_ref[:, q * 768:(q + 1) * 768], win_ref[q])
        dx_ref[...] = dx

    return _call("bwd_c1", body, (nb,),
                 [_row(tm, D), _next(tm, HALO_C, D), _row(tm, 3 * D), _prev(tm, HALO_C, 3 * D), _row(tm, D),
                  _row(tm, D), _const((8, D)), _wspec(win)],
                 [_row(tm, D), _row(tm, 3 * D), _acc(8, D)],
                 [_sds((S, D)), _sds((S, 3 * D), BF16), _sds((8, D))],
                 scratch=[pltpu.VMEM((tm + HALO_C, D), F32), pltpu.VMEM((HALO_C + tm, D), F32)]
                 )(dy, dy, hc, hc, dbg, dz1, wc, win)


def _bwd_b(dz1, h, lg, lb, win, wout, ws, wst, bsx):
    tm = 128
    nb = S // tm

    def body(dz_ref, h_ref, lg_ref, lb_ref, win_ref, wout_ref, ws_ref, wst_ref, bsx_ref,
             dx_ref, dh_ref, mb_ref, acc_ref, dws_ref, dbs_ref, f_scr, dvn_scr):
        _zero_first(acc_ref, dws_ref, dbs_ref)
        lgv = lg_ref[...]
        hu, hv = h_ref[:, :E], h_ref[:, E:]
        u, tu = _gelu(hu)
        v, tv = _gelu(hv)
        vn, xhat, rstd = _ln(v, lgv, lb_ref[...])
        vnb = vn.astype(BF16)
        dzb = dz_ref[...].astype(BF16)
        dm = jnp.concatenate([_mmt(dzb, wout_ref[q]) for q in range(NQ)], axis=1)
        mask, mask_t = _sgu_masks()
        for hd in range(SGU_H):
            wm = jnp.where(mask, ws_ref[hd], 0.0).astype(BF16)
            cs = slice(hd * SGU_G, (hd + 1) * SGU_G)
            for n in range(tm // SGU_T):
                rs = slice(n * SGU_T, (n + 1) * SGU_T)
                f_scr[rs, cs] = jnp.dot(wm, vnb[rs, cs], preferred_element_type=F32) + bsx_ref[hd]
        f = f_scr[...]
        mb_ref[...] = (u * f).astype(BF16)
        du = dm * f
        df = dm * u
        dfb = df.astype(BF16)
        for hd in range(SGU_H):
            wmt = jnp.where(mask_t, wst_ref[hd], 0.0).astype(BF16)
            cs = slice(hd * SGU_G, (hd + 1) * SGU_G)
            for n in range(tm // SGU_T):
                rs = slice(n * SGU_T, (n + 1) * SGU_T)
                dvn_scr[rs, cs] = jnp.dot(wmt, dfb[rs, cs], preferred_element_type=F32)
                dws_ref[hd] += lax.dot_general(dfb[rs, cs], vnb[rs, cs], (((1,), (1,)), ((), ())),
                                               preferred_element_type=F32)
                dbs_ref[hd] += df[rs, cs]
        dvn = dvn_scr[...]
        acc_ref[1:2, 0:E] += _colsum(dvn * xhat)
        acc_ref[2:3, 0:E] += _colsum(dvn)
        dv = _ln_bwd(dvn * lgv, xhat, rstd)
        dhu = du * _gelu_grad(hu, tu)
        dhv = dv * _gelu_grad(hv, tv)
        acc_ref[0:1, 0:E] += _colsum(dhu)
        acc_ref[0:1, E:2 * E] += _colsum(dhv)
        dh_ref[:, 0:E] = dhu.astype(BF16)
        dh_ref[:, E:2 * E] = dhv.astype(BF16)
        dx = ALPHA * dz_ref[...]
        for q in range(NQ):
            dx = dx + _mmt(dh_ref[:, q * 1024:(q + 1) * 1024], win_ref[q])
        dx_ref[...] = dx

        @pl.when(pl.program_id(0) == nb - 1)
        def _():
            for hd in range(SGU_H):
                dws_ref[hd] = jnp.where(mask, dws_ref[hd], 0.0)

    c3 = lambda a, b, c: pl.BlockSpec((a, b, c), lambda i: (0, 0, 0))
    return _call("bwd_b", body, (nb,),
                 [_row(tm, D), _row(tm, 2 * E), _const((1, E)), _const((1, E)), _wspec(win), _wspec(wout),
                  _const((SGU_H, SGU_T, SGU_T)), _const((SGU_H, SGU_T, SGU_T)), _const((SGU_H, SGU_T, SGU_G))],
                 [_row(tm, D), _row(tm, 2 * E), _row(tm, E), _acc(8, 2 * E), c3(SGU_H, SGU_T, SGU_T),
                  c3(SGU_H, SGU_T, SGU_G)],
                 [_sds((S, D)), _sds((S, 2 * E), BF16), _sds((S, E), BF16), _sds((8, 2 * E)),
                  _sds((SGU_H, SGU_T, SGU_T)), _sds((SGU_H, SGU_T, SGU_G))],
                 scratch=[pltpu.VMEM((tm, E), F32), pltpu.VMEM((tm, E), F32)]
                 )(dz1, h, lg, lb, win, wout, ws, wst, bsx)


def _mm_tn(name, a, amode, b, bmode, k, n):
    ts = min(512, S)

    def spec(mode, w):
        if mode == "1":
            return pl.BlockSpec((ts, w), lambda s: (s, 0))
        if mode == "c":
            return pl.BlockSpec((ts, NQ * w), lambda s: (s, 0))
        return pl.BlockSpec((NQ, ts, w), lambda s: (0, s, 0))

    def pick(ref, mode, w, g):
        if mode == "1":
            return ref[...]
        if mode == "c":
            return ref[:, g * w:(g + 1) * w]
        return ref[g]

    def body(a_ref, b_ref, o_ref):
        _zero_first(o_ref)
        a_t = jnp.transpose(a_ref[...].astype(BF16)) if amode == "1" else None
        b_1 = b_ref[...].astype(BF16) if bmode == "1" else None
        for g in range(NQ):
            lhs = a_t if amode == "1" else jnp.transpose(pick(a_ref, amode, k, g).astype(BF16))
            rhs = b_1 if bmode == "1" else pick(b_ref, bmode, n, g).astype(BF16)
            o_ref[0, g] += jnp.dot(lhs, rhs, preferred_element_type=F32)

    return _call(name, body, (S // ts,), [spec(amode, k), spec(bmode, n)],
                 pl.BlockSpec((1, NQ, k, n), lambda s: (0, 0, 0, 0)), _sds((1, NQ, k, n)))(a, b)


def _row_block(k, cap=256):
    return max(t for t in range(16, min(k, cap) + 1, 16) if k % t == 0)


def _cast_bf16(w):
    nl, k, n = w.shape
    tb = _row_block(k, 512)

    def body(w_ref, o_ref):
        o_ref[...] = w_ref[...].astype(BF16)

    spec = pl.BlockSpec((None, tb, n), lambda l, i: (l, i, 0))
    return _call("cast_bf16", body, (nl, k // tb), [spec], spec, _sds(w.shape, BF16))(w)


def _adam(name, w, m, v, gc, l, prev):
    nl, k, n = w.shape
    nc = gc.shape[0]
    tb = _row_block(k)

    def body(w_ref, m_ref, v_ref, g_ref, *rest):
        go_ref, d_ref, mo_ref, vo_ref = rest[-4:]
        g = g_ref[0].astype(F32)
        for c in range(1, nc):
            g = g + g_ref[c].astype(F32)
        m2 = ADAM_B1 * m_ref[...] + (1.0 - ADAM_B1) * g
        v2 = ADAM_B2 * v_ref[...] + (1.0 - ADAM_B2) * (g * g)
        m_hat = m2 / (1.0 - ADAM_B1 ** ADAM_STEP)
        v_hat = v2 / (1.0 - ADAM_B2 ** ADAM_STEP)
        go_ref[...] = g
        d_ref[...] = -ADAM_LR * (m_hat / (jnp.sqrt(v_hat) + ADAM_EPS) + ADAM_WD * w_ref[...])
        mo_ref[...] = m2
        vo_ref[...] = v2

    spec = pl.BlockSpec((None, tb, n), lambda i: (l, i, 0))
    gspec = pl.BlockSpec((nc, None, tb, n), lambda i: (0, 0, i, 0))
    in_specs, args, aliases = [spec, spec, spec, gspec], [w, m, v, gc], {}
    if prev is not None:
        in_specs += [pl.BlockSpec(memory_space=pl.ANY)] * 4
        args += list(prev)
        aliases = {4 + j: j for j in range(4)}
    return _call(name, body, (k // tb,), in_specs, [spec] * 4, [_sds(w.shape)] * 4, aliases=aliases)(*args)


def _sum8(g8):
    r = g8.shape[1]

    def body(g_ref, o_ref):
        acc = g_ref[0]
        for d in range(1, 8):
            acc = acc + g_ref[d]
        o_ref[...] = acc

    return _call("sum8", body, (1,), [pl.BlockSpec((8, r, 128), lambda i: (0, 0, 0))],
                 pl.BlockSpec((r, 128), lambda i: (0, 0)), _sds((r, 128)))(g8)


def _place():
    x, y, c = lax.axis_index("x"), lax.axis_index("y"), lax.axis_index("c")
    return x, y, c, 2 * x + y, (x, y, 1 - c), [(1 - x, y), (x, 1 - y), (1 - x, 1 - y)]


class _Exchange:
    def __init__(self, arrays, out_shapes):
        self.arrays, self.out_shapes = list(arrays), list(out_shapes)
        n = len(self.arrays)
        self.sems = [pltpu.SemaphoreType.DMA((7 * n,)), pltpu.SemaphoreType.DMA((7 * n,)),
                     pltpu.SemaphoreType.DMA((n,))]

    def _copies(self, ins, outs, sems):
        send, recv, lsem = sems
        local_src, remote_src, dst = self.maps(ins, outs)
        x, y, c, q, sib, chips = _place()

        def rcopy(w, k, qq, cc, to, src=None):
            return pltpu.make_async_remote_copy(
                src_ref=dst(w, qq, cc) if src is None else src, dst_ref=dst(w, qq, cc),
                send_sem=send.at[7 * w + k], recv_sem=recv.at[7 * w + k], device_id=to, device_id_type=MESH)

        def mine(w):
            return pltpu.make_async_copy(local_src(w), dst(w, q, c), lsem.at[w])

        def first(w):
            return [rcopy(w, 0, q, c, sib, local_src(w))] + [
                rcopy(w, 1 + j, q, c, (cx, cy, c), remote_src(w, 2 * cx + cy)) for j, (cx, cy) in enumerate(chips)]

        return rcopy, mine, first, (x, y, c), q, c, sib, chips

    def start(self, ins, outs, sems):
        _, mine, first, *_ = self._copies(ins, outs, sems)
        for w in range(len(self.arrays)):
            mine(w).start()
            for cp in first(w):
                cp.start()

    def finish(self, ins, outs, sems):
        rcopy, mine, first, me, q, c, sib, chips = self._copies(ins, outs, sems)
        n = len(self.arrays)
        for w in range(n):
            for j, (cx, cy) in enumerate(chips):
                rcopy(w, 1 + j, 2 * cx + cy, c, me).wait_recv()
                rcopy(w, 4 + j, 2 * cx + cy, c, sib).start()
        for w in range(n):
            rcopy(w, 0, q, 1 - c, me).wait_recv()
            for j, (cx, cy) in enumerate(chips):
                rcopy(w, 4 + j, 2 * cx + cy, 1 - c, me).wait_recv()
        for w in range(n):
            for cp in first(w):
                cp.wait_send()
            for j, (cx, cy) in enumerate(chips):
                rcopy(w, 4 + j, 2 * cx + cy, c, sib).wait_send()
            mine(w).wait()


class _GatherWeights(_Exchange):
    def __init__(self, items):
        self.layers = [l for _, l in items]
        self.kh = [s.shape[1] // 2 for s, _ in items]
        super().__init__([s for s, _ in items], [_sds((NQ, 1) + s.shape[1:], BF16) for s, _ in items])

    def maps(self, ins, outs):
        c = lax.axis_index("c")
        src = lambda w: ins[w].at[pl.ds(self.layers[w], 1), pl.ds(c * self.kh[w], self.kh[w]), :]
        return src, lambda w, q: src(w), lambda w, q, cc: outs[w].at[q, :, pl.ds(cc * self.kh[w], self.kh[w]), :]


class _ScatterPartials(_Exchange):
    def __init__(self, parts):
        super().__init__(parts, [_sds((NQ, 1, 2) + p.shape[2:], BF16) for p in parts])

    def maps(self, ins, outs):
        q = 2 * lax.axis_index("x") + lax.axis_index("y")
        return (lambda w: ins[w].at[:, q]), (lambda w, qq: ins[w].at[:, qq]), (lambda w, qq, cc: outs[w].at[qq, :, cc])


class _Gather8(_Exchange):
    def __init__(self, v):
        super().__init__([v], [_sds((8,) + v.shape)])

    def maps(self, ins, outs):
        return (lambda w: ins[0]), (lambda w, q: ins[0]), (lambda w, q, cc: outs[0].at[2 * q + cc])


class _SwapHalves:
    def __init__(self, dws):
        self.arrays = list(dws)
        self.kh = [d.shape[2] // 2 for d in dws]
        self.out_shapes = [_sds(d.shape[:2] + (kh,) + d.shape[3:]) for d, kh in zip(dws, self.kh)]
        self.sems = [pltpu.SemaphoreType.DMA((len(dws),)), pltpu.SemaphoreType.DMA((len(dws),))]

    def _copies(self, ins, outs, sems):
        send, recv = sems
        _, _, c, _, sib, _ = _place()
        return [pltpu.make_async_remote_copy(
            src_ref=ins[w].at[:, :, pl.ds((1 - c) * self.kh[w], self.kh[w]), :], dst_ref=outs[w],
            send_sem=send.at[w], recv_sem=recv.at[w], device_id=sib, device_id_type=MESH)
            for w in range(len(self.arrays))]

    def start(self, ins, outs, sems):
        for cp in self._copies(ins, outs, sems):
            cp.start()

    def finish(self, ins, outs, sems):
        for cp in self._copies(ins, outs, sems):
            cp.wait()


def _comm_only(name, host):
    n_in, n_out = len(host.arrays), len(host.out_shapes)

    def body(*refs):
        ins, outs, sems = refs[:n_in], refs[n_in:n_in + n_out], refs[n_in + n_out:]
        host.start(ins, outs, sems)
        host.finish(ins, outs, sems)

    any_spec = pl.BlockSpec(memory_space=pl.ANY)
    return pl.pallas_call(body, name=name, in_specs=[any_spec] * n_in, out_specs=[any_spec] * n_out,
                          out_shape=host.out_shapes, scratch_shapes=host.sems)(*host.arrays)


def _add_halves(dw, got, cidx):
    nl, _, k, n = dw.shape
    kh = k // 2

    def body(c_ref, a_ref, b_ref, o_ref):
        o_ref[...] = (a_ref[...] + b_ref[...]).astype(BF16)

    grid_spec = pltpu.PrefetchScalarGridSpec(
        num_scalar_prefetch=1, grid=(nl, NQ),
        in_specs=[pl.BlockSpec((None, None, None, kh, n), lambda l, q, c_ref: (l, q, c_ref[0], 0, 0)),
                  pl.BlockSpec((None, None, kh, n), lambda l, q, c_ref: (l, q, 0, 0))],
        out_specs=pl.BlockSpec((None, None, kh, n), lambda l, q, c_ref: (l, q, 0, 0)))
    return pl.pallas_call(
        body, name="add_halves", grid_spec=grid_spec, out_shape=_sds((nl, NQ, kh, n), BF16),
        compiler_params=pltpu.CompilerParams(dimension_semantics=("arbitrary", "arbitrary"),
                                             vmem_limit_bytes=VMEM_LIMIT))(cidx, dw.reshape(nl, NQ, 2, kh, n), got)


def _gather8(name, v):
    return _comm_only(name, _Gather8(v))[0]


PACK = 16 * 128


def _pack(arrays):
    parts = []
    for a in arrays:
        flat = a.reshape(-1)
        parts.append(jnp.pad(flat, (0, (-flat.shape[0]) % PACK)))
    return jnp.concatenate(parts).reshape(-1, 128)


def _unpack(packed, shapes):
    flat = packed.reshape(-1)
    out, off = [], 0
    for shp in shapes:
        size = 1
        for d in shp:
            size *= d
        out.append(flat[off:off + size].reshape(shp))
        off += size + (-size) % PACK
    return out


def kernel(x, p, a_w_pw1, a_b_pw1, a_w_dw, a_b_dw, a_ln_g, a_ln_b, a_w_pw2, b_w_in, b_b_in, b_ln_g, b_ln_b, b_w_s, b_b_s, b_w_out, c_w_in, c_w_conv, c_w_out, ln1_g, ln1_b, ln2_g, ln2_b, ffn_w_gate, ffn_w_up, ffn_w_down, ple_w_gate, ple_w_proj, ple_norm_g, loss_target, m_a_w_pw1, m_a_b_pw1, m_a_w_dw, m_a_b_dw, m_a_ln_g, m_a_ln_b, m_a_w_pw2, m_b_w_in, m_b_b_in, m_b_ln_g, m_b_ln_b, m_b_w_s, m_b_b_s, m_b_w_out, m_c_w_in, m_c_w_conv, m_c_w_out, m_ln1_g, m_ln1_b, m_ln2_g, m_ln2_b, m_ffn_w_gate, m_ffn_w_up, m_ffn_w_down, m_ple_w_gate, m_ple_w_proj, m_ple_norm_g, v_a_w_pw1, v_a_b_pw1, v_a_w_dw, v_a_b_dw, v_a_ln_g, v_a_ln_b, v_a_w_pw2, v_b_w_in, v_b_b_in, v_b_ln_g, v_b_ln_b, v_b_w_s, v_b_b_s, v_b_w_out, v_c_w_in, v_c_w_conv, v_c_w_out, v_ln1_g, v_ln1_b, v_ln2_g, v_ln2_b, v_ffn_w_gate, v_ffn_w_up, v_ffn_w_down, v_ple_w_gate, v_ple_w_proj, v_ple_norm_g):
    args = dict(locals())
    wts = {k: args[k] for k in WEIGHTS}
    mom = {k: args["m_" + k] for k in WEIGHTS}
    var = {k: args["v_" + k] for k in WEIGHTS}
    for k in TRANSPOSED:
        wts[k], mom[k], var[k] = (jnp.transpose(t[k], (0, 2, 1)) for t in (wts, mom, var))
    q_idx = 2 * lax.axis_index("x") + lax.axis_index("y")
    c_idx = lax.axis_index("c").astype(jnp.int32).reshape(1)

    wb = {k: _cast_bf16(wts[k]) for k in BIG}
    mixw = [[("a_w_pw1", 0), ("a_w_pw2", 0)], [("b_w_in", 0), ("b_w_out", 0)], [("c_w_in", 0), ("c_w_out", 0)],
            [("a_w_pw1", 1), ("a_w_pw2", 1)]]
    ffnw = [[("ffn_w_gate", l), ("ffn_w_up", l), ("ffn_w_down", l)] for l in range(DEPTH)]
    plew = [[("ple_w_gate", l), ("ple_w_proj", l)] for l in range(DEPTH)]
    fwd_plan = {("a2", 0): ffnw[0], ("ffn", 0): mixw[1], ("ple", 0): plew[1],
                ("b", 1): ffnw[1], ("ffn", 1): mixw[2], ("ple", 1): plew[2],
                ("c1", 2): ffnw[2][:2], ("c2", 2): ffnw[2][2:], ("ffn", 2): mixw[3], ("ple", 2): plew[3],
                ("a2", 3): ffnw[3]}
    gw = {}

    def gather(keys):
        return _GatherWeights([(wb[name], l) for name, l in keys])

    def hosted(tag, fn, *fargs):
        keys = fwd_plan.get(tag)
        if not keys:
            return fn(*fargs)
        own, (got,) = fn(*fargs, hosts=[gather(keys)])
        gw.update(zip(keys, got))
        return own

    first_keys = mixw[0] + plew[0]
    gw.update(zip(first_keys, _comm_only("gather_first", gather(first_keys))))
    shard_shapes = [wts[k].shape for k in SMALL_SHARDED]
    small8 = _gather8("gather_small", _pack([wts[k] for k in SMALL_SHARDED]))
    per_chip = [_unpack(small8[2 * qq], shard_shapes) for qq in range(NQ)]
    full = {k: jnp.concatenate([per_chip[qq][i] for qq in range(NQ)], axis=-1) for i, k in enumerate(SMALL_SHARDED)}
    for k in SMALL_REPL:
        full[k] = wts[k]

    def vec(name, l):
        return full[name][l][None, :]

    def conv_w(name, l, rows):
        w = full[name][l]
        return jnp.pad(w, ((0, rows - w.shape[0]), (0, 0)))

    ws = full["b_w_s"][0]
    wst = jnp.transpose(ws, (0, 2, 1))
    bsx = jnp.broadcast_to(full["b_b_s"][0][:, :, None], (SGU_H, SGU_T, SGU_G))

    x0s, z1s, z2s, saved = [], [], [], []
    cur = x[0]
    for i in range(DEPTH):
        mix, j = i % 3, i // 3
        x0s.append(cur)
        if mix == 0:
            h, glu = _fwd_a1(cur, gw["a_w_pw1", j], vec("a_b_pw1", j), i)
            z1, cv = hosted(("a2", i), _fwd_a2, glu, cur, conv_w("a_w_dw", j, 32), vec("a_b_dw", j), vec("a_ln_g", j),
                            vec("a_ln_b", j), gw["a_w_pw2", j], i)
            saved.append((h, glu, cv))
        elif mix == 1:
            z1, h = hosted(("b", i), _fwd_b, cur, gw["b_w_in", 0], vec("b_b_in", 0), vec("b_ln_g", 0),
                           vec("b_ln_b", 0), ws, bsx, gw["b_w_out", 0])
            saved.append((h,))
        else:
            hc = hosted(("c1", i), _fwd_c1, cur, gw["c_w_in", 0])
            z1 = hosted(("c2", i), _fwd_c2, hc, cur, conv_w("c_w_conv", 0, 8), gw["c_w_out", 0])
            saved.append((hc,))
        z2 = hosted(("ffn", i), _fwd_ffn, z1, vec("ln1_g", i), vec("ln1_b", i), gw["ffn_w_gate", i],
                    gw["ffn_w_up", i], gw["ffn_w_down", i], i)
        cur = hosted(("ple", i), _fwd_ple, z2, p[i, 0], vec("ln2_g", i), vec("ln2_b", i), gw["ple_w_gate", i],
                     gw["ple_w_proj", i], vec("ple_norm_g", i), i)
        z1s.append(z1)
        z2s.append(z2)

    g, loss_acc = _loss_head(cur, loss_target[0])
    loss = lax.psum(0.5 / D * jnp.sum(loss_acc[0]), ("x", "y", "c"))

    dws = {}
    sg = {}
    res = {k: None for k in BIG}

    def wgrad(name, l, a, amode, b, bmode):
        _, k, n = wts[name].shape
        dws[name, l] = _mm_tn(f"dw_{name}_{l}", a, amode, b, bmode, k, n)

    def swap(keys):
        return _SwapHalves([dws[k] for k in keys])

    def add_halves(keys, got):
        return [_add_halves(dws[k], r, c_idx) for k, r in zip(keys, got)]

    def update(keys, contribs):
        for (name, l), gc in zip(keys, contribs):
            _, kq, n = wts[name].shape
            res[name] = _adam(f"adam_{name}_{l}", wts[name], mom[name], var[name], gc.reshape(NQ, 1, kq, n), l,
                              res[name])

    pending = None
    for i in reversed(range(DEPTH)):
        mix, j = i % 3, i // 3
        ple_args = (g, z2s[i], p[i, 0], vec("ln2_g", i), vec("ln2_b", i), gw["ple_w_gate", i], gw["ple_w_proj", i],
                    vec("ple_norm_g", i), i)
        if pending:
            (dz2, x2b, dgp, dqp, acc), (got,) = _bwd_ple(*ple_args, hosts=[swap(pending)])
            parts = add_halves(pending, got)
        else:
            dz2, x2b, dgp, dqp, acc = _bwd_ple(*ple_args)
        sg["ple_norm_g", i], sg["ln2_g", i], sg["ln2_b", i] = acc[0], acc[1], acc[2]
        wgrad("ple_w_gate", i, x2b, "c", dgp, "1")
        wgrad("ple_w_proj", i, p[i, 0], "1", dqp, "c")
        ffn_args = (dz2, z1s[i], vec("ln1_g", i), vec("ln1_b", i), gw["ffn_w_gate", i], gw["ffn_w_up", i],
                    gw["ffn_w_down", i], i)
        if pending:
            (dz1, x1b, hm, da, du, acc), (contribs,) = _bwd_ffn(*ffn_args, hosts=[_ScatterPartials(parts)])
            update(pending, contribs)
        else:
            dz1, x1b, hm, da, du, acc = _bwd_ffn(*ffn_args)
        sg["ln1_g", i], sg["ln1_b", i] = acc[0], acc[1]
        wgrad("ffn_w_gate", i, da, "g", x1b, "1")
        wgrad("ffn_w_up", i, du, "g", x1b, "1")
        wgrad("ffn_w_down", i, hm, "g", dz2, "1")
        x0 = x0s[i]
        if mix == 0:
            h, glu, cv = saved[i]
            a2_args = (dz1, cv, vec("a_ln_g", j), vec("a_ln_b", j), gw["a_w_pw2", j], i)
            conv_args = (glu, conv_w("a_w_dw", j, 32), i)
            if i == 0:
                early = ffnw[0] + plew[0]
                (dcv, sb, acc), (got,) = _bwd_a2(*a2_args, hosts=[swap(early)])
                parts = add_halves(early, got)
                (dglu, dwdw), (contribs,) = _bwd_conv_a(dcv, *conv_args, hosts=[_ScatterPartials(parts)])
                update(early, contribs)
            else:
                dcv, sb, acc = _bwd_a2(*a2_args)
                dglu, dwdw = _bwd_conv_a(dcv, *conv_args)
            sg["a_ln_g", j], sg["a_ln_b", j], sg["a_b_dw", j] = acc[0], acc[1], acc[2]
            wgrad("a_w_pw2", j, sb, "c", dz1, "1")
            sg["a_w_dw", j] = dwdw[:CONV_A]
            g, dh, acc = _bwd_a1(dglu, h, dz1, gw["a_w_pw1", j], i)
            sg["a_b_pw1", j] = acc[0]
            wgrad("a_w_pw1", j, x0, "1", dh, "c")
        elif mix == 1:
            (h,) = saved[i]
            g, dh, mb, acc, dw_s, db_s = _bwd_b(dz1, h, vec("b_ln_g", 0), vec("b_ln_b", 0), gw["b_w_in", 0],
                                                gw["b_w_out", 0], ws, wst, bsx)
            sg["b_b_in", 0], sg["b_ln_g", 0], sg["b_ln_b", 0] = acc[0], acc[1, :E], acc[2, :E]
            sg["b_w_s", 0], sg["b_b_s", 0] = dw_s, jnp.sum(db_s, axis=-1)
            wgrad("b_w_out", 0, mb, "c", dz1, "1")
            wgrad("b_w_in", 0, x0, "1", dh, "c")
        else:
            (hc,) = saved[i]
            wc = conv_w("c_w_conv", 0, 8)
            dy, dbg, mb = _bwd_c2(dz1, hc, wc, gw["c_w_out", 0])
            wgrad("c_w_out", 0, mb, "c", dz1, "1")
            g, dhc, dwc = _bwd_c1(dy, hc, dbg, dz1, wc, gw["c_w_in", 0])
            sg["c_w_conv", 0] = dwc[:CONV_C]
            wgrad("c_w_in", 0, x0, "1", dhc, "c")
        pending = mixw[i] + ffnw[i] + plew[i] if i > 0 else mixw[0]
    grad_x = g[None]
    parts = add_halves(pending, _comm_only("swap_last", swap(pending)))
    update(pending, _comm_only("scatter_last", _ScatterPartials(parts)))

    small = SMALL_SHARDED + SMALL_REPL
    gfull = {k: jnp.stack([sg[k, l] for l in range(full[k].shape[0])]) for k in small}
    gsum = _unpack(_sum8(_gather8("gather_small_grads", _pack([gfull[k] for k in small]))),
                   [full[k].shape for k in small])
    gmine = []
    for k, gs in zip(small, gsum):
        if k in SMALL_SHARDED:
            wdt = wts[k].shape[-1]
            gs = lax.dynamic_slice_in_dim(gs, q_idx * wdt, wdt, axis=gs.ndim - 1)
        gmine.append(gs)
    packed = [_pack(t)[None] for t in ([wts[k] for k in small], [mom[k] for k in small], [var[k] for k in small])]
    outs = _adam("adam_small", packed[0], packed[1], packed[2], _pack(gmine)[None, None], 0, None)
    unpacked = [_unpack(o[0], [wts[k].shape for k in small]) for o in outs]
    for i, k in enumerate(small):
        res[k] = tuple(u[i] for u in unpacked)

    for k in TRANSPOSED:
        res[k] = tuple(jnp.transpose(r, (0, 2, 1)) for r in res[k])
    return (loss, grad_x, *[res[k][0] for k in WEIGHTS], *[res[k][1] for k in WEIGHTS],
            *[res[k][2] for k in WEIGHTS], *[res[k][3] for k in WEIGHTS])
```

```python
import jax
import jax.numpy as jnp
from jax import lax
from jax.experimental import pallas as pl
from jax.experimental.pallas import tpu as pltpu

F32, BF16 = jnp.float32, jnp.bfloat16
S = 4096
D = 1024
E = 2048
FF = 2816
FQ = FF // 4
NQ = 4
DEPTH = 4
ALPHA = (2 * DEPTH) ** 0.25
LN_EPS = 1e-5
CONV_A, CONV_C = 31, 3
HALO_A, HALO_C = 32, 8
SGU_T, SGU_H, SGU_G, SGU_CHUNK = 128, 8, 256, 64
VMEM_LIMIT = 56 * 1024 * 1024
MESH = pl.DeviceIdType.MESH
ADAM_LR, ADAM_B1, ADAM_B2, ADAM_EPS, ADAM_WD, ADAM_STEP = 0.001, 0.9, 0.999, 1e-08, 0.01, 10
GELU_C, GELU_A = 0.7978845608028654, 0.044715

BIG = ["a_w_pw1", "a_w_pw2", "b_w_in", "b_w_out", "c_w_in", "c_w_out",
       "ffn_w_gate", "ffn_w_up", "ffn_w_down", "ple_w_gate", "ple_w_proj"]
TRANSPOSED = ["ffn_w_gate", "ffn_w_up"]
SMALL_SHARDED = ["a_b_pw1", "a_w_dw", "a_b_dw", "a_ln_g", "a_ln_b", "c_w_conv"]
SMALL_REPL = ["b_b_in", "b_ln_g", "b_ln_b", "b_w_s", "b_b_s", "ln1_g", "ln1_b", "ln2_g", "ln2_b", "ple_norm_g"]
WEIGHTS = ["a_w_pw1", "a_b_pw1", "a_w_dw", "a_b_dw", "a_ln_g", "a_ln_b", "a_w_pw2", "b_w_in", "b_b_in", "b_ln_g",
           "b_ln_b", "b_w_s", "b_b_s", "b_w_out", "c_w_in", "c_w_conv", "c_w_out", "ln1_g", "ln1_b", "ln2_g",
           "ln2_b", "ffn_w_gate", "ffn_w_up", "ffn_w_down", "ple_w_gate", "ple_w_proj", "ple_norm_g"]


def _call(name, body, grid, in_specs, out_specs, out_shape, scratch=(), aliases=None, hosts=()):
    params = pltpu.CompilerParams(dimension_semantics=("arbitrary",) * len(grid), vmem_limit_bytes=VMEM_LIMIT)
    if not hosts:
        return pl.pallas_call(
            body, name=name, grid=grid, in_specs=in_specs, out_specs=out_specs, out_shape=out_shape,
            scratch_shapes=list(scratch), input_output_aliases=aliases or {}, compiler_params=params)
    assert len(grid) == 1 and not aliases
    single = not isinstance(out_shape, (list, tuple))
    own_shapes = [out_shape] if single else list(out_shape)
    own_specs = [out_specs] if single else list(out_specs)
    n_in, n_out, n_scr = len(in_specs), len(own_shapes), len(scratch)
    h_in = [len(h.arrays) for h in hosts]
    h_out = [len(h.out_shapes) for h in hosts]
    h_sem = [len(h.sems) for h in hosts]

    def split(refs, counts):
        out, off = [], 0
        for cnt in counts:
            out.append(refs[off:off + cnt])
            off += cnt
        return out

    def wrapped(*refs):
        ins, hin, outs, hout, scr, hsem = split(refs, [n_in, sum(h_in), n_out, sum(h_out), n_scr, sum(h_sem)])
        per_host = list(zip(hosts, split(hin, h_in), split(hout, h_out), split(hsem, h_sem)))

        @pl.when(pl.program_id(0) == 0)
        def _():
            for h, a, o, s in per_host:
                h.start(a, o, s)

        body(*ins, *outs, *scr)

        @pl.when(pl.program_id(0) == grid[0] - 1)
        def _():
            for h, a, o, s in per_host:
                h.finish(a, o, s)

    any_spec = pl.BlockSpec(memory_space=pl.ANY)
    call = pl.pallas_call(
        wrapped, name=name, grid=grid, in_specs=list(in_specs) + [any_spec] * sum(h_in),
        out_specs=own_specs + [any_spec] * sum(h_out),
        out_shape=own_shapes + [s for h in hosts for s in h.out_shapes],
        scratch_shapes=list(scratch) + [s for h in hosts for s in h.sems], compiler_params=params)

    def run(*args):
        res = call(*args, *[a for h in hosts for a in h.arrays])
        own = res[0] if single else list(res[:n_out])
        return own, split(list(res[n_out:]), h_out)

    return run


def _sds(shape, dtype=F32):
    return jax.ShapeDtypeStruct(shape, dtype)


def _row(tm, c):
    return pl.BlockSpec((tm, c), lambda i: (i, 0))


def _grow(g, tm, c):
    return pl.BlockSpec((g, tm, c), lambda i: (0, i, 0))


def _const(shape):
    nd = len(shape)
    return pl.BlockSpec(shape, lambda i: (0,) * nd, pipeline_mode=pl.Buffered(1))


def _wspec(w):
    return pl.BlockSpec((NQ, None, w.shape[2], w.shape[3]), lambda i: (0, 0, 0, 0), pipeline_mode=pl.Buffered(1))


def _prev(tm, hb, c):
    return pl.BlockSpec((hb, c), lambda i: (jnp.maximum(i * (tm // hb) - 1, 0), 0))


def _next(tm, hb, c):
    return pl.BlockSpec((hb, c), lambda i: (jnp.minimum((i + 1) * (tm // hb), S // hb - 1), 0))


def _acc(r, c):
    return pl.BlockSpec((r, c), lambda i: (0, 0))


def _sig(x):
    return 1.0 / (1.0 + jnp.exp(-x))


def _ln(z, g, b):
    mu = jnp.mean(z, axis=-1, keepdims=True)
    zc = z - mu
    rstd = lax.rsqrt(jnp.mean(zc * zc, axis=-1, keepdims=True) + LN_EPS)
    xhat = zc * rstd
    return xhat * g + b, xhat, rstd


def _ln_bwd(dyg, xhat, rstd):
    return rstd * (dyg - jnp.mean(dyg, axis=-1, keepdims=True) - xhat * jnp.mean(dyg * xhat, axis=-1, keepdims=True))


def _mm(a, w):
    return jnp.dot(a.astype(BF16), w, preferred_element_type=F32)


def _mmt(a, w):
    return lax.dot_general(a.astype(BF16), w, (((1,), (1,)), ((), ())), preferred_element_type=F32)


def _colsum(x):
    return jnp.sum(x, axis=0, keepdims=True)


def _gelu(x):
    t = jnp.tanh(GELU_C * (x + GELU_A * x * x * x))
    return 0.5 * x * (1.0 + t), t


def _gelu_grad(x, t):
    return 0.5 * (1.0 + t) + 0.5 * x * (1.0 - t * t) * GELU_C * (1.0 + 3.0 * GELU_A * x * x)


def _silu_grad(a, sg):
    return sg * (1.0 + a * (1.0 - sg))


def _sgu_masks():
    r = lax.broadcasted_iota(jnp.int32, (SGU_T, SGU_T), 0) // SGU_CHUNK
    c = lax.broadcasted_iota(jnp.int32, (SGU_T, SGU_T), 1) // SGU_CHUNK
    return r >= c, c >= r


def _fill_halo(buf, lo, n, halo_val_fn, is_edge):
    @pl.when(is_edge)
    def _():
        buf[lo:lo + n, :] = jnp.zeros((n, buf.shape[1]), F32)

    @pl.when(jnp.logical_not(is_edge))
    def _():
        buf[lo:lo + n, :] = halo_val_fn()


SUB, LANE = 8, 128
ROWS_AT_ONCE = 16


def _shift_copies(buf, sh):
    rows = sh.shape[1]
    for s in range(1, SUB):
        sh[s - 1, :, :] = buf[pl.ds(s, rows), :]


def _tiles(buf, sh, s, first, count, group0, lanes):
    src = buf if s == 0 else sh.at[s - 1]
    return {t: src[pl.ds(pl.multiple_of((group0 + t) * SUB, SUB), SUB), lanes] for t in range(first, first + count)}


def _by_shift(offsets):
    out = []
    for s in range(SUB):
        taps = [(k, o // SUB) for k, o in enumerate(offsets) if o % SUB == s]
        if taps:
            out.append((s, taps))
    return out


def _conv_rows(out_ref, w_ref, bias_ref, offsets, buf, sh, tm):
    n = ROWS_AT_ONCE
    for cb in range(D // LANE):
        lanes = slice(cb * LANE, (cb + 1) * LANE)
        bias = None if bias_ref is None else jnp.broadcast_to(bias_ref[:, lanes], (SUB, LANE))

        def body(jb, carry):
            accs = [bias] * n
            for s, taps in _by_shift(offsets):
                ms = [m for _, m in taps]
                tiles = _tiles(buf, sh, s, min(ms), max(ms) - min(ms) + n, jb * n, lanes)
                for k, m in taps:
                    wk = jnp.broadcast_to(w_ref[k:k + 1, lanes], (SUB, LANE))
                    for jj in range(n):
                        t = wk * tiles[m + jj]
                        accs[jj] = t if accs[jj] is None else accs[jj] + t
            for jj in range(n):
                out_ref[pl.ds(pl.multiple_of((jb * n + jj) * SUB, SUB), SUB), lanes] = accs[jj]
            return carry

        lax.fori_loop(0, tm // (SUB * n), body, 0)


def _conv_wgrad(dw_ref, d_ref, offsets, buf, sh, tm):
    n = 4
    for cb in range(D // LANE):
        lanes = slice(cb * LANE, (cb + 1) * LANE)

        def body(jq, accs):
            accs = list(accs)
            d = [d_ref[pl.ds(pl.multiple_of((jq * n + jj) * SUB, SUB), SUB), lanes] for jj in range(n)]
            for s, taps in _by_shift(offsets):
                ms = [m for _, m in taps]
                tiles = _tiles(buf, sh, s, min(ms), max(ms) - min(ms) + n, jq * n, lanes)
                for k, m in taps:
                    for jj in range(n):
                        accs[k] = accs[k] + d[jj] * tiles[m + jj]
            return tuple(accs)

        accs = lax.fori_loop(0, tm // (SUB * n), body, tuple(jnp.zeros((SUB, LANE), F32) for _ in offsets))
        for k, acc in enumerate(accs):
            dw_ref[k:k + 1, lanes] += jnp.sum(acc, axis=0, keepdims=True)


def _fwd_a1(x0, w1, b1, l, hosts=()):
    tm = 512

    def body(x_ref, w_ref, b_ref, h_ref, glu_ref):
        xb = x_ref[...].astype(BF16)
        for q in range(NQ):
            sl = slice(q * 512, (q + 1) * 512)
            h_ref[:, sl] = jnp.dot(xb, w_ref[q], preferred_element_type=F32) + b_ref[:, sl]
        glu_ref[...] = h_ref[:, :D] * _sig(h_ref[:, D:])

    return _call(f"fwd_a1_{l}", body, (S // tm,), [_row(tm, D), _wspec(w1), _const((1, 2 * D))],
                 [_row(tm, 2 * D), _row(tm, D)], [_sds((S, 2 * D)), _sds((S, D))], hosts=hosts)(x0, w1, b1)


def _fwd_a2(glu, x0, wdw, bdw, lg, lb, w2, l, hosts=()):
    tm = 256

    def body(g_ref, gp_ref, x_ref, wdw_ref, bdw_ref, lg_ref, lb_ref, w2_ref, z_ref, cv_ref, buf, sh):
        i = pl.program_id(0)
        _fill_halo(buf, 0, HALO_A, lambda: gp_ref[...], i == 0)
        buf[HALO_A:HALO_A + tm, :] = g_ref[...]
        _shift_copies(buf, sh)
        _conv_rows(cv_ref, wdw_ref, bdw_ref, [HALO_A - (CONV_A - 1) + k for k in range(CONV_A)], buf, sh, tm)
        n, _, _ = _ln(cv_ref[...], lg_ref[...], lb_ref[...])
        sb = (n * _sig(n)).astype(BF16)
        hm = _mm(sb[:, 0:256], w2_ref[0])
        for q in range(1, NQ):
            hm = hm + _mm(sb[:, q * 256:(q + 1) * 256], w2_ref[q])
        z_ref[...] = ALPHA * x_ref[...] + hm

    return _call(f"fwd_a2_{l}", body, (S // tm,),
                 [_row(tm, D), _prev(tm, HALO_A, D), _row(tm, D), _const((32, D)), _const((1, D)), _const((1, D)),
                  _const((1, D)), _wspec(w2)],
                 [_row(tm, D), _row(tm, D)], [_sds((S, D)), _sds((S, D))],
                 scratch=[pltpu.VMEM((HALO_A + tm, D), F32), pltpu.VMEM((SUB - 1, HALO_A + tm - SUB, D), F32)],
                 hosts=hosts)(glu, glu, x0, wdw, bdw, lg, lb, w2)


def _fwd_b(x0, win, b_in, lg, lb, ws, bsx, wout, hosts=()):
    tm = 256

    def body(x_ref, win_ref, bin_ref, lg_ref, lb_ref, ws_ref, bsx_ref, wout_ref, z_ref, h_ref, f_scr):
        xb = x_ref[...].astype(BF16)
        for q in range(NQ):
            sl = slice(q * 1024, (q + 1) * 1024)
            h_ref[:, sl] = jnp.dot(xb, win_ref[q], preferred_element_type=F32) + bin_ref[:, sl]
        u, _ = _gelu(h_ref[:, :E])
        v, _ = _gelu(h_ref[:, E:])
        vn, _, _ = _ln(v, lg_ref[...], lb_ref[...])
        vnb = vn.astype(BF16)
        mask, _ = _sgu_masks()
        for hd in range(SGU_H):
            wm = jnp.where(mask, ws_ref[hd], 0.0).astype(BF16)
            cs = slice(hd * SGU_G, (hd + 1) * SGU_G)
            for n in range(tm // SGU_T):
                rs = slice(n * SGU_T, (n + 1) * SGU_T)
                f_scr[rs, cs] = jnp.dot(wm, vnb[rs, cs], preferred_element_type=F32) + bsx_ref[hd]
        mb = (u * f_scr[...]).astype(BF16)
        out = _mm(mb[:, 0:512], wout_ref[0])
        for q in range(1, NQ):
            out = out + _mm(mb[:, q * 512:(q + 1) * 512], wout_ref[q])
        z_ref[...] = ALPHA * x_ref[...] + out

    return _call("fwd_b", body, (S // tm,),
                 [_row(tm, D), _wspec(win), _const((1, 2 * E)), _const((1, E)), _const((1, E)),
                  _const((SGU_H, SGU_T, SGU_T)), _const((SGU_H, SGU_T, SGU_G)), _wspec(wout)],
                 [_row(tm, D), _row(tm, 2 * E)], [_sds((S, D)), _sds((S, 2 * E))],
                 scratch=[pltpu.VMEM((tm, E), F32)], hosts=hosts)(x0, win, b_in, lg, lb, ws, bsx, wout)


def _fwd_c1(x0, win, hosts=()):
    tm = 512

    def body(x_ref, w_ref, hc_ref):
        xb = x_ref[...].astype(BF16)
        for q in range(NQ):
            hc_ref[:, q * 768:(q + 1) * 768] = jnp.dot(xb, w_ref[q], preferred_element_type=F32)

    return _call("fwd_c1", body, (S // tm,), [_row(tm, D), _wspec(win)], _row(tm, 3 * D),
                 _sds((S, 3 * D)), hosts=hosts)(x0, win)


def _short_conv(buf, hc_ref, hcp_ref, wc_ref, tm, i):
    _fill_halo(buf, 0, HALO_C, lambda: hcp_ref[:, D:2 * D] * hcp_ref[:, 2 * D:], i == 0)
    buf[HALO_C:HALO_C + tm, :] = hc_ref[:, D:2 * D] * hc_ref[:, 2 * D:]
    y = wc_ref[0:1, :] * buf[pl.ds(HALO_C - 2, tm), :]
    for k in range(1, CONV_C):
        y = y + wc_ref[k:k + 1, :] * buf[pl.ds(HALO_C - 2 + k, tm), :]
    return y


def _fwd_c2(hc, x0, wc, wout, hosts=()):
    tm = 256

    def body(hc_ref, hcp_ref, x_ref, wc_ref, wout_ref, z_ref, buf):
        y = _short_conv(buf, hc_ref, hcp_ref, wc_ref, tm, pl.program_id(0))
        mb = (hc_ref[:, :D] * y).astype(BF16)
        out = _mm(mb[:, 0:256], wout_ref[0])
        for q in range(1, NQ):
            out = out + _mm(mb[:, q * 256:(q + 1) * 256], wout_ref[q])
        z_ref[...] = ALPHA * x_ref[...] + out

    return _call("fwd_c2", body, (S // tm,),
                 [_row(tm, 3 * D), _prev(tm, HALO_C, 3 * D), _row(tm, D), _const((8, D)), _wspec(wout)],
                 _row(tm, D), _sds((S, D)), scratch=[pltpu.VMEM((HALO_C + tm, D), F32)], hosts=hosts
                 )(hc, hc, x0, wc, wout)


def _fwd_ffn(z1, lg, lb, wg, wu, wd, l, hosts=()):
    tm = 512

    def body(z_ref, lg_ref, lb_ref, wg_ref, wu_ref, wd_ref, o_ref):
        x1, _, _ = _ln(z_ref[...], lg_ref[...], lb_ref[...])
        xb = x1.astype(BF16)
        f = None
        for q in range(NQ):
            a = _mmt(xb, wg_ref[q])
            u = _mmt(xb, wu_ref[q])
            t = _mm(a * _sig(a) * u, wd_ref[q])
            f = t if f is None else f + t
        o_ref[...] = ALPHA * x1 + f

    return _call(f"fwd_ffn_{l}", body, (S // tm,),
                 [_row(tm, D), _const((1, D)), _const((1, D)), _wspec(wg), _wspec(wu), _wspec(wd)],
                 _row(tm, D), _sds((S, D)), hosts=hosts)(z1, lg, lb, wg, wu, wd)


def _ple_parts(z2, p, lg, lb, wg_ref, wp_ref, pg):
    x2, xhat, rstd = _ln(z2, lg, lb)
    xb = x2.astype(BF16)
    gp = _mm(xb[:, 0:256], wg_ref[0])
    for q in range(1, NQ):
        gp = gp + _mm(xb[:, q * 256:(q + 1) * 256], wg_ref[q])
    gate = _sig(gp)
    pb = p.astype(BF16)
    qp = jnp.concatenate([jnp.dot(pb, wp_ref[q], preferred_element_type=F32) for q in range(NQ)], axis=1)
    rs = lax.rsqrt(jnp.mean(qp * qp, axis=-1, keepdims=True) + LN_EPS)
    qn = qp * rs
    return x2, xhat, rstd, xb, gate, qn, rs, qn * pg


def _fwd_ple(z2, p, lg, lb, wg, wp, pg, l, hosts=()):
    tm = 512

    def body(z_ref, p_ref, lg_ref, lb_ref, wg_ref, wp_ref, pg_ref, o_ref):
        x2, _, _, _, gate, _, _, r = _ple_parts(z_ref[...], p_ref[...], lg_ref[...], lb_ref[...], wg_ref, wp_ref,
                                                pg_ref[...])
        o_ref[...] = x2 + gate * r

    return _call(f"fwd_ple_{l}", body, (S // tm,),
                 [_row(tm, D), _row(tm, 256), _const((1, D)), _const((1, D)), _wspec(wg), _wspec(wp),
                  _const((1, D))],
                 _row(tm, D), _sds((S, D)), hosts=hosts)(z2, p, lg, lb, wg, wp, pg)


def _loss_head(y, target):
    tm = 512

    def body(y_ref, t_ref, dy_ref, acc_ref):
        @pl.when(pl.program_id(0) == 0)
        def _():
            acc_ref[...] = jnp.zeros_like(acc_ref)

        e = y_ref[...] - t_ref[...]
        dy_ref[...] = e * (1.0 / D)
        acc_ref[0:1, :] += _colsum(e * e)

    return _call("loss_head", body, (S // tm,), [_row(tm, D), _row(tm, D)], [_row(tm, D), _acc(8, D)],
                 [_sds((S, D)), _sds((8, D))])(y, target)


def _zero_first(*refs):
    @pl.when(pl.program_id(0) == 0)
    def _():
        for r in refs:
            r[...] = jnp.zeros_like(r)


def _bwd_ple(g, z2, p, lg, lb, wg, wp, pg, l, hosts=()):
    tm = 256

    def body(g_ref, z_ref, p_ref, lg_ref, lb_ref, wg_ref, wp_ref, pg_ref, dz_ref, xb_ref, dgp_ref, dqp_ref, acc_ref):
        _zero_first(acc_ref)
        gin = g_ref[...]
        lgv, pgv = lg_ref[...], pg_ref[...]
        _, xhat, rstd, xb, gate, qn, rs, r = _ple_parts(z_ref[...], p_ref[...], lgv, lb_ref[...], wg_ref, wp_ref, pgv)
        xb_ref[...] = xb
        dgpb = (gin * r * gate * (1.0 - gate)).astype(BF16)
        dgp_ref[...] = dgpb
        dx2 = gin + jnp.concatenate([_mmt(dgpb, wg_ref[q]) for q in range(NQ)], axis=1)
        dr = gin * gate
        acc_ref[0:1, :] += _colsum(dr * qn)
        t = dr * pgv
        dqp_ref[...] = (rs * (t - qn * jnp.mean(t * qn, axis=-1, keepdims=True))).astype(BF16)
        acc_ref[1:2, :] += _colsum(dx2 * xhat)
        acc_ref[2:3, :] += _colsum(dx2)
        dz_ref[...] = _ln_bwd(dx2 * lgv, xhat, rstd)

    return _call(f"bwd_ple_{l}", body, (S // tm,),
                 [_row(tm, D), _row(tm, D), _row(tm, 256), _const((1, D)), _const((1, D)), _wspec(wg),
                  _wspec(wp), _const((1, D))],
                 [_row(tm, D), _row(tm, D), _row(tm, D), _row(tm, D), _acc(8, D)],
                 [_sds((S, D)), _sds((S, D), BF16), _sds((S, D), BF16), _sds((S, D), BF16), _sds((8, D))],
                 hosts=hosts)(g, z2, p, lg, lb, wg, wp, pg)


def _bwd_ffn(dz2, z1, lg, lb, wg, wu, wd, l, hosts=()):
    tm = 256

    def body(dz2_ref, z_ref, lg_ref, lb_ref, wg_ref, wu_ref, wd_ref, dz1_ref, xb_ref, hm_ref, da_ref, du_ref, acc_ref):
        _zero_first(acc_ref)
        dz2v = dz2_ref[...]
        dzb = dz2v.astype(BF16)
        lgv = lg_ref[...]
        x1, xhat, rstd = _ln(z_ref[...], lgv, lb_ref[...])
        xb = x1.astype(BF16)
        xb_ref[...] = xb
        dx1 = ALPHA * dz2v
        for q in range(NQ):
            a = _mmt(xb, wg_ref[q])
            u = _mmt(xb, wu_ref[q])
            sg = _sig(a)
            s = a * sg
            hm_ref[q] = (s * u).astype(BF16)
            dhm = _mmt(dzb, wd_ref[q])
            dub = (dhm * s).astype(BF16)
            dab = (dhm * u * _silu_grad(a, sg)).astype(BF16)
            da_ref[q] = dab
            du_ref[q] = dub
            dx1 = dx1 + _mm(dab, wg_ref[q]) + _mm(dub, wu_ref[q])
        acc_ref[0:1, :] += _colsum(dx1 * xhat)
        acc_ref[1:2, :] += _colsum(dx1)
        dz1_ref[...] = _ln_bwd(dx1 * lgv, xhat, rstd)

    return _call(f"bwd_ffn_{l}", body, (S // tm,),
                 [_row(tm, D), _row(tm, D), _const((1, D)), _const((1, D)), _wspec(wg), _wspec(wu),
                  _wspec(wd)],
                 [_row(tm, D), _row(tm, D), _grow(NQ, tm, FQ), _grow(NQ, tm, FQ), _grow(NQ, tm, FQ), _acc(8, D)],
                 [_sds((S, D)), _sds((S, D), BF16), _sds((NQ, S, FQ), BF16), _sds((NQ, S, FQ), BF16),
                  _sds((NQ, S, FQ), BF16), _sds((8, D))], hosts=hosts)(dz2, z1, lg, lb, wg, wu, wd)


def _bwd_a2(dz1, cv, lg, lb, w2, l, hosts=()):
    tm = 512

    def body(dz_ref, cv_ref, lg_ref, lb_ref, w2_ref, dcv_ref, sb_ref, acc_ref):
        _zero_first(acc_ref)
        lgv = lg_ref[...]
        n, xhat, rstd = _ln(cv_ref[...], lgv, lb_ref[...])
        sg = _sig(n)
        sb_ref[...] = (n * sg).astype(BF16)
        dzb = dz_ref[...].astype(BF16)
        ds = jnp.concatenate([_mmt(dzb, w2_ref[q]) for q in range(NQ)], axis=1)
        dn = ds * _silu_grad(n, sg)
        acc_ref[0:1, :] += _colsum(dn * xhat)
        acc_ref[1:2, :] += _colsum(dn)
        dcv = _ln_bwd(dn * lgv, xhat, rstd)
        acc_ref[2:3, :] += _colsum(dcv)
        dcv_ref[...] = dcv

    return _call(f"bwd_a2_{l}", body, (S // tm,),
                 [_row(tm, D), _row(tm, D), _const((1, D)), _const((1, D)), _wspec(w2)],
                 [_row(tm, D), _row(tm, D), _acc(8, D)],
                 [_sds((S, D)), _sds((S, D), BF16), _sds((8, D))], hosts=hosts)(dz1, cv, lg, lb, w2)


def _bwd_conv_a(dcv, glu, wdw, l, hosts=()):
    tm = 256
    nb = S // tm

    def body(d_ref, dn_ref, g_ref, gp_ref, w_ref, dglu_ref, dw_ref, bufd, bufx, sh):
        i = pl.program_id(0)
        _zero_first(dw_ref)
        bufd[0:tm, :] = d_ref[...]
        _fill_halo(bufd, tm, HALO_A, lambda: dn_ref[...], i == nb - 1)
        _fill_halo(bufx, 0, HALO_A, lambda: gp_ref[...], i == 0)
        bufx[HALO_A:HALO_A + tm, :] = g_ref[...]
        _shift_copies(bufd, sh)
        _conv_rows(dglu_ref, w_ref, None, [CONV_A - 1 - k for k in range(CONV_A)], bufd, sh, tm)
        _shift_copies(bufx, sh)
        _conv_wgrad(dw_ref, d_ref, [HALO_A - (CONV_A - 1) + k for k in range(CONV_A)], bufx, sh, tm)

    return _call(f"bwd_conv_a_{l}", body, (nb,),
                 [_row(tm, D), _next(tm, HALO_A, D), _row(tm, D), _prev(tm, HALO_A, D), _const((32, D))],
                 [_row(tm, D), _acc(32, D)], [_sds((S, D)), _sds((32, D))],
                 scratch=[pltpu.VMEM((tm + HALO_A, D), F32), pltpu.VMEM((HALO_A + tm, D), F32),
                          pltpu.VMEM((SUB - 1, HALO_A + tm - SUB, D), F32)], hosts=hosts)(dcv, dcv, glu, glu, wdw)


def _bwd_a1(dglu, h, dz1, w1, l):
    tm = 256

    def body(dg_ref, h_ref, dz_ref, w_ref, dx_ref, dh_ref, acc_ref):
        _zero_first(acc_ref)
        a, g = h_ref[:, :D], h_ref[:, D:]
        sg = _sig(g)
        dgl = dg_ref[...]
        da = dgl * sg
        dg = dgl * a * sg * (1.0 - sg)
        acc_ref[0:1, 0:D] += _colsum(da)
        acc_ref[0:1, D:2 * D] += _colsum(dg)
        dh_ref[:, 0:D] = da.astype(BF16)
        dh_ref[:, D:2 * D] = dg.astype(BF16)
        dx = ALPHA * dz_ref[...]
        for q in range(NQ):
            dx = dx + _mmt(dh_ref[:, q * 512:(q + 1) * 512], w_ref[q])
        dx_ref[...] = dx

    return _call(f"bwd_a1_{l}", body, (S // tm,),
                 [_row(tm, D), _row(tm, 2 * D), _row(tm, D), _wspec(w1)],
                 [_row(tm, D), _row(tm, 2 * D), _acc(8, 2 * D)],
                 [_sds((S, D)), _sds((S, 2 * D), BF16), _sds((8, 2 * D))])(dglu, h, dz1, w1)


def _bwd_c2(dz1, hc, wc, wout):
    tm = 256

    def body(dz_ref, hc_ref, hcp_ref, wc_ref, wout_ref, dy_ref, dbg_ref, mb_ref, buf):
        y = _short_conv(buf, hc_ref, hcp_ref, wc_ref, tm, pl.program_id(0))
        dzb = dz_ref[...].astype(BF16)
        dm = jnp.concatenate([_mmt(dzb, wout_ref[q]) for q in range(NQ)], axis=1)
        bg = hc_ref[:, :D]
        mb_ref[...] = (bg * y).astype(BF16)
        dbg_ref[...] = (dm * y).astype(BF16)
        dy_ref[...] = dm * bg

    return _call("bwd_c2", body, (S // tm,),
                 [_row(tm, D), _row(tm, 3 * D), _prev(tm, HALO_C, 3 * D), _const((8, D)), _wspec(wout)],
                 [_row(tm, D), _row(tm, D), _row(tm, D)],
                 [_sds((S, D)), _sds((S, D), BF16), _sds((S, D), BF16)],
                 scratch=[pltpu.VMEM((HALO_C + tm, D), F32)])(dz1, hc, hc, wc, wout)


def _bwd_c1(dy, hc, dbg, dz1, wc, win):
    tm = 256
    nb = S // tm

    def body(d_ref, dn_ref, hc_ref, hcp_ref, dbg_ref, dz_ref, wc_ref, win_ref, dx_ref, dhc_ref, dwc_ref, bufd, bufq):
        i = pl.program_id(0)
        _zero_first(dwc_ref)
        bufd[0:tm, :] = d_ref[...]
        _fill_halo(bufd, tm, HALO_C, lambda: dn_ref[...], i == nb - 1)
        _fill_halo(bufq, 0, HALO_C, lambda: hcp_ref[:, D:2 * D] * hcp_ref[:, 2 * D:], i == 0)
        bufq[HALO_C:HALO_C + tm, :] = hc_ref[:, D:2 * D] * hc_ref[:, 2 * D:]
        dq = wc_ref[0:1, :] * bufd[pl.ds(CONV_C - 1, tm), :]
        for k in range(1, CONV_C):
            dq = dq + wc_ref[k:k + 1, :] * bufd[pl.ds(CONV_C - 1 - k, tm), :]
        dv = d_ref[...]
        for k in range(CONV_C):
            dwc_ref[k:k + 1, :] += _colsum(dv * bufq[pl.ds(HALO_C - (CONV_C - 1) + k, tm), :])
        dhc_ref[:, 0:D] = dbg_ref[...]
        dhc_ref[:, D:2 * D] = (dq * hc_ref[:, 2 * D:]).astype(BF16)
        dhc_ref[:, 2 * D:3 * D] = (dq * hc_ref[:, D:2 * D]).astype(BF16)
        dx = ALPHA * dz_ref[...]
        for q in range(NQ):
            dx = dx + _mmt(dhc_ref[:, q * 768:(q + 1) * 768], win_ref[q])
        dx_ref[...] = dx

    return _call("bwd_c1", body, (nb,),
                 [_row(tm, D), _next(tm, HALO_C, D), _row(tm, 3 * D), _prev(tm, HALO_C, 3 * D), _row(tm, D),
                  _row(tm, D), _const((8, D)), _wspec(win)],
                 [_row(tm, D), _row(tm, 3 * D), _acc(8, D)],
                 [_sds((S, D)), _sds((S, 3 * D), BF16), _sds((8, D))],
                 scratch=[pltpu.VMEM((tm + HALO_C, D), F32), pltpu.VMEM((HALO_C + tm, D), F32)]
                 )(dy, dy, hc, hc, dbg, dz1, wc, win)


def _bwd_b(dz1, h, lg, lb, win, wout, ws, wst, bsx):
    tm = 128
    nb = S // tm

    def body(dz_ref, h_ref, lg_ref, lb_ref, win_ref, wout_ref, ws_ref, wst_ref, bsx_ref,
             dx_ref, dh_ref, mb_ref, acc_ref, dws_ref, dbs_ref, f_scr, dvn_scr):
        _zero_first(acc_ref, dws_ref, dbs_ref)
        lgv = lg_ref[...]
        hu, hv = h_ref[:, :E], h_ref[:, E:]
        u, tu = _gelu(hu)
        v, tv = _gelu(hv)
        vn, xhat, rstd = _ln(v, lgv, lb_ref[...])
        vnb = vn.astype(BF16)
        dzb = dz_ref[...].astype(BF16)
        dm = jnp.concatenate([_mmt(dzb, wout_ref[q]) for q in range(NQ)], axis=1)
        mask, mask_t = _sgu_masks()
        for hd in range(SGU_H):
            wm = jnp.where(mask, ws_ref[hd], 0.0).astype(BF16)
            cs = slice(hd * SGU_G, (hd + 1) * SGU_G)
            for n in range(tm // SGU_T):
                rs = slice(n * SGU_T, (n + 1) * SGU_T)
                f_scr[rs, cs] = jnp.dot(wm, vnb[rs, cs], preferred_element_type=F32) + bsx_ref[hd]
        f = f_scr[...]
        mb_ref[...] = (u * f).astype(BF16)
        du = dm * f
        df = dm * u
        dfb = df.astype(BF16)
        for hd in range(SGU_H):
            wmt = jnp.where(mask_t, wst_ref[hd], 0.0).astype(BF16)
            cs = slice(hd * SGU_G, (hd + 1) * SGU_G)
            for n in range(tm // SGU_T):
                rs = slice(n * SGU_T, (n + 1) * SGU_T)
                dvn_scr[rs, cs] = jnp.dot(wmt, dfb[rs, cs], preferred_element_type=F32)
                dws_ref[hd] += lax.dot_general(dfb[rs, cs], vnb[rs, cs], (((1,), (1,)), ((), ())),
                                               preferred_element_type=F32)
                dbs_ref[hd] += df[rs, cs]
        dvn = dvn_scr[...]
        acc_ref[1:2, 0:E] += _colsum(dvn * xhat)
        acc_ref[2:3, 0:E] += _colsum(dvn)
        dv = _ln_bwd(dvn * lgv, xhat, rstd)
        dhu = du * _gelu_grad(hu, tu)
        dhv = dv * _gelu_grad(hv, tv)
        acc_ref[0:1, 0:E] += _colsum(dhu)
        acc_ref[0:1, E:2 * E] += _colsum(dhv)
        dh_ref[:, 0:E] = dhu.astype(BF16)
        dh_ref[:, E:2 * E] = dhv.astype(BF16)
        dx = ALPHA * dz_ref[...]
        for q in range(NQ):
            dx = dx + _mmt(dh_ref[:, q * 1024:(q + 1) * 1024], win_ref[q])
        dx_ref[...] = dx

        @pl.when(pl.program_id(0) == nb - 1)
        def _():
            for hd in range(SGU_H):
                dws_ref[hd] = jnp.where(mask, dws_ref[hd], 0.0)

    c3 = lambda a, b, c: pl.BlockSpec((a, b, c), lambda i: (0, 0, 0))
    return _call("bwd_b", body, (nb,),
                 [_row(tm, D), _row(tm, 2 * E), _const((1, E)), _const((1, E)), _wspec(win), _wspec(wout),
                  _const((SGU_H, SGU_T, SGU_T)), _const((SGU_H, SGU_T, SGU_T)), _const((SGU_H, SGU_T, SGU_G))],
                 [_row(tm, D), _row(tm, 2 * E), _row(tm, E), _acc(8, 2 * E), c3(SGU_H, SGU_T, SGU_T),
                  c3(SGU_H, SGU_T, SGU_G)],
                 [_sds((S, D)), _sds((S, 2 * E), BF16), _sds((S, E), BF16), _sds((8, 2 * E)),
                  _sds((SGU_H, SGU_T, SGU_T)), _sds((SGU_H, SGU_T, SGU_G))],
                 scratch=[pltpu.VMEM((tm, E), F32), pltpu.VMEM((tm, E), F32)]
                 )(dz1, h, lg, lb, win, wout, ws, wst, bsx)


def _mm_tn(name, a, amode, b, bmode, k, n):
    ts = min(512, S)

    def spec(mode, w):
        if mode == "1":
            return pl.BlockSpec((ts, w), lambda s: (s, 0))
        if mode == "c":
            return pl.BlockSpec((ts, NQ * w), lambda s: (s, 0))
        return pl.BlockSpec((NQ, ts, w), lambda s: (0, s, 0))

    def pick(ref, mode, w, g):
        if mode == "1":
            return ref[...]
        if mode == "c":
            return ref[:, g * w:(g + 1) * w]
        return ref[g]

    def body(a_ref, b_ref, o_ref):
        _zero_first(o_ref)
        a_t = jnp.transpose(a_ref[...].astype(BF16)) if amode == "1" else None
        b_1 = b_ref[...].astype(BF16) if bmode == "1" else None
        for g in range(NQ):
            lhs = a_t if amode == "1" else jnp.transpose(pick(a_ref, amode, k, g).astype(BF16))
            rhs = b_1 if bmode == "1" else pick(b_ref, bmode, n, g).astype(BF16)
            o_ref[0, g] += jnp.dot(lhs, rhs, preferred_element_type=F32)

    return _call(name, body, (S // ts,), [spec(amode, k), spec(bmode, n)],
                 pl.BlockSpec((1, NQ, k, n), lambda s: (0, 0, 0, 0)), _sds((1, NQ, k, n)))(a, b)


def _row_block(k, cap=256):
    return max(t for t in range(16, min(k, cap) + 1, 16) if k % t == 0)


def _cast_bf16(w):
    nl, k, n = w.shape
    tb = _row_block(k, 512)

    def body(w_ref, o_ref):
        o_ref[...] = w_ref[...].astype(BF16)

    spec = pl.BlockSpec((None, tb, n), lambda l, i: (l, i, 0))
    return _call("cast_bf16", body, (nl, k // tb), [spec], spec, _sds(w.shape, BF16))(w)


def _adam(name, w, m, v, gc, l, prev):
    nl, k, n = w.shape
    nc = gc.shape[0]
    tb = _row_block(k)

    def body(w_ref, m_ref, v_ref, g_ref, *rest):
        go_ref, d_ref, mo_ref, vo_ref = rest[-4:]
        g = g_ref[0].astype(F32)
        for c in range(1, nc):
            g = g + g_ref[c].astype(F32)
        m2 = ADAM_B1 * m_ref[...] + (1.0 - ADAM_B1) * g
        v2 = ADAM_B2 * v_ref[...] + (1.0 - ADAM_B2) * (g * g)
        m_hat = m2 / (1.0 - ADAM_B1 ** ADAM_STEP)
        v_hat = v2 / (1.0 - ADAM_B2 ** ADAM_STEP)
        go_ref[...] = g
        d_ref[...] = -ADAM_LR * (m_hat / (jnp.sqrt(v_hat) + ADAM_EPS) + ADAM_WD * w_ref[...])
        mo_ref[...] = m2
        vo_ref[...] = v2

    spec = pl.BlockSpec((None, tb, n), lambda i: (l, i, 0))
    gspec = pl.BlockSpec((nc, None, tb, n), lambda i: (0, 0, i, 0))
    in_specs, args, aliases = [spec, spec, spec, gspec], [w, m, v, gc], {}
    if prev is not None:
        in_specs += [pl.BlockSpec(memory_space=pl.ANY)] * 4
        args += list(prev)
        aliases = {4 + j: j for j in range(4)}
    return _call(name, body, (k // tb,), in_specs, [spec] * 4, [_sds(w.shape)] * 4, aliases=aliases)(*args)


def _sum8(name, g8):
    r = g8.shape[1]

    def body(g_ref, o_ref):
        acc = g_ref[0]
        for d in range(1, 8):
            acc = acc + g_ref[d]
        o_ref[...] = acc

    return _call(name, body, (1,), [pl.BlockSpec((8, r, 128), lambda i: (0, 0, 0))],
                 pl.BlockSpec((r, 128), lambda i: (0, 0)), _sds((r, 128)))(g8)


def _place():
    x, y, c = lax.axis_index("x"), lax.axis_index("y"), lax.axis_index("c")
    return x, y, c, 2 * x + y, (x, y, 1 - c), [(1 - x, y), (x, 1 - y), (1 - x, 1 - y)]


class _Exchange:
    def __init__(self, arrays, out_shapes):
        self.arrays, self.out_shapes = list(arrays), list(out_shapes)
        n = len(self.arrays)
        self.sems = [pltpu.SemaphoreType.DMA((7 * n,)), pltpu.SemaphoreType.DMA((7 * n,)),
                     pltpu.SemaphoreType.DMA((n,))]

    def _copies(self, ins, outs, sems):
        send, recv, lsem = sems
        local_src, remote_src, dst = self.maps(ins, outs)
        x, y, c, q, sib, chips = _place()

        def rcopy(w, k, qq, cc, to, src=None):
            return pltpu.make_async_remote_copy(
                src_ref=dst(w, qq, cc) if src is None else src, dst_ref=dst(w, qq, cc),
                send_sem=send.at[7 * w + k], recv_sem=recv.at[7 * w + k], device_id=to, device_id_type=MESH)

        def mine(w):
            return pltpu.make_async_copy(local_src(w), dst(w, q, c), lsem.at[w])

        def first(w):
            return [rcopy(w, 0, q, c, sib, local_src(w))] + [
                rcopy(w, 1 + j, q, c, (cx, cy, c), remote_src(w, 2 * cx + cy)) for j, (cx, cy) in enumerate(chips)]

        return rcopy, mine, first, (x, y, c), q, c, sib, chips

    def start(self, ins, outs, sems):
        _, mine, first, *_ = self._copies(ins, outs, sems)
        for w in range(len(self.arrays)):
            mine(w).start()
            for cp in first(w):
                cp.start()

    def finish(self, ins, outs, sems):
        rcopy, mine, first, me, q, c, sib, chips = self._copies(ins, outs, sems)
        n = len(self.arrays)
        for w in range(n):
            for j, (cx, cy) in enumerate(chips):
                rcopy(w, 1 + j, 2 * cx + cy, c, me).wait_recv()
                rcopy(w, 4 + j, 2 * cx + cy, c, sib).start()
        for w in range(n):
            rcopy(w, 0, q, 1 - c, me).wait_recv()
            for j, (cx, cy) in enumerate(chips):
                rcopy(w, 4 + j, 2 * cx + cy, 1 - c, me).wait_recv()
        for w in range(n):
            for cp in first(w):
                cp.wait_send()
            for j, (cx, cy) in enumerate(chips):
                rcopy(w, 4 + j, 2 * cx + cy, c, sib).wait_send()
            mine(w).wait()


class _GatherWeights(_Exchange):
    def __init__(self, items):
        self.layers = [l for _, l in items]
        self.kh = [s.shape[1] // 2 for s, _ in items]
        super().__init__([s for s, _ in items], [_sds((NQ, 1) + s.shape[1:], BF16) for s, _ in items])

    def maps(self, ins, outs):
        c = lax.axis_index("c")
        src = lambda w: ins[w].at[pl.ds(self.layers[w], 1), pl.ds(c * self.kh[w], self.kh[w]), :]
        return src, lambda w, q: src(w), lambda w, q, cc: outs[w].at[q, :, pl.ds(cc * self.kh[w], self.kh[w]), :]


class _ScatterPartials(_Exchange):
    def __init__(self, parts):
        super().__init__(parts, [_sds((NQ, 1, 2) + p.shape[2:], BF16) for p in parts])

    def maps(self, ins, outs):
        q = 2 * lax.axis_index("x") + lax.axis_index("y")
        return (lambda w: ins[w].at[:, q]), (lambda w, qq: ins[w].at[:, qq]), (lambda w, qq, cc: outs[w].at[qq, :, cc])


class _Gather8(_Exchange):
    def __init__(self, v):
        super().__init__([v], [_sds((8,) + v.shape)])

    def maps(self, ins, outs):
        return (lambda w: ins[0]), (lambda w, q: ins[0]), (lambda w, q, cc: outs[0].at[2 * q + cc])


class _SwapHalves:
    def __init__(self, dws):
        self.arrays = list(dws)
        self.kh = [d.shape[2] // 2 for d in dws]
        self.out_shapes = [_sds(d.shape[:2] + (kh,) + d.shape[3:]) for d, kh in zip(dws, self.kh)]
        self.sems = [pltpu.SemaphoreType.DMA((len(dws),)), pltpu.SemaphoreType.DMA((len(dws),))]

    def _copies(self, ins, outs, sems):
        send, recv = sems
        _, _, c, _, sib, _ = _place()
        return [pltpu.make_async_remote_copy(
            src_ref=ins[w].at[:, :, pl.ds((1 - c) * self.kh[w], self.kh[w]), :], dst_ref=outs[w],
            send_sem=send.at[w], recv_sem=recv.at[w], device_id=sib, device_id_type=MESH)
            for w in range(len(self.arrays))]

    def start(self, ins, outs, sems):
        for cp in self._copies(ins, outs, sems):
            cp.start()

    def finish(self, ins, outs, sems):
        for cp in self._copies(ins, outs, sems):
            cp.wait()


def _comm_only(name, host):
    n_in, n_out = len(host.arrays), len(host.out_shapes)

    def body(*refs):
        ins, outs, sems = refs[:n_in], refs[n_in:n_in + n_out], refs[n_in + n_out:]
        host.start(ins, outs, sems)
        host.finish(ins, outs, sems)

    any_spec = pl.BlockSpec(memory_space=pl.ANY)
    return pl.pallas_call(body, name=name, in_specs=[any_spec] * n_in, out_specs=[any_spec] * n_out,
                          out_shape=host.out_shapes, scratch_shapes=host.sems)(*host.arrays)


def _add_halves(dw, got, cidx):
    nl, _, k, n = dw.shape
    kh = k // 2

    def body(c_ref, a_ref, b_ref, o_ref):
        o_ref[...] = (a_ref[...] + b_ref[...]).astype(BF16)

    grid_spec = pltpu.PrefetchScalarGridSpec(
        num_scalar_prefetch=1, grid=(nl, NQ),
        in_specs=[pl.BlockSpec((None, None, None, kh, n), lambda l, q, c_ref: (l, q, c_ref[0], 0, 0)),
                  pl.BlockSpec((None, None, kh, n), lambda l, q, c_ref: (l, q, 0, 0))],
        out_specs=pl.BlockSpec((None, None, kh, n), lambda l, q, c_ref: (l, q, 0, 0)))
    return pl.pallas_call(
        body, name="add_halves", grid_spec=grid_spec, out_shape=_sds((nl, NQ, kh, n), BF16),
        compiler_params=pltpu.CompilerParams(dimension_semantics=("arbitrary", "arbitrary"),
                                             vmem_limit_bytes=VMEM_LIMIT))(cidx, dw.reshape(nl, NQ, 2, kh, n), got)


def _gather8(name, v):
    return _comm_only(name, _Gather8(v))[0]


PACK = 16 * 128


def _pack(arrays):
    parts = []
    for a in arrays:
        flat = a.reshape(-1)
        parts.append(jnp.pad(flat, (0, (-flat.shape[0]) % PACK)))
    return jnp.concatenate(parts).reshape(-1, 128)


def _unpack(packed, shapes):
    flat = packed.reshape(-1)
    out, off = [], 0
    for shp in shapes:
        size = 1
        for d in shp:
            size *= d
        out.append(flat[off:off + size].reshape(shp))
        off += size + (-size) % PACK
    return out


def kernel(x, p, a_w_pw1, a_b_pw1, a_w_dw, a_b_dw, a_ln_g, a_ln_b, a_w_pw2, b_w_in, b_b_in, b_ln_g, b_ln_b, b_w_s, b_b_s, b_w_out, c_w_in, c_w_conv, c_w_out, ln1_g, ln1_b, ln2_g, ln2_b, ffn_w_gate, ffn_w_up, ffn_w_down, ple_w_gate, ple_w_proj, ple_norm_g, loss_target, m_a_w_pw1, m_a_b_pw1, m_a_w_dw, m_a_b_dw, m_a_ln_g, m_a_ln_b, m_a_w_pw2, m_b_w_in, m_b_b_in, m_b_ln_g, m_b_ln_b, m_b_w_s, m_b_b_s, m_b_w_out, m_c_w_in, m_c_w_conv, m_c_w_out, m_ln1_g, m_ln1_b, m_ln2_g, m_ln2_b, m_ffn_w_gate, m_ffn_w_up, m_ffn_w_down, m_ple_w_gate, m_ple_w_proj, m_ple_norm_g, v_a_w_pw1, v_a_b_pw1, v_a_w_dw, v_a_b_dw, v_a_ln_g, v_a_ln_b, v_a_w_pw2, v_b_w_in, v_b_b_in, v_b_ln_g, v_b_ln_b, v_b_w_s, v_b_b_s, v_b_w_out, v_c_w_in, v_c_w_conv, v_c_w_out, v_ln1_g, v_ln1_b, v_ln2_g, v_ln2_b, v_ffn_w_gate, v_ffn_w_up, v_ffn_w_down, v_ple_w_gate, v_ple_w_proj, v_ple_norm_g):
    args = dict(locals())
    wts = {k: args[k] for k in WEIGHTS}
    mom = {k: args["m_" + k] for k in WEIGHTS}
    var = {k: args["v_" + k] for k in WEIGHTS}
    for k in TRANSPOSED:
        wts[k], mom[k], var[k] = (jnp.transpose(t[k], (0, 2, 1)) for t in (wts, mom, var))
    q_idx = 2 * lax.axis_index("x") + lax.axis_index("y")
    c_idx = lax.axis_index("c").astype(jnp.int32).reshape(1)

    wb = {k: _cast_bf16(wts[k]) for k in BIG}
    mixw = [[("a_w_pw1", 0), ("a_w_pw2", 0)], [("b_w_in", 0), ("b_w_out", 0)], [("c_w_in", 0), ("c_w_out", 0)],
            [("a_w_pw1", 1), ("a_w_pw2", 1)]]
    ffnw = [[("ffn_w_gate", l), ("ffn_w_up", l), ("ffn_w_down", l)] for l in range(DEPTH)]
    plew = [[("ple_w_gate", l), ("ple_w_proj", l)] for l in range(DEPTH)]
    fwd_plan = {("a2", 0): ffnw[0], ("ffn", 0): mixw[1], ("ple", 0): plew[1],
                ("b", 1): ffnw[1], ("ffn", 1): mixw[2], ("ple", 1): plew[2],
                ("c1", 2): ffnw[2][:2], ("c2", 2): ffnw[2][2:], ("ffn", 2): mixw[3], ("ple", 2): plew[3],
                ("a2", 3): ffnw[3]}
    gw = {}

    def gather(keys):
        return _GatherWeights([(wb[name], l) for name, l in keys])

    def hosted(tag, fn, *fargs):
        keys = fwd_plan.get(tag)
        if not keys:
            return fn(*fargs)
        own, (got,) = fn(*fargs, hosts=[gather(keys)])
        gw.update(zip(keys, got))
        return own

    first_keys = mixw[0][:1]
    fwd_plan["a1", 0] = mixw[0][1:] + plew[0]
    gw.update(zip(first_keys, _comm_only("gather_first", gather(first_keys))))
    shard_shapes = [wts[k].shape for k in SMALL_SHARDED]
    small8 = _gather8("gather_small", _pack([wts[k] for k in SMALL_SHARDED]))
    per_chip = [_unpack(small8[2 * qq], shard_shapes) for qq in range(NQ)]
    full = {k: jnp.concatenate([per_chip[qq][i] for qq in range(NQ)], axis=-1) for i, k in enumerate(SMALL_SHARDED)}
    for k in SMALL_REPL:
        full[k] = wts[k]

    def vec(name, l):
        return full[name][l][None, :]

    def conv_w(name, l, rows):
        w = full[name][l]
        return jnp.pad(w, ((0, rows - w.shape[0]), (0, 0)))

    ws = full["b_w_s"][0]
    wst = jnp.transpose(ws, (0, 2, 1))
    bsx = jnp.broadcast_to(full["b_b_s"][0][:, :, None], (SGU_H, SGU_T, SGU_G))

    x0s, z1s, z2s, saved = [], [], [], []
    cur = x[0]
    for i in range(DEPTH):
        mix, j = i % 3, i // 3
        x0s.append(cur)
        if mix == 0:
            h, glu = hosted(("a1", i), _fwd_a1, cur, gw["a_w_pw1", j], vec("a_b_pw1", j), i)
            z1, cv = hosted(("a2", i), _fwd_a2, glu, cur, conv_w("a_w_dw", j, 32), vec("a_b_dw", j), vec("a_ln_g", j),
                            vec("a_ln_b", j), gw["a_w_pw2", j], i)
            saved.append((h, glu, cv))
        elif mix == 1:
            z1, h = hosted(("b", i), _fwd_b, cur, gw["b_w_in", 0], vec("b_b_in", 0), vec("b_ln_g", 0),
                           vec("b_ln_b", 0), ws, bsx, gw["b_w_out", 0])
            saved.append((h,))
        else:
            hc = hosted(("c1", i), _fwd_c1, cur, gw["c_w_in", 0])
            z1 = hosted(("c2", i), _fwd_c2, hc, cur, conv_w("c_w_conv", 0, 8), gw["c_w_out", 0])
            saved.append((hc,))
        z2 = hosted(("ffn", i), _fwd_ffn, z1, vec("ln1_g", i), vec("ln1_b", i), gw["ffn_w_gate", i],
                    gw["ffn_w_up", i], gw["ffn_w_down", i], i)
        cur = hosted(("ple", i), _fwd_ple, z2, p[i, 0], vec("ln2_g", i), vec("ln2_b", i), gw["ple_w_gate", i],
                     gw["ple_w_proj", i], vec("ple_norm_g", i), i)
        z1s.append(z1)
        z2s.append(z2)

    g, loss_acc = _loss_head(cur, loss_target[0])
    loss = lax.psum(0.5 / D * jnp.sum(loss_acc[0]), ("x", "y", "c"))

    dws = {}
    sg = {}
    res = {k: None for k in BIG}

    def wgrad(name, l, a, amode, b, bmode):
        _, k, n = wts[name].shape
        dws[name, l] = _mm_tn(f"dw_{name}_{l}", a, amode, b, bmode, k, n)

    def swap(keys):
        return _SwapHalves([dws[k] for k in keys])

    def add_halves(keys, got):
        return [_add_halves(dws[k], r, c_idx) for k, r in zip(keys, got)]

    def update(keys, contribs):
        for (name, l), gc in zip(keys, contribs):
            _, kq, n = wts[name].shape
            res[name] = _adam(f"adam_{name}_{l}", wts[name], mom[name], var[name], gc.reshape(NQ, 1, kq, n), l,
                              res[name])

    small = SMALL_SHARDED + SMALL_REPL
    late_small = [("a_w_dw", 0), ("a_b_pw1", 0)]
    early_small = [(k, l) for k in small for l in range(full[k].shape[0]) if (k, l) not in late_small]
    pending = None
    for i in reversed(range(DEPTH)):
        mix, j = i % 3, i // 3
        ple_args = (g, z2s[i], p[i, 0], vec("ln2_g", i), vec("ln2_b", i), gw["ple_w_gate", i], gw["ple_w_proj", i],
                    vec("ple_norm_g", i), i)
        if pending:
            (dz2, x2b, dgp, dqp, acc), (got,) = _bwd_ple(*ple_args, hosts=[swap(pending)])
            parts = add_halves(pending, got)
        else:
            dz2, x2b, dgp, dqp, acc = _bwd_ple(*ple_args)
        sg["ple_norm_g", i], sg["ln2_g", i], sg["ln2_b", i] = acc[0], acc[1], acc[2]
        wgrad("ple_w_gate", i, x2b, "c", dgp, "1")
        wgrad("ple_w_proj", i, p[i, 0], "1", dqp, "c")
        ffn_args = (dz2, z1s[i], vec("ln1_g", i), vec("ln1_b", i), gw["ffn_w_gate", i], gw["ffn_w_up", i],
                    gw["ffn_w_down", i], i)
        if pending:
            (dz1, x1b, hm, da, du, acc), (contribs,) = _bwd_ffn(*ffn_args, hosts=[_ScatterPartials(parts)])
            update(pending, contribs)
        else:
            dz1, x1b, hm, da, du, acc = _bwd_ffn(*ffn_args)
        sg["ln1_g", i], sg["ln1_b", i] = acc[0], acc[1]
        wgrad("ffn_w_gate", i, da, "g", x1b, "1")
        wgrad("ffn_w_up", i, du, "g", x1b, "1")
        wgrad("ffn_w_down", i, hm, "g", dz2, "1")
        x0 = x0s[i]
        if mix == 0:
            h, glu, cv = saved[i]
            a2_args = (dz1, cv, vec("a_ln_g", j), vec("a_ln_b", j), gw["a_w_pw2", j], i)
            conv_args = (glu, conv_w("a_w_dw", j, 32), i)
            if i == 0:
                early = ffnw[0] + plew[0]
                (dcv, sb, acc), (got,) = _bwd_a2(*a2_args, hosts=[swap(early)])
                sg["a_ln_g", j], sg["a_ln_b", j], sg["a_b_dw", j] = acc[0], acc[1], acc[2]
                parts = add_halves(early, got)
                (dglu, dwdw), (contribs, (g8_early,)) = _bwd_conv_a(
                    dcv, *conv_args, hosts=[_ScatterPartials(parts), _Gather8(_pack([sg[pc] for pc in early_small]))])
                update(early, contribs)
            else:
                dcv, sb, acc = _bwd_a2(*a2_args)
                sg["a_ln_g", j], sg["a_ln_b", j], sg["a_b_dw", j] = acc[0], acc[1], acc[2]
                dglu, dwdw = _bwd_conv_a(dcv, *conv_args)
            wgrad("a_w_pw2", j, sb, "c", dz1, "1")
            sg["a_w_dw", j] = dwdw[:CONV_A]
            g, dh, acc = _bwd_a1(dglu, h, dz1, gw["a_w_pw1", j], i)
            sg["a_b_pw1", j] = acc[0]
            wgrad("a_w_pw1", j, x0, "1", dh, "c")
        elif mix == 1:
            (h,) = saved[i]
            g, dh, mb, acc, dw_s, db_s = _bwd_b(dz1, h, vec("b_ln_g", 0), vec("b_ln_b", 0), gw["b_w_in", 0],
                                                gw["b_w_out", 0], ws, wst, bsx)
            sg["b_b_in", 0], sg["b_ln_g", 0], sg["b_ln_b", 0] = acc[0], acc[1, :E], acc[2, :E]
            sg["b_w_s", 0], sg["b_b_s", 0] = dw_s, jnp.sum(db_s, axis=-1)
            wgrad("b_w_out", 0, mb, "c", dz1, "1")
            wgrad("b_w_in", 0, x0, "1", dh, "c")
        else:
            (hc,) = saved[i]
            wc = conv_w("c_w_conv", 0, 8)
            dy, dbg, mb = _bwd_c2(dz1, hc, wc, gw["c_w_out", 0])
            wgrad("c_w_out", 0, mb, "c", dz1, "1")
            g, dhc, dwc = _bwd_c1(dy, hc, dbg, dz1, wc, gw["c_w_in", 0])
            sg["c_w_conv", 0] = dwc[:CONV_C]
            wgrad("c_w_in", 0, x0, "1", dhc, "c")
        pending = mixw[i] + ffnw[i] + plew[i] if i > 0 else mixw[0]
    grad_x = g[None]
    parts = add_halves(pending, _comm_only("swap_last", swap(pending)))
    update(pending, _comm_only("scatter_last", _ScatterPartials(parts)))

    g8_late = _gather8("gather_small_late", _pack([sg[pc] for pc in late_small]))
    sums = dict(zip(early_small, _unpack(_sum8("sum8_early", g8_early), [sg[pc].shape for pc in early_small])))
    sums.update(zip(late_small, _unpack(_sum8("sum8_late", g8_late), [sg[pc].shape for pc in late_small])))
    gsum = [jnp.stack([sums[k, l] for l in range(full[k].shape[0])]) for k in small]
    gmine = []
    for k, gs in zip(small, gsum):
        if k in SMALL_SHARDED:
            wdt = wts[k].shape[-1]
            gs = lax.dynamic_slice_in_dim(gs, q_idx * wdt, wdt, axis=gs.ndim - 1)
        gmine.append(gs)
    packed = [_pack(t)[None] for t in ([wts[k] for k in small], [mom[k] for k in small], [var[k] for k in small])]
    outs = _adam("adam_small", packed[0], packed[1], packed[2], _pack(gmine)[None, None], 0, None)
    unpacked = [_unpack(o[0], [wts[k].shape for k in small]) for o in outs]
    for i, k in enumerate(small):
        res[k] = tuple(u[i] for u in unpacked)

    for k in TRANSPOSED:
        res[k] = tuple(jnp.transpose(r, (0, 2, 1)) for r in res[k])
    return (loss, grad_x, *[res[k][0] for k in WEIGHTS], *[res[k][1] for k in WEIGHTS],
            *[res[k][2] for k in WEIGHTS], *[res[k][3] for k in WEIGHTS])
```

```python
import jax
import jax.numpy as jnp
from jax import lax
from jax.experimental import pallas as pl
from jax.experimental.pallas import tpu as pltpu

F32, BF16 = jnp.float32, jnp.bfloat16
S = 4096
D = 1024
E = 2048
FF = 2816
FQ = FF // 4
NQ = 4
DEPTH = 4
ALPHA = (2 * DEPTH) ** 0.25
LN_EPS = 1e-5
CONV_A, CONV_C = 31, 3
HALO_A, HALO_C = 32, 8
SGU_T, SGU_H, SGU_G, SGU_CHUNK = 128, 8, 256, 64
VMEM_LIMIT = 56 * 1024 * 1024
MESH = pl.DeviceIdType.MESH
ADAM_LR, ADAM_B1, ADAM_B2, ADAM_EPS, ADAM_WD, ADAM_STEP = 0.001, 0.9, 0.999, 1e-08, 0.01, 10
GELU_C, GELU_A = 0.7978845608028654, 0.044715

BIG = ["a_w_pw1", "a_w_pw2", "b_w_in", "b_w_out", "c_w_in", "c_w_out",
       "ffn_w_gate", "ffn_w_up", "ffn_w_down", "ple_w_gate", "ple_w_proj"]
TRANSPOSED = ["ffn_w_gate", "ffn_w_up"]
ROW_SHARDED = ["a_w_pw2", "b_w_out", "c_w_out", "ffn_w_gate", "ffn_w_up", "ffn_w_down", "ple_w_gate"]
SMALL_SHARDED = ["a_b_pw1", "a_w_dw", "a_b_dw", "a_ln_g", "a_ln_b", "c_w_conv"]
SMALL_REPL = ["b_b_in", "b_ln_g", "b_ln_b", "b_w_s", "b_b_s", "ln1_g", "ln1_b", "ln2_g", "ln2_b", "ple_norm_g"]
WEIGHTS = ["a_w_pw1", "a_b_pw1", "a_w_dw", "a_b_dw", "a_ln_g", "a_ln_b", "a_w_pw2", "b_w_in", "b_b_in", "b_ln_g",
           "b_ln_b", "b_w_s", "b_b_s", "b_w_out", "c_w_in", "c_w_conv", "c_w_out", "ln1_g", "ln1_b", "ln2_g",
           "ln2_b", "ffn_w_gate", "ffn_w_up", "ffn_w_down", "ple_w_gate", "ple_w_proj", "ple_norm_g"]


def _call(name, body, grid, in_specs, out_specs, out_shape, scratch=(), aliases=None, hosts=()):
    params = pltpu.CompilerParams(dimension_semantics=("arbitrary",) * len(grid), vmem_limit_bytes=VMEM_LIMIT)
    if not hosts:
        return pl.pallas_call(
            body, name=name, grid=grid, in_specs=in_specs, out_specs=out_specs, out_shape=out_shape,
            scratch_shapes=list(scratch), input_output_aliases=aliases or {}, compiler_params=params)
    assert len(grid) == 1 and not aliases
    single = not isinstance(out_shape, (list, tuple))
    own_shapes = [out_shape] if single else list(out_shape)
    own_specs = [out_specs] if single else list(out_specs)
    n_in, n_out, n_scr = len(in_specs), len(own_shapes), len(scratch)
    h_in = [len(h.arrays) for h in hosts]
    h_out = [len(h.out_shapes) for h in hosts]
    h_sem = [len(h.sems) for h in hosts]

    def split(refs, counts):
        out, off = [], 0
        for cnt in counts:
            out.append(refs[off:off + cnt])
            off += cnt
        return out

    def wrapped(*refs):
        ins, hin, outs, hout, scr, hsem = split(refs, [n_in, sum(h_in), n_out, sum(h_out), n_scr, sum(h_sem)])
        per_host = list(zip(hosts, split(hin, h_in), split(hout, h_out), split(hsem, h_sem)))

        @pl.when(pl.program_id(0) == 0)
        def _():
            for h, a, o, s in per_host:
                h.start(a, o, s)

        body(*ins, *outs, *scr)

        @pl.when(pl.program_id(0) == grid[0] - 1)
        def _():
            for h, a, o, s in per_host:
                h.finish(a, o, s)

    any_spec = pl.BlockSpec(memory_space=pl.ANY)
    call = pl.pallas_call(
        wrapped, name=name, grid=grid, in_specs=list(in_specs) + [any_spec] * sum(h_in),
        out_specs=own_specs + [any_spec] * sum(h_out),
        out_shape=own_shapes + [s for h in hosts for s in h.out_shapes],
        scratch_shapes=list(scratch) + [s for h in hosts for s in h.sems], compiler_params=params)

    def run(*args):
        res = call(*args, *[a for h in hosts for a in h.arrays])
        own = res[0] if single else list(res[:n_out])
        return own, split(list(res[n_out:]), h_out)

    return run


def _sds(shape, dtype=F32):
    return jax.ShapeDtypeStruct(shape, dtype)


def _row(tm, c):
    return pl.BlockSpec((tm, c), lambda i: (i, 0))


def _grow(g, tm, c):
    return pl.BlockSpec((g, tm, c), lambda i: (0, i, 0))


def _const(shape):
    nd = len(shape)
    return pl.BlockSpec(shape, lambda i: (0,) * nd, pipeline_mode=pl.Buffered(1))


def _wspec(w):
    return pl.BlockSpec((NQ, None, w.shape[2], w.shape[3]), lambda i: (0, 0, 0, 0), pipeline_mode=pl.Buffered(1))


def _prev(tm, hb, c):
    return pl.BlockSpec((hb, c), lambda i: (jnp.maximum(i * (tm // hb) - 1, 0), 0))


def _next(tm, hb, c):
    return pl.BlockSpec((hb, c), lambda i: (jnp.minimum((i + 1) * (tm // hb), S // hb - 1), 0))


def _acc(r, c):
    return pl.BlockSpec((r, c), lambda i: (0, 0))


def _sig(x):
    return 1.0 / (1.0 + jnp.exp(-x))


def _ln(z, g, b):
    mu = jnp.mean(z, axis=-1, keepdims=True)
    zc = z - mu
    rstd = lax.rsqrt(jnp.mean(zc * zc, axis=-1, keepdims=True) + LN_EPS)
    xhat = zc * rstd
    return xhat * g + b, xhat, rstd


def _ln_bwd(dyg, xhat, rstd):
    return rstd * (dyg - jnp.mean(dyg, axis=-1, keepdims=True) - xhat * jnp.mean(dyg * xhat, axis=-1, keepdims=True))


def _mm(a, w):
    return jnp.dot(a.astype(BF16), w, preferred_element_type=F32)


def _mmt(a, w):
    return lax.dot_general(a.astype(BF16), w, (((1,), (1,)), ((), ())), preferred_element_type=F32)


def _colsum(x):
    return jnp.sum(x, axis=0, keepdims=True)


def _gelu(x):
    t = jnp.tanh(GELU_C * (x + GELU_A * x * x * x))
    return 0.5 * x * (1.0 + t), t


def _gelu_grad(x, t):
    return 0.5 * (1.0 + t) + 0.5 * x * (1.0 - t * t) * GELU_C * (1.0 + 3.0 * GELU_A * x * x)


def _silu_grad(a, sg):
    return sg * (1.0 + a * (1.0 - sg))


def _sgu_masks():
    r = lax.broadcasted_iota(jnp.int32, (SGU_T, SGU_T), 0) // SGU_CHUNK
    c = lax.broadcasted_iota(jnp.int32, (SGU_T, SGU_T), 1) // SGU_CHUNK
    return r >= c, c >= r


def _fill_halo(buf, lo, n, halo_val_fn, is_edge):
    @pl.when(is_edge)
    def _():
        buf[lo:lo + n, :] = jnp.zeros((n, buf.shape[1]), F32)

    @pl.when(jnp.logical_not(is_edge))
    def _():
        buf[lo:lo + n, :] = halo_val_fn()


SUB, LANE = 8, 128
ROWS_AT_ONCE = 16


def _shift_copies(buf, sh):
    rows = sh.shape[1]
    for s in range(1, SUB):
        sh[s - 1, :, :] = buf[pl.ds(s, rows), :]


def _tiles(buf, sh, s, first, count, group0, lanes):
    src = buf if s == 0 else sh.at[s - 1]
    return {t: src[pl.ds(pl.multiple_of((group0 + t) * SUB, SUB), SUB), lanes] for t in range(first, first + count)}


def _by_shift(offsets):
    out = []
    for s in range(SUB):
        taps = [(k, o // SUB) for k, o in enumerate(offsets) if o % SUB == s]
        if taps:
            out.append((s, taps))
    return out


def _conv_rows(out_ref, w_ref, bias_ref, offsets, buf, sh, tm):
    n = ROWS_AT_ONCE
    for cb in range(D // LANE):
        lanes = slice(cb * LANE, (cb + 1) * LANE)
        bias = None if bias_ref is None else jnp.broadcast_to(bias_ref[:, lanes], (SUB, LANE))

        def body(jb, carry):
            accs = [bias] * n
            for s, taps in _by_shift(offsets):
                ms = [m for _, m in taps]
                tiles = _tiles(buf, sh, s, min(ms), max(ms) - min(ms) + n, jb * n, lanes)
                for k, m in taps:
                    wk = jnp.broadcast_to(w_ref[k:k + 1, lanes], (SUB, LANE))
                    for jj in range(n):
                        t = wk * tiles[m + jj]
                        accs[jj] = t if accs[jj] is None else accs[jj] + t
            for jj in range(n):
                out_ref[pl.ds(pl.multiple_of((jb * n + jj) * SUB, SUB), SUB), lanes] = accs[jj]
            return carry

        lax.fori_loop(0, tm // (SUB * n), body, 0)


def _conv_wgrad(dw_ref, d_ref, offsets, buf, sh, tm):
    n = 4
    for cb in range(D // LANE):
        lanes = slice(cb * LANE, (cb + 1) * LANE)

        def body(jq, accs):
            accs = list(accs)
            d = [d_ref[pl.ds(pl.multiple_of((jq * n + jj) * SUB, SUB), SUB), lanes] for jj in range(n)]
            for s, taps in _by_shift(offsets):
                ms = [m for _, m in taps]
                tiles = _tiles(buf, sh, s, min(ms), max(ms) - min(ms) + n, jq * n, lanes)
                for k, m in taps:
                    for jj in range(n):
                        accs[k] = accs[k] + d[jj] * tiles[m + jj]
            return tuple(accs)

        accs = lax.fori_loop(0, tm // (SUB * n), body, tuple(jnp.zeros((SUB, LANE), F32) for _ in offsets))
        for k, acc in enumerate(accs):
            dw_ref[k:k + 1, lanes] += jnp.sum(acc, axis=0, keepdims=True)


def _fwd_a1(x0, w1, b1, l, hosts=()):
    tm = 512

    def body(x_ref, w_ref, b_ref, h_ref, glu_ref):
        xb = x_ref[...].astype(BF16)
        for q in range(NQ):
            sl = slice(q * 512, (q + 1) * 512)
            h_ref[:, sl] = jnp.dot(xb, w_ref[q], preferred_element_type=F32) + b_ref[:, sl]
        glu_ref[...] = h_ref[:, :D] * _sig(h_ref[:, D:])

    return _call(f"fwd_a1_{l}", body, (S // tm,), [_row(tm, D), _wspec(w1), _const((1, 2 * D))],
                 [_row(tm, 2 * D), _row(tm, D)], [_sds((S, 2 * D)), _sds((S, D))], hosts=hosts)(x0, w1, b1)


def _fwd_a2(glu, x0, wdw, bdw, lg, lb, w2, l, hosts=()):
    tm = 256

    def body(g_ref, gp_ref, x_ref, wdw_ref, bdw_ref, lg_ref, lb_ref, w2_ref, z_ref, cv_ref, buf, sh):
        i = pl.program_id(0)
        _fill_halo(buf, 0, HALO_A, lambda: gp_ref[...], i == 0)
        buf[HALO_A:HALO_A + tm, :] = g_ref[...]
        _shift_copies(buf, sh)
        _conv_rows(cv_ref, wdw_ref, bdw_ref, [HALO_A - (CONV_A - 1) + k for k in range(CONV_A)], buf, sh, tm)
        n, _, _ = _ln(cv_ref[...], lg_ref[...], lb_ref[...])
        sb = (n * _sig(n)).astype(BF16)
        z_ref[...] = ALPHA * x_ref[...] + jnp.dot(sb, w2_ref[...], preferred_element_type=F32)

    return _call(f"fwd_a2_{l}", body, (S // tm,),
                 [_row(tm, D), _prev(tm, HALO_A, D), _row(tm, D), _const((32, D)), _const((1, D)), _const((1, D)),
                  _const((1, D)), _const(w2.shape)],
                 [_row(tm, D), _row(tm, D)], [_sds((S, D)), _sds((S, D))],
                 scratch=[pltpu.VMEM((HALO_A + tm, D), F32), pltpu.VMEM((SUB - 1, HALO_A + tm - SUB, D), F32)],
                 hosts=hosts)(glu, glu, x0, wdw, bdw, lg, lb, w2)


def _fwd_b(x0, win, b_in, lg, lb, ws, bsx, wout, hosts=()):
    tm = 256

    def body(x_ref, win_ref, bin_ref, lg_ref, lb_ref, ws_ref, bsx_ref, wout_ref, z_ref, h_ref, f_scr):
        xb = x_ref[...].astype(BF16)
        for q in range(NQ):
            sl = slice(q * 1024, (q + 1) * 1024)
            h_ref[:, sl] = jnp.dot(xb, win_ref[q], preferred_element_type=F32) + bin_ref[:, sl]
        u, _ = _gelu(h_ref[:, :E])
        v, _ = _gelu(h_ref[:, E:])
        vn, _, _ = _ln(v, lg_ref[...], lb_ref[...])
        vnb = vn.astype(BF16)
        mask, _ = _sgu_masks()
        for hd in range(SGU_H):
            wm = jnp.where(mask, ws_ref[hd], 0.0).astype(BF16)
            cs = slice(hd * SGU_G, (hd + 1) * SGU_G)
            for n in range(tm // SGU_T):
                rs = slice(n * SGU_T, (n + 1) * SGU_T)
                f_scr[rs, cs] = jnp.dot(wm, vnb[rs, cs], preferred_element_type=F32) + bsx_ref[hd]
        mb = (u * f_scr[...]).astype(BF16)
        z_ref[...] = ALPHA * x_ref[...] + jnp.dot(mb, wout_ref[...], preferred_element_type=F32)

    return _call("fwd_b", body, (S // tm,),
                 [_row(tm, D), _wspec(win), _const((1, 2 * E)), _const((1, E)), _const((1, E)),
                  _const((SGU_H, SGU_T, SGU_T)), _const((SGU_H, SGU_T, SGU_G)), _const(wout.shape)],
                 [_row(tm, D), _row(tm, 2 * E)], [_sds((S, D)), _sds((S, 2 * E))],
                 scratch=[pltpu.VMEM((tm, E), F32)], hosts=hosts)(x0, win, b_in, lg, lb, ws, bsx, wout)


def _fwd_c1(x0, win, hosts=()):
    tm = 512

    def body(x_ref, w_ref, hc_ref):
        xb = x_ref[...].astype(BF16)
        for q in range(NQ):
            hc_ref[:, q * 768:(q + 1) * 768] = jnp.dot(xb, w_ref[q], preferred_element_type=F32)

    return _call("fwd_c1", body, (S // tm,), [_row(tm, D), _wspec(win)], _row(tm, 3 * D),
                 _sds((S, 3 * D)), hosts=hosts)(x0, win)


def _short_conv(buf, hc_ref, hcp_ref, wc_ref, tm, i):
    _fill_halo(buf, 0, HALO_C, lambda: hcp_ref[:, D:2 * D] * hcp_ref[:, 2 * D:], i == 0)
    buf[HALO_C:HALO_C + tm, :] = hc_ref[:, D:2 * D] * hc_ref[:, 2 * D:]
    y = wc_ref[0:1, :] * buf[pl.ds(HALO_C - 2, tm), :]
    for k in range(1, CONV_C):
        y = y + wc_ref[k:k + 1, :] * buf[pl.ds(HALO_C - 2 + k, tm), :]
    return y


def _fwd_c2(hc, x0, wc, wout, hosts=()):
    tm = 256

    def body(hc_ref, hcp_ref, x_ref, wc_ref, wout_ref, z_ref, buf):
        y = _short_conv(buf, hc_ref, hcp_ref, wc_ref, tm, pl.program_id(0))
        mb = (hc_ref[:, :D] * y).astype(BF16)
        z_ref[...] = ALPHA * x_ref[...] + jnp.dot(mb, wout_ref[...], preferred_element_type=F32)

    return _call("fwd_c2", body, (S // tm,),
                 [_row(tm, 3 * D), _prev(tm, HALO_C, 3 * D), _row(tm, D), _const((8, D)), _const(wout.shape)],
                 _row(tm, D), _sds((S, D)), scratch=[pltpu.VMEM((HALO_C + tm, D), F32)], hosts=hosts
                 )(hc, hc, x0, wc, wout)


def _fwd_ffn(z1, lg, lb, wgt, wut, wd, l, hosts=()):
    tm = 256

    def body(z_ref, lg_ref, lb_ref, wg_ref, wu_ref, wd_ref, o_ref, a_ref, u_ref, hm_ref):
        x1, _, _ = _ln(z_ref[...], lg_ref[...], lb_ref[...])
        xb = x1.astype(BF16)
        a = _mmt(xb, wg_ref[...])
        u = _mmt(xb, wu_ref[...])
        hmb = (a * _sig(a) * u).astype(BF16)
        a_ref[...] = a.astype(BF16)
        u_ref[...] = u.astype(BF16)
        hm_ref[...] = hmb
        o_ref[...] = ALPHA * x1 + jnp.dot(hmb, wd_ref[...], preferred_element_type=F32)

    return _call(f"fwd_ffn_{l}", body, (S // tm,),
                 [_row(tm, D), _const((1, D)), _const((1, D)), _const((FF, D)), _const((FF, D)), _const((FF, D))],
                 [_row(tm, D), _row(tm, FF), _row(tm, FF), _row(tm, FF)],
                 [_sds((S, D)), _sds((S, FF), BF16), _sds((S, FF), BF16), _sds((S, FF), BF16)],
                 hosts=hosts)(z1, lg, lb, wgt, wut, wd)


def _ple_parts(z2, p, lg, lb, wg_ref, wp_ref, pg):
    x2, xhat, rstd = _ln(z2, lg, lb)
    xb = x2.astype(BF16)
    gate = _sig(jnp.dot(xb, wg_ref[...], preferred_element_type=F32))
    pb = p.astype(BF16)
    qp = jnp.concatenate([jnp.dot(pb, wp_ref[q], preferred_element_type=F32) for q in range(NQ)], axis=1)
    rs = lax.rsqrt(jnp.mean(qp * qp, axis=-1, keepdims=True) + LN_EPS)
    qn = qp * rs
    return x2, xhat, rstd, xb, gate, qn, rs, qn * pg


def _fwd_ple(z2, p, lg, lb, wg, wp, pg, l, hosts=()):
    tm = 512

    def body(z_ref, p_ref, lg_ref, lb_ref, wg_ref, wp_ref, pg_ref, o_ref):
        x2, _, _, _, gate, _, _, r = _ple_parts(z_ref[...], p_ref[...], lg_ref[...], lb_ref[...], wg_ref, wp_ref,
                                                pg_ref[...])
        o_ref[...] = x2 + gate * r

    return _call(f"fwd_ple_{l}", body, (S // tm,),
                 [_row(tm, D), _row(tm, 256), _const((1, D)), _const((1, D)), _const(wg.shape), _wspec(wp),
                  _const((1, D))],
                 _row(tm, D), _sds((S, D)), hosts=hosts)(z2, p, lg, lb, wg, wp, pg)


def _loss_head(y, target):
    tm = 512

    def body(y_ref, t_ref, dy_ref, acc_ref):
        @pl.when(pl.program_id(0) == 0)
        def _():
            acc_ref[...] = jnp.zeros_like(acc_ref)

        e = y_ref[...] - t_ref[...]
        dy_ref[...] = e * (1.0 / D)
        acc_ref[0:1, :] += _colsum(e * e)

    return _call("loss_head", body, (S // tm,), [_row(tm, D), _row(tm, D)], [_row(tm, D), _acc(8, D)],
                 [_sds((S, D)), _sds((8, D))])(y, target)


def _zero_first(*refs):
    @pl.when(pl.program_id(0) == 0)
    def _():
        for r in refs:
            r[...] = jnp.zeros_like(r)


def _bwd_ple(g, z2, p, lg, lb, wg, wp, pg, l, hosts=()):
    tm = 256

    def body(g_ref, z_ref, p_ref, lg_ref, lb_ref, wg_ref, wp_ref, pg_ref, dz_ref, xb_ref, dgp_ref, dqp_ref, acc_ref):
        _zero_first(acc_ref)
        gin = g_ref[...]
        lgv, pgv = lg_ref[...], pg_ref[...]
        _, xhat, rstd, xb, gate, qn, rs, r = _ple_parts(z_ref[...], p_ref[...], lgv, lb_ref[...], wg_ref, wp_ref, pgv)
        xb_ref[...] = xb
        dgpb = (gin * r * gate * (1.0 - gate)).astype(BF16)
        dgp_ref[...] = dgpb
        dx2 = gin + _mmt(dgpb, wg_ref[...])
        dr = gin * gate
        acc_ref[0:1, :] += _colsum(dr * qn)
        t = dr * pgv
        dqp_ref[...] = (rs * (t - qn * jnp.mean(t * qn, axis=-1, keepdims=True))).astype(BF16)
        acc_ref[1:2, :] += _colsum(dx2 * xhat)
        acc_ref[2:3, :] += _colsum(dx2)
        dz_ref[...] = _ln_bwd(dx2 * lgv, xhat, rstd)

    return _call(f"bwd_ple_{l}", body, (S // tm,),
                 [_row(tm, D), _row(tm, D), _row(tm, 256), _const((1, D)), _const((1, D)), _const(wg.shape),
                  _wspec(wp), _const((1, D))],
                 [_row(tm, D), _row(tm, D), _row(tm, D), _row(tm, D), _acc(8, D)],
                 [_sds((S, D)), _sds((S, D), BF16), _sds((S, D), BF16), _sds((S, D), BF16), _sds((8, D))],
                 hosts=hosts)(g, z2, p, lg, lb, wg, wp, pg)


def _bwd_ffn(dz2, z1, ab, ub, lg, lb, wgt, wut, wd, l, hosts=()):
    tm = 256

    def body(dz2_ref, z_ref, a_ref, u_ref, lg_ref, lb_ref, wg_ref, wu_ref, wd_ref, dz1_ref, xb_ref, da_ref, du_ref,
             acc_ref):
        _zero_first(acc_ref)
        dz2v = dz2_ref[...]
        lgv = lg_ref[...]
        x1, xhat, rstd = _ln(z_ref[...], lgv, lb_ref[...])
        xb_ref[...] = x1.astype(BF16)
        a = a_ref[...].astype(F32)
        u = u_ref[...].astype(F32)
        sg = _sig(a)
        dhm = _mmt(dz2v, wd_ref[...])
        dub = (dhm * (a * sg)).astype(BF16)
        dab = (dhm * u * _silu_grad(a, sg)).astype(BF16)
        da_ref[...] = dab
        du_ref[...] = dub
        dx1 = ALPHA * dz2v + _mm(dab, wg_ref[...]) + _mm(dub, wu_ref[...])
        acc_ref[0:1, :] += _colsum(dx1 * xhat)
        acc_ref[1:2, :] += _colsum(dx1)
        dz1_ref[...] = _ln_bwd(dx1 * lgv, xhat, rstd)

    return _call(f"bwd_ffn_{l}", body, (S // tm,),
                 [_row(tm, D), _row(tm, D), _row(tm, FF), _row(tm, FF), _const((1, D)), _const((1, D)),
                  _const((FF, D)), _const((FF, D)), _const((FF, D))],
                 [_row(tm, D), _row(tm, D), _row(tm, FF), _row(tm, FF), _acc(8, D)],
                 [_sds((S, D)), _sds((S, D), BF16), _sds((S, FF), BF16), _sds((S, FF), BF16), _sds((8, D))],
                 hosts=hosts)(dz2, z1, ab, ub, lg, lb, wgt, wut, wd)


def _bwd_a2(dz1, cv, lg, lb, w2, l, hosts=()):
    tm = 512

    def body(dz_ref, cv_ref, lg_ref, lb_ref, w2_ref, dcv_ref, sb_ref, acc_ref):
        _zero_first(acc_ref)
        lgv = lg_ref[...]
        n, xhat, rstd = _ln(cv_ref[...], lgv, lb_ref[...])
        sg = _sig(n)
        sb_ref[...] = (n * sg).astype(BF16)
        dzb = dz_ref[...].astype(BF16)
        ds = _mmt(dzb, w2_ref[...])
        dn = ds * _silu_grad(n, sg)
        acc_ref[0:1, :] += _colsum(dn * xhat)
        acc_ref[1:2, :] += _colsum(dn)
        dcv = _ln_bwd(dn * lgv, xhat, rstd)
        acc_ref[2:3, :] += _colsum(dcv)
        dcv_ref[...] = dcv

    return _call(f"bwd_a2_{l}", body, (S // tm,),
                 [_row(tm, D), _row(tm, D), _const((1, D)), _const((1, D)), _const(w2.shape)],
                 [_row(tm, D), _row(tm, D), _acc(8, D)],
                 [_sds((S, D)), _sds((S, D), BF16), _sds((8, D))], hosts=hosts)(dz1, cv, lg, lb, w2)


def _bwd_conv_a(dcv, glu, wdw, l, hosts=()):
    tm = 256
    nb = S // tm

    def body(d_ref, dn_ref, g_ref, gp_ref, w_ref, dglu_ref, dw_ref, bufd, bufx, sh):
        i = pl.program_id(0)
        _zero_first(dw_ref)
        bufd[0:tm, :] = d_ref[...]
        _fill_halo(bufd, tm, HALO_A, lambda: dn_ref[...], i == nb - 1)
        _fill_halo(bufx, 0, HALO_A, lambda: gp_ref[...], i == 0)
        bufx[HALO_A:HALO_A + tm, :] = g_ref[...]
        _shift_copies(bufd, sh)
        _conv_rows(dglu_ref, w_ref, None, [CONV_A - 1 - k for k in range(CONV_A)], bufd, sh, tm)
        _shift_copies(bufx, sh)
        _conv_wgrad(dw_ref, d_ref, [HALO_A - (CONV_A - 1) + k for k in range(CONV_A)], bufx, sh, tm)

    return _call(f"bwd_conv_a_{l}", body, (nb,),
                 [_row(tm, D), _next(tm, HALO_A, D), _row(tm, D), _prev(tm, HALO_A, D), _const((32, D))],
                 [_row(tm, D), _acc(32, D)], [_sds((S, D)), _sds((32, D))],
                 scratch=[pltpu.VMEM((tm + HALO_A, D), F32), pltpu.VMEM((HALO_A + tm, D), F32),
                          pltpu.VMEM((SUB - 1, HALO_A + tm - SUB, D), F32)], hosts=hosts)(dcv, dcv, glu, glu, wdw)


def _bwd_a1(dglu, h, dz1, w1, l):
    tm = 256

    def body(dg_ref, h_ref, dz_ref, w_ref, dx_ref, dh_ref, acc_ref):
        _zero_first(acc_ref)
        a, g = h_ref[:, :D], h_ref[:, D:]
        sg = _sig(g)
        dgl = dg_ref[...]
        da = dgl * sg
        dg = dgl * a * sg * (1.0 - sg)
        acc_ref[0:1, 0:D] += _colsum(da)
        acc_ref[0:1, D:2 * D] += _colsum(dg)
        dh_ref[:, 0:D] = da.astype(BF16)
        dh_ref[:, D:2 * D] = dg.astype(BF16)
        dx = ALPHA * dz_ref[...]
        for q in range(NQ):
            dx = dx + _mmt(dh_ref[:, q * 512:(q + 1) * 512], w_ref[q])
        dx_ref[...] = dx

    return _call(f"bwd_a1_{l}", body, (S // tm,),
                 [_row(tm, D), _row(tm, 2 * D), _row(tm, D), _wspec(w1)],
                 [_row(tm, D), _row(tm, 2 * D), _acc(8, 2 * D)],
                 [_sds((S, D)), _sds((S, 2 * D), BF16), _sds((8, 2 * D))])(dglu, h, dz1, w1)


def _bwd_c2(dz1, hc, wc, wout):
    tm = 256

    def body(dz_ref, hc_ref, hcp_ref, wc_ref, wout_ref, dy_ref, dbg_ref, mb_ref, buf):
        y = _short_conv(buf, hc_ref, hcp_ref, wc_ref, tm, pl.program_id(0))
        dzb = dz_ref[...].astype(BF16)
        dm = _mmt(dzb, wout_ref[...])
        bg = hc_ref[:, :D]
        mb_ref[...] = (bg * y).astype(BF16)
        dbg_ref[...] = (dm * y).astype(BF16)
        dy_ref[...] = dm * bg

    return _call("bwd_c2", body, (S // tm,),
                 [_row(tm, D), _row(tm, 3 * D), _prev(tm, HALO_C, 3 * D), _const((8, D)), _const(wout.shape)],
                 [_row(tm, D), _row(tm, D), _row(tm, D)],
                 [_sds((S, D)), _sds((S, D), BF16), _sds((S, D), BF16)],
                 scratch=[pltpu.VMEM((HALO_C + tm, D), F32)])(dz1, hc, hc, wc, wout)


def _bwd_c1(dy, hc, dbg, dz1, wc, win):
    tm = 256
    nb = S // tm

    def body(d_ref, dn_ref, hc_ref, hcp_ref, dbg_ref, dz_ref, wc_ref, win_ref, dx_ref, dhc_ref, dwc_ref, bufd, bufq):
        i = pl.program_id(0)
        _zero_first(dwc_ref)
        bufd[0:tm, :] = d_ref[...]
        _fill_halo(bufd, tm, HALO_C, lambda: dn_ref[...], i == nb - 1)
        _fill_halo(bufq, 0, HALO_C, lambda: hcp_ref[:, D:2 * D] * hcp_ref[:, 2 * D:], i == 0)
        bufq[HALO_C:HALO_C + tm, :] = hc_ref[:, D:2 * D] * hc_ref[:, 2 * D:]
        dq = wc_ref[0:1, :] * bufd[pl.ds(CONV_C - 1, tm), :]
        for k in range(1, CONV_C):
            dq = dq + wc_ref[k:k + 1, :] * bufd[pl.ds(CONV_C - 1 - k, tm), :]
        dv = d_ref[...]
        for k in range(CONV_C):
            dwc_ref[k:k + 1, :] += _colsum(dv * bufq[pl.ds(HALO_C - (CONV_C - 1) + k, tm), :])
        dhc_ref[:, 0:D] = dbg_ref[...]
        dhc_ref[:, D:2 * D] = (dq * hc_ref[:, 2 * D:]).astype(BF16)
        dhc_ref[:, 2 * D:3 * D] = (dq * hc_ref[:, D:2 * D]).astype(BF16)
        dx = ALPHA * dz_ref[...]
        for q in range(NQ):
            dx = dx + _mmt(dhc_ref[:, q * 768:(q + 1) * 768], win_ref[q])
        dx_ref[...] = dx

    return _call("bwd_c1", body, (nb,),
                 [_row(tm, D), _next(tm, HALO_C, D), _row(tm, 3 * D), _prev(tm, HALO_C, 3 * D), _row(tm, D),
                  _row(tm, D), _const((8, D)), _wspec(win)],
                 [_row(tm, D), _row(tm, 3 * D), _acc(8, D)],
                 [_sds((S, D)), _sds((S, 3 * D), BF16), _sds((8, D))],
                 scratch=[pltpu.VMEM((tm + HALO_C, D), F32), pltpu.VMEM((HALO_C + tm, D), F32)]
                 )(dy, dy, hc, hc, dbg, dz1, wc, win)


def _bwd_b(dz1, h, lg, lb, win, wout, ws, wst, bsx):
    tm = 128
    nb = S // tm

    def body(dz_ref, h_ref, lg_ref, lb_ref, win_ref, wout_ref, ws_ref, wst_ref, bsx_ref,
             dx_ref, dh_ref, mb_ref, acc_ref, dws_ref, dbs_ref, f_scr, dvn_scr):
        _zero_first(acc_ref, dws_ref, dbs_ref)
        lgv = lg_ref[...]
        hu, hv = h_ref[:, :E], h_ref[:, E:]
        u, tu = _gelu(hu)
        v, tv = _gelu(hv)
        vn, xhat, rstd = _ln(v, lgv, lb_ref[...])
        vnb = vn.astype(BF16)
        dzb = dz_ref[...].astype(BF16)
        dm = _mmt(dzb, wout_ref[...])
        mask, mask_t = _sgu_masks()
        for hd in range(SGU_H):
            wm = jnp.where(mask, ws_ref[hd], 0.0).astype(BF16)
            cs = slice(hd * SGU_G, (hd + 1) * SGU_G)
            for n in range(tm // SGU_T):
                rs = slice(n * SGU_T, (n + 1) * SGU_T)
                f_scr[rs, cs] = jnp.dot(wm, vnb[rs, cs], preferred_element_type=F32) + bsx_ref[hd]
        f = f_scr[...]
        mb_ref[...] = (u * f).astype(BF16)
        du = dm * f
        df = dm * u
        dfb = df.astype(BF16)
        for hd in range(SGU_H):
            wmt = jnp.where(mask_t, wst_ref[hd], 0.0).astype(BF16)
            cs = slice(hd * SGU_G, (hd + 1) * SGU_G)
            for n in range(tm // SGU_T):
                rs = slice(n * SGU_T, (n + 1) * SGU_T)
                dvn_scr[rs, cs] = jnp.dot(wmt, dfb[rs, cs], preferred_element_type=F32)
                dws_ref[hd] += lax.dot_general(dfb[rs, cs], vnb[rs, cs], (((1,), (1,)), ((), ())),
                                               preferred_element_type=F32)
                dbs_ref[hd] += df[rs, cs]
        dvn = dvn_scr[...]
        acc_ref[1:2, 0:E] += _colsum(dvn * xhat)
        acc_ref[2:3, 0:E] += _colsum(dvn)
        dv = _ln_bwd(dvn * lgv, xhat, rstd)
        dhu = du * _gelu_grad(hu, tu)
        dhv = dv * _gelu_grad(hv, tv)
        acc_ref[0:1, 0:E] += _colsum(dhu)
        acc_ref[0:1, E:2 * E] += _colsum(dhv)
        dh_ref[:, 0:E] = dhu.astype(BF16)
        dh_ref[:, E:2 * E] = dhv.astype(BF16)
        dx = ALPHA * dz_ref[...]
        for q in range(NQ):
            dx = dx + _mmt(dh_ref[:, q * 1024:(q + 1) * 1024], win_ref[q])
        dx_ref[...] = dx

        @pl.when(pl.program_id(0) == nb - 1)
        def _():
            for hd in range(SGU_H):
                dws_ref[hd] = jnp.where(mask, dws_ref[hd], 0.0)

    c3 = lambda a, b, c: pl.BlockSpec((a, b, c), lambda i: (0, 0, 0))
    return _call("bwd_b", body, (nb,),
                 [_row(tm, D), _row(tm, 2 * E), _const((1, E)), _const((1, E)), _wspec(win), _const(wout.shape),
                  _const((SGU_H, SGU_T, SGU_T)), _const((SGU_H, SGU_T, SGU_T)), _const((SGU_H, SGU_T, SGU_G))],
                 [_row(tm, D), _row(tm, 2 * E), _row(tm, E), _acc(8, 2 * E), c3(SGU_H, SGU_T, SGU_T),
                  c3(SGU_H, SGU_T, SGU_G)],
                 [_sds((S, D)), _sds((S, 2 * E), BF16), _sds((S, E), BF16), _sds((8, 2 * E)),
                  _sds((SGU_H, SGU_T, SGU_T)), _sds((SGU_H, SGU_T, SGU_G))],
                 scratch=[pltpu.VMEM((tm, E), F32), pltpu.VMEM((tm, E), F32)]
                 )(dz1, h, lg, lb, win, wout, ws, wst, bsx)


def _mm_tn(name, a, amode, b, bmode, k, n, groups=NQ):
    ts = min(512, S)

    def spec(mode, w):
        if mode == "1":
            return pl.BlockSpec((ts, w), lambda s: (s, 0))
        if mode == "c":
            return pl.BlockSpec((ts, groups * w), lambda s: (s, 0))
        return pl.BlockSpec((groups, ts, w), lambda s: (0, s, 0))

    def pick(ref, mode, w, g):
        if mode == "1":
            return ref[...]
        if mode == "c":
            return ref[:, g * w:(g + 1) * w]
        return ref[g]

    def body(a_ref, b_ref, o_ref):
        _zero_first(o_ref)
        a_t = jnp.transpose(a_ref[...].astype(BF16)) if amode == "1" else None
        b_1 = b_ref[...].astype(BF16) if bmode == "1" else None
        for g in range(groups):
            lhs = a_t if amode == "1" else jnp.transpose(pick(a_ref, amode, k, g).astype(BF16))
            rhs = b_1 if bmode == "1" else pick(b_ref, bmode, n, g).astype(BF16)
            o_ref[0, g] += jnp.dot(lhs, rhs, preferred_element_type=F32)

    return _call(name, body, (S // ts,), [spec(amode, k), spec(bmode, n)],
                 pl.BlockSpec((1, groups, k, n), lambda s: (0, 0, 0, 0)), _sds((1, groups, k, n)))(a, b)


def _row_block(k, cap=256):
    return max(t for t in range(16, min(k, cap) + 1, 16) if k % t == 0)


def _cast_bf16(w):
    nl, k, n = w.shape
    tb = _row_block(k, 512)

    def body(w_ref, o_ref):
        o_ref[...] = w_ref[...].astype(BF16)

    spec = pl.BlockSpec((None, tb, n), lambda l, i: (l, i, 0))
    return _call("cast_bf16", body, (nl, k // tb), [spec], spec, _sds(w.shape, BF16))(w)


def _adam(name, w, m, v, gc, l, prev):
    nl, k, n = w.shape
    nc = gc.shape[0]
    tb = _row_block(k)

    def body(w_ref, m_ref, v_ref, g_ref, *rest):
        go_ref, d_ref, mo_ref, vo_ref = rest[-4:]
        g = g_ref[0].astype(F32)
        for c in range(1, nc):
            g = g + g_ref[c].astype(F32)
        m2 = ADAM_B1 * m_ref[...] + (1.0 - ADAM_B1) * g
        v2 = ADAM_B2 * v_ref[...] + (1.0 - ADAM_B2) * (g * g)
        m_hat = m2 / (1.0 - ADAM_B1 ** ADAM_STEP)
        v_hat = v2 / (1.0 - ADAM_B2 ** ADAM_STEP)
        go_ref[...] = g
        d_ref[...] = -ADAM_LR * (m_hat / (jnp.sqrt(v_hat) + ADAM_EPS) + ADAM_WD * w_ref[...])
        mo_ref[...] = m2
        vo_ref[...] = v2

    spec = pl.BlockSpec((None, tb, n), lambda i: (l, i, 0))
    gspec = pl.BlockSpec((nc, None, tb, n), lambda i: (0, 0, i, 0))
    in_specs, args, aliases = [spec, spec, spec, gspec], [w, m, v, gc], {}
    if prev is not None:
        in_specs += [pl.BlockSpec(memory_space=pl.ANY)] * 4
        args += list(prev)
        aliases = {4 + j: j for j in range(4)}
    return _call(name, body, (k // tb,), in_specs, [spec] * 4, [_sds(w.shape)] * 4, aliases=aliases)(*args)


def _sum8(name, g8):
    r = g8.shape[1]

    def body(g_ref, o_ref):
        acc = g_ref[0]
        for d in range(1, 8):
            acc = acc + g_ref[d]
        o_ref[...] = acc

    return _call(name, body, (1,), [pl.BlockSpec((8, r, 128), lambda i: (0, 0, 0))],
                 pl.BlockSpec((r, 128), lambda i: (0, 0)), _sds((r, 128)))(g8)


def _place():
    x, y, c = lax.axis_index("x"), lax.axis_index("y"), lax.axis_index("c")
    return x, y, c, 2 * x + y, (x, y, 1 - c), [(1 - x, y), (x, 1 - y), (1 - x, 1 - y)]


class _Exchange:
    def __init__(self, arrays, out_shapes):
        self.arrays, self.out_shapes = list(arrays), list(out_shapes)
        n = len(self.arrays)
        self.sems = [pltpu.SemaphoreType.DMA((7 * n,)), pltpu.SemaphoreType.DMA((7 * n,)),
                     pltpu.SemaphoreType.DMA((n,))]

    def _copies(self, ins, outs, sems):
        send, recv, lsem = sems
        local_src, remote_src, dst = self.maps(ins, outs)
        x, y, c, q, sib, chips = _place()

        def rcopy(w, k, qq, cc, to, src=None):
            return pltpu.make_async_remote_copy(
                src_ref=dst(w, qq, cc) if src is None else src, dst_ref=dst(w, qq, cc),
                send_sem=send.at[7 * w + k], recv_sem=recv.at[7 * w + k], device_id=to, device_id_type=MESH)

        def mine(w):
            return pltpu.make_async_copy(local_src(w), dst(w, q, c), lsem.at[w])

        def first(w):
            return [rcopy(w, 0, q, c, sib, local_src(w))] + [
                rcopy(w, 1 + j, q, c, (cx, cy, c), remote_src(w, 2 * cx + cy)) for j, (cx, cy) in enumerate(chips)]

        return rcopy, mine, first, (x, y, c), q, c, sib, chips

    def start(self, ins, outs, sems):
        _, mine, first, *_ = self._copies(ins, outs, sems)
        for w in range(len(self.arrays)):
            mine(w).start()
            for cp in first(w):
                cp.start()

    def finish(self, ins, outs, sems):
        rcopy, mine, first, me, q, c, sib, chips = self._copies(ins, outs, sems)
        n = len(self.arrays)
        for w in range(n):
            for j, (cx, cy) in enumerate(chips):
                rcopy(w, 1 + j, 2 * cx + cy, c, me).wait_recv()
                rcopy(w, 4 + j, 2 * cx + cy, c, sib).start()
        for w in range(n):
            rcopy(w, 0, q, 1 - c, me).wait_recv()
            for j, (cx, cy) in enumerate(chips):
                rcopy(w, 4 + j, 2 * cx + cy, 1 - c, me).wait_recv()
        for w in range(n):
            for cp in first(w):
                cp.wait_send()
            for j, (cx, cy) in enumerate(chips):
                rcopy(w, 4 + j, 2 * cx + cy, c, sib).wait_send()
            mine(w).wait()


class _GatherWeights(_Exchange):
    def __init__(self, items):
        self.layers = [l for _, l in items]
        self.kh = [s.shape[1] // 2 for s, _ in items]
        super().__init__([s for s, _ in items], [_sds((NQ, 1) + s.shape[1:], BF16) for s, _ in items])

    def maps(self, ins, outs):
        c = lax.axis_index("c")
        src = lambda w: ins[w].at[pl.ds(self.layers[w], 1), pl.ds(c * self.kh[w], self.kh[w]), :]
        return src, lambda w, q: src(w), lambda w, q, cc: outs[w].at[q, :, pl.ds(cc * self.kh[w], self.kh[w]), :]


class _ScatterPartials(_Exchange):
    def __init__(self, parts):
        super().__init__(parts, [_sds((NQ, 1, 2) + p.shape[2:], BF16) for p in parts])

    def maps(self, ins, outs):
        q = 2 * lax.axis_index("x") + lax.axis_index("y")
        return (lambda w: ins[w].at[:, q]), (lambda w, qq: ins[w].at[:, qq]), (lambda w, qq, cc: outs[w].at[qq, :, cc])


class _Gather8(_Exchange):
    def __init__(self, v):
        super().__init__([v], [_sds((8,) + v.shape)])

    def maps(self, ins, outs):
        return (lambda w: ins[0]), (lambda w, q: ins[0]), (lambda w, q, cc: outs[0].at[2 * q + cc])


class _SwapHalves:
    def __init__(self, dws):
        self.arrays = list(dws)
        self.kh = [d.shape[2] // 2 for d in dws]
        self.out_shapes = [_sds(d.shape[:2] + (kh,) + d.shape[3:]) for d, kh in zip(dws, self.kh)]
        self.sems = [pltpu.SemaphoreType.DMA((len(dws),)), pltpu.SemaphoreType.DMA((len(dws),))]

    def _copies(self, ins, outs, sems):
        send, recv = sems
        _, _, c, _, sib, _ = _place()
        return [pltpu.make_async_remote_copy(
            src_ref=ins[w].at[:, :, pl.ds((1 - c) * self.kh[w], self.kh[w]), :], dst_ref=outs[w],
            send_sem=send.at[w], recv_sem=recv.at[w], device_id=sib, device_id_type=MESH)
            for w in range(len(self.arrays))]

    def start(self, ins, outs, sems):
        for cp in self._copies(ins, outs, sems):
            cp.start()

    def finish(self, ins, outs, sems):
        for cp in self._copies(ins, outs, sems):
            cp.wait()


def _comm_only(name, host):
    n_in, n_out = len(host.arrays), len(host.out_shapes)

    def body(*refs):
        ins, outs, sems = refs[:n_in], refs[n_in:n_in + n_out], refs[n_in + n_out:]
        host.start(ins, outs, sems)
        host.finish(ins, outs, sems)

    any_spec = pl.BlockSpec(memory_space=pl.ANY)
    return pl.pallas_call(body, name=name, in_specs=[any_spec] * n_in, out_specs=[any_spec] * n_out,
                          out_shape=host.out_shapes, scratch_shapes=host.sems)(*host.arrays)


def _add_halves(dw, got, cidx):
    nl, _, k, n = dw.shape
    kh = k // 2

    def body(c_ref, a_ref, b_ref, o_ref):
        o_ref[...] = (a_ref[...] + b_ref[...]).astype(BF16)

    grid_spec = pltpu.PrefetchScalarGridSpec(
        num_scalar_prefetch=1, grid=(nl, NQ),
        in_specs=[pl.BlockSpec((None, None, None, kh, n), lambda l, q, c_ref: (l, q, c_ref[0], 0, 0)),
                  pl.BlockSpec((None, None, kh, n), lambda l, q, c_ref: (l, q, 0, 0))],
        out_specs=pl.BlockSpec((None, None, kh, n), lambda l, q, c_ref: (l, q, 0, 0)))
    return pl.pallas_call(
        body, name="add_halves", grid_spec=grid_spec, out_shape=_sds((nl, NQ, kh, n), BF16),
        compiler_params=pltpu.CompilerParams(dimension_semantics=("arbitrary", "arbitrary"),
                                             vmem_limit_bytes=VMEM_LIMIT))(cidx, dw.reshape(nl, NQ, 2, kh, n), got)


def _gather8(name, v):
    return _comm_only(name, _Gather8(v))[0]


PACK = 16 * 128


def _pack(arrays):
    parts = []
    for a in arrays:
        flat = a.reshape(-1)
        parts.append(jnp.pad(flat, (0, (-flat.shape[0]) % PACK)))
    return jnp.concatenate(parts).reshape(-1, 128)


def _unpack(packed, shapes):
    flat = packed.reshape(-1)
    out, off = [], 0
    for shp in shapes:
        size = 1
        for d in shp:
            size *= d
        out.append(flat[off:off + size].reshape(shp))
        off += size + (-size) % PACK
    return out


def kernel(x, p, a_w_pw1, a_b_pw1, a_w_dw, a_b_dw, a_ln_g, a_ln_b, a_w_pw2, b_w_in, b_b_in, b_ln_g, b_ln_b, b_w_s, b_b_s, b_w_out, c_w_in, c_w_conv, c_w_out, ln1_g, ln1_b, ln2_g, ln2_b, ffn_w_gate, ffn_w_up, ffn_w_down, ple_w_gate, ple_w_proj, ple_norm_g, loss_target, m_a_w_pw1, m_a_b_pw1, m_a_w_dw, m_a_b_dw, m_a_ln_g, m_a_ln_b, m_a_w_pw2, m_b_w_in, m_b_b_in, m_b_ln_g, m_b_ln_b, m_b_w_s, m_b_b_s, m_b_w_out, m_c_w_in, m_c_w_conv, m_c_w_out, m_ln1_g, m_ln1_b, m_ln2_g, m_ln2_b, m_ffn_w_gate, m_ffn_w_up, m_ffn_w_down, m_ple_w_gate, m_ple_w_proj, m_ple_norm_g, v_a_w_pw1, v_a_b_pw1, v_a_w_dw, v_a_b_dw, v_a_ln_g, v_a_ln_b, v_a_w_pw2, v_b_w_in, v_b_b_in, v_b_ln_g, v_b_ln_b, v_b_w_s, v_b_b_s, v_b_w_out, v_c_w_in, v_c_w_conv, v_c_w_out, v_ln1_g, v_ln1_b, v_ln2_g, v_ln2_b, v_ffn_w_gate, v_ffn_w_up, v_ffn_w_down, v_ple_w_gate, v_ple_w_proj, v_ple_norm_g):
    args = dict(locals())
    wts = {k: args[k] for k in WEIGHTS}
    mom = {k: args["m_" + k] for k in WEIGHTS}
    var = {k: args["v_" + k] for k in WEIGHTS}
    for k in TRANSPOSED:
        wts[k], mom[k], var[k] = (jnp.transpose(t[k], (0, 2, 1)) for t in (wts, mom, var))
    q_idx = 2 * lax.axis_index("x") + lax.axis_index("y")
    c_idx = lax.axis_index("c").astype(jnp.int32).reshape(1)

    wb = {k: _cast_bf16(wts[k]) for k in BIG}
    mixw = [[("a_w_pw1", 0), ("a_w_pw2", 0)], [("b_w_in", 0), ("b_w_out", 0)], [("c_w_in", 0), ("c_w_out", 0)],
            [("a_w_pw1", 1), ("a_w_pw2", 1)]]
    ffnw = [[("ffn_w_gate", l), ("ffn_w_up", l), ("ffn_w_down", l)] for l in range(DEPTH)]
    plew = [[("ple_w_gate", l), ("ple_w_proj", l)] for l in range(DEPTH)]
    fwd_plan = {("a2", 0): ffnw[0], ("ffn", 0): mixw[1], ("ple", 0): plew[1],
                ("b", 1): ffnw[1], ("ffn", 1): mixw[2], ("ple", 1): plew[2],
                ("c1", 2): ffnw[2][:2], ("c2", 2): ffnw[2][2:], ("ffn", 2): mixw[3], ("ple", 2): plew[3],
                ("a2", 3): ffnw[3]}
    gw = {}

    def gather(keys):
        return _GatherWeights([(wb[name], l) for name, l in keys])

    def hosted(tag, fn, *fargs):
        keys = fwd_plan.get(tag)
        if not keys:
            return fn(*fargs)
        own, (got,) = fn(*fargs, hosts=[gather(keys)])
        store(keys, got)
        return own

    def store(keys, got):
        for (name, l), arr in zip(keys, got):
            gw[name, l] = arr.reshape(NQ * arr.shape[2], arr.shape[3]) if name in ROW_SHARDED else arr

    first_keys = mixw[0][:1]
    fwd_plan["a1", 0] = mixw[0][1:] + plew[0]
    store(first_keys, _comm_only("gather_first", gather(first_keys)))
    shard_shapes = [wts[k].shape for k in SMALL_SHARDED]
    small8 = _gather8("gather_small", _pack([wts[k] for k in SMALL_SHARDED]))
    per_chip = [_unpack(small8[2 * qq], shard_shapes) for qq in range(NQ)]
    full = {k: jnp.concatenate([per_chip[qq][i] for qq in range(NQ)], axis=-1) for i, k in enumerate(SMALL_SHARDED)}
    for k in SMALL_REPL:
        full[k] = wts[k]

    def vec(name, l):
        return full[name][l][None, :]

    def conv_w(name, l, rows):
        w = full[name][l]
        return jnp.pad(w, ((0, rows - w.shape[0]), (0, 0)))

    ws = full["b_w_s"][0]
    wst = jnp.transpose(ws, (0, 2, 1))
    bsx = jnp.broadcast_to(full["b_b_s"][0][:, :, None], (SGU_H, SGU_T, SGU_G))

    x0s, z1s, z2s, saved, ffn_saved = [], [], [], [], []
    cur = x[0]
    for i in range(DEPTH):
        mix, j = i % 3, i // 3
        x0s.append(cur)
        if mix == 0:
            h, glu = hosted(("a1", i), _fwd_a1, cur, gw["a_w_pw1", j], vec("a_b_pw1", j), i)
            z1, cv = hosted(("a2", i), _fwd_a2, glu, cur, conv_w("a_w_dw", j, 32), vec("a_b_dw", j), vec("a_ln_g", j),
                            vec("a_ln_b", j), gw["a_w_pw2", j], i)
            saved.append((h, glu, cv))
        elif mix == 1:
            z1, h = hosted(("b", i), _fwd_b, cur, gw["b_w_in", 0], vec("b_b_in", 0), vec("b_ln_g", 0),
                           vec("b_ln_b", 0), ws, bsx, gw["b_w_out", 0])
            saved.append((h,))
        else:
            hc = hosted(("c1", i), _fwd_c1, cur, gw["c_w_in", 0])
            z1 = hosted(("c2", i), _fwd_c2, hc, cur, conv_w("c_w_conv", 0, 8), gw["c_w_out", 0])
            saved.append((hc,))
        z2, ab, ub, hm = hosted(("ffn", i), _fwd_ffn, z1, vec("ln1_g", i), vec("ln1_b", i), gw["ffn_w_gate", i],
                                gw["ffn_w_up", i], gw["ffn_w_down", i], i)
        ffn_saved.append((ab, ub, hm))
        cur = hosted(("ple", i), _fwd_ple, z2, p[i, 0], vec("ln2_g", i), vec("ln2_b", i), gw["ple_w_gate", i],
                     gw["ple_w_proj", i], vec("ple_norm_g", i), i)
        z1s.append(z1)
        z2s.append(z2)

    g, loss_acc = _loss_head(cur, loss_target[0])
    loss = lax.psum(0.5 / D * jnp.sum(loss_acc[0]), ("x", "y", "c"))

    dws = {}
    sg = {}
    res = {k: None for k in BIG}

    def wgrad(name, l, a, amode, b, bmode):
        _, k, n = wts[name].shape
        if name in ROW_SHARDED:
            dws[name, l] = _mm_tn(f"dw_{name}_{l}", a, "1", b, "1", NQ * k, n, groups=1).reshape(1, NQ, k, n)
        else:
            dws[name, l] = _mm_tn(f"dw_{name}_{l}", a, amode, b, bmode, k, n)

    def swap(keys):
        return _SwapHalves([dws[k] for k in keys])

    def add_halves(keys, got):
        return [_add_halves(dws[k], r, c_idx) for k, r in zip(keys, got)]

    def update(keys, contribs):
        for (name, l), gc in zip(keys, contribs):
            _, kq, n = wts[name].shape
            res[name] = _adam(f"adam_{name}_{l}", wts[name], mom[name], var[name], gc.reshape(NQ, 1, kq, n), l,
                              res[name])

    small = SMALL_SHARDED + SMALL_REPL
    late_small = [("a_w_dw", 0), ("a_b_pw1", 0)]
    early_small = [(k, l) for k in small for l in range(full[k].shape[0]) if (k, l) not in late_small]
    pending = None
    for i in reversed(range(DEPTH)):
        mix, j = i % 3, i // 3
        ple_args = (g, z2s[i], p[i, 0], vec("ln2_g", i), vec("ln2_b", i), gw["ple_w_gate", i], gw["ple_w_proj", i],
                    vec("ple_norm_g", i), i)
        if pending:
            (dz2, x2b, dgp, dqp, acc), (got,) = _bwd_ple(*ple_args, hosts=[swap(pending)])
            parts = add_halves(pending, got)
        else:
            dz2, x2b, dgp, dqp, acc = _bwd_ple(*ple_args)
        sg["ple_norm_g", i], sg["ln2_g", i], sg["ln2_b", i] = acc[0], acc[1], acc[2]
        wgrad("ple_w_gate", i, x2b, "c", dgp, "1")
        wgrad("ple_w_proj", i, p[i, 0], "1", dqp, "c")
        ab, ub, hm = ffn_saved[i]
        ffn_args = (dz2, z1s[i], ab, ub, vec("ln1_g", i), vec("ln1_b", i), gw["ffn_w_gate", i], gw["ffn_w_up", i],
                    gw["ffn_w_down", i], i)
        if pending:
            (dz1, x1b, da, du, acc), (contribs,) = _bwd_ffn(*ffn_args, hosts=[_ScatterPartials(parts)])
            update(pending, contribs)
        else:
            dz1, x1b, da, du, acc = _bwd_ffn(*ffn_args)
        sg["ln1_g", i], sg["ln1_b", i] = acc[0], acc[1]
        wgrad("ffn_w_gate", i, da, "1", x1b, "1")
        wgrad("ffn_w_up", i, du, "1", x1b, "1")
        wgrad("ffn_w_down", i, hm, "1", dz2, "1")
        x0 = x0s[i]
        if mix == 0:
            h, glu, cv = saved[i]
            a2_args = (dz1, cv, vec("a_ln_g", j), vec("a_ln_b", j), gw["a_w_pw2", j], i)
            conv_args = (glu, conv_w("a_w_dw", j, 32), i)
            if i == 0:
                early = ffnw[0] + plew[0]
                (dcv, sb, acc), (got,) = _bwd_a2(*a2_args, hosts=[swap(early)])
                sg["a_ln_g", j], sg["a_ln_b", j], sg["a_b_dw", j] = acc[0], acc[1], acc[2]
                parts = add_halves(early, got)
                (dglu, dwdw), (contribs, (g8_early,)) = _bwd_conv_a(
                    dcv, *conv_args, hosts=[_ScatterPartials(parts), _Gather8(_pack([sg[pc] for pc in early_small]))])
                update(early, contribs)
            else:
                dcv, sb, acc = _bwd_a2(*a2_args)
                sg["a_ln_g", j], sg["a_ln_b", j], sg["a_b_dw", j] = acc[0], acc[1], acc[2]
                dglu, dwdw = _bwd_conv_a(dcv, *conv_args)
            wgrad("a_w_pw2", j, sb, "c", dz1, "1")
            sg["a_w_dw", j] = dwdw[:CONV_A]
            g, dh, acc = _bwd_a1(dglu, h, dz1, gw["a_w_pw1", j], i)
            sg["a_b_pw1", j] = acc[0]
            wgrad("a_w_pw1", j, x0, "1", dh, "c")
        elif mix == 1:
            (h,) = saved[i]
            g, dh, mb, acc, dw_s, db_s = _bwd_b(dz1, h, vec("b_ln_g", 0), vec("b_ln_b", 0), gw["b_w_in", 0],
                                                gw["b_w_out", 0], ws, wst, bsx)
            sg["b_b_in", 0], sg["b_ln_g", 0], sg["b_ln_b", 0] = acc[0], acc[1, :E], acc[2, :E]
            sg["b_w_s", 0], sg["b_b_s", 0] = dw_s, jnp.sum(db_s, axis=-1)
            wgrad("b_w_out", 0, mb, "c", dz1, "1")
            wgrad("b_w_in", 0, x0, "1", dh, "c")
        else:
            (hc,) = saved[i]
            wc = conv_w("c_w_conv", 0, 8)
            dy, dbg, mb = _bwd_c2(dz1, hc, wc, gw["c_w_out", 0])
            wgrad("c_w_out", 0, mb, "c", dz1, "1")
            g, dhc, dwc = _bwd_c1(dy, hc, dbg, dz1, wc, gw["c_w_in", 0])
            sg["c_w_conv", 0] = dwc[:CONV_C]
            wgrad("c_w_in", 0, x0, "1", dhc, "c")
        pending = mixw[i] + ffnw[i] + plew[i] if i > 0 else mixw[0]
    grad_x = g[None]
    parts = add_halves(pending, _comm_only("swap_last", swap(pending)))
    update(pending, _comm_only("scatter_last", _ScatterPartials(parts)))

    g8_late = _gather8("gather_small_late", _pack([sg[pc] for pc in late_small]))
    sums = dict(zip(early_small, _unpack(_sum8("sum8_early", g8_early), [sg[pc].shape for pc in early_small])))
    sums.update(zip(late_small, _unpack(_sum8("sum8_late", g8_late), [sg[pc].shape for pc in late_small])))
    gsum = [jnp.stack([sums[k, l] for l in range(full[k].shape[0])]) for k in small]
    gmine = []
    for k, gs in zip(small, gsum):
        if k in SMALL_SHARDED:
            wdt = wts[k].shape[-1]
            gs = lax.dynamic_slice_in_dim(gs, q_idx * wdt, wdt, axis=gs.ndim - 1)
        gmine.append(gs)
    packed = [_pack(t)[None] for t in ([wts[k] for k in small], [mom[k] for k in small], [var[k] for k in small])]
    outs = _adam("adam_small", packed[0], packed[1], packed[2], _pack(gmine)[None, None], 0, None)
    unpacked = [_unpack(o[0], [wts[k].shape for k in small]) for o in outs]
    for i, k in enumerate(small):
        res[k] = tuple(u[i] for u in unpacked)

    for k in TRANSPOSED:
        res[k] = tuple(jnp.transpose(r, (0, 2, 1)) for r in res[k])
    return (loss, grad_x, *[res[k][0] for k in WEIGHTS], *[res[k][1] for k in WEIGHTS],
            *[res[k][2] for k in WEIGHTS], *[res[k][3] for k in WEIGHTS])
```

```python
import jax
import jax.numpy as jnp
from jax import lax
from jax.experimental import pallas as pl
from jax.experimental.pallas import tpu as pltpu

F32, BF16 = jnp.float32, jnp.bfloat16
S = 4096
D = 1024
E = 2048
FF = 2816
FQ = FF // 4
NQ = 4
DEPTH = 4
ALPHA = (2 * DEPTH) ** 0.25
LN_EPS = 1e-5
CONV_A, CONV_C = 31, 3
HALO_A, HALO_C = 32, 8
SGU_T, SGU_H, SGU_G, SGU_CHUNK = 128, 8, 256, 64
VMEM_LIMIT = 56 * 1024 * 1024
DW_BLOCK_BUDGET = 40 * 1024 * 1024
MESH = pl.DeviceIdType.MESH
ADAM_LR, ADAM_B1, ADAM_B2, ADAM_EPS, ADAM_WD, ADAM_STEP = 0.001, 0.9, 0.999, 1e-08, 0.01, 10
GELU_C, GELU_A = 0.7978845608028654, 0.044715

BIG = ["a_w_pw1", "a_w_pw2", "b_w_in", "b_w_out", "c_w_in", "c_w_out",
       "ffn_w_gate", "ffn_w_up", "ffn_w_down", "ple_w_gate", "ple_w_proj"]
TRANSPOSED = ["ffn_w_gate", "ffn_w_up"]
ROW_SHARDED = ["a_w_pw2", "b_w_out", "c_w_out", "ffn_w_gate", "ffn_w_up", "ffn_w_down", "ple_w_gate"]
SMALL_SHARDED = ["a_b_pw1", "a_w_dw", "a_b_dw", "a_ln_g", "a_ln_b", "c_w_conv"]
SMALL_REPL = ["b_b_in", "b_ln_g", "b_ln_b", "b_w_s", "b_b_s", "ln1_g", "ln1_b", "ln2_g", "ln2_b", "ple_norm_g"]
WEIGHTS = ["a_w_pw1", "a_b_pw1", "a_w_dw", "a_b_dw", "a_ln_g", "a_ln_b", "a_w_pw2", "b_w_in", "b_b_in", "b_ln_g",
           "b_ln_b", "b_w_s", "b_b_s", "b_w_out", "c_w_in", "c_w_conv", "c_w_out", "ln1_g", "ln1_b", "ln2_g",
           "ln2_b", "ffn_w_gate", "ffn_w_up", "ffn_w_down", "ple_w_gate", "ple_w_proj", "ple_norm_g"]


def _call(name, body, grid, in_specs, out_specs, out_shape, scratch=(), aliases=None, hosts=()):
    params = pltpu.CompilerParams(dimension_semantics=("arbitrary",) * len(grid), vmem_limit_bytes=VMEM_LIMIT)
    if not hosts:
        return pl.pallas_call(
            body, name=name, grid=grid, in_specs=in_specs, out_specs=out_specs, out_shape=out_shape,
            scratch_shapes=list(scratch), input_output_aliases=aliases or {}, compiler_params=params)
    assert len(grid) == 1 and not aliases
    single = not isinstance(out_shape, (list, tuple))
    own_shapes = [out_shape] if single else list(out_shape)
    own_specs = [out_specs] if single else list(out_specs)
    n_in, n_out, n_scr = len(in_specs), len(own_shapes), len(scratch)
    h_in = [len(h.arrays) for h in hosts]
    h_out = [len(h.out_shapes) for h in hosts]
    h_sem = [len(h.sems) for h in hosts]

    def split(refs, counts):
        out, off = [], 0
        for cnt in counts:
            out.append(refs[off:off + cnt])
            off += cnt
        return out

    def wrapped(*refs):
        ins, hin, outs, hout, scr, hsem = split(refs, [n_in, sum(h_in), n_out, sum(h_out), n_scr, sum(h_sem)])
        per_host = list(zip(hosts, split(hin, h_in), split(hout, h_out), split(hsem, h_sem)))

        @pl.when(pl.program_id(0) == 0)
        def _():
            for h, a, o, s in per_host:
                h.start(a, o, s)

        body(*ins, *outs, *scr)

        @pl.when(pl.program_id(0) == grid[0] - 1)
        def _():
            for h, a, o, s in per_host:
                h.finish(a, o, s)

    any_spec = pl.BlockSpec(memory_space=pl.ANY)
    call = pl.pallas_call(
        wrapped, name=name, grid=grid, in_specs=list(in_specs) + [any_spec] * sum(h_in),
        out_specs=own_specs + [any_spec] * sum(h_out),
        out_shape=own_shapes + [s for h in hosts for s in h.out_shapes],
        scratch_shapes=list(scratch) + [s for h in hosts for s in h.sems], compiler_params=params)

    def run(*args):
        res = call(*args, *[a for h in hosts for a in h.arrays])
        own = res[0] if single else list(res[:n_out])
        return own, split(list(res[n_out:]), h_out)

    return run


def _sds(shape, dtype=F32):
    return jax.ShapeDtypeStruct(shape, dtype)


def _row(tm, c):
    return pl.BlockSpec((tm, c), lambda i: (i, 0))


def _grow(g, tm, c):
    return pl.BlockSpec((g, tm, c), lambda i: (0, i, 0))


def _const(shape):
    nd = len(shape)
    return pl.BlockSpec(shape, lambda i: (0,) * nd, pipeline_mode=pl.Buffered(1))


def _wspec(w):
    return pl.BlockSpec((NQ, None, w.shape[2], w.shape[3]), lambda i: (0, 0, 0, 0), pipeline_mode=pl.Buffered(1))


def _prev(tm, hb, c):
    return pl.BlockSpec((hb, c), lambda i: (jnp.maximum(i * (tm // hb) - 1, 0), 0))


def _next(tm, hb, c):
    return pl.BlockSpec((hb, c), lambda i: (jnp.minimum((i + 1) * (tm // hb), S // hb - 1), 0))


def _acc(r, c):
    return pl.BlockSpec((r, c), lambda i: (0, 0))


def _sig(x):
    return 1.0 / (1.0 + jnp.exp(-x))


def _ln(z, g, b):
    mu = jnp.mean(z, axis=-1, keepdims=True)
    zc = z - mu
    rstd = lax.rsqrt(jnp.mean(zc * zc, axis=-1, keepdims=True) + LN_EPS)
    xhat = zc * rstd
    return xhat * g + b, xhat, rstd


def _ln_bwd(dyg, xhat, rstd):
    return rstd * (dyg - jnp.mean(dyg, axis=-1, keepdims=True) - xhat * jnp.mean(dyg * xhat, axis=-1, keepdims=True))


def _mm(a, w):
    return jnp.dot(a.astype(BF16), w, preferred_element_type=F32)


def _mmt(a, w):
    return lax.dot_general(a.astype(BF16), w, (((1,), (1,)), ((), ())), preferred_element_type=F32)


def _colsum(x):
    return jnp.sum(x, axis=0, keepdims=True)


def _gelu(x):
    t = jnp.tanh(GELU_C * (x + GELU_A * x * x * x))
    return 0.5 * x * (1.0 + t), t


def _gelu_grad(x, t):
    return 0.5 * (1.0 + t) + 0.5 * x * (1.0 - t * t) * GELU_C * (1.0 + 3.0 * GELU_A * x * x)


def _silu_grad(a, sg):
    return sg * (1.0 + a * (1.0 - sg))


def _sgu_masks():
    r = lax.broadcasted_iota(jnp.int32, (SGU_T, SGU_T), 0) // SGU_CHUNK
    c = lax.broadcasted_iota(jnp.int32, (SGU_T, SGU_T), 1) // SGU_CHUNK
    return r >= c, c >= r


def _fill_halo(buf, lo, n, halo_val_fn, is_edge):
    @pl.when(is_edge)
    def _():
        buf[lo:lo + n, :] = jnp.zeros((n, buf.shape[1]), F32)

    @pl.when(jnp.logical_not(is_edge))
    def _():
        buf[lo:lo + n, :] = halo_val_fn()


SUB, LANE = 8, 128
ROWS_AT_ONCE = 16


def _shift_copies(buf, sh):
    rows = sh.shape[1]
    for s in range(1, SUB):
        sh[s - 1, :, :] = buf[pl.ds(s, rows), :]


def _tiles(buf, sh, s, first, count, group0, lanes):
    src = buf if s == 0 else sh.at[s - 1]
    return {t: src[pl.ds(pl.multiple_of((group0 + t) * SUB, SUB), SUB), lanes] for t in range(first, first + count)}


def _by_shift(offsets):
    out = []
    for s in range(SUB):
        taps = [(k, o // SUB) for k, o in enumerate(offsets) if o % SUB == s]
        if taps:
            out.append((s, taps))
    return out


def _conv_rows(out_ref, w_ref, bias_ref, offsets, buf, sh, tm):
    n = ROWS_AT_ONCE
    for cb in range(D // LANE):
        lanes = slice(cb * LANE, (cb + 1) * LANE)
        bias = None if bias_ref is None else jnp.broadcast_to(bias_ref[:, lanes], (SUB, LANE))

        def body(jb, carry):
            accs = [bias] * n
            for s, taps in _by_shift(offsets):
                ms = [m for _, m in taps]
                tiles = _tiles(buf, sh, s, min(ms), max(ms) - min(ms) + n, jb * n, lanes)
                for k, m in taps:
                    wk = jnp.broadcast_to(w_ref[k:k + 1, lanes], (SUB, LANE))
                    for jj in range(n):
                        t = wk * tiles[m + jj]
                        accs[jj] = t if accs[jj] is None else accs[jj] + t
            for jj in range(n):
                out_ref[pl.ds(pl.multiple_of((jb * n + jj) * SUB, SUB), SUB), lanes] = accs[jj]
            return carry

        lax.fori_loop(0, tm // (SUB * n), body, 0)


def _conv_wgrad(dw_ref, d_ref, offsets, buf, sh, tm):
    n = 4
    for cb in range(D // LANE):
        lanes = slice(cb * LANE, (cb + 1) * LANE)

        def body(jq, accs):
            accs = list(accs)
            d = [d_ref[pl.ds(pl.multiple_of((jq * n + jj) * SUB, SUB), SUB), lanes] for jj in range(n)]
            for s, taps in _by_shift(offsets):
                ms = [m for _, m in taps]
                tiles = _tiles(buf, sh, s, min(ms), max(ms) - min(ms) + n, jq * n, lanes)
                for k, m in taps:
                    for jj in range(n):
                        accs[k] = accs[k] + d[jj] * tiles[m + jj]
            return tuple(accs)

        accs = lax.fori_loop(0, tm // (SUB * n), body, tuple(jnp.zeros((SUB, LANE), F32) for _ in offsets))
        for k, acc in enumerate(accs):
            dw_ref[k:k + 1, lanes] += jnp.sum(acc, axis=0, keepdims=True)


def _fwd_a1(x0, w1, b1, l, hosts=()):
    tm = 512

    def body(x_ref, w_ref, b_ref, h_ref, glu_ref):
        xb = x_ref[...].astype(BF16)
        for q in range(NQ):
            sl = slice(q * 512, (q + 1) * 512)
            h_ref[:, sl] = jnp.dot(xb, w_ref[q], preferred_element_type=F32) + b_ref[:, sl]
        glu_ref[...] = h_ref[:, :D] * _sig(h_ref[:, D:])

    return _call(f"fwd_a1_{l}", body, (S // tm,), [_row(tm, D), _wspec(w1), _const((1, 2 * D))],
                 [_row(tm, 2 * D), _row(tm, D)], [_sds((S, 2 * D)), _sds((S, D))], hosts=hosts)(x0, w1, b1)


def _fwd_a2(glu, x0, wdw, bdw, lg, lb, w2, l, hosts=()):
    tm = 256

    def body(g_ref, gp_ref, x_ref, wdw_ref, bdw_ref, lg_ref, lb_ref, w2_ref, z_ref, cv_ref, buf, sh):
        i = pl.program_id(0)
        _fill_halo(buf, 0, HALO_A, lambda: gp_ref[...], i == 0)
        buf[HALO_A:HALO_A + tm, :] = g_ref[...]
        _shift_copies(buf, sh)
        _conv_rows(cv_ref, wdw_ref, bdw_ref, [HALO_A - (CONV_A - 1) + k for k in range(CONV_A)], buf, sh, tm)
        n, _, _ = _ln(cv_ref[...], lg_ref[...], lb_ref[...])
        sb = (n * _sig(n)).astype(BF16)
        z_ref[...] = ALPHA * x_ref[...] + jnp.dot(sb, w2_ref[...], preferred_element_type=F32)

    return _call(f"fwd_a2_{l}", body, (S // tm,),
                 [_row(tm, D), _prev(tm, HALO_A, D), _row(tm, D), _const((32, D)), _const((1, D)), _const((1, D)),
                  _const((1, D)), _const(w2.shape)],
                 [_row(tm, D), _row(tm, D)], [_sds((S, D)), _sds((S, D))],
                 scratch=[pltpu.VMEM((HALO_A + tm, D), F32), pltpu.VMEM((SUB - 1, HALO_A + tm - SUB, D), F32)],
                 hosts=hosts)(glu, glu, x0, wdw, bdw, lg, lb, w2)


def _fwd_b(x0, win, b_in, lg, lb, ws, bsx, wout, hosts=()):
    tm = 256

    def body(x_ref, win_ref, bin_ref, lg_ref, lb_ref, ws_ref, bsx_ref, wout_ref, z_ref, h_ref, f_scr):
        xb = x_ref[...].astype(BF16)
        for q in range(NQ):
            sl = slice(q * 1024, (q + 1) * 1024)
            h_ref[:, sl] = jnp.dot(xb, win_ref[q], preferred_element_type=F32) + bin_ref[:, sl]
        u, _ = _gelu(h_ref[:, :E])
        v, _ = _gelu(h_ref[:, E:])
        vn, _, _ = _ln(v, lg_ref[...], lb_ref[...])
        vnb = vn.astype(BF16)
        mask, _ = _sgu_masks()
        for hd in range(SGU_H):
            wm = jnp.where(mask, ws_ref[hd], 0.0).astype(BF16)
            cs = slice(hd * SGU_G, (hd + 1) * SGU_G)
            for n in range(tm // SGU_T):
                rs = slice(n * SGU_T, (n + 1) * SGU_T)
                f_scr[rs, cs] = jnp.dot(wm, vnb[rs, cs], preferred_element_type=F32) + bsx_ref[hd]
        mb = (u * f_scr[...]).astype(BF16)
        z_ref[...] = ALPHA * x_ref[...] + jnp.dot(mb, wout_ref[...], preferred_element_type=F32)

    return _call("fwd_b", body, (S // tm,),
                 [_row(tm, D), _wspec(win), _const((1, 2 * E)), _const((1, E)), _const((1, E)),
                  _const((SGU_H, SGU_T, SGU_T)), _const((SGU_H, SGU_T, SGU_G)), _const(wout.shape)],
                 [_row(tm, D), _row(tm, 2 * E)], [_sds((S, D)), _sds((S, 2 * E))],
                 scratch=[pltpu.VMEM((tm, E), F32)], hosts=hosts)(x0, win, b_in, lg, lb, ws, bsx, wout)


def _fwd_c1(x0, win, hosts=()):
    tm = 512

    def body(x_ref, w_ref, hc_ref):
        xb = x_ref[...].astype(BF16)
        for q in range(NQ):
            hc_ref[:, q * 768:(q + 1) * 768] = jnp.dot(xb, w_ref[q], preferred_element_type=F32)

    return _call("fwd_c1", body, (S // tm,), [_row(tm, D), _wspec(win)], _row(tm, 3 * D),
                 _sds((S, 3 * D)), hosts=hosts)(x0, win)


def _short_conv(buf, hc_ref, hcp_ref, wc_ref, tm, i):
    _fill_halo(buf, 0, HALO_C, lambda: hcp_ref[:, D:2 * D] * hcp_ref[:, 2 * D:], i == 0)
    buf[HALO_C:HALO_C + tm, :] = hc_ref[:, D:2 * D] * hc_ref[:, 2 * D:]
    y = wc_ref[0:1, :] * buf[pl.ds(HALO_C - 2, tm), :]
    for k in range(1, CONV_C):
        y = y + wc_ref[k:k + 1, :] * buf[pl.ds(HALO_C - 2 + k, tm), :]
    return y


def _fwd_c2(hc, x0, wc, wout, hosts=()):
    tm = 256

    def body(hc_ref, hcp_ref, x_ref, wc_ref, wout_ref, z_ref, buf):
        y = _short_conv(buf, hc_ref, hcp_ref, wc_ref, tm, pl.program_id(0))
        mb = (hc_ref[:, :D] * y).astype(BF16)
        z_ref[...] = ALPHA * x_ref[...] + jnp.dot(mb, wout_ref[...], preferred_element_type=F32)

    return _call("fwd_c2", body, (S // tm,),
                 [_row(tm, 3 * D), _prev(tm, HALO_C, 3 * D), _row(tm, D), _const((8, D)), _const(wout.shape)],
                 _row(tm, D), _sds((S, D)), scratch=[pltpu.VMEM((HALO_C + tm, D), F32)], hosts=hosts
                 )(hc, hc, x0, wc, wout)


def _fwd_ffn(z1, lg, lb, wgt, wut, wd, l, hosts=()):
    tm = 256

    def body(z_ref, lg_ref, lb_ref, wg_ref, wu_ref, wd_ref, o_ref, a_ref, u_ref, hm_ref):
        x1, _, _ = _ln(z_ref[...], lg_ref[...], lb_ref[...])
        xb = x1.astype(BF16)
        a = _mmt(xb, wg_ref[...])
        u = _mmt(xb, wu_ref[...])
        hmb = (a * _sig(a) * u).astype(BF16)
        a_ref[...] = a.astype(BF16)
        u_ref[...] = u.astype(BF16)
        hm_ref[...] = hmb
        o_ref[...] = ALPHA * x1 + jnp.dot(hmb, wd_ref[...], preferred_element_type=F32)

    return _call(f"fwd_ffn_{l}", body, (S // tm,),
                 [_row(tm, D), _const((1, D)), _const((1, D)), _const((FF, D)), _const((FF, D)), _const((FF, D))],
                 [_row(tm, D), _row(tm, FF), _row(tm, FF), _row(tm, FF)],
                 [_sds((S, D)), _sds((S, FF), BF16), _sds((S, FF), BF16), _sds((S, FF), BF16)],
                 hosts=hosts)(z1, lg, lb, wgt, wut, wd)


def _ple_parts(z2, p, lg, lb, wg_ref, wp_ref, pg):
    x2, xhat, rstd = _ln(z2, lg, lb)
    xb = x2.astype(BF16)
    gate = _sig(jnp.dot(xb, wg_ref[...], preferred_element_type=F32))
    pb = p.astype(BF16)
    qp = jnp.concatenate([jnp.dot(pb, wp_ref[q], preferred_element_type=F32) for q in range(NQ)], axis=1)
    rs = lax.rsqrt(jnp.mean(qp * qp, axis=-1, keepdims=True) + LN_EPS)
    qn = qp * rs
    return x2, xhat, rstd, xb, gate, qn, rs, qn * pg


def _fwd_ple(z2, p, lg, lb, wg, wp, pg, l, hosts=()):
    tm = 512

    def body(z_ref, p_ref, lg_ref, lb_ref, wg_ref, wp_ref, pg_ref, o_ref):
        x2, _, _, _, gate, _, _, r = _ple_parts(z_ref[...], p_ref[...], lg_ref[...], lb_ref[...], wg_ref, wp_ref,
                                                pg_ref[...])
        o_ref[...] = x2 + gate * r

    return _call(f"fwd_ple_{l}", body, (S // tm,),
                 [_row(tm, D), _row(tm, 256), _const((1, D)), _const((1, D)), _const(wg.shape), _wspec(wp),
                  _const((1, D))],
                 _row(tm, D), _sds((S, D)), hosts=hosts)(z2, p, lg, lb, wg, wp, pg)


def _loss_head(y, target):
    tm = 512

    def body(y_ref, t_ref, dy_ref, acc_ref):
        @pl.when(pl.program_id(0) == 0)
        def _():
            acc_ref[...] = jnp.zeros_like(acc_ref)

        e = y_ref[...] - t_ref[...]
        dy_ref[...] = e * (1.0 / D)
        acc_ref[0:1, :] += _colsum(e * e)

    return _call("loss_head", body, (S // tm,), [_row(tm, D), _row(tm, D)], [_row(tm, D), _acc(8, D)],
                 [_sds((S, D)), _sds((8, D))])(y, target)


def _zero_first(*refs):
    @pl.when(pl.program_id(0) == 0)
    def _():
        for r in refs:
            r[...] = jnp.zeros_like(r)


def _bwd_ple(g, z2, p, lg, lb, wg, wp, pg, l, hosts=()):
    tm = 256

    def body(g_ref, z_ref, p_ref, lg_ref, lb_ref, wg_ref, wp_ref, pg_ref, dz_ref, xb_ref, dgp_ref, dqp_ref, acc_ref):
        _zero_first(acc_ref)
        gin = g_ref[...]
        lgv, pgv = lg_ref[...], pg_ref[...]
        _, xhat, rstd, xb, gate, qn, rs, r = _ple_parts(z_ref[...], p_ref[...], lgv, lb_ref[...], wg_ref, wp_ref, pgv)
        xb_ref[...] = xb
        dgpb = (gin * r * gate * (1.0 - gate)).astype(BF16)
        dgp_ref[...] = dgpb
        dx2 = gin + _mmt(dgpb, wg_ref[...])
        dr = gin * gate
        acc_ref[0:1, :] += _colsum(dr * qn)
        t = dr * pgv
        dqp_ref[...] = (rs * (t - qn * jnp.mean(t * qn, axis=-1, keepdims=True))).astype(BF16)
        acc_ref[1:2, :] += _colsum(dx2 * xhat)
        acc_ref[2:3, :] += _colsum(dx2)
        dz_ref[...] = _ln_bwd(dx2 * lgv, xhat, rstd)

    return _call(f"bwd_ple_{l}", body, (S // tm,),
                 [_row(tm, D), _row(tm, D), _row(tm, 256), _const((1, D)), _const((1, D)), _const(wg.shape),
                  _wspec(wp), _const((1, D))],
                 [_row(tm, D), _row(tm, D), _row(tm, D), _row(tm, D), _acc(8, D)],
                 [_sds((S, D)), _sds((S, D), BF16), _sds((S, D), BF16), _sds((S, D), BF16), _sds((8, D))],
                 hosts=hosts)(g, z2, p, lg, lb, wg, wp, pg)


def _bwd_ffn(dz2, z1, ab, ub, lg, lb, wgt, wut, wd, l, hosts=()):
    tm = 256

    def body(dz2_ref, z_ref, a_ref, u_ref, lg_ref, lb_ref, wg_ref, wu_ref, wd_ref, dz1_ref, xb_ref, da_ref, du_ref,
             acc_ref):
        _zero_first(acc_ref)
        dz2v = dz2_ref[...]
        lgv = lg_ref[...]
        x1, xhat, rstd = _ln(z_ref[...], lgv, lb_ref[...])
        xb_ref[...] = x1.astype(BF16)
        a = a_ref[...].astype(F32)
        u = u_ref[...].astype(F32)
        sg = _sig(a)
        dhm = _mmt(dz2v, wd_ref[...])
        dub = (dhm * (a * sg)).astype(BF16)
        dab = (dhm * u * _silu_grad(a, sg)).astype(BF16)
        da_ref[...] = dab
        du_ref[...] = dub
        dx1 = ALPHA * dz2v + _mm(dab, wg_ref[...]) + _mm(dub, wu_ref[...])
        acc_ref[0:1, :] += _colsum(dx1 * xhat)
        acc_ref[1:2, :] += _colsum(dx1)
        dz1_ref[...] = _ln_bwd(dx1 * lgv, xhat, rstd)

    return _call(f"bwd_ffn_{l}", body, (S // tm,),
                 [_row(tm, D), _row(tm, D), _row(tm, FF), _row(tm, FF), _const((1, D)), _const((1, D)),
                  _const((FF, D)), _const((FF, D)), _const((FF, D))],
                 [_row(tm, D), _row(tm, D), _row(tm, FF), _row(tm, FF), _acc(8, D)],
                 [_sds((S, D)), _sds((S, D), BF16), _sds((S, FF), BF16), _sds((S, FF), BF16), _sds((8, D))],
                 hosts=hosts)(dz2, z1, ab, ub, lg, lb, wgt, wut, wd)


def _bwd_a2(dz1, cv, lg, lb, w2, l, hosts=()):
    tm = 512

    def body(dz_ref, cv_ref, lg_ref, lb_ref, w2_ref, dcv_ref, sb_ref, acc_ref):
        _zero_first(acc_ref)
        lgv = lg_ref[...]
        n, xhat, rstd = _ln(cv_ref[...], lgv, lb_ref[...])
        sg = _sig(n)
        sb_ref[...] = (n * sg).astype(BF16)
        dzb = dz_ref[...].astype(BF16)
        ds = _mmt(dzb, w2_ref[...])
        dn = ds * _silu_grad(n, sg)
        acc_ref[0:1, :] += _colsum(dn * xhat)
        acc_ref[1:2, :] += _colsum(dn)
        dcv = _ln_bwd(dn * lgv, xhat, rstd)
        acc_ref[2:3, :] += _colsum(dcv)
        dcv_ref[...] = dcv

    return _call(f"bwd_a2_{l}", body, (S // tm,),
                 [_row(tm, D), _row(tm, D), _const((1, D)), _const((1, D)), _const(w2.shape)],
                 [_row(tm, D), _row(tm, D), _acc(8, D)],
                 [_sds((S, D)), _sds((S, D), BF16), _sds((8, D))], hosts=hosts)(dz1, cv, lg, lb, w2)


def _bwd_conv_a(dcv, glu, wdw, l, hosts=()):
    tm = 256
    nb = S // tm

    def body(d_ref, dn_ref, g_ref, gp_ref, w_ref, dglu_ref, dw_ref, bufd, bufx, sh):
        i = pl.program_id(0)
        _zero_first(dw_ref)
        bufd[0:tm, :] = d_ref[...]
        _fill_halo(bufd, tm, HALO_A, lambda: dn_ref[...], i == nb - 1)
        _fill_halo(bufx, 0, HALO_A, lambda: gp_ref[...], i == 0)
        bufx[HALO_A:HALO_A + tm, :] = g_ref[...]
        _shift_copies(bufd, sh)
        _conv_rows(dglu_ref, w_ref, None, [CONV_A - 1 - k for k in range(CONV_A)], bufd, sh, tm)
        _shift_copies(bufx, sh)
        _conv_wgrad(dw_ref, d_ref, [HALO_A - (CONV_A - 1) + k for k in range(CONV_A)], bufx, sh, tm)

    return _call(f"bwd_conv_a_{l}", body, (nb,),
                 [_row(tm, D), _next(tm, HALO_A, D), _row(tm, D), _prev(tm, HALO_A, D), _const((32, D))],
                 [_row(tm, D), _acc(32, D)], [_sds((S, D)), _sds((32, D))],
                 scratch=[pltpu.VMEM((tm + HALO_A, D), F32), pltpu.VMEM((HALO_A + tm, D), F32),
                          pltpu.VMEM((SUB - 1, HALO_A + tm - SUB, D), F32)], hosts=hosts)(dcv, dcv, glu, glu, wdw)


def _bwd_a1(dglu, h, dz1, w1, l):
    tm = 256

    def body(dg_ref, h_ref, dz_ref, w_ref, dx_ref, dh_ref, acc_ref):
        _zero_first(acc_ref)
        a, g = h_ref[:, :D], h_ref[:, D:]
        sg = _sig(g)
        dgl = dg_ref[...]
        da = dgl * sg
        dg = dgl * a * sg * (1.0 - sg)
        acc_ref[0:1, 0:D] += _colsum(da)
        acc_ref[0:1, D:2 * D] += _colsum(dg)
        dh_ref[:, 0:D] = da.astype(BF16)
        dh_ref[:, D:2 * D] = dg.astype(BF16)
        dx = ALPHA * dz_ref[...]
        for q in range(NQ):
            dx = dx + _mmt(dh_ref[:, q * 512:(q + 1) * 512], w_ref[q])
        dx_ref[...] = dx

    return _call(f"bwd_a1_{l}", body, (S // tm,),
                 [_row(tm, D), _row(tm, 2 * D), _row(tm, D), _wspec(w1)],
                 [_row(tm, D), _row(tm, 2 * D), _acc(8, 2 * D)],
                 [_sds((S, D)), _sds((S, 2 * D), BF16), _sds((8, 2 * D))])(dglu, h, dz1, w1)


def _bwd_c2(dz1, hc, wc, wout):
    tm = 256

    def body(dz_ref, hc_ref, hcp_ref, wc_ref, wout_ref, dy_ref, dbg_ref, mb_ref, buf):
        y = _short_conv(buf, hc_ref, hcp_ref, wc_ref, tm, pl.program_id(0))
        dzb = dz_ref[...].astype(BF16)
        dm = _mmt(dzb, wout_ref[...])
        bg = hc_ref[:, :D]
        mb_ref[...] = (bg * y).astype(BF16)
        dbg_ref[...] = (dm * y).astype(BF16)
        dy_ref[...] = dm * bg

    return _call("bwd_c2", body, (S // tm,),
                 [_row(tm, D), _row(tm, 3 * D), _prev(tm, HALO_C, 3 * D), _const((8, D)), _const(wout.shape)],
                 [_row(tm, D), _row(tm, D), _row(tm, D)],
                 [_sds((S, D)), _sds((S, D), BF16), _sds((S, D), BF16)],
                 scratch=[pltpu.VMEM((HALO_C + tm, D), F32)])(dz1, hc, hc, wc, wout)


def _bwd_c1(dy, hc, dbg, dz1, wc, win):
    tm = 256
    nb = S // tm

    def body(d_ref, dn_ref, hc_ref, hcp_ref, dbg_ref, dz_ref, wc_ref, win_ref, dx_ref, dhc_ref, dwc_ref, bufd, bufq):
        i = pl.program_id(0)
        _zero_first(dwc_ref)
        bufd[0:tm, :] = d_ref[...]
        _fill_halo(bufd, tm, HALO_C, lambda: dn_ref[...], i == nb - 1)
        _fill_halo(bufq, 0, HALO_C, lambda: hcp_ref[:, D:2 * D] * hcp_ref[:, 2 * D:], i == 0)
        bufq[HALO_C:HALO_C + tm, :] = hc_ref[:, D:2 * D] * hc_ref[:, 2 * D:]
        dq = wc_ref[0:1, :] * bufd[pl.ds(CONV_C - 1, tm), :]
        for k in range(1, CONV_C):
            dq = dq + wc_ref[k:k + 1, :] * bufd[pl.ds(CONV_C - 1 - k, tm), :]
        dv = d_ref[...]
        for k in range(CONV_C):
            dwc_ref[k:k + 1, :] += _colsum(dv * bufq[pl.ds(HALO_C - (CONV_C - 1) + k, tm), :])
        dhc_ref[:, 0:D] = dbg_ref[...]
        dhc_ref[:, D:2 * D] = (dq * hc_ref[:, 2 * D:]).astype(BF16)
        dhc_ref[:, 2 * D:3 * D] = (dq * hc_ref[:, D:2 * D]).astype(BF16)
        dx = ALPHA * dz_ref[...]
        for q in range(NQ):
            dx = dx + _mmt(dhc_ref[:, q * 768:(q + 1) * 768], win_ref[q])
        dx_ref[...] = dx

    return _call("bwd_c1", body, (nb,),
                 [_row(tm, D), _next(tm, HALO_C, D), _row(tm, 3 * D), _prev(tm, HALO_C, 3 * D), _row(tm, D),
                  _row(tm, D), _const((8, D)), _wspec(win)],
                 [_row(tm, D), _row(tm, 3 * D), _acc(8, D)],
                 [_sds((S, D)), _sds((S, 3 * D), BF16), _sds((8, D))],
                 scratch=[pltpu.VMEM((tm + HALO_C, D), F32), pltpu.VMEM((HALO_C + tm, D), F32)]
                 )(dy, dy, hc, hc, dbg, dz1, wc, win)


def _bwd_b(dz1, h, lg, lb, win, wout, ws, wst, bsx):
    tm = 128
    nb = S // tm

    def body(dz_ref, h_ref, lg_ref, lb_ref, win_ref, wout_ref, ws_ref, wst_ref, bsx_ref,
             dx_ref, dh_ref, mb_ref, acc_ref, dws_ref, dbs_ref, f_scr, dvn_scr):
        _zero_first(acc_ref, dws_ref, dbs_ref)
        lgv = lg_ref[...]
        hu, hv = h_ref[:, :E], h_ref[:, E:]
        u, tu = _gelu(hu)
        v, tv = _gelu(hv)
        vn, xhat, rstd = _ln(v, lgv, lb_ref[...])
        vnb = vn.astype(BF16)
        dzb = dz_ref[...].astype(BF16)
        dm = _mmt(dzb, wout_ref[...])
        mask, mask_t = _sgu_masks()
        for hd in range(SGU_H):
            wm = jnp.where(mask, ws_ref[hd], 0.0).astype(BF16)
            cs = slice(hd * SGU_G, (hd + 1) * SGU_G)
            for n in range(tm // SGU_T):
                rs = slice(n * SGU_T, (n + 1) * SGU_T)
                f_scr[rs, cs] = jnp.dot(wm, vnb[rs, cs], preferred_element_type=F32) + bsx_ref[hd]
        f = f_scr[...]
        mb_ref[...] = (u * f).astype(BF16)
        du = dm * f
        df = dm * u
        dfb = df.astype(BF16)
        for hd in range(SGU_H):
            wmt = jnp.where(mask_t, wst_ref[hd], 0.0).astype(BF16)
            cs = slice(hd * SGU_G, (hd + 1) * SGU_G)
            for n in range(tm // SGU_T):
                rs = slice(n * SGU_T, (n + 1) * SGU_T)
                dvn_scr[rs, cs] = jnp.dot(wmt, dfb[rs, cs], preferred_element_type=F32)
                dws_ref[hd] += lax.dot_general(dfb[rs, cs], vnb[rs, cs], (((1,), (1,)), ((), ())),
                                               preferred_element_type=F32)
                dbs_ref[hd] += df[rs, cs]
        dvn = dvn_scr[...]
        acc_ref[1:2, 0:E] += _colsum(dvn * xhat)
        acc_ref[2:3, 0:E] += _colsum(dvn)
        dv = _ln_bwd(dvn * lgv, xhat, rstd)
        dhu = du * _gelu_grad(hu, tu)
        dhv = dv * _gelu_grad(hv, tv)
        acc_ref[0:1, 0:E] += _colsum(dhu)
        acc_ref[0:1, E:2 * E] += _colsum(dhv)
        dh_ref[:, 0:E] = dhu.astype(BF16)
        dh_ref[:, E:2 * E] = dhv.astype(BF16)
        dx = ALPHA * dz_ref[...]
        for q in range(NQ):
            dx = dx + _mmt(dh_ref[:, q * 1024:(q + 1) * 1024], win_ref[q])
        dx_ref[...] = dx

        @pl.when(pl.program_id(0) == nb - 1)
        def _():
            for hd in range(SGU_H):
                dws_ref[hd] = jnp.where(mask, dws_ref[hd], 0.0)

    c3 = lambda a, b, c: pl.BlockSpec((a, b, c), lambda i: (0, 0, 0))
    return _call("bwd_b", body, (nb,),
                 [_row(tm, D), _row(tm, 2 * E), _const((1, E)), _const((1, E)), _wspec(win), _const(wout.shape),
                  _const((SGU_H, SGU_T, SGU_T)), _const((SGU_H, SGU_T, SGU_T)), _const((SGU_H, SGU_T, SGU_G))],
                 [_row(tm, D), _row(tm, 2 * E), _row(tm, E), _acc(8, 2 * E), c3(SGU_H, SGU_T, SGU_T),
                  c3(SGU_H, SGU_T, SGU_G)],
                 [_sds((S, D)), _sds((S, 2 * E), BF16), _sds((S, E), BF16), _sds((8, 2 * E)),
                  _sds((SGU_H, SGU_T, SGU_T)), _sds((SGU_H, SGU_T, SGU_G))],
                 scratch=[pltpu.VMEM((tm, E), F32), pltpu.VMEM((tm, E), F32)]
                 )(dz1, h, lg, lb, win, wout, ws, wst, bsx)


def _mm_tn(name, a, amode, b, bmode, k, n, groups=NQ, hosts=()):
    def block_bytes(ts):
        ka = k if amode == "1" else groups * k
        nb = n if bmode == "1" else groups * n
        return 2 * (ts * ka * a.dtype.itemsize + ts * nb * b.dtype.itemsize + groups * k * n * 4)

    ts = min(1024 if block_bytes(1024) <= DW_BLOCK_BUDGET else 512, S)

    def spec(mode, w):
        if mode == "1":
            return pl.BlockSpec((ts, w), lambda s: (s, 0))
        if mode == "c":
            return pl.BlockSpec((ts, groups * w), lambda s: (s, 0))
        return pl.BlockSpec((groups, ts, w), lambda s: (0, s, 0))

    def pick(ref, mode, w, g):
        if mode == "1":
            return ref[...]
        if mode == "c":
            return ref[:, g * w:(g + 1) * w]
        return ref[g]

    def body(a_ref, b_ref, o_ref):
        _zero_first(o_ref)
        a_t = jnp.transpose(a_ref[...].astype(BF16)) if amode == "1" else None
        b_1 = b_ref[...].astype(BF16) if bmode == "1" else None
        for g in range(groups):
            lhs = a_t if amode == "1" else jnp.transpose(pick(a_ref, amode, k, g).astype(BF16))
            rhs = b_1 if bmode == "1" else pick(b_ref, bmode, n, g).astype(BF16)
            o_ref[0, g] += jnp.dot(lhs, rhs, preferred_element_type=F32)

    return _call(name, body, (S // ts,), [spec(amode, k), spec(bmode, n)],
                 pl.BlockSpec((1, groups, k, n), lambda s: (0, 0, 0, 0)), _sds((1, groups, k, n)), hosts=hosts)(a, b)


def _row_block(k, cap=256):
    return max(t for t in range(16, min(k, cap) + 1, 16) if k % t == 0)


def _cast_bf16(w):
    nl, k, n = w.shape
    tb = _row_block(k, 512)

    def body(w_ref, o_ref):
        o_ref[...] = w_ref[...].astype(BF16)

    spec = pl.BlockSpec((None, tb, n), lambda l, i: (l, i, 0))
    return _call("cast_bf16", body, (nl, k // tb), [spec], spec, _sds(w.shape, BF16))(w)


def _adam(name, w, m, v, gc, l, prev):
    nl, k, n = w.shape
    nc = gc.shape[0]
    tb = _row_block(k)

    def body(w_ref, m_ref, v_ref, g_ref, *rest):
        go_ref, d_ref, mo_ref, vo_ref = rest[-4:]
        g = g_ref[0].astype(F32)
        for c in range(1, nc):
            g = g + g_ref[c].astype(F32)
        m2 = ADAM_B1 * m_ref[...] + (1.0 - ADAM_B1) * g
        v2 = ADAM_B2 * v_ref[...] + (1.0 - ADAM_B2) * (g * g)
        m_hat = m2 / (1.0 - ADAM_B1 ** ADAM_STEP)
        v_hat = v2 / (1.0 - ADAM_B2 ** ADAM_STEP)
        go_ref[...] = g
        d_ref[...] = -ADAM_LR * (m_hat / (jnp.sqrt(v_hat) + ADAM_EPS) + ADAM_WD * w_ref[...])
        mo_ref[...] = m2
        vo_ref[...] = v2

    spec = pl.BlockSpec((None, tb, n), lambda i: (l, i, 0))
    gspec = pl.BlockSpec((nc, None, tb, n), lambda i: (0, 0, i, 0))
    in_specs, args, aliases = [spec, spec, spec, gspec], [w, m, v, gc], {}
    if prev is not None:
        in_specs += [pl.BlockSpec(memory_space=pl.ANY)] * 4
        args += list(prev)
        aliases = {4 + j: j for j in range(4)}
    return _call(name, body, (k // tb,), in_specs, [spec] * 4, [_sds(w.shape)] * 4, aliases=aliases)(*args)


def _sum8(name, g8):
    r = g8.shape[1]

    def body(g_ref, o_ref):
        acc = g_ref[0]
        for d in range(1, 8):
            acc = acc + g_ref[d]
        o_ref[...] = acc

    return _call(name, body, (1,), [pl.BlockSpec((8, r, 128), lambda i: (0, 0, 0))],
                 pl.BlockSpec((r, 128), lambda i: (0, 0)), _sds((r, 128)))(g8)


def _place():
    x, y, c = lax.axis_index("x"), lax.axis_index("y"), lax.axis_index("c")
    return x, y, c, 2 * x + y, (x, y, 1 - c), [(1 - x, y), (x, 1 - y), (1 - x, 1 - y)]


class _Exchange:
    def __init__(self, arrays, out_shapes):
        self.arrays, self.out_shapes = list(arrays), list(out_shapes)
        n = len(self.arrays)
        self.sems = [pltpu.SemaphoreType.DMA((7 * n,)), pltpu.SemaphoreType.DMA((7 * n,)),
                     pltpu.SemaphoreType.DMA((n,))]

    def _copies(self, ins, outs, sems):
        send, recv, lsem = sems
        local_src, remote_src, dst = self.maps(ins, outs)
        x, y, c, q, sib, chips = _place()

        def rcopy(w, k, qq, cc, to, src=None):
            return pltpu.make_async_remote_copy(
                src_ref=dst(w, qq, cc) if src is None else src, dst_ref=dst(w, qq, cc),
                send_sem=send.at[7 * w + k], recv_sem=recv.at[7 * w + k], device_id=to, device_id_type=MESH)

        def mine(w):
            return pltpu.make_async_copy(local_src(w), dst(w, q, c), lsem.at[w])

        def first(w):
            return [rcopy(w, 0, q, c, sib, local_src(w))] + [
                rcopy(w, 1 + j, q, c, (cx, cy, c), remote_src(w, 2 * cx + cy)) for j, (cx, cy) in enumerate(chips)]

        return rcopy, mine, first, (x, y, c), q, c, sib, chips

    def start(self, ins, outs, sems):
        _, mine, first, *_ = self._copies(ins, outs, sems)
        for w in range(len(self.arrays)):
            mine(w).start()
            for cp in first(w):
                cp.start()

    def finish(self, ins, outs, sems):
        rcopy, mine, first, me, q, c, sib, chips = self._copies(ins, outs, sems)
        n = len(self.arrays)
        for w in range(n):
            for j, (cx, cy) in enumerate(chips):
                rcopy(w, 1 + j, 2 * cx + cy, c, me).wait_recv()
                rcopy(w, 4 + j, 2 * cx + cy, c, sib).start()
        for w in range(n):
            rcopy(w, 0, q, 1 - c, me).wait_recv()
            for j, (cx, cy) in enumerate(chips):
                rcopy(w, 4 + j, 2 * cx + cy, 1 - c, me).wait_recv()
        for w in range(n):
            for cp in first(w):
                cp.wait_send()
            for j, (cx, cy) in enumerate(chips):
                rcopy(w, 4 + j, 2 * cx + cy, c, sib).wait_send()
            mine(w).wait()


class _GatherWeights(_Exchange):
    def __init__(self, items):
        self.layers = [l for _, l in items]
        self.kh = [s.shape[1] // 2 for s, _ in items]
        super().__init__([s for s, _ in items], [_sds((NQ, 1) + s.shape[1:], BF16) for s, _ in items])

    def maps(self, ins, outs):
        c = lax.axis_index("c")
        src = lambda w: ins[w].at[pl.ds(self.layers[w], 1), pl.ds(c * self.kh[w], self.kh[w]), :]
        return src, lambda w, q: src(w), lambda w, q, cc: outs[w].at[q, :, pl.ds(cc * self.kh[w], self.kh[w]), :]


class _ScatterPartials(_Exchange):
    def __init__(self, parts):
        super().__init__(parts, [_sds((NQ, 1, 2) + p.shape[2:], BF16) for p in parts])

    def maps(self, ins, outs):
        q = 2 * lax.axis_index("x") + lax.axis_index("y")
        return (lambda w: ins[w].at[:, q]), (lambda w, qq: ins[w].at[:, qq]), (lambda w, qq, cc: outs[w].at[qq, :, cc])


class _Gather8(_Exchange):
    def __init__(self, v):
        super().__init__([v], [_sds((8,) + v.shape)])

    def maps(self, ins, outs):
        return (lambda w: ins[0]), (lambda w, q: ins[0]), (lambda w, q, cc: outs[0].at[2 * q + cc])


class _SwapHalves:
    def __init__(self, dws):
        self.arrays = list(dws)
        self.kh = [d.shape[2] // 2 for d in dws]
        self.out_shapes = [_sds(d.shape[:2] + (kh,) + d.shape[3:]) for d, kh in zip(dws, self.kh)]
        self.sems = [pltpu.SemaphoreType.DMA((len(dws),)), pltpu.SemaphoreType.DMA((len(dws),))]

    def _copies(self, ins, outs, sems):
        send, recv = sems
        _, _, c, _, sib, _ = _place()
        return [pltpu.make_async_remote_copy(
            src_ref=ins[w].at[:, :, pl.ds((1 - c) * self.kh[w], self.kh[w]), :], dst_ref=outs[w],
            send_sem=send.at[w], recv_sem=recv.at[w], device_id=sib, device_id_type=MESH)
            for w in range(len(self.arrays))]

    def start(self, ins, outs, sems):
        for cp in self._copies(ins, outs, sems):
            cp.start()

    def finish(self, ins, outs, sems):
        for cp in self._copies(ins, outs, sems):
            cp.wait()


def _comm_only(name, host):
    n_in, n_out = len(host.arrays), len(host.out_shapes)

    def body(*refs):
        ins, outs, sems = refs[:n_in], refs[n_in:n_in + n_out], refs[n_in + n_out:]
        host.start(ins, outs, sems)
        host.finish(ins, outs, sems)

    any_spec = pl.BlockSpec(memory_space=pl.ANY)
    return pl.pallas_call(body, name=name, in_specs=[any_spec] * n_in, out_specs=[any_spec] * n_out,
                          out_shape=host.out_shapes, scratch_shapes=host.sems)(*host.arrays)


def _add_halves(dw, got, cidx):
    nl, _, k, n = dw.shape
    kh = k // 2

    def body(c_ref, a_ref, b_ref, o_ref):
        o_ref[...] = (a_ref[...] + b_ref[...]).astype(BF16)

    grid_spec = pltpu.PrefetchScalarGridSpec(
        num_scalar_prefetch=1, grid=(nl, NQ),
        in_specs=[pl.BlockSpec((None, None, None, kh, n), lambda l, q, c_ref: (l, q, c_ref[0], 0, 0)),
                  pl.BlockSpec((None, None, kh, n), lambda l, q, c_ref: (l, q, 0, 0))],
        out_specs=pl.BlockSpec((None, None, kh, n), lambda l, q, c_ref: (l, q, 0, 0)))
    return pl.pallas_call(
        body, name="add_halves", grid_spec=grid_spec, out_shape=_sds((nl, NQ, kh, n), BF16),
        compiler_params=pltpu.CompilerParams(dimension_semantics=("arbitrary", "arbitrary"),
                                             vmem_limit_bytes=VMEM_LIMIT))(cidx, dw.reshape(nl, NQ, 2, kh, n), got)


def _gather8(name, v):
    return _comm_only(name, _Gather8(v))[0]


PACK = 16 * 128


def _pack(arrays):
    parts = []
    for a in arrays:
        flat = a.reshape(-1)
        parts.append(jnp.pad(flat, (0, (-flat.shape[0]) % PACK)))
    return jnp.concatenate(parts).reshape(-1, 128)


def _unpack(packed, shapes):
    flat = packed.reshape(-1)
    out, off = [], 0
    for shp in shapes:
        size = 1
        for d in shp:
            size *= d
        out.append(flat[off:off + size].reshape(shp))
        off += size + (-size) % PACK
    return out


def kernel(x, p, a_w_pw1, a_b_pw1, a_w_dw, a_b_dw, a_ln_g, a_ln_b, a_w_pw2, b_w_in, b_b_in, b_ln_g, b_ln_b, b_w_s, b_b_s, b_w_out, c_w_in, c_w_conv, c_w_out, ln1_g, ln1_b, ln2_g, ln2_b, ffn_w_gate, ffn_w_up, ffn_w_down, ple_w_gate, ple_w_proj, ple_norm_g, loss_target, m_a_w_pw1, m_a_b_pw1, m_a_w_dw, m_a_b_dw, m_a_ln_g, m_a_ln_b, m_a_w_pw2, m_b_w_in, m_b_b_in, m_b_ln_g, m_b_ln_b, m_b_w_s, m_b_b_s, m_b_w_out, m_c_w_in, m_c_w_conv, m_c_w_out, m_ln1_g, m_ln1_b, m_ln2_g, m_ln2_b, m_ffn_w_gate, m_ffn_w_up, m_ffn_w_down, m_ple_w_gate, m_ple_w_proj, m_ple_norm_g, v_a_w_pw1, v_a_b_pw1, v_a_w_dw, v_a_b_dw, v_a_ln_g, v_a_ln_b, v_a_w_pw2, v_b_w_in, v_b_b_in, v_b_ln_g, v_b_ln_b, v_b_w_s, v_b_b_s, v_b_w_out, v_c_w_in, v_c_w_conv, v_c_w_out, v_ln1_g, v_ln1_b, v_ln2_g, v_ln2_b, v_ffn_w_gate, v_ffn_w_up, v_ffn_w_down, v_ple_w_gate, v_ple_w_proj, v_ple_norm_g):
    args = dict(locals())
    wts = {k: args[k] for k in WEIGHTS}
    mom = {k: args["m_" + k] for k in WEIGHTS}
    var = {k: args["v_" + k] for k in WEIGHTS}
    for k in TRANSPOSED:
        wts[k], mom[k], var[k] = (jnp.transpose(t[k], (0, 2, 1)) for t in (wts, mom, var))
    q_idx = 2 * lax.axis_index("x") + lax.axis_index("y")
    c_idx = lax.axis_index("c").astype(jnp.int32).reshape(1)

    wb = {k: _cast_bf16(wts[k]) for k in BIG}
    mixw = [[("a_w_pw1", 0), ("a_w_pw2", 0)], [("b_w_in", 0), ("b_w_out", 0)], [("c_w_in", 0), ("c_w_out", 0)],
            [("a_w_pw1", 1), ("a_w_pw2", 1)]]
    ffnw = [[("ffn_w_gate", l), ("ffn_w_up", l), ("ffn_w_down", l)] for l in range(DEPTH)]
    plew = [[("ple_w_gate", l), ("ple_w_proj", l)] for l in range(DEPTH)]
    fwd_plan = {("a1", 0): mixw[0][1:] + plew[0], ("a2", 0): ffnw[0], ("ffn", 0): mixw[1], ("ple", 0): plew[1],
                ("b", 1): ffnw[1], ("ffn", 1): mixw[2] + ffnw[2][:1], ("ple", 1): plew[2],
                ("c1", 2): ffnw[2][1:2], ("c2", 2): ffnw[2][2:], ("ffn", 2): mixw[3] + ffnw[3][:1], ("ple", 2): plew[3],
                ("a1", 3): ffnw[3][1:2], ("a2", 3): ffnw[3][2:]}
    gw = {}

    def gather(keys):
        return _GatherWeights([(wb[name], l) for name, l in keys])

    def hosted(tag, fn, *fargs):
        keys = fwd_plan.get(tag)
        if not keys:
            return fn(*fargs)
        own, (got,) = fn(*fargs, hosts=[gather(keys)])
        store(keys, got)
        return own

    def store(keys, got):
        for (name, l), arr in zip(keys, got):
            gw[name, l] = arr.reshape(NQ * arr.shape[2], arr.shape[3]) if name in ROW_SHARDED else arr

    first_keys = mixw[0][:1]
    store(first_keys, _comm_only("gather_first", gather(first_keys)))
    shard_shapes = [wts[k].shape for k in SMALL_SHARDED]
    small8 = _gather8("gather_small", _pack([wts[k] for k in SMALL_SHARDED]))
    per_chip = [_unpack(small8[2 * qq], shard_shapes) for qq in range(NQ)]
    full = {k: jnp.concatenate([per_chip[qq][i] for qq in range(NQ)], axis=-1) for i, k in enumerate(SMALL_SHARDED)}
    for k in SMALL_REPL:
        full[k] = wts[k]

    def vec(name, l):
        return full[name][l][None, :]

    def conv_w(name, l, rows):
        w = full[name][l]
        return jnp.pad(w, ((0, rows - w.shape[0]), (0, 0)))

    ws = full["b_w_s"][0]
    wst = jnp.transpose(ws, (0, 2, 1))
    bsx = jnp.broadcast_to(full["b_b_s"][0][:, :, None], (SGU_H, SGU_T, SGU_G))

    x0s, z1s, z2s, saved, ffn_saved = [], [], [], [], []
    cur = x[0]
    for i in range(DEPTH):
        mix, j = i % 3, i // 3
        x0s.append(cur)
        if mix == 0:
            h, glu = hosted(("a1", i), _fwd_a1, cur, gw["a_w_pw1", j], vec("a_b_pw1", j), i)
            z1, cv = hosted(("a2", i), _fwd_a2, glu, cur, conv_w("a_w_dw", j, 32), vec("a_b_dw", j), vec("a_ln_g", j),
                            vec("a_ln_b", j), gw["a_w_pw2", j], i)
            saved.append((h, glu, cv))
        elif mix == 1:
            z1, h = hosted(("b", i), _fwd_b, cur, gw["b_w_in", 0], vec("b_b_in", 0), vec("b_ln_g", 0),
                           vec("b_ln_b", 0), ws, bsx, gw["b_w_out", 0])
            saved.append((h,))
        else:
            hc = hosted(("c1", i), _fwd_c1, cur, gw["c_w_in", 0])
            z1 = hosted(("c2", i), _fwd_c2, hc, cur, conv_w("c_w_conv", 0, 8), gw["c_w_out", 0])
            saved.append((hc,))
        z2, ab, ub, hm = hosted(("ffn", i), _fwd_ffn, z1, vec("ln1_g", i), vec("ln1_b", i), gw["ffn_w_gate", i],
                                gw["ffn_w_up", i], gw["ffn_w_down", i], i)
        ffn_saved.append((ab, ub, hm))
        cur = hosted(("ple", i), _fwd_ple, z2, p[i, 0], vec("ln2_g", i), vec("ln2_b", i), gw["ple_w_gate", i],
                     gw["ple_w_proj", i], vec("ple_norm_g", i), i)
        z1s.append(z1)
        z2s.append(z2)

    g, loss_acc = _loss_head(cur, loss_target[0])
    loss = lax.psum(0.5 / D * jnp.sum(loss_acc[0]), ("x", "y", "c"))

    dws = {}
    sg = {}
    res = {k: None for k in BIG}

    def wgrad(name, l, a, amode, b, bmode, scatter_keys=()):
        _, k, n = wts[name].shape
        hosts = [_ScatterPartials([parts[key] for key in scatter_keys])] if scatter_keys else ()
        if name in ROW_SHARDED:
            out = _mm_tn(f"dw_{name}_{l}", a, "1", b, "1", NQ * k, n, groups=1, hosts=hosts)
        else:
            out = _mm_tn(f"dw_{name}_{l}", a, amode, b, bmode, k, n, hosts=hosts)
        if scatter_keys:
            out, (contribs,) = out
            update(scatter_keys, contribs)
        dws[name, l] = out.reshape(1, NQ, k, n)

    def swap(keys):
        return _SwapHalves([dws[k] for k in keys])

    parts = {}

    def add_halves(keys, got):
        parts.update((k, _add_halves(dws[k], r, c_idx)) for k, r in zip(keys, got))

    def update(keys, contribs):
        for (name, l), gc in zip(keys, contribs):
            _, kq, n = wts[name].shape
            res[name] = _adam(f"adam_{name}_{l}", wts[name], mom[name], var[name], gc.reshape(NQ, 1, kq, n), l,
                              res[name])

    small = SMALL_SHARDED + SMALL_REPL
    late_small = [("a_w_dw", 0), ("a_b_pw1", 0)]
    early_small = [(k, l) for k in small for l in range(full[k].shape[0]) if (k, l) not in late_small]
    pending = None
    for i in reversed(range(DEPTH)):
        mix, j = i % 3, i // 3
        ple_args = (g, z2s[i], p[i, 0], vec("ln2_g", i), vec("ln2_b", i), gw["ple_w_gate", i], gw["ple_w_proj", i],
                    vec("ple_norm_g", i), i)
        if pending:
            (dz2, x2b, dgp, dqp, acc), (got,) = _bwd_ple(*ple_args, hosts=[swap(pending)])
            add_halves(pending, got)
        else:
            dz2, x2b, dgp, dqp, acc = _bwd_ple(*ple_args)
        sg["ple_norm_g", i], sg["ln2_g", i], sg["ln2_b", i] = acc[0], acc[1], acc[2]
        wgrad("ple_w_gate", i, x2b, "c", dgp, "1")
        wgrad("ple_w_proj", i, p[i, 0], "1", dqp, "c")
        ab, ub, hm = ffn_saved[i]
        ffn_args = (dz2, z1s[i], ab, ub, vec("ln1_g", i), vec("ln1_b", i), gw["ffn_w_gate", i], gw["ffn_w_up", i],
                    gw["ffn_w_down", i], i)
        if pending:
            (dz1, x1b, da, du, acc), (contribs,) = _bwd_ffn(
                *ffn_args, hosts=[_ScatterPartials([parts[key] for key in ffnw[i + 1]])])
            update(ffnw[i + 1], contribs)
        else:
            dz1, x1b, da, du, acc = _bwd_ffn(*ffn_args)
        sg["ln1_g", i], sg["ln1_b", i] = acc[0], acc[1]
        wgrad("ffn_w_gate", i, da, "1", x1b, "1", scatter_keys=mixw[i + 1][:1] if pending else ())
        wgrad("ffn_w_up", i, du, "1", x1b, "1", scatter_keys=mixw[i + 1][1:] + plew[i + 1] if pending else ())
        wgrad("ffn_w_down", i, hm, "1", dz2, "1")
        x0 = x0s[i]
        if mix == 0:
            h, glu, cv = saved[i]
            a2_args = (dz1, cv, vec("a_ln_g", j), vec("a_ln_b", j), gw["a_w_pw2", j], i)
            conv_args = (glu, conv_w("a_w_dw", j, 32), i)
            if i == 0:
                early = ffnw[0] + plew[0]
                (dcv, sb, acc), (got,) = _bwd_a2(*a2_args, hosts=[swap(early)])
                sg["a_ln_g", j], sg["a_ln_b", j], sg["a_b_dw", j] = acc[0], acc[1], acc[2]
                add_halves(early, got)
                (dglu, dwdw), (contribs, (g8_early,)) = _bwd_conv_a(
                    dcv, *conv_args, hosts=[_ScatterPartials([parts[key] for key in early]),
                                            _Gather8(_pack([sg[pc] for pc in early_small]))])
                update(early, contribs)
            else:
                dcv, sb, acc = _bwd_a2(*a2_args)
                sg["a_ln_g", j], sg["a_ln_b", j], sg["a_b_dw", j] = acc[0], acc[1], acc[2]
                dglu, dwdw = _bwd_conv_a(dcv, *conv_args)
            wgrad("a_w_pw2", j, sb, "c", dz1, "1")
            sg["a_w_dw", j] = dwdw[:CONV_A]
            g, dh, acc = _bwd_a1(dglu, h, dz1, gw["a_w_pw1", j], i)
            sg["a_b_pw1", j] = acc[0]
            wgrad("a_w_pw1", j, x0, "1", dh, "c")
        elif mix == 1:
            (h,) = saved[i]
            g, dh, mb, acc, dw_s, db_s = _bwd_b(dz1, h, vec("b_ln_g", 0), vec("b_ln_b", 0), gw["b_w_in", 0],
                                                gw["b_w_out", 0], ws, wst, bsx)
            sg["b_b_in", 0], sg["b_ln_g", 0], sg["b_ln_b", 0] = acc[0], acc[1, :E], acc[2, :E]
            sg["b_w_s", 0], sg["b_b_s", 0] = dw_s, jnp.sum(db_s, axis=-1)
            wgrad("b_w_out", 0, mb, "c", dz1, "1")
            wgrad("b_w_in", 0, x0, "1", dh, "c")
        else:
            (hc,) = saved[i]
            wc = conv_w("c_w_conv", 0, 8)
            dy, dbg, mb = _bwd_c2(dz1, hc, wc, gw["c_w_out", 0])
            wgrad("c_w_out", 0, mb, "c", dz1, "1")
            g, dhc, dwc = _bwd_c1(dy, hc, dbg, dz1, wc, gw["c_w_in", 0])
            sg["c_w_conv", 0] = dwc[:CONV_C]
            wgrad("c_w_in", 0, x0, "1", dhc, "c")
        pending = mixw[i] + ffnw[i] + plew[i] if i > 0 else mixw[0]
    grad_x = g[None]
    add_halves(pending, _comm_only("swap_last", swap(pending)))
    update(pending, _comm_only("scatter_last", _ScatterPartials([parts[key] for key in pending])))

    g8_late = _gather8("gather_small_late", _pack([sg[pc] for pc in late_small]))
    sums = dict(zip(early_small, _unpack(_sum8("sum8_early", g8_early), [sg[pc].shape for pc in early_small])))
    sums.update(zip(late_small, _unpack(_sum8("sum8_late", g8_late), [sg[pc].shape for pc in late_small])))
    gsum = [jnp.stack([sums[k, l] for l in range(full[k].shape[0])]) for k in small]
    gmine = []
    for k, gs in zip(small, gsum):
        if k in SMALL_SHARDED:
            wdt = wts[k].shape[-1]
            gs = lax.dynamic_slice_in_dim(gs, q_idx * wdt, wdt, axis=gs.ndim - 1)
        gmine.append(gs)
    packed = [_pack(t)[None] for t in ([wts[k] for k in small], [mom[k] for k in small], [var[k] for k in small])]
    outs = _adam("adam_small", packed[0], packed[1], packed[2], _pack(gmine)[None, None], 0, None)
    unpacked = [_unpack(o[0], [wts[k].shape for k in small]) for o in outs]
    for i, k in enumerate(small):
        res[k] = tuple(u[i] for u in unpacked)

    for k in TRANSPOSED:
        res[k] = tuple(jnp.transpose(r, (0, 2, 1)) for r in res[k])
    return (loss, grad_x, *[res[k][0] for k in WEIGHTS], *[res[k][1] for k in WEIGHTS],
            *[res[k][2] for k in WEIGHTS], *[res[k][3] for k in WEIGHTS])
```

```python
import jax
import jax.numpy as jnp
from jax import lax
from jax.experimental import pallas as pl
from jax.experimental.pallas import tpu as pltpu

F32, BF16 = jnp.float32, jnp.bfloat16
S = 4096
D = 1024
E = 2048
FF = 2816
FQ = FF // 4
NQ = 4
DEPTH = 4
ALPHA = (2 * DEPTH) ** 0.25
LN_EPS = 1e-5
CONV_A, CONV_C = 31, 3
HALO_A, HALO_C = 32, 8
SGU_T, SGU_H, SGU_G, SGU_CHUNK = 128, 8, 256, 64
VMEM_LIMIT = 56 * 1024 * 1024
DW_BLOCK_BUDGET = 40 * 1024 * 1024
MESH = pl.DeviceIdType.MESH
ADAM_LR, ADAM_B1, ADAM_B2, ADAM_EPS, ADAM_WD, ADAM_STEP = 0.001, 0.9, 0.999, 1e-08, 0.01, 10
GELU_C, GELU_A = 0.7978845608028654, 0.044715

BIG = ["a_w_pw1", "a_w_pw2", "b_w_in", "b_w_out", "c_w_in", "c_w_out",
       "ffn_w_gate", "ffn_w_up", "ffn_w_down", "ple_w_gate", "ple_w_proj"]
TRANSPOSED = ["ffn_w_gate", "ffn_w_up"]
ROW_SHARDED = ["a_w_pw2", "b_w_out", "c_w_out", "ffn_w_gate", "ffn_w_up", "ffn_w_down", "ple_w_gate"]
SMALL_SHARDED = ["a_b_pw1", "a_w_dw", "a_b_dw", "a_ln_g", "a_ln_b", "c_w_conv"]
SMALL_REPL = ["b_b_in", "b_ln_g", "b_ln_b", "b_w_s", "b_b_s", "ln1_g", "ln1_b", "ln2_g", "ln2_b", "ple_norm_g"]
WEIGHTS = ["a_w_pw1", "a_b_pw1", "a_w_dw", "a_b_dw", "a_ln_g", "a_ln_b", "a_w_pw2", "b_w_in", "b_b_in", "b_ln_g",
           "b_ln_b", "b_w_s", "b_b_s", "b_w_out", "c_w_in", "c_w_conv", "c_w_out", "ln1_g", "ln1_b", "ln2_g",
           "ln2_b", "ffn_w_gate", "ffn_w_up", "ffn_w_down", "ple_w_gate", "ple_w_proj", "ple_norm_g"]


def _call(name, body, grid, in_specs, out_specs, out_shape, scratch=(), aliases=None, hosts=()):
    params = pltpu.CompilerParams(dimension_semantics=("arbitrary",) * len(grid), vmem_limit_bytes=VMEM_LIMIT)
    if not hosts:
        return pl.pallas_call(
            body, name=name, grid=grid, in_specs=in_specs, out_specs=out_specs, out_shape=out_shape,
            scratch_shapes=list(scratch), input_output_aliases=aliases or {}, compiler_params=params)
    assert len(grid) == 1 and not aliases
    single = not isinstance(out_shape, (list, tuple))
    own_shapes = [out_shape] if single else list(out_shape)
    own_specs = [out_specs] if single else list(out_specs)
    n_in, n_out, n_scr = len(in_specs), len(own_shapes), len(scratch)
    h_in = [len(h.arrays) for h in hosts]
    h_out = [len(h.out_shapes) for h in hosts]
    h_sem = [len(h.sems) for h in hosts]

    def split(refs, counts):
        out, off = [], 0
        for cnt in counts:
            out.append(refs[off:off + cnt])
            off += cnt
        return out

    def wrapped(*refs):
        ins, hin, outs, hout, scr, hsem = split(refs, [n_in, sum(h_in), n_out, sum(h_out), n_scr, sum(h_sem)])
        per_host = list(zip(hosts, split(hin, h_in), split(hout, h_out), split(hsem, h_sem)))

        @pl.when(pl.program_id(0) == 0)
        def _():
            for h, a, o, s in per_host:
                h.start(a, o, s)

        body(*ins, *outs, *scr)

        @pl.when(pl.program_id(0) == grid[0] - 1)
        def _():
            for h, a, o, s in per_host:
                h.finish(a, o, s)

    any_spec = pl.BlockSpec(memory_space=pl.ANY)
    call = pl.pallas_call(
        wrapped, name=name, grid=grid, in_specs=list(in_specs) + [any_spec] * sum(h_in),
        out_specs=own_specs + [any_spec] * sum(h_out),
        out_shape=own_shapes + [s for h in hosts for s in h.out_shapes],
        scratch_shapes=list(scratch) + [s for h in hosts for s in h.sems], compiler_params=params)

    def run(*args):
        res = call(*args, *[a for h in hosts for a in h.arrays])
        own = res[0] if single else list(res[:n_out])
        return own, split(list(res[n_out:]), h_out)

    return run


def _sds(shape, dtype=F32):
    return jax.ShapeDtypeStruct(shape, dtype)


def _row(tm, c):
    return pl.BlockSpec((tm, c), lambda i: (i, 0))


def _grow(g, tm, c):
    return pl.BlockSpec((g, tm, c), lambda i: (0, i, 0))


def _const(shape):
    nd = len(shape)
    return pl.BlockSpec(shape, lambda i: (0,) * nd, pipeline_mode=pl.Buffered(1))


def _wspec(w):
    return pl.BlockSpec((NQ, None, w.shape[2], w.shape[3]), lambda i: (0, 0, 0, 0), pipeline_mode=pl.Buffered(1))


def _prev(tm, hb, c):
    return pl.BlockSpec((hb, c), lambda i: (jnp.maximum(i * (tm // hb) - 1, 0), 0))


def _next(tm, hb, c):
    return pl.BlockSpec((hb, c), lambda i: (jnp.minimum((i + 1) * (tm // hb), S // hb - 1), 0))


def _acc(r, c):
    return pl.BlockSpec((r, c), lambda i: (0, 0))


def _sig(x):
    return 1.0 / (1.0 + jnp.exp(-x))


def _ln(z, g, b):
    mu = jnp.mean(z, axis=-1, keepdims=True)
    zc = z - mu
    rstd = lax.rsqrt(jnp.mean(zc * zc, axis=-1, keepdims=True) + LN_EPS)
    xhat = zc * rstd
    return xhat * g + b, xhat, rstd


def _ln_bwd(dyg, xhat, rstd):
    return rstd * (dyg - jnp.mean(dyg, axis=-1, keepdims=True) - xhat * jnp.mean(dyg * xhat, axis=-1, keepdims=True))


def _mm(a, w):
    return jnp.dot(a.astype(BF16), w, preferred_element_type=F32)


def _mmt(a, w):
    return lax.dot_general(a.astype(BF16), w, (((1,), (1,)), ((), ())), preferred_element_type=F32)


def _colsum(x):
    return jnp.sum(x, axis=0, keepdims=True)


def _gelu(x):
    t = jnp.tanh(GELU_C * (x + GELU_A * x * x * x))
    return 0.5 * x * (1.0 + t), t


def _gelu_grad(x, t):
    return 0.5 * (1.0 + t) + 0.5 * x * (1.0 - t * t) * GELU_C * (1.0 + 3.0 * GELU_A * x * x)


def _silu_grad(a, sg):
    return sg * (1.0 + a * (1.0 - sg))


def _sgu_masks():
    r = lax.broadcasted_iota(jnp.int32, (SGU_T, SGU_T), 0) // SGU_CHUNK
    c = lax.broadcasted_iota(jnp.int32, (SGU_T, SGU_T), 1) // SGU_CHUNK
    return r >= c, c >= r


def _fill_halo(buf, lo, n, halo_val_fn, is_edge):
    @pl.when(is_edge)
    def _():
        buf[lo:lo + n, :] = jnp.zeros((n, buf.shape[1]), F32)

    @pl.when(jnp.logical_not(is_edge))
    def _():
        buf[lo:lo + n, :] = halo_val_fn()


SUB, LANE = 8, 128
ROWS_AT_ONCE = 16


def _shift_copies(buf, sh):
    rows = sh.shape[1]
    for s in range(1, SUB):
        sh[s - 1, :, :] = buf[pl.ds(s, rows), :]


def _tiles(buf, sh, s, first, count, group0, lanes):
    src = buf if s == 0 else sh.at[s - 1]
    return {t: src[pl.ds(pl.multiple_of((group0 + t) * SUB, SUB), SUB), lanes] for t in range(first, first + count)}


def _by_shift(offsets):
    out = []
    for s in range(SUB):
        taps = [(k, o // SUB) for k, o in enumerate(offsets) if o % SUB == s]
        if taps:
            out.append((s, taps))
    return out


def _conv_rows(out_ref, w_ref, bias_ref, offsets, buf, sh, tm):
    n = ROWS_AT_ONCE
    for cb in range(D // LANE):
        lanes = slice(cb * LANE, (cb + 1) * LANE)
        bias = None if bias_ref is None else jnp.broadcast_to(bias_ref[:, lanes], (SUB, LANE))

        def body(jb, carry):
            accs = [bias] * n
            for s, taps in _by_shift(offsets):
                ms = [m for _, m in taps]
                tiles = _tiles(buf, sh, s, min(ms), max(ms) - min(ms) + n, jb * n, lanes)
                for k, m in taps:
                    wk = jnp.broadcast_to(w_ref[k:k + 1, lanes], (SUB, LANE))
                    for jj in range(n):
                        t = wk * tiles[m + jj]
                        accs[jj] = t if accs[jj] is None else accs[jj] + t
            for jj in range(n):
                out_ref[pl.ds(pl.multiple_of((jb * n + jj) * SUB, SUB), SUB), lanes] = accs[jj]
            return carry

        lax.fori_loop(0, tm // (SUB * n), body, 0)


def _conv_wgrad(dw_ref, d_ref, offsets, buf, sh, tm):
    n = 4
    for cb in range(D // LANE):
        lanes = slice(cb * LANE, (cb + 1) * LANE)

        def body(jq, accs):
            accs = list(accs)
            d = [d_ref[pl.ds(pl.multiple_of((jq * n + jj) * SUB, SUB), SUB), lanes] for jj in range(n)]
            for s, taps in _by_shift(offsets):
                ms = [m for _, m in taps]
                tiles = _tiles(buf, sh, s, min(ms), max(ms) - min(ms) + n, jq * n, lanes)
                for k, m in taps:
                    for jj in range(n):
                        accs[k] = accs[k] + d[jj] * tiles[m + jj]
            return tuple(accs)

        accs = lax.fori_loop(0, tm // (SUB * n), body, tuple(jnp.zeros((SUB, LANE), F32) for _ in offsets))
        for k, acc in enumerate(accs):
            dw_ref[k:k + 1, lanes] += jnp.sum(acc, axis=0, keepdims=True)


def _fwd_a1(x0, w1, b1, l, hosts=()):
    tm = 512

    def body(x_ref, w_ref, b_ref, h_ref, glu_ref):
        xb = x_ref[...].astype(BF16)
        for q in range(NQ):
            sl = slice(q * 512, (q + 1) * 512)
            h_ref[:, sl] = jnp.dot(xb, w_ref[q], preferred_element_type=F32) + b_ref[:, sl]
        glu_ref[...] = h_ref[:, :D] * _sig(h_ref[:, D:])

    return _call(f"fwd_a1_{l}", body, (S // tm,), [_row(tm, D), _wspec(w1), _const((1, 2 * D))],
                 [_row(tm, 2 * D), _row(tm, D)], [_sds((S, 2 * D)), _sds((S, D))], hosts=hosts)(x0, w1, b1)


def _fwd_a2(glu, x0, wdw, bdw, lg, lb, w2, l, hosts=()):
    tm = 256

    def body(g_ref, gp_ref, x_ref, wdw_ref, bdw_ref, lg_ref, lb_ref, w2_ref, z_ref, cv_ref, buf, sh):
        i = pl.program_id(0)
        _fill_halo(buf, 0, HALO_A, lambda: gp_ref[...], i == 0)
        buf[HALO_A:HALO_A + tm, :] = g_ref[...]
        _shift_copies(buf, sh)
        _conv_rows(cv_ref, wdw_ref, bdw_ref, [HALO_A - (CONV_A - 1) + k for k in range(CONV_A)], buf, sh, tm)
        n, _, _ = _ln(cv_ref[...], lg_ref[...], lb_ref[...])
        sb = (n * _sig(n)).astype(BF16)
        z_ref[...] = ALPHA * x_ref[...] + jnp.dot(sb, w2_ref[...], preferred_element_type=F32)

    return _call(f"fwd_a2_{l}", body, (S // tm,),
                 [_row(tm, D), _prev(tm, HALO_A, D), _row(tm, D), _const((32, D)), _const((1, D)), _const((1, D)),
                  _const((1, D)), _const(w2.shape)],
                 [_row(tm, D), _row(tm, D)], [_sds((S, D)), _sds((S, D))],
                 scratch=[pltpu.VMEM((HALO_A + tm, D), F32), pltpu.VMEM((SUB - 1, HALO_A + tm - SUB, D), F32)],
                 hosts=hosts)(glu, glu, x0, wdw, bdw, lg, lb, w2)


def _fwd_b(x0, win, b_in, lg, lb, ws, bsx, wout, hosts=()):
    tm = 256

    def body(x_ref, win_ref, bin_ref, lg_ref, lb_ref, ws_ref, bsx_ref, wout_ref, z_ref, zg_ref, gg_ref, f_scr, h_ref):
        xb = x_ref[...].astype(BF16)
        for q in range(NQ):
            sl = slice(q * 1024, (q + 1) * 1024)
            h_ref[:, sl] = jnp.dot(xb, win_ref[q], preferred_element_type=F32) + bin_ref[:, sl]
        u, tu = _gelu(h_ref[:, :E])
        v, tv = _gelu(h_ref[:, E:])
        zg_ref[:, 0:E] = u.astype(BF16)
        zg_ref[:, E:2 * E] = v.astype(BF16)
        gg_ref[:, 0:E] = _gelu_grad(h_ref[:, :E], tu).astype(BF16)
        gg_ref[:, E:2 * E] = _gelu_grad(h_ref[:, E:], tv).astype(BF16)
        vn, _, _ = _ln(v, lg_ref[...], lb_ref[...])
        vnb = vn.astype(BF16)
        mask, _ = _sgu_masks()
        for hd in range(SGU_H):
            wm = jnp.where(mask, ws_ref[hd], 0.0).astype(BF16)
            cs = slice(hd * SGU_G, (hd + 1) * SGU_G)
            for n in range(tm // SGU_T):
                rs = slice(n * SGU_T, (n + 1) * SGU_T)
                f_scr[rs, cs] = jnp.dot(wm, vnb[rs, cs], preferred_element_type=F32) + bsx_ref[hd]
        mb = (u * f_scr[...]).astype(BF16)
        z_ref[...] = ALPHA * x_ref[...] + jnp.dot(mb, wout_ref[...], preferred_element_type=F32)

    return _call("fwd_b", body, (S // tm,),
                 [_row(tm, D), _wspec(win), _const((1, 2 * E)), _const((1, E)), _const((1, E)),
                  _const((SGU_H, SGU_T, SGU_T)), _const((SGU_H, SGU_T, SGU_G)), _const(wout.shape)],
                 [_row(tm, D), _row(tm, 2 * E), _row(tm, 2 * E)],
                 [_sds((S, D)), _sds((S, 2 * E), BF16), _sds((S, 2 * E), BF16)],
                 scratch=[pltpu.VMEM((tm, E), F32), pltpu.VMEM((tm, 2 * E), F32)], hosts=hosts
                 )(x0, win, b_in, lg, lb, ws, bsx, wout)


def _fwd_c1(x0, win, hosts=()):
    tm = 512

    def body(x_ref, w_ref, hc_ref):
        xb = x_ref[...].astype(BF16)
        for q in range(NQ):
            hc_ref[:, q * 768:(q + 1) * 768] = jnp.dot(xb, w_ref[q], preferred_element_type=F32)

    return _call("fwd_c1", body, (S // tm,), [_row(tm, D), _wspec(win)], _row(tm, 3 * D),
                 _sds((S, 3 * D)), hosts=hosts)(x0, win)


def _short_conv(buf, hc_ref, hcp_ref, wc_ref, tm, i):
    _fill_halo(buf, 0, HALO_C, lambda: hcp_ref[:, D:2 * D] * hcp_ref[:, 2 * D:], i == 0)
    buf[HALO_C:HALO_C + tm, :] = hc_ref[:, D:2 * D] * hc_ref[:, 2 * D:]
    y = wc_ref[0:1, :] * buf[pl.ds(HALO_C - 2, tm), :]
    for k in range(1, CONV_C):
        y = y + wc_ref[k:k + 1, :] * buf[pl.ds(HALO_C - 2 + k, tm), :]
    return y


def _fwd_c2(hc, x0, wc, wout, hosts=()):
    tm = 256

    def body(hc_ref, hcp_ref, x_ref, wc_ref, wout_ref, z_ref, buf):
        y = _short_conv(buf, hc_ref, hcp_ref, wc_ref, tm, pl.program_id(0))
        mb = (hc_ref[:, :D] * y).astype(BF16)
        z_ref[...] = ALPHA * x_ref[...] + jnp.dot(mb, wout_ref[...], preferred_element_type=F32)

    return _call("fwd_c2", body, (S // tm,),
                 [_row(tm, 3 * D), _prev(tm, HALO_C, 3 * D), _row(tm, D), _const((8, D)), _const(wout.shape)],
                 _row(tm, D), _sds((S, D)), scratch=[pltpu.VMEM((HALO_C + tm, D), F32)], hosts=hosts
                 )(hc, hc, x0, wc, wout)


def _fwd_ffn(z1, lg, lb, wgt, wut, wd, l, hosts=()):
    tm = 256

    def body(z_ref, lg_ref, lb_ref, wg_ref, wu_ref, wd_ref, o_ref, a_ref, u_ref, hm_ref):
        x1, _, _ = _ln(z_ref[...], lg_ref[...], lb_ref[...])
        xb = x1.astype(BF16)
        a = _mmt(xb, wg_ref[...])
        u = _mmt(xb, wu_ref[...])
        hmb = (a * _sig(a) * u).astype(BF16)
        a_ref[...] = a.astype(BF16)
        u_ref[...] = u.astype(BF16)
        hm_ref[...] = hmb
        o_ref[...] = ALPHA * x1 + jnp.dot(hmb, wd_ref[...], preferred_element_type=F32)

    return _call(f"fwd_ffn_{l}", body, (S // tm,),
                 [_row(tm, D), _const((1, D)), _const((1, D)), _const((FF, D)), _const((FF, D)), _const((FF, D))],
                 [_row(tm, D), _row(tm, FF), _row(tm, FF), _row(tm, FF)],
                 [_sds((S, D)), _sds((S, FF), BF16), _sds((S, FF), BF16), _sds((S, FF), BF16)],
                 hosts=hosts)(z1, lg, lb, wgt, wut, wd)


def _ple_parts(z2, p, lg, lb, wg_ref, wp_ref, pg):
    x2, xhat, rstd = _ln(z2, lg, lb)
    xb = x2.astype(BF16)
    gate = _sig(jnp.dot(xb, wg_ref[...], preferred_element_type=F32))
    pb = p.astype(BF16)
    qp = jnp.concatenate([jnp.dot(pb, wp_ref[q], preferred_element_type=F32) for q in range(NQ)], axis=1)
    rs = lax.rsqrt(jnp.mean(qp * qp, axis=-1, keepdims=True) + LN_EPS)
    qn = qp * rs
    return x2, xhat, rstd, xb, gate, qn, rs, qn * pg


def _fwd_ple(z2, p, lg, lb, wg, wp, pg, l, hosts=()):
    tm = 512

    def body(z_ref, p_ref, lg_ref, lb_ref, wg_ref, wp_ref, pg_ref, o_ref):
        x2, _, _, _, gate, _, _, r = _ple_parts(z_ref[...], p_ref[...], lg_ref[...], lb_ref[...], wg_ref, wp_ref,
                                                pg_ref[...])
        o_ref[...] = x2 + gate * r

    return _call(f"fwd_ple_{l}", body, (S // tm,),
                 [_row(tm, D), _row(tm, 256), _const((1, D)), _const((1, D)), _const(wg.shape), _wspec(wp),
                  _const((1, D))],
                 _row(tm, D), _sds((S, D)), hosts=hosts)(z2, p, lg, lb, wg, wp, pg)


def _loss_head(y, target):
    tm = 512

    def body(y_ref, t_ref, dy_ref, acc_ref):
        @pl.when(pl.program_id(0) == 0)
        def _():
            acc_ref[...] = jnp.zeros_like(acc_ref)

        e = y_ref[...] - t_ref[...]
        dy_ref[...] = e * (1.0 / D)
        acc_ref[0:1, :] += _colsum(e * e)

    return _call("loss_head", body, (S // tm,), [_row(tm, D), _row(tm, D)], [_row(tm, D), _acc(8, D)],
                 [_sds((S, D)), _sds((8, D))])(y, target)


def _zero_first(*refs):
    @pl.when(pl.program_id(0) == 0)
    def _():
        for r in refs:
            r[...] = jnp.zeros_like(r)


def _bwd_ple(g, z2, p, lg, lb, wg, wp, pg, l, hosts=()):
    tm = 256

    def body(g_ref, z_ref, p_ref, lg_ref, lb_ref, wg_ref, wp_ref, pg_ref, dz_ref, xb_ref, dgp_ref, dqp_ref, acc_ref):
        _zero_first(acc_ref)
        gin = g_ref[...]
        lgv, pgv = lg_ref[...], pg_ref[...]
        _, xhat, rstd, xb, gate, qn, rs, r = _ple_parts(z_ref[...], p_ref[...], lgv, lb_ref[...], wg_ref, wp_ref, pgv)
        xb_ref[...] = xb
        dgpb = (gin * r * gate * (1.0 - gate)).astype(BF16)
        dgp_ref[...] = dgpb
        dx2 = gin + _mmt(dgpb, wg_ref[...])
        dr = gin * gate
        acc_ref[0:1, :] += _colsum(dr * qn)
        t = dr * pgv
        dqp_ref[...] = (rs * (t - qn * jnp.mean(t * qn, axis=-1, keepdims=True))).astype(BF16)
        acc_ref[1:2, :] += _colsum(dx2 * xhat)
        acc_ref[2:3, :] += _colsum(dx2)
        dz_ref[...] = _ln_bwd(dx2 * lgv, xhat, rstd)

    return _call(f"bwd_ple_{l}", body, (S // tm,),
                 [_row(tm, D), _row(tm, D), _row(tm, 256), _const((1, D)), _const((1, D)), _const(wg.shape),
                  _wspec(wp), _const((1, D))],
                 [_row(tm, D), _row(tm, D), _row(tm, D), _row(tm, D), _acc(8, D)],
                 [_sds((S, D)), _sds((S, D), BF16), _sds((S, D), BF16), _sds((S, D), BF16), _sds((8, D))],
                 hosts=hosts)(g, z2, p, lg, lb, wg, wp, pg)


def _bwd_ffn(dz2, z1, ab, ub, lg, lb, wgt, wut, wd, l, hosts=()):
    tm = 256

    def body(dz2_ref, z_ref, a_ref, u_ref, lg_ref, lb_ref, wg_ref, wu_ref, wd_ref, dz1_ref, xb_ref, da_ref, du_ref,
             acc_ref):
        _zero_first(acc_ref)
        dz2v = dz2_ref[...]
        lgv = lg_ref[...]
        x1, xhat, rstd = _ln(z_ref[...], lgv, lb_ref[...])
        xb_ref[...] = x1.astype(BF16)
        a = a_ref[...].astype(F32)
        u = u_ref[...].astype(F32)
        sg = _sig(a)
        dhm = _mmt(dz2v, wd_ref[...])
        dub = (dhm * (a * sg)).astype(BF16)
        dab = (dhm * u * _silu_grad(a, sg)).astype(BF16)
        da_ref[...] = dab
        du_ref[...] = dub
        dx1 = ALPHA * dz2v + _mm(dab, wg_ref[...]) + _mm(dub, wu_ref[...])
        acc_ref[0:1, :] += _colsum(dx1 * xhat)
        acc_ref[1:2, :] += _colsum(dx1)
        dz1_ref[...] = _ln_bwd(dx1 * lgv, xhat, rstd)

    return _call(f"bwd_ffn_{l}", body, (S // tm,),
                 [_row(tm, D), _row(tm, D), _row(tm, FF), _row(tm, FF), _const((1, D)), _const((1, D)),
                  _const((FF, D)), _const((FF, D)), _const((FF, D))],
                 [_row(tm, D), _row(tm, D), _row(tm, FF), _row(tm, FF), _acc(8, D)],
                 [_sds((S, D)), _sds((S, D), BF16), _sds((S, FF), BF16), _sds((S, FF), BF16), _sds((8, D))],
                 hosts=hosts)(dz2, z1, ab, ub, lg, lb, wgt, wut, wd)


def _bwd_a2(dz1, cv, lg, lb, w2, l, hosts=()):
    tm = 512

    def body(dz_ref, cv_ref, lg_ref, lb_ref, w2_ref, dcv_ref, sb_ref, acc_ref):
        _zero_first(acc_ref)
        lgv = lg_ref[...]
        n, xhat, rstd = _ln(cv_ref[...], lgv, lb_ref[...])
        sg = _sig(n)
        sb_ref[...] = (n * sg).astype(BF16)
        dzb = dz_ref[...].astype(BF16)
        ds = _mmt(dzb, w2_ref[...])
        dn = ds * _silu_grad(n, sg)
        acc_ref[0:1, :] += _colsum(dn * xhat)
        acc_ref[1:2, :] += _colsum(dn)
        dcv = _ln_bwd(dn * lgv, xhat, rstd)
        acc_ref[2:3, :] += _colsum(dcv)
        dcv_ref[...] = dcv

    return _call(f"bwd_a2_{l}", body, (S // tm,),
                 [_row(tm, D), _row(tm, D), _const((1, D)), _const((1, D)), _const(w2.shape)],
                 [_row(tm, D), _row(tm, D), _acc(8, D)],
                 [_sds((S, D)), _sds((S, D), BF16), _sds((8, D))], hosts=hosts)(dz1, cv, lg, lb, w2)


def _bwd_conv_a(dcv, glu, wdw, l, hosts=()):
    tm = 256
    nb = S // tm

    def body(d_ref, dn_ref, g_ref, gp_ref, w_ref, dglu_ref, dw_ref, bufd, bufx, sh):
        i = pl.program_id(0)
        _zero_first(dw_ref)
        bufd[0:tm, :] = d_ref[...]
        _fill_halo(bufd, tm, HALO_A, lambda: dn_ref[...], i == nb - 1)
        _fill_halo(bufx, 0, HALO_A, lambda: gp_ref[...], i == 0)
        bufx[HALO_A:HALO_A + tm, :] = g_ref[...]
        _shift_copies(bufd, sh)
        _conv_rows(dglu_ref, w_ref, None, [CONV_A - 1 - k for k in range(CONV_A)], bufd, sh, tm)
        _shift_copies(bufx, sh)
        _conv_wgrad(dw_ref, d_ref, [HALO_A - (CONV_A - 1) + k for k in range(CONV_A)], bufx, sh, tm)

    return _call(f"bwd_conv_a_{l}", body, (nb,),
                 [_row(tm, D), _next(tm, HALO_A, D), _row(tm, D), _prev(tm, HALO_A, D), _const((32, D))],
                 [_row(tm, D), _acc(32, D)], [_sds((S, D)), _sds((32, D))],
                 scratch=[pltpu.VMEM((tm + HALO_A, D), F32), pltpu.VMEM((HALO_A + tm, D), F32),
                          pltpu.VMEM((SUB - 1, HALO_A + tm - SUB, D), F32)], hosts=hosts)(dcv, dcv, glu, glu, wdw)


def _bwd_a1(dglu, h, dz1, w1, l, hosts=()):
    tm = 256

    def body(dg_ref, h_ref, dz_ref, w_ref, dx_ref, dh_ref, acc_ref):
        _zero_first(acc_ref)
        a, g = h_ref[:, :D], h_ref[:, D:]
        sg = _sig(g)
        dgl = dg_ref[...]
        da = dgl * sg
        dg = dgl * a * sg * (1.0 - sg)
        acc_ref[0:1, 0:D] += _colsum(da)
        acc_ref[0:1, D:2 * D] += _colsum(dg)
        dh_ref[:, 0:D] = da.astype(BF16)
        dh_ref[:, D:2 * D] = dg.astype(BF16)
        dx = ALPHA * dz_ref[...]
        for q in range(NQ):
            dx = dx + _mmt(dh_ref[:, q * 512:(q + 1) * 512], w_ref[q])
        dx_ref[...] = dx

    return _call(f"bwd_a1_{l}", body, (S // tm,),
                 [_row(tm, D), _row(tm, 2 * D), _row(tm, D), _wspec(w1)],
                 [_row(tm, D), _row(tm, 2 * D), _acc(8, 2 * D)],
                 [_sds((S, D)), _sds((S, 2 * D), BF16), _sds((8, 2 * D))], hosts=hosts)(dglu, h, dz1, w1)


def _bwd_c2(dz1, hc, wc, wout):
    tm = 256

    def body(dz_ref, hc_ref, hcp_ref, wc_ref, wout_ref, dy_ref, dbg_ref, mb_ref, buf):
        y = _short_conv(buf, hc_ref, hcp_ref, wc_ref, tm, pl.program_id(0))
        dzb = dz_ref[...].astype(BF16)
        dm = _mmt(dzb, wout_ref[...])
        bg = hc_ref[:, :D]
        mb_ref[...] = (bg * y).astype(BF16)
        dbg_ref[...] = (dm * y).astype(BF16)
        dy_ref[...] = dm * bg

    return _call("bwd_c2", body, (S // tm,),
                 [_row(tm, D), _row(tm, 3 * D), _prev(tm, HALO_C, 3 * D), _const((8, D)), _const(wout.shape)],
                 [_row(tm, D), _row(tm, D), _row(tm, D)],
                 [_sds((S, D)), _sds((S, D), BF16), _sds((S, D), BF16)],
                 scratch=[pltpu.VMEM((HALO_C + tm, D), F32)])(dz1, hc, hc, wc, wout)


def _bwd_c1(dy, hc, dbg, dz1, wc, win):
    tm = 256
    nb = S // tm

    def body(d_ref, dn_ref, hc_ref, hcp_ref, dbg_ref, dz_ref, wc_ref, win_ref, dx_ref, dhc_ref, dwc_ref, bufd, bufq):
        i = pl.program_id(0)
        _zero_first(dwc_ref)
        bufd[0:tm, :] = d_ref[...]
        _fill_halo(bufd, tm, HALO_C, lambda: dn_ref[...], i == nb - 1)
        _fill_halo(bufq, 0, HALO_C, lambda: hcp_ref[:, D:2 * D] * hcp_ref[:, 2 * D:], i == 0)
        bufq[HALO_C:HALO_C + tm, :] = hc_ref[:, D:2 * D] * hc_ref[:, 2 * D:]
        dq = wc_ref[0:1, :] * bufd[pl.ds(CONV_C - 1, tm), :]
        for k in range(1, CONV_C):
            dq = dq + wc_ref[k:k + 1, :] * bufd[pl.ds(CONV_C - 1 - k, tm), :]
        dv = d_ref[...]
        for k in range(CONV_C):
            dwc_ref[k:k + 1, :] += _colsum(dv * bufq[pl.ds(HALO_C - (CONV_C - 1) + k, tm), :])
        dhc_ref[:, 0:D] = dbg_ref[...]
        dhc_ref[:, D:2 * D] = (dq * hc_ref[:, 2 * D:]).astype(BF16)
        dhc_ref[:, 2 * D:3 * D] = (dq * hc_ref[:, D:2 * D]).astype(BF16)
        dx = ALPHA * dz_ref[...]
        for q in range(NQ):
            dx = dx + _mmt(dhc_ref[:, q * 768:(q + 1) * 768], win_ref[q])
        dx_ref[...] = dx

    return _call("bwd_c1", body, (nb,),
                 [_row(tm, D), _next(tm, HALO_C, D), _row(tm, 3 * D), _prev(tm, HALO_C, 3 * D), _row(tm, D),
                  _row(tm, D), _const((8, D)), _wspec(win)],
                 [_row(tm, D), _row(tm, 3 * D), _acc(8, D)],
                 [_sds((S, D)), _sds((S, 3 * D), BF16), _sds((8, D))],
                 scratch=[pltpu.VMEM((tm + HALO_C, D), F32), pltpu.VMEM((HALO_C + tm, D), F32)]
                 )(dy, dy, hc, hc, dbg, dz1, wc, win)


def _bwd_b(dz1, zg, gg, lg, lb, win, wout, ws, wst, bsx):
    tm = 128
    nb = S // tm

    def body(dz_ref, zg_ref, gg_ref, lg_ref, lb_ref, win_ref, wout_ref, ws_ref, wst_ref, bsx_ref,
             dx_ref, dh_ref, mb_ref, acc_ref, dws_ref, dbs_ref, f_scr, dvn_scr):
        _zero_first(acc_ref, dws_ref, dbs_ref)
        lgv = lg_ref[...]
        u = zg_ref[:, :E].astype(F32)
        v = zg_ref[:, E:].astype(F32)
        vn, xhat, rstd = _ln(v, lgv, lb_ref[...])
        vnb = vn.astype(BF16)
        dzb = dz_ref[...].astype(BF16)
        dm = _mmt(dzb, wout_ref[...])
        mask, mask_t = _sgu_masks()
        for hd in range(SGU_H):
            wm = jnp.where(mask, ws_ref[hd], 0.0).astype(BF16)
            cs = slice(hd * SGU_G, (hd + 1) * SGU_G)
            for n in range(tm // SGU_T):
                rs = slice(n * SGU_T, (n + 1) * SGU_T)
                f_scr[rs, cs] = jnp.dot(wm, vnb[rs, cs], preferred_element_type=F32) + bsx_ref[hd]
        f = f_scr[...]
        mb_ref[...] = (u * f).astype(BF16)
        du = dm * f
        df = dm * u
        dfb = df.astype(BF16)
        for hd in range(SGU_H):
            wmt = jnp.where(mask_t, wst_ref[hd], 0.0).astype(BF16)
            cs = slice(hd * SGU_G, (hd + 1) * SGU_G)
            for n in range(tm // SGU_T):
                rs = slice(n * SGU_T, (n + 1) * SGU_T)
                dvn_scr[rs, cs] = jnp.dot(wmt, dfb[rs, cs], preferred_element_type=F32)
                dws_ref[hd] += lax.dot_general(dfb[rs, cs], vnb[rs, cs], (((1,), (1,)), ((), ())),
                                               preferred_element_type=F32)
                dbs_ref[hd] += df[rs, cs]
        dvn = dvn_scr[...]
        acc_ref[1:2, 0:E] += _colsum(dvn * xhat)
        acc_ref[2:3, 0:E] += _colsum(dvn)
        dv = _ln_bwd(dvn * lgv, xhat, rstd)
        dhu = du * gg_ref[:, :E].astype(F32)
        dhv = dv * gg_ref[:, E:].astype(F32)
        acc_ref[0:1, 0:E] += _colsum(dhu)
        acc_ref[0:1, E:2 * E] += _colsum(dhv)
        dh_ref[:, 0:E] = dhu.astype(BF16)
        dh_ref[:, E:2 * E] = dhv.astype(BF16)
        dx = ALPHA * dz_ref[...]
        for q in range(NQ):
            dx = dx + _mmt(dh_ref[:, q * 1024:(q + 1) * 1024], win_ref[q])
        dx_ref[...] = dx

        @pl.when(pl.program_id(0) == nb - 1)
        def _():
            for hd in range(SGU_H):
                dws_ref[hd] = jnp.where(mask, dws_ref[hd], 0.0)

    c3 = lambda a, b, c: pl.BlockSpec((a, b, c), lambda i: (0, 0, 0))
    return _call("bwd_b", body, (nb,),
                 [_row(tm, D), _row(tm, 2 * E), _row(tm, 2 * E), _const((1, E)), _const((1, E)), _wspec(win),
                  _const(wout.shape), _const((SGU_H, SGU_T, SGU_T)), _const((SGU_H, SGU_T, SGU_T)),
                  _const((SGU_H, SGU_T, SGU_G))],
                 [_row(tm, D), _row(tm, 2 * E), _row(tm, E), _acc(8, 2 * E), c3(SGU_H, SGU_T, SGU_T),
                  c3(SGU_H, SGU_T, SGU_G)],
                 [_sds((S, D)), _sds((S, 2 * E), BF16), _sds((S, E), BF16), _sds((8, 2 * E)),
                  _sds((SGU_H, SGU_T, SGU_T)), _sds((SGU_H, SGU_T, SGU_G))],
                 scratch=[pltpu.VMEM((tm, E), F32), pltpu.VMEM((tm, E), F32)]
                 )(dz1, zg, gg, lg, lb, win, wout, ws, wst, bsx)


def _mm_tn(name, a, amode, b, bmode, k, n, groups=NQ, hosts=()):
    def block_bytes(ts):
        ka = k if amode == "1" else groups * k
        nb = n if bmode == "1" else groups * n
        return 2 * (ts * ka * a.dtype.itemsize + ts * nb * b.dtype.itemsize + groups * k * n * 4)

    ts = min(1024 if block_bytes(1024) <= DW_BLOCK_BUDGET else 512, S)

    def spec(mode, w):
        if mode == "1":
            return pl.BlockSpec((ts, w), lambda s: (s, 0))
        if mode == "c":
            return pl.BlockSpec((ts, groups * w), lambda s: (s, 0))
        return pl.BlockSpec((groups, ts, w), lambda s: (0, s, 0))

    def pick(ref, mode, w, g):
        if mode == "1":
            return ref[...]
        if mode == "c":
            return ref[:, g * w:(g + 1) * w]
        return ref[g]

    def body(a_ref, b_ref, o_ref):
        _zero_first(o_ref)
        a_t = jnp.transpose(a_ref[...].astype(BF16)) if amode == "1" else None
        b_1 = b_ref[...].astype(BF16) if bmode == "1" else None
        for g in range(groups):
            lhs = a_t if amode == "1" else jnp.transpose(pick(a_ref, amode, k, g).astype(BF16))
            rhs = b_1 if bmode == "1" else pick(b_ref, bmode, n, g).astype(BF16)
            o_ref[0, g] += jnp.dot(lhs, rhs, preferred_element_type=F32)

    return _call(name, body, (S // ts,), [spec(amode, k), spec(bmode, n)],
                 pl.BlockSpec((1, groups, k, n), lambda s: (0, 0, 0, 0)), _sds((1, groups, k, n)), hosts=hosts)(a, b)


def _row_block(k, cap=256):
    return max(t for t in range(16, min(k, cap) + 1, 16) if k % t == 0)


def _cast_bf16(w, hosts=()):
    nl, k, n = w.shape
    tb = _row_block(k, 512)
    nb = k // tb

    def body(w_ref, o_ref):
        o_ref[...] = w_ref[...].astype(BF16)

    spec = pl.BlockSpec((None, tb, n), lambda i: (i // nb, i % nb, 0))
    return _call("cast_bf16", body, (nl * nb,), [spec], spec, _sds(w.shape, BF16), hosts=hosts)(w)


def _adam(name, w, m, v, gc, l, prev):
    nl, k, n = w.shape
    nc = gc.shape[0]
    tb = _row_block(k)

    def body(w_ref, m_ref, v_ref, g_ref, *rest):
        go_ref, d_ref, mo_ref, vo_ref = rest[-4:]
        g = g_ref[0].astype(F32)
        for c in range(1, nc):
            g = g + g_ref[c].astype(F32)
        m2 = ADAM_B1 * m_ref[...] + (1.0 - ADAM_B1) * g
        v2 = ADAM_B2 * v_ref[...] + (1.0 - ADAM_B2) * (g * g)
        m_hat = m2 / (1.0 - ADAM_B1 ** ADAM_STEP)
        v_hat = v2 / (1.0 - ADAM_B2 ** ADAM_STEP)
        go_ref[...] = g
        d_ref[...] = -ADAM_LR * (m_hat / (jnp.sqrt(v_hat) + ADAM_EPS) + ADAM_WD * w_ref[...])
        mo_ref[...] = m2
        vo_ref[...] = v2

    spec = pl.BlockSpec((None, tb, n), lambda i: (l, i, 0))
    gspec = pl.BlockSpec((nc, None, tb, n), lambda i: (0, 0, i, 0))
    in_specs, args, aliases = [spec, spec, spec, gspec], [w, m, v, gc], {}
    if prev is not None:
        in_specs += [pl.BlockSpec(memory_space=pl.ANY)] * 4
        args += list(prev)
        aliases = {4 + j: j for j in range(4)}
    return _call(name, body, (k // tb,), in_specs, [spec] * 4, [_sds(w.shape)] * 4, aliases=aliases)(*args)


def _sum8(name, g8):
    r = g8.shape[1]

    def body(g_ref, o_ref):
        acc = g_ref[0]
        for d in range(1, 8):
            acc = acc + g_ref[d]
        o_ref[...] = acc

    return _call(name, body, (1,), [pl.BlockSpec((8, r, 128), lambda i: (0, 0, 0))],
                 pl.BlockSpec((r, 128), lambda i: (0, 0)), _sds((r, 128)))(g8)


def _place():
    x, y, c = lax.axis_index("x"), lax.axis_index("y"), lax.axis_index("c")
    return x, y, c, 2 * x + y, (x, y, 1 - c), [(1 - x, y), (x, 1 - y), (1 - x, 1 - y)]


class _Exchange:
    def __init__(self, arrays, out_shapes):
        self.arrays, self.out_shapes = list(arrays), list(out_shapes)
        n = len(self.arrays)
        self.sems = [pltpu.SemaphoreType.DMA((7 * n,)), pltpu.SemaphoreType.DMA((7 * n,)),
                     pltpu.SemaphoreType.DMA((n,))]

    def _copies(self, ins, outs, sems):
        send, recv, lsem = sems
        local_src, remote_src, dst = self.maps(ins, outs)
        x, y, c, q, sib, chips = _place()

        def rcopy(w, k, qq, cc, to, src=None):
            return pltpu.make_async_remote_copy(
                src_ref=dst(w, qq, cc) if src is None else src, dst_ref=dst(w, qq, cc),
                send_sem=send.at[7 * w + k], recv_sem=recv.at[7 * w + k], device_id=to, device_id_type=MESH)

        def mine(w):
            return pltpu.make_async_copy(local_src(w), dst(w, q, c), lsem.at[w])

        def first(w):
            return [rcopy(w, 0, q, c, sib, local_src(w))] + [
                rcopy(w, 1 + j, q, c, (cx, cy, c), remote_src(w, 2 * cx + cy)) for j, (cx, cy) in enumerate(chips)]

        return rcopy, mine, first, (x, y, c), q, c, sib, chips

    def start(self, ins, outs, sems):
        _, mine, first, *_ = self._copies(ins, outs, sems)
        for w in range(len(self.arrays)):
            mine(w).start()
            for cp in first(w):
                cp.start()

    def finish(self, ins, outs, sems):
        rcopy, mine, first, me, q, c, sib, chips = self._copies(ins, outs, sems)
        n = len(self.arrays)
        for w in range(n):
            for j, (cx, cy) in enumerate(chips):
                rcopy(w, 1 + j, 2 * cx + cy, c, me).wait_recv()
                rcopy(w, 4 + j, 2 * cx + cy, c, sib).start()
        for w in range(n):
            rcopy(w, 0, q, 1 - c, me).wait_recv()
            for j, (cx, cy) in enumerate(chips):
                rcopy(w, 4 + j, 2 * cx + cy, 1 - c, me).wait_recv()
        for w in range(n):
            for cp in first(w):
                cp.wait_send()
            for j, (cx, cy) in enumerate(chips):
                rcopy(w, 4 + j, 2 * cx + cy, c, sib).wait_send()
            mine(w).wait()


class _GatherWeights(_Exchange):
    def __init__(self, items):
        self.layers = [l for _, l in items]
        self.kh = [s.shape[1] // 2 for s, _ in items]
        super().__init__([s for s, _ in items], [_sds((NQ, 1) + s.shape[1:], BF16) for s, _ in items])

    def maps(self, ins, outs):
        c = lax.axis_index("c")
        src = lambda w: ins[w].at[pl.ds(self.layers[w], 1), pl.ds(c * self.kh[w], self.kh[w]), :]
        return src, lambda w, q: src(w), lambda w, q, cc: outs[w].at[q, :, pl.ds(cc * self.kh[w], self.kh[w]), :]


class _ScatterPartials(_Exchange):
    def __init__(self, parts):
        super().__init__(parts, [_sds((NQ, 1, 2) + p.shape[2:], BF16) for p in parts])

    def maps(self, ins, outs):
        q = 2 * lax.axis_index("x") + lax.axis_index("y")
        return (lambda w: ins[w].at[:, q]), (lambda w, qq: ins[w].at[:, qq]), (lambda w, qq, cc: outs[w].at[qq, :, cc])


class _Gather8(_Exchange):
    def __init__(self, v):
        super().__init__([v], [_sds((8,) + v.shape)])

    def maps(self, ins, outs):
        return (lambda w: ins[0]), (lambda w, q: ins[0]), (lambda w, q, cc: outs[0].at[2 * q + cc])


class _SwapHalves:
    def __init__(self, dws):
        self.arrays = list(dws)
        self.kh = [d.shape[2] // 2 for d in dws]
        self.out_shapes = [_sds(d.shape[:2] + (kh,) + d.shape[3:]) for d, kh in zip(dws, self.kh)]
        self.sems = [pltpu.SemaphoreType.DMA((len(dws),)), pltpu.SemaphoreType.DMA((len(dws),))]

    def _copies(self, ins, outs, sems):
        send, recv = sems
        _, _, c, _, sib, _ = _place()
        return [pltpu.make_async_remote_copy(
            src_ref=ins[w].at[:, :, pl.ds((1 - c) * self.kh[w], self.kh[w]), :], dst_ref=outs[w],
            send_sem=send.at[w], recv_sem=recv.at[w], device_id=sib, device_id_type=MESH)
            for w in range(len(self.arrays))]

    def start(self, ins, outs, sems):
        for cp in self._copies(ins, outs, sems):
            cp.start()

    def finish(self, ins, outs, sems):
        for cp in self._copies(ins, outs, sems):
            cp.wait()


def _comm_only(name, host):
    n_in, n_out = len(host.arrays), len(host.out_shapes)

    def body(*refs):
        ins, outs, sems = refs[:n_in], refs[n_in:n_in + n_out], refs[n_in + n_out:]
        host.start(ins, outs, sems)
        host.finish(ins, outs, sems)

    any_spec = pl.BlockSpec(memory_space=pl.ANY)
    return pl.pallas_call(body, name=name, in_specs=[any_spec] * n_in, out_specs=[any_spec] * n_out,
                          out_shape=host.out_shapes, scratch_shapes=host.sems)(*host.arrays)


def _add_halves(dw, got, cidx):
    nl, _, k, n = dw.shape
    kh = k // 2

    def body(c_ref, a_ref, b_ref, o_ref):
        o_ref[...] = (a_ref[...] + b_ref[...]).astype(BF16)

    grid_spec = pltpu.PrefetchScalarGridSpec(
        num_scalar_prefetch=1, grid=(nl, NQ),
        in_specs=[pl.BlockSpec((None, None, None, kh, n), lambda l, q, c_ref: (l, q, c_ref[0], 0, 0)),
                  pl.BlockSpec((None, None, kh, n), lambda l, q, c_ref: (l, q, 0, 0))],
        out_specs=pl.BlockSpec((None, None, kh, n), lambda l, q, c_ref: (l, q, 0, 0)))
    return pl.pallas_call(
        body, name="add_halves", grid_spec=grid_spec, out_shape=_sds((nl, NQ, kh, n), BF16),
        compiler_params=pltpu.CompilerParams(dimension_semantics=("arbitrary", "arbitrary"),
                                             vmem_limit_bytes=VMEM_LIMIT))(cidx, dw.reshape(nl, NQ, 2, kh, n), got)


def _gather8(name, v):
    return _comm_only(name, _Gather8(v))[0]


PACK = 16 * 128


def _pack(arrays):
    parts = []
    for a in arrays:
        flat = a.reshape(-1)
        parts.append(jnp.pad(flat, (0, (-flat.shape[0]) % PACK)))
    return jnp.concatenate(parts).reshape(-1, 128)


def _unpack(packed, shapes):
    flat = packed.reshape(-1)
    out, off = [], 0
    for shp in shapes:
        size = 1
        for d in shp:
            size *= d
        out.append(flat[off:off + size].reshape(shp))
        off += size + (-size) % PACK
    return out


def kernel(x, p, a_w_pw1, a_b_pw1, a_w_dw, a_b_dw, a_ln_g, a_ln_b, a_w_pw2, b_w_in, b_b_in, b_ln_g, b_ln_b, b_w_s, b_b_s, b_w_out, c_w_in, c_w_conv, c_w_out, ln1_g, ln1_b, ln2_g, ln2_b, ffn_w_gate, ffn_w_up, ffn_w_down, ple_w_gate, ple_w_proj, ple_norm_g, loss_target, m_a_w_pw1, m_a_b_pw1, m_a_w_dw, m_a_b_dw, m_a_ln_g, m_a_ln_b, m_a_w_pw2, m_b_w_in, m_b_b_in, m_b_ln_g, m_b_ln_b, m_b_w_s, m_b_b_s, m_b_w_out, m_c_w_in, m_c_w_conv, m_c_w_out, m_ln1_g, m_ln1_b, m_ln2_g, m_ln2_b, m_ffn_w_gate, m_ffn_w_up, m_ffn_w_down, m_ple_w_gate, m_ple_w_proj, m_ple_norm_g, v_a_w_pw1, v_a_b_pw1, v_a_w_dw, v_a_b_dw, v_a_ln_g, v_a_ln_b, v_a_w_pw2, v_b_w_in, v_b_b_in, v_b_ln_g, v_b_ln_b, v_b_w_s, v_b_b_s, v_b_w_out, v_c_w_in, v_c_w_conv, v_c_w_out, v_ln1_g, v_ln1_b, v_ln2_g, v_ln2_b, v_ffn_w_gate, v_ffn_w_up, v_ffn_w_down, v_ple_w_gate, v_ple_w_proj, v_ple_norm_g):
    args = dict(locals())
    wts = {k: args[k] for k in WEIGHTS}
    mom = {k: args["m_" + k] for k in WEIGHTS}
    var = {k: args["v_" + k] for k in WEIGHTS}
    for k in TRANSPOSED:
        wts[k], mom[k], var[k] = (jnp.transpose(t[k], (0, 2, 1)) for t in (wts, mom, var))
    q_idx = 2 * lax.axis_index("x") + lax.axis_index("y")
    c_idx = lax.axis_index("c").astype(jnp.int32).reshape(1)

    wb = {k: _cast_bf16(wts[k]) for k in BIG if k not in ("ffn_w_gate", "ffn_w_up")}
    mixw = [[("a_w_pw1", 0), ("a_w_pw2", 0)], [("b_w_in", 0), ("b_w_out", 0)], [("c_w_in", 0), ("c_w_out", 0)],
            [("a_w_pw1", 1), ("a_w_pw2", 1)]]
    ffnw = [[("ffn_w_gate", l), ("ffn_w_up", l), ("ffn_w_down", l)] for l in range(DEPTH)]
    plew = [[("ple_w_gate", l), ("ple_w_proj", l)] for l in range(DEPTH)]
    fwd_plan = {("a1", 0): mixw[0][1:] + plew[0], ("a2", 0): ffnw[0], ("ffn", 0): mixw[1], ("ple", 0): plew[1],
                ("b", 1): ffnw[1], ("ffn", 1): mixw[2] + ffnw[2][:1], ("ple", 1): plew[2],
                ("c1", 2): ffnw[2][1:2], ("c2", 2): ffnw[2][2:], ("ffn", 2): mixw[3] + ffnw[3][:1], ("ple", 2): plew[3],
                ("a1", 3): ffnw[3][1:2], ("a2", 3): ffnw[3][2:]}
    gw = {}

    def gather(keys):
        return _GatherWeights([(wb[name], l) for name, l in keys])

    def hosted(tag, fn, *fargs):
        keys = fwd_plan.get(tag)
        if not keys:
            return fn(*fargs)
        own, (got,) = fn(*fargs, hosts=[gather(keys)])
        store(keys, got)
        return own

    def store(keys, got):
        for (name, l), arr in zip(keys, got):
            gw[name, l] = arr.reshape(NQ * arr.shape[2], arr.shape[3]) if name in ROW_SHARDED else arr

    first_keys = mixw[0][:1]
    wb["ffn_w_gate"], (got,) = _cast_bf16(wts["ffn_w_gate"], hosts=[gather(first_keys)])
    store(first_keys, got)
    shard_shapes = [wts[k].shape for k in SMALL_SHARDED]
    wb["ffn_w_up"], ((small8,),) = _cast_bf16(wts["ffn_w_up"], hosts=[_Gather8(_pack([wts[k] for k in SMALL_SHARDED]))])
    per_chip = [_unpack(small8[2 * qq], shard_shapes) for qq in range(NQ)]
    full = {k: jnp.concatenate([per_chip[qq][i] for qq in range(NQ)], axis=-1) for i, k in enumerate(SMALL_SHARDED)}
    for k in SMALL_REPL:
        full[k] = wts[k]

    def vec(name, l):
        return full[name][l][None, :]

    def conv_w(name, l, rows):
        w = full[name][l]
        return jnp.pad(w, ((0, rows - w.shape[0]), (0, 0)))

    ws = full["b_w_s"][0]
    wst = jnp.transpose(ws, (0, 2, 1))
    bsx = jnp.broadcast_to(full["b_b_s"][0][:, :, None], (SGU_H, SGU_T, SGU_G))

    x0s, z1s, z2s, saved, ffn_saved = [], [], [], [], []
    cur = x[0]
    for i in range(DEPTH):
        mix, j = i % 3, i // 3
        x0s.append(cur)
        if mix == 0:
            h, glu = hosted(("a1", i), _fwd_a1, cur, gw["a_w_pw1", j], vec("a_b_pw1", j), i)
            z1, cv = hosted(("a2", i), _fwd_a2, glu, cur, conv_w("a_w_dw", j, 32), vec("a_b_dw", j), vec("a_ln_g", j),
                            vec("a_ln_b", j), gw["a_w_pw2", j], i)
            saved.append((h, glu, cv))
        elif mix == 1:
            z1, zg, gg = hosted(("b", i), _fwd_b, cur, gw["b_w_in", 0], vec("b_b_in", 0), vec("b_ln_g", 0),
                                vec("b_ln_b", 0), ws, bsx, gw["b_w_out", 0])
            saved.append((zg, gg))
        else:
            hc = hosted(("c1", i), _fwd_c1, cur, gw["c_w_in", 0])
            z1 = hosted(("c2", i), _fwd_c2, hc, cur, conv_w("c_w_conv", 0, 8), gw["c_w_out", 0])
            saved.append((hc,))
        z2, ab, ub, hm = hosted(("ffn", i), _fwd_ffn, z1, vec("ln1_g", i), vec("ln1_b", i), gw["ffn_w_gate", i],
                                gw["ffn_w_up", i], gw["ffn_w_down", i], i)
        ffn_saved.append((ab, ub, hm))
        cur = hosted(("ple", i), _fwd_ple, z2, p[i, 0], vec("ln2_g", i), vec("ln2_b", i), gw["ple_w_gate", i],
                     gw["ple_w_proj", i], vec("ple_norm_g", i), i)
        z1s.append(z1)
        z2s.append(z2)

    g, loss_acc = _loss_head(cur, loss_target[0])
    loss = lax.psum(0.5 / D * jnp.sum(loss_acc[0]), ("x", "y", "c"))

    dws = {}
    sg = {}
    res = {k: None for k in BIG}

    def wgrad(name, l, a, amode, b, bmode, scatter_keys=()):
        _, k, n = wts[name].shape
        hosts = [_ScatterPartials([parts[key] for key in scatter_keys])] if scatter_keys else ()
        if name in ROW_SHARDED:
            out = _mm_tn(f"dw_{name}_{l}", a, "1", b, "1", NQ * k, n, groups=1, hosts=hosts)
        else:
            out = _mm_tn(f"dw_{name}_{l}", a, amode, b, bmode, k, n, hosts=hosts)
        if scatter_keys:
            out, (contribs,) = out
            update(scatter_keys, contribs)
        dws[name, l] = out.reshape(1, NQ, k, n)

    def swap(keys):
        return _SwapHalves([dws[k] for k in keys])

    parts = {}

    def add_halves(keys, got):
        parts.update((k, _add_halves(dws[k], r, c_idx)) for k, r in zip(keys, got))

    def update(keys, contribs):
        for (name, l), gc in zip(keys, contribs):
            _, kq, n = wts[name].shape
            res[name] = _adam(f"adam_{name}_{l}", wts[name], mom[name], var[name], gc.reshape(NQ, 1, kq, n), l,
                              res[name])

    small = SMALL_SHARDED + SMALL_REPL
    late_small = [("a_w_dw", 0), ("a_b_pw1", 0)]
    early_small = [(k, l) for k in small for l in range(full[k].shape[0]) if (k, l) not in late_small]
    pending = None
    for i in reversed(range(DEPTH)):
        mix, j = i % 3, i // 3
        ple_args = (g, z2s[i], p[i, 0], vec("ln2_g", i), vec("ln2_b", i), gw["ple_w_gate", i], gw["ple_w_proj", i],
                    vec("ple_norm_g", i), i)
        if pending:
            (dz2, x2b, dgp, dqp, acc), (got,) = _bwd_ple(*ple_args, hosts=[swap(pending)])
            add_halves(pending, got)
        else:
            dz2, x2b, dgp, dqp, acc = _bwd_ple(*ple_args)
        sg["ple_norm_g", i], sg["ln2_g", i], sg["ln2_b", i] = acc[0], acc[1], acc[2]
        wgrad("ple_w_gate", i, x2b, "c", dgp, "1")
        wgrad("ple_w_proj", i, p[i, 0], "1", dqp, "c")
        ab, ub, hm = ffn_saved[i]
        ffn_args = (dz2, z1s[i], ab, ub, vec("ln1_g", i), vec("ln1_b", i), gw["ffn_w_gate", i], gw["ffn_w_up", i],
                    gw["ffn_w_down", i], i)
        if pending:
            (dz1, x1b, da, du, acc), (contribs,) = _bwd_ffn(
                *ffn_args, hosts=[_ScatterPartials([parts[key] for key in ffnw[i + 1]])])
            update(ffnw[i + 1], contribs)
        else:
            dz1, x1b, da, du, acc = _bwd_ffn(*ffn_args)
        sg["ln1_g", i], sg["ln1_b", i] = acc[0], acc[1]
        wgrad("ffn_w_gate", i, da, "1", x1b, "1", scatter_keys=mixw[i + 1][:1] if pending else ())
        wgrad("ffn_w_up", i, du, "1", x1b, "1", scatter_keys=mixw[i + 1][1:] + plew[i + 1] if pending else ())
        wgrad("ffn_w_down", i, hm, "1", dz2, "1")
        x0 = x0s[i]
        if mix == 0:
            h, glu, cv = saved[i]
            a2_args = (dz1, cv, vec("a_ln_g", j), vec("a_ln_b", j), gw["a_w_pw2", j], i)
            conv_args = (glu, conv_w("a_w_dw", j, 32), i)
            if i == 0:
                early = ffnw[0] + plew[0]
                (dcv, sb, acc), (got,) = _bwd_a2(*a2_args, hosts=[swap(early)])
                sg["a_ln_g", j], sg["a_ln_b", j], sg["a_b_dw", j] = acc[0], acc[1], acc[2]
                add_halves(early, got)
                wgrad("a_w_pw2", j, sb, "c", dz1, "1")
                (dglu, dwdw), (contribs, (g8_early,), got) = _bwd_conv_a(
                    dcv, *conv_args, hosts=[_ScatterPartials([parts[key] for key in early]),
                                            _Gather8(_pack([sg[pc] for pc in early_small])), swap(mixw[0][1:])])
                update(early, contribs)
                add_halves(mixw[0][1:], got)
                (g, dh, acc), (contribs,) = _bwd_a1(dglu, h, dz1, gw["a_w_pw1", j], i,
                                                    hosts=[_ScatterPartials([parts[key] for key in mixw[0][1:]])])
                update(mixw[0][1:], contribs)
            else:
                dcv, sb, acc = _bwd_a2(*a2_args)
                sg["a_ln_g", j], sg["a_ln_b", j], sg["a_b_dw", j] = acc[0], acc[1], acc[2]
                dglu, dwdw = _bwd_conv_a(dcv, *conv_args)
                wgrad("a_w_pw2", j, sb, "c", dz1, "1")
                g, dh, acc = _bwd_a1(dglu, h, dz1, gw["a_w_pw1", j], i)
            sg["a_w_dw", j] = dwdw[:CONV_A]
            sg["a_b_pw1", j] = acc[0]
            wgrad("a_w_pw1", j, x0, "1", dh, "c")
        elif mix == 1:
            zg, gg = saved[i]
            g, dh, mb, acc, dw_s, db_s = _bwd_b(dz1, zg, gg, vec("b_ln_g", 0), vec("b_ln_b", 0), gw["b_w_in", 0],
                                                gw["b_w_out", 0], ws, wst, bsx)
            sg["b_b_in", 0], sg["b_ln_g", 0], sg["b_ln_b", 0] = acc[0], acc[1, :E], acc[2, :E]
            sg["b_w_s", 0], sg["b_b_s", 0] = dw_s, jnp.sum(db_s, axis=-1)
            wgrad("b_w_out", 0, mb, "c", dz1, "1")
            wgrad("b_w_in", 0, x0, "1", dh, "c")
        else:
            (hc,) = saved[i]
            wc = conv_w("c_w_conv", 0, 8)
            dy, dbg, mb = _bwd_c2(dz1, hc, wc, gw["c_w_out", 0])
            wgrad("c_w_out", 0, mb, "c", dz1, "1")
            g, dhc, dwc = _bwd_c1(dy, hc, dbg, dz1, wc, gw["c_w_in", 0])
            sg["c_w_conv", 0] = dwc[:CONV_C]
            wgrad("c_w_in", 0, x0, "1", dhc, "c")
        pending = mixw[i] + ffnw[i] + plew[i] if i > 0 else mixw[0][:1]
    grad_x = g[None]
    add_halves(pending, _comm_only("swap_last", swap(pending)))
    update(pending, _comm_only("scatter_last", _ScatterPartials([parts[key] for key in pending])))

    g8_late = _gather8("gather_small_late", _pack([sg[pc] for pc in late_small]))
    sums = dict(zip(early_small, _unpack(_sum8("sum8_early", g8_early), [sg[pc].shape for pc in early_small])))
    sums.update(zip(late_small, _unpack(_sum8("sum8_late", g8_late), [sg[pc].shape for pc in late_small])))
    gsum = [jnp.stack([sums[k, l] for l in range(full[k].shape[0])]) for k in small]
    gmine = []
    for k, gs in zip(small, gsum):
        if k in SMALL_SHARDED:
            wdt = wts[k].shape[-1]
            gs = lax.dynamic_slice_in_dim(gs, q_idx * wdt, wdt, axis=gs.ndim - 1)
        gmine.append(gs)
    packed = [_pack(t)[None] for t in ([wts[k] for k in small], [mom[k] for k in small], [var[k] for k in small])]
    outs = _adam("adam_small", packed[0], packed[1], packed[2], _pack(gmine)[None, None], 0, None)
    unpacked = [_unpack(o[0], [wts[k].shape for k in small]) for o in outs]
    for i, k in enumerate(small):
        res[k] = tuple(u[i] for u in unpacked)

    for k in TRANSPOSED:
        res[k] = tuple(jnp.transpose(r, (0, 2, 1)) for r in res[k])
    return (loss, grad_x, *[res[k][0] for k in WEIGHTS], *[res[k][1] for k in WEIGHTS],
            *[res[k][2] for k in WEIGHTS], *[res[k][3] for k in WEIGHTS])
```

```python
import jax
import jax.numpy as jnp
from jax import lax
from jax.experimental import pallas as pl
from jax.experimental.pallas import tpu as pltpu

F32, BF16 = jnp.float32, jnp.bfloat16
S = 4096
D = 1024
E = 2048
FF = 2816
FQ = FF // 4
NQ = 4
DEPTH = 4
ALPHA = (2 * DEPTH) ** 0.25
LN_EPS = 1e-5
CONV_A, CONV_C = 31, 3
HALO_A, HALO_C = 32, 8
SGU_T, SGU_H, SGU_G, SGU_CHUNK = 128, 8, 256, 64
VMEM_LIMIT = 56 * 1024 * 1024
DW_BLOCK_BUDGET = 40 * 1024 * 1024
MESH = pl.DeviceIdType.MESH
ADAM_LR, ADAM_B1, ADAM_B2, ADAM_EPS, ADAM_WD, ADAM_STEP = 0.001, 0.9, 0.999, 1e-08, 0.01, 10
GELU_C, GELU_A = 0.7978845608028654, 0.044715

BIG = ["a_w_pw1", "a_w_pw2", "b_w_in", "b_w_out", "c_w_in", "c_w_out",
       "ffn_w_gate", "ffn_w_up", "ffn_w_down", "ple_w_gate", "ple_w_proj"]
TRANSPOSED = ["ffn_w_gate", "ffn_w_up"]
ROW_SHARDED = ["a_w_pw2", "b_w_out", "c_w_out", "ffn_w_gate", "ffn_w_up", "ffn_w_down", "ple_w_gate"]
SMALL_SHARDED = ["a_b_pw1", "a_w_dw", "a_b_dw", "a_ln_g", "a_ln_b", "c_w_conv"]
SMALL_REPL = ["b_b_in", "b_ln_g", "b_ln_b", "b_w_s", "b_b_s", "ln1_g", "ln1_b", "ln2_g", "ln2_b", "ple_norm_g"]
WEIGHTS = ["a_w_pw1", "a_b_pw1", "a_w_dw", "a_b_dw", "a_ln_g", "a_ln_b", "a_w_pw2", "b_w_in", "b_b_in", "b_ln_g",
           "b_ln_b", "b_w_s", "b_b_s", "b_w_out", "c_w_in", "c_w_conv", "c_w_out", "ln1_g", "ln1_b", "ln2_g",
           "ln2_b", "ffn_w_gate", "ffn_w_up", "ffn_w_down", "ple_w_gate", "ple_w_proj", "ple_norm_g"]


def _call(name, body, grid, in_specs, out_specs, out_shape, scratch=(), aliases=None, hosts=()):
    params = pltpu.CompilerParams(dimension_semantics=("arbitrary",) * len(grid), vmem_limit_bytes=VMEM_LIMIT)
    if not hosts:
        return pl.pallas_call(
            body, name=name, grid=grid, in_specs=in_specs, out_specs=out_specs, out_shape=out_shape,
            scratch_shapes=list(scratch), input_output_aliases=aliases or {}, compiler_params=params)
    assert len(grid) == 1 and not aliases
    single = not isinstance(out_shape, (list, tuple))
    own_shapes = [out_shape] if single else list(out_shape)
    own_specs = [out_specs] if single else list(out_specs)
    n_in, n_out, n_scr = len(in_specs), len(own_shapes), len(scratch)
    h_in = [len(h.arrays) for h in hosts]
    h_out = [len(h.out_shapes) for h in hosts]
    h_sem = [len(h.sems) for h in hosts]

    def split(refs, counts):
        out, off = [], 0
        for cnt in counts:
            out.append(refs[off:off + cnt])
            off += cnt
        return out

    def wrapped(*refs):
        ins, hin, outs, hout, scr, hsem = split(refs, [n_in, sum(h_in), n_out, sum(h_out), n_scr, sum(h_sem)])
        per_host = list(zip(hosts, split(hin, h_in), split(hout, h_out), split(hsem, h_sem)))

        @pl.when(pl.program_id(0) == 0)
        def _():
            for h, a, o, s in per_host:
                h.start(a, o, s)

        body(*ins, *outs, *scr)

        @pl.when(pl.program_id(0) == grid[0] - 1)
        def _():
            for h, a, o, s in per_host:
                h.finish(a, o, s)

    any_spec = pl.BlockSpec(memory_space=pl.ANY)
    call = pl.pallas_call(
        wrapped, name=name, grid=grid, in_specs=list(in_specs) + [any_spec] * sum(h_in),
        out_specs=own_specs + [any_spec] * sum(h_out),
        out_shape=own_shapes + [s for h in hosts for s in h.out_shapes],
        scratch_shapes=list(scratch) + [s for h in hosts for s in h.sems], compiler_params=params)

    def run(*args):
        res = call(*args, *[a for h in hosts for a in h.arrays])
        own = res[0] if single else list(res[:n_out])
        return own, split(list(res[n_out:]), h_out)

    return run


def _sds(shape, dtype=F32):
    return jax.ShapeDtypeStruct(shape, dtype)


def _row(tm, c):
    return pl.BlockSpec((tm, c), lambda i: (i, 0))


def _grow(g, tm, c):
    return pl.BlockSpec((g, tm, c), lambda i: (0, i, 0))


def _const(shape):
    nd = len(shape)
    return pl.BlockSpec(shape, lambda i: (0,) * nd, pipeline_mode=pl.Buffered(1))


def _wspec(w):
    return pl.BlockSpec((NQ, None, w.shape[2], w.shape[3]), lambda i: (0, 0, 0, 0), pipeline_mode=pl.Buffered(1))


def _prev(tm, hb, c):
    return pl.BlockSpec((hb, c), lambda i: (jnp.maximum(i * (tm // hb) - 1, 0), 0))


def _next(tm, hb, c):
    return pl.BlockSpec((hb, c), lambda i: (jnp.minimum((i + 1) * (tm // hb), S // hb - 1), 0))


def _acc(r, c):
    return pl.BlockSpec((r, c), lambda i: (0, 0))


def _sig(x):
    return 1.0 / (1.0 + jnp.exp(-x))


def _ln(z, g, b):
    mu = jnp.mean(z, axis=-1, keepdims=True)
    zc = z - mu
    rstd = lax.rsqrt(jnp.mean(zc * zc, axis=-1, keepdims=True) + LN_EPS)
    xhat = zc * rstd
    return xhat * g + b, xhat, rstd


def _ln_bwd(dyg, xhat, rstd):
    return rstd * (dyg - jnp.mean(dyg, axis=-1, keepdims=True) - xhat * jnp.mean(dyg * xhat, axis=-1, keepdims=True))


def _mm(a, w):
    return jnp.dot(a.astype(BF16), w, preferred_element_type=F32)


def _mmt(a, w):
    return lax.dot_general(a.astype(BF16), w, (((1,), (1,)), ((), ())), preferred_element_type=F32)


def _colsum(x):
    return jnp.sum(x, axis=0, keepdims=True)


def _gelu(x):
    t = jnp.tanh(GELU_C * (x + GELU_A * x * x * x))
    return 0.5 * x * (1.0 + t), t


def _gelu_grad(x, t):
    return 0.5 * (1.0 + t) + 0.5 * x * (1.0 - t * t) * GELU_C * (1.0 + 3.0 * GELU_A * x * x)


def _silu_grad(a, sg):
    return sg * (1.0 + a * (1.0 - sg))


def _sgu_masks():
    r = lax.broadcasted_iota(jnp.int32, (SGU_T, SGU_T), 0) // SGU_CHUNK
    c = lax.broadcasted_iota(jnp.int32, (SGU_T, SGU_T), 1) // SGU_CHUNK
    return r >= c, c >= r


def _fill_halo(buf, lo, n, halo_val_fn, is_edge):
    @pl.when(is_edge)
    def _():
        buf[lo:lo + n, :] = jnp.zeros((n, buf.shape[1]), F32)

    @pl.when(jnp.logical_not(is_edge))
    def _():
        buf[lo:lo + n, :] = halo_val_fn()


SUB, LANE = 8, 128
ROWS_AT_ONCE = 16


def _shift_copies(buf, sh):
    rows = sh.shape[1]
    for s in range(1, SUB):
        sh[s - 1, :, :] = buf[pl.ds(s, rows), :]


def _tiles(buf, sh, s, first, count, group0, lanes):
    src = buf if s == 0 else sh.at[s - 1]
    return {t: src[pl.ds(pl.multiple_of((group0 + t) * SUB, SUB), SUB), lanes] for t in range(first, first + count)}


def _by_shift(offsets):
    out = []
    for s in range(SUB):
        taps = [(k, o // SUB) for k, o in enumerate(offsets) if o % SUB == s]
        if taps:
            out.append((s, taps))
    return out


def _conv_rows(out_ref, w_ref, bias_ref, offsets, buf, sh, tm):
    n = ROWS_AT_ONCE
    for cb in range(D // LANE):
        lanes = slice(cb * LANE, (cb + 1) * LANE)
        bias = None if bias_ref is None else jnp.broadcast_to(bias_ref[:, lanes], (SUB, LANE))

        def body(jb, carry):
            accs = [bias] * n
            for s, taps in _by_shift(offsets):
                ms = [m for _, m in taps]
                tiles = _tiles(buf, sh, s, min(ms), max(ms) - min(ms) + n, jb * n, lanes)
                for k, m in taps:
                    wk = jnp.broadcast_to(w_ref[k:k + 1, lanes], (SUB, LANE))
                    for jj in range(n):
                        t = wk * tiles[m + jj]
                        accs[jj] = t if accs[jj] is None else accs[jj] + t
            for jj in range(n):
                out_ref[pl.ds(pl.multiple_of((jb * n + jj) * SUB, SUB), SUB), lanes] = accs[jj]
            return carry

        lax.fori_loop(0, tm // (SUB * n), body, 0)


def _conv_wgrad(dw_ref, d_ref, offsets, buf, sh, tm):
    n = 4
    for cb in range(D // LANE):
        lanes = slice(cb * LANE, (cb + 1) * LANE)

        def body(jq, accs):
            accs = list(accs)
            d = [d_ref[pl.ds(pl.multiple_of((jq * n + jj) * SUB, SUB), SUB), lanes] for jj in range(n)]
            for s, taps in _by_shift(offsets):
                ms = [m for _, m in taps]
                tiles = _tiles(buf, sh, s, min(ms), max(ms) - min(ms) + n, jq * n, lanes)
                for k, m in taps:
                    for jj in range(n):
                        accs[k] = accs[k] + d[jj] * tiles[m + jj]
            return tuple(accs)

        accs = lax.fori_loop(0, tm // (SUB * n), body, tuple(jnp.zeros((SUB, LANE), F32) for _ in offsets))
        for k, acc in enumerate(accs):
            dw_ref[k:k + 1, lanes] += jnp.sum(acc, axis=0, keepdims=True)


def _fwd_a1(x0, w1, b1, l, hosts=()):
    tm = 512

    def body(x_ref, w_ref, b_ref, h_ref, glu_ref):
        xb = x_ref[...].astype(BF16)
        for q in range(NQ):
            sl = slice(q * 512, (q + 1) * 512)
            h_ref[:, sl] = jnp.dot(xb, w_ref[q], preferred_element_type=F32) + b_ref[:, sl]
        glu_ref[...] = h_ref[:, :D] * _sig(h_ref[:, D:])

    return _call(f"fwd_a1_{l}", body, (S // tm,), [_row(tm, D), _wspec(w1), _const((1, 2 * D))],
                 [_row(tm, 2 * D), _row(tm, D)], [_sds((S, 2 * D)), _sds((S, D))], hosts=hosts)(x0, w1, b1)


def _fwd_a2(glu, x0, wdw, bdw, lg, lb, w2, l, hosts=()):
    tm = 256

    def body(g_ref, gp_ref, x_ref, wdw_ref, bdw_ref, lg_ref, lb_ref, w2_ref, z_ref, cv_ref, buf, sh):
        i = pl.program_id(0)
        _fill_halo(buf, 0, HALO_A, lambda: gp_ref[...], i == 0)
        buf[HALO_A:HALO_A + tm, :] = g_ref[...]
        _shift_copies(buf, sh)
        _conv_rows(cv_ref, wdw_ref, bdw_ref, [HALO_A - (CONV_A - 1) + k for k in range(CONV_A)], buf, sh, tm)
        n, _, _ = _ln(cv_ref[...], lg_ref[...], lb_ref[...])
        sb = (n * _sig(n)).astype(BF16)
        z_ref[...] = ALPHA * x_ref[...] + jnp.dot(sb, w2_ref[...], preferred_element_type=F32)

    return _call(f"fwd_a2_{l}", body, (S // tm,),
                 [_row(tm, D), _prev(tm, HALO_A, D), _row(tm, D), _const((32, D)), _const((1, D)), _const((1, D)),
                  _const((1, D)), _const(w2.shape)],
                 [_row(tm, D), _row(tm, D)], [_sds((S, D)), _sds((S, D))],
                 scratch=[pltpu.VMEM((HALO_A + tm, D), F32), pltpu.VMEM((SUB - 1, HALO_A + tm - SUB, D), F32)],
                 hosts=hosts)(glu, glu, x0, wdw, bdw, lg, lb, w2)


def _fwd_b(x0, win, b_in, lg, lb, ws, bsx, wout, hosts=()):
    tm = 256

    def body(x_ref, win_ref, bin_ref, lg_ref, lb_ref, ws_ref, bsx_ref, wout_ref, z_ref, zg_ref, gg_ref, f_scr, h_ref):
        xb = x_ref[...].astype(BF16)
        for q in range(NQ):
            sl = slice(q * 1024, (q + 1) * 1024)
            h_ref[:, sl] = jnp.dot(xb, win_ref[q], preferred_element_type=F32) + bin_ref[:, sl]
        u, tu = _gelu(h_ref[:, :E])
        v, tv = _gelu(h_ref[:, E:])
        zg_ref[:, 0:E] = u.astype(BF16)
        zg_ref[:, E:2 * E] = v.astype(BF16)
        gg_ref[:, 0:E] = _gelu_grad(h_ref[:, :E], tu).astype(BF16)
        gg_ref[:, E:2 * E] = _gelu_grad(h_ref[:, E:], tv).astype(BF16)
        vn, _, _ = _ln(v, lg_ref[...], lb_ref[...])
        vnb = vn.astype(BF16)
        mask, _ = _sgu_masks()
        for hd in range(SGU_H):
            wm = jnp.where(mask, ws_ref[hd], 0.0).astype(BF16)
            cs = slice(hd * SGU_G, (hd + 1) * SGU_G)
            for n in range(tm // SGU_T):
                rs = slice(n * SGU_T, (n + 1) * SGU_T)
                f_scr[rs, cs] = jnp.dot(wm, vnb[rs, cs], preferred_element_type=F32) + bsx_ref[hd]
        mb = (u * f_scr[...]).astype(BF16)
        z_ref[...] = ALPHA * x_ref[...] + jnp.dot(mb, wout_ref[...], preferred_element_type=F32)

    return _call("fwd_b", body, (S // tm,),
                 [_row(tm, D), _wspec(win), _const((1, 2 * E)), _const((1, E)), _const((1, E)),
                  _const((SGU_H, SGU_T, SGU_T)), _const((SGU_H, SGU_T, SGU_G)), _const(wout.shape)],
                 [_row(tm, D), _row(tm, 2 * E), _row(tm, 2 * E)],
                 [_sds((S, D)), _sds((S, 2 * E), BF16), _sds((S, 2 * E), BF16)],
                 scratch=[pltpu.VMEM((tm, E), F32), pltpu.VMEM((tm, 2 * E), F32)], hosts=hosts
                 )(x0, win, b_in, lg, lb, ws, bsx, wout)


def _fwd_c1(x0, win, hosts=()):
    tm = 512

    def body(x_ref, w_ref, hc_ref):
        xb = x_ref[...].astype(BF16)
        for q in range(NQ):
            hc_ref[:, q * 768:(q + 1) * 768] = jnp.dot(xb, w_ref[q], preferred_element_type=F32)

    return _call("fwd_c1", body, (S // tm,), [_row(tm, D), _wspec(win)], _row(tm, 3 * D),
                 _sds((S, 3 * D)), hosts=hosts)(x0, win)


def _short_conv(buf, hc_ref, hcp_ref, wc_ref, tm, i):
    _fill_halo(buf, 0, HALO_C, lambda: hcp_ref[:, D:2 * D] * hcp_ref[:, 2 * D:], i == 0)
    buf[HALO_C:HALO_C + tm, :] = hc_ref[:, D:2 * D] * hc_ref[:, 2 * D:]
    y = wc_ref[0:1, :] * buf[pl.ds(HALO_C - 2, tm), :]
    for k in range(1, CONV_C):
        y = y + wc_ref[k:k + 1, :] * buf[pl.ds(HALO_C - 2 + k, tm), :]
    return y


def _fwd_c2(hc, x0, wc, wout, hosts=()):
    tm = 256

    def body(hc_ref, hcp_ref, x_ref, wc_ref, wout_ref, z_ref, buf):
        y = _short_conv(buf, hc_ref, hcp_ref, wc_ref, tm, pl.program_id(0))
        mb = (hc_ref[:, :D] * y).astype(BF16)
        z_ref[...] = ALPHA * x_ref[...] + jnp.dot(mb, wout_ref[...], preferred_element_type=F32)

    return _call("fwd_c2", body, (S // tm,),
                 [_row(tm, 3 * D), _prev(tm, HALO_C, 3 * D), _row(tm, D), _const((8, D)), _const(wout.shape)],
                 _row(tm, D), _sds((S, D)), scratch=[pltpu.VMEM((HALO_C + tm, D), F32)], hosts=hosts
                 )(hc, hc, x0, wc, wout)


def _fwd_ffn(z1, lg, lb, wgt, wut, wd, l, hosts=()):
    tm = 256

    def body(z_ref, lg_ref, lb_ref, wg_ref, wu_ref, wd_ref, o_ref, a_ref, u_ref, hm_ref):
        x1, _, _ = _ln(z_ref[...], lg_ref[...], lb_ref[...])
        xb = x1.astype(BF16)
        a = _mmt(xb, wg_ref[...])
        u = _mmt(xb, wu_ref[...])
        hmb = (a * _sig(a) * u).astype(BF16)
        a_ref[...] = a.astype(BF16)
        u_ref[...] = u.astype(BF16)
        hm_ref[...] = hmb
        o_ref[...] = ALPHA * x1 + jnp.dot(hmb, wd_ref[...], preferred_element_type=F32)

    return _call(f"fwd_ffn_{l}", body, (S // tm,),
                 [_row(tm, D), _const((1, D)), _const((1, D)), _const((FF, D)), _const((FF, D)), _const((FF, D))],
                 [_row(tm, D), _row(tm, FF), _row(tm, FF), _row(tm, FF)],
                 [_sds((S, D)), _sds((S, FF), BF16), _sds((S, FF), BF16), _sds((S, FF), BF16)],
                 hosts=hosts)(z1, lg, lb, wgt, wut, wd)


def _ple_parts(z2, p, lg, lb, wg_ref, wp_ref, pg):
    x2, xhat, rstd = _ln(z2, lg, lb)
    xb = x2.astype(BF16)
    gate = _sig(jnp.dot(xb, wg_ref[...], preferred_element_type=F32))
    pb = p.astype(BF16)
    qp = jnp.concatenate([jnp.dot(pb, wp_ref[q], preferred_element_type=F32) for q in range(NQ)], axis=1)
    rs = lax.rsqrt(jnp.mean(qp * qp, axis=-1, keepdims=True) + LN_EPS)
    qn = qp * rs
    return x2, xhat, rstd, xb, gate, qn, rs, qn * pg


def _fwd_ple(z2, p, lg, lb, wg, wp, pg, l, hosts=()):
    tm = 512

    def body(z_ref, p_ref, lg_ref, lb_ref, wg_ref, wp_ref, pg_ref, o_ref):
        x2, _, _, _, gate, _, _, r = _ple_parts(z_ref[...], p_ref[...], lg_ref[...], lb_ref[...], wg_ref, wp_ref,
                                                pg_ref[...])
        o_ref[...] = x2 + gate * r

    return _call(f"fwd_ple_{l}", body, (S // tm,),
                 [_row(tm, D), _row(tm, 256), _const((1, D)), _const((1, D)), _const(wg.shape), _wspec(wp),
                  _const((1, D))],
                 _row(tm, D), _sds((S, D)), hosts=hosts)(z2, p, lg, lb, wg, wp, pg)


def _loss_head(y, target):
    tm = 512

    def body(y_ref, t_ref, dy_ref, acc_ref):
        @pl.when(pl.program_id(0) == 0)
        def _():
            acc_ref[...] = jnp.zeros_like(acc_ref)

        e = y_ref[...] - t_ref[...]
        dy_ref[...] = e * (1.0 / D)
        acc_ref[0:1, :] += _colsum(e * e)

    return _call("loss_head", body, (S // tm,), [_row(tm, D), _row(tm, D)], [_row(tm, D), _acc(8, D)],
                 [_sds((S, D)), _sds((8, D))])(y, target)


def _zero_first(*refs):
    @pl.when(pl.program_id(0) == 0)
    def _():
        for r in refs:
            r[...] = jnp.zeros_like(r)


def _bwd_ple(g, z2, p, lg, lb, wg, wp, pg, l, hosts=()):
    tm = 256

    def body(g_ref, z_ref, p_ref, lg_ref, lb_ref, wg_ref, wp_ref, pg_ref, dz_ref, xb_ref, dgp_ref, dqp_ref, acc_ref):
        _zero_first(acc_ref)
        gin = g_ref[...]
        lgv, pgv = lg_ref[...], pg_ref[...]
        _, xhat, rstd, xb, gate, qn, rs, r = _ple_parts(z_ref[...], p_ref[...], lgv, lb_ref[...], wg_ref, wp_ref, pgv)
        xb_ref[...] = xb
        dgpb = (gin * r * gate * (1.0 - gate)).astype(BF16)
        dgp_ref[...] = dgpb
        dx2 = gin + _mmt(dgpb, wg_ref[...])
        dr = gin * gate
        acc_ref[0:1, :] += _colsum(dr * qn)
        t = dr * pgv
        dqp_ref[...] = (rs * (t - qn * jnp.mean(t * qn, axis=-1, keepdims=True))).astype(BF16)
        acc_ref[1:2, :] += _colsum(dx2 * xhat)
        acc_ref[2:3, :] += _colsum(dx2)
        dz_ref[...] = _ln_bwd(dx2 * lgv, xhat, rstd)

    return _call(f"bwd_ple_{l}", body, (S // tm,),
                 [_row(tm, D), _row(tm, D), _row(tm, 256), _const((1, D)), _const((1, D)), _const(wg.shape),
                  _wspec(wp), _const((1, D))],
                 [_row(tm, D), _row(tm, D), _row(tm, D), _row(tm, D), _acc(8, D)],
                 [_sds((S, D)), _sds((S, D), BF16), _sds((S, D), BF16), _sds((S, D), BF16), _sds((8, D))],
                 hosts=hosts)(g, z2, p, lg, lb, wg, wp, pg)


def _bwd_ffn(dz2, z1, ab, ub, lg, lb, wgt, wut, wd, l, hosts=()):
    tm = 256

    def body(dz2_ref, z_ref, a_ref, u_ref, lg_ref, lb_ref, wg_ref, wu_ref, wd_ref, dz1_ref, xb_ref, da_ref, du_ref,
             acc_ref):
        _zero_first(acc_ref)
        dz2v = dz2_ref[...]
        lgv = lg_ref[...]
        x1, xhat, rstd = _ln(z_ref[...], lgv, lb_ref[...])
        xb_ref[...] = x1.astype(BF16)
        a = a_ref[...].astype(F32)
        u = u_ref[...].astype(F32)
        sg = _sig(a)
        dhm = _mmt(dz2v, wd_ref[...])
        dub = (dhm * (a * sg)).astype(BF16)
        dab = (dhm * u * _silu_grad(a, sg)).astype(BF16)
        da_ref[...] = dab
        du_ref[...] = dub
        dx1 = ALPHA * dz2v + _mm(dab, wg_ref[...]) + _mm(dub, wu_ref[...])
        acc_ref[0:1, :] += _colsum(dx1 * xhat)
        acc_ref[1:2, :] += _colsum(dx1)
        dz1_ref[...] = _ln_bwd(dx1 * lgv, xhat, rstd)

    return _call(f"bwd_ffn_{l}", body, (S // tm,),
                 [_row(tm, D), _row(tm, D), _row(tm, FF), _row(tm, FF), _const((1, D)), _const((1, D)),
                  _const((FF, D)), _const((FF, D)), _const((FF, D))],
                 [_row(tm, D), _row(tm, D), _row(tm, FF), _row(tm, FF), _acc(8, D)],
                 [_sds((S, D)), _sds((S, D), BF16), _sds((S, FF), BF16), _sds((S, FF), BF16), _sds((8, D))],
                 hosts=hosts)(dz2, z1, ab, ub, lg, lb, wgt, wut, wd)


def _bwd_a2(dz1, cv, lg, lb, w2, l, hosts=()):
    tm = 512

    def body(dz_ref, cv_ref, lg_ref, lb_ref, w2_ref, dcv_ref, sb_ref, acc_ref):
        _zero_first(acc_ref)
        lgv = lg_ref[...]
        n, xhat, rstd = _ln(cv_ref[...], lgv, lb_ref[...])
        sg = _sig(n)
        sb_ref[...] = (n * sg).astype(BF16)
        dzb = dz_ref[...].astype(BF16)
        ds = _mmt(dzb, w2_ref[...])
        dn = ds * _silu_grad(n, sg)
        acc_ref[0:1, :] += _colsum(dn * xhat)
        acc_ref[1:2, :] += _colsum(dn)
        dcv = _ln_bwd(dn * lgv, xhat, rstd)
        acc_ref[2:3, :] += _colsum(dcv)
        dcv_ref[...] = dcv

    return _call(f"bwd_a2_{l}", body, (S // tm,),
                 [_row(tm, D), _row(tm, D), _const((1, D)), _const((1, D)), _const(w2.shape)],
                 [_row(tm, D), _row(tm, D), _acc(8, D)],
                 [_sds((S, D)), _sds((S, D), BF16), _sds((8, D))], hosts=hosts)(dz1, cv, lg, lb, w2)


def _bwd_conv_a(dcv, glu, wdw, l, hosts=()):
    tm = 256
    nb = S // tm

    def body(d_ref, dn_ref, g_ref, gp_ref, w_ref, dglu_ref, dw_ref, bufd, bufx, sh):
        i = pl.program_id(0)
        _zero_first(dw_ref)
        bufd[0:tm, :] = d_ref[...]
        _fill_halo(bufd, tm, HALO_A, lambda: dn_ref[...], i == nb - 1)
        _fill_halo(bufx, 0, HALO_A, lambda: gp_ref[...], i == 0)
        bufx[HALO_A:HALO_A + tm, :] = g_ref[...]
        _shift_copies(bufd, sh)
        _conv_rows(dglu_ref, w_ref, None, [CONV_A - 1 - k for k in range(CONV_A)], bufd, sh, tm)
        _shift_copies(bufx, sh)
        _conv_wgrad(dw_ref, d_ref, [HALO_A - (CONV_A - 1) + k for k in range(CONV_A)], bufx, sh, tm)

    return _call(f"bwd_conv_a_{l}", body, (nb,),
                 [_row(tm, D), _next(tm, HALO_A, D), _row(tm, D), _prev(tm, HALO_A, D), _const((32, D))],
                 [_row(tm, D), _acc(32, D)], [_sds((S, D)), _sds((32, D))],
                 scratch=[pltpu.VMEM((tm + HALO_A, D), F32), pltpu.VMEM((HALO_A + tm, D), F32),
                          pltpu.VMEM((SUB - 1, HALO_A + tm - SUB, D), F32)], hosts=hosts)(dcv, dcv, glu, glu, wdw)


def _bwd_a1(dglu, h, dz1, w1, l, hosts=()):
    tm = 256

    def body(dg_ref, h_ref, dz_ref, w_ref, dx_ref, dh_ref, acc_ref):
        _zero_first(acc_ref)
        a, g = h_ref[:, :D], h_ref[:, D:]
        sg = _sig(g)
        dgl = dg_ref[...]
        da = dgl * sg
        dg = dgl * a * sg * (1.0 - sg)
        acc_ref[0:1, 0:D] += _colsum(da)
        acc_ref[0:1, D:2 * D] += _colsum(dg)
        dh_ref[:, 0:D] = da.astype(BF16)
        dh_ref[:, D:2 * D] = dg.astype(BF16)
        dx = ALPHA * dz_ref[...]
        for q in range(NQ):
            dx = dx + _mmt(dh_ref[:, q * 512:(q + 1) * 512], w_ref[q])
        dx_ref[...] = dx

    return _call(f"bwd_a1_{l}", body, (S // tm,),
                 [_row(tm, D), _row(tm, 2 * D), _row(tm, D), _wspec(w1)],
                 [_row(tm, D), _row(tm, 2 * D), _acc(8, 2 * D)],
                 [_sds((S, D)), _sds((S, 2 * D), BF16), _sds((8, 2 * D))], hosts=hosts)(dglu, h, dz1, w1)


def _bwd_c2(dz1, hc, wc, wout):
    tm = 256

    def body(dz_ref, hc_ref, hcp_ref, wc_ref, wout_ref, dy_ref, dbg_ref, mb_ref, buf):
        y = _short_conv(buf, hc_ref, hcp_ref, wc_ref, tm, pl.program_id(0))
        dzb = dz_ref[...].astype(BF16)
        dm = _mmt(dzb, wout_ref[...])
        bg = hc_ref[:, :D]
        mb_ref[...] = (bg * y).astype(BF16)
        dbg_ref[...] = (dm * y).astype(BF16)
        dy_ref[...] = dm * bg

    return _call("bwd_c2", body, (S // tm,),
                 [_row(tm, D), _row(tm, 3 * D), _prev(tm, HALO_C, 3 * D), _const((8, D)), _const(wout.shape)],
                 [_row(tm, D), _row(tm, D), _row(tm, D)],
                 [_sds((S, D)), _sds((S, D), BF16), _sds((S, D), BF16)],
                 scratch=[pltpu.VMEM((HALO_C + tm, D), F32)])(dz1, hc, hc, wc, wout)


def _bwd_c1(dy, hc, dbg, dz1, wc, win):
    tm = 256
    nb = S // tm

    def body(d_ref, dn_ref, hc_ref, hcp_ref, dbg_ref, dz_ref, wc_ref, win_ref, dx_ref, dhc_ref, dwc_ref, bufd, bufq):
        i = pl.program_id(0)
        _zero_first(dwc_ref)
        bufd[0:tm, :] = d_ref[...]
        _fill_halo(bufd, tm, HALO_C, lambda: dn_ref[...], i == nb - 1)
        _fill_halo(bufq, 0, HALO_C, lambda: hcp_ref[:, D:2 * D] * hcp_ref[:, 2 * D:], i == 0)
        bufq[HALO_C:HALO_C + tm, :] = hc_ref[:, D:2 * D] * hc_ref[:, 2 * D:]
        dq = wc_ref[0:1, :] * bufd[pl.ds(CONV_C - 1, tm), :]
        for k in range(1, CONV_C):
            dq = dq + wc_ref[k:k + 1, :] * bufd[pl.ds(CONV_C - 1 - k, tm), :]
        dv = d_ref[...]
        for k in range(CONV_C):
            dwc_ref[k:k + 1, :] += _colsum(dv * bufq[pl.ds(HALO_C - (CONV_C - 1) + k, tm), :])
        dhc_ref[:, 0:D] = dbg_ref[...]
        dhc_ref[:, D:2 * D] = (dq * hc_ref[:, 2 * D:]).astype(BF16)
        dhc_ref[:, 2 * D:3 * D] = (dq * hc_ref[:, D:2 * D]).astype(BF16)
        dx = ALPHA * dz_ref[...]
        for q in range(NQ):
            dx = dx + _mmt(dhc_ref[:, q * 768:(q + 1) * 768], win_ref[q])
        dx_ref[...] = dx

    return _call("bwd_c1", body, (nb,),
                 [_row(tm, D), _next(tm, HALO_C, D), _row(tm, 3 * D), _prev(tm, HALO_C, 3 * D), _row(tm, D),
                  _row(tm, D), _const((8, D)), _wspec(win)],
                 [_row(tm, D), _row(tm, 3 * D), _acc(8, D)],
                 [_sds((S, D)), _sds((S, 3 * D), BF16), _sds((8, D))],
                 scratch=[pltpu.VMEM((tm + HALO_C, D), F32), pltpu.VMEM((HALO_C + tm, D), F32)]
                 )(dy, dy, hc, hc, dbg, dz1, wc, win)


def _bwd_b(dz1, zg, gg, lg, lb, win, wout, ws, wst, bsx):
    tm = 128
    nb = S // tm

    def body(dz_ref, zg_ref, gg_ref, lg_ref, lb_ref, win_ref, wout_ref, ws_ref, wst_ref, bsx_ref,
             dx_ref, dh_ref, mb_ref, acc_ref, dws_ref, dbs_ref, f_scr, dvn_scr):
        _zero_first(acc_ref, dws_ref, dbs_ref)
        lgv = lg_ref[...]
        u = zg_ref[:, :E].astype(F32)
        v = zg_ref[:, E:].astype(F32)
        vn, xhat, rstd = _ln(v, lgv, lb_ref[...])
        vnb = vn.astype(BF16)
        dzb = dz_ref[...].astype(BF16)
        dm = _mmt(dzb, wout_ref[...])
        mask, mask_t = _sgu_masks()
        for hd in range(SGU_H):
            wm = jnp.where(mask, ws_ref[hd], 0.0).astype(BF16)
            cs = slice(hd * SGU_G, (hd + 1) * SGU_G)
            for n in range(tm // SGU_T):
                rs = slice(n * SGU_T, (n + 1) * SGU_T)
                f_scr[rs, cs] = jnp.dot(wm, vnb[rs, cs], preferred_element_type=F32) + bsx_ref[hd]
        f = f_scr[...]
        mb_ref[...] = (u * f).astype(BF16)
        du = dm * f
        df = dm * u
        dfb = df.astype(BF16)
        for hd in range(SGU_H):
            wmt = jnp.where(mask_t, wst_ref[hd], 0.0).astype(BF16)
            cs = slice(hd * SGU_G, (hd + 1) * SGU_G)
            for n in range(tm // SGU_T):
                rs = slice(n * SGU_T, (n + 1) * SGU_T)
                dvn_scr[rs, cs] = jnp.dot(wmt, dfb[rs, cs], preferred_element_type=F32)
                dws_ref[hd] += lax.dot_general(dfb[rs, cs], vnb[rs, cs], (((1,), (1,)), ((), ())),
                                               preferred_element_type=F32)
                dbs_ref[hd] += df[rs, cs]
        dvn = dvn_scr[...]
        acc_ref[1:2, 0:E] += _colsum(dvn * xhat)
        acc_ref[2:3, 0:E] += _colsum(dvn)
        dv = _ln_bwd(dvn * lgv, xhat, rstd)
        dhu = du * gg_ref[:, :E].astype(F32)
        dhv = dv * gg_ref[:, E:].astype(F32)
        acc_ref[0:1, 0:E] += _colsum(dhu)
        acc_ref[0:1, E:2 * E] += _colsum(dhv)
        dh_ref[:, 0:E] = dhu.astype(BF16)
        dh_ref[:, E:2 * E] = dhv.astype(BF16)
        dx = ALPHA * dz_ref[...]
        for q in range(NQ):
            dx = dx + _mmt(dh_ref[:, q * 1024:(q + 1) * 1024], win_ref[q])
        dx_ref[...] = dx

        @pl.when(pl.program_id(0) == nb - 1)
        def _():
            for hd in range(SGU_H):
                dws_ref[hd] = jnp.where(mask, dws_ref[hd], 0.0)

    c3 = lambda a, b, c: pl.BlockSpec((a, b, c), lambda i: (0, 0, 0))
    return _call("bwd_b", body, (nb,),
                 [_row(tm, D), _row(tm, 2 * E), _row(tm, 2 * E), _const((1, E)), _const((1, E)), _wspec(win),
                  _const(wout.shape), _const((SGU_H, SGU_T, SGU_T)), _const((SGU_H, SGU_T, SGU_T)),
                  _const((SGU_H, SGU_T, SGU_G))],
                 [_row(tm, D), _row(tm, 2 * E), _row(tm, E), _acc(8, 2 * E), c3(SGU_H, SGU_T, SGU_T),
                  c3(SGU_H, SGU_T, SGU_G)],
                 [_sds((S, D)), _sds((S, 2 * E), BF16), _sds((S, E), BF16), _sds((8, 2 * E)),
                  _sds((SGU_H, SGU_T, SGU_T)), _sds((SGU_H, SGU_T, SGU_G))],
                 scratch=[pltpu.VMEM((tm, E), F32), pltpu.VMEM((tm, E), F32)]
                 )(dz1, zg, gg, lg, lb, win, wout, ws, wst, bsx)


def _mm_tn(name, a, amode, b, bmode, k, n, groups=NQ, hosts=()):
    def block_bytes(ts):
        ka = k if amode == "1" else groups * k
        nb = n if bmode == "1" else groups * n
        return 2 * (ts * ka * a.dtype.itemsize + ts * nb * b.dtype.itemsize + groups * k * n * 4)

    ts = min(1024 if block_bytes(1024) <= DW_BLOCK_BUDGET else 512, S)

    def spec(mode, w):
        if mode == "1":
            return pl.BlockSpec((ts, w), lambda s: (s, 0))
        if mode == "c":
            return pl.BlockSpec((ts, groups * w), lambda s: (s, 0))
        return pl.BlockSpec((groups, ts, w), lambda s: (0, s, 0))

    def pick(ref, mode, w, g):
        if mode == "1":
            return ref[...]
        if mode == "c":
            return ref[:, g * w:(g + 1) * w]
        return ref[g]

    def body(a_ref, b_ref, o_ref):
        _zero_first(o_ref)
        a_t = jnp.transpose(a_ref[...].astype(BF16)) if amode == "1" else None
        b_1 = b_ref[...].astype(BF16) if bmode == "1" else None
        for g in range(groups):
            lhs = a_t if amode == "1" else jnp.transpose(pick(a_ref, amode, k, g).astype(BF16))
            rhs = b_1 if bmode == "1" else pick(b_ref, bmode, n, g).astype(BF16)
            o_ref[0, g] += jnp.dot(lhs, rhs, preferred_element_type=F32)

    return _call(name, body, (S // ts,), [spec(amode, k), spec(bmode, n)],
                 pl.BlockSpec((1, groups, k, n), lambda s: (0, 0, 0, 0)), _sds((1, groups, k, n)), hosts=hosts)(a, b)


STREAM_STEPS = 8


def _stream_block(k, steps=STREAM_STEPS):
    return min((t for t in range(16, k + 1, 16) if k % t == 0), key=lambda t: abs(k // t - steps))


def _row_block(k, cap=256):
    return max(t for t in range(16, min(k, cap) + 1, 16) if k % t == 0)


def _cast_bf16(w, hosts=()):
    nl, k, n = w.shape
    tb = _row_block(k, 512)
    nb = k // tb

    def body(w_ref, o_ref):
        o_ref[...] = w_ref[...].astype(BF16)

    spec = pl.BlockSpec((None, tb, n), lambda i: (i // nb, i % nb, 0))
    return _call("cast_bf16", body, (nl * nb,), [spec], spec, _sds(w.shape, BF16), hosts=hosts)(w)


def _adam(name, w, m, v, gc, l, prev):
    nl, k, n = w.shape
    nc = gc.shape[0]
    tb = _stream_block(k)

    def body(w_ref, m_ref, v_ref, g_ref, *rest):
        go_ref, d_ref, mo_ref, vo_ref = rest[-4:]
        g = g_ref[0].astype(F32)
        for c in range(1, nc):
            g = g + g_ref[c].astype(F32)
        m2 = ADAM_B1 * m_ref[...] + (1.0 - ADAM_B1) * g
        v2 = ADAM_B2 * v_ref[...] + (1.0 - ADAM_B2) * (g * g)
        m_hat = m2 / (1.0 - ADAM_B1 ** ADAM_STEP)
        v_hat = v2 / (1.0 - ADAM_B2 ** ADAM_STEP)
        go_ref[...] = g
        d_ref[...] = -ADAM_LR * (m_hat / (jnp.sqrt(v_hat) + ADAM_EPS) + ADAM_WD * w_ref[...])
        mo_ref[...] = m2
        vo_ref[...] = v2

    spec = pl.BlockSpec((None, tb, n), lambda i: (l, i, 0))
    gspec = pl.BlockSpec((nc, None, tb, n), lambda i: (0, 0, i, 0))
    in_specs, args, aliases = [spec, spec, spec, gspec], [w, m, v, gc], {}
    if prev is not None:
        in_specs += [pl.BlockSpec(memory_space=pl.ANY)] * 4
        args += list(prev)
        aliases = {4 + j: j for j in range(4)}
    return _call(name, body, (k // tb,), in_specs, [spec] * 4, [_sds(w.shape)] * 4, aliases=aliases)(*args)


def _sum8(name, g8):
    r = g8.shape[1]

    def body(g_ref, o_ref):
        acc = g_ref[0]
        for d in range(1, 8):
            acc = acc + g_ref[d]
        o_ref[...] = acc

    return _call(name, body, (1,), [pl.BlockSpec((8, r, 128), lambda i: (0, 0, 0))],
                 pl.BlockSpec((r, 128), lambda i: (0, 0)), _sds((r, 128)))(g8)


def _place():
    x, y, c = lax.axis_index("x"), lax.axis_index("y"), lax.axis_index("c")
    return x, y, c, 2 * x + y, (x, y, 1 - c), [(1 - x, y), (x, 1 - y), (1 - x, 1 - y)]


class _Exchange:
    def __init__(self, arrays, out_shapes):
        self.arrays, self.out_shapes = list(arrays), list(out_shapes)
        n = len(self.arrays)
        self.sems = [pltpu.SemaphoreType.DMA((7 * n,)), pltpu.SemaphoreType.DMA((7 * n,)),
                     pltpu.SemaphoreType.DMA((n,))]

    def _copies(self, ins, outs, sems):
        send, recv, lsem = sems
        local_src, remote_src, dst = self.maps(ins, outs)
        x, y, c, q, sib, chips = _place()

        def rcopy(w, k, qq, cc, to, src=None):
            return pltpu.make_async_remote_copy(
                src_ref=dst(w, qq, cc) if src is None else src, dst_ref=dst(w, qq, cc),
                send_sem=send.at[7 * w + k], recv_sem=recv.at[7 * w + k], device_id=to, device_id_type=MESH)

        def mine(w):
            return pltpu.make_async_copy(local_src(w), dst(w, q, c), lsem.at[w])

        def first(w):
            return [rcopy(w, 0, q, c, sib, local_src(w))] + [
                rcopy(w, 1 + j, q, c, (cx, cy, c), remote_src(w, 2 * cx + cy)) for j, (cx, cy) in enumerate(chips)]

        return rcopy, mine, first, (x, y, c), q, c, sib, chips

    def start(self, ins, outs, sems):
        _, mine, first, *_ = self._copies(ins, outs, sems)
        for w in range(len(self.arrays)):
            mine(w).start()
            for cp in first(w):
                cp.start()

    def finish(self, ins, outs, sems):
        rcopy, mine, first, me, q, c, sib, chips = self._copies(ins, outs, sems)
        n = len(self.arrays)
        for w in range(n):
            for j, (cx, cy) in enumerate(chips):
                rcopy(w, 1 + j, 2 * cx + cy, c, me).wait_recv()
                rcopy(w, 4 + j, 2 * cx + cy, c, sib).start()
        for w in range(n):
            rcopy(w, 0, q, 1 - c, me).wait_recv()
            for j, (cx, cy) in enumerate(chips):
                rcopy(w, 4 + j, 2 * cx + cy, 1 - c, me).wait_recv()
        for w in range(n):
            for cp in first(w):
                cp.wait_send()
            for j, (cx, cy) in enumerate(chips):
                rcopy(w, 4 + j, 2 * cx + cy, c, sib).wait_send()
            mine(w).wait()


class _GatherWeights(_Exchange):
    def __init__(self, items):
        self.layers = [l for _, l in items]
        self.kh = [s.shape[1] // 2 for s, _ in items]
        super().__init__([s for s, _ in items], [_sds((NQ, 1) + s.shape[1:], BF16) for s, _ in items])

    def maps(self, ins, outs):
        c = lax.axis_index("c")
        src = lambda w: ins[w].at[pl.ds(self.layers[w], 1), pl.ds(c * self.kh[w], self.kh[w]), :]
        return src, lambda w, q: src(w), lambda w, q, cc: outs[w].at[q, :, pl.ds(cc * self.kh[w], self.kh[w]), :]


class _ScatterPartials(_Exchange):
    def __init__(self, parts):
        super().__init__(parts, [_sds((NQ, 1, 2) + p.shape[2:], BF16) for p in parts])

    def maps(self, ins, outs):
        q = 2 * lax.axis_index("x") + lax.axis_index("y")
        return (lambda w: ins[w].at[:, q]), (lambda w, qq: ins[w].at[:, qq]), (lambda w, qq, cc: outs[w].at[qq, :, cc])


class _Gather8(_Exchange):
    def __init__(self, v):
        super().__init__([v], [_sds((8,) + v.shape)])

    def maps(self, ins, outs):
        return (lambda w: ins[0]), (lambda w, q: ins[0]), (lambda w, q, cc: outs[0].at[2 * q + cc])


class _SwapHalves:
    def __init__(self, dws):
        self.arrays = list(dws)
        self.kh = [d.shape[2] // 2 for d in dws]
        self.out_shapes = [_sds(d.shape[:2] + (kh,) + d.shape[3:]) for d, kh in zip(dws, self.kh)]
        self.sems = [pltpu.SemaphoreType.DMA((len(dws),)), pltpu.SemaphoreType.DMA((len(dws),))]

    def _copies(self, ins, outs, sems):
        send, recv = sems
        _, _, c, _, sib, _ = _place()
        return [pltpu.make_async_remote_copy(
            src_ref=ins[w].at[:, :, pl.ds((1 - c) * self.kh[w], self.kh[w]), :], dst_ref=outs[w],
            send_sem=send.at[w], recv_sem=recv.at[w], device_id=sib, device_id_type=MESH)
            for w in range(len(self.arrays))]

    def start(self, ins, outs, sems):
        for cp in self._copies(ins, outs, sems):
            cp.start()

    def finish(self, ins, outs, sems):
        for cp in self._copies(ins, outs, sems):
            cp.wait()


def _comm_only(name, host):
    n_in, n_out = len(host.arrays), len(host.out_shapes)

    def body(*refs):
        ins, outs, sems = refs[:n_in], refs[n_in:n_in + n_out], refs[n_in + n_out:]
        host.start(ins, outs, sems)
        host.finish(ins, outs, sems)

    any_spec = pl.BlockSpec(memory_space=pl.ANY)
    return pl.pallas_call(body, name=name, in_specs=[any_spec] * n_in, out_specs=[any_spec] * n_out,
                          out_shape=host.out_shapes, scratch_shapes=host.sems)(*host.arrays)


def _add_halves(dw, got, cidx):
    nl, _, k, n = dw.shape
    kh = k // 2
    tb = _stream_block(kh, STREAM_STEPS // NQ)

    def body(c_ref, a_ref, b_ref, o_ref):
        o_ref[...] = (a_ref[...] + b_ref[...]).astype(BF16)

    grid_spec = pltpu.PrefetchScalarGridSpec(
        num_scalar_prefetch=1, grid=(NQ, kh // tb),
        in_specs=[pl.BlockSpec((None, None, None, tb, n), lambda q, i, c_ref: (0, q, c_ref[0], i, 0)),
                  pl.BlockSpec((None, None, tb, n), lambda q, i, c_ref: (0, q, i, 0))],
        out_specs=pl.BlockSpec((None, None, tb, n), lambda q, i, c_ref: (0, q, i, 0)))
    return pl.pallas_call(
        body, name="add_halves", grid_spec=grid_spec, out_shape=_sds((nl, NQ, kh, n), BF16),
        compiler_params=pltpu.CompilerParams(dimension_semantics=("arbitrary", "arbitrary"),
                                             vmem_limit_bytes=VMEM_LIMIT))(cidx, dw.reshape(nl, NQ, 2, kh, n), got)


def _gather8(name, v):
    return _comm_only(name, _Gather8(v))[0]


PACK = 16 * 128


def _pack(arrays):
    parts = []
    for a in arrays:
        flat = a.reshape(-1)
        parts.append(jnp.pad(flat, (0, (-flat.shape[0]) % PACK)))
    return jnp.concatenate(parts).reshape(-1, 128)


def _unpack(packed, shapes):
    flat = packed.reshape(-1)
    out, off = [], 0
    for shp in shapes:
        size = 1
        for d in shp:
            size *= d
        out.append(flat[off:off + size].reshape(shp))
        off += size + (-size) % PACK
    return out


def kernel(x, p, a_w_pw1, a_b_pw1, a_w_dw, a_b_dw, a_ln_g, a_ln_b, a_w_pw2, b_w_in, b_b_in, b_ln_g, b_ln_b, b_w_s, b_b_s, b_w_out, c_w_in, c_w_conv, c_w_out, ln1_g, ln1_b, ln2_g, ln2_b, ffn_w_gate, ffn_w_up, ffn_w_down, ple_w_gate, ple_w_proj, ple_norm_g, loss_target, m_a_w_pw1, m_a_b_pw1, m_a_w_dw, m_a_b_dw, m_a_ln_g, m_a_ln_b, m_a_w_pw2, m_b_w_in, m_b_b_in, m_b_ln_g, m_b_ln_b, m_b_w_s, m_b_b_s, m_b_w_out, m_c_w_in, m_c_w_conv, m_c_w_out, m_ln1_g, m_ln1_b, m_ln2_g, m_ln2_b, m_ffn_w_gate, m_ffn_w_up, m_ffn_w_down, m_ple_w_gate, m_ple_w_proj, m_ple_norm_g, v_a_w_pw1, v_a_b_pw1, v_a_w_dw, v_a_b_dw, v_a_ln_g, v_a_ln_b, v_a_w_pw2, v_b_w_in, v_b_b_in, v_b_ln_g, v_b_ln_b, v_b_w_s, v_b_b_s, v_b_w_out, v_c_w_in, v_c_w_conv, v_c_w_out, v_ln1_g, v_ln1_b, v_ln2_g, v_ln2_b, v_ffn_w_gate, v_ffn_w_up, v_ffn_w_down, v_ple_w_gate, v_ple_w_proj, v_ple_norm_g):
    args = dict(locals())
    wts = {k: args[k] for k in WEIGHTS}
    mom = {k: args["m_" + k] for k in WEIGHTS}
    var = {k: args["v_" + k] for k in WEIGHTS}
    for k in TRANSPOSED:
        wts[k], mom[k], var[k] = (jnp.transpose(t[k], (0, 2, 1)) for t in (wts, mom, var))
    q_idx = 2 * lax.axis_index("x") + lax.axis_index("y")
    c_idx = lax.axis_index("c").astype(jnp.int32).reshape(1)

    wb = {k: _cast_bf16(wts[k]) for k in BIG if k not in ("ffn_w_gate", "ffn_w_up")}
    mixw = [[("a_w_pw1", 0), ("a_w_pw2", 0)], [("b_w_in", 0), ("b_w_out", 0)], [("c_w_in", 0), ("c_w_out", 0)],
            [("a_w_pw1", 1), ("a_w_pw2", 1)]]
    ffnw = [[("ffn_w_gate", l), ("ffn_w_up", l), ("ffn_w_down", l)] for l in range(DEPTH)]
    plew = [[("ple_w_gate", l), ("ple_w_proj", l)] for l in range(DEPTH)]
    fwd_plan = {("a1", 0): mixw[0][1:] + plew[0], ("a2", 0): ffnw[0], ("ffn", 0): mixw[1], ("ple", 0): plew[1],
                ("b", 1): ffnw[1], ("ffn", 1): mixw[2] + ffnw[2][:1], ("ple", 1): plew[2],
                ("c1", 2): ffnw[2][1:2], ("c2", 2): ffnw[2][2:], ("ffn", 2): mixw[3] + ffnw[3][:1], ("ple", 2): plew[3],
                ("a1", 3): ffnw[3][1:2], ("a2", 3): ffnw[3][2:]}
    gw = {}

    def gather(keys):
        return _GatherWeights([(wb[name], l) for name, l in keys])

    def hosted(tag, fn, *fargs):
        keys = fwd_plan.get(tag)
        if not keys:
            return fn(*fargs)
        own, (got,) = fn(*fargs, hosts=[gather(keys)])
        store(keys, got)
        return own

    def store(keys, got):
        for (name, l), arr in zip(keys, got):
            gw[name, l] = arr.reshape(NQ * arr.shape[2], arr.shape[3]) if name in ROW_SHARDED else arr

    first_keys = mixw[0][:1]
    wb["ffn_w_gate"], (got,) = _cast_bf16(wts["ffn_w_gate"], hosts=[gather(first_keys)])
    store(first_keys, got)
    shard_shapes = [wts[k].shape for k in SMALL_SHARDED]
    wb["ffn_w_up"], ((small8,),) = _cast_bf16(wts["ffn_w_up"], hosts=[_Gather8(_pack([wts[k] for k in SMALL_SHARDED]))])
    per_chip = [_unpack(small8[2 * qq], shard_shapes) for qq in range(NQ)]
    full = {k: jnp.concatenate([per_chip[qq][i] for qq in range(NQ)], axis=-1) for i, k in enumerate(SMALL_SHARDED)}
    for k in SMALL_REPL:
        full[k] = wts[k]

    def vec(name, l):
        return full[name][l][None, :]

    def conv_w(name, l, rows):
        w = full[name][l]
        return jnp.pad(w, ((0, rows - w.shape[0]), (0, 0)))

    ws = full["b_w_s"][0]
    wst = jnp.transpose(ws, (0, 2, 1))
    bsx = jnp.broadcast_to(full["b_b_s"][0][:, :, None], (SGU_H, SGU_T, SGU_G))

    x0s, z1s, z2s, saved, ffn_saved = [], [], [], [], []
    cur = x[0]
    for i in range(DEPTH):
        mix, j = i % 3, i // 3
        x0s.append(cur)
        if mix == 0:
            h, glu = hosted(("a1", i), _fwd_a1, cur, gw["a_w_pw1", j], vec("a_b_pw1", j), i)
            z1, cv = hosted(("a2", i), _fwd_a2, glu, cur, conv_w("a_w_dw", j, 32), vec("a_b_dw", j), vec("a_ln_g", j),
                            vec("a_ln_b", j), gw["a_w_pw2", j], i)
            saved.append((h, glu, cv))
        elif mix == 1:
            z1, zg, gg = hosted(("b", i), _fwd_b, cur, gw["b_w_in", 0], vec("b_b_in", 0), vec("b_ln_g", 0),
                                vec("b_ln_b", 0), ws, bsx, gw["b_w_out", 0])
            saved.append((zg, gg))
        else:
            hc = hosted(("c1", i), _fwd_c1, cur, gw["c_w_in", 0])
            z1 = hosted(("c2", i), _fwd_c2, hc, cur, conv_w("c_w_conv", 0, 8), gw["c_w_out", 0])
            saved.append((hc,))
        z2, ab, ub, hm = hosted(("ffn", i), _fwd_ffn, z1, vec("ln1_g", i), vec("ln1_b", i), gw["ffn_w_gate", i],
                                gw["ffn_w_up", i], gw["ffn_w_down", i], i)
        ffn_saved.append((ab, ub, hm))
        cur = hosted(("ple", i), _fwd_ple, z2, p[i, 0], vec("ln2_g", i), vec("ln2_b", i), gw["ple_w_gate", i],
                     gw["ple_w_proj", i], vec("ple_norm_g", i), i)
        z1s.append(z1)
        z2s.append(z2)

    g, loss_acc = _loss_head(cur, loss_target[0])
    loss = lax.psum(0.5 / D * jnp.sum(loss_acc[0]), ("x", "y", "c"))

    dws = {}
    sg = {}
    res = {k: None for k in BIG}

    def wgrad(name, l, a, amode, b, bmode, scatter_keys=()):
        _, k, n = wts[name].shape
        hosts = [_ScatterPartials([parts[key] for key in scatter_keys])] if scatter_keys else ()
        if name in ROW_SHARDED:
            out = _mm_tn(f"dw_{name}_{l}", a, "1", b, "1", NQ * k, n, groups=1, hosts=hosts)
        else:
            out = _mm_tn(f"dw_{name}_{l}", a, amode, b, bmode, k, n, hosts=hosts)
        if scatter_keys:
            out, (contribs,) = out
            update(scatter_keys, contribs)
        dws[name, l] = out.reshape(1, NQ, k, n)

    def swap(keys):
        return _SwapHalves([dws[k] for k in keys])

    parts = {}

    def add_halves(keys, got):
        parts.update((k, _add_halves(dws[k], r, c_idx)) for k, r in zip(keys, got))

    def update(keys, contribs):
        for (name, l), gc in zip(keys, contribs):
            _, kq, n = wts[name].shape
            res[name] = _adam(f"adam_{name}_{l}", wts[name], mom[name], var[name], gc.reshape(NQ, 1, kq, n), l,
                              res[name])

    small = SMALL_SHARDED + SMALL_REPL
    late_small = [("a_w_dw", 0), ("a_b_pw1", 0)]
    early_small = [(k, l) for k in small for l in range(full[k].shape[0]) if (k, l) not in late_small]
    pending = None
    for i in reversed(range(DEPTH)):
        mix, j = i % 3, i // 3
        ple_args = (g, z2s[i], p[i, 0], vec("ln2_g", i), vec("ln2_b", i), gw["ple_w_gate", i], gw["ple_w_proj", i],
                    vec("ple_norm_g", i), i)
        if pending:
            (dz2, x2b, dgp, dqp, acc), (got,) = _bwd_ple(*ple_args, hosts=[swap(pending)])
            add_halves(pending, got)
        else:
            dz2, x2b, dgp, dqp, acc = _bwd_ple(*ple_args)
        sg["ple_norm_g", i], sg["ln2_g", i], sg["ln2_b", i] = acc[0], acc[1], acc[2]
        wgrad("ple_w_gate", i, x2b, "c", dgp, "1")
        wgrad("ple_w_proj", i, p[i, 0], "1", dqp, "c")
        ab, ub, hm = ffn_saved[i]
        ffn_args = (dz2, z1s[i], ab, ub, vec("ln1_g", i), vec("ln1_b", i), gw["ffn_w_gate", i], gw["ffn_w_up", i],
                    gw["ffn_w_down", i], i)
        if pending:
            (dz1, x1b, da, du, acc), (contribs,) = _bwd_ffn(
                *ffn_args, hosts=[_ScatterPartials([parts[key] for key in ffnw[i + 1]])])
            update(ffnw[i + 1], contribs)
        else:
            dz1, x1b, da, du, acc = _bwd_ffn(*ffn_args)
        sg["ln1_g", i], sg["ln1_b", i] = acc[0], acc[1]
        wgrad("ffn_w_gate", i, da, "1", x1b, "1", scatter_keys=mixw[i + 1][:1] if pending else ())
        wgrad("ffn_w_up", i, du, "1", x1b, "1", scatter_keys=mixw[i + 1][1:] + plew[i + 1] if pending else ())
        wgrad("ffn_w_down", i, hm, "1", dz2, "1")
        x0 = x0s[i]
        if mix == 0:
            h, glu, cv = saved[i]
            a2_args = (dz1, cv, vec("a_ln_g", j), vec("a_ln_b", j), gw["a_w_pw2", j], i)
            conv_args = (glu, conv_w("a_w_dw", j, 32), i)
            if i == 0:
                early = ffnw[0] + plew[0]
                (dcv, sb, acc), (got,) = _bwd_a2(*a2_args, hosts=[swap(early)])
                sg["a_ln_g", j], sg["a_ln_b", j], sg["a_b_dw", j] = acc[0], acc[1], acc[2]
                add_halves(early, got)
                wgrad("a_w_pw2", j, sb, "c", dz1, "1")
                (dglu, dwdw), (contribs, (g8_early,), got) = _bwd_conv_a(
                    dcv, *conv_args, hosts=[_ScatterPartials([parts[key] for key in early]),
                                            _Gather8(_pack([sg[pc] for pc in early_small])), swap(mixw[0][1:])])
                update(early, contribs)
                add_halves(mixw[0][1:], got)
                (g, dh, acc), (contribs,) = _bwd_a1(dglu, h, dz1, gw["a_w_pw1", j], i,
                                                    hosts=[_ScatterPartials([parts[key] for key in mixw[0][1:]])])
                update(mixw[0][1:], contribs)
            else:
                dcv, sb, acc = _bwd_a2(*a2_args)
                sg["a_ln_g", j], sg["a_ln_b", j], sg["a_b_dw", j] = acc[0], acc[1], acc[2]
                dglu, dwdw = _bwd_conv_a(dcv, *conv_args)
                wgrad("a_w_pw2", j, sb, "c", dz1, "1")
                g, dh, acc = _bwd_a1(dglu, h, dz1, gw["a_w_pw1", j], i)
            sg["a_w_dw", j] = dwdw[:CONV_A]
            sg["a_b_pw1", j] = acc[0]
            wgrad("a_w_pw1", j, x0, "1", dh, "c")
        elif mix == 1:
            zg, gg = saved[i]
            g, dh, mb, acc, dw_s, db_s = _bwd_b(dz1, zg, gg, vec("b_ln_g", 0), vec("b_ln_b", 0), gw["b_w_in", 0],
                                                gw["b_w_out", 0], ws, wst, bsx)
            sg["b_b_in", 0], sg["b_ln_g", 0], sg["b_ln_b", 0] = acc[0], acc[1, :E], acc[2, :E]
            sg["b_w_s", 0], sg["b_b_s", 0] = dw_s, jnp.sum(db_s, axis=-1)
            wgrad("b_w_out", 0, mb, "c", dz1, "1")
            wgrad("b_w_in", 0, x0, "1", dh, "c")
        else:
            (hc,) = saved[i]
            wc = conv_w("c_w_conv", 0, 8)
            dy, dbg, mb = _bwd_c2(dz1, hc, wc, gw["c_w_out", 0])
            wgrad("c_w_out", 0, mb, "c", dz1, "1")
            g, dhc, dwc = _bwd_c1(dy, hc, dbg, dz1, wc, gw["c_w_in", 0])
            sg["c_w_conv", 0] = dwc[:CONV_C]
            wgrad("c_w_in", 0, x0, "1", dhc, "c")
        pending = mixw[i] + ffnw[i] + plew[i] if i > 0 else mixw[0][:1]
    grad_x = g[None]
    add_halves(pending, _comm_only("swap_last", swap(pending)))
    update(pending, _comm_only("scatter_last", _ScatterPartials([parts[key] for key in pending])))

    g8_late = _gather8("gather_small_late", _pack([sg[pc] for pc in late_small]))
    sums = dict(zip(early_small, _unpack(_sum8("sum8_early", g8_early), [sg[pc].shape for pc in early_small])))
    sums.update(zip(late_small, _unpack(_sum8("sum8_late", g8_late), [sg[pc].shape for pc in late_small])))
    gsum = [jnp.stack([sums[k, l] for l in range(full[k].shape[0])]) for k in small]
    gmine = []
    for k, gs in zip(small, gsum):
        if k in SMALL_SHARDED:
            wdt = wts[k].shape[-1]
            gs = lax.dynamic_slice_in_dim(gs, q_idx * wdt, wdt, axis=gs.ndim - 1)
        gmine.append(gs)
    packed = [_pack(t)[None] for t in ([wts[k] for k in small], [mom[k] for k in small], [var[k] for k in small])]
    outs = _adam("adam_small", packed[0], packed[1], packed[2], _pack(gmine)[None, None], 0, None)
    unpacked = [_unpack(o[0], [wts[k].shape for k in small]) for o in outs]
    for i, k in enumerate(small):
        res[k] = tuple(u[i] for u in unpacked)

    for k in TRANSPOSED:
        res[k] = tuple(jnp.transpose(r, (0, 2, 1)) for r in res[k])
    return (loss, grad_x, *[res[k][0] for k in WEIGHTS], *[res[k][1] for k in WEIGHTS],
            *[res[k][2] for k in WEIGHTS], *[res[k][3] for k in WEIGHTS])
```

```python
import jax
import jax.numpy as jnp
from jax import lax
from jax.experimental import pallas as pl
from jax.experimental.pallas import tpu as pltpu

F32, BF16 = jnp.float32, jnp.bfloat16
S = 4096
D = 1024
E = 2048
FF = 2816
FQ = FF // 4
NQ = 4
DEPTH = 4
ALPHA = (2 * DEPTH) ** 0.25
LN_EPS = 1e-5
CONV_A, CONV_C = 31, 3
HALO_A, HALO_C = 32, 8
SGU_T, SGU_H, SGU_G, SGU_CHUNK = 128, 8, 256, 64
VMEM_LIMIT = 56 * 1024 * 1024
DW_BLOCK_BUDGET = 40 * 1024 * 1024
MESH = pl.DeviceIdType.MESH
ADAM_LR, ADAM_B1, ADAM_B2, ADAM_EPS, ADAM_WD, ADAM_STEP = 0.001, 0.9, 0.999, 1e-08, 0.01, 10
GELU_C, GELU_A = 0.7978845608028654, 0.044715

BIG = ["a_w_pw1", "a_w_pw2", "b_w_in", "b_w_out", "c_w_in", "c_w_out",
       "ffn_w_gate", "ffn_w_up", "ffn_w_down", "ple_w_gate", "ple_w_proj"]
TRANSPOSED = ["ffn_w_gate", "ffn_w_up"]
ROW_SHARDED = ["a_w_pw2", "b_w_out", "c_w_out", "ffn_w_gate", "ffn_w_up", "ffn_w_down", "ple_w_gate"]
SMALL_SHARDED = ["a_b_pw1", "a_w_dw", "a_b_dw", "a_ln_g", "a_ln_b", "c_w_conv"]
SMALL_REPL = ["b_b_in", "b_ln_g", "b_ln_b", "b_w_s", "b_b_s", "ln1_g", "ln1_b", "ln2_g", "ln2_b", "ple_norm_g"]
WEIGHTS = ["a_w_pw1", "a_b_pw1", "a_w_dw", "a_b_dw", "a_ln_g", "a_ln_b", "a_w_pw2", "b_w_in", "b_b_in", "b_ln_g",
           "b_ln_b", "b_w_s", "b_b_s", "b_w_out", "c_w_in", "c_w_conv", "c_w_out", "ln1_g", "ln1_b", "ln2_g",
           "ln2_b", "ffn_w_gate", "ffn_w_up", "ffn_w_down", "ple_w_gate", "ple_w_proj", "ple_norm_g"]


def _call(name, body, grid, in_specs, out_specs, out_shape, scratch=(), aliases=None, hosts=()):
    params = pltpu.CompilerParams(dimension_semantics=("arbitrary",) * len(grid), vmem_limit_bytes=VMEM_LIMIT)
    if not hosts:
        return pl.pallas_call(
            body, name=name, grid=grid, in_specs=in_specs, out_specs=out_specs, out_shape=out_shape,
            scratch_shapes=list(scratch), input_output_aliases=aliases or {}, compiler_params=params)
    assert len(grid) == 1 and not aliases
    single = not isinstance(out_shape, (list, tuple))
    own_shapes = [out_shape] if single else list(out_shape)
    own_specs = [out_specs] if single else list(out_specs)
    n_in, n_out, n_scr = len(in_specs), len(own_shapes), len(scratch)
    h_in = [len(h.arrays) for h in hosts]
    h_out = [len(h.out_shapes) for h in hosts]
    h_sem = [len(h.sems) for h in hosts]

    def split(refs, counts):
        out, off = [], 0
        for cnt in counts:
            out.append(refs[off:off + cnt])
            off += cnt
        return out

    def wrapped(*refs):
        ins, hin, outs, hout, scr, hsem = split(refs, [n_in, sum(h_in), n_out, sum(h_out), n_scr, sum(h_sem)])
        per_host = list(zip(hosts, split(hin, h_in), split(hout, h_out), split(hsem, h_sem)))

        @pl.when(pl.program_id(0) == 0)
        def _():
            for h, a, o, s in per_host:
                h.start(a, o, s)

        body(*ins, *outs, *scr)

        @pl.when(pl.program_id(0) == grid[0] - 1)
        def _():
            for h, a, o, s in per_host:
                h.finish(a, o, s)

    any_spec = pl.BlockSpec(memory_space=pl.ANY)
    call = pl.pallas_call(
        wrapped, name=name, grid=grid, in_specs=list(in_specs) + [any_spec] * sum(h_in),
        out_specs=own_specs + [any_spec] * sum(h_out),
        out_shape=own_shapes + [s for h in hosts for s in h.out_shapes],
        scratch_shapes=list(scratch) + [s for h in hosts for s in h.sems], compiler_params=params)

    def run(*args):
        res = call(*args, *[a for h in hosts for a in h.arrays])
        own = res[0] if single else list(res[:n_out])
        return own, split(list(res[n_out:]), h_out)

    return run


def _sds(shape, dtype=F32):
    return jax.ShapeDtypeStruct(shape, dtype)


def _row(tm, c):
    return pl.BlockSpec((tm, c), lambda i: (i, 0))


def _grow(g, tm, c):
    return pl.BlockSpec((g, tm, c), lambda i: (0, i, 0))


def _const(shape):
    nd = len(shape)
    return pl.BlockSpec(shape, lambda i: (0,) * nd, pipeline_mode=pl.Buffered(1))


def _wspec(w):
    return pl.BlockSpec((NQ, None, w.shape[2], w.shape[3]), lambda i: (0, 0, 0, 0), pipeline_mode=pl.Buffered(1))


def _prev(tm, hb, c):
    return pl.BlockSpec((hb, c), lambda i: (jnp.maximum(i * (tm // hb) - 1, 0), 0))


def _next(tm, hb, c):
    return pl.BlockSpec((hb, c), lambda i: (jnp.minimum((i + 1) * (tm // hb), S // hb - 1), 0))


def _acc(r, c):
    return pl.BlockSpec((r, c), lambda i: (0, 0))


def _sig(x):
    return 1.0 / (1.0 + jnp.exp(-x))


def _ln(z, g, b):
    mu = jnp.mean(z, axis=-1, keepdims=True)
    zc = z - mu
    rstd = lax.rsqrt(jnp.mean(zc * zc, axis=-1, keepdims=True) + LN_EPS)
    xhat = zc * rstd
    return xhat * g + b, xhat, rstd


def _ln_bwd(dyg, xhat, rstd):
    return rstd * (dyg - jnp.mean(dyg, axis=-1, keepdims=True) - xhat * jnp.mean(dyg * xhat, axis=-1, keepdims=True))


def _mm(a, w):
    return jnp.dot(a.astype(BF16), w, preferred_element_type=F32)


def _mmt(a, w):
    return lax.dot_general(a.astype(BF16), w, (((1,), (1,)), ((), ())), preferred_element_type=F32)


def _colsum(x):
    return jnp.sum(x, axis=0, keepdims=True)


def _gelu(x):
    t = jnp.tanh(GELU_C * (x + GELU_A * x * x * x))
    return 0.5 * x * (1.0 + t), t


def _gelu_grad(x, t):
    return 0.5 * (1.0 + t) + 0.5 * x * (1.0 - t * t) * GELU_C * (1.0 + 3.0 * GELU_A * x * x)


def _silu_grad(a, sg):
    return sg * (1.0 + a * (1.0 - sg))


def _sgu_masks():
    r = lax.broadcasted_iota(jnp.int32, (SGU_T, SGU_T), 0) // SGU_CHUNK
    c = lax.broadcasted_iota(jnp.int32, (SGU_T, SGU_T), 1) // SGU_CHUNK
    return r >= c, c >= r


def _fill_halo(buf, lo, n, halo_val_fn, is_edge):
    @pl.when(is_edge)
    def _():
        buf[lo:lo + n, :] = jnp.zeros((n, buf.shape[1]), F32)

    @pl.when(jnp.logical_not(is_edge))
    def _():
        buf[lo:lo + n, :] = halo_val_fn()


SUB, LANE = 8, 128
ROWS_AT_ONCE = 16


def _shift_copies(buf, sh):
    rows = sh.shape[1]
    for s in range(1, SUB):
        sh[s - 1, :, :] = buf[pl.ds(s, rows), :]


def _tiles(buf, sh, s, first, count, group0, lanes):
    src = buf if s == 0 else sh.at[s - 1]
    return {t: src[pl.ds(pl.multiple_of((group0 + t) * SUB, SUB), SUB), lanes] for t in range(first, first + count)}


def _by_shift(offsets):
    out = []
    for s in range(SUB):
        taps = [(k, o // SUB) for k, o in enumerate(offsets) if o % SUB == s]
        if taps:
            out.append((s, taps))
    return out


def _conv_rows(out_ref, w_ref, bias_ref, offsets, buf, sh, tm):
    n = ROWS_AT_ONCE
    for cb in range(D // LANE):
        lanes = slice(cb * LANE, (cb + 1) * LANE)
        bias = None if bias_ref is None else jnp.broadcast_to(bias_ref[:, lanes], (SUB, LANE))

        def body(jb, carry):
            accs = [bias] * n
            for s, taps in _by_shift(offsets):
                ms = [m for _, m in taps]
                tiles = _tiles(buf, sh, s, min(ms), max(ms) - min(ms) + n, jb * n, lanes)
                for k, m in taps:
                    wk = jnp.broadcast_to(w_ref[k:k + 1, lanes], (SUB, LANE))
                    for jj in range(n):
                        t = wk * tiles[m + jj]
                        accs[jj] = t if accs[jj] is None else accs[jj] + t
            for jj in range(n):
                out_ref[pl.ds(pl.multiple_of((jb * n + jj) * SUB, SUB), SUB), lanes] = accs[jj]
            return carry

        lax.fori_loop(0, tm // (SUB * n), body, 0)


def _conv_wgrad(dw_ref, d_ref, offsets, buf, sh, tm):
    n = 4
    for cb in range(D // LANE):
        lanes = slice(cb * LANE, (cb + 1) * LANE)

        def body(jq, accs):
            accs = list(accs)
            d = [d_ref[pl.ds(pl.multiple_of((jq * n + jj) * SUB, SUB), SUB), lanes] for jj in range(n)]
            for s, taps in _by_shift(offsets):
                ms = [m for _, m in taps]
                tiles = _tiles(buf, sh, s, min(ms), max(ms) - min(ms) + n, jq * n, lanes)
                for k, m in taps:
                    for jj in range(n):
                        accs[k] = accs[k] + d[jj] * tiles[m + jj]
            return tuple(accs)

        accs = lax.fori_loop(0, tm // (SUB * n), body, tuple(jnp.zeros((SUB, LANE), F32) for _ in offsets))
        for k, acc in enumerate(accs):
            dw_ref[k:k + 1, lanes] += jnp.sum(acc, axis=0, keepdims=True)


def _fwd_a1(x0, w1, b1, l, hosts=()):
    tm = 512

    def body(x_ref, w_ref, b_ref, h_ref, glu_ref):
        xb = x_ref[...].astype(BF16)
        for q in range(NQ):
            sl = slice(q * 512, (q + 1) * 512)
            h_ref[:, sl] = jnp.dot(xb, w_ref[q], preferred_element_type=F32) + b_ref[:, sl]
        glu_ref[...] = h_ref[:, :D] * _sig(h_ref[:, D:])

    return _call(f"fwd_a1_{l}", body, (S // tm,), [_row(tm, D), _wspec(w1), _const((1, 2 * D))],
                 [_row(tm, 2 * D), _row(tm, D)], [_sds((S, 2 * D)), _sds((S, D))], hosts=hosts)(x0, w1, b1)


def _fwd_a2(glu, x0, wdw, bdw, lg, lb, w2, l, hosts=()):
    tm = 256

    def body(g_ref, gp_ref, x_ref, wdw_ref, bdw_ref, lg_ref, lb_ref, w2_ref, z_ref, cv_ref, buf, sh):
        i = pl.program_id(0)
        _fill_halo(buf, 0, HALO_A, lambda: gp_ref[...], i == 0)
        buf[HALO_A:HALO_A + tm, :] = g_ref[...]
        _shift_copies(buf, sh)
        _conv_rows(cv_ref, wdw_ref, bdw_ref, [HALO_A - (CONV_A - 1) + k for k in range(CONV_A)], buf, sh, tm)
        n, _, _ = _ln(cv_ref[...], lg_ref[...], lb_ref[...])
        sb = (n * _sig(n)).astype(BF16)
        z_ref[...] = ALPHA * x_ref[...] + jnp.dot(sb, w2_ref[...], preferred_element_type=F32)

    return _call(f"fwd_a2_{l}", body, (S // tm,),
                 [_row(tm, D), _prev(tm, HALO_A, D), _row(tm, D), _const((32, D)), _const((1, D)), _const((1, D)),
                  _const((1, D)), _const(w2.shape)],
                 [_row(tm, D), _row(tm, D)], [_sds((S, D)), _sds((S, D))],
                 scratch=[pltpu.VMEM((HALO_A + tm, D), F32), pltpu.VMEM((SUB - 1, HALO_A + tm - SUB, D), F32)],
                 hosts=hosts)(glu, glu, x0, wdw, bdw, lg, lb, w2)


def _fwd_b(x0, win, b_in, lg, lb, ws, bsx, wout, hosts=()):
    tm = 256

    def body(x_ref, win_ref, bin_ref, lg_ref, lb_ref, ws_ref, bsx_ref, wout_ref, z_ref, zg_ref, gg_ref, f_scr, h_ref):
        xb = x_ref[...].astype(BF16)
        for q in range(NQ):
            sl = slice(q * 1024, (q + 1) * 1024)
            h_ref[:, sl] = jnp.dot(xb, win_ref[q], preferred_element_type=F32) + bin_ref[:, sl]
        u, tu = _gelu(h_ref[:, :E])
        v, tv = _gelu(h_ref[:, E:])
        zg_ref[:, 0:E] = u.astype(BF16)
        zg_ref[:, E:2 * E] = v.astype(BF16)
        gg_ref[:, 0:E] = _gelu_grad(h_ref[:, :E], tu).astype(BF16)
        gg_ref[:, E:2 * E] = _gelu_grad(h_ref[:, E:], tv).astype(BF16)
        vn, _, _ = _ln(v, lg_ref[...], lb_ref[...])
        vnb = vn.astype(BF16)
        mask, _ = _sgu_masks()
        for hd in range(SGU_H):
            wm = jnp.where(mask, ws_ref[hd], 0.0).astype(BF16)
            cs = slice(hd * SGU_G, (hd + 1) * SGU_G)
            for n in range(tm // SGU_T):
                rs = slice(n * SGU_T, (n + 1) * SGU_T)
                f_scr[rs, cs] = jnp.dot(wm, vnb[rs, cs], preferred_element_type=F32) + bsx_ref[hd]
        mb = (u * f_scr[...]).astype(BF16)
        z_ref[...] = ALPHA * x_ref[...] + jnp.dot(mb, wout_ref[...], preferred_element_type=F32)

    return _call("fwd_b", body, (S // tm,),
                 [_row(tm, D), _wspec(win), _const((1, 2 * E)), _const((1, E)), _const((1, E)),
                  _const((SGU_H, SGU_T, SGU_T)), _const((SGU_H, SGU_T, SGU_G)), _const(wout.shape)],
                 [_row(tm, D), _row(tm, 2 * E), _row(tm, 2 * E)],
                 [_sds((S, D)), _sds((S, 2 * E), BF16), _sds((S, 2 * E), BF16)],
                 scratch=[pltpu.VMEM((tm, E), F32), pltpu.VMEM((tm, 2 * E), F32)], hosts=hosts
                 )(x0, win, b_in, lg, lb, ws, bsx, wout)


def _fwd_c1(x0, win, hosts=()):
    tm = 512

    def body(x_ref, w_ref, hc_ref):
        xb = x_ref[...].astype(BF16)
        for q in range(NQ):
            hc_ref[:, q * 768:(q + 1) * 768] = jnp.dot(xb, w_ref[q], preferred_element_type=F32)

    return _call("fwd_c1", body, (S // tm,), [_row(tm, D), _wspec(win)], _row(tm, 3 * D),
                 _sds((S, 3 * D)), hosts=hosts)(x0, win)


def _short_conv(buf, hc_ref, hcp_ref, wc_ref, tm, i):
    _fill_halo(buf, 0, HALO_C, lambda: hcp_ref[:, D:2 * D] * hcp_ref[:, 2 * D:], i == 0)
    buf[HALO_C:HALO_C + tm, :] = hc_ref[:, D:2 * D] * hc_ref[:, 2 * D:]
    y = wc_ref[0:1, :] * buf[pl.ds(HALO_C - 2, tm), :]
    for k in range(1, CONV_C):
        y = y + wc_ref[k:k + 1, :] * buf[pl.ds(HALO_C - 2 + k, tm), :]
    return y


def _fwd_c2(hc, x0, wc, wout, hosts=()):
    tm = 256

    def body(hc_ref, hcp_ref, x_ref, wc_ref, wout_ref, z_ref, buf):
        y = _short_conv(buf, hc_ref, hcp_ref, wc_ref, tm, pl.program_id(0))
        mb = (hc_ref[:, :D] * y).astype(BF16)
        z_ref[...] = ALPHA * x_ref[...] + jnp.dot(mb, wout_ref[...], preferred_element_type=F32)

    return _call("fwd_c2", body, (S // tm,),
                 [_row(tm, 3 * D), _prev(tm, HALO_C, 3 * D), _row(tm, D), _const((8, D)), _const(wout.shape)],
                 _row(tm, D), _sds((S, D)), scratch=[pltpu.VMEM((HALO_C + tm, D), F32)], hosts=hosts
                 )(hc, hc, x0, wc, wout)


def _fwd_ffn(z1, lg, lb, wgt, wut, wd, l, hosts=()):
    tm = 256

    def body(z_ref, lg_ref, lb_ref, wg_ref, wu_ref, wd_ref, o_ref, a_ref, u_ref, hm_ref):
        x1, _, _ = _ln(z_ref[...], lg_ref[...], lb_ref[...])
        xb = x1.astype(BF16)
        a = _mmt(xb, wg_ref[...])
        u = _mmt(xb, wu_ref[...])
        hmb = (a * _sig(a) * u).astype(BF16)
        a_ref[...] = a.astype(BF16)
        u_ref[...] = u.astype(BF16)
        hm_ref[...] = hmb
        o_ref[...] = ALPHA * x1 + jnp.dot(hmb, wd_ref[...], preferred_element_type=F32)

    return _call(f"fwd_ffn_{l}", body, (S // tm,),
                 [_row(tm, D), _const((1, D)), _const((1, D)), _const((FF, D)), _const((FF, D)), _const((FF, D))],
                 [_row(tm, D), _row(tm, FF), _row(tm, FF), _row(tm, FF)],
                 [_sds((S, D)), _sds((S, FF), BF16), _sds((S, FF), BF16), _sds((S, FF), BF16)],
                 hosts=hosts)(z1, lg, lb, wgt, wut, wd)


def _ple_parts(z2, p, lg, lb, wg_ref, wp_ref, pg):
    x2, xhat, rstd = _ln(z2, lg, lb)
    xb = x2.astype(BF16)
    gate = _sig(jnp.dot(xb, wg_ref[...], preferred_element_type=F32))
    pb = p.astype(BF16)
    qp = jnp.concatenate([jnp.dot(pb, wp_ref[q], preferred_element_type=F32) for q in range(NQ)], axis=1)
    rs = lax.rsqrt(jnp.mean(qp * qp, axis=-1, keepdims=True) + LN_EPS)
    qn = qp * rs
    return x2, xhat, rstd, xb, gate, qn, rs, qn * pg


def _fwd_ple(z2, p, lg, lb, wg, wp, pg, l, hosts=()):
    tm = 512

    def body(z_ref, p_ref, lg_ref, lb_ref, wg_ref, wp_ref, pg_ref, o_ref):
        x2, _, _, _, gate, _, _, r = _ple_parts(z_ref[...], p_ref[...], lg_ref[...], lb_ref[...], wg_ref, wp_ref,
                                                pg_ref[...])
        o_ref[...] = x2 + gate * r

    return _call(f"fwd_ple_{l}", body, (S // tm,),
                 [_row(tm, D), _row(tm, 256), _const((1, D)), _const((1, D)), _const(wg.shape), _wspec(wp),
                  _const((1, D))],
                 _row(tm, D), _sds((S, D)), hosts=hosts)(z2, p, lg, lb, wg, wp, pg)


def _loss_head(y, target):
    tm = 512

    def body(y_ref, t_ref, dy_ref, acc_ref):
        @pl.when(pl.program_id(0) == 0)
        def _():
            acc_ref[...] = jnp.zeros_like(acc_ref)

        e = y_ref[...] - t_ref[...]
        dy_ref[...] = e * (1.0 / D)
        acc_ref[0:1, :] += _colsum(e * e)

    return _call("loss_head", body, (S // tm,), [_row(tm, D), _row(tm, D)], [_row(tm, D), _acc(8, D)],
                 [_sds((S, D)), _sds((8, D))])(y, target)


def _zero_first(*refs):
    @pl.when(pl.program_id(0) == 0)
    def _():
        for r in refs:
            r[...] = jnp.zeros_like(r)


def _bwd_ple(g, z2, p, lg, lb, wg, wp, pg, l, hosts=()):
    tm = 256

    def body(g_ref, z_ref, p_ref, lg_ref, lb_ref, wg_ref, wp_ref, pg_ref, dz_ref, xb_ref, dgp_ref, dqp_ref, acc_ref):
        _zero_first(acc_ref)
        gin = g_ref[...]
        lgv, pgv = lg_ref[...], pg_ref[...]
        _, xhat, rstd, xb, gate, qn, rs, r = _ple_parts(z_ref[...], p_ref[...], lgv, lb_ref[...], wg_ref, wp_ref, pgv)
        xb_ref[...] = xb
        dgpb = (gin * r * gate * (1.0 - gate)).astype(BF16)
        dgp_ref[...] = dgpb
        dx2 = gin + _mmt(dgpb, wg_ref[...])
        dr = gin * gate
        acc_ref[0:1, :] += _colsum(dr * qn)
        t = dr * pgv
        dqp_ref[...] = (rs * (t - qn * jnp.mean(t * qn, axis=-1, keepdims=True))).astype(BF16)
        acc_ref[1:2, :] += _colsum(dx2 * xhat)
        acc_ref[2:3, :] += _colsum(dx2)
        dz_ref[...] = _ln_bwd(dx2 * lgv, xhat, rstd)

    return _call(f"bwd_ple_{l}", body, (S // tm,),
                 [_row(tm, D), _row(tm, D), _row(tm, 256), _const((1, D)), _const((1, D)), _const(wg.shape),
                  _wspec(wp), _const((1, D))],
                 [_row(tm, D), _row(tm, D), _row(tm, D), _row(tm, D), _acc(8, D)],
                 [_sds((S, D)), _sds((S, D), BF16), _sds((S, D), BF16), _sds((S, D), BF16), _sds((8, D))],
                 hosts=hosts)(g, z2, p, lg, lb, wg, wp, pg)


def _bwd_ffn(dz2, z1, ab, ub, lg, lb, wgt, wut, wd, l, hosts=()):
    tm = 256

    def body(dz2_ref, z_ref, a_ref, u_ref, lg_ref, lb_ref, wg_ref, wu_ref, wd_ref, dz1_ref, xb_ref, da_ref, du_ref,
             acc_ref):
        _zero_first(acc_ref)
        dz2v = dz2_ref[...]
        lgv = lg_ref[...]
        x1, xhat, rstd = _ln(z_ref[...], lgv, lb_ref[...])
        xb_ref[...] = x1.astype(BF16)
        a = a_ref[...].astype(F32)
        u = u_ref[...].astype(F32)
        sg = _sig(a)
        dhm = _mmt(dz2v, wd_ref[...])
        dub = (dhm * (a * sg)).astype(BF16)
        dab = (dhm * u * _silu_grad(a, sg)).astype(BF16)
        da_ref[...] = dab
        du_ref[...] = dub
        dx1 = ALPHA * dz2v + _mm(dab, wg_ref[...]) + _mm(dub, wu_ref[...])
        acc_ref[0:1, :] += _colsum(dx1 * xhat)
        acc_ref[1:2, :] += _colsum(dx1)
        dz1_ref[...] = _ln_bwd(dx1 * lgv, xhat, rstd)

    return _call(f"bwd_ffn_{l}", body, (S // tm,),
                 [_row(tm, D), _row(tm, D), _row(tm, FF), _row(tm, FF), _const((1, D)), _const((1, D)),
                  _const((FF, D)), _const((FF, D)), _const((FF, D))],
                 [_row(tm, D), _row(tm, D), _row(tm, FF), _row(tm, FF), _acc(8, D)],
                 [_sds((S, D)), _sds((S, D), BF16), _sds((S, FF), BF16), _sds((S, FF), BF16), _sds((8, D))],
                 hosts=hosts)(dz2, z1, ab, ub, lg, lb, wgt, wut, wd)


def _bwd_a2(dz1, cv, lg, lb, w2, l, hosts=()):
    tm = 512

    def body(dz_ref, cv_ref, lg_ref, lb_ref, w2_ref, dcv_ref, sb_ref, acc_ref):
        _zero_first(acc_ref)
        lgv = lg_ref[...]
        n, xhat, rstd = _ln(cv_ref[...], lgv, lb_ref[...])
        sg = _sig(n)
        sb_ref[...] = (n * sg).astype(BF16)
        dzb = dz_ref[...].astype(BF16)
        ds = _mmt(dzb, w2_ref[...])
        dn = ds * _silu_grad(n, sg)
        acc_ref[0:1, :] += _colsum(dn * xhat)
        acc_ref[1:2, :] += _colsum(dn)
        dcv = _ln_bwd(dn * lgv, xhat, rstd)
        acc_ref[2:3, :] += _colsum(dcv)
        dcv_ref[...] = dcv

    return _call(f"bwd_a2_{l}", body, (S // tm,),
                 [_row(tm, D), _row(tm, D), _const((1, D)), _const((1, D)), _const(w2.shape)],
                 [_row(tm, D), _row(tm, D), _acc(8, D)],
                 [_sds((S, D)), _sds((S, D), BF16), _sds((8, D))], hosts=hosts)(dz1, cv, lg, lb, w2)


def _bwd_conv_a(dcv, glu, wdw, l, hosts=()):
    tm = 256
    nb = S // tm

    def body(d_ref, dn_ref, g_ref, gp_ref, w_ref, dglu_ref, dw_ref, bufd, bufx, sh):
        i = pl.program_id(0)
        _zero_first(dw_ref)
        bufd[0:tm, :] = d_ref[...]
        _fill_halo(bufd, tm, HALO_A, lambda: dn_ref[...], i == nb - 1)
        _fill_halo(bufx, 0, HALO_A, lambda: gp_ref[...], i == 0)
        bufx[HALO_A:HALO_A + tm, :] = g_ref[...]
        _shift_copies(bufd, sh)
        _conv_rows(dglu_ref, w_ref, None, [CONV_A - 1 - k for k in range(CONV_A)], bufd, sh, tm)
        _shift_copies(bufx, sh)
        _conv_wgrad(dw_ref, d_ref, [HALO_A - (CONV_A - 1) + k for k in range(CONV_A)], bufx, sh, tm)

    return _call(f"bwd_conv_a_{l}", body, (nb,),
                 [_row(tm, D), _next(tm, HALO_A, D), _row(tm, D), _prev(tm, HALO_A, D), _const((32, D))],
                 [_row(tm, D), _acc(32, D)], [_sds((S, D)), _sds((32, D))],
                 scratch=[pltpu.VMEM((tm + HALO_A, D), F32), pltpu.VMEM((HALO_A + tm, D), F32),
                          pltpu.VMEM((SUB - 1, HALO_A + tm - SUB, D), F32)], hosts=hosts)(dcv, dcv, glu, glu, wdw)


def _bwd_a1(dglu, h, dz1, w1, l, hosts=()):
    tm = 256

    def body(dg_ref, h_ref, dz_ref, w_ref, dx_ref, dh_ref, acc_ref):
        _zero_first(acc_ref)
        a, g = h_ref[:, :D], h_ref[:, D:]
        sg = _sig(g)
        dgl = dg_ref[...]
        da = dgl * sg
        dg = dgl * a * sg * (1.0 - sg)
        acc_ref[0:1, 0:D] += _colsum(da)
        acc_ref[0:1, D:2 * D] += _colsum(dg)
        dh_ref[:, 0:D] = da.astype(BF16)
        dh_ref[:, D:2 * D] = dg.astype(BF16)
        dx = ALPHA * dz_ref[...]
        for q in range(NQ):
            dx = dx + _mmt(dh_ref[:, q * 512:(q + 1) * 512], w_ref[q])
        dx_ref[...] = dx

    return _call(f"bwd_a1_{l}", body, (S // tm,),
                 [_row(tm, D), _row(tm, 2 * D), _row(tm, D), _wspec(w1)],
                 [_row(tm, D), _row(tm, 2 * D), _acc(8, 2 * D)],
                 [_sds((S, D)), _sds((S, 2 * D), BF16), _sds((8, 2 * D))], hosts=hosts)(dglu, h, dz1, w1)


def _bwd_c2(dz1, hc, wc, wout):
    tm = 256

    def body(dz_ref, hc_ref, hcp_ref, wc_ref, wout_ref, dy_ref, dbg_ref, mb_ref, buf):
        y = _short_conv(buf, hc_ref, hcp_ref, wc_ref, tm, pl.program_id(0))
        dzb = dz_ref[...].astype(BF16)
        dm = _mmt(dzb, wout_ref[...])
        bg = hc_ref[:, :D]
        mb_ref[...] = (bg * y).astype(BF16)
        dbg_ref[...] = (dm * y).astype(BF16)
        dy_ref[...] = dm * bg

    return _call("bwd_c2", body, (S // tm,),
                 [_row(tm, D), _row(tm, 3 * D), _prev(tm, HALO_C, 3 * D), _const((8, D)), _const(wout.shape)],
                 [_row(tm, D), _row(tm, D), _row(tm, D)],
                 [_sds((S, D)), _sds((S, D), BF16), _sds((S, D), BF16)],
                 scratch=[pltpu.VMEM((HALO_C + tm, D), F32)])(dz1, hc, hc, wc, wout)


def _bwd_c1(dy, hc, dbg, dz1, wc, win):
    tm = 256
    nb = S // tm

    def body(d_ref, dn_ref, hc_ref, hcp_ref, dbg_ref, dz_ref, wc_ref, win_ref, dx_ref, dhc_ref, dwc_ref, bufd, bufq):
        i = pl.program_id(0)
        _zero_first(dwc_ref)
        bufd[0:tm, :] = d_ref[...]
        _fill_halo(bufd, tm, HALO_C, lambda: dn_ref[...], i == nb - 1)
        _fill_halo(bufq, 0, HALO_C, lambda: hcp_ref[:, D:2 * D] * hcp_ref[:, 2 * D:], i == 0)
        bufq[HALO_C:HALO_C + tm, :] = hc_ref[:, D:2 * D] * hc_ref[:, 2 * D:]
        dq = wc_ref[0:1, :] * bufd[pl.ds(CONV_C - 1, tm), :]
        for k in range(1, CONV_C):
            dq = dq + wc_ref[k:k + 1, :] * bufd[pl.ds(CONV_C - 1 - k, tm), :]
        dv = d_ref[...]
        for k in range(CONV_C):
            dwc_ref[k:k + 1, :] += _colsum(dv * bufq[pl.ds(HALO_C - (CONV_C - 1) + k, tm), :])
        dhc_ref[:, 0:D] = dbg_ref[...]
        dhc_ref[:, D:2 * D] = (dq * hc_ref[:, 2 * D:]).astype(BF16)
        dhc_ref[:, 2 * D:3 * D] = (dq * hc_ref[:, D:2 * D]).astype(BF16)
        dx = ALPHA * dz_ref[...]
        for q in range(NQ):
            dx = dx + _mmt(dhc_ref[:, q * 768:(q + 1) * 768], win_ref[q])
        dx_ref[...] = dx

    return _call("bwd_c1", body, (nb,),
                 [_row(tm, D), _next(tm, HALO_C, D), _row(tm, 3 * D), _prev(tm, HALO_C, 3 * D), _row(tm, D),
                  _row(tm, D), _const((8, D)), _wspec(win)],
                 [_row(tm, D), _row(tm, 3 * D), _acc(8, D)],
                 [_sds((S, D)), _sds((S, 3 * D), BF16), _sds((8, D))],
                 scratch=[pltpu.VMEM((tm + HALO_C, D), F32), pltpu.VMEM((HALO_C + tm, D), F32)]
                 )(dy, dy, hc, hc, dbg, dz1, wc, win)


def _bwd_b(dz1, zg, gg, lg, lb, win, wout, ws, wst, bsx):
    tm = 128
    nb = S // tm

    def body(dz_ref, zg_ref, gg_ref, lg_ref, lb_ref, win_ref, wout_ref, ws_ref, wst_ref, bsx_ref,
             dx_ref, dh_ref, mb_ref, acc_ref, dws_ref, dbs_ref, f_scr, dvn_scr):
        _zero_first(acc_ref, dws_ref, dbs_ref)
        lgv = lg_ref[...]
        u = zg_ref[:, :E].astype(F32)
        v = zg_ref[:, E:].astype(F32)
        vn, xhat, rstd = _ln(v, lgv, lb_ref[...])
        vnb = vn.astype(BF16)
        dzb = dz_ref[...].astype(BF16)
        dm = _mmt(dzb, wout_ref[...])
        mask, mask_t = _sgu_masks()
        for hd in range(SGU_H):
            wm = jnp.where(mask, ws_ref[hd], 0.0).astype(BF16)
            cs = slice(hd * SGU_G, (hd + 1) * SGU_G)
            for n in range(tm // SGU_T):
                rs = slice(n * SGU_T, (n + 1) * SGU_T)
                f_scr[rs, cs] = jnp.dot(wm, vnb[rs, cs], preferred_element_type=F32) + bsx_ref[hd]
        f = f_scr[...]
        mb_ref[...] = (u * f).astype(BF16)
        du = dm * f
        df = dm * u
        dfb = df.astype(BF16)
        for hd in range(SGU_H):
            wmt = jnp.where(mask_t, wst_ref[hd], 0.0).astype(BF16)
            cs = slice(hd * SGU_G, (hd + 1) * SGU_G)
            for n in range(tm // SGU_T):
                rs = slice(n * SGU_T, (n + 1) * SGU_T)
                dvn_scr[rs, cs] = jnp.dot(wmt, dfb[rs, cs], preferred_element_type=F32)
                dws_ref[hd] += lax.dot_general(dfb[rs, cs], vnb[rs, cs], (((1,), (1,)), ((), ())),
                                               preferred_element_type=F32)
                dbs_ref[hd] += df[rs, cs]
        dvn = dvn_scr[...]
        acc_ref[1:2, 0:E] += _colsum(dvn * xhat)
        acc_ref[2:3, 0:E] += _colsum(dvn)
        dv = _ln_bwd(dvn * lgv, xhat, rstd)
        dhu = du * gg_ref[:, :E].astype(F32)
        dhv = dv * gg_ref[:, E:].astype(F32)
        acc_ref[0:1, 0:E] += _colsum(dhu)
        acc_ref[0:1, E:2 * E] += _colsum(dhv)
        dh_ref[:, 0:E] = dhu.astype(BF16)
        dh_ref[:, E:2 * E] = dhv.astype(BF16)
        dx = ALPHA * dz_ref[...]
        for q in range(NQ):
            dx = dx + _mmt(dh_ref[:, q * 1024:(q + 1) * 1024], win_ref[q])
        dx_ref[...] = dx

        @pl.when(pl.program_id(0) == nb - 1)
        def _():
            for hd in range(SGU_H):
                dws_ref[hd] = jnp.where(mask, dws_ref[hd], 0.0)

    c3 = lambda a, b, c: pl.BlockSpec((a, b, c), lambda i: (0, 0, 0))
    return _call("bwd_b", body, (nb,),
                 [_row(tm, D), _row(tm, 2 * E), _row(tm, 2 * E), _const((1, E)), _const((1, E)), _wspec(win),
                  _const(wout.shape), _const((SGU_H, SGU_T, SGU_T)), _const((SGU_H, SGU_T, SGU_T)),
                  _const((SGU_H, SGU_T, SGU_G))],
                 [_row(tm, D), _row(tm, 2 * E), _row(tm, E), _acc(8, 2 * E), c3(SGU_H, SGU_T, SGU_T),
                  c3(SGU_H, SGU_T, SGU_G)],
                 [_sds((S, D)), _sds((S, 2 * E), BF16), _sds((S, E), BF16), _sds((8, 2 * E)),
                  _sds((SGU_H, SGU_T, SGU_T)), _sds((SGU_H, SGU_T, SGU_G))],
                 scratch=[pltpu.VMEM((tm, E), F32), pltpu.VMEM((tm, E), F32)]
                 )(dz1, zg, gg, lg, lb, win, wout, ws, wst, bsx)


def _mm_tn(name, a, amode, b, bmode, k, n, groups=NQ, hosts=()):
    def block_bytes(ts):
        ka = k if amode == "1" else groups * k
        nb = n if bmode == "1" else groups * n
        return 2 * (ts * ka * a.dtype.itemsize + ts * nb * b.dtype.itemsize + groups * k * n * 4)

    ts = min(1024 if block_bytes(1024) <= DW_BLOCK_BUDGET else 512, S)

    def spec(mode, w):
        if mode == "1":
            return pl.BlockSpec((ts, w), lambda s: (s, 0))
        if mode == "c":
            return pl.BlockSpec((ts, groups * w), lambda s: (s, 0))
        return pl.BlockSpec((groups, ts, w), lambda s: (0, s, 0))

    def pick(ref, mode, w, g):
        if mode == "1":
            return ref[...]
        if mode == "c":
            return ref[:, g * w:(g + 1) * w]
        return ref[g]

    def body(a_ref, b_ref, o_ref):
        _zero_first(o_ref)
        a_t = jnp.transpose(a_ref[...].astype(BF16)) if amode == "1" else None
        b_1 = b_ref[...].astype(BF16) if bmode == "1" else None
        for g in range(groups):
            lhs = a_t if amode == "1" else jnp.transpose(pick(a_ref, amode, k, g).astype(BF16))
            rhs = b_1 if bmode == "1" else pick(b_ref, bmode, n, g).astype(BF16)
            o_ref[0, g] += jnp.dot(lhs, rhs, preferred_element_type=F32)

    return _call(name, body, (S // ts,), [spec(amode, k), spec(bmode, n)],
                 pl.BlockSpec((1, groups, k, n), lambda s: (0, 0, 0, 0)), _sds((1, groups, k, n)), hosts=hosts)(a, b)


def _row_block(k, cap=256):
    return max(t for t in range(16, min(k, cap) + 1, 16) if k % t == 0)


def _cast_bf16(w, hosts=()):
    nl, k, n = w.shape
    tb = _row_block(k, 512)
    nb = k // tb

    def body(w_ref, o_ref):
        o_ref[...] = w_ref[...].astype(BF16)

    spec = pl.BlockSpec((None, tb, n), lambda i: (i // nb, i % nb, 0))
    return _call("cast_bf16", body, (nl * nb,), [spec], spec, _sds(w.shape, BF16), hosts=hosts)(w)


def _adam(name, w, m, v, gc, l, prev):
    nl, k, n = w.shape
    nc = gc.shape[0]
    tb = _row_block(k, 512)

    def body(w_ref, m_ref, v_ref, g_ref, *rest):
        go_ref, d_ref, mo_ref, vo_ref = rest[-4:]
        g = g_ref[0].astype(F32)
        for c in range(1, nc):
            g = g + g_ref[c].astype(F32)
        m2 = ADAM_B1 * m_ref[...] + (1.0 - ADAM_B1) * g
        v2 = ADAM_B2 * v_ref[...] + (1.0 - ADAM_B2) * (g * g)
        m_hat = m2 / (1.0 - ADAM_B1 ** ADAM_STEP)
        v_hat = v2 / (1.0 - ADAM_B2 ** ADAM_STEP)
        go_ref[...] = g
        d_ref[...] = -ADAM_LR * (m_hat / (jnp.sqrt(v_hat) + ADAM_EPS) + ADAM_WD * w_ref[...])
        mo_ref[...] = m2
        vo_ref[...] = v2

    spec = pl.BlockSpec((None, tb, n), lambda i: (l, i, 0))
    gspec = pl.BlockSpec((nc, None, tb, n), lambda i: (0, 0, i, 0))
    in_specs, args, aliases = [spec, spec, spec, gspec], [w, m, v, gc], {}
    if prev is not None:
        in_specs += [pl.BlockSpec(memory_space=pl.ANY)] * 4
        args += list(prev)
        aliases = {4 + j: j for j in range(4)}
    return _call(name, body, (k // tb,), in_specs, [spec] * 4, [_sds(w.shape)] * 4, aliases=aliases)(*args)


def _sum8(name, g8):
    r = g8.shape[1]

    def body(g_ref, o_ref):
        acc = g_ref[0]
        for d in range(1, 8):
            acc = acc + g_ref[d]
        o_ref[...] = acc

    return _call(name, body, (1,), [pl.BlockSpec((8, r, 128), lambda i: (0, 0, 0))],
                 pl.BlockSpec((r, 128), lambda i: (0, 0)), _sds((r, 128)))(g8)


def _place():
    x, y, c = lax.axis_index("x"), lax.axis_index("y"), lax.axis_index("c")
    return x, y, c, 2 * x + y, (x, y, 1 - c), [(1 - x, y), (x, 1 - y), (1 - x, 1 - y)]


class _Exchange:
    def __init__(self, arrays, out_shapes):
        self.arrays, self.out_shapes = list(arrays), list(out_shapes)
        n = len(self.arrays)
        self.sems = [pltpu.SemaphoreType.DMA((7 * n,)), pltpu.SemaphoreType.DMA((7 * n,)),
                     pltpu.SemaphoreType.DMA((n,))]

    def _copies(self, ins, outs, sems):
        send, recv, lsem = sems
        local_src, remote_src, dst = self.maps(ins, outs)
        x, y, c, q, sib, chips = _place()

        def rcopy(w, k, qq, cc, to, src=None):
            return pltpu.make_async_remote_copy(
                src_ref=dst(w, qq, cc) if src is None else src, dst_ref=dst(w, qq, cc),
                send_sem=send.at[7 * w + k], recv_sem=recv.at[7 * w + k], device_id=to, device_id_type=MESH)

        def mine(w):
            return pltpu.make_async_copy(local_src(w), dst(w, q, c), lsem.at[w])

        def first(w):
            return [rcopy(w, 0, q, c, sib, local_src(w))] + [
                rcopy(w, 1 + j, q, c, (cx, cy, c), remote_src(w, 2 * cx + cy)) for j, (cx, cy) in enumerate(chips)]

        return rcopy, mine, first, (x, y, c), q, c, sib, chips

    def start(self, ins, outs, sems):
        _, mine, first, *_ = self._copies(ins, outs, sems)
        for w in range(len(self.arrays)):
            mine(w).start()
            for cp in first(w):
                cp.start()

    def finish(self, ins, outs, sems):
        rcopy, mine, first, me, q, c, sib, chips = self._copies(ins, outs, sems)
        n = len(self.arrays)
        for w in range(n):
            for j, (cx, cy) in enumerate(chips):
                rcopy(w, 1 + j, 2 * cx + cy, c, me).wait_recv()
                rcopy(w, 4 + j, 2 * cx + cy, c, sib).start()
        for w in range(n):
            rcopy(w, 0, q, 1 - c, me).wait_recv()
            for j, (cx, cy) in enumerate(chips):
                rcopy(w, 4 + j, 2 * cx + cy, 1 - c, me).wait_recv()
        for w in range(n):
            for cp in first(w):
                cp.wait_send()
            for j, (cx, cy) in enumerate(chips):
                rcopy(w, 4 + j, 2 * cx + cy, c, sib).wait_send()
            mine(w).wait()


class _GatherWeights(_Exchange):
    def __init__(self, items):
        self.layers = [l for _, l in items]
        self.kh = [s.shape[1] // 2 for s, _ in items]
        super().__init__([s for s, _ in items], [_sds((NQ, 1) + s.shape[1:], BF16) for s, _ in items])

    def maps(self, ins, outs):
        c = lax.axis_index("c")
        src = lambda w: ins[w].at[pl.ds(self.layers[w], 1), pl.ds(c * self.kh[w], self.kh[w]), :]
        return src, lambda w, q: src(w), lambda w, q, cc: outs[w].at[q, :, pl.ds(cc * self.kh[w], self.kh[w]), :]


class _ScatterPartials(_Exchange):
    def __init__(self, parts):
        super().__init__(parts, [_sds((NQ, 1, 2) + p.shape[2:], BF16) for p in parts])

    def maps(self, ins, outs):
        q = 2 * lax.axis_index("x") + lax.axis_index("y")
        return (lambda w: ins[w].at[:, q]), (lambda w, qq: ins[w].at[:, qq]), (lambda w, qq, cc: outs[w].at[qq, :, cc])


class _Gather8(_Exchange):
    def __init__(self, v):
        super().__init__([v], [_sds((8,) + v.shape)])

    def maps(self, ins, outs):
        return (lambda w: ins[0]), (lambda w, q: ins[0]), (lambda w, q, cc: outs[0].at[2 * q + cc])


class _SwapHalves:
    def __init__(self, dws):
        self.arrays = list(dws)
        self.kh = [d.shape[2] // 2 for d in dws]
        self.out_shapes = [_sds(d.shape[:2] + (kh,) + d.shape[3:]) for d, kh in zip(dws, self.kh)]
        self.sems = [pltpu.SemaphoreType.DMA((len(dws),)), pltpu.SemaphoreType.DMA((len(dws),))]

    def _copies(self, ins, outs, sems):
        send, recv = sems
        _, _, c, _, sib, _ = _place()
        return [pltpu.make_async_remote_copy(
            src_ref=ins[w].at[:, :, pl.ds((1 - c) * self.kh[w], self.kh[w]), :], dst_ref=outs[w],
            send_sem=send.at[w], recv_sem=recv.at[w], device_id=sib, device_id_type=MESH)
            for w in range(len(self.arrays))]

    def start(self, ins, outs, sems):
        for cp in self._copies(ins, outs, sems):
            cp.start()

    def finish(self, ins, outs, sems):
        for cp in self._copies(ins, outs, sems):
            cp.wait()


def _comm_only(name, host):
    n_in, n_out = len(host.arrays), len(host.out_shapes)

    def body(*refs):
        ins, outs, sems = refs[:n_in], refs[n_in:n_in + n_out], refs[n_in + n_out:]
        host.start(ins, outs, sems)
        host.finish(ins, outs, sems)

    any_spec = pl.BlockSpec(memory_space=pl.ANY)
    return pl.pallas_call(body, name=name, in_specs=[any_spec] * n_in, out_specs=[any_spec] * n_out,
                          out_shape=host.out_shapes, scratch_shapes=host.sems)(*host.arrays)


def _add_halves(dw, got, cidx):
    nl, _, k, n = dw.shape
    kh = k // 2
    qb = 2

    def body(c_ref, a_ref, b_ref, o_ref):
        o_ref[...] = (a_ref[...] + b_ref[...]).astype(BF16)

    grid_spec = pltpu.PrefetchScalarGridSpec(
        num_scalar_prefetch=1, grid=(nl, NQ // qb),
        in_specs=[pl.BlockSpec((None, qb, None, kh, n), lambda l, q, c_ref: (l, q, c_ref[0], 0, 0)),
                  pl.BlockSpec((None, qb, kh, n), lambda l, q, c_ref: (l, q, 0, 0))],
        out_specs=pl.BlockSpec((None, qb, kh, n), lambda l, q, c_ref: (l, q, 0, 0)))
    return pl.pallas_call(
        body, name="add_halves", grid_spec=grid_spec, out_shape=_sds((nl, NQ, kh, n), BF16),
        compiler_params=pltpu.CompilerParams(dimension_semantics=("arbitrary", "arbitrary"),
                                             vmem_limit_bytes=VMEM_LIMIT))(cidx, dw.reshape(nl, NQ, 2, kh, n), got)


def _gather8(name, v):
    return _comm_only(name, _Gather8(v))[0]


PACK = 16 * 128


def _pack(arrays):
    parts = []
    for a in arrays:
        flat = a.reshape(-1)
        parts.append(jnp.pad(flat, (0, (-flat.shape[0]) % PACK)))
    return jnp.concatenate(parts).reshape(-1, 128)


def _unpack(packed, shapes):
    flat = packed.reshape(-1)
    out, off = [], 0
    for shp in shapes:
        size = 1
        for d in shp:
            size *= d
        out.append(flat[off:off + size].reshape(shp))
        off += size + (-size) % PACK
    return out


def kernel(x, p, a_w_pw1, a_b_pw1, a_w_dw, a_b_dw, a_ln_g, a_ln_b, a_w_pw2, b_w_in, b_b_in, b_ln_g, b_ln_b, b_w_s, b_b_s, b_w_out, c_w_in, c_w_conv, c_w_out, ln1_g, ln1_b, ln2_g, ln2_b, ffn_w_gate, ffn_w_up, ffn_w_down, ple_w_gate, ple_w_proj, ple_norm_g, loss_target, m_a_w_pw1, m_a_b_pw1, m_a_w_dw, m_a_b_dw, m_a_ln_g, m_a_ln_b, m_a_w_pw2, m_b_w_in, m_b_b_in, m_b_ln_g, m_b_ln_b, m_b_w_s, m_b_b_s, m_b_w_out, m_c_w_in, m_c_w_conv, m_c_w_out, m_ln1_g, m_ln1_b, m_ln2_g, m_ln2_b, m_ffn_w_gate, m_ffn_w_up, m_ffn_w_down, m_ple_w_gate, m_ple_w_proj, m_ple_norm_g, v_a_w_pw1, v_a_b_pw1, v_a_w_dw, v_a_b_dw, v_a_ln_g, v_a_ln_b, v_a_w_pw2, v_b_w_in, v_b_b_in, v_b_ln_g, v_b_ln_b, v_b_w_s, v_b_b_s, v_b_w_out, v_c_w_in, v_c_w_conv, v_c_w_out, v_ln1_g, v_ln1_b, v_ln2_g, v_ln2_b, v_ffn_w_gate, v_ffn_w_up, v_ffn_w_down, v_ple_w_gate, v_ple_w_proj, v_ple_norm_g):
    args = dict(locals())
    wts = {k: args[k] for k in WEIGHTS}
    mom = {k: args["m_" + k] for k in WEIGHTS}
    var = {k: args["v_" + k] for k in WEIGHTS}
    for k in TRANSPOSED:
        wts[k], mom[k], var[k] = (jnp.transpose(t[k], (0, 2, 1)) for t in (wts, mom, var))
    q_idx = 2 * lax.axis_index("x") + lax.axis_index("y")
    c_idx = lax.axis_index("c").astype(jnp.int32).reshape(1)

    wb = {k: _cast_bf16(wts[k]) for k in BIG if k not in ("ffn_w_gate", "ffn_w_up")}
    mixw = [[("a_w_pw1", 0), ("a_w_pw2", 0)], [("b_w_in", 0), ("b_w_out", 0)], [("c_w_in", 0), ("c_w_out", 0)],
            [("a_w_pw1", 1), ("a_w_pw2", 1)]]
    ffnw = [[("ffn_w_gate", l), ("ffn_w_up", l), ("ffn_w_down", l)] for l in range(DEPTH)]
    plew = [[("ple_w_gate", l), ("ple_w_proj", l)] for l in range(DEPTH)]
    fwd_plan = {("a1", 0): mixw[0][1:] + plew[0], ("a2", 0): ffnw[0], ("ffn", 0): mixw[1], ("ple", 0): plew[1],
                ("b", 1): ffnw[1], ("ffn", 1): mixw[2] + ffnw[2][:1], ("ple", 1): plew[2],
                ("c1", 2): ffnw[2][1:2], ("c2", 2): ffnw[2][2:], ("ffn", 2): mixw[3] + ffnw[3][:1], ("ple", 2): plew[3],
                ("a1", 3): ffnw[3][1:2], ("a2", 3): ffnw[3][2:]}
    gw = {}

    def gather(keys):
        return _GatherWeights([(wb[name], l) for name, l in keys])

    def hosted(tag, fn, *fargs):
        keys = fwd_plan.get(tag)
        if not keys:
            return fn(*fargs)
        own, (got,) = fn(*fargs, hosts=[gather(keys)])
        store(keys, got)
        return own

    def store(keys, got):
        for (name, l), arr in zip(keys, got):
            gw[name, l] = arr.reshape(NQ * arr.shape[2], arr.shape[3]) if name in ROW_SHARDED else arr

    first_keys = mixw[0][:1]
    wb["ffn_w_gate"], (got,) = _cast_bf16(wts["ffn_w_gate"], hosts=[gather(first_keys)])
    store(first_keys, got)
    shard_shapes = [wts[k].shape for k in SMALL_SHARDED]
    wb["ffn_w_up"], ((small8,),) = _cast_bf16(wts["ffn_w_up"], hosts=[_Gather8(_pack([wts[k] for k in SMALL_SHARDED]))])
    per_chip = [_unpack(small8[2 * qq], shard_shapes) for qq in range(NQ)]
    full = {k: jnp.concatenate([per_chip[qq][i] for qq in range(NQ)], axis=-1) for i, k in enumerate(SMALL_SHARDED)}
    for k in SMALL_REPL:
        full[k] = wts[k]

    def vec(name, l):
        return full[name][l][None, :]

    def conv_w(name, l, rows):
        w = full[name][l]
        return jnp.pad(w, ((0, rows - w.shape[0]), (0, 0)))

    ws = full["b_w_s"][0]
    wst = jnp.transpose(ws, (0, 2, 1))
    bsx = jnp.broadcast_to(full["b_b_s"][0][:, :, None], (SGU_H, SGU_T, SGU_G))

    x0s, z1s, z2s, saved, ffn_saved = [], [], [], [], []
    cur = x[0]
    for i in range(DEPTH):
        mix, j = i % 3, i // 3
        x0s.append(cur)
        if mix == 0:
            h, glu = hosted(("a1", i), _fwd_a1, cur, gw["a_w_pw1", j], vec("a_b_pw1", j), i)
            z1, cv = hosted(("a2", i), _fwd_a2, glu, cur, conv_w("a_w_dw", j, 32), vec("a_b_dw", j), vec("a_ln_g", j),
                            vec("a_ln_b", j), gw["a_w_pw2", j], i)
            saved.append((h, glu, cv))
        elif mix == 1:
            z1, zg, gg = hosted(("b", i), _fwd_b, cur, gw["b_w_in", 0], vec("b_b_in", 0), vec("b_ln_g", 0),
                                vec("b_ln_b", 0), ws, bsx, gw["b_w_out", 0])
            saved.append((zg, gg))
        else:
            hc = hosted(("c1", i), _fwd_c1, cur, gw["c_w_in", 0])
            z1 = hosted(("c2", i), _fwd_c2, hc, cur, conv_w("c_w_conv", 0, 8), gw["c_w_out", 0])
            saved.append((hc,))
        z2, ab, ub, hm = hosted(("ffn", i), _fwd_ffn, z1, vec("ln1_g", i), vec("ln1_b", i), gw["ffn_w_gate", i],
                                gw["ffn_w_up", i], gw["ffn_w_down", i], i)
        ffn_saved.append((ab, ub, hm))
        cur = hosted(("ple", i), _fwd_ple, z2, p[i, 0], vec("ln2_g", i), vec("ln2_b", i), gw["ple_w_gate", i],
                     gw["ple_w_proj", i], vec("ple_norm_g", i), i)
        z1s.append(z1)
        z2s.append(z2)

    g, loss_acc = _loss_head(cur, loss_target[0])
    loss = lax.psum(0.5 / D * jnp.sum(loss_acc[0]), ("x", "y", "c"))

    dws = {}
    sg = {}
    res = {k: None for k in BIG}

    def wgrad(name, l, a, amode, b, bmode, scatter_keys=()):
        _, k, n = wts[name].shape
        hosts = [_ScatterPartials([parts[key] for key in scatter_keys])] if scatter_keys else ()
        if name in ROW_SHARDED:
            out = _mm_tn(f"dw_{name}_{l}", a, "1", b, "1", NQ * k, n, groups=1, hosts=hosts)
        else:
            out = _mm_tn(f"dw_{name}_{l}", a, amode, b, bmode, k, n, hosts=hosts)
        if scatter_keys:
            out, (contribs,) = out
            update(scatter_keys, contribs)
        dws[name, l] = out.reshape(1, NQ, k, n)

    def swap(keys):
        return _SwapHalves([dws[k] for k in keys])

    parts = {}

    def add_halves(keys, got):
        parts.update((k, _add_halves(dws[k], r, c_idx)) for k, r in zip(keys, got))

    def update(keys, contribs):
        for (name, l), gc in zip(keys, contribs):
            _, kq, n = wts[name].shape
            res[name] = _adam(f"adam_{name}_{l}", wts[name], mom[name], var[name], gc.reshape(NQ, 1, kq, n), l,
                              res[name])

    small = SMALL_SHARDED + SMALL_REPL
    late_small = [("a_w_dw", 0), ("a_b_pw1", 0)]
    early_small = [(k, l) for k in small for l in range(full[k].shape[0]) if (k, l) not in late_small]
    pending = None
    for i in reversed(range(DEPTH)):
        mix, j = i % 3, i // 3
        ple_args = (g, z2s[i], p[i, 0], vec("ln2_g", i), vec("ln2_b", i), gw["ple_w_gate", i], gw["ple_w_proj", i],
                    vec("ple_norm_g", i), i)
        if pending:
            (dz2, x2b, dgp, dqp, acc), (got,) = _bwd_ple(*ple_args, hosts=[swap(pending)])
            add_halves(pending, got)
        else:
            dz2, x2b, dgp, dqp, acc = _bwd_ple(*ple_args)
        sg["ple_norm_g", i], sg["ln2_g", i], sg["ln2_b", i] = acc[0], acc[1], acc[2]
        wgrad("ple_w_gate", i, x2b, "c", dgp, "1")
        wgrad("ple_w_proj", i, p[i, 0], "1", dqp, "c")
        ab, ub, hm = ffn_saved[i]
        ffn_args = (dz2, z1s[i], ab, ub, vec("ln1_g", i), vec("ln1_b", i), gw["ffn_w_gate", i], gw["ffn_w_up", i],
                    gw["ffn_w_down", i], i)
        if pending:
            (dz1, x1b, da, du, acc), (contribs,) = _bwd_ffn(
                *ffn_args, hosts=[_ScatterPartials([parts[key] for key in ffnw[i + 1]])])
            update(ffnw[i + 1], contribs)
        else:
            dz1, x1b, da, du, acc = _bwd_ffn(*ffn_args)
        sg["ln1_g", i], sg["ln1_b", i] = acc[0], acc[1]
        wgrad("ffn_w_gate", i, da, "1", x1b, "1", scatter_keys=mixw[i + 1][:1] if pending else ())
        wgrad("ffn_w_up", i, du, "1", x1b, "1", scatter_keys=mixw[i + 1][1:] + plew[i + 1] if pending else ())
        wgrad("ffn_w_down", i, hm, "1", dz2, "1")
        x0 = x0s[i]
        if mix == 0:
            h, glu, cv = saved[i]
            a2_args = (dz1, cv, vec("a_ln_g", j), vec("a_ln_b", j), gw["a_w_pw2", j], i)
            conv_args = (glu, conv_w("a_w_dw", j, 32), i)
            if i == 0:
                early = ffnw[0] + plew[0]
                (dcv, sb, acc), (got,) = _bwd_a2(*a2_args, hosts=[swap(early)])
                sg["a_ln_g", j], sg["a_ln_b", j], sg["a_b_dw", j] = acc[0], acc[1], acc[2]
                add_halves(early, got)
                wgrad("a_w_pw2", j, sb, "c", dz1, "1")
                (dglu, dwdw), (contribs, (g8_early,), got) = _bwd_conv_a(
                    dcv, *conv_args, hosts=[_ScatterPartials([parts[key] for key in early]),
                                            _Gather8(_pack([sg[pc] for pc in early_small])), swap(mixw[0][1:])])
                update(early, contribs)
                add_halves(mixw[0][1:], got)
                (g, dh, acc), (contribs,) = _bwd_a1(dglu, h, dz1, gw["a_w_pw1", j], i,
                                                    hosts=[_ScatterPartials([parts[key] for key in mixw[0][1:]])])
                update(mixw[0][1:], contribs)
            else:
                dcv, sb, acc = _bwd_a2(*a2_args)
                sg["a_ln_g", j], sg["a_ln_b", j], sg["a_b_dw", j] = acc[0], acc[1], acc[2]
                dglu, dwdw = _bwd_conv_a(dcv, *conv_args)
                wgrad("a_w_pw2", j, sb, "c", dz1, "1")
                g, dh, acc = _bwd_a1(dglu, h, dz1, gw["a_w_pw1", j], i)
            sg["a_w_dw", j] = dwdw[:CONV_A]
            sg["a_b_pw1", j] = acc[0]
            wgrad("a_w_pw1", j, x0, "1", dh, "c")
        elif mix == 1:
            zg, gg = saved[i]
            g, dh, mb, acc, dw_s, db_s = _bwd_b(dz1, zg, gg, vec("b_ln_g", 0), vec("b_ln_b", 0), gw["b_w_in", 0],
                                                gw["b_w_out", 0], ws, wst, bsx)
            sg["b_b_in", 0], sg["b_ln_g", 0], sg["b_ln_b", 0] = acc[0], acc[1, :E], acc[2, :E]
            sg["b_w_s", 0], sg["b_b_s", 0] = dw_s, jnp.sum(db_s, axis=-1)
            wgrad("b_w_out", 0, mb, "c", dz1, "1")
            wgrad("b_w_in", 0, x0, "1", dh, "c")
        else:
            (hc,) = saved[i]
            wc = conv_w("c_w_conv", 0, 8)
            dy, dbg, mb = _bwd_c2(dz1, hc, wc, gw["c_w_out", 0])
            wgrad("c_w_out", 0, mb, "c", dz1, "1")
            g, dhc, dwc = _bwd_c1(dy, hc, dbg, dz1, wc, gw["c_w_in", 0])
            sg["c_w_conv", 0] = dwc[:CONV_C]
            wgrad("c_w_in", 0, x0, "1", dhc, "c")
        pending = mixw[i] + ffnw[i] + plew[i] if i > 0 else mixw[0][:1]
    grad_x = g[None]
    add_halves(pending, _comm_only("swap_last", swap(pending)))
    update(pending, _comm_only("scatter_last", _ScatterPartials([parts[key] for key in pending])))

    g8_late = _gather8("gather_small_late", _pack([sg[pc] for pc in late_small]))
    sums = dict(zip(early_small, _unpack(_sum8("sum8_early", g8_early), [sg[pc].shape for pc in early_small])))
    sums.update(zip(late_small, _unpack(_sum8("sum8_late", g8_late), [sg[pc].shape for pc in late_small])))
    gsum = [jnp.stack([sums[k, l] for l in range(full[k].shape[0])]) for k in small]
    gmine = []
    for k, gs in zip(small, gsum):
        if k in SMALL_SHARDED:
            wdt = wts[k].shape[-1]
            gs = lax.dynamic_slice_in_dim(gs, q_idx * wdt, wdt, axis=gs.ndim - 1)
        gmine.append(gs)
    packed = [_pack(t)[None] for t in ([wts[k] for k in small], [mom[k] for k in small], [var[k] for k in small])]
    outs = _adam("adam_small", packed[0], packed[1], packed[2], _pack(gmine)[None, None], 0, None)
    unpacked = [_unpack(o[0], [wts[k].shape for k in small]) for o in outs]
    for i, k in enumerate(small):
        res[k] = tuple(u[i] for u in unpacked)

    for k in TRANSPOSED:
        res[k] = tuple(jnp.transpose(r, (0, 2, 1)) for r in res[k])
    return (loss, grad_x, *[res[k][0] for k in WEIGHTS], *[res[k][1] for k in WEIGHTS],
            *[res[k][2] for k in WEIGHTS], *[res[k][3] for k in WEIGHTS])
```

```python
import functools

import jax
import jax.numpy as jnp
from jax import lax
from jax.experimental import pallas as pl
from jax.experimental.pallas import tpu as pltpu

F32, BF16 = jnp.float32, jnp.bfloat16
S = 4096
D = 1024
E = 2048
FF = 2816
FQ = FF // 4
NQ = 4
DEPTH = 4
ALPHA = (2 * DEPTH) ** 0.25
LN_EPS = 1e-5
CONV_A, CONV_C = 31, 3
HALO_A, HALO_C = 32, 8
SGU_T, SGU_H, SGU_G, SGU_CHUNK = 128, 8, 256, 64
VMEM_LIMIT = 56 * 1024 * 1024
DW_BLOCK_BUDGET = 40 * 1024 * 1024
MESH = pl.DeviceIdType.MESH
ADAM_LR, ADAM_B1, ADAM_B2, ADAM_EPS, ADAM_WD, ADAM_STEP = 0.001, 0.9, 0.999, 1e-08, 0.01, 10
GELU_C, GELU_A = 0.7978845608028654, 0.044715

BIG = ["a_w_pw1", "a_w_pw2", "b_w_in", "b_w_out", "c_w_in", "c_w_out",
       "ffn_w_gate", "ffn_w_up", "ffn_w_down", "ple_w_gate", "ple_w_proj"]
TRANSPOSED = ["ffn_w_gate", "ffn_w_up"]
ROW_SHARDED = ["a_w_pw2", "b_w_out", "c_w_out", "ffn_w_gate", "ffn_w_up", "ffn_w_down", "ple_w_gate"]
SMALL_SHARDED = ["a_b_pw1", "a_w_dw", "a_b_dw", "a_ln_g", "a_ln_b", "c_w_conv"]
SMALL_REPL = ["b_b_in", "b_ln_g", "b_ln_b", "b_w_s", "b_b_s", "ln1_g", "ln1_b", "ln2_g", "ln2_b", "ple_norm_g"]
WEIGHTS = ["a_w_pw1", "a_b_pw1", "a_w_dw", "a_b_dw", "a_ln_g", "a_ln_b", "a_w_pw2", "b_w_in", "b_b_in", "b_ln_g",
           "b_ln_b", "b_w_s", "b_b_s", "b_w_out", "c_w_in", "c_w_conv", "c_w_out", "ln1_g", "ln1_b", "ln2_g",
           "ln2_b", "ffn_w_gate", "ffn_w_up", "ffn_w_down", "ple_w_gate", "ple_w_proj", "ple_norm_g"]


def _call(name, body, grid, in_specs, out_specs, out_shape, scratch=(), aliases=None, hosts=()):
    params = pltpu.CompilerParams(dimension_semantics=("arbitrary",) * len(grid), vmem_limit_bytes=VMEM_LIMIT)
    if not hosts:
        return pl.pallas_call(
            body, name=name, grid=grid, in_specs=in_specs, out_specs=out_specs, out_shape=out_shape,
            scratch_shapes=list(scratch), input_output_aliases=aliases or {}, compiler_params=params)
    assert len(grid) == 1 and not aliases
    single = not isinstance(out_shape, (list, tuple))
    own_shapes = [out_shape] if single else list(out_shape)
    own_specs = [out_specs] if single else list(out_specs)
    n_in, n_out, n_scr = len(in_specs), len(own_shapes), len(scratch)
    h_in = [len(h.arrays) for h in hosts]
    h_out = [len(h.out_shapes) for h in hosts]
    h_sem = [len(h.sems) for h in hosts]

    def split(refs, counts):
        out, off = [], 0
        for cnt in counts:
            out.append(refs[off:off + cnt])
            off += cnt
        return out

    def wrapped(*refs):
        ins, hin, outs, hout, scr, hsem = split(refs, [n_in, sum(h_in), n_out, sum(h_out), n_scr, sum(h_sem)])
        per_host = list(zip(hosts, split(hin, h_in), split(hout, h_out), split(hsem, h_sem)))

        @pl.when(pl.program_id(0) == 0)
        def _():
            for h, a, o, s in per_host:
                h.start(a, o, s)

        body(*ins, *outs, *scr)

        for h, a, o, s in per_host:
            for step, ws in sorted(h.forward_steps(grid[0]).items()):
                pl.when(pl.program_id(0) == step)(functools.partial(h.forward, ws, a, o, s))

        @pl.when(pl.program_id(0) == grid[0] - 1)
        def _():
            for h, a, o, s in per_host:
                h.complete(a, o, s)

    any_spec = pl.BlockSpec(memory_space=pl.ANY)
    call = pl.pallas_call(
        wrapped, name=name, grid=grid, in_specs=list(in_specs) + [any_spec] * sum(h_in),
        out_specs=own_specs + [any_spec] * sum(h_out),
        out_shape=own_shapes + [s for h in hosts for s in h.out_shapes],
        scratch_shapes=list(scratch) + [s for h in hosts for s in h.sems], compiler_params=params)

    def run(*args):
        res = call(*args, *[a for h in hosts for a in h.arrays])
        own = res[0] if single else list(res[:n_out])
        return own, split(list(res[n_out:]), h_out)

    return run


def _sds(shape, dtype=F32):
    return jax.ShapeDtypeStruct(shape, dtype)


def _row(tm, c):
    return pl.BlockSpec((tm, c), lambda i: (i, 0))


def _grow(g, tm, c):
    return pl.BlockSpec((g, tm, c), lambda i: (0, i, 0))


def _const(shape):
    nd = len(shape)
    return pl.BlockSpec(shape, lambda i: (0,) * nd, pipeline_mode=pl.Buffered(1))


def _wspec(w):
    return pl.BlockSpec((NQ, None, w.shape[2], w.shape[3]), lambda i: (0, 0, 0, 0), pipeline_mode=pl.Buffered(1))


def _prev(tm, hb, c):
    return pl.BlockSpec((hb, c), lambda i: (jnp.maximum(i * (tm // hb) - 1, 0), 0))


def _next(tm, hb, c):
    return pl.BlockSpec((hb, c), lambda i: (jnp.minimum((i + 1) * (tm // hb), S // hb - 1), 0))


def _acc(r, c):
    return pl.BlockSpec((r, c), lambda i: (0, 0))


def _sig(x):
    return 1.0 / (1.0 + jnp.exp(-x))


def _ln(z, g, b):
    mu = jnp.mean(z, axis=-1, keepdims=True)
    zc = z - mu
    rstd = lax.rsqrt(jnp.mean(zc * zc, axis=-1, keepdims=True) + LN_EPS)
    xhat = zc * rstd
    return xhat * g + b, xhat, rstd


def _ln_bwd(dyg, xhat, rstd):
    return rstd * (dyg - jnp.mean(dyg, axis=-1, keepdims=True) - xhat * jnp.mean(dyg * xhat, axis=-1, keepdims=True))


def _mm(a, w):
    return jnp.dot(a.astype(BF16), w, preferred_element_type=F32)


def _mmt(a, w):
    return lax.dot_general(a.astype(BF16), w, (((1,), (1,)), ((), ())), preferred_element_type=F32)


def _colsum(x):
    return jnp.sum(x, axis=0, keepdims=True)


def _gelu(x):
    t = jnp.tanh(GELU_C * (x + GELU_A * x * x * x))
    return 0.5 * x * (1.0 + t), t


def _gelu_grad(x, t):
    return 0.5 * (1.0 + t) + 0.5 * x * (1.0 - t * t) * GELU_C * (1.0 + 3.0 * GELU_A * x * x)


def _silu_grad(a, sg):
    return sg * (1.0 + a * (1.0 - sg))


def _sgu_masks():
    r = lax.broadcasted_iota(jnp.int32, (SGU_T, SGU_T), 0) // SGU_CHUNK
    c = lax.broadcasted_iota(jnp.int32, (SGU_T, SGU_T), 1) // SGU_CHUNK
    return r >= c, c >= r


def _fill_halo(buf, lo, n, halo_val_fn, is_edge):
    @pl.when(is_edge)
    def _():
        buf[lo:lo + n, :] = jnp.zeros((n, buf.shape[1]), F32)

    @pl.when(jnp.logical_not(is_edge))
    def _():
        buf[lo:lo + n, :] = halo_val_fn()


SUB, LANE = 8, 128
ROWS_AT_ONCE = 16


def _shift_copies(buf, sh):
    rows = sh.shape[1]
    for s in range(1, SUB):
        sh[s - 1, :, :] = buf[pl.ds(s, rows), :]


def _tiles(buf, sh, s, first, count, group0, lanes):
    src = buf if s == 0 else sh.at[s - 1]
    return {t: src[pl.ds(pl.multiple_of((group0 + t) * SUB, SUB), SUB), lanes] for t in range(first, first + count)}


def _by_shift(offsets):
    out = []
    for s in range(SUB):
        taps = [(k, o // SUB) for k, o in enumerate(offsets) if o % SUB == s]
        if taps:
            out.append((s, taps))
    return out


def _conv_rows(out_ref, w_ref, bias_ref, offsets, buf, sh, tm):
    n = ROWS_AT_ONCE
    for cb in range(D // LANE):
        lanes = slice(cb * LANE, (cb + 1) * LANE)
        bias = None if bias_ref is None else jnp.broadcast_to(bias_ref[:, lanes], (SUB, LANE))

        def body(jb, carry):
            accs = [bias] * n
            for s, taps in _by_shift(offsets):
                ms = [m for _, m in taps]
                tiles = _tiles(buf, sh, s, min(ms), max(ms) - min(ms) + n, jb * n, lanes)
                for k, m in taps:
                    wk = jnp.broadcast_to(w_ref[k:k + 1, lanes], (SUB, LANE))
                    for jj in range(n):
                        t = wk * tiles[m + jj]
                        accs[jj] = t if accs[jj] is None else accs[jj] + t
            for jj in range(n):
                out_ref[pl.ds(pl.multiple_of((jb * n + jj) * SUB, SUB), SUB), lanes] = accs[jj]
            return carry

        lax.fori_loop(0, tm // (SUB * n), body, 0)


def _conv_wgrad(dw_ref, d_ref, offsets, buf, sh, tm):
    n = 4
    for cb in range(D // LANE):
        lanes = slice(cb * LANE, (cb + 1) * LANE)

        def body(jq, accs):
            accs = list(accs)
            d = [d_ref[pl.ds(pl.multiple_of((jq * n + jj) * SUB, SUB), SUB), lanes] for jj in range(n)]
            for s, taps in _by_shift(offsets):
                ms = [m for _, m in taps]
                tiles = _tiles(buf, sh, s, min(ms), max(ms) - min(ms) + n, jq * n, lanes)
                for k, m in taps:
                    for jj in range(n):
                        accs[k] = accs[k] + d[jj] * tiles[m + jj]
            return tuple(accs)

        accs = lax.fori_loop(0, tm // (SUB * n), body, tuple(jnp.zeros((SUB, LANE), F32) for _ in offsets))
        for k, acc in enumerate(accs):
            dw_ref[k:k + 1, lanes] += jnp.sum(acc, axis=0, keepdims=True)


def _fwd_a1(x0, w1, b1, l, hosts=()):
    tm = 512

    def body(x_ref, w_ref, b_ref, h_ref, glu_ref):
        xb = x_ref[...].astype(BF16)
        for q in range(NQ):
            sl = slice(q * 512, (q + 1) * 512)
            h_ref[:, sl] = jnp.dot(xb, w_ref[q], preferred_element_type=F32) + b_ref[:, sl]
        glu_ref[...] = h_ref[:, :D] * _sig(h_ref[:, D:])

    return _call(f"fwd_a1_{l}", body, (S // tm,), [_row(tm, D), _wspec(w1), _const((1, 2 * D))],
                 [_row(tm, 2 * D), _row(tm, D)], [_sds((S, 2 * D)), _sds((S, D))], hosts=hosts)(x0, w1, b1)


def _fwd_a2(glu, x0, wdw, bdw, lg, lb, w2, l, hosts=()):
    tm = 256

    def body(g_ref, gp_ref, x_ref, wdw_ref, bdw_ref, lg_ref, lb_ref, w2_ref, z_ref, cv_ref, buf, sh):
        i = pl.program_id(0)
        _fill_halo(buf, 0, HALO_A, lambda: gp_ref[...], i == 0)
        buf[HALO_A:HALO_A + tm, :] = g_ref[...]
        _shift_copies(buf, sh)
        _conv_rows(cv_ref, wdw_ref, bdw_ref, [HALO_A - (CONV_A - 1) + k for k in range(CONV_A)], buf, sh, tm)
        n, _, _ = _ln(cv_ref[...], lg_ref[...], lb_ref[...])
        sb = (n * _sig(n)).astype(BF16)
        z_ref[...] = ALPHA * x_ref[...] + jnp.dot(sb, w2_ref[...], preferred_element_type=F32)

    return _call(f"fwd_a2_{l}", body, (S // tm,),
                 [_row(tm, D), _prev(tm, HALO_A, D), _row(tm, D), _const((32, D)), _const((1, D)), _const((1, D)),
                  _const((1, D)), _const(w2.shape)],
                 [_row(tm, D), _row(tm, D)], [_sds((S, D)), _sds((S, D))],
                 scratch=[pltpu.VMEM((HALO_A + tm, D), F32), pltpu.VMEM((SUB - 1, HALO_A + tm - SUB, D), F32)],
                 hosts=hosts)(glu, glu, x0, wdw, bdw, lg, lb, w2)


def _fwd_b(x0, win, b_in, lg, lb, ws, bsx, wout, hosts=()):
    tm = 256

    def body(x_ref, win_ref, bin_ref, lg_ref, lb_ref, ws_ref, bsx_ref, wout_ref, z_ref, zg_ref, gg_ref, f_scr, h_ref):
        xb = x_ref[...].astype(BF16)
        for q in range(NQ):
            sl = slice(q * 1024, (q + 1) * 1024)
            h_ref[:, sl] = jnp.dot(xb, win_ref[q], preferred_element_type=F32) + bin_ref[:, sl]
        u, tu = _gelu(h_ref[:, :E])
        v, tv = _gelu(h_ref[:, E:])
        zg_ref[:, 0:E] = u.astype(BF16)
        zg_ref[:, E:2 * E] = v.astype(BF16)
        gg_ref[:, 0:E] = _gelu_grad(h_ref[:, :E], tu).astype(BF16)
        gg_ref[:, E:2 * E] = _gelu_grad(h_ref[:, E:], tv).astype(BF16)
        vn, _, _ = _ln(v, lg_ref[...], lb_ref[...])
        vnb = vn.astype(BF16)
        mask, _ = _sgu_masks()
        for hd in range(SGU_H):
            wm = jnp.where(mask, ws_ref[hd], 0.0).astype(BF16)
            cs = slice(hd * SGU_G, (hd + 1) * SGU_G)
            for n in range(tm // SGU_T):
                rs = slice(n * SGU_T, (n + 1) * SGU_T)
                f_scr[rs, cs] = jnp.dot(wm, vnb[rs, cs], preferred_element_type=F32) + bsx_ref[hd]
        mb = (u * f_scr[...]).astype(BF16)
        z_ref[...] = ALPHA * x_ref[...] + jnp.dot(mb, wout_ref[...], preferred_element_type=F32)

    return _call("fwd_b", body, (S // tm,),
                 [_row(tm, D), _wspec(win), _const((1, 2 * E)), _const((1, E)), _const((1, E)),
                  _const((SGU_H, SGU_T, SGU_T)), _const((SGU_H, SGU_T, SGU_G)), _const(wout.shape)],
                 [_row(tm, D), _row(tm, 2 * E), _row(tm, 2 * E)],
                 [_sds((S, D)), _sds((S, 2 * E), BF16), _sds((S, 2 * E), BF16)],
                 scratch=[pltpu.VMEM((tm, E), F32), pltpu.VMEM((tm, 2 * E), F32)], hosts=hosts
                 )(x0, win, b_in, lg, lb, ws, bsx, wout)


def _fwd_c1(x0, win, hosts=()):
    tm = 512

    def body(x_ref, w_ref, hc_ref):
        xb = x_ref[...].astype(BF16)
        for q in range(NQ):
            hc_ref[:, q * 768:(q + 1) * 768] = jnp.dot(xb, w_ref[q], preferred_element_type=F32)

    return _call("fwd_c1", body, (S // tm,), [_row(tm, D), _wspec(win)], _row(tm, 3 * D),
                 _sds((S, 3 * D)), hosts=hosts)(x0, win)


def _short_conv(buf, hc_ref, hcp_ref, wc_ref, tm, i):
    _fill_halo(buf, 0, HALO_C, lambda: hcp_ref[:, D:2 * D] * hcp_ref[:, 2 * D:], i == 0)
    buf[HALO_C:HALO_C + tm, :] = hc_ref[:, D:2 * D] * hc_ref[:, 2 * D:]
    y = wc_ref[0:1, :] * buf[pl.ds(HALO_C - 2, tm), :]
    for k in range(1, CONV_C):
        y = y + wc_ref[k:k + 1, :] * buf[pl.ds(HALO_C - 2 + k, tm), :]
    return y


def _fwd_c2(hc, x0, wc, wout, hosts=()):
    tm = 256

    def body(hc_ref, hcp_ref, x_ref, wc_ref, wout_ref, z_ref, buf):
        y = _short_conv(buf, hc_ref, hcp_ref, wc_ref, tm, pl.program_id(0))
        mb = (hc_ref[:, :D] * y).astype(BF16)
        z_ref[...] = ALPHA * x_ref[...] + jnp.dot(mb, wout_ref[...], preferred_element_type=F32)

    return _call("fwd_c2", body, (S // tm,),
                 [_row(tm, 3 * D), _prev(tm, HALO_C, 3 * D), _row(tm, D), _const((8, D)), _const(wout.shape)],
                 _row(tm, D), _sds((S, D)), scratch=[pltpu.VMEM((HALO_C + tm, D), F32)], hosts=hosts
                 )(hc, hc, x0, wc, wout)


def _fwd_ffn(z1, lg, lb, wgt, wut, wd, l, hosts=()):
    tm = 256

    def body(z_ref, lg_ref, lb_ref, wg_ref, wu_ref, wd_ref, o_ref, a_ref, u_ref, hm_ref):
        x1, _, _ = _ln(z_ref[...], lg_ref[...], lb_ref[...])
        xb = x1.astype(BF16)
        a = _mmt(xb, wg_ref[...])
        u = _mmt(xb, wu_ref[...])
        hmb = (a * _sig(a) * u).astype(BF16)
        a_ref[...] = a.astype(BF16)
        u_ref[...] = u.astype(BF16)
        hm_ref[...] = hmb
        o_ref[...] = ALPHA * x1 + jnp.dot(hmb, wd_ref[...], preferred_element_type=F32)

    return _call(f"fwd_ffn_{l}", body, (S // tm,),
                 [_row(tm, D), _const((1, D)), _const((1, D)), _const((FF, D)), _const((FF, D)), _const((FF, D))],
                 [_row(tm, D), _row(tm, FF), _row(tm, FF), _row(tm, FF)],
                 [_sds((S, D)), _sds((S, FF), BF16), _sds((S, FF), BF16), _sds((S, FF), BF16)],
                 hosts=hosts)(z1, lg, lb, wgt, wut, wd)


def _ple_parts(z2, p, lg, lb, wg_ref, wp_ref, pg):
    x2, xhat, rstd = _ln(z2, lg, lb)
    xb = x2.astype(BF16)
    gate = _sig(jnp.dot(xb, wg_ref[...], preferred_element_type=F32))
    pb = p.astype(BF16)
    qp = jnp.concatenate([jnp.dot(pb, wp_ref[q], preferred_element_type=F32) for q in range(NQ)], axis=1)
    rs = lax.rsqrt(jnp.mean(qp * qp, axis=-1, keepdims=True) + LN_EPS)
    qn = qp * rs
    return x2, xhat, rstd, xb, gate, qn, rs, qn * pg


def _fwd_ple(z2, p, lg, lb, wg, wp, pg, l, hosts=()):
    tm = 512

    def body(z_ref, p_ref, lg_ref, lb_ref, wg_ref, wp_ref, pg_ref, o_ref):
        x2, _, _, _, gate, _, _, r = _ple_parts(z_ref[...], p_ref[...], lg_ref[...], lb_ref[...], wg_ref, wp_ref,
                                                pg_ref[...])
        o_ref[...] = x2 + gate * r

    return _call(f"fwd_ple_{l}", body, (S // tm,),
                 [_row(tm, D), _row(tm, 256), _const((1, D)), _const((1, D)), _const(wg.shape), _wspec(wp),
                  _const((1, D))],
                 _row(tm, D), _sds((S, D)), hosts=hosts)(z2, p, lg, lb, wg, wp, pg)


def _loss_head(y, target):
    tm = 512

    def body(y_ref, t_ref, dy_ref, acc_ref):
        @pl.when(pl.program_id(0) == 0)
        def _():
            acc_ref[...] = jnp.zeros_like(acc_ref)

        e = y_ref[...] - t_ref[...]
        dy_ref[...] = e * (1.0 / D)
        acc_ref[0:1, :] += _colsum(e * e)

    return _call("loss_head", body, (S // tm,), [_row(tm, D), _row(tm, D)], [_row(tm, D), _acc(8, D)],
                 [_sds((S, D)), _sds((8, D))])(y, target)


def _zero_first(*refs):
    @pl.when(pl.program_id(0) == 0)
    def _():
        for r in refs:
            r[...] = jnp.zeros_like(r)


def _bwd_ple(g, z2, p, lg, lb, wg, wp, pg, l, hosts=()):
    tm = 256

    def body(g_ref, z_ref, p_ref, lg_ref, lb_ref, wg_ref, wp_ref, pg_ref, dz_ref, xb_ref, dgp_ref, dqp_ref, acc_ref):
        _zero_first(acc_ref)
        gin = g_ref[...]
        lgv, pgv = lg_ref[...], pg_ref[...]
        _, xhat, rstd, xb, gate, qn, rs, r = _ple_parts(z_ref[...], p_ref[...], lgv, lb_ref[...], wg_ref, wp_ref, pgv)
        xb_ref[...] = xb
        dgpb = (gin * r * gate * (1.0 - gate)).astype(BF16)
        dgp_ref[...] = dgpb
        dx2 = gin + _mmt(dgpb, wg_ref[...])
        dr = gin * gate
        acc_ref[0:1, :] += _colsum(dr * qn)
        t = dr * pgv
        dqp_ref[...] = (rs * (t - qn * jnp.mean(t * qn, axis=-1, keepdims=True))).astype(BF16)
        acc_ref[1:2, :] += _colsum(dx2 * xhat)
        acc_ref[2:3, :] += _colsum(dx2)
        dz_ref[...] = _ln_bwd(dx2 * lgv, xhat, rstd)

    return _call(f"bwd_ple_{l}", body, (S // tm,),
                 [_row(tm, D), _row(tm, D), _row(tm, 256), _const((1, D)), _const((1, D)), _const(wg.shape),
                  _wspec(wp), _const((1, D))],
                 [_row(tm, D), _row(tm, D), _row(tm, D), _row(tm, D), _acc(8, D)],
                 [_sds((S, D)), _sds((S, D), BF16), _sds((S, D), BF16), _sds((S, D), BF16), _sds((8, D))],
                 hosts=hosts)(g, z2, p, lg, lb, wg, wp, pg)


def _bwd_ffn(dz2, z1, ab, ub, lg, lb, wgt, wut, wd, l, hosts=()):
    tm = 256

    def body(dz2_ref, z_ref, a_ref, u_ref, lg_ref, lb_ref, wg_ref, wu_ref, wd_ref, dz1_ref, xb_ref, da_ref, du_ref,
             acc_ref):
        _zero_first(acc_ref)
        dz2v = dz2_ref[...]
        lgv = lg_ref[...]
        x1, xhat, rstd = _ln(z_ref[...], lgv, lb_ref[...])
        xb_ref[...] = x1.astype(BF16)
        a = a_ref[...].astype(F32)
        u = u_ref[...].astype(F32)
        sg = _sig(a)
        dhm = _mmt(dz2v, wd_ref[...])
        dub = (dhm * (a * sg)).astype(BF16)
        dab = (dhm * u * _silu_grad(a, sg)).astype(BF16)
        da_ref[...] = dab
        du_ref[...] = dub
        dx1 = ALPHA * dz2v + _mm(dab, wg_ref[...]) + _mm(dub, wu_ref[...])
        acc_ref[0:1, :] += _colsum(dx1 * xhat)
        acc_ref[1:2, :] += _colsum(dx1)
        dz1_ref[...] = _ln_bwd(dx1 * lgv, xhat, rstd)

    return _call(f"bwd_ffn_{l}", body, (S // tm,),
                 [_row(tm, D), _row(tm, D), _row(tm, FF), _row(tm, FF), _const((1, D)), _const((1, D)),
                  _const((FF, D)), _const((FF, D)), _const((FF, D))],
                 [_row(tm, D), _row(tm, D), _row(tm, FF), _row(tm, FF), _acc(8, D)],
                 [_sds((S, D)), _sds((S, D), BF16), _sds((S, FF), BF16), _sds((S, FF), BF16), _sds((8, D))],
                 hosts=hosts)(dz2, z1, ab, ub, lg, lb, wgt, wut, wd)


def _bwd_a2(dz1, cv, lg, lb, w2, l, hosts=()):
    tm = 512

    def body(dz_ref, cv_ref, lg_ref, lb_ref, w2_ref, dcv_ref, sb_ref, acc_ref):
        _zero_first(acc_ref)
        lgv = lg_ref[...]
        n, xhat, rstd = _ln(cv_ref[...], lgv, lb_ref[...])
        sg = _sig(n)
        sb_ref[...] = (n * sg).astype(BF16)
        dzb = dz_ref[...].astype(BF16)
        ds = _mmt(dzb, w2_ref[...])
        dn = ds * _silu_grad(n, sg)
        acc_ref[0:1, :] += _colsum(dn * xhat)
        acc_ref[1:2, :] += _colsum(dn)
        dcv = _ln_bwd(dn * lgv, xhat, rstd)
        acc_ref[2:3, :] += _colsum(dcv)
        dcv_ref[...] = dcv

    return _call(f"bwd_a2_{l}", body, (S // tm,),
                 [_row(tm, D), _row(tm, D), _const((1, D)), _const((1, D)), _const(w2.shape)],
                 [_row(tm, D), _row(tm, D), _acc(8, D)],
                 [_sds((S, D)), _sds((S, D), BF16), _sds((8, D))], hosts=hosts)(dz1, cv, lg, lb, w2)


def _bwd_conv_a(dcv, glu, wdw, l, hosts=()):
    tm = 256
    nb = S // tm

    def body(d_ref, dn_ref, g_ref, gp_ref, w_ref, dglu_ref, dw_ref, bufd, bufx, sh):
        i = pl.program_id(0)
        _zero_first(dw_ref)
        bufd[0:tm, :] = d_ref[...]
        _fill_halo(bufd, tm, HALO_A, lambda: dn_ref[...], i == nb - 1)
        _fill_halo(bufx, 0, HALO_A, lambda: gp_ref[...], i == 0)
        bufx[HALO_A:HALO_A + tm, :] = g_ref[...]
        _shift_copies(bufd, sh)
        _conv_rows(dglu_ref, w_ref, None, [CONV_A - 1 - k for k in range(CONV_A)], bufd, sh, tm)
        _shift_copies(bufx, sh)
        _conv_wgrad(dw_ref, d_ref, [HALO_A - (CONV_A - 1) + k for k in range(CONV_A)], bufx, sh, tm)

    return _call(f"bwd_conv_a_{l}", body, (nb,),
                 [_row(tm, D), _next(tm, HALO_A, D), _row(tm, D), _prev(tm, HALO_A, D), _const((32, D))],
                 [_row(tm, D), _acc(32, D)], [_sds((S, D)), _sds((32, D))],
                 scratch=[pltpu.VMEM((tm + HALO_A, D), F32), pltpu.VMEM((HALO_A + tm, D), F32),
                          pltpu.VMEM((SUB - 1, HALO_A + tm - SUB, D), F32)], hosts=hosts)(dcv, dcv, glu, glu, wdw)


def _bwd_a1(dglu, h, dz1, w1, l, hosts=()):
    tm = 256

    def body(dg_ref, h_ref, dz_ref, w_ref, dx_ref, dh_ref, acc_ref):
        _zero_first(acc_ref)
        a, g = h_ref[:, :D], h_ref[:, D:]
        sg = _sig(g)
        dgl = dg_ref[...]
        da = dgl * sg
        dg = dgl * a * sg * (1.0 - sg)
        acc_ref[0:1, 0:D] += _colsum(da)
        acc_ref[0:1, D:2 * D] += _colsum(dg)
        dh_ref[:, 0:D] = da.astype(BF16)
        dh_ref[:, D:2 * D] = dg.astype(BF16)
        dx = ALPHA * dz_ref[...]
        for q in range(NQ):
            dx = dx + _mmt(dh_ref[:, q * 512:(q + 1) * 512], w_ref[q])
        dx_ref[...] = dx

    return _call(f"bwd_a1_{l}", body, (S // tm,),
                 [_row(tm, D), _row(tm, 2 * D), _row(tm, D), _wspec(w1)],
                 [_row(tm, D), _row(tm, 2 * D), _acc(8, 2 * D)],
                 [_sds((S, D)), _sds((S, 2 * D), BF16), _sds((8, 2 * D))], hosts=hosts)(dglu, h, dz1, w1)


def _bwd_c2(dz1, hc, wc, wout):
    tm = 256

    def body(dz_ref, hc_ref, hcp_ref, wc_ref, wout_ref, dy_ref, dbg_ref, mb_ref, buf):
        y = _short_conv(buf, hc_ref, hcp_ref, wc_ref, tm, pl.program_id(0))
        dzb = dz_ref[...].astype(BF16)
        dm = _mmt(dzb, wout_ref[...])
        bg = hc_ref[:, :D]
        mb_ref[...] = (bg * y).astype(BF16)
        dbg_ref[...] = (dm * y).astype(BF16)
        dy_ref[...] = dm * bg

    return _call("bwd_c2", body, (S // tm,),
                 [_row(tm, D), _row(tm, 3 * D), _prev(tm, HALO_C, 3 * D), _const((8, D)), _const(wout.shape)],
                 [_row(tm, D), _row(tm, D), _row(tm, D)],
                 [_sds((S, D)), _sds((S, D), BF16), _sds((S, D), BF16)],
                 scratch=[pltpu.VMEM((HALO_C + tm, D), F32)])(dz1, hc, hc, wc, wout)


def _bwd_c1(dy, hc, dbg, dz1, wc, win):
    tm = 256
    nb = S // tm

    def body(d_ref, dn_ref, hc_ref, hcp_ref, dbg_ref, dz_ref, wc_ref, win_ref, dx_ref, dhc_ref, dwc_ref, bufd, bufq):
        i = pl.program_id(0)
        _zero_first(dwc_ref)
        bufd[0:tm, :] = d_ref[...]
        _fill_halo(bufd, tm, HALO_C, lambda: dn_ref[...], i == nb - 1)
        _fill_halo(bufq, 0, HALO_C, lambda: hcp_ref[:, D:2 * D] * hcp_ref[:, 2 * D:], i == 0)
        bufq[HALO_C:HALO_C + tm, :] = hc_ref[:, D:2 * D] * hc_ref[:, 2 * D:]
        dq = wc_ref[0:1, :] * bufd[pl.ds(CONV_C - 1, tm), :]
        for k in range(1, CONV_C):
            dq = dq + wc_ref[k:k + 1, :] * bufd[pl.ds(CONV_C - 1 - k, tm), :]
        dv = d_ref[...]
        for k in range(CONV_C):
            dwc_ref[k:k + 1, :] += _colsum(dv * bufq[pl.ds(HALO_C - (CONV_C - 1) + k, tm), :])
        dhc_ref[:, 0:D] = dbg_ref[...]
        dhc_ref[:, D:2 * D] = (dq * hc_ref[:, 2 * D:]).astype(BF16)
        dhc_ref[:, 2 * D:3 * D] = (dq * hc_ref[:, D:2 * D]).astype(BF16)
        dx = ALPHA * dz_ref[...]
        for q in range(NQ):
            dx = dx + _mmt(dhc_ref[:, q * 768:(q + 1) * 768], win_ref[q])
        dx_ref[...] = dx

    return _call("bwd_c1", body, (nb,),
                 [_row(tm, D), _next(tm, HALO_C, D), _row(tm, 3 * D), _prev(tm, HALO_C, 3 * D), _row(tm, D),
                  _row(tm, D), _const((8, D)), _wspec(win)],
                 [_row(tm, D), _row(tm, 3 * D), _acc(8, D)],
                 [_sds((S, D)), _sds((S, 3 * D), BF16), _sds((8, D))],
                 scratch=[pltpu.VMEM((tm + HALO_C, D), F32), pltpu.VMEM((HALO_C + tm, D), F32)]
                 )(dy, dy, hc, hc, dbg, dz1, wc, win)


def _bwd_b(dz1, zg, gg, lg, lb, win, wout, ws, wst, bsx):
    tm = 128
    nb = S // tm

    def body(dz_ref, zg_ref, gg_ref, lg_ref, lb_ref, win_ref, wout_ref, ws_ref, wst_ref, bsx_ref,
             dx_ref, dh_ref, mb_ref, acc_ref, dws_ref, dbs_ref, f_scr, dvn_scr):
        _zero_first(acc_ref, dws_ref, dbs_ref)
        lgv = lg_ref[...]
        u = zg_ref[:, :E].astype(F32)
        v = zg_ref[:, E:].astype(F32)
        vn, xhat, rstd = _ln(v, lgv, lb_ref[...])
        vnb = vn.astype(BF16)
        dzb = dz_ref[...].astype(BF16)
        dm = _mmt(dzb, wout_ref[...])
        mask, mask_t = _sgu_masks()
        for hd in range(SGU_H):
            wm = jnp.where(mask, ws_ref[hd], 0.0).astype(BF16)
            cs = slice(hd * SGU_G, (hd + 1) * SGU_G)
            for n in range(tm // SGU_T):
                rs = slice(n * SGU_T, (n + 1) * SGU_T)
                f_scr[rs, cs] = jnp.dot(wm, vnb[rs, cs], preferred_element_type=F32) + bsx_ref[hd]
        f = f_scr[...]
        mb_ref[...] = (u * f).astype(BF16)
        du = dm * f
        df = dm * u
        dfb = df.astype(BF16)
        for hd in range(SGU_H):
            wmt = jnp.where(mask_t, wst_ref[hd], 0.0).astype(BF16)
            cs = slice(hd * SGU_G, (hd + 1) * SGU_G)
            for n in range(tm // SGU_T):
                rs = slice(n * SGU_T, (n + 1) * SGU_T)
                dvn_scr[rs, cs] = jnp.dot(wmt, dfb[rs, cs], preferred_element_type=F32)
                dws_ref[hd] += lax.dot_general(dfb[rs, cs], vnb[rs, cs], (((1,), (1,)), ((), ())),
                                               preferred_element_type=F32)
                dbs_ref[hd] += df[rs, cs]
        dvn = dvn_scr[...]
        acc_ref[1:2, 0:E] += _colsum(dvn * xhat)
        acc_ref[2:3, 0:E] += _colsum(dvn)
        dv = _ln_bwd(dvn * lgv, xhat, rstd)
        dhu = du * gg_ref[:, :E].astype(F32)
        dhv = dv * gg_ref[:, E:].astype(F32)
        acc_ref[0:1, 0:E] += _colsum(dhu)
        acc_ref[0:1, E:2 * E] += _colsum(dhv)
        dh_ref[:, 0:E] = dhu.astype(BF16)
        dh_ref[:, E:2 * E] = dhv.astype(BF16)
        dx = ALPHA * dz_ref[...]
        for q in range(NQ):
            dx = dx + _mmt(dh_ref[:, q * 1024:(q + 1) * 1024], win_ref[q])
        dx_ref[...] = dx

        @pl.when(pl.program_id(0) == nb - 1)
        def _():
            for hd in range(SGU_H):
                dws_ref[hd] = jnp.where(mask, dws_ref[hd], 0.0)

    c3 = lambda a, b, c: pl.BlockSpec((a, b, c), lambda i: (0, 0, 0))
    return _call("bwd_b", body, (nb,),
                 [_row(tm, D), _row(tm, 2 * E), _row(tm, 2 * E), _const((1, E)), _const((1, E)), _wspec(win),
                  _const(wout.shape), _const((SGU_H, SGU_T, SGU_T)), _const((SGU_H, SGU_T, SGU_T)),
                  _const((SGU_H, SGU_T, SGU_G))],
                 [_row(tm, D), _row(tm, 2 * E), _row(tm, E), _acc(8, 2 * E), c3(SGU_H, SGU_T, SGU_T),
                  c3(SGU_H, SGU_T, SGU_G)],
                 [_sds((S, D)), _sds((S, 2 * E), BF16), _sds((S, E), BF16), _sds((8, 2 * E)),
                  _sds((SGU_H, SGU_T, SGU_T)), _sds((SGU_H, SGU_T, SGU_G))],
                 scratch=[pltpu.VMEM((tm, E), F32), pltpu.VMEM((tm, E), F32)]
                 )(dz1, zg, gg, lg, lb, win, wout, ws, wst, bsx)


def _mm_tn(name, a, amode, b, bmode, k, n, groups=NQ, hosts=()):
    def block_bytes(ts):
        ka = k if amode == "1" else groups * k
        nb = n if bmode == "1" else groups * n
        return 2 * (ts * ka * a.dtype.itemsize + ts * nb * b.dtype.itemsize + groups * k * n * 4)

    ts = min(1024 if block_bytes(1024) <= DW_BLOCK_BUDGET else 512, S)

    def spec(mode, w):
        if mode == "1":
            return pl.BlockSpec((ts, w), lambda s: (s, 0))
        if mode == "c":
            return pl.BlockSpec((ts, groups * w), lambda s: (s, 0))
        return pl.BlockSpec((groups, ts, w), lambda s: (0, s, 0))

    def pick(ref, mode, w, g):
        if mode == "1":
            return ref[...]
        if mode == "c":
            return ref[:, g * w:(g + 1) * w]
        return ref[g]

    def body(a_ref, b_ref, o_ref):
        _zero_first(o_ref)
        a_t = jnp.transpose(a_ref[...].astype(BF16)) if amode == "1" else None
        b_1 = b_ref[...].astype(BF16) if bmode == "1" else None
        for g in range(groups):
            lhs = a_t if amode == "1" else jnp.transpose(pick(a_ref, amode, k, g).astype(BF16))
            rhs = b_1 if bmode == "1" else pick(b_ref, bmode, n, g).astype(BF16)
            o_ref[0, g] += jnp.dot(lhs, rhs, preferred_element_type=F32)

    return _call(name, body, (S // ts,), [spec(amode, k), spec(bmode, n)],
                 pl.BlockSpec((1, groups, k, n), lambda s: (0, 0, 0, 0)), _sds((1, groups, k, n)), hosts=hosts)(a, b)


def _row_block(k, cap=256):
    return max(t for t in range(16, min(k, cap) + 1, 16) if k % t == 0)


def _cast_bf16(w, hosts=()):
    nl, k, n = w.shape
    tb = _row_block(k, 512)
    nb = k // tb

    def body(w_ref, o_ref):
        o_ref[...] = w_ref[...].astype(BF16)

    spec = pl.BlockSpec((None, tb, n), lambda i: (i // nb, i % nb, 0))
    return _call("cast_bf16", body, (nl * nb,), [spec], spec, _sds(w.shape, BF16), hosts=hosts)(w)


def _adam(name, w, m, v, gc, l, prev):
    nl, k, n = w.shape
    nc = gc.shape[0]
    tb = _row_block(k, 512)

    def body(w_ref, m_ref, v_ref, g_ref, *rest):
        go_ref, d_ref, mo_ref, vo_ref = rest[-4:]
        g = g_ref[0].astype(F32)
        for c in range(1, nc):
            g = g + g_ref[c].astype(F32)
        m2 = ADAM_B1 * m_ref[...] + (1.0 - ADAM_B1) * g
        v2 = ADAM_B2 * v_ref[...] + (1.0 - ADAM_B2) * (g * g)
        m_hat = m2 / (1.0 - ADAM_B1 ** ADAM_STEP)
        v_hat = v2 / (1.0 - ADAM_B2 ** ADAM_STEP)
        go_ref[...] = g
        d_ref[...] = -ADAM_LR * (m_hat / (jnp.sqrt(v_hat) + ADAM_EPS) + ADAM_WD * w_ref[...])
        mo_ref[...] = m2
        vo_ref[...] = v2

    spec = pl.BlockSpec((None, tb, n), lambda i: (l, i, 0))
    gspec = pl.BlockSpec((nc, None, tb, n), lambda i: (0, 0, i, 0))
    in_specs, args, aliases = [spec, spec, spec, gspec], [w, m, v, gc], {}
    if prev is not None:
        in_specs += [pl.BlockSpec(memory_space=pl.ANY)] * 4
        args += list(prev)
        aliases = {4 + j: j for j in range(4)}
    return _call(name, body, (k // tb,), in_specs, [spec] * 4, [_sds(w.shape)] * 4, aliases=aliases)(*args)


def _sum8(name, g8):
    r = g8.shape[1]

    def body(g_ref, o_ref):
        acc = g_ref[0]
        for d in range(1, 8):
            acc = acc + g_ref[d]
        o_ref[...] = acc

    return _call(name, body, (1,), [pl.BlockSpec((8, r, 128), lambda i: (0, 0, 0))],
                 pl.BlockSpec((r, 128), lambda i: (0, 0)), _sds((r, 128)))(g8)


def _place():
    x, y, c = lax.axis_index("x"), lax.axis_index("y"), lax.axis_index("c")
    return x, y, c, 2 * x + y, (x, y, 1 - c), [(1 - x, y), (x, 1 - y), (1 - x, 1 - y)]


class _Exchange:
    def __init__(self, arrays, out_shapes):
        self.arrays, self.out_shapes = list(arrays), list(out_shapes)
        n = len(self.arrays)
        self.sems = [pltpu.SemaphoreType.DMA((7 * n,)), pltpu.SemaphoreType.DMA((7 * n,)),
                     pltpu.SemaphoreType.DMA((n,))]

    def _copies(self, ins, outs, sems):
        send, recv, lsem = sems
        local_src, remote_src, dst = self.maps(ins, outs)
        x, y, c, q, sib, chips = _place()

        def rcopy(w, k, qq, cc, to, src=None):
            return pltpu.make_async_remote_copy(
                src_ref=dst(w, qq, cc) if src is None else src, dst_ref=dst(w, qq, cc),
                send_sem=send.at[7 * w + k], recv_sem=recv.at[7 * w + k], device_id=to, device_id_type=MESH)

        def mine(w):
            return pltpu.make_async_copy(local_src(w), dst(w, q, c), lsem.at[w])

        def first(w):
            return [rcopy(w, 0, q, c, sib, local_src(w))] + [
                rcopy(w, 1 + j, q, c, (cx, cy, c), remote_src(w, 2 * cx + cy)) for j, (cx, cy) in enumerate(chips)]

        return rcopy, mine, first, (x, y, c), q, c, sib, chips

    def start(self, ins, outs, sems):
        _, mine, first, *_ = self._copies(ins, outs, sems)
        for w in range(len(self.arrays)):
            mine(w).start()
            for cp in first(w):
                cp.start()

    def forward_steps(self, n_steps):
        sizes = [a.size // a.shape[0] for a in self.arrays]
        plan, moved = {}, 0
        for w, size in enumerate(sizes):
            moved += size
            plan.setdefault(min(n_steps - 1, -(-moved * n_steps // sum(sizes))), []).append(w)
        return plan

    def forward(self, ws, ins, outs, sems):
        rcopy, _, _, me, _, c, sib, chips = self._copies(ins, outs, sems)
        for w in ws:
            for j, (cx, cy) in enumerate(chips):
                rcopy(w, 1 + j, 2 * cx + cy, c, me).wait_recv()
                rcopy(w, 4 + j, 2 * cx + cy, c, sib).start()

    def complete(self, ins, outs, sems):
        rcopy, mine, first, me, q, c, sib, chips = self._copies(ins, outs, sems)
        n = len(self.arrays)
        for w in range(n):
            rcopy(w, 0, q, 1 - c, me).wait_recv()
            for j, (cx, cy) in enumerate(chips):
                rcopy(w, 4 + j, 2 * cx + cy, 1 - c, me).wait_recv()
        for w in range(n):
            for cp in first(w):
                cp.wait_send()
            for j, (cx, cy) in enumerate(chips):
                rcopy(w, 4 + j, 2 * cx + cy, c, sib).wait_send()
            mine(w).wait()


class _GatherWeights(_Exchange):
    def __init__(self, items):
        self.layers = [l for _, l in items]
        self.kh = [s.shape[1] // 2 for s, _ in items]
        super().__init__([s for s, _ in items], [_sds((NQ, 1) + s.shape[1:], BF16) for s, _ in items])

    def maps(self, ins, outs):
        c = lax.axis_index("c")
        src = lambda w: ins[w].at[pl.ds(self.layers[w], 1), pl.ds(c * self.kh[w], self.kh[w]), :]
        return src, lambda w, q: src(w), lambda w, q, cc: outs[w].at[q, :, pl.ds(cc * self.kh[w], self.kh[w]), :]


class _ScatterPartials(_Exchange):
    def __init__(self, parts):
        super().__init__(parts, [_sds((NQ, 1, 2) + p.shape[2:], BF16) for p in parts])

    def maps(self, ins, outs):
        q = 2 * lax.axis_index("x") + lax.axis_index("y")
        return (lambda w: ins[w].at[:, q]), (lambda w, qq: ins[w].at[:, qq]), (lambda w, qq, cc: outs[w].at[qq, :, cc])


class _Gather8(_Exchange):
    def __init__(self, v):
        super().__init__([v], [_sds((8,) + v.shape)])

    def maps(self, ins, outs):
        return (lambda w: ins[0]), (lambda w, q: ins[0]), (lambda w, q, cc: outs[0].at[2 * q + cc])


class _SwapHalves:
    def __init__(self, dws):
        self.arrays = list(dws)
        self.kh = [d.shape[2] // 2 for d in dws]
        self.out_shapes = [_sds(d.shape[:2] + (kh,) + d.shape[3:]) for d, kh in zip(dws, self.kh)]
        self.sems = [pltpu.SemaphoreType.DMA((len(dws),)), pltpu.SemaphoreType.DMA((len(dws),))]

    def _copies(self, ins, outs, sems):
        send, recv = sems
        _, _, c, _, sib, _ = _place()
        return [pltpu.make_async_remote_copy(
            src_ref=ins[w].at[:, :, pl.ds((1 - c) * self.kh[w], self.kh[w]), :], dst_ref=outs[w],
            send_sem=send.at[w], recv_sem=recv.at[w], device_id=sib, device_id_type=MESH)
            for w in range(len(self.arrays))]

    def start(self, ins, outs, sems):
        for cp in self._copies(ins, outs, sems):
            cp.start()

    def forward_steps(self, n_steps):
        return {}

    def complete(self, ins, outs, sems):
        for cp in self._copies(ins, outs, sems):
            cp.wait()


def _comm_only(name, host):
    n_in, n_out = len(host.arrays), len(host.out_shapes)

    def body(*refs):
        ins, outs, sems = refs[:n_in], refs[n_in:n_in + n_out], refs[n_in + n_out:]
        host.start(ins, outs, sems)
        for ws in host.forward_steps(1).values():
            host.forward(ws, ins, outs, sems)
        host.complete(ins, outs, sems)

    any_spec = pl.BlockSpec(memory_space=pl.ANY)
    return pl.pallas_call(body, name=name, in_specs=[any_spec] * n_in, out_specs=[any_spec] * n_out,
                          out_shape=host.out_shapes, scratch_shapes=host.sems)(*host.arrays)


def _add_halves(dw, got, cidx):
    nl, _, k, n = dw.shape
    kh = k // 2
    qb = 2

    def body(c_ref, a_ref, b_ref, o_ref):
        o_ref[...] = (a_ref[...] + b_ref[...]).astype(BF16)

    grid_spec = pltpu.PrefetchScalarGridSpec(
        num_scalar_prefetch=1, grid=(nl, NQ // qb),
        in_specs=[pl.BlockSpec((None, qb, None, kh, n), lambda l, q, c_ref: (l, q, c_ref[0], 0, 0)),
                  pl.BlockSpec((None, qb, kh, n), lambda l, q, c_ref: (l, q, 0, 0))],
        out_specs=pl.BlockSpec((None, qb, kh, n), lambda l, q, c_ref: (l, q, 0, 0)))
    return pl.pallas_call(
        body, name="add_halves", grid_spec=grid_spec, out_shape=_sds((nl, NQ, kh, n), BF16),
        compiler_params=pltpu.CompilerParams(dimension_semantics=("arbitrary", "arbitrary"),
                                             vmem_limit_bytes=VMEM_LIMIT))(cidx, dw.reshape(nl, NQ, 2, kh, n), got)


def _gather8(name, v):
    return _comm_only(name, _Gather8(v))[0]


PACK = 16 * 128


def _pack(arrays):
    parts = []
    for a in arrays:
        flat = a.reshape(-1)
        parts.append(jnp.pad(flat, (0, (-flat.shape[0]) % PACK)))
    return jnp.concatenate(parts).reshape(-1, 128)


def _unpack(packed, shapes):
    flat = packed.reshape(-1)
    out, off = [], 0
    for shp in shapes:
        size = 1
        for d in shp:
            size *= d
        out.append(flat[off:off + size].reshape(shp))
        off += size + (-size) % PACK
    return out


def kernel(x, p, a_w_pw1, a_b_pw1, a_w_dw, a_b_dw, a_ln_g, a_ln_b, a_w_pw2, b_w_in, b_b_in, b_ln_g, b_ln_b, b_w_s, b_b_s, b_w_out, c_w_in, c_w_conv, c_w_out, ln1_g, ln1_b, ln2_g, ln2_b, ffn_w_gate, ffn_w_up, ffn_w_down, ple_w_gate, ple_w_proj, ple_norm_g, loss_target, m_a_w_pw1, m_a_b_pw1, m_a_w_dw, m_a_b_dw, m_a_ln_g, m_a_ln_b, m_a_w_pw2, m_b_w_in, m_b_b_in, m_b_ln_g, m_b_ln_b, m_b_w_s, m_b_b_s, m_b_w_out, m_c_w_in, m_c_w_conv, m_c_w_out, m_ln1_g, m_ln1_b, m_ln2_g, m_ln2_b, m_ffn_w_gate, m_ffn_w_up, m_ffn_w_down, m_ple_w_gate, m_ple_w_proj, m_ple_norm_g, v_a_w_pw1, v_a_b_pw1, v_a_w_dw, v_a_b_dw, v_a_ln_g, v_a_ln_b, v_a_w_pw2, v_b_w_in, v_b_b_in, v_b_ln_g, v_b_ln_b, v_b_w_s, v_b_b_s, v_b_w_out, v_c_w_in, v_c_w_conv, v_c_w_out, v_ln1_g, v_ln1_b, v_ln2_g, v_ln2_b, v_ffn_w_gate, v_ffn_w_up, v_ffn_w_down, v_ple_w_gate, v_ple_w_proj, v_ple_norm_g):
    args = dict(locals())
    wts = {k: args[k] for k in WEIGHTS}
    mom = {k: args["m_" + k] for k in WEIGHTS}
    var = {k: args["v_" + k] for k in WEIGHTS}
    for k in TRANSPOSED:
        wts[k], mom[k], var[k] = (jnp.transpose(t[k], (0, 2, 1)) for t in (wts, mom, var))
    q_idx = 2 * lax.axis_index("x") + lax.axis_index("y")
    c_idx = lax.axis_index("c").astype(jnp.int32).reshape(1)

    wb = {k: _cast_bf16(wts[k]) for k in BIG if k not in ("ffn_w_gate", "ffn_w_up")}
    mixw = [[("a_w_pw1", 0), ("a_w_pw2", 0)], [("b_w_in", 0), ("b_w_out", 0)], [("c_w_in", 0), ("c_w_out", 0)],
            [("a_w_pw1", 1), ("a_w_pw2", 1)]]
    ffnw = [[("ffn_w_gate", l), ("ffn_w_up", l), ("ffn_w_down", l)] for l in range(DEPTH)]
    plew = [[("ple_w_gate", l), ("ple_w_proj", l)] for l in range(DEPTH)]
    fwd_plan = {("a1", 0): mixw[0][1:] + plew[0], ("a2", 0): ffnw[0], ("ffn", 0): mixw[1], ("ple", 0): plew[1],
                ("b", 1): ffnw[1], ("ffn", 1): mixw[2] + ffnw[2][:1], ("ple", 1): plew[2],
                ("c1", 2): ffnw[2][1:2], ("c2", 2): ffnw[2][2:], ("ffn", 2): mixw[3] + ffnw[3][:1], ("ple", 2): plew[3],
                ("a1", 3): ffnw[3][1:2], ("a2", 3): ffnw[3][2:]}
    gw = {}

    def gather(keys):
        return _GatherWeights([(wb[name], l) for name, l in keys])

    def hosted(tag, fn, *fargs):
        keys = fwd_plan.get(tag)
        if not keys:
            return fn(*fargs)
        own, (got,) = fn(*fargs, hosts=[gather(keys)])
        store(keys, got)
        return own

    def store(keys, got):
        for (name, l), arr in zip(keys, got):
            gw[name, l] = arr.reshape(NQ * arr.shape[2], arr.shape[3]) if name in ROW_SHARDED else arr

    first_keys = mixw[0][:1]
    wb["ffn_w_gate"], (got,) = _cast_bf16(wts["ffn_w_gate"], hosts=[gather(first_keys)])
    store(first_keys, got)
    shard_shapes = [wts[k].shape for k in SMALL_SHARDED]
    wb["ffn_w_up"], ((small8,),) = _cast_bf16(wts["ffn_w_up"], hosts=[_Gather8(_pack([wts[k] for k in SMALL_SHARDED]))])
    per_chip = [_unpack(small8[2 * qq], shard_shapes) for qq in range(NQ)]
    full = {k: jnp.concatenate([per_chip[qq][i] for qq in range(NQ)], axis=-1) for i, k in enumerate(SMALL_SHARDED)}
    for k in SMALL_REPL:
        full[k] = wts[k]

    def vec(name, l):
        return full[name][l][None, :]

    def conv_w(name, l, rows):
        w = full[name][l]
        return jnp.pad(w, ((0, rows - w.shape[0]), (0, 0)))

    ws = full["b_w_s"][0]
    wst = jnp.transpose(ws, (0, 2, 1))
    bsx = jnp.broadcast_to(full["b_b_s"][0][:, :, None], (SGU_H, SGU_T, SGU_G))

    x0s, z1s, z2s, saved, ffn_saved = [], [], [], [], []
    cur = x[0]
    for i in range(DEPTH):
        mix, j = i % 3, i // 3
        x0s.append(cur)
        if mix == 0:
            h, glu = hosted(("a1", i), _fwd_a1, cur, gw["a_w_pw1", j], vec("a_b_pw1", j), i)
            z1, cv = hosted(("a2", i), _fwd_a2, glu, cur, conv_w("a_w_dw", j, 32), vec("a_b_dw", j), vec("a_ln_g", j),
                            vec("a_ln_b", j), gw["a_w_pw2", j], i)
            saved.append((h, glu, cv))
        elif mix == 1:
            z1, zg, gg = hosted(("b", i), _fwd_b, cur, gw["b_w_in", 0], vec("b_b_in", 0), vec("b_ln_g", 0),
                                vec("b_ln_b", 0), ws, bsx, gw["b_w_out", 0])
            saved.append((zg, gg))
        else:
            hc = hosted(("c1", i), _fwd_c1, cur, gw["c_w_in", 0])
            z1 = hosted(("c2", i), _fwd_c2, hc, cur, conv_w("c_w_conv", 0, 8), gw["c_w_out", 0])
            saved.append((hc,))
        z2, ab, ub, hm = hosted(("ffn", i), _fwd_ffn, z1, vec("ln1_g", i), vec("ln1_b", i), gw["ffn_w_gate", i],
                                gw["ffn_w_up", i], gw["ffn_w_down", i], i)
        ffn_saved.append((ab, ub, hm))
        cur = hosted(("ple", i), _fwd_ple, z2, p[i, 0], vec("ln2_g", i), vec("ln2_b", i), gw["ple_w_gate", i],
                     gw["ple_w_proj", i], vec("ple_norm_g", i), i)
        z1s.append(z1)
        z2s.append(z2)

    g, loss_acc = _loss_head(cur, loss_target[0])
    loss = lax.psum(0.5 / D * jnp.sum(loss_acc[0]), ("x", "y", "c"))

    dws = {}
    sg = {}
    res = {k: None for k in BIG}

    def wgrad(name, l, a, amode, b, bmode, scatter_keys=()):
        _, k, n = wts[name].shape
        hosts = [_ScatterPartials([parts[key] for key in scatter_keys])] if scatter_keys else ()
        if name in ROW_SHARDED:
            out = _mm_tn(f"dw_{name}_{l}", a, "1", b, "1", NQ * k, n, groups=1, hosts=hosts)
        else:
            out = _mm_tn(f"dw_{name}_{l}", a, amode, b, bmode, k, n, hosts=hosts)
        if scatter_keys:
            out, (contribs,) = out
            update(scatter_keys, contribs)
        dws[name, l] = out.reshape(1, NQ, k, n)

    def swap(keys):
        return _SwapHalves([dws[k] for k in keys])

    parts = {}

    def add_halves(keys, got):
        parts.update((k, _add_halves(dws[k], r, c_idx)) for k, r in zip(keys, got))

    def update(keys, contribs):
        for (name, l), gc in zip(keys, contribs):
            _, kq, n = wts[name].shape
            res[name] = _adam(f"adam_{name}_{l}", wts[name], mom[name], var[name], gc.reshape(NQ, 1, kq, n), l,
                              res[name])

    small = SMALL_SHARDED + SMALL_REPL
    late_small = [("a_w_dw", 0), ("a_b_pw1", 0)]
    early_small = [(k, l) for k in small for l in range(full[k].shape[0]) if (k, l) not in late_small]
    pending = None
    for i in reversed(range(DEPTH)):
        mix, j = i % 3, i // 3
        ple_args = (g, z2s[i], p[i, 0], vec("ln2_g", i), vec("ln2_b", i), gw["ple_w_gate", i], gw["ple_w_proj", i],
                    vec("ple_norm_g", i), i)
        if pending:
            (dz2, x2b, dgp, dqp, acc), (got,) = _bwd_ple(*ple_args, hosts=[swap(pending)])
            add_halves(pending, got)
        else:
            dz2, x2b, dgp, dqp, acc = _bwd_ple(*ple_args)
        sg["ple_norm_g", i], sg["ln2_g", i], sg["ln2_b", i] = acc[0], acc[1], acc[2]
        wgrad("ple_w_gate", i, x2b, "c", dgp, "1")
        wgrad("ple_w_proj", i, p[i, 0], "1", dqp, "c")
        ab, ub, hm = ffn_saved[i]
        ffn_args = (dz2, z1s[i], ab, ub, vec("ln1_g", i), vec("ln1_b", i), gw["ffn_w_gate", i], gw["ffn_w_up", i],
                    gw["ffn_w_down", i], i)
        if pending:
            (dz1, x1b, da, du, acc), (contribs,) = _bwd_ffn(
                *ffn_args, hosts=[_ScatterPartials([parts[key] for key in ffnw[i + 1]])])
            update(ffnw[i + 1], contribs)
        else:
            dz1, x1b, da, du, acc = _bwd_ffn(*ffn_args)
        sg["ln1_g", i], sg["ln1_b", i] = acc[0], acc[1]
        wgrad("ffn_w_gate", i, da, "1", x1b, "1", scatter_keys=mixw[i + 1][:1] if pending else ())
        wgrad("ffn_w_up", i, du, "1", x1b, "1", scatter_keys=mixw[i + 1][1:] + plew[i + 1] if pending else ())
        wgrad("ffn_w_down", i, hm, "1", dz2, "1")
        x0 = x0s[i]
        if mix == 0:
            h, glu, cv = saved[i]
            a2_args = (dz1, cv, vec("a_ln_g", j), vec("a_ln_b", j), gw["a_w_pw2", j], i)
            conv_args = (glu, conv_w("a_w_dw", j, 32), i)
            if i == 0:
                early = ffnw[0] + plew[0]
                (dcv, sb, acc), (got,) = _bwd_a2(*a2_args, hosts=[swap(early)])
                sg["a_ln_g", j], sg["a_ln_b", j], sg["a_b_dw", j] = acc[0], acc[1], acc[2]
                add_halves(early, got)
                wgrad("a_w_pw2", j, sb, "c", dz1, "1")
                (dglu, dwdw), (contribs, (g8_early,), got) = _bwd_conv_a(
                    dcv, *conv_args, hosts=[_ScatterPartials([parts[key] for key in early]),
                                            _Gather8(_pack([sg[pc] for pc in early_small])), swap(mixw[0][1:])])
                update(early, contribs)
                add_halves(mixw[0][1:], got)
                (g, dh, acc), (contribs,) = _bwd_a1(dglu, h, dz1, gw["a_w_pw1", j], i,
                                                    hosts=[_ScatterPartials([parts[key] for key in mixw[0][1:]])])
                update(mixw[0][1:], contribs)
            else:
                dcv, sb, acc = _bwd_a2(*a2_args)
                sg["a_ln_g", j], sg["a_ln_b", j], sg["a_b_dw", j] = acc[0], acc[1], acc[2]
                dglu, dwdw = _bwd_conv_a(dcv, *conv_args)
                wgrad("a_w_pw2", j, sb, "c", dz1, "1")
                g, dh, acc = _bwd_a1(dglu, h, dz1, gw["a_w_pw1", j], i)
            sg["a_w_dw", j] = dwdw[:CONV_A]
            sg["a_b_pw1", j] = acc[0]
            wgrad("a_w_pw1", j, x0, "1", dh, "c")
        elif mix == 1:
            zg, gg = saved[i]
            g, dh, mb, acc, dw_s, db_s = _bwd_b(dz1, zg, gg, vec("b_ln_g", 0), vec("b_ln_b", 0), gw["b_w_in", 0],
                                                gw["b_w_out", 0], ws, wst, bsx)
            sg["b_b_in", 0], sg["b_ln_g", 0], sg["b_ln_b", 0] = acc[0], acc[1, :E], acc[2, :E]
            sg["b_w_s", 0], sg["b_b_s", 0] = dw_s, jnp.sum(db_s, axis=-1)
            wgrad("b_w_out", 0, mb, "c", dz1, "1")
            wgrad("b_w_in", 0, x0, "1", dh, "c")
        else:
            (hc,) = saved[i]
            wc = conv_w("c_w_conv", 0, 8)
            dy, dbg, mb = _bwd_c2(dz1, hc, wc, gw["c_w_out", 0])
            wgrad("c_w_out", 0, mb, "c", dz1, "1")
            g, dhc, dwc = _bwd_c1(dy, hc, dbg, dz1, wc, gw["c_w_in", 0])
            sg["c_w_conv", 0] = dwc[:CONV_C]
            wgrad("c_w_in", 0, x0, "1", dhc, "c")
        pending = mixw[i] + ffnw[i] + plew[i] if i > 0 else mixw[0][:1]
    grad_x = g[None]
    add_halves(pending, _comm_only("swap_last", swap(pending)))
    update(pending, _comm_only("scatter_last", _ScatterPartials([parts[key] for key in pending])))

    g8_late = _gather8("gather_small_late", _pack([sg[pc] for pc in late_small]))
    sums = dict(zip(early_small, _unpack(_sum8("sum8_early", g8_early), [sg[pc].shape for pc in early_small])))
    sums.update(zip(late_small, _unpack(_sum8("sum8_late", g8_late), [sg[pc].shape for pc in late_small])))
    gsum = [jnp.stack([sums[k, l] for l in range(full[k].shape[0])]) for k in small]
    gmine = []
    for k, gs in zip(small, gsum):
        if k in SMALL_SHARDED:
            wdt = wts[k].shape[-1]
            gs = lax.dynamic_slice_in_dim(gs, q_idx * wdt, wdt, axis=gs.ndim - 1)
        gmine.append(gs)
    packed = [_pack(t)[None] for t in ([wts[k] for k in small], [mom[k] for k in small], [var[k] for k in small])]
    outs = _adam("adam_small", packed[0], packed[1], packed[2], _pack(gmine)[None, None], 0, None)
    unpacked = [_unpack(o[0], [wts[k].shape for k in small]) for o in outs]
    for i, k in enumerate(small):
        res[k] = tuple(u[i] for u in unpacked)

    for k in TRANSPOSED:
        res[k] = tuple(jnp.transpose(r, (0, 2, 1)) for r in res[k])
    return (loss, grad_x, *[res[k][0] for k in WEIGHTS], *[res[k][1] for k in WEIGHTS],
            *[res[k][2] for k in WEIGHTS], *[res[k][3] for k in WEIGHTS])
```

```python
import functools

import jax
import jax.numpy as jnp
from jax import lax
from jax.experimental import pallas as pl
from jax.experimental.pallas import tpu as pltpu

F32, BF16 = jnp.float32, jnp.bfloat16
S = 4096
D = 1024
E = 2048
FF = 2816
FQ = FF // 4
NQ = 4
DEPTH = 4
ALPHA = (2 * DEPTH) ** 0.25
LN_EPS = 1e-5
CONV_A, CONV_C = 31, 3
HALO_A, HALO_C = 32, 8
SGU_T, SGU_H, SGU_G, SGU_CHUNK = 128, 8, 256, 64
VMEM_LIMIT = 56 * 1024 * 1024
DW_BLOCK_BUDGET = 40 * 1024 * 1024
MESH = pl.DeviceIdType.MESH
ADAM_LR, ADAM_B1, ADAM_B2, ADAM_EPS, ADAM_WD, ADAM_STEP = 0.001, 0.9, 0.999, 1e-08, 0.01, 10
GELU_C, GELU_A = 0.7978845608028654, 0.044715

BIG = ["a_w_pw1", "a_w_pw2", "b_w_in", "b_w_out", "c_w_in", "c_w_out",
       "ffn_w_gate", "ffn_w_up", "ffn_w_down", "ple_w_gate", "ple_w_proj"]
TRANSPOSED = ["ffn_w_gate", "ffn_w_up"]
ROW_SHARDED = ["a_w_pw2", "b_w_out", "c_w_out", "ffn_w_gate", "ffn_w_up", "ffn_w_down", "ple_w_gate"]
SMALL_SHARDED = ["a_b_pw1", "a_w_dw", "a_b_dw", "a_ln_g", "a_ln_b", "c_w_conv"]
SMALL_REPL = ["b_b_in", "b_ln_g", "b_ln_b", "b_w_s", "b_b_s", "ln1_g", "ln1_b", "ln2_g", "ln2_b", "ple_norm_g"]
WEIGHTS = ["a_w_pw1", "a_b_pw1", "a_w_dw", "a_b_dw", "a_ln_g", "a_ln_b", "a_w_pw2", "b_w_in", "b_b_in", "b_ln_g",
           "b_ln_b", "b_w_s", "b_b_s", "b_w_out", "c_w_in", "c_w_conv", "c_w_out", "ln1_g", "ln1_b", "ln2_g",
           "ln2_b", "ffn_w_gate", "ffn_w_up", "ffn_w_down", "ple_w_gate", "ple_w_proj", "ple_norm_g"]


def _call(name, body, grid, in_specs, out_specs, out_shape, scratch=(), aliases=None, hosts=()):
    params = pltpu.CompilerParams(dimension_semantics=("arbitrary",) * len(grid), vmem_limit_bytes=VMEM_LIMIT)
    if not hosts:
        return pl.pallas_call(
            body, name=name, grid=grid, in_specs=in_specs, out_specs=out_specs, out_shape=out_shape,
            scratch_shapes=list(scratch), input_output_aliases=aliases or {}, compiler_params=params)
    assert len(grid) == 1 and not aliases
    single = not isinstance(out_shape, (list, tuple))
    own_shapes = [out_shape] if single else list(out_shape)
    own_specs = [out_specs] if single else list(out_specs)
    n_in, n_out, n_scr = len(in_specs), len(own_shapes), len(scratch)
    h_in = [len(h.arrays) for h in hosts]
    h_out = [len(h.out_shapes) for h in hosts]
    h_sem = [len(h.sems) for h in hosts]

    def split(refs, counts):
        out, off = [], 0
        for cnt in counts:
            out.append(refs[off:off + cnt])
            off += cnt
        return out

    def wrapped(*refs):
        ins, hin, outs, hout, scr, hsem = split(refs, [n_in, sum(h_in), n_out, sum(h_out), n_scr, sum(h_sem)])
        per_host = list(zip(hosts, split(hin, h_in), split(hout, h_out), split(hsem, h_sem)))

        @pl.when(pl.program_id(0) == 0)
        def _():
            for h, a, o, s in per_host:
                h.start(a, o, s)

        body(*ins, *outs, *scr)

        for h, a, o, s in per_host:
            for step, ws in sorted(h.forward_steps(grid[0]).items()):
                pl.when(pl.program_id(0) == step)(functools.partial(h.forward, ws, a, o, s))

        @pl.when(pl.program_id(0) == grid[0] - 1)
        def _():
            for h, a, o, s in per_host:
                h.complete(a, o, s)

    any_spec = pl.BlockSpec(memory_space=pl.ANY)
    call = pl.pallas_call(
        wrapped, name=name, grid=grid, in_specs=list(in_specs) + [any_spec] * sum(h_in),
        out_specs=own_specs + [any_spec] * sum(h_out),
        out_shape=own_shapes + [s for h in hosts for s in h.out_shapes],
        scratch_shapes=list(scratch) + [s for h in hosts for s in h.sems], compiler_params=params)

    def run(*args):
        res = call(*args, *[a for h in hosts for a in h.arrays])
        own = res[0] if single else list(res[:n_out])
        return own, split(list(res[n_out:]), h_out)

    return run


def _sds(shape, dtype=F32):
    return jax.ShapeDtypeStruct(shape, dtype)


def _row(tm, c):
    return pl.BlockSpec((tm, c), lambda i: (i, 0))


def _grow(g, tm, c):
    return pl.BlockSpec((g, tm, c), lambda i: (0, i, 0))


def _const(shape):
    nd = len(shape)
    return pl.BlockSpec(shape, lambda i: (0,) * nd, pipeline_mode=pl.Buffered(1))


def _wspec(w):
    return pl.BlockSpec((NQ, None, w.shape[2], w.shape[3]), lambda i: (0, 0, 0, 0), pipeline_mode=pl.Buffered(1))


def _prev(tm, hb, c):
    return pl.BlockSpec((hb, c), lambda i: (jnp.maximum(i * (tm // hb) - 1, 0), 0))


def _next(tm, hb, c):
    return pl.BlockSpec((hb, c), lambda i: (jnp.minimum((i + 1) * (tm // hb), S // hb - 1), 0))


def _acc(r, c):
    return pl.BlockSpec((r, c), lambda i: (0, 0))


def _sig(x):
    return 1.0 / (1.0 + jnp.exp(-x))


def _ln(z, g, b):
    mu = jnp.mean(z, axis=-1, keepdims=True)
    zc = z - mu
    rstd = lax.rsqrt(jnp.mean(zc * zc, axis=-1, keepdims=True) + LN_EPS)
    xhat = zc * rstd
    return xhat * g + b, xhat, rstd


def _ln_bwd(dyg, xhat, rstd):
    return rstd * (dyg - jnp.mean(dyg, axis=-1, keepdims=True) - xhat * jnp.mean(dyg * xhat, axis=-1, keepdims=True))


def _mm(a, w):
    return jnp.dot(a.astype(BF16), w, preferred_element_type=F32)


def _mmt(a, w):
    return lax.dot_general(a.astype(BF16), w, (((1,), (1,)), ((), ())), preferred_element_type=F32)


def _colsum(x):
    return jnp.sum(x, axis=0, keepdims=True)


def _gelu(x):
    t = jnp.tanh(GELU_C * (x + GELU_A * x * x * x))
    return 0.5 * x * (1.0 + t), t


def _gelu_grad(x, t):
    return 0.5 * (1.0 + t) + 0.5 * x * (1.0 - t * t) * GELU_C * (1.0 + 3.0 * GELU_A * x * x)


def _silu_grad(a, sg):
    return sg * (1.0 + a * (1.0 - sg))


def _sgu_masks():
    r = lax.broadcasted_iota(jnp.int32, (SGU_T, SGU_T), 0) // SGU_CHUNK
    c = lax.broadcasted_iota(jnp.int32, (SGU_T, SGU_T), 1) // SGU_CHUNK
    return r >= c, c >= r


def _fill_halo(buf, lo, n, halo_val_fn, is_edge):
    @pl.when(is_edge)
    def _():
        buf[lo:lo + n, :] = jnp.zeros((n, buf.shape[1]), F32)

    @pl.when(jnp.logical_not(is_edge))
    def _():
        buf[lo:lo + n, :] = halo_val_fn()


SUB, LANE = 8, 128
ROWS_AT_ONCE = 16


def _shift_copies(buf, sh):
    rows = sh.shape[1]
    for s in range(1, SUB):
        sh[s - 1, :, :] = buf[pl.ds(s, rows), :]


def _tiles(buf, sh, s, first, count, group0, lanes):
    src = buf if s == 0 else sh.at[s - 1]
    return {t: src[pl.ds(pl.multiple_of((group0 + t) * SUB, SUB), SUB), lanes] for t in range(first, first + count)}


def _by_shift(offsets):
    out = []
    for s in range(SUB):
        taps = [(k, o // SUB) for k, o in enumerate(offsets) if o % SUB == s]
        if taps:
            out.append((s, taps))
    return out


def _conv_rows(out_ref, w_ref, bias_ref, offsets, buf, sh, tm):
    n = ROWS_AT_ONCE
    for cb in range(D // LANE):
        lanes = slice(cb * LANE, (cb + 1) * LANE)
        bias = None if bias_ref is None else jnp.broadcast_to(bias_ref[:, lanes], (SUB, LANE))

        def body(jb, carry):
            accs = [bias] * n
            for s, taps in _by_shift(offsets):
                ms = [m for _, m in taps]
                tiles = _tiles(buf, sh, s, min(ms), max(ms) - min(ms) + n, jb * n, lanes)
                for k, m in taps:
                    wk = jnp.broadcast_to(w_ref[k:k + 1, lanes], (SUB, LANE))
                    for jj in range(n):
                        t = wk * tiles[m + jj]
                        accs[jj] = t if accs[jj] is None else accs[jj] + t
            for jj in range(n):
                out_ref[pl.ds(pl.multiple_of((jb * n + jj) * SUB, SUB), SUB), lanes] = accs[jj]
            return carry

        lax.fori_loop(0, tm // (SUB * n), body, 0)


def _conv_wgrad(dw_ref, d_ref, offsets, buf, sh, tm):
    n = 4
    for cb in range(D // LANE):
        lanes = slice(cb * LANE, (cb + 1) * LANE)

        def body(jq, accs):
            accs = list(accs)
            d = [d_ref[pl.ds(pl.multiple_of((jq * n + jj) * SUB, SUB), SUB), lanes] for jj in range(n)]
            for s, taps in _by_shift(offsets):
                ms = [m for _, m in taps]
                tiles = _tiles(buf, sh, s, min(ms), max(ms) - min(ms) + n, jq * n, lanes)
                for k, m in taps:
                    for jj in range(n):
                        accs[k] = accs[k] + d[jj] * tiles[m + jj]
            return tuple(accs)

        accs = lax.fori_loop(0, tm // (SUB * n), body, tuple(jnp.zeros((SUB, LANE), F32) for _ in offsets))
        for k, acc in enumerate(accs):
            dw_ref[k:k + 1, lanes] += jnp.sum(acc, axis=0, keepdims=True)


def _fwd_a1(x0, w1, b1, l, hosts=()):
    tm = 512

    def body(x_ref, w_ref, b_ref, h_ref, glu_ref):
        xb = x_ref[...].astype(BF16)
        for q in range(NQ):
            sl = slice(q * 512, (q + 1) * 512)
            h_ref[:, sl] = jnp.dot(xb, w_ref[q], preferred_element_type=F32) + b_ref[:, sl]
        glu_ref[...] = h_ref[:, :D] * _sig(h_ref[:, D:])

    return _call(f"fwd_a1_{l}", body, (S // tm,), [_row(tm, D), _wspec(w1), _const((1, 2 * D))],
                 [_row(tm, 2 * D), _row(tm, D)], [_sds((S, 2 * D)), _sds((S, D))], hosts=hosts)(x0, w1, b1)


def _fwd_a2(glu, x0, wdw, bdw, lg, lb, w2, l, hosts=()):
    tm = 256

    def body(g_ref, gp_ref, x_ref, wdw_ref, bdw_ref, lg_ref, lb_ref, w2_ref, z_ref, cv_ref, buf, sh):
        i = pl.program_id(0)
        _fill_halo(buf, 0, HALO_A, lambda: gp_ref[...], i == 0)
        buf[HALO_A:HALO_A + tm, :] = g_ref[...]
        _shift_copies(buf, sh)
        _conv_rows(cv_ref, wdw_ref, bdw_ref, [HALO_A - (CONV_A - 1) + k for k in range(CONV_A)], buf, sh, tm)
        n, _, _ = _ln(cv_ref[...], lg_ref[...], lb_ref[...])
        sb = (n * _sig(n)).astype(BF16)
        z_ref[...] = ALPHA * x_ref[...] + jnp.dot(sb, w2_ref[...], preferred_element_type=F32)

    return _call(f"fwd_a2_{l}", body, (S // tm,),
                 [_row(tm, D), _prev(tm, HALO_A, D), _row(tm, D), _const((32, D)), _const((1, D)), _const((1, D)),
                  _const((1, D)), _const(w2.shape)],
                 [_row(tm, D), _row(tm, D)], [_sds((S, D)), _sds((S, D))],
                 scratch=[pltpu.VMEM((HALO_A + tm, D), F32), pltpu.VMEM((SUB - 1, HALO_A + tm - SUB, D), F32)],
                 hosts=hosts)(glu, glu, x0, wdw, bdw, lg, lb, w2)


def _fwd_b(x0, win, b_in, lg, lb, ws, bsx, wout, hosts=()):
    tm = 256

    def body(x_ref, win_ref, bin_ref, lg_ref, lb_ref, ws_ref, bsx_ref, wout_ref, z_ref, zg_ref, gg_ref, f_scr, h_ref):
        xb = x_ref[...].astype(BF16)
        for q in range(NQ):
            sl = slice(q * 1024, (q + 1) * 1024)
            h_ref[:, sl] = jnp.dot(xb, win_ref[q], preferred_element_type=F32) + bin_ref[:, sl]
        u, tu = _gelu(h_ref[:, :E])
        v, tv = _gelu(h_ref[:, E:])
        zg_ref[:, 0:E] = u.astype(BF16)
        zg_ref[:, E:2 * E] = v.astype(BF16)
        gg_ref[:, 0:E] = _gelu_grad(h_ref[:, :E], tu).astype(BF16)
        gg_ref[:, E:2 * E] = _gelu_grad(h_ref[:, E:], tv).astype(BF16)
        vn, _, _ = _ln(v, lg_ref[...], lb_ref[...])
        vnb = vn.astype(BF16)
        mask, _ = _sgu_masks()
        for hd in range(SGU_H):
            wm = jnp.where(mask, ws_ref[hd], 0.0).astype(BF16)
            cs = slice(hd * SGU_G, (hd + 1) * SGU_G)
            for n in range(tm // SGU_T):
                rs = slice(n * SGU_T, (n + 1) * SGU_T)
                f_scr[rs, cs] = jnp.dot(wm, vnb[rs, cs], preferred_element_type=F32) + bsx_ref[hd]
        mb = (u * f_scr[...]).astype(BF16)
        z_ref[...] = ALPHA * x_ref[...] + jnp.dot(mb, wout_ref[...], preferred_element_type=F32)

    return _call("fwd_b", body, (S // tm,),
                 [_row(tm, D), _wspec(win), _const((1, 2 * E)), _const((1, E)), _const((1, E)),
                  _const((SGU_H, SGU_T, SGU_T)), _const((SGU_H, SGU_T, SGU_G)), _const(wout.shape)],
                 [_row(tm, D), _row(tm, 2 * E), _row(tm, 2 * E)],
                 [_sds((S, D)), _sds((S, 2 * E), BF16), _sds((S, 2 * E), BF16)],
                 scratch=[pltpu.VMEM((tm, E), F32), pltpu.VMEM((tm, 2 * E), F32)], hosts=hosts
                 )(x0, win, b_in, lg, lb, ws, bsx, wout)


def _fwd_c1(x0, win, hosts=()):
    tm = 512

    def body(x_ref, w_ref, hc_ref):
        xb = x_ref[...].astype(BF16)
        for q in range(NQ):
            hc_ref[:, q * 768:(q + 1) * 768] = jnp.dot(xb, w_ref[q], preferred_element_type=F32)

    return _call("fwd_c1", body, (S // tm,), [_row(tm, D), _wspec(win)], _row(tm, 3 * D),
                 _sds((S, 3 * D)), hosts=hosts)(x0, win)


def _short_conv(buf, hc_ref, hcp_ref, wc_ref, tm, i):
    _fill_halo(buf, 0, HALO_C, lambda: hcp_ref[:, D:2 * D] * hcp_ref[:, 2 * D:], i == 0)
    buf[HALO_C:HALO_C + tm, :] = hc_ref[:, D:2 * D] * hc_ref[:, 2 * D:]
    y = wc_ref[0:1, :] * buf[pl.ds(HALO_C - 2, tm), :]
    for k in range(1, CONV_C):
        y = y + wc_ref[k:k + 1, :] * buf[pl.ds(HALO_C - 2 + k, tm), :]
    return y


def _fwd_c2(hc, x0, wc, wout, hosts=()):
    tm = 256

    def body(hc_ref, hcp_ref, x_ref, wc_ref, wout_ref, z_ref, buf):
        y = _short_conv(buf, hc_ref, hcp_ref, wc_ref, tm, pl.program_id(0))
        mb = (hc_ref[:, :D] * y).astype(BF16)
        z_ref[...] = ALPHA * x_ref[...] + jnp.dot(mb, wout_ref[...], preferred_element_type=F32)

    return _call("fwd_c2", body, (S // tm,),
                 [_row(tm, 3 * D), _prev(tm, HALO_C, 3 * D), _row(tm, D), _const((8, D)), _const(wout.shape)],
                 _row(tm, D), _sds((S, D)), scratch=[pltpu.VMEM((HALO_C + tm, D), F32)], hosts=hosts
                 )(hc, hc, x0, wc, wout)


def _fwd_ffn(z1, lg, lb, wgt, wut, wd, l, hosts=()):
    tm = 256

    def body(z_ref, lg_ref, lb_ref, wg_ref, wu_ref, wd_ref, o_ref, a_ref, u_ref, hm_ref):
        x1, _, _ = _ln(z_ref[...], lg_ref[...], lb_ref[...])
        xb = x1.astype(BF16)
        a = _mmt(xb, wg_ref[...])
        u = _mmt(xb, wu_ref[...])
        hmb = (a * _sig(a) * u).astype(BF16)
        a_ref[...] = a.astype(BF16)
        u_ref[...] = u.astype(BF16)
        hm_ref[...] = hmb
        o_ref[...] = ALPHA * x1 + jnp.dot(hmb, wd_ref[...], preferred_element_type=F32)

    return _call(f"fwd_ffn_{l}", body, (S // tm,),
                 [_row(tm, D), _const((1, D)), _const((1, D)), _const((FF, D)), _const((FF, D)), _const((FF, D))],
                 [_row(tm, D), _row(tm, FF), _row(tm, FF), _row(tm, FF)],
                 [_sds((S, D)), _sds((S, FF), BF16), _sds((S, FF), BF16), _sds((S, FF), BF16)],
                 hosts=hosts)(z1, lg, lb, wgt, wut, wd)


def _ple_parts(z2, p, lg, lb, wg_ref, wp_ref, pg):
    x2, xhat, rstd = _ln(z2, lg, lb)
    xb = x2.astype(BF16)
    gate = _sig(jnp.dot(xb, wg_ref[...], preferred_element_type=F32))
    pb = p.astype(BF16)
    qp = jnp.concatenate([jnp.dot(pb, wp_ref[q], preferred_element_type=F32) for q in range(NQ)], axis=1)
    rs = lax.rsqrt(jnp.mean(qp * qp, axis=-1, keepdims=True) + LN_EPS)
    qn = qp * rs
    return x2, xhat, rstd, xb, gate, qn, rs, qn * pg


def _fwd_ple(z2, p, lg, lb, wg, wp, pg, l, hosts=()):
    tm = 512

    def body(z_ref, p_ref, lg_ref, lb_ref, wg_ref, wp_ref, pg_ref, o_ref):
        x2, _, _, _, gate, _, _, r = _ple_parts(z_ref[...], p_ref[...], lg_ref[...], lb_ref[...], wg_ref, wp_ref,
                                                pg_ref[...])
        o_ref[...] = x2 + gate * r

    return _call(f"fwd_ple_{l}", body, (S // tm,),
                 [_row(tm, D), _row(tm, 256), _const((1, D)), _const((1, D)), _const(wg.shape), _wspec(wp),
                  _const((1, D))],
                 _row(tm, D), _sds((S, D)), hosts=hosts)(z2, p, lg, lb, wg, wp, pg)


def _loss_head(y, target):
    tm = 512

    def body(y_ref, t_ref, dy_ref, acc_ref):
        @pl.when(pl.program_id(0) == 0)
        def _():
            acc_ref[...] = jnp.zeros_like(acc_ref)

        e = y_ref[...] - t_ref[...]
        dy_ref[...] = e * (1.0 / D)
        acc_ref[0:1, :] += _colsum(e * e)

    return _call("loss_head", body, (S // tm,), [_row(tm, D), _row(tm, D)], [_row(tm, D), _acc(8, D)],
                 [_sds((S, D)), _sds((8, D))])(y, target)


def _zero_first(*refs):
    @pl.when(pl.program_id(0) == 0)
    def _():
        for r in refs:
            r[...] = jnp.zeros_like(r)


def _bwd_ple(g, z2, p, lg, lb, wg, wp, pg, l, hosts=()):
    tm = 256

    def body(g_ref, z_ref, p_ref, lg_ref, lb_ref, wg_ref, wp_ref, pg_ref, dz_ref, xb_ref, dgp_ref, dqp_ref, acc_ref):
        _zero_first(acc_ref)
        gin = g_ref[...]
        lgv, pgv = lg_ref[...], pg_ref[...]
        _, xhat, rstd, xb, gate, qn, rs, r = _ple_parts(z_ref[...], p_ref[...], lgv, lb_ref[...], wg_ref, wp_ref, pgv)
        xb_ref[...] = xb
        dgpb = (gin * r * gate * (1.0 - gate)).astype(BF16)
        dgp_ref[...] = dgpb
        dx2 = gin + _mmt(dgpb, wg_ref[...])
        dr = gin * gate
        acc_ref[0:1, :] += _colsum(dr * qn)
        t = dr * pgv
        dqp_ref[...] = (rs * (t - qn * jnp.mean(t * qn, axis=-1, keepdims=True))).astype(BF16)
        acc_ref[1:2, :] += _colsum(dx2 * xhat)
        acc_ref[2:3, :] += _colsum(dx2)
        dz_ref[...] = _ln_bwd(dx2 * lgv, xhat, rstd)

    return _call(f"bwd_ple_{l}", body, (S // tm,),
                 [_row(tm, D), _row(tm, D), _row(tm, 256), _const((1, D)), _const((1, D)), _const(wg.shape),
                  _wspec(wp), _const((1, D))],
                 [_row(tm, D), _row(tm, D), _row(tm, D), _row(tm, D), _acc(8, D)],
                 [_sds((S, D)), _sds((S, D), BF16), _sds((S, D), BF16), _sds((S, D), BF16), _sds((8, D))],
                 hosts=hosts)(g, z2, p, lg, lb, wg, wp, pg)


def _bwd_ffn(dz2, z1, ab, ub, lg, lb, wgt, wut, wd, l, hosts=()):
    tm = 256

    def body(dz2_ref, z_ref, a_ref, u_ref, lg_ref, lb_ref, wg_ref, wu_ref, wd_ref, dz1_ref, xb_ref, da_ref, du_ref,
             acc_ref):
        _zero_first(acc_ref)
        dz2v = dz2_ref[...]
        lgv = lg_ref[...]
        x1, xhat, rstd = _ln(z_ref[...], lgv, lb_ref[...])
        xb_ref[...] = x1.astype(BF16)
        a = a_ref[...].astype(F32)
        u = u_ref[...].astype(F32)
        sg = _sig(a)
        dhm = _mmt(dz2v, wd_ref[...])
        dub = (dhm * (a * sg)).astype(BF16)
        dab = (dhm * u * _silu_grad(a, sg)).astype(BF16)
        da_ref[...] = dab
        du_ref[...] = dub
        dx1 = ALPHA * dz2v + _mm(dab, wg_ref[...]) + _mm(dub, wu_ref[...])
        acc_ref[0:1, :] += _colsum(dx1 * xhat)
        acc_ref[1:2, :] += _colsum(dx1)
        dz1_ref[...] = _ln_bwd(dx1 * lgv, xhat, rstd)

    return _call(f"bwd_ffn_{l}", body, (S // tm,),
                 [_row(tm, D), _row(tm, D), _row(tm, FF), _row(tm, FF), _const((1, D)), _const((1, D)),
                  _const((FF, D)), _const((FF, D)), _const((FF, D))],
                 [_row(tm, D), _row(tm, D), _row(tm, FF), _row(tm, FF), _acc(8, D)],
                 [_sds((S, D)), _sds((S, D), BF16), _sds((S, FF), BF16), _sds((S, FF), BF16), _sds((8, D))],
                 hosts=hosts)(dz2, z1, ab, ub, lg, lb, wgt, wut, wd)


def _bwd_a2(dz1, cv, lg, lb, w2, l, hosts=()):
    tm = 512

    def body(dz_ref, cv_ref, lg_ref, lb_ref, w2_ref, dcv_ref, sb_ref, acc_ref):
        _zero_first(acc_ref)
        lgv = lg_ref[...]
        n, xhat, rstd = _ln(cv_ref[...], lgv, lb_ref[...])
        sg = _sig(n)
        sb_ref[...] = (n * sg).astype(BF16)
        dzb = dz_ref[...].astype(BF16)
        ds = _mmt(dzb, w2_ref[...])
        dn = ds * _silu_grad(n, sg)
        acc_ref[0:1, :] += _colsum(dn * xhat)
        acc_ref[1:2, :] += _colsum(dn)
        dcv = _ln_bwd(dn * lgv, xhat, rstd)
        acc_ref[2:3, :] += _colsum(dcv)
        dcv_ref[...] = dcv

    return _call(f"bwd_a2_{l}", body, (S // tm,),
                 [_row(tm, D), _row(tm, D), _const((1, D)), _const((1, D)), _const(w2.shape)],
                 [_row(tm, D), _row(tm, D), _acc(8, D)],
                 [_sds((S, D)), _sds((S, D), BF16), _sds((8, D))], hosts=hosts)(dz1, cv, lg, lb, w2)


def _bwd_conv_a(dcv, glu, wdw, l, hosts=()):
    tm = 256
    nb = S // tm

    def body(d_ref, dn_ref, g_ref, gp_ref, w_ref, dglu_ref, dw_ref, bufd, bufx, sh):
        i = pl.program_id(0)
        _zero_first(dw_ref)
        bufd[0:tm, :] = d_ref[...]
        _fill_halo(bufd, tm, HALO_A, lambda: dn_ref[...], i == nb - 1)
        _fill_halo(bufx, 0, HALO_A, lambda: gp_ref[...], i == 0)
        bufx[HALO_A:HALO_A + tm, :] = g_ref[...]
        _shift_copies(bufd, sh)
        _conv_rows(dglu_ref, w_ref, None, [CONV_A - 1 - k for k in range(CONV_A)], bufd, sh, tm)
        _shift_copies(bufx, sh)
        _conv_wgrad(dw_ref, d_ref, [HALO_A - (CONV_A - 1) + k for k in range(CONV_A)], bufx, sh, tm)

    return _call(f"bwd_conv_a_{l}", body, (nb,),
                 [_row(tm, D), _next(tm, HALO_A, D), _row(tm, D), _prev(tm, HALO_A, D), _const((32, D))],
                 [_row(tm, D), _acc(32, D)], [_sds((S, D)), _sds((32, D))],
                 scratch=[pltpu.VMEM((tm + HALO_A, D), F32), pltpu.VMEM((HALO_A + tm, D), F32),
                          pltpu.VMEM((SUB - 1, HALO_A + tm - SUB, D), F32)], hosts=hosts)(dcv, dcv, glu, glu, wdw)


def _bwd_a1(dglu, h, dz1, w1, l, hosts=()):
    tm = 256

    def body(dg_ref, h_ref, dz_ref, w_ref, dx_ref, dh_ref, acc_ref):
        _zero_first(acc_ref)
        a, g = h_ref[:, :D], h_ref[:, D:]
        sg = _sig(g)
        dgl = dg_ref[...]
        da = dgl * sg
        dg = dgl * a * sg * (1.0 - sg)
        acc_ref[0:1, 0:D] += _colsum(da)
        acc_ref[0:1, D:2 * D] += _colsum(dg)
        dh_ref[:, 0:D] = da.astype(BF16)
        dh_ref[:, D:2 * D] = dg.astype(BF16)
        dx = ALPHA * dz_ref[...]
        for q in range(NQ):
            dx = dx + _mmt(dh_ref[:, q * 512:(q + 1) * 512], w_ref[q])
        dx_ref[...] = dx

    return _call(f"bwd_a1_{l}", body, (S // tm,),
                 [_row(tm, D), _row(tm, 2 * D), _row(tm, D), _wspec(w1)],
                 [_row(tm, D), _row(tm, 2 * D), _acc(8, 2 * D)],
                 [_sds((S, D)), _sds((S, 2 * D), BF16), _sds((8, 2 * D))], hosts=hosts)(dglu, h, dz1, w1)


def _bwd_c2(dz1, hc, wc, wout):
    tm = 256

    def body(dz_ref, hc_ref, hcp_ref, wc_ref, wout_ref, dy_ref, dbg_ref, mb_ref, buf):
        y = _short_conv(buf, hc_ref, hcp_ref, wc_ref, tm, pl.program_id(0))
        dzb = dz_ref[...].astype(BF16)
        dm = _mmt(dzb, wout_ref[...])
        bg = hc_ref[:, :D]
        mb_ref[...] = (bg * y).astype(BF16)
        dbg_ref[...] = (dm * y).astype(BF16)
        dy_ref[...] = dm * bg

    return _call("bwd_c2", body, (S // tm,),
                 [_row(tm, D), _row(tm, 3 * D), _prev(tm, HALO_C, 3 * D), _const((8, D)), _const(wout.shape)],
                 [_row(tm, D), _row(tm, D), _row(tm, D)],
                 [_sds((S, D)), _sds((S, D), BF16), _sds((S, D), BF16)],
                 scratch=[pltpu.VMEM((HALO_C + tm, D), F32)])(dz1, hc, hc, wc, wout)


def _bwd_c1(dy, hc, dbg, dz1, wc, win):
    tm = 256
    nb = S // tm

    def body(d_ref, dn_ref, hc_ref, hcp_ref, dbg_ref, dz_ref, wc_ref, win_ref, dx_ref, dhc_ref, dwc_ref, bufd, bufq):
        i = pl.program_id(0)
        _zero_first(dwc_ref)
        bufd[0:tm, :] = d_ref[...]
        _fill_halo(bufd, tm, HALO_C, lambda: dn_ref[...], i == nb - 1)
        _fill_halo(bufq, 0, HALO_C, lambda: hcp_ref[:, D:2 * D] * hcp_ref[:, 2 * D:], i == 0)
        bufq[HALO_C:HALO_C + tm, :] = hc_ref[:, D:2 * D] * hc_ref[:, 2 * D:]
        dq = wc_ref[0:1, :] * bufd[pl.ds(CONV_C - 1, tm), :]
        for k in range(1, CONV_C):
            dq = dq + wc_ref[k:k + 1, :] * bufd[pl.ds(CONV_C - 1 - k, tm), :]
        dv = d_ref[...]
        for k in range(CONV_C):
            dwc_ref[k:k + 1, :] += _colsum(dv * bufq[pl.ds(HALO_C - (CONV_C - 1) + k, tm), :])
        dhc_ref[:, 0:D] = dbg_ref[...]
        dhc_ref[:, D:2 * D] = (dq * hc_ref[:, 2 * D:]).astype(BF16)
        dhc_ref[:, 2 * D:3 * D] = (dq * hc_ref[:, D:2 * D]).astype(BF16)
        dx = ALPHA * dz_ref[...]
        for q in range(NQ):
            dx = dx + _mmt(dhc_ref[:, q * 768:(q + 1) * 768], win_ref[q])
        dx_ref[...] = dx

    return _call("bwd_c1", body, (nb,),
                 [_row(tm, D), _next(tm, HALO_C, D), _row(tm, 3 * D), _prev(tm, HALO_C, 3 * D), _row(tm, D),
                  _row(tm, D), _const((8, D)), _wspec(win)],
                 [_row(tm, D), _row(tm, 3 * D), _acc(8, D)],
                 [_sds((S, D)), _sds((S, 3 * D), BF16), _sds((8, D))],
                 scratch=[pltpu.VMEM((tm + HALO_C, D), F32), pltpu.VMEM((HALO_C + tm, D), F32)]
                 )(dy, dy, hc, hc, dbg, dz1, wc, win)


def _bwd_b(dz1, zg, gg, lg, lb, win, wout, ws, wst, bsx):
    tm = 128
    nb = S // tm

    def body(dz_ref, zg_ref, gg_ref, lg_ref, lb_ref, win_ref, wout_ref, ws_ref, wst_ref, bsx_ref,
             dx_ref, dh_ref, mb_ref, acc_ref, dws_ref, dbs_ref, f_scr, dvn_scr):
        _zero_first(acc_ref, dws_ref, dbs_ref)
        lgv = lg_ref[...]
        u = zg_ref[:, :E].astype(F32)
        v = zg_ref[:, E:].astype(F32)
        vn, xhat, rstd = _ln(v, lgv, lb_ref[...])
        vnb = vn.astype(BF16)
        dzb = dz_ref[...].astype(BF16)
        dm = _mmt(dzb, wout_ref[...])
        mask, mask_t = _sgu_masks()
        for hd in range(SGU_H):
            wm = jnp.where(mask, ws_ref[hd], 0.0).astype(BF16)
            cs = slice(hd * SGU_G, (hd + 1) * SGU_G)
            for n in range(tm // SGU_T):
                rs = slice(n * SGU_T, (n + 1) * SGU_T)
                f_scr[rs, cs] = jnp.dot(wm, vnb[rs, cs], preferred_element_type=F32) + bsx_ref[hd]
        f = f_scr[...]
        mb_ref[...] = (u * f).astype(BF16)
        du = dm * f
        df = dm * u
        dfb = df.astype(BF16)
        for hd in range(SGU_H):
            wmt = jnp.where(mask_t, wst_ref[hd], 0.0).astype(BF16)
            cs = slice(hd * SGU_G, (hd + 1) * SGU_G)
            for n in range(tm // SGU_T):
                rs = slice(n * SGU_T, (n + 1) * SGU_T)
                dvn_scr[rs, cs] = jnp.dot(wmt, dfb[rs, cs], preferred_element_type=F32)
                dws_ref[hd] += lax.dot_general(dfb[rs, cs], vnb[rs, cs], (((1,), (1,)), ((), ())),
                                               preferred_element_type=F32)
                dbs_ref[hd] += df[rs, cs]
        dvn = dvn_scr[...]
        acc_ref[1:2, 0:E] += _colsum(dvn * xhat)
        acc_ref[2:3, 0:E] += _colsum(dvn)
        dv = _ln_bwd(dvn * lgv, xhat, rstd)
        dhu = du * gg_ref[:, :E].astype(F32)
        dhv = dv * gg_ref[:, E:].astype(F32)
        acc_ref[0:1, 0:E] += _colsum(dhu)
        acc_ref[0:1, E:2 * E] += _colsum(dhv)
        dh_ref[:, 0:E] = dhu.astype(BF16)
        dh_ref[:, E:2 * E] = dhv.astype(BF16)
        dx = ALPHA * dz_ref[...]
        for q in range(NQ):
            dx = dx + _mmt(dh_ref[:, q * 1024:(q + 1) * 1024], win_ref[q])
        dx_ref[...] = dx

        @pl.when(pl.program_id(0) == nb - 1)
        def _():
            for hd in range(SGU_H):
                dws_ref[hd] = jnp.where(mask, dws_ref[hd], 0.0)

    c3 = lambda a, b, c: pl.BlockSpec((a, b, c), lambda i: (0, 0, 0))
    return _call("bwd_b", body, (nb,),
                 [_row(tm, D), _row(tm, 2 * E), _row(tm, 2 * E), _const((1, E)), _const((1, E)), _wspec(win),
                  _const(wout.shape), _const((SGU_H, SGU_T, SGU_T)), _const((SGU_H, SGU_T, SGU_T)),
                  _const((SGU_H, SGU_T, SGU_G))],
                 [_row(tm, D), _row(tm, 2 * E), _row(tm, E), _acc(8, 2 * E), c3(SGU_H, SGU_T, SGU_T),
                  c3(SGU_H, SGU_T, SGU_G)],
                 [_sds((S, D)), _sds((S, 2 * E), BF16), _sds((S, E), BF16), _sds((8, 2 * E)),
                  _sds((SGU_H, SGU_T, SGU_T)), _sds((SGU_H, SGU_T, SGU_G))],
                 scratch=[pltpu.VMEM((tm, E), F32), pltpu.VMEM((tm, E), F32)]
                 )(dz1, zg, gg, lg, lb, win, wout, ws, wst, bsx)


def _mm_tn(name, a, amode, b, bmode, k, n, groups=NQ, hosts=()):
    def block_bytes(ts):
        ka = k if amode == "1" else groups * k
        nb = n if bmode == "1" else groups * n
        return 2 * (ts * ka * a.dtype.itemsize + ts * nb * b.dtype.itemsize + groups * k * n * 4)

    ts = min(1024 if block_bytes(1024) <= DW_BLOCK_BUDGET else 512, S)

    def spec(mode, w):
        if mode == "1":
            return pl.BlockSpec((ts, w), lambda s: (s, 0))
        if mode == "c":
            return pl.BlockSpec((ts, groups * w), lambda s: (s, 0))
        return pl.BlockSpec((groups, ts, w), lambda s: (0, s, 0))

    def pick(ref, mode, w, g):
        if mode == "1":
            return ref[...]
        if mode == "c":
            return ref[:, g * w:(g + 1) * w]
        return ref[g]

    def body(a_ref, b_ref, o_ref):
        _zero_first(o_ref)
        a_t = jnp.transpose(a_ref[...].astype(BF16)) if amode == "1" else None
        b_1 = b_ref[...].astype(BF16) if bmode == "1" else None
        for g in range(groups):
            lhs = a_t if amode == "1" else jnp.transpose(pick(a_ref, amode, k, g).astype(BF16))
            rhs = b_1 if bmode == "1" else pick(b_ref, bmode, n, g).astype(BF16)
            o_ref[0, g] += jnp.dot(lhs, rhs, preferred_element_type=F32)

    return _call(name, body, (S // ts,), [spec(amode, k), spec(bmode, n)],
                 pl.BlockSpec((1, groups, k, n), lambda s: (0, 0, 0, 0)), _sds((1, groups, k, n)), hosts=hosts)(a, b)


def _row_block(k, cap=256):
    return max(t for t in range(16, min(k, cap) + 1, 16) if k % t == 0)


def _cast_bf16(w, hosts=()):
    nl, k, n = w.shape
    tb = _row_block(k, 512)
    nb = k // tb

    def body(w_ref, o_ref):
        o_ref[...] = w_ref[...].astype(BF16)

    spec = pl.BlockSpec((None, tb, n), lambda i: (i // nb, i % nb, 0))
    return _call("cast_bf16", body, (nl * nb,), [spec], spec, _sds(w.shape, BF16), hosts=hosts)(w)


def _adam(name, w, m, v, gc, l, prev):
    nl, k, n = w.shape
    nc = gc.shape[0]
    tb = _row_block(k, 512)

    def body(w_ref, m_ref, v_ref, g_ref, *rest):
        go_ref, d_ref, mo_ref, vo_ref = rest[-4:]
        g = g_ref[0].astype(F32)
        for c in range(1, nc):
            g = g + g_ref[c].astype(F32)
        m2 = ADAM_B1 * m_ref[...] + (1.0 - ADAM_B1) * g
        v2 = ADAM_B2 * v_ref[...] + (1.0 - ADAM_B2) * (g * g)
        m_hat = m2 / (1.0 - ADAM_B1 ** ADAM_STEP)
        v_hat = v2 / (1.0 - ADAM_B2 ** ADAM_STEP)
        go_ref[...] = g
        d_ref[...] = -ADAM_LR * (m_hat / (jnp.sqrt(v_hat) + ADAM_EPS) + ADAM_WD * w_ref[...])
        mo_ref[...] = m2
        vo_ref[...] = v2

    spec = pl.BlockSpec((None, tb, n), lambda i: (l, i, 0))
    gspec = pl.BlockSpec((nc, None, tb, n), lambda i: (0, 0, i, 0))
    in_specs, args, aliases = [spec, spec, spec, gspec], [w, m, v, gc], {}
    if prev is not None:
        in_specs += [pl.BlockSpec(memory_space=pl.ANY)] * 4
        args += list(prev)
        aliases = {4 + j: j for j in range(4)}
    return _call(name, body, (k // tb,), in_specs, [spec] * 4, [_sds(w.shape)] * 4, aliases=aliases)(*args)


def _sum8(name, g8):
    r = g8.shape[1]

    def body(g_ref, o_ref):
        acc = g_ref[0]
        for d in range(1, 8):
            acc = acc + g_ref[d]
        o_ref[...] = acc

    return _call(name, body, (1,), [pl.BlockSpec((8, r, 128), lambda i: (0, 0, 0))],
                 pl.BlockSpec((r, 128), lambda i: (0, 0)), _sds((r, 128)))(g8)


def _place():
    x, y, c = lax.axis_index("x"), lax.axis_index("y"), lax.axis_index("c")
    return x, y, c, 2 * x + y, (x, y, 1 - c), [(1 - x, y), (x, 1 - y), (1 - x, 1 - y)]


class _Exchange:
    def __init__(self, arrays, out_shapes):
        self.arrays, self.out_shapes = list(arrays), list(out_shapes)
        n = len(self.arrays)
        self.sems = [pltpu.SemaphoreType.DMA((7 * n,)), pltpu.SemaphoreType.DMA((7 * n,)),
                     pltpu.SemaphoreType.DMA((n,))]

    def _copies(self, ins, outs, sems):
        send, recv, lsem = sems
        local_src, remote_src, dst = self.maps(ins, outs)
        x, y, c, q, sib, chips = _place()

        def rcopy(w, k, qq, cc, to, src=None):
            return pltpu.make_async_remote_copy(
                src_ref=dst(w, qq, cc) if src is None else src, dst_ref=dst(w, qq, cc),
                send_sem=send.at[7 * w + k], recv_sem=recv.at[7 * w + k], device_id=to, device_id_type=MESH)

        def mine(w):
            return pltpu.make_async_copy(local_src(w), dst(w, q, c), lsem.at[w])

        def first(w):
            return [rcopy(w, 0, q, c, sib, local_src(w))] + [
                rcopy(w, 1 + j, q, c, (cx, cy, c), remote_src(w, 2 * cx + cy)) for j, (cx, cy) in enumerate(chips)]

        return rcopy, mine, first, (x, y, c), q, c, sib, chips

    def start(self, ins, outs, sems):
        _, mine, first, *_ = self._copies(ins, outs, sems)
        for w in range(len(self.arrays)):
            mine(w).start()
            for cp in first(w):
                cp.start()

    def forward_steps(self, n_steps):
        sizes = [a.size // a.shape[0] for a in self.arrays]
        plan, moved = {}, 0
        for w, size in enumerate(sizes):
            moved += size
            plan.setdefault(min(n_steps - 1, -(-moved * n_steps // sum(sizes))), []).append(w)
        return plan

    def forward(self, ws, ins, outs, sems):
        rcopy, _, _, me, _, c, sib, chips = self._copies(ins, outs, sems)
        for w in ws:
            for j, (cx, cy) in enumerate(chips):
                rcopy(w, 1 + j, 2 * cx + cy, c, me).wait_recv()
                rcopy(w, 4 + j, 2 * cx + cy, c, sib).start()

    def complete(self, ins, outs, sems):
        rcopy, mine, first, me, q, c, sib, chips = self._copies(ins, outs, sems)
        n = len(self.arrays)
        for w in range(n):
            rcopy(w, 0, q, 1 - c, me).wait_recv()
            for j, (cx, cy) in enumerate(chips):
                rcopy(w, 4 + j, 2 * cx + cy, 1 - c, me).wait_recv()
        for w in range(n):
            for cp in first(w):
                cp.wait_send()
            for j, (cx, cy) in enumerate(chips):
                rcopy(w, 4 + j, 2 * cx + cy, c, sib).wait_send()
            mine(w).wait()


class _GatherWeights(_Exchange):
    def __init__(self, items):
        self.layers = [l for _, l in items]
        self.kh = [s.shape[1] // 2 for s, _ in items]
        super().__init__([s for s, _ in items], [_sds((NQ, 1) + s.shape[1:], BF16) for s, _ in items])

    def maps(self, ins, outs):
        c = lax.axis_index("c")
        src = lambda w: ins[w].at[pl.ds(self.layers[w], 1), pl.ds(c * self.kh[w], self.kh[w]), :]
        return src, lambda w, q: src(w), lambda w, q, cc: outs[w].at[q, :, pl.ds(cc * self.kh[w], self.kh[w]), :]


class _ScatterPartials(_Exchange):
    def __init__(self, parts):
        super().__init__(parts, [_sds((NQ, 1, 2) + p.shape[2:], BF16) for p in parts])

    def maps(self, ins, outs):
        q = 2 * lax.axis_index("x") + lax.axis_index("y")
        return (lambda w: ins[w].at[:, q]), (lambda w, qq: ins[w].at[:, qq]), (lambda w, qq, cc: outs[w].at[qq, :, cc])


class _Gather8(_Exchange):
    def __init__(self, v):
        super().__init__([v], [_sds((8,) + v.shape)])

    def maps(self, ins, outs):
        return (lambda w: ins[0]), (lambda w, q: ins[0]), (lambda w, q, cc: outs[0].at[2 * q + cc])


class _SwapHalves:
    def __init__(self, dws):
        self.arrays = list(dws)
        self.kh = [d.shape[2] // 2 for d in dws]
        self.out_shapes = [_sds(d.shape[:2] + (kh,) + d.shape[3:]) for d, kh in zip(dws, self.kh)]
        self.sems = [pltpu.SemaphoreType.DMA((len(dws),)), pltpu.SemaphoreType.DMA((len(dws),))]

    def _copies(self, ins, outs, sems):
        send, recv = sems
        _, _, c, _, sib, _ = _place()
        return [pltpu.make_async_remote_copy(
            src_ref=ins[w].at[:, :, pl.ds((1 - c) * self.kh[w], self.kh[w]), :], dst_ref=outs[w],
            send_sem=send.at[w], recv_sem=recv.at[w], device_id=sib, device_id_type=MESH)
            for w in range(len(self.arrays))]

    def start(self, ins, outs, sems):
        for cp in self._copies(ins, outs, sems):
            cp.start()

    def forward_steps(self, n_steps):
        return {}

    def complete(self, ins, outs, sems):
        for cp in self._copies(ins, outs, sems):
            cp.wait()


def _comm_only(name, host):
    n_in, n_out = len(host.arrays), len(host.out_shapes)

    def body(*refs):
        ins, outs, sems = refs[:n_in], refs[n_in:n_in + n_out], refs[n_in + n_out:]
        host.start(ins, outs, sems)
        for ws in host.forward_steps(1).values():
            host.forward(ws, ins, outs, sems)
        host.complete(ins, outs, sems)

    any_spec = pl.BlockSpec(memory_space=pl.ANY)
    return pl.pallas_call(body, name=name, in_specs=[any_spec] * n_in, out_specs=[any_spec] * n_out,
                          out_shape=host.out_shapes, scratch_shapes=host.sems)(*host.arrays)


def _add_halves(dw, got, cidx):
    nl, _, k, n = dw.shape
    kh = k // 2
    qb = 2

    def body(c_ref, a_ref, b_ref, o_ref):
        o_ref[...] = (a_ref[...] + b_ref[...]).astype(BF16)

    grid_spec = pltpu.PrefetchScalarGridSpec(
        num_scalar_prefetch=1, grid=(nl, NQ // qb),
        in_specs=[pl.BlockSpec((None, qb, None, kh, n), lambda l, q, c_ref: (l, q, c_ref[0], 0, 0)),
                  pl.BlockSpec((None, qb, kh, n), lambda l, q, c_ref: (l, q, 0, 0))],
        out_specs=pl.BlockSpec((None, qb, kh, n), lambda l, q, c_ref: (l, q, 0, 0)))
    return pl.pallas_call(
        body, name="add_halves", grid_spec=grid_spec, out_shape=_sds((nl, NQ, kh, n), BF16),
        compiler_params=pltpu.CompilerParams(dimension_semantics=("arbitrary", "arbitrary"),
                                             vmem_limit_bytes=VMEM_LIMIT))(cidx, dw.reshape(nl, NQ, 2, kh, n), got)


def _gather8(name, v):
    return _comm_only(name, _Gather8(v))[0]


PACK = 16 * 128


def _pack(arrays):
    parts = []
    for a in arrays:
        flat = a.reshape(-1)
        parts.append(jnp.pad(flat, (0, (-flat.shape[0]) % PACK)))
    return jnp.concatenate(parts).reshape(-1, 128)


def _unpack(packed, shapes):
    flat = packed.reshape(-1)
    out, off = [], 0
    for shp in shapes:
        size = 1
        for d in shp:
            size *= d
        out.append(flat[off:off + size].reshape(shp))
        off += size + (-size) % PACK
    return out


def kernel(x, p, a_w_pw1, a_b_pw1, a_w_dw, a_b_dw, a_ln_g, a_ln_b, a_w_pw2, b_w_in, b_b_in, b_ln_g, b_ln_b, b_w_s, b_b_s, b_w_out, c_w_in, c_w_conv, c_w_out, ln1_g, ln1_b, ln2_g, ln2_b, ffn_w_gate, ffn_w_up, ffn_w_down, ple_w_gate, ple_w_proj, ple_norm_g, loss_target, m_a_w_pw1, m_a_b_pw1, m_a_w_dw, m_a_b_dw, m_a_ln_g, m_a_ln_b, m_a_w_pw2, m_b_w_in, m_b_b_in, m_b_ln_g, m_b_ln_b, m_b_w_s, m_b_b_s, m_b_w_out, m_c_w_in, m_c_w_conv, m_c_w_out, m_ln1_g, m_ln1_b, m_ln2_g, m_ln2_b, m_ffn_w_gate, m_ffn_w_up, m_ffn_w_down, m_ple_w_gate, m_ple_w_proj, m_ple_norm_g, v_a_w_pw1, v_a_b_pw1, v_a_w_dw, v_a_b_dw, v_a_ln_g, v_a_ln_b, v_a_w_pw2, v_b_w_in, v_b_b_in, v_b_ln_g, v_b_ln_b, v_b_w_s, v_b_b_s, v_b_w_out, v_c_w_in, v_c_w_conv, v_c_w_out, v_ln1_g, v_ln1_b, v_ln2_g, v_ln2_b, v_ffn_w_gate, v_ffn_w_up, v_ffn_w_down, v_ple_w_gate, v_ple_w_proj, v_ple_norm_g):
    args = dict(locals())
    wts = {k: args[k] for k in WEIGHTS}
    mom = {k: args["m_" + k] for k in WEIGHTS}
    var = {k: args["v_" + k] for k in WEIGHTS}
    for k in TRANSPOSED:
        wts[k], mom[k], var[k] = (jnp.transpose(t[k], (0, 2, 1)) for t in (wts, mom, var))
    q_idx = 2 * lax.axis_index("x") + lax.axis_index("y")
    c_idx = lax.axis_index("c").astype(jnp.int32).reshape(1)

    wb = {k: _cast_bf16(wts[k]) for k in BIG if k not in ("ffn_w_gate", "ffn_w_up")}
    mixw = [[("a_w_pw1", 0), ("a_w_pw2", 0)], [("b_w_in", 0), ("b_w_out", 0)], [("c_w_in", 0), ("c_w_out", 0)],
            [("a_w_pw1", 1), ("a_w_pw2", 1)]]
    ffnw = [[("ffn_w_gate", l), ("ffn_w_up", l), ("ffn_w_down", l)] for l in range(DEPTH)]
    plew = [[("ple_w_gate", l), ("ple_w_proj", l)] for l in range(DEPTH)]
    fwd_plan = {("a1", 0): mixw[0][1:] + plew[0], ("a2", 0): ffnw[0], ("ffn", 0): mixw[1], ("ple", 0): plew[1],
                ("b", 1): ffnw[1], ("ffn", 1): mixw[2] + ffnw[2][:1], ("ple", 1): plew[2],
                ("c1", 2): ffnw[2][1:2], ("c2", 2): ffnw[2][2:], ("ffn", 2): mixw[3] + ffnw[3][:1], ("ple", 2): plew[3],
                ("a1", 3): ffnw[3][1:2], ("a2", 3): ffnw[3][2:]}
    gw = {}

    def gather(keys):
        return _GatherWeights([(wb[name], l) for name, l in keys])

    def hosted(tag, fn, *fargs):
        keys = fwd_plan.get(tag)
        if not keys:
            return fn(*fargs)
        own, (got,) = fn(*fargs, hosts=[gather(keys)])
        store(keys, got)
        return own

    def store(keys, got):
        for (name, l), arr in zip(keys, got):
            gw[name, l] = arr.reshape(NQ * arr.shape[2], arr.shape[3]) if name in ROW_SHARDED else arr

    first_keys = mixw[0][:1]
    wb["ffn_w_gate"], (got,) = _cast_bf16(wts["ffn_w_gate"], hosts=[gather(first_keys)])
    store(first_keys, got)
    shard_shapes = [wts[k].shape for k in SMALL_SHARDED]
    wb["ffn_w_up"], ((small8,),) = _cast_bf16(wts["ffn_w_up"], hosts=[_Gather8(_pack([wts[k] for k in SMALL_SHARDED]))])
    per_chip = [_unpack(small8[2 * qq], shard_shapes) for qq in range(NQ)]
    full = {k: jnp.concatenate([per_chip[qq][i] for qq in range(NQ)], axis=-1) for i, k in enumerate(SMALL_SHARDED)}
    for k in SMALL_REPL:
        full[k] = wts[k]

    def vec(name, l):
        return full[name][l][None, :]

    def conv_w(name, l, rows):
        w = full[name][l]
        return jnp.pad(w, ((0, rows - w.shape[0]), (0, 0)))

    ws = full["b_w_s"][0]
    wst = jnp.transpose(ws, (0, 2, 1))
    bsx = jnp.broadcast_to(full["b_b_s"][0][:, :, None], (SGU_H, SGU_T, SGU_G))

    x0s, z1s, z2s, saved, ffn_saved = [], [], [], [], []
    cur = x[0]
    for i in range(DEPTH):
        mix, j = i % 3, i // 3
        x0s.append(cur)
        if mix == 0:
            h, glu = hosted(("a1", i), _fwd_a1, cur, gw["a_w_pw1", j], vec("a_b_pw1", j), i)
            z1, cv = hosted(("a2", i), _fwd_a2, glu, cur, conv_w("a_w_dw", j, 32), vec("a_b_dw", j), vec("a_ln_g", j),
                            vec("a_ln_b", j), gw["a_w_pw2", j], i)
            saved.append((h, glu, cv))
        elif mix == 1:
            z1, zg, gg = hosted(("b", i), _fwd_b, cur, gw["b_w_in", 0], vec("b_b_in", 0), vec("b_ln_g", 0),
                                vec("b_ln_b", 0), ws, bsx, gw["b_w_out", 0])
            saved.append((zg, gg))
        else:
            hc = hosted(("c1", i), _fwd_c1, cur, gw["c_w_in", 0])
            z1 = hosted(("c2", i), _fwd_c2, hc, cur, conv_w("c_w_conv", 0, 8), gw["c_w_out", 0])
            saved.append((hc,))
        z2, ab, ub, hm = hosted(("ffn", i), _fwd_ffn, z1, vec("ln1_g", i), vec("ln1_b", i), gw["ffn_w_gate", i],
                                gw["ffn_w_up", i], gw["ffn_w_down", i], i)
        ffn_saved.append((ab, ub, hm))
        cur = hosted(("ple", i), _fwd_ple, z2, p[i, 0], vec("ln2_g", i), vec("ln2_b", i), gw["ple_w_gate", i],
                     gw["ple_w_proj", i], vec("ple_norm_g", i), i)
        z1s.append(z1)
        z2s.append(z2)

    g, loss_acc = _loss_head(cur, loss_target[0])
    loss = lax.psum(0.5 / D * jnp.sum(loss_acc[0]), ("x", "y", "c"))

    dws = {}
    sg = {}
    res = {k: None for k in BIG}

    def wgrad(name, l, a, amode, b, bmode, scatter_keys=(), gather_small=None):
        _, k, n = wts[name].shape
        hosts = [_ScatterPartials([parts[key] for key in scatter_keys])] if scatter_keys else []
        if gather_small is not None:
            hosts = [_Gather8(gather_small)]
        if name in ROW_SHARDED:
            out = _mm_tn(f"dw_{name}_{l}", a, "1", b, "1", NQ * k, n, groups=1, hosts=hosts)
        else:
            out = _mm_tn(f"dw_{name}_{l}", a, amode, b, bmode, k, n, hosts=hosts)
        got = None
        if hosts:
            out, (got,) = out
        if scatter_keys:
            update(scatter_keys, got)
        dws[name, l] = out.reshape(1, NQ, k, n)
        return got

    def swap(keys):
        return _SwapHalves([dws[k] for k in keys])

    parts = {}

    def add_halves(keys, got):
        parts.update((k, _add_halves(dws[k], r, c_idx)) for k, r in zip(keys, got))

    def update(keys, contribs):
        for (name, l), gc in zip(keys, contribs):
            _, kq, n = wts[name].shape
            res[name] = _adam(f"adam_{name}_{l}", wts[name], mom[name], var[name], gc.reshape(NQ, 1, kq, n), l,
                              res[name])

    small = SMALL_SHARDED + SMALL_REPL
    late_small = [("a_ln_g", 0), ("a_ln_b", 0), ("a_b_dw", 0), ("a_w_dw", 0), ("a_b_pw1", 0)]
    early_small = [(k, l) for k in small for l in range(full[k].shape[0]) if (k, l) not in late_small]
    pending = None
    for i in reversed(range(DEPTH)):
        mix, j = i % 3, i // 3
        ple_args = (g, z2s[i], p[i, 0], vec("ln2_g", i), vec("ln2_b", i), gw["ple_w_gate", i], gw["ple_w_proj", i],
                    vec("ple_norm_g", i), i)
        if pending:
            (dz2, x2b, dgp, dqp, acc), (got,) = _bwd_ple(*ple_args, hosts=[swap(pending)])
            add_halves(pending, got)
        else:
            dz2, x2b, dgp, dqp, acc = _bwd_ple(*ple_args)
        sg["ple_norm_g", i], sg["ln2_g", i], sg["ln2_b", i] = acc[0], acc[1], acc[2]
        wgrad("ple_w_gate", i, x2b, "c", dgp, "1", scatter_keys=plew[i + 1] if pending else ())
        wgrad("ple_w_proj", i, p[i, 0], "1", dqp, "c")
        ab, ub, hm = ffn_saved[i]
        ffn_args = (dz2, z1s[i], ab, ub, vec("ln1_g", i), vec("ln1_b", i), gw["ffn_w_gate", i], gw["ffn_w_up", i],
                    gw["ffn_w_down", i], i)
        if pending:
            (dz1, x1b, da, du, acc), (contribs,) = _bwd_ffn(
                *ffn_args, hosts=[_ScatterPartials([parts[key] for key in ffnw[i + 1]])])
            update(ffnw[i + 1], contribs)
        else:
            dz1, x1b, da, du, acc = _bwd_ffn(*ffn_args)
        sg["ln1_g", i], sg["ln1_b", i] = acc[0], acc[1]
        wgrad("ffn_w_gate", i, da, "1", x1b, "1", scatter_keys=mixw[i + 1][1:] if pending else ())
        if i == 0:
            (g8_early,) = wgrad("ffn_w_up", i, du, "1", x1b, "1", gather_small=_pack([sg[pc] for pc in early_small]))
        else:
            wgrad("ffn_w_up", i, du, "1", x1b, "1")
        wgrad("ffn_w_down", i, hm, "1", dz2, "1", scatter_keys=mixw[i + 1][:1] if pending else ())
        x0 = x0s[i]
        if mix == 0:
            h, glu, cv = saved[i]
            a2_args = (dz1, cv, vec("a_ln_g", j), vec("a_ln_b", j), gw["a_w_pw2", j], i)
            conv_args = (glu, conv_w("a_w_dw", j, 32), i)
            if i == 0:
                early = ffnw[0] + plew[0]
                (dcv, sb, acc), (got,) = _bwd_a2(*a2_args, hosts=[swap(early)])
                sg["a_ln_g", j], sg["a_ln_b", j], sg["a_b_dw", j] = acc[0], acc[1], acc[2]
                add_halves(early, got)
                wgrad("a_w_pw2", j, sb, "c", dz1, "1")
                (dglu, dwdw), (contribs, got) = _bwd_conv_a(
                    dcv, *conv_args, hosts=[_ScatterPartials([parts[key] for key in early]), swap(mixw[0][1:])])
                update(early, contribs)
                add_halves(mixw[0][1:], got)
                (g, dh, acc), (contribs,) = _bwd_a1(dglu, h, dz1, gw["a_w_pw1", j], i,
                                                    hosts=[_ScatterPartials([parts[key] for key in mixw[0][1:]])])
                update(mixw[0][1:], contribs)
            else:
                dcv, sb, acc = _bwd_a2(*a2_args)
                sg["a_ln_g", j], sg["a_ln_b", j], sg["a_b_dw", j] = acc[0], acc[1], acc[2]
                dglu, dwdw = _bwd_conv_a(dcv, *conv_args)
                wgrad("a_w_pw2", j, sb, "c", dz1, "1")
                g, dh, acc = _bwd_a1(dglu, h, dz1, gw["a_w_pw1", j], i)
            sg["a_w_dw", j] = dwdw[:CONV_A]
            sg["a_b_pw1", j] = acc[0]
            wgrad("a_w_pw1", j, x0, "1", dh, "c")
        elif mix == 1:
            zg, gg = saved[i]
            g, dh, mb, acc, dw_s, db_s = _bwd_b(dz1, zg, gg, vec("b_ln_g", 0), vec("b_ln_b", 0), gw["b_w_in", 0],
                                                gw["b_w_out", 0], ws, wst, bsx)
            sg["b_b_in", 0], sg["b_ln_g", 0], sg["b_ln_b", 0] = acc[0], acc[1, :E], acc[2, :E]
            sg["b_w_s", 0], sg["b_b_s", 0] = dw_s, jnp.sum(db_s, axis=-1)
            wgrad("b_w_out", 0, mb, "c", dz1, "1")
            wgrad("b_w_in", 0, x0, "1", dh, "c")
        else:
            (hc,) = saved[i]
            wc = conv_w("c_w_conv", 0, 8)
            dy, dbg, mb = _bwd_c2(dz1, hc, wc, gw["c_w_out", 0])
            wgrad("c_w_out", 0, mb, "c", dz1, "1")
            g, dhc, dwc = _bwd_c1(dy, hc, dbg, dz1, wc, gw["c_w_in", 0])
            sg["c_w_conv", 0] = dwc[:CONV_C]
            wgrad("c_w_in", 0, x0, "1", dhc, "c")
        pending = mixw[i] + ffnw[i] + plew[i] if i > 0 else mixw[0][:1]
    grad_x = g[None]
    add_halves(pending, _comm_only("swap_last", swap(pending)))
    update(pending, _comm_only("scatter_last", _ScatterPartials([parts[key] for key in pending])))

    g8_late = _gather8("gather_small_late", _pack([sg[pc] for pc in late_small]))
    sums = dict(zip(early_small, _unpack(_sum8("sum8_early", g8_early), [sg[pc].shape for pc in early_small])))
    sums.update(zip(late_small, _unpack(_sum8("sum8_late", g8_late), [sg[pc].shape for pc in late_small])))
    gsum = [jnp.stack([sums[k, l] for l in range(full[k].shape[0])]) for k in small]
    gmine = []
    for k, gs in zip(small, gsum):
        if k in SMALL_SHARDED:
            wdt = wts[k].shape[-1]
            gs = lax.dynamic_slice_in_dim(gs, q_idx * wdt, wdt, axis=gs.ndim - 1)
        gmine.append(gs)
    packed = [_pack(t)[None] for t in ([wts[k] for k in small], [mom[k] for k in small], [var[k] for k in small])]
    outs = _adam("adam_small", packed[0], packed[1], packed[2], _pack(gmine)[None, None], 0, None)
    unpacked = [_unpack(o[0], [wts[k].shape for k in small]) for o in outs]
    for i, k in enumerate(small):
        res[k] = tuple(u[i] for u in unpacked)

    for k in TRANSPOSED:
        res[k] = tuple(jnp.transpose(r, (0, 2, 1)) for r in res[k])
    return (loss, grad_x, *[res[k][0] for k in WEIGHTS], *[res[k][1] for k in WEIGHTS],
            *[res[k][2] for k in WEIGHTS], *[res[k][3] for k in WEIGHTS])
```

```python
import functools

import jax
import jax.numpy as jnp
from jax import lax
from jax.experimental import pallas as pl
from jax.experimental.pallas import tpu as pltpu

F32, BF16 = jnp.float32, jnp.bfloat16
S = 4096
D = 1024
E = 2048
FF = 2816
FQ = FF // 4
NQ = 4
DEPTH = 4
ALPHA = (2 * DEPTH) ** 0.25
LN_EPS = 1e-5
CONV_A, CONV_C = 31, 3
HALO_A, HALO_C = 32, 8
SGU_T, SGU_H, SGU_G, SGU_CHUNK = 128, 8, 256, 64
VMEM_LIMIT = 56 * 1024 * 1024
DW_BLOCK_BUDGET = 40 * 1024 * 1024
MESH = pl.DeviceIdType.MESH
ADAM_LR, ADAM_B1, ADAM_B2, ADAM_EPS, ADAM_WD, ADAM_STEP = 0.001, 0.9, 0.999, 1e-08, 0.01, 10
GELU_C, GELU_A = 0.7978845608028654, 0.044715

BIG = ["a_w_pw1", "a_w_pw2", "b_w_in", "b_w_out", "c_w_in", "c_w_out",
       "ffn_w_gate", "ffn_w_up", "ffn_w_down", "ple_w_gate", "ple_w_proj"]
TRANSPOSED = ["ffn_w_gate", "ffn_w_up"]
ROW_SHARDED = ["a_w_pw2", "b_w_out", "c_w_out", "ffn_w_gate", "ffn_w_up", "ffn_w_down", "ple_w_gate"]
SMALL_SHARDED = ["a_b_pw1", "a_w_dw", "a_b_dw", "a_ln_g", "a_ln_b", "c_w_conv"]
SMALL_REPL = ["b_b_in", "b_ln_g", "b_ln_b", "b_w_s", "b_b_s", "ln1_g", "ln1_b", "ln2_g", "ln2_b", "ple_norm_g"]
WEIGHTS = ["a_w_pw1", "a_b_pw1", "a_w_dw", "a_b_dw", "a_ln_g", "a_ln_b", "a_w_pw2", "b_w_in", "b_b_in", "b_ln_g",
           "b_ln_b", "b_w_s", "b_b_s", "b_w_out", "c_w_in", "c_w_conv", "c_w_out", "ln1_g", "ln1_b", "ln2_g",
           "ln2_b", "ffn_w_gate", "ffn_w_up", "ffn_w_down", "ple_w_gate", "ple_w_proj", "ple_norm_g"]


def _call(name, body, grid, in_specs, out_specs, out_shape, scratch=(), aliases=None, hosts=()):
    params = pltpu.CompilerParams(dimension_semantics=("arbitrary",) * len(grid), vmem_limit_bytes=VMEM_LIMIT)
    if not hosts:
        return pl.pallas_call(
            body, name=name, grid=grid, in_specs=in_specs, out_specs=out_specs, out_shape=out_shape,
            scratch_shapes=list(scratch), input_output_aliases=aliases or {}, compiler_params=params)
    assert len(grid) == 1 and not aliases
    single = not isinstance(out_shape, (list, tuple))
    own_shapes = [out_shape] if single else list(out_shape)
    own_specs = [out_specs] if single else list(out_specs)
    n_in, n_out, n_scr = len(in_specs), len(own_shapes), len(scratch)
    h_in = [len(h.arrays) for h in hosts]
    h_out = [len(h.out_shapes) for h in hosts]
    h_sem = [len(h.sems) for h in hosts]

    def split(refs, counts):
        out, off = [], 0
        for cnt in counts:
            out.append(refs[off:off + cnt])
            off += cnt
        return out

    def wrapped(*refs):
        ins, hin, outs, hout, scr, hsem = split(refs, [n_in, sum(h_in), n_out, sum(h_out), n_scr, sum(h_sem)])
        per_host = list(zip(hosts, split(hin, h_in), split(hout, h_out), split(hsem, h_sem)))

        @pl.when(pl.program_id(0) == 0)
        def _():
            for h, a, o, s in per_host:
                h.start(a, o, s)

        body(*ins, *outs, *scr)

        for h, a, o, s in per_host:
            for step, ws in sorted(h.forward_steps(grid[0]).items()):
                pl.when(pl.program_id(0) == step)(functools.partial(h.forward, ws, a, o, s))

        @pl.when(pl.program_id(0) == grid[0] - 1)
        def _():
            for h, a, o, s in per_host:
                h.complete(a, o, s)

    any_spec = pl.BlockSpec(memory_space=pl.ANY)
    call = pl.pallas_call(
        wrapped, name=name, grid=grid, in_specs=list(in_specs) + [any_spec] * sum(h_in),
        out_specs=own_specs + [any_spec] * sum(h_out),
        out_shape=own_shapes + [s for h in hosts for s in h.out_shapes],
        scratch_shapes=list(scratch) + [s for h in hosts for s in h.sems], compiler_params=params)

    def run(*args):
        res = call(*args, *[a for h in hosts for a in h.arrays])
        own = res[0] if single else list(res[:n_out])
        return own, split(list(res[n_out:]), h_out)

    return run


def _sds(shape, dtype=F32):
    return jax.ShapeDtypeStruct(shape, dtype)


def _row(tm, c):
    return pl.BlockSpec((tm, c), lambda i: (i, 0))


def _grow(g, tm, c):
    return pl.BlockSpec((g, tm, c), lambda i: (0, i, 0))


def _const(shape):
    nd = len(shape)
    return pl.BlockSpec(shape, lambda i: (0,) * nd, pipeline_mode=pl.Buffered(1))


def _wspec(w):
    return pl.BlockSpec((NQ, None, w.shape[2], w.shape[3]), lambda i: (0, 0, 0, 0), pipeline_mode=pl.Buffered(1))


def _prev(tm, hb, c):
    return pl.BlockSpec((hb, c), lambda i: (jnp.maximum(i * (tm // hb) - 1, 0), 0))


def _next(tm, hb, c):
    return pl.BlockSpec((hb, c), lambda i: (jnp.minimum((i + 1) * (tm // hb), S // hb - 1), 0))


def _acc(r, c):
    return pl.BlockSpec((r, c), lambda i: (0, 0))


def _sig(x):
    return 1.0 / (1.0 + jnp.exp(-x))


def _ln(z, g, b):
    mu = jnp.mean(z, axis=-1, keepdims=True)
    zc = z - mu
    rstd = lax.rsqrt(jnp.mean(zc * zc, axis=-1, keepdims=True) + LN_EPS)
    xhat = zc * rstd
    return xhat * g + b, xhat, rstd


def _ln_bwd(dyg, xhat, rstd):
    return rstd * (dyg - jnp.mean(dyg, axis=-1, keepdims=True) - xhat * jnp.mean(dyg * xhat, axis=-1, keepdims=True))


def _mm(a, w):
    return jnp.dot(a.astype(BF16), w, preferred_element_type=F32)


def _mmt(a, w):
    return lax.dot_general(a.astype(BF16), w, (((1,), (1,)), ((), ())), preferred_element_type=F32)


def _colsum(x):
    return jnp.sum(x, axis=0, keepdims=True)


def _gelu(x):
    t = jnp.tanh(GELU_C * (x + GELU_A * x * x * x))
    return 0.5 * x * (1.0 + t), t


def _gelu_grad(x, t):
    return 0.5 * (1.0 + t) + 0.5 * x * (1.0 - t * t) * GELU_C * (1.0 + 3.0 * GELU_A * x * x)


def _silu_grad(a, sg):
    return sg * (1.0 + a * (1.0 - sg))


def _sgu_masks():
    r = lax.broadcasted_iota(jnp.int32, (SGU_T, SGU_T), 0) // SGU_CHUNK
    c = lax.broadcasted_iota(jnp.int32, (SGU_T, SGU_T), 1) // SGU_CHUNK
    return r >= c, c >= r


def _fill_halo(buf, lo, n, halo_val_fn, is_edge):
    @pl.when(is_edge)
    def _():
        buf[lo:lo + n, :] = jnp.zeros((n, buf.shape[1]), F32)

    @pl.when(jnp.logical_not(is_edge))
    def _():
        buf[lo:lo + n, :] = halo_val_fn()


SUB, LANE = 8, 128
ROWS_AT_ONCE = 16


def _shift_copies(buf, sh):
    rows = sh.shape[1]
    for s in range(1, SUB):
        sh[s - 1, :, :] = buf[pl.ds(s, rows), :]


def _tiles(buf, sh, s, first, count, group0, lanes):
    src = buf if s == 0 else sh.at[s - 1]
    return {t: src[pl.ds(pl.multiple_of((group0 + t) * SUB, SUB), SUB), lanes] for t in range(first, first + count)}


def _by_shift(offsets):
    out = []
    for s in range(SUB):
        taps = [(k, o // SUB) for k, o in enumerate(offsets) if o % SUB == s]
        if taps:
            out.append((s, taps))
    return out


def _conv_rows(out_ref, w_ref, bias_ref, offsets, buf, sh, tm):
    n = ROWS_AT_ONCE
    for cb in range(D // LANE):
        lanes = slice(cb * LANE, (cb + 1) * LANE)
        bias = None if bias_ref is None else jnp.broadcast_to(bias_ref[:, lanes], (SUB, LANE))

        def body(jb, carry):
            accs = [bias] * n
            for s, taps in _by_shift(offsets):
                ms = [m for _, m in taps]
                tiles = _tiles(buf, sh, s, min(ms), max(ms) - min(ms) + n, jb * n, lanes)
                for k, m in taps:
                    wk = jnp.broadcast_to(w_ref[k:k + 1, lanes], (SUB, LANE))
                    for jj in range(n):
                        t = wk * tiles[m + jj]
                        accs[jj] = t if accs[jj] is None else accs[jj] + t
            for jj in range(n):
                out_ref[pl.ds(pl.multiple_of((jb * n + jj) * SUB, SUB), SUB), lanes] = accs[jj]
            return carry

        lax.fori_loop(0, tm // (SUB * n), body, 0)


def _conv_wgrad(dw_ref, d_ref, offsets, buf, sh, tm):
    n = 4
    for cb in range(D // LANE):
        lanes = slice(cb * LANE, (cb + 1) * LANE)

        def body(jq, accs):
            accs = list(accs)
            d = [d_ref[pl.ds(pl.multiple_of((jq * n + jj) * SUB, SUB), SUB), lanes] for jj in range(n)]
            for s, taps in _by_shift(offsets):
                ms = [m for _, m in taps]
                tiles = _tiles(buf, sh, s, min(ms), max(ms) - min(ms) + n, jq * n, lanes)
                for k, m in taps:
                    for jj in range(n):
                        accs[k] = accs[k] + d[jj] * tiles[m + jj]
            return tuple(accs)

        accs = lax.fori_loop(0, tm // (SUB * n), body, tuple(jnp.zeros((SUB, LANE), F32) for _ in offsets))
        for k, acc in enumerate(accs):
            dw_ref[k:k + 1, lanes] += jnp.sum(acc, axis=0, keepdims=True)


def _fwd_a1(x0, w1, b1, l, hosts=()):
    tm = 512

    def body(x_ref, w_ref, b_ref, h_ref, glu_ref):
        xb = x_ref[...].astype(BF16)
        for q in range(NQ):
            sl = slice(q * 512, (q + 1) * 512)
            h_ref[:, sl] = jnp.dot(xb, w_ref[q], preferred_element_type=F32) + b_ref[:, sl]
        glu_ref[...] = h_ref[:, :D] * _sig(h_ref[:, D:])

    return _call(f"fwd_a1_{l}", body, (S // tm,), [_row(tm, D), _wspec(w1), _const((1, 2 * D))],
                 [_row(tm, 2 * D), _row(tm, D)], [_sds((S, 2 * D)), _sds((S, D))], hosts=hosts)(x0, w1, b1)


def _fwd_a2(glu, x0, wdw, bdw, lg, lb, w2, l, hosts=()):
    tm = 256

    def body(g_ref, gp_ref, x_ref, wdw_ref, bdw_ref, lg_ref, lb_ref, w2_ref, z_ref, cv_ref, buf, sh):
        i = pl.program_id(0)
        _fill_halo(buf, 0, HALO_A, lambda: gp_ref[...], i == 0)
        buf[HALO_A:HALO_A + tm, :] = g_ref[...]
        _shift_copies(buf, sh)
        _conv_rows(cv_ref, wdw_ref, bdw_ref, [HALO_A - (CONV_A - 1) + k for k in range(CONV_A)], buf, sh, tm)
        n, _, _ = _ln(cv_ref[...], lg_ref[...], lb_ref[...])
        sb = (n * _sig(n)).astype(BF16)
        z_ref[...] = ALPHA * x_ref[...] + jnp.dot(sb, w2_ref[...], preferred_element_type=F32)

    return _call(f"fwd_a2_{l}", body, (S // tm,),
                 [_row(tm, D), _prev(tm, HALO_A, D), _row(tm, D), _const((32, D)), _const((1, D)), _const((1, D)),
                  _const((1, D)), _const(w2.shape)],
                 [_row(tm, D), _row(tm, D)], [_sds((S, D)), _sds((S, D))],
                 scratch=[pltpu.VMEM((HALO_A + tm, D), F32), pltpu.VMEM((SUB - 1, HALO_A + tm - SUB, D), F32)],
                 hosts=hosts)(glu, glu, x0, wdw, bdw, lg, lb, w2)


def _fwd_b(x0, win, b_in, lg, lb, ws, bsx, wout, hosts=()):
    tm = 256

    def body(x_ref, win_ref, bin_ref, lg_ref, lb_ref, ws_ref, bsx_ref, wout_ref, z_ref, zg_ref, gg_ref, f_scr, h_ref):
        xb = x_ref[...].astype(BF16)
        for q in range(NQ):
            sl = slice(q * 1024, (q + 1) * 1024)
            h_ref[:, sl] = jnp.dot(xb, win_ref[q], preferred_element_type=F32) + bin_ref[:, sl]
        u, tu = _gelu(h_ref[:, :E])
        v, tv = _gelu(h_ref[:, E:])
        zg_ref[:, 0:E] = u.astype(BF16)
        zg_ref[:, E:2 * E] = v.astype(BF16)
        gg_ref[:, 0:E] = _gelu_grad(h_ref[:, :E], tu).astype(BF16)
        gg_ref[:, E:2 * E] = _gelu_grad(h_ref[:, E:], tv).astype(BF16)
        vn, _, _ = _ln(v, lg_ref[...], lb_ref[...])
        vnb = vn.astype(BF16)
        mask, _ = _sgu_masks()
        for hd in range(SGU_H):
            wm = jnp.where(mask, ws_ref[hd], 0.0).astype(BF16)
            cs = slice(hd * SGU_G, (hd + 1) * SGU_G)
            for n in range(tm // SGU_T):
                rs = slice(n * SGU_T, (n + 1) * SGU_T)
                f_scr[rs, cs] = jnp.dot(wm, vnb[rs, cs], preferred_element_type=F32) + bsx_ref[hd]
        mb = (u * f_scr[...]).astype(BF16)
        z_ref[...] = ALPHA * x_ref[...] + jnp.dot(mb, wout_ref[...], preferred_element_type=F32)

    return _call("fwd_b", body, (S // tm,),
                 [_row(tm, D), _wspec(win), _const((1, 2 * E)), _const((1, E)), _const((1, E)),
                  _const((SGU_H, SGU_T, SGU_T)), _const((SGU_H, SGU_T, SGU_G)), _const(wout.shape)],
                 [_row(tm, D), _row(tm, 2 * E), _row(tm, 2 * E)],
                 [_sds((S, D)), _sds((S, 2 * E), BF16), _sds((S, 2 * E), BF16)],
                 scratch=[pltpu.VMEM((tm, E), F32), pltpu.VMEM((tm, 2 * E), F32)], hosts=hosts
                 )(x0, win, b_in, lg, lb, ws, bsx, wout)


def _fwd_c1(x0, win, hosts=()):
    tm = 512

    def body(x_ref, w_ref, hc_ref):
        xb = x_ref[...].astype(BF16)
        for q in range(NQ):
            hc_ref[:, q * 768:(q + 1) * 768] = jnp.dot(xb, w_ref[q], preferred_element_type=F32)

    return _call("fwd_c1", body, (S // tm,), [_row(tm, D), _wspec(win)], _row(tm, 3 * D),
                 _sds((S, 3 * D)), hosts=hosts)(x0, win)


def _short_conv(buf, hc_ref, hcp_ref, wc_ref, tm, i):
    _fill_halo(buf, 0, HALO_C, lambda: hcp_ref[:, D:2 * D] * hcp_ref[:, 2 * D:], i == 0)
    buf[HALO_C:HALO_C + tm, :] = hc_ref[:, D:2 * D] * hc_ref[:, 2 * D:]
    y = wc_ref[0:1, :] * buf[pl.ds(HALO_C - 2, tm), :]
    for k in range(1, CONV_C):
        y = y + wc_ref[k:k + 1, :] * buf[pl.ds(HALO_C - 2 + k, tm), :]
    return y


def _fwd_c2(hc, x0, wc, wout, hosts=()):
    tm = 256

    def body(hc_ref, hcp_ref, x_ref, wc_ref, wout_ref, z_ref, buf):
        y = _short_conv(buf, hc_ref, hcp_ref, wc_ref, tm, pl.program_id(0))
        mb = (hc_ref[:, :D] * y).astype(BF16)
        z_ref[...] = ALPHA * x_ref[...] + jnp.dot(mb, wout_ref[...], preferred_element_type=F32)

    return _call("fwd_c2", body, (S // tm,),
                 [_row(tm, 3 * D), _prev(tm, HALO_C, 3 * D), _row(tm, D), _const((8, D)), _const(wout.shape)],
                 _row(tm, D), _sds((S, D)), scratch=[pltpu.VMEM((HALO_C + tm, D), F32)], hosts=hosts
                 )(hc, hc, x0, wc, wout)


def _fwd_ffn(z1, lg, lb, wgt, wut, wd, l, hosts=()):
    tm = 256

    def body(z_ref, lg_ref, lb_ref, wg_ref, wu_ref, wd_ref, o_ref, a_ref, u_ref, hm_ref):
        x1, _, _ = _ln(z_ref[...], lg_ref[...], lb_ref[...])
        xb = x1.astype(BF16)
        a = _mmt(xb, wg_ref[...])
        u = _mmt(xb, wu_ref[...])
        hmb = (a * _sig(a) * u).astype(BF16)
        a_ref[...] = a.astype(BF16)
        u_ref[...] = u.astype(BF16)
        hm_ref[...] = hmb
        o_ref[...] = ALPHA * x1 + jnp.dot(hmb, wd_ref[...], preferred_element_type=F32)

    return _call(f"fwd_ffn_{l}", body, (S // tm,),
                 [_row(tm, D), _const((1, D)), _const((1, D)), _const((FF, D)), _const((FF, D)), _const((FF, D))],
                 [_row(tm, D), _row(tm, FF), _row(tm, FF), _row(tm, FF)],
                 [_sds((S, D)), _sds((S, FF), BF16), _sds((S, FF), BF16), _sds((S, FF), BF16)],
                 hosts=hosts)(z1, lg, lb, wgt, wut, wd)


def _ple_parts(z2, p, lg, lb, wg_ref, wp_ref, pg):
    x2, xhat, rstd = _ln(z2, lg, lb)
    xb = x2.astype(BF16)
    gate = _sig(jnp.dot(xb, wg_ref[...], preferred_element_type=F32))
    pb = p.astype(BF16)
    qp = jnp.concatenate([jnp.dot(pb, wp_ref[q], preferred_element_type=F32) for q in range(NQ)], axis=1)
    rs = lax.rsqrt(jnp.mean(qp * qp, axis=-1, keepdims=True) + LN_EPS)
    qn = qp * rs
    return x2, xhat, rstd, xb, gate, qn, rs, qn * pg


def _fwd_ple(z2, p, lg, lb, wg, wp, pg, l, hosts=()):
    tm = 512

    def body(z_ref, p_ref, lg_ref, lb_ref, wg_ref, wp_ref, pg_ref, o_ref):
        x2, _, _, _, gate, _, _, r = _ple_parts(z_ref[...], p_ref[...], lg_ref[...], lb_ref[...], wg_ref, wp_ref,
                                                pg_ref[...])
        o_ref[...] = x2 + gate * r

    return _call(f"fwd_ple_{l}", body, (S // tm,),
                 [_row(tm, D), _row(tm, 256), _const((1, D)), _const((1, D)), _const(wg.shape), _wspec(wp),
                  _const((1, D))],
                 _row(tm, D), _sds((S, D)), hosts=hosts)(z2, p, lg, lb, wg, wp, pg)


def _loss_head(y, target):
    tm = 512

    def body(y_ref, t_ref, dy_ref, acc_ref):
        @pl.when(pl.program_id(0) == 0)
        def _():
            acc_ref[...] = jnp.zeros_like(acc_ref)

        e = y_ref[...] - t_ref[...]
        dy_ref[...] = e * (1.0 / D)
        acc_ref[0:1, :] += _colsum(e * e)

    return _call("loss_head", body, (S // tm,), [_row(tm, D), _row(tm, D)], [_row(tm, D), _acc(8, D)],
                 [_sds((S, D)), _sds((8, D))])(y, target)


def _zero_first(*refs):
    @pl.when(pl.program_id(0) == 0)
    def _():
        for r in refs:
            r[...] = jnp.zeros_like(r)


def _bwd_ple(g, z2, p, lg, lb, wg, wp, pg, l, hosts=()):
    tm = 256

    def body(g_ref, z_ref, p_ref, lg_ref, lb_ref, wg_ref, wp_ref, pg_ref, dz_ref, xb_ref, dgp_ref, dqp_ref, acc_ref):
        _zero_first(acc_ref)
        gin = g_ref[...]
        lgv, pgv = lg_ref[...], pg_ref[...]
        _, xhat, rstd, xb, gate, qn, rs, r = _ple_parts(z_ref[...], p_ref[...], lgv, lb_ref[...], wg_ref, wp_ref, pgv)
        xb_ref[...] = xb
        dgpb = (gin * r * gate * (1.0 - gate)).astype(BF16)
        dgp_ref[...] = dgpb
        dx2 = gin + _mmt(dgpb, wg_ref[...])
        dr = gin * gate
        acc_ref[0:1, :] += _colsum(dr * qn)
        t = dr * pgv
        dqp_ref[...] = (rs * (t - qn * jnp.mean(t * qn, axis=-1, keepdims=True))).astype(BF16)
        acc_ref[1:2, :] += _colsum(dx2 * xhat)
        acc_ref[2:3, :] += _colsum(dx2)
        dz_ref[...] = _ln_bwd(dx2 * lgv, xhat, rstd)

    return _call(f"bwd_ple_{l}", body, (S // tm,),
                 [_row(tm, D), _row(tm, D), _row(tm, 256), _const((1, D)), _const((1, D)), _const(wg.shape),
                  _wspec(wp), _const((1, D))],
                 [_row(tm, D), _row(tm, D), _row(tm, D), _row(tm, D), _acc(8, D)],
                 [_sds((S, D)), _sds((S, D), BF16), _sds((S, D), BF16), _sds((S, D), BF16), _sds((8, D))],
                 hosts=hosts)(g, z2, p, lg, lb, wg, wp, pg)


def _bwd_ffn(dz2, z1, ab, ub, lg, lb, wgt, wut, wd, l, hosts=()):
    tm = 256

    def body(dz2_ref, z_ref, a_ref, u_ref, lg_ref, lb_ref, wg_ref, wu_ref, wd_ref, dz1_ref, xb_ref, da_ref, du_ref,
             acc_ref):
        _zero_first(acc_ref)
        dz2v = dz2_ref[...]
        lgv = lg_ref[...]
        x1, xhat, rstd = _ln(z_ref[...], lgv, lb_ref[...])
        xb_ref[...] = x1.astype(BF16)
        a = a_ref[...].astype(F32)
        u = u_ref[...].astype(F32)
        sg = _sig(a)
        dhm = _mmt(dz2v, wd_ref[...])
        dub = (dhm * (a * sg)).astype(BF16)
        dab = (dhm * u * _silu_grad(a, sg)).astype(BF16)
        da_ref[...] = dab
        du_ref[...] = dub
        dx1 = ALPHA * dz2v + _mm(dab, wg_ref[...]) + _mm(dub, wu_ref[...])
        acc_ref[0:1, :] += _colsum(dx1 * xhat)
        acc_ref[1:2, :] += _colsum(dx1)
        dz1_ref[...] = _ln_bwd(dx1 * lgv, xhat, rstd)

    return _call(f"bwd_ffn_{l}", body, (S // tm,),
                 [_row(tm, D), _row(tm, D), _row(tm, FF), _row(tm, FF), _const((1, D)), _const((1, D)),
                  _const((FF, D)), _const((FF, D)), _const((FF, D))],
                 [_row(tm, D), _row(tm, D), _row(tm, FF), _row(tm, FF), _acc(8, D)],
                 [_sds((S, D)), _sds((S, D), BF16), _sds((S, FF), BF16), _sds((S, FF), BF16), _sds((8, D))],
                 hosts=hosts)(dz2, z1, ab, ub, lg, lb, wgt, wut, wd)


def _bwd_a2(dz1, cv, lg, lb, w2, l, hosts=()):
    tm = 512

    def body(dz_ref, cv_ref, lg_ref, lb_ref, w2_ref, dcv_ref, sb_ref, acc_ref):
        _zero_first(acc_ref)
        lgv = lg_ref[...]
        n, xhat, rstd = _ln(cv_ref[...], lgv, lb_ref[...])
        sg = _sig(n)
        sb_ref[...] = (n * sg).astype(BF16)
        dzb = dz_ref[...].astype(BF16)
        ds = _mmt(dzb, w2_ref[...])
        dn = ds * _silu_grad(n, sg)
        acc_ref[0:1, :] += _colsum(dn * xhat)
        acc_ref[1:2, :] += _colsum(dn)
        dcv = _ln_bwd(dn * lgv, xhat, rstd)
        acc_ref[2:3, :] += _colsum(dcv)
        dcv_ref[...] = dcv

    return _call(f"bwd_a2_{l}", body, (S // tm,),
                 [_row(tm, D), _row(tm, D), _const((1, D)), _const((1, D)), _const(w2.shape)],
                 [_row(tm, D), _row(tm, D), _acc(8, D)],
                 [_sds((S, D)), _sds((S, D), BF16), _sds((8, D))], hosts=hosts)(dz1, cv, lg, lb, w2)


def _bwd_conv_a(dcv, glu, wdw, l, hosts=()):
    tm = 256
    nb = S // tm

    def body(d_ref, dn_ref, g_ref, gp_ref, w_ref, dglu_ref, dw_ref, bufd, bufx, sh):
        i = pl.program_id(0)
        _zero_first(dw_ref)
        bufd[0:tm, :] = d_ref[...]
        _fill_halo(bufd, tm, HALO_A, lambda: dn_ref[...], i == nb - 1)
        _fill_halo(bufx, 0, HALO_A, lambda: gp_ref[...], i == 0)
        bufx[HALO_A:HALO_A + tm, :] = g_ref[...]
        _shift_copies(bufd, sh)
        _conv_rows(dglu_ref, w_ref, None, [CONV_A - 1 - k for k in range(CONV_A)], bufd, sh, tm)
        _shift_copies(bufx, sh)
        _conv_wgrad(dw_ref, d_ref, [HALO_A - (CONV_A - 1) + k for k in range(CONV_A)], bufx, sh, tm)

    return _call(f"bwd_conv_a_{l}", body, (nb,),
                 [_row(tm, D), _next(tm, HALO_A, D), _row(tm, D), _prev(tm, HALO_A, D), _const((32, D))],
                 [_row(tm, D), _acc(32, D)], [_sds((S, D)), _sds((32, D))],
                 scratch=[pltpu.VMEM((tm + HALO_A, D), F32), pltpu.VMEM((HALO_A + tm, D), F32),
                          pltpu.VMEM((SUB - 1, HALO_A + tm - SUB, D), F32)], hosts=hosts)(dcv, dcv, glu, glu, wdw)


def _bwd_a1(dglu, h, dz1, w1, l, hosts=()):
    tm = 256

    def body(dg_ref, h_ref, dz_ref, w_ref, dx_ref, dh_ref, acc_ref):
        _zero_first(acc_ref)
        a, g = h_ref[:, :D], h_ref[:, D:]
        sg = _sig(g)
        dgl = dg_ref[...]
        da = dgl * sg
        dg = dgl * a * sg * (1.0 - sg)
        acc_ref[0:1, 0:D] += _colsum(da)
        acc_ref[0:1, D:2 * D] += _colsum(dg)
        dh_ref[:, 0:D] = da.astype(BF16)
        dh_ref[:, D:2 * D] = dg.astype(BF16)
        dx = ALPHA * dz_ref[...]
        for q in range(NQ):
            dx = dx + _mmt(dh_ref[:, q * 512:(q + 1) * 512], w_ref[q])
        dx_ref[...] = dx

    return _call(f"bwd_a1_{l}", body, (S // tm,),
                 [_row(tm, D), _row(tm, 2 * D), _row(tm, D), _wspec(w1)],
                 [_row(tm, D), _row(tm, 2 * D), _acc(8, 2 * D)],
                 [_sds((S, D)), _sds((S, 2 * D), BF16), _sds((8, 2 * D))], hosts=hosts)(dglu, h, dz1, w1)


def _bwd_c2(dz1, hc, wc, wout):
    tm = 256

    def body(dz_ref, hc_ref, hcp_ref, wc_ref, wout_ref, dy_ref, dbg_ref, mb_ref, buf):
        y = _short_conv(buf, hc_ref, hcp_ref, wc_ref, tm, pl.program_id(0))
        dzb = dz_ref[...].astype(BF16)
        dm = _mmt(dzb, wout_ref[...])
        bg = hc_ref[:, :D]
        mb_ref[...] = (bg * y).astype(BF16)
        dbg_ref[...] = (dm * y).astype(BF16)
        dy_ref[...] = dm * bg

    return _call("bwd_c2", body, (S // tm,),
                 [_row(tm, D), _row(tm, 3 * D), _prev(tm, HALO_C, 3 * D), _const((8, D)), _const(wout.shape)],
                 [_row(tm, D), _row(tm, D), _row(tm, D)],
                 [_sds((S, D)), _sds((S, D), BF16), _sds((S, D), BF16)],
                 scratch=[pltpu.VMEM((HALO_C + tm, D), F32)])(dz1, hc, hc, wc, wout)


def _bwd_c1(dy, hc, dbg, dz1, wc, win):
    tm = 256
    nb = S // tm

    def body(d_ref, dn_ref, hc_ref, hcp_ref, dbg_ref, dz_ref, wc_ref, win_ref, dx_ref, dhc_ref, dwc_ref, bufd, bufq):
        i = pl.program_id(0)
        _zero_first(dwc_ref)
        bufd[0:tm, :] = d_ref[...]
        _fill_halo(bufd, tm, HALO_C, lambda: dn_ref[...], i == nb - 1)
        _fill_halo(bufq, 0, HALO_C, lambda: hcp_ref[:, D:2 * D] * hcp_ref[:, 2 * D:], i == 0)
        bufq[HALO_C:HALO_C + tm, :] = hc_ref[:, D:2 * D] * hc_ref[:, 2 * D:]
        dq = wc_ref[0:1, :] * bufd[pl.ds(CONV_C - 1, tm), :]
        for k in range(1, CONV_C):
            dq = dq + wc_ref[k:k + 1, :] * bufd[pl.ds(CONV_C - 1 - k, tm), :]
        dv = d_ref[...]
        for k in range(CONV_C):
            dwc_ref[k:k + 1, :] += _colsum(dv * bufq[pl.ds(HALO_C - (CONV_C - 1) + k, tm), :])
        dhc_ref[:, 0:D] = dbg_ref[...]
        dhc_ref[:, D:2 * D] = (dq * hc_ref[:, 2 * D:]).astype(BF16)
        dhc_ref[:, 2 * D:3 * D] = (dq * hc_ref[:, D:2 * D]).astype(BF16)
        dx = ALPHA * dz_ref[...]
        for q in range(NQ):
            dx = dx + _mmt(dhc_ref[:, q * 768:(q + 1) * 768], win_ref[q])
        dx_ref[...] = dx

    return _call("bwd_c1", body, (nb,),
                 [_row(tm, D), _next(tm, HALO_C, D), _row(tm, 3 * D), _prev(tm, HALO_C, 3 * D), _row(tm, D),
                  _row(tm, D), _const((8, D)), _wspec(win)],
                 [_row(tm, D), _row(tm, 3 * D), _acc(8, D)],
                 [_sds((S, D)), _sds((S, 3 * D), BF16), _sds((8, D))],
                 scratch=[pltpu.VMEM((tm + HALO_C, D), F32), pltpu.VMEM((HALO_C + tm, D), F32)]
                 )(dy, dy, hc, hc, dbg, dz1, wc, win)


def _bwd_b(dz1, zg, gg, lg, lb, win, wout, ws, wst, bsx):
    tm = 128
    nb = S // tm

    def body(dz_ref, zg_ref, gg_ref, lg_ref, lb_ref, win_ref, wout_ref, ws_ref, wst_ref, bsx_ref,
             dx_ref, dh_ref, mb_ref, acc_ref, dws_ref, dbs_ref, f_scr, dvn_scr):
        _zero_first(acc_ref, dws_ref, dbs_ref)
        lgv = lg_ref[...]
        u = zg_ref[:, :E].astype(F32)
        v = zg_ref[:, E:].astype(F32)
        vn, xhat, rstd = _ln(v, lgv, lb_ref[...])
        vnb = vn.astype(BF16)
        dzb = dz_ref[...].astype(BF16)
        dm = _mmt(dzb, wout_ref[...])
        mask, mask_t = _sgu_masks()
        for hd in range(SGU_H):
            wm = jnp.where(mask, ws_ref[hd], 0.0).astype(BF16)
            cs = slice(hd * SGU_G, (hd + 1) * SGU_G)
            for n in range(tm // SGU_T):
                rs = slice(n * SGU_T, (n + 1) * SGU_T)
                f_scr[rs, cs] = jnp.dot(wm, vnb[rs, cs], preferred_element_type=F32) + bsx_ref[hd]
        f = f_scr[...]
        mb_ref[...] = (u * f).astype(BF16)
        du = dm * f
        df = dm * u
        dfb = df.astype(BF16)
        for hd in range(SGU_H):
            wmt = jnp.where(mask_t, wst_ref[hd], 0.0).astype(BF16)
            cs = slice(hd * SGU_G, (hd + 1) * SGU_G)
            for n in range(tm // SGU_T):
                rs = slice(n * SGU_T, (n + 1) * SGU_T)
                dvn_scr[rs, cs] = jnp.dot(wmt, dfb[rs, cs], preferred_element_type=F32)
                dws_ref[hd] += lax.dot_general(dfb[rs, cs], vnb[rs, cs], (((1,), (1,)), ((), ())),
                                               preferred_element_type=F32)
                dbs_ref[hd] += df[rs, cs]
        dvn = dvn_scr[...]
        acc_ref[1:2, 0:E] += _colsum(dvn * xhat)
        acc_ref[2:3, 0:E] += _colsum(dvn)
        dv = _ln_bwd(dvn * lgv, xhat, rstd)
        dhu = du * gg_ref[:, :E].astype(F32)
        dhv = dv * gg_ref[:, E:].astype(F32)
        acc_ref[0:1, 0:E] += _colsum(dhu)
        acc_ref[0:1, E:2 * E] += _colsum(dhv)
        dh_ref[:, 0:E] = dhu.astype(BF16)
        dh_ref[:, E:2 * E] = dhv.astype(BF16)
        dx = ALPHA * dz_ref[...]
        for q in range(NQ):
            dx = dx + _mmt(dh_ref[:, q * 1024:(q + 1) * 1024], win_ref[q])
        dx_ref[...] = dx

        @pl.when(pl.program_id(0) == nb - 1)
        def _():
            for hd in range(SGU_H):
                dws_ref[hd] = jnp.where(mask, dws_ref[hd], 0.0)

    c3 = lambda a, b, c: pl.BlockSpec((a, b, c), lambda i: (0, 0, 0))
    return _call("bwd_b", body, (nb,),
                 [_row(tm, D), _row(tm, 2 * E), _row(tm, 2 * E), _const((1, E)), _const((1, E)), _wspec(win),
                  _const(wout.shape), _const((SGU_H, SGU_T, SGU_T)), _const((SGU_H, SGU_T, SGU_T)),
                  _const((SGU_H, SGU_T, SGU_G))],
                 [_row(tm, D), _row(tm, 2 * E), _row(tm, E), _acc(8, 2 * E), c3(SGU_H, SGU_T, SGU_T),
                  c3(SGU_H, SGU_T, SGU_G)],
                 [_sds((S, D)), _sds((S, 2 * E), BF16), _sds((S, E), BF16), _sds((8, 2 * E)),
                  _sds((SGU_H, SGU_T, SGU_T)), _sds((SGU_H, SGU_T, SGU_G))],
                 scratch=[pltpu.VMEM((tm, E), F32), pltpu.VMEM((tm, E), F32)]
                 )(dz1, zg, gg, lg, lb, win, wout, ws, wst, bsx)


def _mm_tn(name, a, amode, b, bmode, k, n, groups=NQ, hosts=()):
    def block_bytes(ts):
        ka = k if amode == "1" else groups * k
        nb = n if bmode == "1" else groups * n
        return 2 * (ts * ka * a.dtype.itemsize + ts * nb * b.dtype.itemsize + groups * k * n * 4)

    ts = min(1024 if block_bytes(1024) <= DW_BLOCK_BUDGET else 512, S)

    def spec(mode, w):
        if mode == "1":
            return pl.BlockSpec((ts, w), lambda s: (s, 0))
        if mode == "c":
            return pl.BlockSpec((ts, groups * w), lambda s: (s, 0))
        return pl.BlockSpec((groups, ts, w), lambda s: (0, s, 0))

    def pick(ref, mode, w, g):
        if mode == "1":
            return ref[...]
        if mode == "c":
            return ref[:, g * w:(g + 1) * w]
        return ref[g]

    def body(a_ref, b_ref, o_ref):
        _zero_first(o_ref)
        a_t = jnp.transpose(a_ref[...].astype(BF16)) if amode == "1" else None
        b_1 = b_ref[...].astype(BF16) if bmode == "1" else None
        for g in range(groups):
            lhs = a_t if amode == "1" else jnp.transpose(pick(a_ref, amode, k, g).astype(BF16))
            rhs = b_1 if bmode == "1" else pick(b_ref, bmode, n, g).astype(BF16)
            o_ref[0, g] += jnp.dot(lhs, rhs, preferred_element_type=F32)

    return _call(name, body, (S // ts,), [spec(amode, k), spec(bmode, n)],
                 pl.BlockSpec((1, groups, k, n), lambda s: (0, 0, 0, 0)), _sds((1, groups, k, n)), hosts=hosts)(a, b)


def _row_block(k, cap=256):
    return max(t for t in range(16, min(k, cap) + 1, 16) if k % t == 0)


def _cast_bf16(w, hosts=()):
    nl, k, n = w.shape
    tb = _row_block(k, 512)
    nb = k // tb

    def body(w_ref, o_ref):
        o_ref[...] = w_ref[...].astype(BF16)

    spec = pl.BlockSpec((None, tb, n), lambda i: (i // nb, i % nb, 0))
    return _call("cast_bf16", body, (nl * nb,), [spec], spec, _sds(w.shape, BF16), hosts=hosts)(w)


def _adam(name, w, m, v, gc, l, prev):
    nl, k, n = w.shape
    nc = gc.shape[0]
    tb = _row_block(k, 512)

    def body(w_ref, m_ref, v_ref, g_ref, *rest):
        go_ref, d_ref, mo_ref, vo_ref = rest[-4:]
        g = g_ref[0].astype(F32)
        for c in range(1, nc):
            g = g + g_ref[c].astype(F32)
        m2 = ADAM_B1 * m_ref[...] + (1.0 - ADAM_B1) * g
        v2 = ADAM_B2 * v_ref[...] + (1.0 - ADAM_B2) * (g * g)
        m_hat = m2 / (1.0 - ADAM_B1 ** ADAM_STEP)
        v_hat = v2 / (1.0 - ADAM_B2 ** ADAM_STEP)
        go_ref[...] = g
        d_ref[...] = -ADAM_LR * (m_hat / (jnp.sqrt(v_hat) + ADAM_EPS) + ADAM_WD * w_ref[...])
        mo_ref[...] = m2
        vo_ref[...] = v2

    spec = pl.BlockSpec((None, tb, n), lambda i: (l, i, 0))
    gspec = pl.BlockSpec((nc, None, tb, n), lambda i: (0, 0, i, 0))
    in_specs, args, aliases = [spec, spec, spec, gspec], [w, m, v, gc], {}
    if prev is not None:
        in_specs += [pl.BlockSpec(memory_space=pl.ANY)] * 4
        args += list(prev)
        aliases = {4 + j: j for j in range(4)}
    return _call(name, body, (k // tb,), in_specs, [spec] * 4, [_sds(w.shape)] * 4, aliases=aliases)(*args)


def _sum8(name, g8):
    r = g8.shape[1]

    def body(g_ref, o_ref):
        acc = g_ref[0]
        for d in range(1, 8):
            acc = acc + g_ref[d]
        o_ref[...] = acc

    return _call(name, body, (1,), [pl.BlockSpec((8, r, 128), lambda i: (0, 0, 0))],
                 pl.BlockSpec((r, 128), lambda i: (0, 0)), _sds((r, 128)))(g8)


def _place():
    x, y, c = lax.axis_index("x"), lax.axis_index("y"), lax.axis_index("c")
    return x, y, c, 2 * x + y, (x, y, 1 - c), [(1 - x, y), (x, 1 - y), (1 - x, 1 - y)]


class _Exchange:
    def __init__(self, arrays, out_shapes):
        self.arrays, self.out_shapes = list(arrays), list(out_shapes)
        n = len(self.arrays)
        self.sems = [pltpu.SemaphoreType.DMA((7 * n,)), pltpu.SemaphoreType.DMA((7 * n,)),
                     pltpu.SemaphoreType.DMA((n,))]

    def _copies(self, ins, outs, sems):
        send, recv, lsem = sems
        local_src, remote_src, dst = self.maps(ins, outs)
        x, y, c, q, sib, chips = _place()

        def rcopy(w, k, qq, cc, to, src=None):
            return pltpu.make_async_remote_copy(
                src_ref=dst(w, qq, cc) if src is None else src, dst_ref=dst(w, qq, cc),
                send_sem=send.at[7 * w + k], recv_sem=recv.at[7 * w + k], device_id=to, device_id_type=MESH)

        def mine(w):
            return pltpu.make_async_copy(local_src(w), dst(w, q, c), lsem.at[w])

        def first(w):
            return [rcopy(w, 0, q, c, sib, local_src(w))] + [
                rcopy(w, 1 + j, q, c, (cx, cy, c), remote_src(w, 2 * cx + cy)) for j, (cx, cy) in enumerate(chips)]

        return rcopy, mine, first, (x, y, c), q, c, sib, chips

    def start(self, ins, outs, sems):
        _, mine, first, *_ = self._copies(ins, outs, sems)
        for w in range(len(self.arrays)):
            mine(w).start()
            for cp in first(w):
                cp.start()

    def forward_steps(self, n_steps):
        sizes = [a.size // a.shape[0] for a in self.arrays]
        plan, moved = {}, 0
        for w, size in enumerate(sizes):
            moved += size
            plan.setdefault(min(n_steps - 1, -(-moved * n_steps // sum(sizes))), []).append(w)
        return plan

    def forward(self, ws, ins, outs, sems):
        rcopy, _, _, me, _, c, sib, chips = self._copies(ins, outs, sems)
        for w in ws:
            for j, (cx, cy) in enumerate(chips):
                rcopy(w, 1 + j, 2 * cx + cy, c, me).wait_recv()
                rcopy(w, 4 + j, 2 * cx + cy, c, sib).start()

    def complete(self, ins, outs, sems):
        rcopy, mine, first, me, q, c, sib, chips = self._copies(ins, outs, sems)
        n = len(self.arrays)
        for w in range(n):
            rcopy(w, 0, q, 1 - c, me).wait_recv()
            for j, (cx, cy) in enumerate(chips):
                rcopy(w, 4 + j, 2 * cx + cy, 1 - c, me).wait_recv()
        for w in range(n):
            for cp in first(w):
                cp.wait_send()
            for j, (cx, cy) in enumerate(chips):
                rcopy(w, 4 + j, 2 * cx + cy, c, sib).wait_send()
            mine(w).wait()


class _GatherWeights(_Exchange):
    def __init__(self, items):
        self.layers = [l for _, l in items]
        self.kh = [s.shape[1] // 2 for s, _ in items]
        super().__init__([s for s, _ in items], [_sds((NQ, 1) + s.shape[1:], BF16) for s, _ in items])

    def maps(self, ins, outs):
        c = lax.axis_index("c")
        src = lambda w: ins[w].at[pl.ds(self.layers[w], 1), pl.ds(c * self.kh[w], self.kh[w]), :]
        return src, lambda w, q: src(w), lambda w, q, cc: outs[w].at[q, :, pl.ds(cc * self.kh[w], self.kh[w]), :]


class _ScatterPartials(_Exchange):
    def __init__(self, parts):
        super().__init__(parts, [_sds((NQ, 1, 2) + p.shape[2:], BF16) for p in parts])

    def maps(self, ins, outs):
        q = 2 * lax.axis_index("x") + lax.axis_index("y")
        return (lambda w: ins[w].at[:, q]), (lambda w, qq: ins[w].at[:, qq]), (lambda w, qq, cc: outs[w].at[qq, :, cc])


class _Gather8(_Exchange):
    def __init__(self, v):
        super().__init__([v], [_sds((8,) + v.shape)])

    def maps(self, ins, outs):
        return (lambda w: ins[0]), (lambda w, q: ins[0]), (lambda w, q, cc: outs[0].at[2 * q + cc])


class _SwapHalves:
    def __init__(self, dws):
        self.arrays = list(dws)
        self.kh = [d.shape[2] // 2 for d in dws]
        self.out_shapes = [_sds(d.shape[:2] + (kh,) + d.shape[3:]) for d, kh in zip(dws, self.kh)]
        self.sems = [pltpu.SemaphoreType.DMA((len(dws),)), pltpu.SemaphoreType.DMA((len(dws),))]

    def _copies(self, ins, outs, sems):
        send, recv = sems
        _, _, c, _, sib, _ = _place()
        return [pltpu.make_async_remote_copy(
            src_ref=ins[w].at[:, :, pl.ds((1 - c) * self.kh[w], self.kh[w]), :], dst_ref=outs[w],
            send_sem=send.at[w], recv_sem=recv.at[w], device_id=sib, device_id_type=MESH)
            for w in range(len(self.arrays))]

    def start(self, ins, outs, sems):
        for cp in self._copies(ins, outs, sems):
            cp.start()

    def forward_steps(self, n_steps):
        return {}

    def complete(self, ins, outs, sems):
        for cp in self._copies(ins, outs, sems):
            cp.wait()


def _comm_only(name, host):
    n_in, n_out = len(host.arrays), len(host.out_shapes)

    def body(*refs):
        ins, outs, sems = refs[:n_in], refs[n_in:n_in + n_out], refs[n_in + n_out:]
        host.start(ins, outs, sems)
        for ws in host.forward_steps(1).values():
            host.forward(ws, ins, outs, sems)
        host.complete(ins, outs, sems)

    any_spec = pl.BlockSpec(memory_space=pl.ANY)
    return pl.pallas_call(body, name=name, in_specs=[any_spec] * n_in, out_specs=[any_spec] * n_out,
                          out_shape=host.out_shapes, scratch_shapes=host.sems)(*host.arrays)


def _add_halves(dw, got, cidx):
    nl, _, k, n = dw.shape
    kh = k // 2
    qb = 2

    def body(c_ref, a_ref, b_ref, o_ref):
        o_ref[...] = (a_ref[...] + b_ref[...]).astype(BF16)

    grid_spec = pltpu.PrefetchScalarGridSpec(
        num_scalar_prefetch=1, grid=(nl, NQ // qb),
        in_specs=[pl.BlockSpec((None, qb, None, kh, n), lambda l, q, c_ref: (l, q, c_ref[0], 0, 0)),
                  pl.BlockSpec((None, qb, kh, n), lambda l, q, c_ref: (l, q, 0, 0))],
        out_specs=pl.BlockSpec((None, qb, kh, n), lambda l, q, c_ref: (l, q, 0, 0)))
    return pl.pallas_call(
        body, name="add_halves", grid_spec=grid_spec, out_shape=_sds((nl, NQ, kh, n), BF16),
        compiler_params=pltpu.CompilerParams(dimension_semantics=("arbitrary", "arbitrary"),
                                             vmem_limit_bytes=VMEM_LIMIT))(cidx, dw.reshape(nl, NQ, 2, kh, n), got)


def _gather8(name, v):
    return _comm_only(name, _Gather8(v))[0]


PACK = 16 * 128


def _pack(arrays):
    parts = []
    for a in arrays:
        flat = a.reshape(-1)
        parts.append(jnp.pad(flat, (0, (-flat.shape[0]) % PACK)))
    return jnp.concatenate(parts).reshape(-1, 128)


def _unpack(packed, shapes):
    flat = packed.reshape(-1)
    out, off = [], 0
    for shp in shapes:
        size = 1
        for d in shp:
            size *= d
        out.append(flat[off:off + size].reshape(shp))
        off += size + (-size) % PACK
    return out


def kernel(x, p, a_w_pw1, a_b_pw1, a_w_dw, a_b_dw, a_ln_g, a_ln_b, a_w_pw2, b_w_in, b_b_in, b_ln_g, b_ln_b, b_w_s, b_b_s, b_w_out, c_w_in, c_w_conv, c_w_out, ln1_g, ln1_b, ln2_g, ln2_b, ffn_w_gate, ffn_w_up, ffn_w_down, ple_w_gate, ple_w_proj, ple_norm_g, loss_target, m_a_w_pw1, m_a_b_pw1, m_a_w_dw, m_a_b_dw, m_a_ln_g, m_a_ln_b, m_a_w_pw2, m_b_w_in, m_b_b_in, m_b_ln_g, m_b_ln_b, m_b_w_s, m_b_b_s, m_b_w_out, m_c_w_in, m_c_w_conv, m_c_w_out, m_ln1_g, m_ln1_b, m_ln2_g, m_ln2_b, m_ffn_w_gate, m_ffn_w_up, m_ffn_w_down, m_ple_w_gate, m_ple_w_proj, m_ple_norm_g, v_a_w_pw1, v_a_b_pw1, v_a_w_dw, v_a_b_dw, v_a_ln_g, v_a_ln_b, v_a_w_pw2, v_b_w_in, v_b_b_in, v_b_ln_g, v_b_ln_b, v_b_w_s, v_b_b_s, v_b_w_out, v_c_w_in, v_c_w_conv, v_c_w_out, v_ln1_g, v_ln1_b, v_ln2_g, v_ln2_b, v_ffn_w_gate, v_ffn_w_up, v_ffn_w_down, v_ple_w_gate, v_ple_w_proj, v_ple_norm_g):
    args = dict(locals())
    wts = {k: args[k] for k in WEIGHTS}
    mom = {k: args["m_" + k] for k in WEIGHTS}
    var = {k: args["v_" + k] for k in WEIGHTS}
    for k in TRANSPOSED:
        wts[k], mom[k], var[k] = (jnp.transpose(t[k], (0, 2, 1)) for t in (wts, mom, var))
    q_idx = 2 * lax.axis_index("x") + lax.axis_index("y")
    c_idx = lax.axis_index("c").astype(jnp.int32).reshape(1)

    wb = {k: _cast_bf16(wts[k]) for k in BIG if k not in ("ffn_w_gate", "ffn_w_up")}
    mixw = [[("a_w_pw1", 0), ("a_w_pw2", 0)], [("b_w_in", 0), ("b_w_out", 0)], [("c_w_in", 0), ("c_w_out", 0)],
            [("a_w_pw1", 1), ("a_w_pw2", 1)]]
    ffnw = [[("ffn_w_gate", l), ("ffn_w_up", l), ("ffn_w_down", l)] for l in range(DEPTH)]
    plew = [[("ple_w_gate", l), ("ple_w_proj", l)] for l in range(DEPTH)]
    fwd_plan = {("a1", 0): mixw[0][1:] + plew[0], ("a2", 0): ffnw[0], ("ffn", 0): mixw[1], ("ple", 0): plew[1],
                ("b", 1): ffnw[1], ("ffn", 1): mixw[2] + ffnw[2][:1], ("ple", 1): plew[2],
                ("c1", 2): ffnw[2][1:2], ("c2", 2): ffnw[2][2:], ("ffn", 2): mixw[3] + ffnw[3][:1], ("ple", 2): plew[3],
                ("a1", 3): ffnw[3][1:2], ("a2", 3): ffnw[3][2:]}
    gw = {}

    def gather(keys):
        return _GatherWeights([(wb[name], l) for name, l in keys])

    def hosted(tag, fn, *fargs):
        keys = fwd_plan.get(tag)
        if not keys:
            return fn(*fargs)
        own, (got,) = fn(*fargs, hosts=[gather(keys)])
        store(keys, got)
        return own

    def store(keys, got):
        for (name, l), arr in zip(keys, got):
            gw[name, l] = arr.reshape(NQ * arr.shape[2], arr.shape[3]) if name in ROW_SHARDED else arr

    first_keys = mixw[0][:1]
    wb["ffn_w_gate"], (got,) = _cast_bf16(wts["ffn_w_gate"], hosts=[gather(first_keys)])
    store(first_keys, got)
    shard_shapes = [wts[k].shape for k in SMALL_SHARDED]
    wb["ffn_w_up"], ((small8,),) = _cast_bf16(wts["ffn_w_up"], hosts=[_Gather8(_pack([wts[k] for k in SMALL_SHARDED]))])
    per_chip = [_unpack(small8[2 * qq], shard_shapes) for qq in range(NQ)]
    full = {k: jnp.concatenate([per_chip[qq][i] for qq in range(NQ)], axis=-1) for i, k in enumerate(SMALL_SHARDED)}
    for k in SMALL_REPL:
        full[k] = wts[k]

    def vec(name, l):
        return full[name][l][None, :]

    def conv_w(name, l, rows):
        w = full[name][l]
        return jnp.pad(w, ((0, rows - w.shape[0]), (0, 0)))

    ws = full["b_w_s"][0]
    wst = jnp.transpose(ws, (0, 2, 1))
    bsx = jnp.broadcast_to(full["b_b_s"][0][:, :, None], (SGU_H, SGU_T, SGU_G))

    x0s, z1s, z2s, saved, ffn_saved = [], [], [], [], []
    cur = x[0]
    for i in range(DEPTH):
        mix, j = i % 3, i // 3
        x0s.append(cur)
        if mix == 0:
            h, glu = hosted(("a1", i), _fwd_a1, cur, gw["a_w_pw1", j], vec("a_b_pw1", j), i)
            z1, cv = hosted(("a2", i), _fwd_a2, glu, cur, conv_w("a_w_dw", j, 32), vec("a_b_dw", j), vec("a_ln_g", j),
                            vec("a_ln_b", j), gw["a_w_pw2", j], i)
            saved.append((h, glu, cv))
        elif mix == 1:
            z1, zg, gg = hosted(("b", i), _fwd_b, cur, gw["b_w_in", 0], vec("b_b_in", 0), vec("b_ln_g", 0),
                                vec("b_ln_b", 0), ws, bsx, gw["b_w_out", 0])
            saved.append((zg, gg))
        else:
            hc = hosted(("c1", i), _fwd_c1, cur, gw["c_w_in", 0])
            z1 = hosted(("c2", i), _fwd_c2, hc, cur, conv_w("c_w_conv", 0, 8), gw["c_w_out", 0])
            saved.append((hc,))
        z2, ab, ub, hm = hosted(("ffn", i), _fwd_ffn, z1, vec("ln1_g", i), vec("ln1_b", i), gw["ffn_w_gate", i],
                                gw["ffn_w_up", i], gw["ffn_w_down", i], i)
        ffn_saved.append((ab, ub, hm))
        cur = hosted(("ple", i), _fwd_ple, z2, p[i, 0], vec("ln2_g", i), vec("ln2_b", i), gw["ple_w_gate", i],
                     gw["ple_w_proj", i], vec("ple_norm_g", i), i)
        z1s.append(z1)
        z2s.append(z2)

    g, loss_acc = _loss_head(cur, loss_target[0])
    loss = lax.psum(0.5 / D * jnp.sum(loss_acc[0]), ("x", "y", "c"))

    dws = {}
    sg = {}
    res = {k: None for k in BIG}

    def wgrad(name, l, a, amode, b, bmode, scatter_keys=(), gather_small=None):
        _, k, n = wts[name].shape
        hosts = [_ScatterPartials([parts[key] for key in scatter_keys])] if scatter_keys else []
        if gather_small is not None:
            hosts = [_Gather8(gather_small)]
        if name in ROW_SHARDED:
            out = _mm_tn(f"dw_{name}_{l}", a, "1", b, "1", NQ * k, n, groups=1, hosts=hosts)
        else:
            out = _mm_tn(f"dw_{name}_{l}", a, amode, b, bmode, k, n, hosts=hosts)
        got = None
        if hosts:
            out, (got,) = out
        if scatter_keys:
            update(scatter_keys, got)
        dws[name, l] = out.reshape(1, NQ, k, n)
        return got

    def swap(keys):
        return _SwapHalves([dws[k] for k in keys])

    parts = {}

    def add_halves(keys, got):
        parts.update((k, _add_halves(dws[k], r, c_idx)) for k, r in zip(keys, got))

    def update(keys, contribs):
        for (name, l), gc in zip(keys, contribs):
            _, kq, n = wts[name].shape
            res[name] = _adam(f"adam_{name}_{l}", wts[name], mom[name], var[name], gc.reshape(NQ, 1, kq, n), l,
                              res[name])

    small = SMALL_SHARDED + SMALL_REPL
    late_small = [("a_ln_g", 0), ("a_ln_b", 0), ("a_b_dw", 0), ("a_w_dw", 0), ("a_b_pw1", 0)]
    early_small = [(k, l) for k in small for l in range(full[k].shape[0]) if (k, l) not in late_small]
    pending = None
    for i in reversed(range(DEPTH)):
        mix, j = i % 3, i // 3
        ple_args = (g, z2s[i], p[i, 0], vec("ln2_g", i), vec("ln2_b", i), gw["ple_w_gate", i], gw["ple_w_proj", i],
                    vec("ple_norm_g", i), i)
        if pending:
            (dz2, x2b, dgp, dqp, acc), (got,) = _bwd_ple(*ple_args, hosts=[swap(pending)])
            add_halves(pending, got)
        else:
            dz2, x2b, dgp, dqp, acc = _bwd_ple(*ple_args)
        sg["ple_norm_g", i], sg["ln2_g", i], sg["ln2_b", i] = acc[0], acc[1], acc[2]
        wgrad("ple_w_gate", i, x2b, "c", dgp, "1")
        wgrad("ple_w_proj", i, p[i, 0], "1", dqp, "c")
        ab, ub, hm = ffn_saved[i]
        ffn_args = (dz2, z1s[i], ab, ub, vec("ln1_g", i), vec("ln1_b", i), gw["ffn_w_gate", i], gw["ffn_w_up", i],
                    gw["ffn_w_down", i], i)
        if pending:
            (dz1, x1b, da, du, acc), (contribs,) = _bwd_ffn(
                *ffn_args, hosts=[_ScatterPartials([parts[key] for key in ffnw[i + 1]])])
            update(ffnw[i + 1], contribs)
        else:
            dz1, x1b, da, du, acc = _bwd_ffn(*ffn_args)
        sg["ln1_g", i], sg["ln1_b", i] = acc[0], acc[1]
        wgrad("ffn_w_gate", i, da, "1", x1b, "1", scatter_keys=mixw[i + 1][1:] + plew[i + 1] if pending else ())
        if i == 0:
            (g8_early,) = wgrad("ffn_w_up", i, du, "1", x1b, "1", gather_small=_pack([sg[pc] for pc in early_small]))
        else:
            wgrad("ffn_w_up", i, du, "1", x1b, "1")
        wgrad("ffn_w_down", i, hm, "1", dz2, "1", scatter_keys=mixw[i + 1][:1] if pending else ())
        x0 = x0s[i]
        if mix == 0:
            h, glu, cv = saved[i]
            a2_args = (dz1, cv, vec("a_ln_g", j), vec("a_ln_b", j), gw["a_w_pw2", j], i)
            conv_args = (glu, conv_w("a_w_dw", j, 32), i)
            if i == 0:
                early = ffnw[0] + plew[0]
                (dcv, sb, acc), (got,) = _bwd_a2(*a2_args, hosts=[swap(early)])
                sg["a_ln_g", j], sg["a_ln_b", j], sg["a_b_dw", j] = acc[0], acc[1], acc[2]
                add_halves(early, got)
                wgrad("a_w_pw2", j, sb, "c", dz1, "1")
                (dglu, dwdw), (contribs, got) = _bwd_conv_a(
                    dcv, *conv_args, hosts=[_ScatterPartials([parts[key] for key in early]), swap(mixw[0][1:])])
                update(early, contribs)
                add_halves(mixw[0][1:], got)
                (g, dh, acc), (contribs,) = _bwd_a1(dglu, h, dz1, gw["a_w_pw1", j], i,
                                                    hosts=[_ScatterPartials([parts[key] for key in mixw[0][1:]])])
                update(mixw[0][1:], contribs)
            else:
                dcv, sb, acc = _bwd_a2(*a2_args)
                sg["a_ln_g", j], sg["a_ln_b", j], sg["a_b_dw", j] = acc[0], acc[1], acc[2]
                dglu, dwdw = _bwd_conv_a(dcv, *conv_args)
                wgrad("a_w_pw2", j, sb, "c", dz1, "1")
                g, dh, acc = _bwd_a1(dglu, h, dz1, gw["a_w_pw1", j], i)
            sg["a_w_dw", j] = dwdw[:CONV_A]
            sg["a_b_pw1", j] = acc[0]
            wgrad("a_w_pw1", j, x0, "1", dh, "c")
        elif mix == 1:
            zg, gg = saved[i]
            g, dh, mb, acc, dw_s, db_s = _bwd_b(dz1, zg, gg, vec("b_ln_g", 0), vec("b_ln_b", 0), gw["b_w_in", 0],
                                                gw["b_w_out", 0], ws, wst, bsx)
            sg["b_b_in", 0], sg["b_ln_g", 0], sg["b_ln_b", 0] = acc[0], acc[1, :E], acc[2, :E]
            sg["b_w_s", 0], sg["b_b_s", 0] = dw_s, jnp.sum(db_s, axis=-1)
            wgrad("b_w_out", 0, mb, "c", dz1, "1")
            wgrad("b_w_in", 0, x0, "1", dh, "c")
        else:
            (hc,) = saved[i]
            wc = conv_w("c_w_conv", 0, 8)
            dy, dbg, mb = _bwd_c2(dz1, hc, wc, gw["c_w_out", 0])
            wgrad("c_w_out", 0, mb, "c", dz1, "1")
            g, dhc, dwc = _bwd_c1(dy, hc, dbg, dz1, wc, gw["c_w_in", 0])
            sg["c_w_conv", 0] = dwc[:CONV_C]
            wgrad("c_w_in", 0, x0, "1", dhc, "c")
        pending = mixw[i] + ffnw[i] + plew[i] if i > 0 else mixw[0][:1]
    grad_x = g[None]
    add_halves(pending, _comm_only("swap_last", swap(pending)))
    update(pending, _comm_only("scatter_last", _ScatterPartials([parts[key] for key in pending])))

    g8_late = _gather8("gather_small_late", _pack([sg[pc] for pc in late_small]))
    sums = dict(zip(early_small, _unpack(_sum8("sum8_early", g8_early), [sg[pc].shape for pc in early_small])))
    sums.update(zip(late_small, _unpack(_sum8("sum8_late", g8_late), [sg[pc].shape for pc in late_small])))
    gsum = [jnp.stack([sums[k, l] for l in range(full[k].shape[0])]) for k in small]
    gmine = []
    for k, gs in zip(small, gsum):
        if k in SMALL_SHARDED:
            wdt = wts[k].shape[-1]
            gs = lax.dynamic_slice_in_dim(gs, q_idx * wdt, wdt, axis=gs.ndim - 1)
        gmine.append(gs)
    packed = [_pack(t)[None] for t in ([wts[k] for k in small], [mom[k] for k in small], [var[k] for k in small])]
    outs = _adam("adam_small", packed[0], packed[1], packed[2], _pack(gmine)[None, None], 0, None)
    unpacked = [_unpack(o[0], [wts[k].shape for k in small]) for o in outs]
    for i, k in enumerate(small):
        res[k] = tuple(u[i] for u in unpacked)

    for k in TRANSPOSED:
        res[k] = tuple(jnp.transpose(r, (0, 2, 1)) for r in res[k])
    return (loss, grad_x, *[res[k][0] for k in WEIGHTS], *[res[k][1] for k in WEIGHTS],
            *[res[k][2] for k in WEIGHTS], *[res[k][3] for k in WEIGHTS])
```

```python
import functools

import jax
import jax.numpy as jnp
from jax import lax
from jax.experimental import pallas as pl
from jax.experimental.pallas import tpu as pltpu

F32, BF16 = jnp.float32, jnp.bfloat16
S = 4096
D = 1024
E = 2048
FF = 2816
FQ = FF // 4
NQ = 4
DEPTH = 4
ALPHA = (2 * DEPTH) ** 0.25
LN_EPS = 1e-5
CONV_A, CONV_C = 31, 3
HALO_A, HALO_C = 32, 8
SGU_T, SGU_H, SGU_G, SGU_CHUNK = 128, 8, 256, 64
VMEM_LIMIT = 56 * 1024 * 1024
DW_BLOCK_BUDGET = 40 * 1024 * 1024
MESH = pl.DeviceIdType.MESH
ADAM_LR, ADAM_B1, ADAM_B2, ADAM_EPS, ADAM_WD, ADAM_STEP = 0.001, 0.9, 0.999, 1e-08, 0.01, 10
GELU_C, GELU_A = 0.7978845608028654, 0.044715

BIG = ["a_w_pw1", "a_w_pw2", "b_w_in", "b_w_out", "c_w_in", "c_w_out",
       "ffn_w_gate", "ffn_w_up", "ffn_w_down", "ple_w_gate", "ple_w_proj"]
TRANSPOSED = ["ffn_w_gate", "ffn_w_up"]
ROW_SHARDED = ["a_w_pw2", "b_w_out", "c_w_out", "ffn_w_gate", "ffn_w_up", "ffn_w_down", "ple_w_gate"]
SMALL_SHARDED = ["a_b_pw1", "a_w_dw", "a_b_dw", "a_ln_g", "a_ln_b", "c_w_conv"]
SMALL_REPL = ["b_b_in", "b_ln_g", "b_ln_b", "b_w_s", "b_b_s", "ln1_g", "ln1_b", "ln2_g", "ln2_b", "ple_norm_g"]
WEIGHTS = ["a_w_pw1", "a_b_pw1", "a_w_dw", "a_b_dw", "a_ln_g", "a_ln_b", "a_w_pw2", "b_w_in", "b_b_in", "b_ln_g",
           "b_ln_b", "b_w_s", "b_b_s", "b_w_out", "c_w_in", "c_w_conv", "c_w_out", "ln1_g", "ln1_b", "ln2_g",
           "ln2_b", "ffn_w_gate", "ffn_w_up", "ffn_w_down", "ple_w_gate", "ple_w_proj", "ple_norm_g"]


def _call(name, body, grid, in_specs, out_specs, out_shape, scratch=(), aliases=None, hosts=()):
    params = pltpu.CompilerParams(dimension_semantics=("arbitrary",) * len(grid), vmem_limit_bytes=VMEM_LIMIT)
    if not hosts:
        return pl.pallas_call(
            body, name=name, grid=grid, in_specs=in_specs, out_specs=out_specs, out_shape=out_shape,
            scratch_shapes=list(scratch), input_output_aliases=aliases or {}, compiler_params=params)
    assert len(grid) == 1 and not aliases
    single = not isinstance(out_shape, (list, tuple))
    own_shapes = [out_shape] if single else list(out_shape)
    own_specs = [out_specs] if single else list(out_specs)
    n_in, n_out, n_scr = len(in_specs), len(own_shapes), len(scratch)
    h_in = [len(h.arrays) for h in hosts]
    h_out = [len(h.out_shapes) for h in hosts]
    h_sem = [len(h.sems) for h in hosts]

    def split(refs, counts):
        out, off = [], 0
        for cnt in counts:
            out.append(refs[off:off + cnt])
            off += cnt
        return out

    def wrapped(*refs):
        ins, hin, outs, hout, scr, hsem = split(refs, [n_in, sum(h_in), n_out, sum(h_out), n_scr, sum(h_sem)])
        per_host = list(zip(hosts, split(hin, h_in), split(hout, h_out), split(hsem, h_sem)))

        @pl.when(pl.program_id(0) == 0)
        def _():
            for h, a, o, s in per_host:
                h.start(a, o, s)

        body(*ins, *outs, *scr)

        for h, a, o, s in per_host:
            for step, ws in sorted(h.forward_steps(grid[0]).items()):
                pl.when(pl.program_id(0) == step)(functools.partial(h.forward, ws, a, o, s))

        @pl.when(pl.program_id(0) == grid[0] - 1)
        def _():
            for h, a, o, s in per_host:
                h.complete(a, o, s)

    any_spec = pl.BlockSpec(memory_space=pl.ANY)
    call = pl.pallas_call(
        wrapped, name=name, grid=grid, in_specs=list(in_specs) + [any_spec] * sum(h_in),
        out_specs=own_specs + [any_spec] * sum(h_out),
        out_shape=own_shapes + [s for h in hosts for s in h.out_shapes],
        scratch_shapes=list(scratch) + [s for h in hosts for s in h.sems], compiler_params=params)

    def run(*args):
        res = call(*args, *[a for h in hosts for a in h.arrays])
        own = res[0] if single else list(res[:n_out])
        return own, split(list(res[n_out:]), h_out)

    return run


def _sds(shape, dtype=F32):
    return jax.ShapeDtypeStruct(shape, dtype)


def _row(tm, c):
    return pl.BlockSpec((tm, c), lambda i: (i, 0))


def _grow(g, tm, c):
    return pl.BlockSpec((g, tm, c), lambda i: (0, i, 0))


def _const(shape):
    nd = len(shape)
    return pl.BlockSpec(shape, lambda i: (0,) * nd, pipeline_mode=pl.Buffered(1))


def _wspec(w):
    return pl.BlockSpec((NQ, None, w.shape[2], w.shape[3]), lambda i: (0, 0, 0, 0), pipeline_mode=pl.Buffered(1))


def _prev(tm, hb, c):
    return pl.BlockSpec((hb, c), lambda i: (jnp.maximum(i * (tm // hb) - 1, 0), 0))


def _next(tm, hb, c):
    return pl.BlockSpec((hb, c), lambda i: (jnp.minimum((i + 1) * (tm // hb), S // hb - 1), 0))


def _acc(r, c):
    return pl.BlockSpec((r, c), lambda i: (0, 0))


def _sig(x):
    return 1.0 / (1.0 + jnp.exp(-x))


def _ln(z, g, b):
    mu = jnp.mean(z, axis=-1, keepdims=True)
    zc = z - mu
    rstd = lax.rsqrt(jnp.mean(zc * zc, axis=-1, keepdims=True) + LN_EPS)
    xhat = zc * rstd
    return xhat * g + b, xhat, rstd


def _ln_bwd(dyg, xhat, rstd):
    return rstd * (dyg - jnp.mean(dyg, axis=-1, keepdims=True) - xhat * jnp.mean(dyg * xhat, axis=-1, keepdims=True))


def _mm(a, w):
    return jnp.dot(a.astype(BF16), w, preferred_element_type=F32)


def _mmt(a, w):
    return lax.dot_general(a.astype(BF16), w, (((1,), (1,)), ((), ())), preferred_element_type=F32)


def _colsum(x):
    return jnp.sum(x, axis=0, keepdims=True)


def _gelu_and_grad(x):
    x2 = x * x
    t = jnp.tanh(x * (GELU_C + (GELU_C * GELU_A) * x2))
    hx = 0.5 * x
    return hx + hx * t, 0.5 + 0.5 * t + hx * (1.0 - t * t) * (GELU_C + (3.0 * GELU_C * GELU_A) * x2)


def _silu_grad(a, sg):
    return sg * (1.0 + a * (1.0 - sg))


def _sgu_masks():
    r = lax.broadcasted_iota(jnp.int32, (SGU_T, SGU_T), 0) // SGU_CHUNK
    c = lax.broadcasted_iota(jnp.int32, (SGU_T, SGU_T), 1) // SGU_CHUNK
    return r >= c, c >= r


def _fill_halo(buf, lo, n, halo_val_fn, is_edge):
    @pl.when(is_edge)
    def _():
        buf[lo:lo + n, :] = jnp.zeros((n, buf.shape[1]), F32)

    @pl.when(jnp.logical_not(is_edge))
    def _():
        buf[lo:lo + n, :] = halo_val_fn()


SUB, LANE = 8, 128
ROWS_AT_ONCE = 16


def _shift_copies(buf, sh):
    rows = sh.shape[1]
    for s in range(1, SUB):
        sh[s - 1, :, :] = buf[pl.ds(s, rows), :]


def _tiles(buf, sh, s, first, count, group0, lanes):
    src = buf if s == 0 else sh.at[s - 1]
    return {t: src[pl.ds(pl.multiple_of((group0 + t) * SUB, SUB), SUB), lanes] for t in range(first, first + count)}


def _by_shift(offsets):
    out = []
    for s in range(SUB):
        taps = [(k, o // SUB) for k, o in enumerate(offsets) if o % SUB == s]
        if taps:
            out.append((s, taps))
    return out


def _conv_rows(out_ref, w_ref, bias_ref, offsets, buf, sh, tm):
    n = ROWS_AT_ONCE
    for cb in range(D // LANE):
        lanes = slice(cb * LANE, (cb + 1) * LANE)
        bias = None if bias_ref is None else jnp.broadcast_to(bias_ref[:, lanes], (SUB, LANE))

        def body(jb, carry):
            accs = [bias] * n
            for s, taps in _by_shift(offsets):
                ms = [m for _, m in taps]
                tiles = _tiles(buf, sh, s, min(ms), max(ms) - min(ms) + n, jb * n, lanes)
                for k, m in taps:
                    wk = jnp.broadcast_to(w_ref[k:k + 1, lanes], (SUB, LANE))
                    for jj in range(n):
                        t = wk * tiles[m + jj]
                        accs[jj] = t if accs[jj] is None else accs[jj] + t
            for jj in range(n):
                out_ref[pl.ds(pl.multiple_of((jb * n + jj) * SUB, SUB), SUB), lanes] = accs[jj]
            return carry

        lax.fori_loop(0, tm // (SUB * n), body, 0)


def _conv_wgrad(dw_ref, d_ref, offsets, buf, sh, tm):
    n = 4
    for cb in range(D // LANE):
        lanes = slice(cb * LANE, (cb + 1) * LANE)

        def body(jq, accs):
            accs = list(accs)
            d = [d_ref[pl.ds(pl.multiple_of((jq * n + jj) * SUB, SUB), SUB), lanes] for jj in range(n)]
            for s, taps in _by_shift(offsets):
                ms = [m for _, m in taps]
                tiles = _tiles(buf, sh, s, min(ms), max(ms) - min(ms) + n, jq * n, lanes)
                for k, m in taps:
                    for jj in range(n):
                        accs[k] = accs[k] + d[jj] * tiles[m + jj]
            return tuple(accs)

        accs = lax.fori_loop(0, tm // (SUB * n), body, tuple(jnp.zeros((SUB, LANE), F32) for _ in offsets))
        for k, acc in enumerate(accs):
            dw_ref[k:k + 1, lanes] += jnp.sum(acc, axis=0, keepdims=True)


def _fwd_a1(x0, w1, b1, l, hosts=()):
    tm = 512

    def body(x_ref, w_ref, b_ref, h_ref, glu_ref):
        xb = x_ref[...].astype(BF16)
        for q in range(NQ):
            sl = slice(q * 512, (q + 1) * 512)
            h_ref[:, sl] = jnp.dot(xb, w_ref[q], preferred_element_type=F32) + b_ref[:, sl]
        glu_ref[...] = h_ref[:, :D] * _sig(h_ref[:, D:])

    return _call(f"fwd_a1_{l}", body, (S // tm,), [_row(tm, D), _wspec(w1), _const((1, 2 * D))],
                 [_row(tm, 2 * D), _row(tm, D)], [_sds((S, 2 * D)), _sds((S, D))], hosts=hosts)(x0, w1, b1)


def _fwd_a2(glu, x0, wdw, bdw, lg, lb, w2, l, hosts=()):
    tm = 256

    def body(g_ref, gp_ref, x_ref, wdw_ref, bdw_ref, lg_ref, lb_ref, w2_ref, z_ref, cv_ref, buf, sh):
        i = pl.program_id(0)
        _fill_halo(buf, 0, HALO_A, lambda: gp_ref[...], i == 0)
        buf[HALO_A:HALO_A + tm, :] = g_ref[...]
        _shift_copies(buf, sh)
        _conv_rows(cv_ref, wdw_ref, bdw_ref, [HALO_A - (CONV_A - 1) + k for k in range(CONV_A)], buf, sh, tm)
        n, _, _ = _ln(cv_ref[...], lg_ref[...], lb_ref[...])
        sb = (n * _sig(n)).astype(BF16)
        z_ref[...] = ALPHA * x_ref[...] + jnp.dot(sb, w2_ref[...], preferred_element_type=F32)

    return _call(f"fwd_a2_{l}", body, (S // tm,),
                 [_row(tm, D), _prev(tm, HALO_A, D), _row(tm, D), _const((32, D)), _const((1, D)), _const((1, D)),
                  _const((1, D)), _const(w2.shape)],
                 [_row(tm, D), _row(tm, D)], [_sds((S, D)), _sds((S, D))],
                 scratch=[pltpu.VMEM((HALO_A + tm, D), F32), pltpu.VMEM((SUB - 1, HALO_A + tm - SUB, D), F32)],
                 hosts=hosts)(glu, glu, x0, wdw, bdw, lg, lb, w2)


def _fwd_b(x0, win, b_in, lg, lb, ws, bsx, wout, hosts=()):
    tm = 256

    def body(x_ref, win_ref, bin_ref, lg_ref, lb_ref, ws_ref, bsx_ref, wout_ref, z_ref, zg_ref, gg_ref, f_scr, h_ref):
        xb = x_ref[...].astype(BF16)
        for q in range(NQ):
            sl = slice(q * 1024, (q + 1) * 1024)
            h_ref[:, sl] = jnp.dot(xb, win_ref[q], preferred_element_type=F32) + bin_ref[:, sl]
        u, du = _gelu_and_grad(h_ref[:, :E])
        v, dv = _gelu_and_grad(h_ref[:, E:])
        zg_ref[:, 0:E] = u.astype(BF16)
        zg_ref[:, E:2 * E] = v.astype(BF16)
        gg_ref[:, 0:E] = du.astype(BF16)
        gg_ref[:, E:2 * E] = dv.astype(BF16)
        vn, _, _ = _ln(v, lg_ref[...], lb_ref[...])
        vnb = vn.astype(BF16)
        mask, _ = _sgu_masks()
        for hd in range(SGU_H):
            wm = jnp.where(mask, ws_ref[hd], 0.0).astype(BF16)
            cs = slice(hd * SGU_G, (hd + 1) * SGU_G)
            for n in range(tm // SGU_T):
                rs = slice(n * SGU_T, (n + 1) * SGU_T)
                f_scr[rs, cs] = jnp.dot(wm, vnb[rs, cs], preferred_element_type=F32) + bsx_ref[hd]
        mb = (u * f_scr[...]).astype(BF16)
        z_ref[...] = ALPHA * x_ref[...] + jnp.dot(mb, wout_ref[...], preferred_element_type=F32)

    return _call("fwd_b", body, (S // tm,),
                 [_row(tm, D), _wspec(win), _const((1, 2 * E)), _const((1, E)), _const((1, E)),
                  _const((SGU_H, SGU_T, SGU_T)), _const((SGU_H, SGU_T, SGU_G)), _const(wout.shape)],
                 [_row(tm, D), _row(tm, 2 * E), _row(tm, 2 * E)],
                 [_sds((S, D)), _sds((S, 2 * E), BF16), _sds((S, 2 * E), BF16)],
                 scratch=[pltpu.VMEM((tm, E), F32), pltpu.VMEM((tm, 2 * E), F32)], hosts=hosts
                 )(x0, win, b_in, lg, lb, ws, bsx, wout)


def _fwd_c1(x0, win, hosts=()):
    tm = 512

    def body(x_ref, w_ref, hc_ref):
        xb = x_ref[...].astype(BF16)
        for q in range(NQ):
            hc_ref[:, q * 768:(q + 1) * 768] = jnp.dot(xb, w_ref[q], preferred_element_type=F32)

    return _call("fwd_c1", body, (S // tm,), [_row(tm, D), _wspec(win)], _row(tm, 3 * D),
                 _sds((S, 3 * D)), hosts=hosts)(x0, win)


def _short_conv(buf, hc_ref, hcp_ref, wc_ref, tm, i):
    _fill_halo(buf, 0, HALO_C, lambda: hcp_ref[:, D:2 * D] * hcp_ref[:, 2 * D:], i == 0)
    buf[HALO_C:HALO_C + tm, :] = hc_ref[:, D:2 * D] * hc_ref[:, 2 * D:]
    y = wc_ref[0:1, :] * buf[pl.ds(HALO_C - 2, tm), :]
    for k in range(1, CONV_C):
        y = y + wc_ref[k:k + 1, :] * buf[pl.ds(HALO_C - 2 + k, tm), :]
    return y


def _fwd_c2(hc, x0, wc, wout, hosts=()):
    tm = 256

    def body(hc_ref, hcp_ref, x_ref, wc_ref, wout_ref, z_ref, buf):
        y = _short_conv(buf, hc_ref, hcp_ref, wc_ref, tm, pl.program_id(0))
        mb = (hc_ref[:, :D] * y).astype(BF16)
        z_ref[...] = ALPHA * x_ref[...] + jnp.dot(mb, wout_ref[...], preferred_element_type=F32)

    return _call("fwd_c2", body, (S // tm,),
                 [_row(tm, 3 * D), _prev(tm, HALO_C, 3 * D), _row(tm, D), _const((8, D)), _const(wout.shape)],
                 _row(tm, D), _sds((S, D)), scratch=[pltpu.VMEM((HALO_C + tm, D), F32)], hosts=hosts
                 )(hc, hc, x0, wc, wout)


def _fwd_ffn(z1, lg, lb, wgt, wut, wd, l, hosts=()):
    tm = 256

    def body(z_ref, lg_ref, lb_ref, wg_ref, wu_ref, wd_ref, o_ref, a_ref, u_ref, hm_ref):
        x1, _, _ = _ln(z_ref[...], lg_ref[...], lb_ref[...])
        xb = x1.astype(BF16)
        a = _mmt(xb, wg_ref[...])
        u = _mmt(xb, wu_ref[...])
        hmb = (a * _sig(a) * u).astype(BF16)
        a_ref[...] = a.astype(BF16)
        u_ref[...] = u.astype(BF16)
        hm_ref[...] = hmb
        o_ref[...] = ALPHA * x1 + jnp.dot(hmb, wd_ref[...], preferred_element_type=F32)

    return _call(f"fwd_ffn_{l}", body, (S // tm,),
                 [_row(tm, D), _const((1, D)), _const((1, D)), _const((FF, D)), _const((FF, D)), _const((FF, D))],
                 [_row(tm, D), _row(tm, FF), _row(tm, FF), _row(tm, FF)],
                 [_sds((S, D)), _sds((S, FF), BF16), _sds((S, FF), BF16), _sds((S, FF), BF16)],
                 hosts=hosts)(z1, lg, lb, wgt, wut, wd)


def _ple_parts(z2, p, lg, lb, wg_ref, wp_ref, pg):
    x2, xhat, rstd = _ln(z2, lg, lb)
    xb = x2.astype(BF16)
    gate = _sig(jnp.dot(xb, wg_ref[...], preferred_element_type=F32))
    pb = p.astype(BF16)
    qp = jnp.concatenate([jnp.dot(pb, wp_ref[q], preferred_element_type=F32) for q in range(NQ)], axis=1)
    rs = lax.rsqrt(jnp.mean(qp * qp, axis=-1, keepdims=True) + LN_EPS)
    qn = qp * rs
    return x2, xhat, rstd, xb, gate, qn, rs, qn * pg


def _fwd_ple(z2, p, lg, lb, wg, wp, pg, l, hosts=()):
    tm = 512

    def body(z_ref, p_ref, lg_ref, lb_ref, wg_ref, wp_ref, pg_ref, o_ref):
        x2, _, _, _, gate, _, _, r = _ple_parts(z_ref[...], p_ref[...], lg_ref[...], lb_ref[...], wg_ref, wp_ref,
                                                pg_ref[...])
        o_ref[...] = x2 + gate * r

    return _call(f"fwd_ple_{l}", body, (S // tm,),
                 [_row(tm, D), _row(tm, 256), _const((1, D)), _const((1, D)), _const(wg.shape), _wspec(wp),
                  _const((1, D))],
                 _row(tm, D), _sds((S, D)), hosts=hosts)(z2, p, lg, lb, wg, wp, pg)


def _loss_head(y, target):
    tm = 512

    def body(y_ref, t_ref, dy_ref, acc_ref):
        @pl.when(pl.program_id(0) == 0)
        def _():
            acc_ref[...] = jnp.zeros_like(acc_ref)

        e = y_ref[...] - t_ref[...]
        dy_ref[...] = e * (1.0 / D)
        acc_ref[0:1, :] += _colsum(e * e)

    return _call("loss_head", body, (S // tm,), [_row(tm, D), _row(tm, D)], [_row(tm, D), _acc(8, D)],
                 [_sds((S, D)), _sds((8, D))])(y, target)


def _zero_first(*refs):
    @pl.when(pl.program_id(0) == 0)
    def _():
        for r in refs:
            r[...] = jnp.zeros_like(r)


def _bwd_ple(g, z2, p, lg, lb, wg, wp, pg, l, hosts=()):
    tm = 256

    def body(g_ref, z_ref, p_ref, lg_ref, lb_ref, wg_ref, wp_ref, pg_ref, dz_ref, xb_ref, dgp_ref, dqp_ref, acc_ref):
        _zero_first(acc_ref)
        gin = g_ref[...]
        lgv, pgv = lg_ref[...], pg_ref[...]
        _, xhat, rstd, xb, gate, qn, rs, r = _ple_parts(z_ref[...], p_ref[...], lgv, lb_ref[...], wg_ref, wp_ref, pgv)
        xb_ref[...] = xb
        dgpb = (gin * r * gate * (1.0 - gate)).astype(BF16)
        dgp_ref[...] = dgpb
        dx2 = gin + _mmt(dgpb, wg_ref[...])
        dr = gin * gate
        acc_ref[0:1, :] += _colsum(dr * qn)
        t = dr * pgv
        dqp_ref[...] = (rs * (t - qn * jnp.mean(t * qn, axis=-1, keepdims=True))).astype(BF16)
        acc_ref[1:2, :] += _colsum(dx2 * xhat)
        acc_ref[2:3, :] += _colsum(dx2)
        dz_ref[...] = _ln_bwd(dx2 * lgv, xhat, rstd)

    return _call(f"bwd_ple_{l}", body, (S // tm,),
                 [_row(tm, D), _row(tm, D), _row(tm, 256), _const((1, D)), _const((1, D)), _const(wg.shape),
                  _wspec(wp), _const((1, D))],
                 [_row(tm, D), _row(tm, D), _row(tm, D), _row(tm, D), _acc(8, D)],
                 [_sds((S, D)), _sds((S, D), BF16), _sds((S, D), BF16), _sds((S, D), BF16), _sds((8, D))],
                 hosts=hosts)(g, z2, p, lg, lb, wg, wp, pg)


def _bwd_ffn(dz2, z1, ab, ub, lg, lb, wgt, wut, wd, l, hosts=()):
    tm = 256

    def body(dz2_ref, z_ref, a_ref, u_ref, lg_ref, lb_ref, wg_ref, wu_ref, wd_ref, dz1_ref, xb_ref, da_ref, du_ref,
             acc_ref):
        _zero_first(acc_ref)
        dz2v = dz2_ref[...]
        lgv = lg_ref[...]
        x1, xhat, rstd = _ln(z_ref[...], lgv, lb_ref[...])
        xb_ref[...] = x1.astype(BF16)
        a = a_ref[...].astype(F32)
        u = u_ref[...].astype(F32)
        sg = _sig(a)
        dhm = _mmt(dz2v, wd_ref[...])
        dub = (dhm * (a * sg)).astype(BF16)
        dab = (dhm * u * _silu_grad(a, sg)).astype(BF16)
        da_ref[...] = dab
        du_ref[...] = dub
        dx1 = ALPHA * dz2v + _mm(dab, wg_ref[...]) + _mm(dub, wu_ref[...])
        acc_ref[0:1, :] += _colsum(dx1 * xhat)
        acc_ref[1:2, :] += _colsum(dx1)
        dz1_ref[...] = _ln_bwd(dx1 * lgv, xhat, rstd)

    return _call(f"bwd_ffn_{l}", body, (S // tm,),
                 [_row(tm, D), _row(tm, D), _row(tm, FF), _row(tm, FF), _const((1, D)), _const((1, D)),
                  _const((FF, D)), _const((FF, D)), _const((FF, D))],
                 [_row(tm, D), _row(tm, D), _row(tm, FF), _row(tm, FF), _acc(8, D)],
                 [_sds((S, D)), _sds((S, D), BF16), _sds((S, FF), BF16), _sds((S, FF), BF16), _sds((8, D))],
                 hosts=hosts)(dz2, z1, ab, ub, lg, lb, wgt, wut, wd)


def _bwd_a2(dz1, cv, lg, lb, w2, l, hosts=()):
    tm = 512

    def body(dz_ref, cv_ref, lg_ref, lb_ref, w2_ref, dcv_ref, sb_ref, acc_ref):
        _zero_first(acc_ref)
        lgv = lg_ref[...]
        n, xhat, rstd = _ln(cv_ref[...], lgv, lb_ref[...])
        sg = _sig(n)
        sb_ref[...] = (n * sg).astype(BF16)
        dzb = dz_ref[...].astype(BF16)
        ds = _mmt(dzb, w2_ref[...])
        dn = ds * _silu_grad(n, sg)
        acc_ref[0:1, :] += _colsum(dn * xhat)
        acc_ref[1:2, :] += _colsum(dn)
        dcv = _ln_bwd(dn * lgv, xhat, rstd)
        acc_ref[2:3, :] += _colsum(dcv)
        dcv_ref[...] = dcv

    return _call(f"bwd_a2_{l}", body, (S // tm,),
                 [_row(tm, D), _row(tm, D), _const((1, D)), _const((1, D)), _const(w2.shape)],
                 [_row(tm, D), _row(tm, D), _acc(8, D)],
                 [_sds((S, D)), _sds((S, D), BF16), _sds((8, D))], hosts=hosts)(dz1, cv, lg, lb, w2)


def _bwd_conv_a(dcv, glu, wdw, l, hosts=()):
    tm = 256
    nb = S // tm

    def body(d_ref, dn_ref, g_ref, gp_ref, w_ref, dglu_ref, dw_ref, bufd, bufx, sh):
        i = pl.program_id(0)
        _zero_first(dw_ref)
        bufd[0:tm, :] = d_ref[...]
        _fill_halo(bufd, tm, HALO_A, lambda: dn_ref[...], i == nb - 1)
        _fill_halo(bufx, 0, HALO_A, lambda: gp_ref[...], i == 0)
        bufx[HALO_A:HALO_A + tm, :] = g_ref[...]
        _shift_copies(bufd, sh)
        _conv_rows(dglu_ref, w_ref, None, [CONV_A - 1 - k for k in range(CONV_A)], bufd, sh, tm)
        _shift_copies(bufx, sh)
        _conv_wgrad(dw_ref, d_ref, [HALO_A - (CONV_A - 1) + k for k in range(CONV_A)], bufx, sh, tm)

    return _call(f"bwd_conv_a_{l}", body, (nb,),
                 [_row(tm, D), _next(tm, HALO_A, D), _row(tm, D), _prev(tm, HALO_A, D), _const((32, D))],
                 [_row(tm, D), _acc(32, D)], [_sds((S, D)), _sds((32, D))],
                 scratch=[pltpu.VMEM((tm + HALO_A, D), F32), pltpu.VMEM((HALO_A + tm, D), F32),
                          pltpu.VMEM((SUB - 1, HALO_A + tm - SUB, D), F32)], hosts=hosts)(dcv, dcv, glu, glu, wdw)


def _bwd_a1(dglu, h, dz1, w1, l, hosts=()):
    tm = 256

    def body(dg_ref, h_ref, dz_ref, w_ref, dx_ref, dh_ref, acc_ref):
        _zero_first(acc_ref)
        a, g = h_ref[:, :D], h_ref[:, D:]
        sg = _sig(g)
        dgl = dg_ref[...]
        da = dgl * sg
        dg = dgl * a * sg * (1.0 - sg)
        acc_ref[0:1, 0:D] += _colsum(da)
        acc_ref[0:1, D:2 * D] += _colsum(dg)
        dh_ref[:, 0:D] = da.astype(BF16)
        dh_ref[:, D:2 * D] = dg.astype(BF16)
        dx = ALPHA * dz_ref[...]
        for q in range(NQ):
            dx = dx + _mmt(dh_ref[:, q * 512:(q + 1) * 512], w_ref[q])
        dx_ref[...] = dx

    return _call(f"bwd_a1_{l}", body, (S // tm,),
                 [_row(tm, D), _row(tm, 2 * D), _row(tm, D), _wspec(w1)],
                 [_row(tm, D), _row(tm, 2 * D), _acc(8, 2 * D)],
                 [_sds((S, D)), _sds((S, 2 * D), BF16), _sds((8, 2 * D))], hosts=hosts)(dglu, h, dz1, w1)


def _bwd_c2(dz1, hc, wc, wout):
    tm = 256

    def body(dz_ref, hc_ref, hcp_ref, wc_ref, wout_ref, dy_ref, dbg_ref, mb_ref, buf):
        y = _short_conv(buf, hc_ref, hcp_ref, wc_ref, tm, pl.program_id(0))
        dzb = dz_ref[...].astype(BF16)
        dm = _mmt(dzb, wout_ref[...])
        bg = hc_ref[:, :D]
        mb_ref[...] = (bg * y).astype(BF16)
        dbg_ref[...] = (dm * y).astype(BF16)
        dy_ref[...] = dm * bg

    return _call("bwd_c2", body, (S // tm,),
                 [_row(tm, D), _row(tm, 3 * D), _prev(tm, HALO_C, 3 * D), _const((8, D)), _const(wout.shape)],
                 [_row(tm, D), _row(tm, D), _row(tm, D)],
                 [_sds((S, D)), _sds((S, D), BF16), _sds((S, D), BF16)],
                 scratch=[pltpu.VMEM((HALO_C + tm, D), F32)])(dz1, hc, hc, wc, wout)


def _bwd_c1(dy, hc, dbg, dz1, wc, win):
    tm = 256
    nb = S // tm

    def body(d_ref, dn_ref, hc_ref, hcp_ref, dbg_ref, dz_ref, wc_ref, win_ref, dx_ref, dhc_ref, dwc_ref, bufd, bufq):
        i = pl.program_id(0)
        _zero_first(dwc_ref)
        bufd[0:tm, :] = d_ref[...]
        _fill_halo(bufd, tm, HALO_C, lambda: dn_ref[...], i == nb - 1)
        _fill_halo(bufq, 0, HALO_C, lambda: hcp_ref[:, D:2 * D] * hcp_ref[:, 2 * D:], i == 0)
        bufq[HALO_C:HALO_C + tm, :] = hc_ref[:, D:2 * D] * hc_ref[:, 2 * D:]
        dq = wc_ref[0:1, :] * bufd[pl.ds(CONV_C - 1, tm), :]
        for k in range(1, CONV_C):
            dq = dq + wc_ref[k:k + 1, :] * bufd[pl.ds(CONV_C - 1 - k, tm), :]
        dv = d_ref[...]
        for k in range(CONV_C):
            dwc_ref[k:k + 1, :] += _colsum(dv * bufq[pl.ds(HALO_C - (CONV_C - 1) + k, tm), :])
        dhc_ref[:, 0:D] = dbg_ref[...]
        dhc_ref[:, D:2 * D] = (dq * hc_ref[:, 2 * D:]).astype(BF16)
        dhc_ref[:, 2 * D:3 * D] = (dq * hc_ref[:, D:2 * D]).astype(BF16)
        dx = ALPHA * dz_ref[...]
        for q in range(NQ):
            dx = dx + _mmt(dhc_ref[:, q * 768:(q + 1) * 768], win_ref[q])
        dx_ref[...] = dx

    return _call("bwd_c1", body, (nb,),
                 [_row(tm, D), _next(tm, HALO_C, D), _row(tm, 3 * D), _prev(tm, HALO_C, 3 * D), _row(tm, D),
                  _row(tm, D), _const((8, D)), _wspec(win)],
                 [_row(tm, D), _row(tm, 3 * D), _acc(8, D)],
                 [_sds((S, D)), _sds((S, 3 * D), BF16), _sds((8, D))],
                 scratch=[pltpu.VMEM((tm + HALO_C, D), F32), pltpu.VMEM((HALO_C + tm, D), F32)]
                 )(dy, dy, hc, hc, dbg, dz1, wc, win)


def _bwd_b(dz1, zg, gg, lg, lb, win, wout, ws, wst, bsx):
    tm = 128
    nb = S // tm

    def body(dz_ref, zg_ref, gg_ref, lg_ref, lb_ref, win_ref, wout_ref, ws_ref, wst_ref, bsx_ref,
             dx_ref, dh_ref, mb_ref, acc_ref, dws_ref, dbs_ref, f_scr, dvn_scr):
        _zero_first(acc_ref, dws_ref, dbs_ref)
        lgv = lg_ref[...]
        u = zg_ref[:, :E].astype(F32)
        v = zg_ref[:, E:].astype(F32)
        vn, xhat, rstd = _ln(v, lgv, lb_ref[...])
        vnb = vn.astype(BF16)
        dzb = dz_ref[...].astype(BF16)
        dm = _mmt(dzb, wout_ref[...])
        mask, mask_t = _sgu_masks()
        for hd in range(SGU_H):
            wm = jnp.where(mask, ws_ref[hd], 0.0).astype(BF16)
            cs = slice(hd * SGU_G, (hd + 1) * SGU_G)
            for n in range(tm // SGU_T):
                rs = slice(n * SGU_T, (n + 1) * SGU_T)
                f_scr[rs, cs] = jnp.dot(wm, vnb[rs, cs], preferred_element_type=F32) + bsx_ref[hd]
        f = f_scr[...]
        mb_ref[...] = (u * f).astype(BF16)
        du = dm * f
        df = dm * u
        dfb = df.astype(BF16)
        for hd in range(SGU_H):
            wmt = jnp.where(mask_t, wst_ref[hd], 0.0).astype(BF16)
            cs = slice(hd * SGU_G, (hd + 1) * SGU_G)
            for n in range(tm // SGU_T):
                rs = slice(n * SGU_T, (n + 1) * SGU_T)
                dvn_scr[rs, cs] = jnp.dot(wmt, dfb[rs, cs], preferred_element_type=F32)
                dws_ref[hd] += lax.dot_general(dfb[rs, cs], vnb[rs, cs], (((1,), (1,)), ((), ())),
                                               preferred_element_type=F32)
                dbs_ref[hd] += df[rs, cs]
        dvn = dvn_scr[...]
        acc_ref[1:2, 0:E] += _colsum(dvn * xhat)
        acc_ref[2:3, 0:E] += _colsum(dvn)
        dv = _ln_bwd(dvn * lgv, xhat, rstd)
        dhu = du * gg_ref[:, :E].astype(F32)
        dhv = dv * gg_ref[:, E:].astype(F32)
        acc_ref[0:1, 0:E] += _colsum(dhu)
        acc_ref[0:1, E:2 * E] += _colsum(dhv)
        dh_ref[:, 0:E] = dhu.astype(BF16)
        dh_ref[:, E:2 * E] = dhv.astype(BF16)
        dx = ALPHA * dz_ref[...]
        for q in range(NQ):
            dx = dx + _mmt(dh_ref[:, q * 1024:(q + 1) * 1024], win_ref[q])
        dx_ref[...] = dx

        @pl.when(pl.program_id(0) == nb - 1)
        def _():
            for hd in range(SGU_H):
                dws_ref[hd] = jnp.where(mask, dws_ref[hd], 0.0)

    c3 = lambda a, b, c: pl.BlockSpec((a, b, c), lambda i: (0, 0, 0))
    return _call("bwd_b", body, (nb,),
                 [_row(tm, D), _row(tm, 2 * E), _row(tm, 2 * E), _const((1, E)), _const((1, E)), _wspec(win),
                  _const(wout.shape), _const((SGU_H, SGU_T, SGU_T)), _const((SGU_H, SGU_T, SGU_T)),
                  _const((SGU_H, SGU_T, SGU_G))],
                 [_row(tm, D), _row(tm, 2 * E), _row(tm, E), _acc(8, 2 * E), c3(SGU_H, SGU_T, SGU_T),
                  c3(SGU_H, SGU_T, SGU_G)],
                 [_sds((S, D)), _sds((S, 2 * E), BF16), _sds((S, E), BF16), _sds((8, 2 * E)),
                  _sds((SGU_H, SGU_T, SGU_T)), _sds((SGU_H, SGU_T, SGU_G))],
                 scratch=[pltpu.VMEM((tm, E), F32), pltpu.VMEM((tm, E), F32)]
                 )(dz1, zg, gg, lg, lb, win, wout, ws, wst, bsx)


def _mm_tn(name, a, amode, b, bmode, k, n, groups=NQ, hosts=()):
    def block_bytes(ts):
        ka = k if amode == "1" else groups * k
        nb = n if bmode == "1" else groups * n
        return 2 * (ts * ka * a.dtype.itemsize + ts * nb * b.dtype.itemsize + groups * k * n * 4)

    ts = min(1024 if block_bytes(1024) <= DW_BLOCK_BUDGET else 512, S)

    def spec(mode, w):
        if mode == "1":
            return pl.BlockSpec((ts, w), lambda s: (s, 0))
        if mode == "c":
            return pl.BlockSpec((ts, groups * w), lambda s: (s, 0))
        return pl.BlockSpec((groups, ts, w), lambda s: (0, s, 0))

    def pick(ref, mode, w, g):
        if mode == "1":
            return ref[...]
        if mode == "c":
            return ref[:, g * w:(g + 1) * w]
        return ref[g]

    def body(a_ref, b_ref, o_ref):
        _zero_first(o_ref)
        a_t = jnp.transpose(a_ref[...].astype(BF16)) if amode == "1" else None
        b_1 = b_ref[...].astype(BF16) if bmode == "1" else None
        for g in range(groups):
            lhs = a_t if amode == "1" else jnp.transpose(pick(a_ref, amode, k, g).astype(BF16))
            rhs = b_1 if bmode == "1" else pick(b_ref, bmode, n, g).astype(BF16)
            o_ref[0, g] += jnp.dot(lhs, rhs, preferred_element_type=F32)

    return _call(name, body, (S // ts,), [spec(amode, k), spec(bmode, n)],
                 pl.BlockSpec((1, groups, k, n), lambda s: (0, 0, 0, 0)), _sds((1, groups, k, n)), hosts=hosts)(a, b)


def _row_block(k, cap=256):
    return max(t for t in range(16, min(k, cap) + 1, 16) if k % t == 0)


def _cast_bf16(w, hosts=()):
    nl, k, n = w.shape
    tb = _row_block(k, 512)
    nb = k // tb

    def body(w_ref, o_ref):
        o_ref[...] = w_ref[...].astype(BF16)

    spec = pl.BlockSpec((None, tb, n), lambda i: (i // nb, i % nb, 0))
    return _call("cast_bf16", body, (nl * nb,), [spec], spec, _sds(w.shape, BF16), hosts=hosts)(w)


def _adam(name, w, m, v, gc, l, prev):
    nl, k, n = w.shape
    nc = gc.shape[0]
    tb = _row_block(k, 512)

    def body(w_ref, m_ref, v_ref, g_ref, *rest):
        go_ref, d_ref, mo_ref, vo_ref = rest[-4:]
        g = g_ref[0].astype(F32)
        for c in range(1, nc):
            g = g + g_ref[c].astype(F32)
        m2 = ADAM_B1 * m_ref[...] + (1.0 - ADAM_B1) * g
        v2 = ADAM_B2 * v_ref[...] + (1.0 - ADAM_B2) * (g * g)
        m_hat = m2 / (1.0 - ADAM_B1 ** ADAM_STEP)
        v_hat = v2 / (1.0 - ADAM_B2 ** ADAM_STEP)
        go_ref[...] = g
        d_ref[...] = -ADAM_LR * (m_hat / (jnp.sqrt(v_hat) + ADAM_EPS) + ADAM_WD * w_ref[...])
        mo_ref[...] = m2
        vo_ref[...] = v2

    spec = pl.BlockSpec((None, tb, n), lambda i: (l, i, 0))
    gspec = pl.BlockSpec((nc, None, tb, n), lambda i: (0, 0, i, 0))
    in_specs, args, aliases = [spec, spec, spec, gspec], [w, m, v, gc], {}
    if prev is not None:
        in_specs += [pl.BlockSpec(memory_space=pl.ANY)] * 4
        args += list(prev)
        aliases = {4 + j: j for j in range(4)}
    return _call(name, body, (k // tb,), in_specs, [spec] * 4, [_sds(w.shape)] * 4, aliases=aliases)(*args)


def _sum8(name, g8):
    r = g8.shape[1]

    def body(g_ref, o_ref):
        acc = g_ref[0]
        for d in range(1, 8):
            acc = acc + g_ref[d]
        o_ref[...] = acc

    return _call(name, body, (1,), [pl.BlockSpec((8, r, 128), lambda i: (0, 0, 0))],
                 pl.BlockSpec((r, 128), lambda i: (0, 0)), _sds((r, 128)))(g8)


def _place():
    x, y, c = lax.axis_index("x"), lax.axis_index("y"), lax.axis_index("c")
    return x, y, c, 2 * x + y, (x, y, 1 - c), [(1 - x, y), (x, 1 - y), (1 - x, 1 - y)]


class _Exchange:
    def __init__(self, arrays, out_shapes):
        self.arrays, self.out_shapes = list(arrays), list(out_shapes)
        n = len(self.arrays)
        self.sems = [pltpu.SemaphoreType.DMA((7 * n,)), pltpu.SemaphoreType.DMA((7 * n,)),
                     pltpu.SemaphoreType.DMA((n,))]

    def _copies(self, ins, outs, sems):
        send, recv, lsem = sems
        local_src, remote_src, dst = self.maps(ins, outs)
        x, y, c, q, sib, chips = _place()

        def rcopy(w, k, qq, cc, to, src=None):
            return pltpu.make_async_remote_copy(
                src_ref=dst(w, qq, cc) if src is None else src, dst_ref=dst(w, qq, cc),
                send_sem=send.at[7 * w + k], recv_sem=recv.at[7 * w + k], device_id=to, device_id_type=MESH)

        def mine(w):
            return pltpu.make_async_copy(local_src(w), dst(w, q, c), lsem.at[w])

        def first(w):
            return [rcopy(w, 0, q, c, sib, local_src(w))] + [
                rcopy(w, 1 + j, q, c, (cx, cy, c), remote_src(w, 2 * cx + cy)) for j, (cx, cy) in enumerate(chips)]

        return rcopy, mine, first, (x, y, c), q, c, sib, chips

    def start(self, ins, outs, sems):
        _, mine, first, *_ = self._copies(ins, outs, sems)
        for w in range(len(self.arrays)):
            mine(w).start()
            for cp in first(w):
                cp.start()

    def forward_steps(self, n_steps):
        sizes = [a.size // a.shape[0] for a in self.arrays]
        plan, moved = {}, 0
        for w, size in enumerate(sizes):
            moved += size
            plan.setdefault(min(n_steps - 1, -(-moved * n_steps // sum(sizes))), []).append(w)
        return plan

    def forward(self, ws, ins, outs, sems):
        rcopy, _, _, me, _, c, sib, chips = self._copies(ins, outs, sems)
        for w in ws:
            for j, (cx, cy) in enumerate(chips):
                rcopy(w, 1 + j, 2 * cx + cy, c, me).wait_recv()
                rcopy(w, 4 + j, 2 * cx + cy, c, sib).start()

    def complete(self, ins, outs, sems):
        rcopy, mine, first, me, q, c, sib, chips = self._copies(ins, outs, sems)
        n = len(self.arrays)
        for w in range(n):
            rcopy(w, 0, q, 1 - c, me).wait_recv()
            for j, (cx, cy) in enumerate(chips):
                rcopy(w, 4 + j, 2 * cx + cy, 1 - c, me).wait_recv()
        for w in range(n):
            for cp in first(w):
                cp.wait_send()
            for j, (cx, cy) in enumerate(chips):
                rcopy(w, 4 + j, 2 * cx + cy, c, sib).wait_send()
            mine(w).wait()


class _GatherWeights(_Exchange):
    def __init__(self, items):
        self.layers = [l for _, l in items]
        self.kh = [s.shape[1] // 2 for s, _ in items]
        super().__init__([s for s, _ in items], [_sds((NQ, 1) + s.shape[1:], BF16) for s, _ in items])

    def maps(self, ins, outs):
        c = lax.axis_index("c")
        src = lambda w: ins[w].at[pl.ds(self.layers[w], 1), pl.ds(c * self.kh[w], self.kh[w]), :]
        return src, lambda w, q: src(w), lambda w, q, cc: outs[w].at[q, :, pl.ds(cc * self.kh[w], self.kh[w]), :]


class _ScatterPartials(_Exchange):
    def __init__(self, parts):
        super().__init__(parts, [_sds((NQ, 1, 2) + p.shape[2:], BF16) for p in parts])

    def maps(self, ins, outs):
        q = 2 * lax.axis_index("x") + lax.axis_index("y")
        return (lambda w: ins[w].at[:, q]), (lambda w, qq: ins[w].at[:, qq]), (lambda w, qq, cc: outs[w].at[qq, :, cc])


class _Gather8(_Exchange):
    def __init__(self, v):
        super().__init__([v], [_sds((8,) + v.shape)])

    def maps(self, ins, outs):
        return (lambda w: ins[0]), (lambda w, q: ins[0]), (lambda w, q, cc: outs[0].at[2 * q + cc])


class _SwapHalves:
    def __init__(self, dws):
        self.arrays = list(dws)
        self.kh = [d.shape[2] // 2 for d in dws]
        self.out_shapes = [_sds(d.shape[:2] + (kh,) + d.shape[3:]) for d, kh in zip(dws, self.kh)]
        self.sems = [pltpu.SemaphoreType.DMA((len(dws),)), pltpu.SemaphoreType.DMA((len(dws),))]

    def _copies(self, ins, outs, sems):
        send, recv = sems
        _, _, c, _, sib, _ = _place()
        return [pltpu.make_async_remote_copy(
            src_ref=ins[w].at[:, :, pl.ds((1 - c) * self.kh[w], self.kh[w]), :], dst_ref=outs[w],
            send_sem=send.at[w], recv_sem=recv.at[w], device_id=sib, device_id_type=MESH)
            for w in range(len(self.arrays))]

    def start(self, ins, outs, sems):
        for cp in self._copies(ins, outs, sems):
            cp.start()

    def forward_steps(self, n_steps):
        return {}

    def complete(self, ins, outs, sems):
        for cp in self._copies(ins, outs, sems):
            cp.wait()


def _comm_only(name, host):
    n_in, n_out = len(host.arrays), len(host.out_shapes)

    def body(*refs):
        ins, outs, sems = refs[:n_in], refs[n_in:n_in + n_out], refs[n_in + n_out:]
        host.start(ins, outs, sems)
        for ws in host.forward_steps(1).values():
            host.forward(ws, ins, outs, sems)
        host.complete(ins, outs, sems)

    any_spec = pl.BlockSpec(memory_space=pl.ANY)
    return pl.pallas_call(body, name=name, in_specs=[any_spec] * n_in, out_specs=[any_spec] * n_out,
                          out_shape=host.out_shapes, scratch_shapes=host.sems)(*host.arrays)


def _add_halves(dw, got, cidx):
    nl, _, k, n = dw.shape
    kh = k // 2
    qb = 2

    def body(c_ref, a_ref, b_ref, o_ref):
        o_ref[...] = (a_ref[...] + b_ref[...]).astype(BF16)

    grid_spec = pltpu.PrefetchScalarGridSpec(
        num_scalar_prefetch=1, grid=(nl, NQ // qb),
        in_specs=[pl.BlockSpec((None, qb, None, kh, n), lambda l, q, c_ref: (l, q, c_ref[0], 0, 0)),
                  pl.BlockSpec((None, qb, kh, n), lambda l, q, c_ref: (l, q, 0, 0))],
        out_specs=pl.BlockSpec((None, qb, kh, n), lambda l, q, c_ref: (l, q, 0, 0)))
    return pl.pallas_call(
        body, name="add_halves", grid_spec=grid_spec, out_shape=_sds((nl, NQ, kh, n), BF16),
        compiler_params=pltpu.CompilerParams(dimension_semantics=("arbitrary", "arbitrary"),
                                             vmem_limit_bytes=VMEM_LIMIT))(cidx, dw.reshape(nl, NQ, 2, kh, n), got)


def _gather8(name, v):
    return _comm_only(name, _Gather8(v))[0]


PACK = 16 * 128


def _pack(arrays):
    parts = []
    for a in arrays:
        flat = a.reshape(-1)
        parts.append(jnp.pad(flat, (0, (-flat.shape[0]) % PACK)))
    return jnp.concatenate(parts).reshape(-1, 128)


def _unpack(packed, shapes):
    flat = packed.reshape(-1)
    out, off = [], 0
    for shp in shapes:
        size = 1
        for d in shp:
            size *= d
        out.append(flat[off:off + size].reshape(shp))
        off += size + (-size) % PACK
    return out


def kernel(x, p, a_w_pw1, a_b_pw1, a_w_dw, a_b_dw, a_ln_g, a_ln_b, a_w_pw2, b_w_in, b_b_in, b_ln_g, b_ln_b, b_w_s, b_b_s, b_w_out, c_w_in, c_w_conv, c_w_out, ln1_g, ln1_b, ln2_g, ln2_b, ffn_w_gate, ffn_w_up, ffn_w_down, ple_w_gate, ple_w_proj, ple_norm_g, loss_target, m_a_w_pw1, m_a_b_pw1, m_a_w_dw, m_a_b_dw, m_a_ln_g, m_a_ln_b, m_a_w_pw2, m_b_w_in, m_b_b_in, m_b_ln_g, m_b_ln_b, m_b_w_s, m_b_b_s, m_b_w_out, m_c_w_in, m_c_w_conv, m_c_w_out, m_ln1_g, m_ln1_b, m_ln2_g, m_ln2_b, m_ffn_w_gate, m_ffn_w_up, m_ffn_w_down, m_ple_w_gate, m_ple_w_proj, m_ple_norm_g, v_a_w_pw1, v_a_b_pw1, v_a_w_dw, v_a_b_dw, v_a_ln_g, v_a_ln_b, v_a_w_pw2, v_b_w_in, v_b_b_in, v_b_ln_g, v_b_ln_b, v_b_w_s, v_b_b_s, v_b_w_out, v_c_w_in, v_c_w_conv, v_c_w_out, v_ln1_g, v_ln1_b, v_ln2_g, v_ln2_b, v_ffn_w_gate, v_ffn_w_up, v_ffn_w_down, v_ple_w_gate, v_ple_w_proj, v_ple_norm_g):
    args = dict(locals())
    wts = {k: args[k] for k in WEIGHTS}
    mom = {k: args["m_" + k] for k in WEIGHTS}
    var = {k: args["v_" + k] for k in WEIGHTS}
    for k in TRANSPOSED:
        wts[k], mom[k], var[k] = (jnp.transpose(t[k], (0, 2, 1)) for t in (wts, mom, var))
    q_idx = 2 * lax.axis_index("x") + lax.axis_index("y")
    c_idx = lax.axis_index("c").astype(jnp.int32).reshape(1)

    wb = {k: _cast_bf16(wts[k]) for k in BIG if k not in ("ffn_w_gate", "ffn_w_up")}
    mixw = [[("a_w_pw1", 0), ("a_w_pw2", 0)], [("b_w_in", 0), ("b_w_out", 0)], [("c_w_in", 0), ("c_w_out", 0)],
            [("a_w_pw1", 1), ("a_w_pw2", 1)]]
    ffnw = [[("ffn_w_gate", l), ("ffn_w_up", l), ("ffn_w_down", l)] for l in range(DEPTH)]
    plew = [[("ple_w_gate", l), ("ple_w_proj", l)] for l in range(DEPTH)]
    fwd_plan = {("a1", 0): mixw[0][1:] + plew[0], ("a2", 0): ffnw[0], ("ffn", 0): mixw[1] + plew[1],
                ("b", 1): ffnw[1], ("ffn", 1): mixw[2] + plew[2] + ffnw[2][:1],
                ("c1", 2): ffnw[2][1:2], ("c2", 2): ffnw[2][2:], ("ffn", 2): mixw[3] + plew[3] + ffnw[3][:1],
                ("a1", 3): ffnw[3][1:2], ("a2", 3): ffnw[3][2:]}
    gw = {}

    def gather(keys):
        return _GatherWeights([(wb[name], l) for name, l in keys])

    def hosted(tag, fn, *fargs):
        keys = fwd_plan.get(tag)
        if not keys:
            return fn(*fargs)
        own, (got,) = fn(*fargs, hosts=[gather(keys)])
        store(keys, got)
        return own

    def store(keys, got):
        for (name, l), arr in zip(keys, got):
            gw[name, l] = arr.reshape(NQ * arr.shape[2], arr.shape[3]) if name in ROW_SHARDED else arr

    first_keys = mixw[0][:1]
    wb["ffn_w_gate"], (got,) = _cast_bf16(wts["ffn_w_gate"], hosts=[gather(first_keys)])
    store(first_keys, got)
    shard_shapes = [wts[k].shape for k in SMALL_SHARDED]
    wb["ffn_w_up"], ((small8,),) = _cast_bf16(wts["ffn_w_up"], hosts=[_Gather8(_pack([wts[k] for k in SMALL_SHARDED]))])
    per_chip = [_unpack(small8[2 * qq], shard_shapes) for qq in range(NQ)]
    full = {k: jnp.concatenate([per_chip[qq][i] for qq in range(NQ)], axis=-1) for i, k in enumerate(SMALL_SHARDED)}
    for k in SMALL_REPL:
        full[k] = wts[k]

    def vec(name, l):
        return full[name][l][None, :]

    def conv_w(name, l, rows):
        w = full[name][l]
        return jnp.pad(w, ((0, rows - w.shape[0]), (0, 0)))

    ws = full["b_w_s"][0]
    wst = jnp.transpose(ws, (0, 2, 1))
    bsx = jnp.broadcast_to(full["b_b_s"][0][:, :, None], (SGU_H, SGU_T, SGU_G))

    x0s, z1s, z2s, saved, ffn_saved = [], [], [], [], []
    cur = x[0]
    for i in range(DEPTH):
        mix, j = i % 3, i // 3
        x0s.append(cur)
        if mix == 0:
            h, glu = hosted(("a1", i), _fwd_a1, cur, gw["a_w_pw1", j], vec("a_b_pw1", j), i)
            z1, cv = hosted(("a2", i), _fwd_a2, glu, cur, conv_w("a_w_dw", j, 32), vec("a_b_dw", j), vec("a_ln_g", j),
                            vec("a_ln_b", j), gw["a_w_pw2", j], i)
            saved.append((h, glu, cv))
        elif mix == 1:
            z1, zg, gg = hosted(("b", i), _fwd_b, cur, gw["b_w_in", 0], vec("b_b_in", 0), vec("b_ln_g", 0),
                                vec("b_ln_b", 0), ws, bsx, gw["b_w_out", 0])
            saved.append((zg, gg))
        else:
            hc = hosted(("c1", i), _fwd_c1, cur, gw["c_w_in", 0])
            z1 = hosted(("c2", i), _fwd_c2, hc, cur, conv_w("c_w_conv", 0, 8), gw["c_w_out", 0])
            saved.append((hc,))
        z2, ab, ub, hm = hosted(("ffn", i), _fwd_ffn, z1, vec("ln1_g", i), vec("ln1_b", i), gw["ffn_w_gate", i],
                                gw["ffn_w_up", i], gw["ffn_w_down", i], i)
        ffn_saved.append((ab, ub, hm))
        cur = hosted(("ple", i), _fwd_ple, z2, p[i, 0], vec("ln2_g", i), vec("ln2_b", i), gw["ple_w_gate", i],
                     gw["ple_w_proj", i], vec("ple_norm_g", i), i)
        z1s.append(z1)
        z2s.append(z2)

    g, loss_acc = _loss_head(cur, loss_target[0])
    loss = lax.psum(0.5 / D * jnp.sum(loss_acc[0]), ("x", "y", "c"))

    dws = {}
    sg = {}
    res = {k: None for k in BIG}

    def wgrad(name, l, a, amode, b, bmode, scatter_keys=(), gather_small=None):
        _, k, n = wts[name].shape
        hosts = [_ScatterPartials([parts[key] for key in scatter_keys])] if scatter_keys else []
        if gather_small is not None:
            hosts = [_Gather8(gather_small)]
        if name in ROW_SHARDED:
            out = _mm_tn(f"dw_{name}_{l}", a, "1", b, "1", NQ * k, n, groups=1, hosts=hosts)
        else:
            out = _mm_tn(f"dw_{name}_{l}", a, amode, b, bmode, k, n, hosts=hosts)
        got = None
        if hosts:
            out, (got,) = out
        if scatter_keys:
            update(scatter_keys, got)
        dws[name, l] = out.reshape(1, NQ, k, n)
        return got

    def swap(keys):
        return _SwapHalves([dws[k] for k in keys])

    parts = {}

    def add_halves(keys, got):
        parts.update((k, _add_halves(dws[k], r, c_idx)) for k, r in zip(keys, got))

    def update(keys, contribs):
        for (name, l), gc in zip(keys, contribs):
            _, kq, n = wts[name].shape
            res[name] = _adam(f"adam_{name}_{l}", wts[name], mom[name], var[name], gc.reshape(NQ, 1, kq, n), l,
                              res[name])

    small = SMALL_SHARDED + SMALL_REPL
    late_small = [("a_ln_g", 0), ("a_ln_b", 0), ("a_b_dw", 0), ("a_w_dw", 0), ("a_b_pw1", 0)]
    early_small = [(k, l) for k in small for l in range(full[k].shape[0]) if (k, l) not in late_small]
    pending = None
    for i in reversed(range(DEPTH)):
        mix, j = i % 3, i // 3
        ple_args = (g, z2s[i], p[i, 0], vec("ln2_g", i), vec("ln2_b", i), gw["ple_w_gate", i], gw["ple_w_proj", i],
                    vec("ple_norm_g", i), i)
        if pending:
            (dz2, x2b, dgp, dqp, acc), (got,) = _bwd_ple(*ple_args, hosts=[swap(pending)])
            add_halves(pending, got)
        else:
            dz2, x2b, dgp, dqp, acc = _bwd_ple(*ple_args)
        sg["ple_norm_g", i], sg["ln2_g", i], sg["ln2_b", i] = acc[0], acc[1], acc[2]
        wgrad("ple_w_gate", i, x2b, "c", dgp, "1")
        wgrad("ple_w_proj", i, p[i, 0], "1", dqp, "c")
        ab, ub, hm = ffn_saved[i]
        ffn_args = (dz2, z1s[i], ab, ub, vec("ln1_g", i), vec("ln1_b", i), gw["ffn_w_gate", i], gw["ffn_w_up", i],
                    gw["ffn_w_down", i], i)
        if pending:
            (dz1, x1b, da, du, acc), (contribs,) = _bwd_ffn(
                *ffn_args, hosts=[_ScatterPartials([parts[key] for key in ffnw[i + 1]])])
            update(ffnw[i + 1], contribs)
        else:
            dz1, x1b, da, du, acc = _bwd_ffn(*ffn_args)
        sg["ln1_g", i], sg["ln1_b", i] = acc[0], acc[1]
        wgrad("ffn_w_gate", i, da, "1", x1b, "1", scatter_keys=mixw[i + 1][1:] + plew[i + 1] if pending else ())
        if i == 0:
            (g8_early,) = wgrad("ffn_w_up", i, du, "1", x1b, "1", gather_small=_pack([sg[pc] for pc in early_small]))
        else:
            wgrad("ffn_w_up", i, du, "1", x1b, "1")
        wgrad("ffn_w_down", i, hm, "1", dz2, "1", scatter_keys=mixw[i + 1][:1] if pending else ())
        x0 = x0s[i]
        if mix == 0:
            h, glu, cv = saved[i]
            a2_args = (dz1, cv, vec("a_ln_g", j), vec("a_ln_b", j), gw["a_w_pw2", j], i)
            conv_args = (glu, conv_w("a_w_dw", j, 32), i)
            if i == 0:
                early = ffnw[0] + plew[0]
                (dcv, sb, acc), (got,) = _bwd_a2(*a2_args, hosts=[swap(early)])
                sg["a_ln_g", j], sg["a_ln_b", j], sg["a_b_dw", j] = acc[0], acc[1], acc[2]
                add_halves(early, got)
                wgrad("a_w_pw2", j, sb, "c", dz1, "1")
                (dglu, dwdw), (contribs, got) = _bwd_conv_a(
                    dcv, *conv_args, hosts=[_ScatterPartials([parts[key] for key in early]), swap(mixw[0][1:])])
                update(early, contribs)
                add_halves(mixw[0][1:], got)
                (g, dh, acc), (contribs,) = _bwd_a1(dglu, h, dz1, gw["a_w_pw1", j], i,
                                                    hosts=[_ScatterPartials([parts[key] for key in mixw[0][1:]])])
                update(mixw[0][1:], contribs)
            else:
                dcv, sb, acc = _bwd_a2(*a2_args)
                sg["a_ln_g", j], sg["a_ln_b", j], sg["a_b_dw", j] = acc[0], acc[1], acc[2]
                dglu, dwdw = _bwd_conv_a(dcv, *conv_args)
                wgrad("a_w_pw2", j, sb, "c", dz1, "1")
                g, dh, acc = _bwd_a1(dglu, h, dz1, gw["a_w_pw1", j], i)
            sg["a_w_dw", j] = dwdw[:CONV_A]
            sg["a_b_pw1", j] = acc[0]
            wgrad("a_w_pw1", j, x0, "1", dh, "c")
        elif mix == 1:
            zg, gg = saved[i]
            g, dh, mb, acc, dw_s, db_s = _bwd_b(dz1, zg, gg, vec("b_ln_g", 0), vec("b_ln_b", 0), gw["b_w_in", 0],
                                                gw["b_w_out", 0], ws, wst, bsx)
            sg["b_b_in", 0], sg["b_ln_g", 0], sg["b_ln_b", 0] = acc[0], acc[1, :E], acc[2, :E]
            sg["b_w_s", 0], sg["b_b_s", 0] = dw_s, jnp.sum(db_s, axis=-1)
            wgrad("b_w_out", 0, mb, "c", dz1, "1")
            wgrad("b_w_in", 0, x0, "1", dh, "c")
        else:
            (hc,) = saved[i]
            wc = conv_w("c_w_conv", 0, 8)
            dy, dbg, mb = _bwd_c2(dz1, hc, wc, gw["c_w_out", 0])
            wgrad("c_w_out", 0, mb, "c", dz1, "1")
            g, dhc, dwc = _bwd_c1(dy, hc, dbg, dz1, wc, gw["c_w_in", 0])
            sg["c_w_conv", 0] = dwc[:CONV_C]
            wgrad("c_w_in", 0, x0, "1", dhc, "c")
        pending = mixw[i] + ffnw[i] + plew[i] if i > 0 else mixw[0][:1]
    grad_x = g[None]
    add_halves(pending, _comm_only("swap_last", swap(pending)))
    update(pending, _comm_only("scatter_last", _ScatterPartials([parts[key] for key in pending])))

    g8_late = _gather8("gather_small_late", _pack([sg[pc] for pc in late_small]))
    sums = dict(zip(early_small, _unpack(_sum8("sum8_early", g8_early), [sg[pc].shape for pc in early_small])))
    sums.update(zip(late_small, _unpack(_sum8("sum8_late", g8_late), [sg[pc].shape for pc in late_small])))
    gsum = [jnp.stack([sums[k, l] for l in range(full[k].shape[0])]) for k in small]
    gmine = []
    for k, gs in zip(small, gsum):
        if k in SMALL_SHARDED:
            wdt = wts[k].shape[-1]
            gs = lax.dynamic_slice_in_dim(gs, q_idx * wdt, wdt, axis=gs.ndim - 1)
        gmine.append(gs)
    packed = [_pack(t)[None] for t in ([wts[k] for k in small], [mom[k] for k in small], [var[k] for k in small])]
    outs = _adam("adam_small", packed[0], packed[1], packed[2], _pack(gmine)[None, None], 0, None)
    unpacked = [_unpack(o[0], [wts[k].shape for k in small]) for o in outs]
    for i, k in enumerate(small):
        res[k] = tuple(u[i] for u in unpacked)

    for k in TRANSPOSED:
        res[k] = tuple(jnp.transpose(r, (0, 2, 1)) for r in res[k])
    return (loss, grad_x, *[res[k][0] for k in WEIGHTS], *[res[k][1] for k in WEIGHTS],
            *[res[k][2] for k in WEIGHTS], *[res[k][3] for k in WEIGHTS])
```

```python
import functools

import jax
import jax.numpy as jnp
from jax import lax
from jax.experimental import pallas as pl
from jax.experimental.pallas import tpu as pltpu

F32, BF16 = jnp.float32, jnp.bfloat16
S = 4096
D = 1024
E = 2048
FF = 2816
FQ = FF // 4
NQ = 4
DEPTH = 4
ALPHA = (2 * DEPTH) ** 0.25
LN_EPS = 1e-5
CONV_A, CONV_C = 31, 3
HALO_A, HALO_C = 32, 8
SGU_T, SGU_H, SGU_G, SGU_CHUNK = 128, 8, 256, 64
VMEM_LIMIT = 56 * 1024 * 1024
DW_BLOCK_BUDGET = 40 * 1024 * 1024
MESH = pl.DeviceIdType.MESH
ADAM_LR, ADAM_B1, ADAM_B2, ADAM_EPS, ADAM_WD, ADAM_STEP = 0.001, 0.9, 0.999, 1e-08, 0.01, 10
GELU_C, GELU_A = 0.7978845608028654, 0.044715

BIG = ["a_w_pw1", "a_w_pw2", "b_w_in", "b_w_out", "c_w_in", "c_w_out",
       "ffn_w_gate", "ffn_w_up", "ffn_w_down", "ple_w_gate", "ple_w_proj"]
TRANSPOSED = ["ffn_w_gate", "ffn_w_up"]
ROW_SHARDED = ["a_w_pw2", "b_w_out", "c_w_out", "ffn_w_gate", "ffn_w_up", "ffn_w_down", "ple_w_gate"]
SMALL_SHARDED = ["a_b_pw1", "a_w_dw", "a_b_dw", "a_ln_g", "a_ln_b", "c_w_conv"]
SMALL_REPL = ["b_b_in", "b_ln_g", "b_ln_b", "b_w_s", "b_b_s", "ln1_g", "ln1_b", "ln2_g", "ln2_b", "ple_norm_g"]
WEIGHTS = ["a_w_pw1", "a_b_pw1", "a_w_dw", "a_b_dw", "a_ln_g", "a_ln_b", "a_w_pw2", "b_w_in", "b_b_in", "b_ln_g",
           "b_ln_b", "b_w_s", "b_b_s", "b_w_out", "c_w_in", "c_w_conv", "c_w_out", "ln1_g", "ln1_b", "ln2_g",
           "ln2_b", "ffn_w_gate", "ffn_w_up", "ffn_w_down", "ple_w_gate", "ple_w_proj", "ple_norm_g"]


def _call(name, body, grid, in_specs, out_specs, out_shape, scratch=(), aliases=None, hosts=()):
    params = pltpu.CompilerParams(dimension_semantics=("arbitrary",) * len(grid), vmem_limit_bytes=VMEM_LIMIT)
    if not hosts:
        return pl.pallas_call(
            body, name=name, grid=grid, in_specs=in_specs, out_specs=out_specs, out_shape=out_shape,
            scratch_shapes=list(scratch), input_output_aliases=aliases or {}, compiler_params=params)
    assert len(grid) == 1 and not aliases
    single = not isinstance(out_shape, (list, tuple))
    own_shapes = [out_shape] if single else list(out_shape)
    own_specs = [out_specs] if single else list(out_specs)
    n_in, n_out, n_scr = len(in_specs), len(own_shapes), len(scratch)
    h_in = [len(h.arrays) for h in hosts]
    h_out = [len(h.out_shapes) for h in hosts]
    h_sem = [len(h.sems) for h in hosts]

    def split(refs, counts):
        out, off = [], 0
        for cnt in counts:
            out.append(refs[off:off + cnt])
            off += cnt
        return out

    def wrapped(*refs):
        ins, hin, outs, hout, scr, hsem = split(refs, [n_in, sum(h_in), n_out, sum(h_out), n_scr, sum(h_sem)])
        per_host = list(zip(hosts, split(hin, h_in), split(hout, h_out), split(hsem, h_sem)))

        @pl.when(pl.program_id(0) == 0)
        def _():
            for h, a, o, s in per_host:
                h.start(a, o, s)

        body(*ins, *outs, *scr)

        for h, a, o, s in per_host:
            for step, ws in sorted(h.forward_steps(grid[0]).items()):
                pl.when(pl.program_id(0) == step)(functools.partial(h.forward, ws, a, o, s))

        @pl.when(pl.program_id(0) == grid[0] - 1)
        def _():
            for h, a, o, s in per_host:
                h.complete(a, o, s)

    any_spec = pl.BlockSpec(memory_space=pl.ANY)
    call = pl.pallas_call(
        wrapped, name=name, grid=grid, in_specs=list(in_specs) + [any_spec] * sum(h_in),
        out_specs=own_specs + [any_spec] * sum(h_out),
        out_shape=own_shapes + [s for h in hosts for s in h.out_shapes],
        scratch_shapes=list(scratch) + [s for h in hosts for s in h.sems], compiler_params=params)

    def run(*args):
        res = call(*args, *[a for h in hosts for a in h.arrays])
        own = res[0] if single else list(res[:n_out])
        return own, split(list(res[n_out:]), h_out)

    return run


def _sds(shape, dtype=F32):
    return jax.ShapeDtypeStruct(shape, dtype)


def _row(tm, c):
    return pl.BlockSpec((tm, c), lambda i: (i, 0))


def _grow(g, tm, c):
    return pl.BlockSpec((g, tm, c), lambda i: (0, i, 0))


def _const(shape):
    nd = len(shape)
    return pl.BlockSpec(shape, lambda i: (0,) * nd, pipeline_mode=pl.Buffered(1))


def _wspec(w):
    return pl.BlockSpec((NQ, None, w.shape[2], w.shape[3]), lambda i: (0, 0, 0, 0), pipeline_mode=pl.Buffered(1))


def _prev(tm, hb, c):
    return pl.BlockSpec((hb, c), lambda i: (jnp.maximum(i * (tm // hb) - 1, 0), 0))


def _next(tm, hb, c):
    return pl.BlockSpec((hb, c), lambda i: (jnp.minimum((i + 1) * (tm // hb), S // hb - 1), 0))


def _acc(r, c):
    return pl.BlockSpec((r, c), lambda i: (0, 0))


def _sig(x):
    return 1.0 / (1.0 + jnp.exp(-x))


def _ln(z, g, b):
    mu = jnp.mean(z, axis=-1, keepdims=True)
    zc = z - mu
    rstd = lax.rsqrt(jnp.mean(zc * zc, axis=-1, keepdims=True) + LN_EPS)
    xhat = zc * rstd
    return xhat * g + b, xhat, rstd


def _ln_bwd(dyg, xhat, rstd):
    return rstd * (dyg - jnp.mean(dyg, axis=-1, keepdims=True) - xhat * jnp.mean(dyg * xhat, axis=-1, keepdims=True))


def _mm(a, w):
    return jnp.dot(a.astype(BF16), w, preferred_element_type=F32)


def _mmt(a, w):
    return lax.dot_general(a.astype(BF16), w, (((1,), (1,)), ((), ())), preferred_element_type=F32)


def _colsum(x):
    return jnp.sum(x, axis=0, keepdims=True)


def _gelu_and_grad(x):
    x2 = x * x
    t = jnp.tanh(x * (GELU_C + (GELU_C * GELU_A) * x2))
    hx = 0.5 * x
    return hx + hx * t, 0.5 + 0.5 * t + hx * (1.0 - t * t) * (GELU_C + (3.0 * GELU_C * GELU_A) * x2)


def _silu_grad(a, sg):
    return sg * (1.0 + a * (1.0 - sg))


def _sgu_masks():
    r = lax.broadcasted_iota(jnp.int32, (SGU_T, SGU_T), 0) // SGU_CHUNK
    c = lax.broadcasted_iota(jnp.int32, (SGU_T, SGU_T), 1) // SGU_CHUNK
    return r >= c, c >= r


def _fill_halo(buf, lo, n, halo_val_fn, is_edge):
    @pl.when(is_edge)
    def _():
        buf[lo:lo + n, :] = jnp.zeros((n, buf.shape[1]), F32)

    @pl.when(jnp.logical_not(is_edge))
    def _():
        buf[lo:lo + n, :] = halo_val_fn()


SUB, LANE = 8, 128
ROWS_AT_ONCE = 16


def _shift_copies(buf, sh):
    rows = sh.shape[1]
    for s in range(1, SUB):
        sh[s - 1, :, :] = buf[pl.ds(s, rows), :]


def _tiles(buf, sh, s, first, count, group0, lanes):
    src = buf if s == 0 else sh.at[s - 1]
    return {t: src[pl.ds(pl.multiple_of((group0 + t) * SUB, SUB), SUB), lanes] for t in range(first, first + count)}


def _by_shift(offsets):
    out = []
    for s in range(SUB):
        taps = [(k, o // SUB) for k, o in enumerate(offsets) if o % SUB == s]
        if taps:
            out.append((s, taps))
    return out


def _conv_rows(out_ref, w_ref, bias_ref, offsets, buf, sh, tm):
    n = ROWS_AT_ONCE
    for cb in range(D // LANE):
        lanes = slice(cb * LANE, (cb + 1) * LANE)
        bias = None if bias_ref is None else jnp.broadcast_to(bias_ref[:, lanes], (SUB, LANE))

        def body(jb, carry):
            accs = [bias] * n
            for s, taps in _by_shift(offsets):
                ms = [m for _, m in taps]
                tiles = _tiles(buf, sh, s, min(ms), max(ms) - min(ms) + n, jb * n, lanes)
                for k, m in taps:
                    wk = jnp.broadcast_to(w_ref[k:k + 1, lanes], (SUB, LANE))
                    for jj in range(n):
                        t = wk * tiles[m + jj]
                        accs[jj] = t if accs[jj] is None else accs[jj] + t
            for jj in range(n):
                out_ref[pl.ds(pl.multiple_of((jb * n + jj) * SUB, SUB), SUB), lanes] = accs[jj]
            return carry

        lax.fori_loop(0, tm // (SUB * n), body, 0)


def _conv_wgrad(dw_ref, d_ref, offsets, buf, sh, tm):
    n = 4
    for cb in range(D // LANE):
        lanes = slice(cb * LANE, (cb + 1) * LANE)

        def body(jq, accs):
            accs = list(accs)
            d = [d_ref[pl.ds(pl.multiple_of((jq * n + jj) * SUB, SUB), SUB), lanes] for jj in range(n)]
            for s, taps in _by_shift(offsets):
                ms = [m for _, m in taps]
                tiles = _tiles(buf, sh, s, min(ms), max(ms) - min(ms) + n, jq * n, lanes)
                for k, m in taps:
                    for jj in range(n):
                        accs[k] = accs[k] + d[jj] * tiles[m + jj]
            return tuple(accs)

        accs = lax.fori_loop(0, tm // (SUB * n), body, tuple(jnp.zeros((SUB, LANE), F32) for _ in offsets))
        for k, acc in enumerate(accs):
            dw_ref[k:k + 1, lanes] += jnp.sum(acc, axis=0, keepdims=True)


def _fwd_a1(x0, w1, b1, l, hosts=()):
    tm = 512

    def body(x_ref, w_ref, b_ref, h_ref, glu_ref):
        xb = x_ref[...].astype(BF16)
        for q in range(NQ):
            sl = slice(q * 512, (q + 1) * 512)
            h_ref[:, sl] = jnp.dot(xb, w_ref[q], preferred_element_type=F32) + b_ref[:, sl]
        glu_ref[...] = h_ref[:, :D] * _sig(h_ref[:, D:])

    return _call(f"fwd_a1_{l}", body, (S // tm,), [_row(tm, D), _wspec(w1), _const((1, 2 * D))],
                 [_row(tm, 2 * D), _row(tm, D)], [_sds((S, 2 * D)), _sds((S, D))], hosts=hosts)(x0, w1, b1)


def _fwd_a2(glu, x0, wdw, bdw, lg, lb, w2, l, hosts=()):
    tm = 256

    def body(g_ref, gp_ref, x_ref, wdw_ref, bdw_ref, lg_ref, lb_ref, w2_ref, z_ref, cv_ref, buf, sh):
        i = pl.program_id(0)
        _fill_halo(buf, 0, HALO_A, lambda: gp_ref[...], i == 0)
        buf[HALO_A:HALO_A + tm, :] = g_ref[...]
        _shift_copies(buf, sh)
        _conv_rows(cv_ref, wdw_ref, bdw_ref, [HALO_A - (CONV_A - 1) + k for k in range(CONV_A)], buf, sh, tm)
        n, _, _ = _ln(cv_ref[...], lg_ref[...], lb_ref[...])
        sb = (n * _sig(n)).astype(BF16)
        z_ref[...] = ALPHA * x_ref[...] + jnp.dot(sb, w2_ref[...], preferred_element_type=F32)

    return _call(f"fwd_a2_{l}", body, (S // tm,),
                 [_row(tm, D), _prev(tm, HALO_A, D), _row(tm, D), _const((32, D)), _const((1, D)), _const((1, D)),
                  _const((1, D)), _const(w2.shape)],
                 [_row(tm, D), _row(tm, D)], [_sds((S, D)), _sds((S, D))],
                 scratch=[pltpu.VMEM((HALO_A + tm, D), F32), pltpu.VMEM((SUB - 1, HALO_A + tm - SUB, D), F32)],
                 hosts=hosts)(glu, glu, x0, wdw, bdw, lg, lb, w2)


def _fwd_b(x0, win, b_in, lg, lb, ws, bsx, wout, hosts=()):
    tm = 256

    def body(x_ref, win_ref, bin_ref, lg_ref, lb_ref, ws_ref, bsx_ref, wout_ref, z_ref, zg_ref, gg_ref, f_scr, h_ref):
        xb = x_ref[...].astype(BF16)
        for q in range(NQ):
            sl = slice(q * 1024, (q + 1) * 1024)
            h_ref[:, sl] = jnp.dot(xb, win_ref[q], preferred_element_type=F32) + bin_ref[:, sl]
        u, du = _gelu_and_grad(h_ref[:, :E])
        v, dv = _gelu_and_grad(h_ref[:, E:])
        zg_ref[:, 0:E] = u.astype(BF16)
        zg_ref[:, E:2 * E] = v.astype(BF16)
        gg_ref[:, 0:E] = du.astype(BF16)
        gg_ref[:, E:2 * E] = dv.astype(BF16)
        vn, _, _ = _ln(v, lg_ref[...], lb_ref[...])
        vnb = vn.astype(BF16)
        mask, _ = _sgu_masks()
        for hd in range(SGU_H):
            wm = jnp.where(mask, ws_ref[hd], 0.0).astype(BF16)
            cs = slice(hd * SGU_G, (hd + 1) * SGU_G)
            for n in range(tm // SGU_T):
                rs = slice(n * SGU_T, (n + 1) * SGU_T)
                f_scr[rs, cs] = jnp.dot(wm, vnb[rs, cs], preferred_element_type=F32) + bsx_ref[hd]
        mb = (u * f_scr[...]).astype(BF16)
        z_ref[...] = ALPHA * x_ref[...] + jnp.dot(mb, wout_ref[...], preferred_element_type=F32)

    return _call("fwd_b", body, (S // tm,),
                 [_row(tm, D), _wspec(win), _const((1, 2 * E)), _const((1, E)), _const((1, E)),
                  _const((SGU_H, SGU_T, SGU_T)), _const((SGU_H, SGU_T, SGU_G)), _const(wout.shape)],
                 [_row(tm, D), _row(tm, 2 * E), _row(tm, 2 * E)],
                 [_sds((S, D)), _sds((S, 2 * E), BF16), _sds((S, 2 * E), BF16)],
                 scratch=[pltpu.VMEM((tm, E), F32), pltpu.VMEM((tm, 2 * E), F32)], hosts=hosts
                 )(x0, win, b_in, lg, lb, ws, bsx, wout)


def _fwd_c1(x0, win, hosts=()):
    tm = 512

    def body(x_ref, w_ref, hc_ref):
        xb = x_ref[...].astype(BF16)
        for q in range(NQ):
            hc_ref[:, q * 768:(q + 1) * 768] = jnp.dot(xb, w_ref[q], preferred_element_type=F32)

    return _call("fwd_c1", body, (S // tm,), [_row(tm, D), _wspec(win)], _row(tm, 3 * D),
                 _sds((S, 3 * D)), hosts=hosts)(x0, win)


def _short_conv(buf, hc_ref, hcp_ref, wc_ref, tm, i):
    _fill_halo(buf, 0, HALO_C, lambda: hcp_ref[:, D:2 * D] * hcp_ref[:, 2 * D:], i == 0)
    buf[HALO_C:HALO_C + tm, :] = hc_ref[:, D:2 * D] * hc_ref[:, 2 * D:]
    y = wc_ref[0:1, :] * buf[pl.ds(HALO_C - 2, tm), :]
    for k in range(1, CONV_C):
        y = y + wc_ref[k:k + 1, :] * buf[pl.ds(HALO_C - 2 + k, tm), :]
    return y


def _fwd_c2(hc, x0, wc, wout, hosts=()):
    tm = 256

    def body(hc_ref, hcp_ref, x_ref, wc_ref, wout_ref, z_ref, buf):
        y = _short_conv(buf, hc_ref, hcp_ref, wc_ref, tm, pl.program_id(0))
        mb = (hc_ref[:, :D] * y).astype(BF16)
        z_ref[...] = ALPHA * x_ref[...] + jnp.dot(mb, wout_ref[...], preferred_element_type=F32)

    return _call("fwd_c2", body, (S // tm,),
                 [_row(tm, 3 * D), _prev(tm, HALO_C, 3 * D), _row(tm, D), _const((8, D)), _const(wout.shape)],
                 _row(tm, D), _sds((S, D)), scratch=[pltpu.VMEM((HALO_C + tm, D), F32)], hosts=hosts
                 )(hc, hc, x0, wc, wout)


def _fwd_ffn(z1, lg, lb, wgt, wut, wd, l, hosts=()):
    tm = 256

    def body(z_ref, lg_ref, lb_ref, wg_ref, wu_ref, wd_ref, o_ref, a_ref, u_ref, hm_ref):
        x1, _, _ = _ln(z_ref[...], lg_ref[...], lb_ref[...])
        xb = x1.astype(BF16)
        a = _mmt(xb, wg_ref[...])
        u = _mmt(xb, wu_ref[...])
        hmb = (a * _sig(a) * u).astype(BF16)
        a_ref[...] = a.astype(BF16)
        u_ref[...] = u.astype(BF16)
        hm_ref[...] = hmb
        o_ref[...] = ALPHA * x1 + jnp.dot(hmb, wd_ref[...], preferred_element_type=F32)

    return _call(f"fwd_ffn_{l}", body, (S // tm,),
                 [_row(tm, D), _const((1, D)), _const((1, D)), _const((FF, D)), _const((FF, D)), _const((FF, D))],
                 [_row(tm, D), _row(tm, FF), _row(tm, FF), _row(tm, FF)],
                 [_sds((S, D)), _sds((S, FF), BF16), _sds((S, FF), BF16), _sds((S, FF), BF16)],
                 hosts=hosts)(z1, lg, lb, wgt, wut, wd)


def _ple_parts(z2, p, lg, lb, wg_ref, wp_ref, pg):
    x2, xhat, rstd = _ln(z2, lg, lb)
    xb = x2.astype(BF16)
    gate = _sig(jnp.dot(xb, wg_ref[...], preferred_element_type=F32))
    pb = p.astype(BF16)
    qp = jnp.concatenate([jnp.dot(pb, wp_ref[q], preferred_element_type=F32) for q in range(NQ)], axis=1)
    rs = lax.rsqrt(jnp.mean(qp * qp, axis=-1, keepdims=True) + LN_EPS)
    qn = qp * rs
    return x2, xhat, rstd, xb, gate, qn, rs, qn * pg


def _fwd_ple(z2, p, lg, lb, wg, wp, pg, l, hosts=()):
    tm = 512

    def body(z_ref, p_ref, lg_ref, lb_ref, wg_ref, wp_ref, pg_ref, o_ref):
        x2, _, _, _, gate, _, _, r = _ple_parts(z_ref[...], p_ref[...], lg_ref[...], lb_ref[...], wg_ref, wp_ref,
                                                pg_ref[...])
        o_ref[...] = x2 + gate * r

    return _call(f"fwd_ple_{l}", body, (S // tm,),
                 [_row(tm, D), _row(tm, 256), _const((1, D)), _const((1, D)), _const(wg.shape), _wspec(wp),
                  _const((1, D))],
                 _row(tm, D), _sds((S, D)), hosts=hosts)(z2, p, lg, lb, wg, wp, pg)


def _loss_head(y, target):
    tm = 512

    def body(y_ref, t_ref, dy_ref, acc_ref):
        @pl.when(pl.program_id(0) == 0)
        def _():
            acc_ref[...] = jnp.zeros_like(acc_ref)

        e = y_ref[...] - t_ref[...]
        dy_ref[...] = e * (1.0 / D)
        acc_ref[0:1, :] += _colsum(e * e)

    return _call("loss_head", body, (S // tm,), [_row(tm, D), _row(tm, D)], [_row(tm, D), _acc(8, D)],
                 [_sds((S, D)), _sds((8, D))])(y, target)


def _zero_first(*refs):
    @pl.when(pl.program_id(0) == 0)
    def _():
        for r in refs:
            r[...] = jnp.zeros_like(r)


def _bwd_ple(g, z2, p, lg, lb, wg, wp, pg, l, hosts=()):
    tm = 256

    def body(g_ref, z_ref, p_ref, lg_ref, lb_ref, wg_ref, wp_ref, pg_ref, dz_ref, xb_ref, dgp_ref, dqp_ref, acc_ref):
        _zero_first(acc_ref)
        gin = g_ref[...]
        lgv, pgv = lg_ref[...], pg_ref[...]
        _, xhat, rstd, xb, gate, qn, rs, r = _ple_parts(z_ref[...], p_ref[...], lgv, lb_ref[...], wg_ref, wp_ref, pgv)
        xb_ref[...] = xb
        dgpb = (gin * r * gate * (1.0 - gate)).astype(BF16)
        dgp_ref[...] = dgpb
        dx2 = gin + _mmt(dgpb, wg_ref[...])
        dr = gin * gate
        acc_ref[0:1, :] += _colsum(dr * qn)
        t = dr * pgv
        dqp_ref[...] = (rs * (t - qn * jnp.mean(t * qn, axis=-1, keepdims=True))).astype(BF16)
        acc_ref[1:2, :] += _colsum(dx2 * xhat)
        acc_ref[2:3, :] += _colsum(dx2)
        dz_ref[...] = _ln_bwd(dx2 * lgv, xhat, rstd)

    return _call(f"bwd_ple_{l}", body, (S // tm,),
                 [_row(tm, D), _row(tm, D), _row(tm, 256), _const((1, D)), _const((1, D)), _const(wg.shape),
                  _wspec(wp), _const((1, D))],
                 [_row(tm, D), _row(tm, D), _row(tm, D), _row(tm, D), _acc(8, D)],
                 [_sds((S, D)), _sds((S, D), BF16), _sds((S, D), BF16), _sds((S, D), BF16), _sds((8, D))],
                 hosts=hosts)(g, z2, p, lg, lb, wg, wp, pg)


def _bwd_ffn(dz2, z1, ab, ub, lg, lb, wgt, wut, wd, l, hosts=()):
    tm = 256

    def body(dz2_ref, z_ref, a_ref, u_ref, lg_ref, lb_ref, wg_ref, wu_ref, wd_ref, dz1_ref, xb_ref, da_ref, du_ref,
             acc_ref):
        _zero_first(acc_ref)
        dz2v = dz2_ref[...]
        lgv = lg_ref[...]
        x1, xhat, rstd = _ln(z_ref[...], lgv, lb_ref[...])
        xb_ref[...] = x1.astype(BF16)
        a = a_ref[...].astype(F32)
        u = u_ref[...].astype(F32)
        sg = _sig(a)
        dhm = _mmt(dz2v, wd_ref[...])
        dub = (dhm * (a * sg)).astype(BF16)
        dab = (dhm * u * _silu_grad(a, sg)).astype(BF16)
        da_ref[...] = dab
        du_ref[...] = dub
        dx1 = ALPHA * dz2v + _mm(dab, wg_ref[...]) + _mm(dub, wu_ref[...])
        acc_ref[0:1, :] += _colsum(dx1 * xhat)
        acc_ref[1:2, :] += _colsum(dx1)
        dz1_ref[...] = _ln_bwd(dx1 * lgv, xhat, rstd)

    return _call(f"bwd_ffn_{l}", body, (S // tm,),
                 [_row(tm, D), _row(tm, D), _row(tm, FF), _row(tm, FF), _const((1, D)), _const((1, D)),
                  _const((FF, D)), _const((FF, D)), _const((FF, D))],
                 [_row(tm, D), _row(tm, D), _row(tm, FF), _row(tm, FF), _acc(8, D)],
                 [_sds((S, D)), _sds((S, D), BF16), _sds((S, FF), BF16), _sds((S, FF), BF16), _sds((8, D))],
                 hosts=hosts)(dz2, z1, ab, ub, lg, lb, wgt, wut, wd)


def _bwd_a2(dz1, cv, lg, lb, w2, l, hosts=()):
    tm = 512

    def body(dz_ref, cv_ref, lg_ref, lb_ref, w2_ref, dcv_ref, sb_ref, acc_ref):
        _zero_first(acc_ref)
        lgv = lg_ref[...]
        n, xhat, rstd = _ln(cv_ref[...], lgv, lb_ref[...])
        sg = _sig(n)
        sb_ref[...] = (n * sg).astype(BF16)
        dzb = dz_ref[...].astype(BF16)
        ds = _mmt(dzb, w2_ref[...])
        dn = ds * _silu_grad(n, sg)
        acc_ref[0:1, :] += _colsum(dn * xhat)
        acc_ref[1:2, :] += _colsum(dn)
        dcv = _ln_bwd(dn * lgv, xhat, rstd)
        acc_ref[2:3, :] += _colsum(dcv)
        dcv_ref[...] = dcv

    return _call(f"bwd_a2_{l}", body, (S // tm,),
                 [_row(tm, D), _row(tm, D), _const((1, D)), _const((1, D)), _const(w2.shape)],
                 [_row(tm, D), _row(tm, D), _acc(8, D)],
                 [_sds((S, D)), _sds((S, D), BF16), _sds((8, D))], hosts=hosts)(dz1, cv, lg, lb, w2)


def _bwd_conv_a(dcv, glu, wdw, l, hosts=()):
    tm = 256
    nb = S // tm

    def body(d_ref, dn_ref, g_ref, gp_ref, w_ref, dglu_ref, dw_ref, bufd, bufx, sh):
        i = pl.program_id(0)
        _zero_first(dw_ref)
        bufd[0:tm, :] = d_ref[...]
        _fill_halo(bufd, tm, HALO_A, lambda: dn_ref[...], i == nb - 1)
        _fill_halo(bufx, 0, HALO_A, lambda: gp_ref[...], i == 0)
        bufx[HALO_A:HALO_A + tm, :] = g_ref[...]
        _shift_copies(bufd, sh)
        _conv_rows(dglu_ref, w_ref, None, [CONV_A - 1 - k for k in range(CONV_A)], bufd, sh, tm)
        _shift_copies(bufx, sh)
        _conv_wgrad(dw_ref, d_ref, [HALO_A - (CONV_A - 1) + k for k in range(CONV_A)], bufx, sh, tm)

    return _call(f"bwd_conv_a_{l}", body, (nb,),
                 [_row(tm, D), _next(tm, HALO_A, D), _row(tm, D), _prev(tm, HALO_A, D), _const((32, D))],
                 [_row(tm, D), _acc(32, D)], [_sds((S, D)), _sds((32, D))],
                 scratch=[pltpu.VMEM((tm + HALO_A, D), F32), pltpu.VMEM((HALO_A + tm, D), F32),
                          pltpu.VMEM((SUB - 1, HALO_A + tm - SUB, D), F32)], hosts=hosts)(dcv, dcv, glu, glu, wdw)


def _bwd_a1(dglu, h, dz1, w1, l, hosts=()):
    tm = 256

    def body(dg_ref, h_ref, dz_ref, w_ref, dx_ref, dh_ref, acc_ref):
        _zero_first(acc_ref)
        a, g = h_ref[:, :D], h_ref[:, D:]
        sg = _sig(g)
        dgl = dg_ref[...]
        da = dgl * sg
        dg = dgl * a * sg * (1.0 - sg)
        acc_ref[0:1, 0:D] += _colsum(da)
        acc_ref[0:1, D:2 * D] += _colsum(dg)
        dh_ref[:, 0:D] = da.astype(BF16)
        dh_ref[:, D:2 * D] = dg.astype(BF16)
        dx = ALPHA * dz_ref[...]
        for q in range(NQ):
            dx = dx + _mmt(dh_ref[:, q * 512:(q + 1) * 512], w_ref[q])
        dx_ref[...] = dx

    return _call(f"bwd_a1_{l}", body, (S // tm,),
                 [_row(tm, D), _row(tm, 2 * D), _row(tm, D), _wspec(w1)],
                 [_row(tm, D), _row(tm, 2 * D), _acc(8, 2 * D)],
                 [_sds((S, D)), _sds((S, 2 * D), BF16), _sds((8, 2 * D))], hosts=hosts)(dglu, h, dz1, w1)


def _bwd_c2(dz1, hc, wc, wout):
    tm = 256

    def body(dz_ref, hc_ref, hcp_ref, wc_ref, wout_ref, dy_ref, dbg_ref, mb_ref, buf):
        y = _short_conv(buf, hc_ref, hcp_ref, wc_ref, tm, pl.program_id(0))
        dzb = dz_ref[...].astype(BF16)
        dm = _mmt(dzb, wout_ref[...])
        bg = hc_ref[:, :D]
        mb_ref[...] = (bg * y).astype(BF16)
        dbg_ref[...] = (dm * y).astype(BF16)
        dy_ref[...] = dm * bg

    return _call("bwd_c2", body, (S // tm,),
                 [_row(tm, D), _row(tm, 3 * D), _prev(tm, HALO_C, 3 * D), _const((8, D)), _const(wout.shape)],
                 [_row(tm, D), _row(tm, D), _row(tm, D)],
                 [_sds((S, D)), _sds((S, D), BF16), _sds((S, D), BF16)],
                 scratch=[pltpu.VMEM((HALO_C + tm, D), F32)])(dz1, hc, hc, wc, wout)


def _bwd_c1(dy, hc, dbg, dz1, wc, win):
    tm = 256
    nb = S // tm

    def body(d_ref, dn_ref, hc_ref, hcp_ref, dbg_ref, dz_ref, wc_ref, win_ref, dx_ref, dhc_ref, dwc_ref, bufd, bufq):
        i = pl.program_id(0)
        _zero_first(dwc_ref)
        bufd[0:tm, :] = d_ref[...]
        _fill_halo(bufd, tm, HALO_C, lambda: dn_ref[...], i == nb - 1)
        _fill_halo(bufq, 0, HALO_C, lambda: hcp_ref[:, D:2 * D] * hcp_ref[:, 2 * D:], i == 0)
        bufq[HALO_C:HALO_C + tm, :] = hc_ref[:, D:2 * D] * hc_ref[:, 2 * D:]
        dq = wc_ref[0:1, :] * bufd[pl.ds(CONV_C - 1, tm), :]
        for k in range(1, CONV_C):
            dq = dq + wc_ref[k:k + 1, :] * bufd[pl.ds(CONV_C - 1 - k, tm), :]
        dv = d_ref[...]
        for k in range(CONV_C):
            dwc_ref[k:k + 1, :] += _colsum(dv * bufq[pl.ds(HALO_C - (CONV_C - 1) + k, tm), :])
        dhc_ref[:, 0:D] = dbg_ref[...]
        dhc_ref[:, D:2 * D] = (dq * hc_ref[:, 2 * D:]).astype(BF16)
        dhc_ref[:, 2 * D:3 * D] = (dq * hc_ref[:, D:2 * D]).astype(BF16)
        dx = ALPHA * dz_ref[...]
        for q in range(NQ):
            dx = dx + _mmt(dhc_ref[:, q * 768:(q + 1) * 768], win_ref[q])
        dx_ref[...] = dx

    return _call("bwd_c1", body, (nb,),
                 [_row(tm, D), _next(tm, HALO_C, D), _row(tm, 3 * D), _prev(tm, HALO_C, 3 * D), _row(tm, D),
                  _row(tm, D), _const((8, D)), _wspec(win)],
                 [_row(tm, D), _row(tm, 3 * D), _acc(8, D)],
                 [_sds((S, D)), _sds((S, 3 * D), BF16), _sds((8, D))],
                 scratch=[pltpu.VMEM((tm + HALO_C, D), F32), pltpu.VMEM((HALO_C + tm, D), F32)]
                 )(dy, dy, hc, hc, dbg, dz1, wc, win)


def _bwd_b(dz1, zg, gg, lg, lb, win, wout, ws, wst, bsx):
    tm = 128
    nb = S // tm

    def body(dz_ref, zg_ref, gg_ref, lg_ref, lb_ref, win_ref, wout_ref, ws_ref, wst_ref, bsx_ref,
             dx_ref, dh_ref, mb_ref, acc_ref, dws_ref, dbs_ref, f_scr, dvn_scr):
        _zero_first(acc_ref, dws_ref, dbs_ref)
        lgv = lg_ref[...]
        u = zg_ref[:, :E].astype(F32)
        v = zg_ref[:, E:].astype(F32)
        vn, xhat, rstd = _ln(v, lgv, lb_ref[...])
        vnb = vn.astype(BF16)
        dzb = dz_ref[...].astype(BF16)
        dm = _mmt(dzb, wout_ref[...])
        mask, mask_t = _sgu_masks()
        for hd in range(SGU_H):
            wm = jnp.where(mask, ws_ref[hd], 0.0).astype(BF16)
            cs = slice(hd * SGU_G, (hd + 1) * SGU_G)
            for n in range(tm // SGU_T):
                rs = slice(n * SGU_T, (n + 1) * SGU_T)
                f_scr[rs, cs] = jnp.dot(wm, vnb[rs, cs], preferred_element_type=F32) + bsx_ref[hd]
        f = f_scr[...]
        mb_ref[...] = (u * f).astype(BF16)
        du = dm * f
        df = dm * u
        dfb = df.astype(BF16)
        for hd in range(SGU_H):
            wmt = jnp.where(mask_t, wst_ref[hd], 0.0).astype(BF16)
            cs = slice(hd * SGU_G, (hd + 1) * SGU_G)
            for n in range(tm // SGU_T):
                rs = slice(n * SGU_T, (n + 1) * SGU_T)
                dvn_scr[rs, cs] = jnp.dot(wmt, dfb[rs, cs], preferred_element_type=F32)
                dws_ref[hd] += lax.dot_general(dfb[rs, cs], vnb[rs, cs], (((1,), (1,)), ((), ())),
                                               preferred_element_type=F32)
                dbs_ref[hd] += df[rs, cs]
        dvn = dvn_scr[...]
        acc_ref[1:2, 0:E] += _colsum(dvn * xhat)
        acc_ref[2:3, 0:E] += _colsum(dvn)
        dv = _ln_bwd(dvn * lgv, xhat, rstd)
        dhu = du * gg_ref[:, :E].astype(F32)
        dhv = dv * gg_ref[:, E:].astype(F32)
        acc_ref[0:1, 0:E] += _colsum(dhu)
        acc_ref[0:1, E:2 * E] += _colsum(dhv)
        dh_ref[:, 0:E] = dhu.astype(BF16)
        dh_ref[:, E:2 * E] = dhv.astype(BF16)
        dx = ALPHA * dz_ref[...]
        for q in range(NQ):
            dx = dx + _mmt(dh_ref[:, q * 1024:(q + 1) * 1024], win_ref[q])
        dx_ref[...] = dx

        @pl.when(pl.program_id(0) == nb - 1)
        def _():
            for hd in range(SGU_H):
                dws_ref[hd] = jnp.where(mask, dws_ref[hd], 0.0)

    c3 = lambda a, b, c: pl.BlockSpec((a, b, c), lambda i: (0, 0, 0))
    return _call("bwd_b", body, (nb,),
                 [_row(tm, D), _row(tm, 2 * E), _row(tm, 2 * E), _const((1, E)), _const((1, E)), _wspec(win),
                  _const(wout.shape), _const((SGU_H, SGU_T, SGU_T)), _const((SGU_H, SGU_T, SGU_T)),
                  _const((SGU_H, SGU_T, SGU_G))],
                 [_row(tm, D), _row(tm, 2 * E), _row(tm, E), _acc(8, 2 * E), c3(SGU_H, SGU_T, SGU_T),
                  c3(SGU_H, SGU_T, SGU_G)],
                 [_sds((S, D)), _sds((S, 2 * E), BF16), _sds((S, E), BF16), _sds((8, 2 * E)),
                  _sds((SGU_H, SGU_T, SGU_T)), _sds((SGU_H, SGU_T, SGU_G))],
                 scratch=[pltpu.VMEM((tm, E), F32), pltpu.VMEM((tm, E), F32)]
                 )(dz1, zg, gg, lg, lb, win, wout, ws, wst, bsx)


def _mm_tn(name, a, amode, b, bmode, k, n, groups=NQ, hosts=()):
    def block_bytes(ts):
        ka = k if amode == "1" else groups * k
        nb = n if bmode == "1" else groups * n
        return 2 * (ts * ka * a.dtype.itemsize + ts * nb * b.dtype.itemsize + groups * k * n * 4)

    ts = min(1024 if block_bytes(1024) <= DW_BLOCK_BUDGET else 512, S)

    def spec(mode, w):
        if mode == "1":
            return pl.BlockSpec((ts, w), lambda s: (s, 0))
        if mode == "c":
            return pl.BlockSpec((ts, groups * w), lambda s: (s, 0))
        return pl.BlockSpec((groups, ts, w), lambda s: (0, s, 0))

    def pick(ref, mode, w, g):
        if mode == "1":
            return ref[...]
        if mode == "c":
            return ref[:, g * w:(g + 1) * w]
        return ref[g]

    def body(a_ref, b_ref, o_ref):
        _zero_first(o_ref)
        a_t = jnp.transpose(a_ref[...].astype(BF16)) if amode == "1" else None
        b_1 = b_ref[...].astype(BF16) if bmode == "1" else None
        for g in range(groups):
            lhs = a_t if amode == "1" else jnp.transpose(pick(a_ref, amode, k, g).astype(BF16))
            rhs = b_1 if bmode == "1" else pick(b_ref, bmode, n, g).astype(BF16)
            o_ref[0, g] += jnp.dot(lhs, rhs, preferred_element_type=F32)

    return _call(name, body, (S // ts,), [spec(amode, k), spec(bmode, n)],
                 pl.BlockSpec((1, groups, k, n), lambda s: (0, 0, 0, 0)), _sds((1, groups, k, n)), hosts=hosts)(a, b)


def _row_block(k, cap=256):
    return max(t for t in range(16, min(k, cap) + 1, 16) if k % t == 0)


def _cast_bf16(w, hosts=()):
    nl, k, n = w.shape
    tb = _row_block(k, 512)
    nb = k // tb

    def body(w_ref, o_ref):
        o_ref[...] = w_ref[...].astype(BF16)

    spec = pl.BlockSpec((None, tb, n), lambda i: (i // nb, i % nb, 0))
    return _call("cast_bf16", body, (nl * nb,), [spec], spec, _sds(w.shape, BF16), hosts=hosts)(w)


def _adam(name, w, m, v, gc, l, prev):
    nl, k, n = w.shape
    nc = gc.shape[0]
    tb = _row_block(k, 512)

    def body(w_ref, m_ref, v_ref, g_ref, *rest):
        go_ref, d_ref, mo_ref, vo_ref = rest[-4:]
        g = g_ref[0].astype(F32)
        for c in range(1, nc):
            g = g + g_ref[c].astype(F32)
        m2 = ADAM_B1 * m_ref[...] + (1.0 - ADAM_B1) * g
        v2 = ADAM_B2 * v_ref[...] + (1.0 - ADAM_B2) * (g * g)
        m_hat = m2 / (1.0 - ADAM_B1 ** ADAM_STEP)
        v_hat = v2 / (1.0 - ADAM_B2 ** ADAM_STEP)
        go_ref[...] = g
        d_ref[...] = -ADAM_LR * (m_hat / (jnp.sqrt(v_hat) + ADAM_EPS) + ADAM_WD * w_ref[...])
        mo_ref[...] = m2
        vo_ref[...] = v2

    spec = pl.BlockSpec((None, tb, n), lambda i: (l, i, 0))
    gspec = pl.BlockSpec((nc, None, tb, n), lambda i: (0, 0, i, 0))
    in_specs, args, aliases = [spec, spec, spec, gspec], [w, m, v, gc], {}
    if prev is not None:
        in_specs += [pl.BlockSpec(memory_space=pl.ANY)] * 4
        args += list(prev)
        aliases = {4 + j: j for j in range(4)}
    return _call(name, body, (k // tb,), in_specs, [spec] * 4, [_sds(w.shape)] * 4, aliases=aliases)(*args)


def _sum8(name, g8):
    r = g8.shape[1]

    def body(g_ref, o_ref):
        acc = g_ref[0]
        for d in range(1, 8):
            acc = acc + g_ref[d]
        o_ref[...] = acc

    return _call(name, body, (1,), [pl.BlockSpec((8, r, 128), lambda i: (0, 0, 0))],
                 pl.BlockSpec((r, 128), lambda i: (0, 0)), _sds((r, 128)))(g8)


def _place():
    x, y, c = lax.axis_index("x"), lax.axis_index("y"), lax.axis_index("c")
    return x, y, c, 2 * x + y, (x, y, 1 - c), [(1 - x, y), (x, 1 - y), (1 - x, 1 - y)]


class _Exchange:
    def __init__(self, arrays, out_shapes):
        self.arrays, self.out_shapes = list(arrays), list(out_shapes)
        n = len(self.arrays)
        self.sems = [pltpu.SemaphoreType.DMA((7 * n,)), pltpu.SemaphoreType.DMA((7 * n,)),
                     pltpu.SemaphoreType.DMA((n,))]

    def _copies(self, ins, outs, sems):
        send, recv, lsem = sems
        local_src, remote_src, dst = self.maps(ins, outs)
        x, y, c, q, sib, chips = _place()

        def rcopy(w, k, qq, cc, to, src=None):
            return pltpu.make_async_remote_copy(
                src_ref=dst(w, qq, cc) if src is None else src, dst_ref=dst(w, qq, cc),
                send_sem=send.at[7 * w + k], recv_sem=recv.at[7 * w + k], device_id=to, device_id_type=MESH)

        def mine(w):
            return pltpu.make_async_copy(local_src(w), dst(w, q, c), lsem.at[w])

        def first(w):
            return [rcopy(w, 0, q, c, sib, local_src(w))] + [
                rcopy(w, 1 + j, q, c, (cx, cy, c), remote_src(w, 2 * cx + cy)) for j, (cx, cy) in enumerate(chips)]

        return rcopy, mine, first, (x, y, c), q, c, sib, chips

    def start(self, ins, outs, sems):
        _, mine, first, *_ = self._copies(ins, outs, sems)
        for w in range(len(self.arrays)):
            mine(w).start()
            for cp in first(w):
                cp.start()

    def forward_steps(self, n_steps):
        sizes = [a.size // a.shape[0] for a in self.arrays]
        plan, moved = {}, 0
        for w, size in enumerate(sizes):
            moved += size
            plan.setdefault(min(n_steps - 1, -(-moved * n_steps // sum(sizes))), []).append(w)
        return plan

    def forward(self, ws, ins, outs, sems):
        rcopy, _, _, me, _, c, sib, chips = self._copies(ins, outs, sems)
        for w in ws:
            for j, (cx, cy) in enumerate(chips):
                rcopy(w, 1 + j, 2 * cx + cy, c, me).wait_recv()
                rcopy(w, 4 + j, 2 * cx + cy, c, sib).start()

    def complete(self, ins, outs, sems):
        rcopy, mine, first, me, q, c, sib, chips = self._copies(ins, outs, sems)
        n = len(self.arrays)
        for w in range(n):
            rcopy(w, 0, q, 1 - c, me).wait_recv()
            for j, (cx, cy) in enumerate(chips):
                rcopy(w, 4 + j, 2 * cx + cy, 1 - c, me).wait_recv()
        for w in range(n):
            for cp in first(w):
                cp.wait_send()
            for j, (cx, cy) in enumerate(chips):
                rcopy(w, 4 + j, 2 * cx + cy, c, sib).wait_send()
            mine(w).wait()


class _GatherWeights(_Exchange):
    def __init__(self, items):
        self.layers = [l for _, l in items]
        self.kh = [s.shape[1] // 2 for s, _ in items]
        super().__init__([s for s, _ in items], [_sds((NQ, 1) + s.shape[1:], BF16) for s, _ in items])

    def maps(self, ins, outs):
        c = lax.axis_index("c")
        src = lambda w: ins[w].at[pl.ds(self.layers[w], 1), pl.ds(c * self.kh[w], self.kh[w]), :]
        return src, lambda w, q: src(w), lambda w, q, cc: outs[w].at[q, :, pl.ds(cc * self.kh[w], self.kh[w]), :]


class _ScatterPartials(_Exchange):
    def __init__(self, parts):
        super().__init__(parts, [_sds((NQ, 1, 2) + p.shape[2:], BF16) for p in parts])

    def maps(self, ins, outs):
        q = 2 * lax.axis_index("x") + lax.axis_index("y")
        return (lambda w: ins[w].at[:, q]), (lambda w, qq: ins[w].at[:, qq]), (lambda w, qq, cc: outs[w].at[qq, :, cc])


class _Gather8(_Exchange):
    def __init__(self, v):
        super().__init__([v], [_sds((8,) + v.shape)])

    def maps(self, ins, outs):
        return (lambda w: ins[0]), (lambda w, q: ins[0]), (lambda w, q, cc: outs[0].at[2 * q + cc])


class _SwapHalves:
    def __init__(self, dws):
        self.arrays = list(dws)
        self.kh = [d.shape[2] // 2 for d in dws]
        self.out_shapes = [_sds(d.shape[:2] + (kh,) + d.shape[3:]) for d, kh in zip(dws, self.kh)]
        self.sems = [pltpu.SemaphoreType.DMA((len(dws),)), pltpu.SemaphoreType.DMA((len(dws),))]

    def _copies(self, ins, outs, sems):
        send, recv = sems
        _, _, c, _, sib, _ = _place()
        return [pltpu.make_async_remote_copy(
            src_ref=ins[w].at[:, :, pl.ds((1 - c) * self.kh[w], self.kh[w]), :], dst_ref=outs[w],
            send_sem=send.at[w], recv_sem=recv.at[w], device_id=sib, device_id_type=MESH)
            for w in range(len(self.arrays))]

    def start(self, ins, outs, sems):
        for cp in self._copies(ins, outs, sems):
            cp.start()

    def forward_steps(self, n_steps):
        return {}

    def complete(self, ins, outs, sems):
        for cp in self._copies(ins, outs, sems):
            cp.wait()


def _comm_only(name, host):
    n_in, n_out = len(host.arrays), len(host.out_shapes)

    def body(*refs):
        ins, outs, sems = refs[:n_in], refs[n_in:n_in + n_out], refs[n_in + n_out:]
        host.start(ins, outs, sems)
        for ws in host.forward_steps(1).values():
            host.forward(ws, ins, outs, sems)
        host.complete(ins, outs, sems)

    any_spec = pl.BlockSpec(memory_space=pl.ANY)
    return pl.pallas_call(body, name=name, in_specs=[any_spec] * n_in, out_specs=[any_spec] * n_out,
                          out_shape=host.out_shapes, scratch_shapes=host.sems)(*host.arrays)


def _add_halves(dw, got, cidx):
    nl, _, k, n = dw.shape
    kh = k // 2
    qb = 2

    def body(c_ref, a_ref, b_ref, o_ref):
        o_ref[...] = (a_ref[...] + b_ref[...]).astype(BF16)

    grid_spec = pltpu.PrefetchScalarGridSpec(
        num_scalar_prefetch=1, grid=(nl, NQ // qb),
        in_specs=[pl.BlockSpec((None, qb, None, kh, n), lambda l, q, c_ref: (l, q, c_ref[0], 0, 0)),
                  pl.BlockSpec((None, qb, kh, n), lambda l, q, c_ref: (l, q, 0, 0))],
        out_specs=pl.BlockSpec((None, qb, kh, n), lambda l, q, c_ref: (l, q, 0, 0)))
    return pl.pallas_call(
        body, name="add_halves", grid_spec=grid_spec, out_shape=_sds((nl, NQ, kh, n), BF16),
        compiler_params=pltpu.CompilerParams(dimension_semantics=("arbitrary", "arbitrary"),
                                             vmem_limit_bytes=VMEM_LIMIT))(cidx, dw.reshape(nl, NQ, 2, kh, n), got)


def _gather8(name, v):
    return _comm_only(name, _Gather8(v))[0]


PACK = 16 * 128


def _pack(arrays):
    parts = []
    for a in arrays:
        flat = a.reshape(-1)
        parts.append(jnp.pad(flat, (0, (-flat.shape[0]) % PACK)))
    return jnp.concatenate(parts).reshape(-1, 128)


def _unpack(packed, shapes):
    flat = packed.reshape(-1)
    out, off = [], 0
    for shp in shapes:
        size = 1
        for d in shp:
            size *= d
        out.append(flat[off:off + size].reshape(shp))
        off += size + (-size) % PACK
    return out


def kernel(x, p, a_w_pw1, a_b_pw1, a_w_dw, a_b_dw, a_ln_g, a_ln_b, a_w_pw2, b_w_in, b_b_in, b_ln_g, b_ln_b, b_w_s, b_b_s, b_w_out, c_w_in, c_w_conv, c_w_out, ln1_g, ln1_b, ln2_g, ln2_b, ffn_w_gate, ffn_w_up, ffn_w_down, ple_w_gate, ple_w_proj, ple_norm_g, loss_target, m_a_w_pw1, m_a_b_pw1, m_a_w_dw, m_a_b_dw, m_a_ln_g, m_a_ln_b, m_a_w_pw2, m_b_w_in, m_b_b_in, m_b_ln_g, m_b_ln_b, m_b_w_s, m_b_b_s, m_b_w_out, m_c_w_in, m_c_w_conv, m_c_w_out, m_ln1_g, m_ln1_b, m_ln2_g, m_ln2_b, m_ffn_w_gate, m_ffn_w_up, m_ffn_w_down, m_ple_w_gate, m_ple_w_proj, m_ple_norm_g, v_a_w_pw1, v_a_b_pw1, v_a_w_dw, v_a_b_dw, v_a_ln_g, v_a_ln_b, v_a_w_pw2, v_b_w_in, v_b_b_in, v_b_ln_g, v_b_ln_b, v_b_w_s, v_b_b_s, v_b_w_out, v_c_w_in, v_c_w_conv, v_c_w_out, v_ln1_g, v_ln1_b, v_ln2_g, v_ln2_b, v_ffn_w_gate, v_ffn_w_up, v_ffn_w_down, v_ple_w_gate, v_ple_w_proj, v_ple_norm_g):
    args = dict(locals())
    wts = {k: args[k] for k in WEIGHTS}
    mom = {k: args["m_" + k] for k in WEIGHTS}
    var = {k: args["v_" + k] for k in WEIGHTS}
    for k in TRANSPOSED:
        wts[k], mom[k], var[k] = (jnp.transpose(t[k], (0, 2, 1)) for t in (wts, mom, var))
    q_idx = 2 * lax.axis_index("x") + lax.axis_index("y")
    c_idx = lax.axis_index("c").astype(jnp.int32).reshape(1)

    wb = {k: _cast_bf16(wts[k]) for k in BIG if k not in ("ffn_w_gate", "ffn_w_up")}
    mixw = [[("a_w_pw1", 0), ("a_w_pw2", 0)], [("b_w_in", 0), ("b_w_out", 0)], [("c_w_in", 0), ("c_w_out", 0)],
            [("a_w_pw1", 1), ("a_w_pw2", 1)]]
    ffnw = [[("ffn_w_gate", l), ("ffn_w_up", l), ("ffn_w_down", l)] for l in range(DEPTH)]
    plew = [[("ple_w_gate", l), ("ple_w_proj", l)] for l in range(DEPTH)]
    fwd_plan = {("a1", 0): mixw[0][1:] + plew[0], ("a2", 0): ffnw[0], ("ffn", 0): mixw[1] + plew[1],
                ("b", 1): ffnw[1], ("ffn", 1): mixw[2] + plew[2] + ffnw[2][:1],
                ("c1", 2): ffnw[2][1:2], ("c2", 2): ffnw[2][2:], ("ffn", 2): mixw[3] + plew[3] + ffnw[3][:1],
                ("a2", 3): ffnw[3][1:]}
    gw = {}

    def gather(keys):
        return _GatherWeights([(wb[name], l) for name, l in keys])

    def hosted(tag, fn, *fargs):
        keys = fwd_plan.get(tag)
        if not keys:
            return fn(*fargs)
        own, (got,) = fn(*fargs, hosts=[gather(keys)])
        store(keys, got)
        return own

    def store(keys, got):
        for (name, l), arr in zip(keys, got):
            gw[name, l] = arr.reshape(NQ * arr.shape[2], arr.shape[3]) if name in ROW_SHARDED else arr

    first_keys = mixw[0][:1]
    wb["ffn_w_gate"], (got,) = _cast_bf16(wts["ffn_w_gate"], hosts=[gather(first_keys)])
    store(first_keys, got)
    shard_shapes = [wts[k].shape for k in SMALL_SHARDED]
    wb["ffn_w_up"], ((small8,),) = _cast_bf16(wts["ffn_w_up"], hosts=[_Gather8(_pack([wts[k] for k in SMALL_SHARDED]))])
    per_chip = [_unpack(small8[2 * qq], shard_shapes) for qq in range(NQ)]
    full = {k: jnp.concatenate([per_chip[qq][i] for qq in range(NQ)], axis=-1) for i, k in enumerate(SMALL_SHARDED)}
    for k in SMALL_REPL:
        full[k] = wts[k]

    def vec(name, l):
        return full[name][l][None, :]

    def conv_w(name, l, rows):
        w = full[name][l]
        return jnp.pad(w, ((0, rows - w.shape[0]), (0, 0)))

    ws = full["b_w_s"][0]
    wst = jnp.transpose(ws, (0, 2, 1))
    bsx = jnp.broadcast_to(full["b_b_s"][0][:, :, None], (SGU_H, SGU_T, SGU_G))

    x0s, z1s, z2s, saved, ffn_saved = [], [], [], [], []
    cur = x[0]
    for i in range(DEPTH):
        mix, j = i % 3, i // 3
        x0s.append(cur)
        if mix == 0:
            h, glu = hosted(("a1", i), _fwd_a1, cur, gw["a_w_pw1", j], vec("a_b_pw1", j), i)
            z1, cv = hosted(("a2", i), _fwd_a2, glu, cur, conv_w("a_w_dw", j, 32), vec("a_b_dw", j), vec("a_ln_g", j),
                            vec("a_ln_b", j), gw["a_w_pw2", j], i)
            saved.append((h, glu, cv))
        elif mix == 1:
            z1, zg, gg = hosted(("b", i), _fwd_b, cur, gw["b_w_in", 0], vec("b_b_in", 0), vec("b_ln_g", 0),
                                vec("b_ln_b", 0), ws, bsx, gw["b_w_out", 0])
            saved.append((zg, gg))
        else:
            hc = hosted(("c1", i), _fwd_c1, cur, gw["c_w_in", 0])
            z1 = hosted(("c2", i), _fwd_c2, hc, cur, conv_w("c_w_conv", 0, 8), gw["c_w_out", 0])
            saved.append((hc,))
        z2, ab, ub, hm = hosted(("ffn", i), _fwd_ffn, z1, vec("ln1_g", i), vec("ln1_b", i), gw["ffn_w_gate", i],
                                gw["ffn_w_up", i], gw["ffn_w_down", i], i)
        ffn_saved.append((ab, ub, hm))
        cur = hosted(("ple", i), _fwd_ple, z2, p[i, 0], vec("ln2_g", i), vec("ln2_b", i), gw["ple_w_gate", i],
                     gw["ple_w_proj", i], vec("ple_norm_g", i), i)
        z1s.append(z1)
        z2s.append(z2)

    g, loss_acc = _loss_head(cur, loss_target[0])
    loss = lax.psum(0.5 / D * jnp.sum(loss_acc[0]), ("x", "y", "c"))

    dws = {}
    sg = {}
    res = {k: None for k in BIG}

    def wgrad(name, l, a, amode, b, bmode, scatter_keys=()):
        _, k, n = wts[name].shape
        hosts = [_ScatterPartials([parts[key] for key in scatter_keys])] if scatter_keys else ()
        if name in ROW_SHARDED:
            out = _mm_tn(f"dw_{name}_{l}", a, "1", b, "1", NQ * k, n, groups=1, hosts=hosts)
        else:
            out = _mm_tn(f"dw_{name}_{l}", a, amode, b, bmode, k, n, hosts=hosts)
        if scatter_keys:
            out, (contribs,) = out
            update(scatter_keys, contribs)
        dws[name, l] = out.reshape(1, NQ, k, n)

    def swap(keys):
        return _SwapHalves([dws[k] for k in keys])

    parts = {}

    def add_halves(keys, got):
        parts.update((k, _add_halves(dws[k], r, c_idx)) for k, r in zip(keys, got))

    def update(keys, contribs):
        for (name, l), gc in zip(keys, contribs):
            _, kq, n = wts[name].shape
            res[name] = _adam(f"adam_{name}_{l}", wts[name], mom[name], var[name], gc.reshape(NQ, 1, kq, n), l,
                              res[name])

    small = SMALL_SHARDED + SMALL_REPL
    late_small = [("a_b_pw1", 0)]
    early_small = [(k, l) for k in small for l in range(full[k].shape[0]) if (k, l) not in late_small]
    pending = None
    for i in reversed(range(DEPTH)):
        mix, j = i % 3, i // 3
        ple_args = (g, z2s[i], p[i, 0], vec("ln2_g", i), vec("ln2_b", i), gw["ple_w_gate", i], gw["ple_w_proj", i],
                    vec("ple_norm_g", i), i)
        if pending:
            (dz2, x2b, dgp, dqp, acc), (got,) = _bwd_ple(*ple_args, hosts=[swap(pending)])
            add_halves(pending, got)
        else:
            dz2, x2b, dgp, dqp, acc = _bwd_ple(*ple_args)
        sg["ple_norm_g", i], sg["ln2_g", i], sg["ln2_b", i] = acc[0], acc[1], acc[2]
        wgrad("ple_w_gate", i, x2b, "c", dgp, "1")
        wgrad("ple_w_proj", i, p[i, 0], "1", dqp, "c")
        ab, ub, hm = ffn_saved[i]
        ffn_args = (dz2, z1s[i], ab, ub, vec("ln1_g", i), vec("ln1_b", i), gw["ffn_w_gate", i], gw["ffn_w_up", i],
                    gw["ffn_w_down", i], i)
        if pending:
            (dz1, x1b, da, du, acc), (contribs,) = _bwd_ffn(
                *ffn_args, hosts=[_ScatterPartials([parts[key] for key in ffnw[i + 1]])])
            update(ffnw[i + 1], contribs)
        else:
            dz1, x1b, da, du, acc = _bwd_ffn(*ffn_args)
        sg["ln1_g", i], sg["ln1_b", i] = acc[0], acc[1]
        wgrad("ffn_w_gate", i, da, "1", x1b, "1", scatter_keys=mixw[i + 1][1:] + plew[i + 1] if pending else ())
        wgrad("ffn_w_up", i, du, "1", x1b, "1")
        wgrad("ffn_w_down", i, hm, "1", dz2, "1", scatter_keys=mixw[i + 1][:1] if pending else ())
        x0 = x0s[i]
        if mix == 0:
            h, glu, cv = saved[i]
            a2_args = (dz1, cv, vec("a_ln_g", j), vec("a_ln_b", j), gw["a_w_pw2", j], i)
            conv_args = (glu, conv_w("a_w_dw", j, 32), i)
            if i == 0:
                early = ffnw[0] + plew[0]
                (dcv, sb, acc), (got,) = _bwd_a2(*a2_args, hosts=[swap(early)])
                sg["a_ln_g", j], sg["a_ln_b", j], sg["a_b_dw", j] = acc[0], acc[1], acc[2]
                add_halves(early, got)
                wgrad("a_w_pw2", j, sb, "c", dz1, "1")
                (dglu, dwdw), (contribs, got) = _bwd_conv_a(
                    dcv, *conv_args, hosts=[_ScatterPartials([parts[key] for key in early]), swap(mixw[0][1:])])
                update(early, contribs)
                add_halves(mixw[0][1:], got)
                sg["a_w_dw", j] = dwdw[:CONV_A]
                (g, dh, acc), (contribs, (g8_early,)) = _bwd_a1(
                    dglu, h, dz1, gw["a_w_pw1", j], i,
                    hosts=[_ScatterPartials([parts[key] for key in mixw[0][1:]]),
                           _Gather8(_pack([sg[pc] for pc in early_small]))])
                update(mixw[0][1:], contribs)
            else:
                dcv, sb, acc = _bwd_a2(*a2_args)
                sg["a_ln_g", j], sg["a_ln_b", j], sg["a_b_dw", j] = acc[0], acc[1], acc[2]
                dglu, dwdw = _bwd_conv_a(dcv, *conv_args)
                wgrad("a_w_pw2", j, sb, "c", dz1, "1")
                sg["a_w_dw", j] = dwdw[:CONV_A]
                g, dh, acc = _bwd_a1(dglu, h, dz1, gw["a_w_pw1", j], i)
            sg["a_b_pw1", j] = acc[0]
            wgrad("a_w_pw1", j, x0, "1", dh, "c")
        elif mix == 1:
            zg, gg = saved[i]
            g, dh, mb, acc, dw_s, db_s = _bwd_b(dz1, zg, gg, vec("b_ln_g", 0), vec("b_ln_b", 0), gw["b_w_in", 0],
                                                gw["b_w_out", 0], ws, wst, bsx)
            sg["b_b_in", 0], sg["b_ln_g", 0], sg["b_ln_b", 0] = acc[0], acc[1, :E], acc[2, :E]
            sg["b_w_s", 0], sg["b_b_s", 0] = dw_s, jnp.sum(db_s, axis=-1)
            wgrad("b_w_out", 0, mb, "c", dz1, "1")
            wgrad("b_w_in", 0, x0, "1", dh, "c")
        else:
            (hc,) = saved[i]
            wc = conv_w("c_w_conv", 0, 8)
            dy, dbg, mb = _bwd_c2(dz1, hc, wc, gw["c_w_out", 0])
            wgrad("c_w_out", 0, mb, "c", dz1, "1")
            g, dhc, dwc = _bwd_c1(dy, hc, dbg, dz1, wc, gw["c_w_in", 0])
            sg["c_w_conv", 0] = dwc[:CONV_C]
            wgrad("c_w_in", 0, x0, "1", dhc, "c")
        pending = mixw[i] + ffnw[i] + plew[i] if i > 0 else mixw[0][:1]
    grad_x = g[None]
    add_halves(pending, _comm_only("swap_last", swap(pending)))
    update(pending, _comm_only("scatter_last", _ScatterPartials([parts[key] for key in pending])))

    g8_late = _gather8("gather_small_late", _pack([sg[pc] for pc in late_small]))
    sums = dict(zip(early_small, _unpack(_sum8("sum8_early", g8_early), [sg[pc].shape for pc in early_small])))
    sums.update(zip(late_small, _unpack(_sum8("sum8_late", g8_late), [sg[pc].shape for pc in late_small])))
    gsum = [jnp.stack([sums[k, l] for l in range(full[k].shape[0])]) for k in small]
    gmine = []
    for k, gs in zip(small, gsum):
        if k in SMALL_SHARDED:
            wdt = wts[k].shape[-1]
            gs = lax.dynamic_slice_in_dim(gs, q_idx * wdt, wdt, axis=gs.ndim - 1)
        gmine.append(gs)
    packed = [_pack(t)[None] for t in ([wts[k] for k in small], [mom[k] for k in small], [var[k] for k in small])]
    outs = _adam("adam_small", packed[0], packed[1], packed[2], _pack(gmine)[None, None], 0, None)
    unpacked = [_unpack(o[0], [wts[k].shape for k in small]) for o in outs]
    for i, k in enumerate(small):
        res[k] = tuple(u[i] for u in unpacked)

    for k in TRANSPOSED:
        res[k] = tuple(jnp.transpose(r, (0, 2, 1)) for r in res[k])
    return (loss, grad_x, *[res[k][0] for k in WEIGHTS], *[res[k][1] for k in WEIGHTS],
            *[res[k][2] for k in WEIGHTS], *[res[k][3] for k in WEIGHTS])
```

```python
import functools

import jax
import jax.numpy as jnp
from jax import lax
from jax.experimental import pallas as pl
from jax.experimental.pallas import tpu as pltpu

F32, BF16 = jnp.float32, jnp.bfloat16
S = 4096
D = 1024
E = 2048
FF = 2816
FQ = FF // 4
NQ = 4
DEPTH = 4
ALPHA = (2 * DEPTH) ** 0.25
LN_EPS = 1e-5
CONV_A, CONV_C = 31, 3
HALO_A, HALO_C = 32, 8
SGU_T, SGU_H, SGU_G, SGU_CHUNK = 128, 8, 256, 64
VMEM_LIMIT = 56 * 1024 * 1024
DW_BLOCK_BUDGET = 40 * 1024 * 1024
MESH = pl.DeviceIdType.MESH
ADAM_LR, ADAM_B1, ADAM_B2, ADAM_EPS, ADAM_WD, ADAM_STEP = 0.001, 0.9, 0.999, 1e-08, 0.01, 10
GELU_C, GELU_A = 0.7978845608028654, 0.044715

BIG = ["a_w_pw1", "a_w_pw2", "b_w_in", "b_w_out", "c_w_in", "c_w_out",
       "ffn_w_gate", "ffn_w_up", "ffn_w_down", "ple_w_gate", "ple_w_proj"]
TRANSPOSED = ["ffn_w_gate", "ffn_w_up"]
ROW_SHARDED = ["a_w_pw2", "b_w_out", "c_w_out", "ffn_w_gate", "ffn_w_up", "ffn_w_down", "ple_w_gate"]
SMALL_SHARDED = ["a_b_pw1", "a_w_dw", "a_b_dw", "a_ln_g", "a_ln_b", "c_w_conv"]
SMALL_REPL = ["b_b_in", "b_ln_g", "b_ln_b", "b_w_s", "b_b_s", "ln1_g", "ln1_b", "ln2_g", "ln2_b", "ple_norm_g"]
WEIGHTS = ["a_w_pw1", "a_b_pw1", "a_w_dw", "a_b_dw", "a_ln_g", "a_ln_b", "a_w_pw2", "b_w_in", "b_b_in", "b_ln_g",
           "b_ln_b", "b_w_s", "b_b_s", "b_w_out", "c_w_in", "c_w_conv", "c_w_out", "ln1_g", "ln1_b", "ln2_g",
           "ln2_b", "ffn_w_gate", "ffn_w_up", "ffn_w_down", "ple_w_gate", "ple_w_proj", "ple_norm_g"]


def _call(name, body, grid, in_specs, out_specs, out_shape, scratch=(), aliases=None, hosts=()):
    params = pltpu.CompilerParams(dimension_semantics=("arbitrary",) * len(grid), vmem_limit_bytes=VMEM_LIMIT)
    if not hosts:
        return pl.pallas_call(
            body, name=name, grid=grid, in_specs=in_specs, out_specs=out_specs, out_shape=out_shape,
            scratch_shapes=list(scratch), input_output_aliases=aliases or {}, compiler_params=params)
    assert len(grid) == 1 and not aliases
    single = not isinstance(out_shape, (list, tuple))
    own_shapes = [out_shape] if single else list(out_shape)
    own_specs = [out_specs] if single else list(out_specs)
    n_in, n_out, n_scr = len(in_specs), len(own_shapes), len(scratch)
    h_in = [len(h.arrays) for h in hosts]
    h_out = [len(h.out_shapes) for h in hosts]
    h_sem = [len(h.sems) for h in hosts]

    def split(refs, counts):
        out, off = [], 0
        for cnt in counts:
            out.append(refs[off:off + cnt])
            off += cnt
        return out

    def wrapped(*refs):
        ins, hin, outs, hout, scr, hsem = split(refs, [n_in, sum(h_in), n_out, sum(h_out), n_scr, sum(h_sem)])
        per_host = list(zip(hosts, split(hin, h_in), split(hout, h_out), split(hsem, h_sem)))

        @pl.when(pl.program_id(0) == 0)
        def _():
            for h, a, o, s in per_host:
                h.start(a, o, s)

        body(*ins, *outs, *scr)

        for h, a, o, s in per_host:
            for step, ws in sorted(h.forward_steps(grid[0]).items()):
                pl.when(pl.program_id(0) == step)(functools.partial(h.forward, ws, a, o, s))

        @pl.when(pl.program_id(0) == grid[0] - 1)
        def _():
            for h, a, o, s in per_host:
                h.complete(a, o, s)

    any_spec = pl.BlockSpec(memory_space=pl.ANY)
    call = pl.pallas_call(
        wrapped, name=name, grid=grid, in_specs=list(in_specs) + [any_spec] * sum(h_in),
        out_specs=own_specs + [any_spec] * sum(h_out),
        out_shape=own_shapes + [s for h in hosts for s in h.out_shapes],
        scratch_shapes=list(scratch) + [s for h in hosts for s in h.sems], compiler_params=params)

    def run(*args):
        res = call(*args, *[a for h in hosts for a in h.arrays])
        own = res[0] if single else list(res[:n_out])
        return own, split(list(res[n_out:]), h_out)

    return run


def _sds(shape, dtype=F32):
    return jax.ShapeDtypeStruct(shape, dtype)


def _row(tm, c):
    return pl.BlockSpec((tm, c), lambda i: (i, 0))


def _grow(g, tm, c):
    return pl.BlockSpec((g, tm, c), lambda i: (0, i, 0))


def _const(shape):
    nd = len(shape)
    return pl.BlockSpec(shape, lambda i: (0,) * nd, pipeline_mode=pl.Buffered(1))


def _wspec(w):
    return pl.BlockSpec((NQ, None, w.shape[2], w.shape[3]), lambda i: (0, 0, 0, 0), pipeline_mode=pl.Buffered(1))


def _prev(tm, hb, c):
    return pl.BlockSpec((hb, c), lambda i: (jnp.maximum(i * (tm // hb) - 1, 0), 0))


def _next(tm, hb, c):
    return pl.BlockSpec((hb, c), lambda i: (jnp.minimum((i + 1) * (tm // hb), S // hb - 1), 0))


def _acc(r, c):
    return pl.BlockSpec((r, c), lambda i: (0, 0))


def _sig(x):
    return 1.0 / (1.0 + jnp.exp(-x))


def _ln(z, g, b):
    mu = jnp.mean(z, axis=-1, keepdims=True)
    zc = z - mu
    rstd = lax.rsqrt(jnp.mean(zc * zc, axis=-1, keepdims=True) + LN_EPS)
    xhat = zc * rstd
    return xhat * g + b, xhat, rstd


def _ln_bwd(dyg, xhat, rstd):
    return rstd * (dyg - jnp.mean(dyg, axis=-1, keepdims=True) - xhat * jnp.mean(dyg * xhat, axis=-1, keepdims=True))


def _mm(a, w):
    return jnp.dot(a.astype(BF16), w, preferred_element_type=F32)


def _mmt(a, w):
    return lax.dot_general(a.astype(BF16), w, (((1,), (1,)), ((), ())), preferred_element_type=F32)


def _colsum(x):
    return jnp.sum(x, axis=0, keepdims=True)


def _gelu_and_grad(x):
    x2 = x * x
    t = jnp.tanh(x * (GELU_C + (GELU_C * GELU_A) * x2))
    hx = 0.5 * x
    return hx + hx * t, 0.5 + 0.5 * t + hx * (1.0 - t * t) * (GELU_C + (3.0 * GELU_C * GELU_A) * x2)


def _silu_grad(a, sg):
    return sg * (1.0 + a * (1.0 - sg))


def _sgu_masks():
    r = lax.broadcasted_iota(jnp.int32, (SGU_T, SGU_T), 0) // SGU_CHUNK
    c = lax.broadcasted_iota(jnp.int32, (SGU_T, SGU_T), 1) // SGU_CHUNK
    return r >= c, c >= r


def _fill_halo(buf, lo, n, halo_val_fn, is_edge):
    @pl.when(is_edge)
    def _():
        buf[lo:lo + n, :] = jnp.zeros((n, buf.shape[1]), F32)

    @pl.when(jnp.logical_not(is_edge))
    def _():
        buf[lo:lo + n, :] = halo_val_fn()


SUB, LANE = 8, 128
ROWS_AT_ONCE = 16


def _shift_copies(buf, sh):
    rows = sh.shape[1]
    for s in range(1, SUB):
        sh[s - 1, :, :] = buf[pl.ds(s, rows), :]


def _tiles(buf, sh, s, first, count, group0, lanes):
    src = buf if s == 0 else sh.at[s - 1]
    return {t: src[pl.ds(pl.multiple_of((group0 + t) * SUB, SUB), SUB), lanes] for t in range(first, first + count)}


def _by_shift(offsets):
    out = []
    for s in range(SUB):
        taps = [(k, o // SUB) for k, o in enumerate(offsets) if o % SUB == s]
        if taps:
            out.append((s, taps))
    return out


def _conv_rows(out_ref, w_ref, bias_ref, offsets, buf, sh, tm):
    n = ROWS_AT_ONCE
    for cb in range(D // LANE):
        lanes = slice(cb * LANE, (cb + 1) * LANE)
        bias = None if bias_ref is None else jnp.broadcast_to(bias_ref[:, lanes], (SUB, LANE))

        def body(jb, carry):
            accs = [bias] * n
            for s, taps in _by_shift(offsets):
                ms = [m for _, m in taps]
                tiles = _tiles(buf, sh, s, min(ms), max(ms) - min(ms) + n, jb * n, lanes)
                for k, m in taps:
                    wk = jnp.broadcast_to(w_ref[k:k + 1, lanes], (SUB, LANE))
                    for jj in range(n):
                        t = wk * tiles[m + jj]
                        accs[jj] = t if accs[jj] is None else accs[jj] + t
            for jj in range(n):
                out_ref[pl.ds(pl.multiple_of((jb * n + jj) * SUB, SUB), SUB), lanes] = accs[jj]
            return carry

        lax.fori_loop(0, tm // (SUB * n), body, 0)


def _conv_wgrad(dw_ref, d_ref, offsets, buf, sh, tm):
    n = 4
    for cb in range(D // LANE):
        lanes = slice(cb * LANE, (cb + 1) * LANE)

        def body(jq, accs):
            accs = list(accs)
            d = [d_ref[pl.ds(pl.multiple_of((jq * n + jj) * SUB, SUB), SUB), lanes] for jj in range(n)]
            for s, taps in _by_shift(offsets):
                ms = [m for _, m in taps]
                tiles = _tiles(buf, sh, s, min(ms), max(ms) - min(ms) + n, jq * n, lanes)
                for k, m in taps:
                    for jj in range(n):
                        accs[k] = accs[k] + d[jj] * tiles[m + jj]
            return tuple(accs)

        accs = lax.fori_loop(0, tm // (SUB * n), body, tuple(jnp.zeros((SUB, LANE), F32) for _ in offsets))
        for k, acc in enumerate(accs):
            dw_ref[k:k + 1, lanes] += jnp.sum(acc, axis=0, keepdims=True)


def _fwd_a1(x0, w1, b1, l, hosts=()):
    tm = 512

    def body(x_ref, w_ref, b_ref, h_ref, glu_ref):
        xb = x_ref[...].astype(BF16)
        for q in range(NQ):
            sl = slice(q * 512, (q + 1) * 512)
            h_ref[:, sl] = jnp.dot(xb, w_ref[q], preferred_element_type=F32) + b_ref[:, sl]
        glu_ref[...] = h_ref[:, :D] * _sig(h_ref[:, D:])

    return _call(f"fwd_a1_{l}", body, (S // tm,), [_row(tm, D), _wspec(w1), _const((1, 2 * D))],
                 [_row(tm, 2 * D), _row(tm, D)], [_sds((S, 2 * D)), _sds((S, D))], hosts=hosts)(x0, w1, b1)


def _fwd_a2(glu, x0, wdw, bdw, lg, lb, w2, l, hosts=()):
    tm = 256

    def body(g_ref, gp_ref, x_ref, wdw_ref, bdw_ref, lg_ref, lb_ref, w2_ref, z_ref, cv_ref, buf, sh):
        i = pl.program_id(0)
        _fill_halo(buf, 0, HALO_A, lambda: gp_ref[...], i == 0)
        buf[HALO_A:HALO_A + tm, :] = g_ref[...]
        _shift_copies(buf, sh)
        _conv_rows(cv_ref, wdw_ref, bdw_ref, [HALO_A - (CONV_A - 1) + k for k in range(CONV_A)], buf, sh, tm)
        n, _, _ = _ln(cv_ref[...], lg_ref[...], lb_ref[...])
        sb = (n * _sig(n)).astype(BF16)
        z_ref[...] = ALPHA * x_ref[...] + jnp.dot(sb, w2_ref[...], preferred_element_type=F32)

    return _call(f"fwd_a2_{l}", body, (S // tm,),
                 [_row(tm, D), _prev(tm, HALO_A, D), _row(tm, D), _const((32, D)), _const((1, D)), _const((1, D)),
                  _const((1, D)), _const(w2.shape)],
                 [_row(tm, D), _row(tm, D)], [_sds((S, D)), _sds((S, D))],
                 scratch=[pltpu.VMEM((HALO_A + tm, D), F32), pltpu.VMEM((SUB - 1, HALO_A + tm - SUB, D), F32)],
                 hosts=hosts)(glu, glu, x0, wdw, bdw, lg, lb, w2)


def _fwd_b(x0, win, b_in, lg, lb, ws, bsx, wout, hosts=()):
    tm = 256

    def body(x_ref, win_ref, bin_ref, lg_ref, lb_ref, ws_ref, bsx_ref, wout_ref, z_ref, zg_ref, gg_ref, f_scr, h_ref):
        xb = x_ref[...].astype(BF16)
        for q in range(NQ):
            sl = slice(q * 1024, (q + 1) * 1024)
            h_ref[:, sl] = jnp.dot(xb, win_ref[q], preferred_element_type=F32) + bin_ref[:, sl]
        u, du = _gelu_and_grad(h_ref[:, :E])
        v, dv = _gelu_and_grad(h_ref[:, E:])
        zg_ref[:, 0:E] = u.astype(BF16)
        zg_ref[:, E:2 * E] = v.astype(BF16)
        gg_ref[:, 0:E] = du.astype(BF16)
        gg_ref[:, E:2 * E] = dv.astype(BF16)
        vn, _, _ = _ln(v, lg_ref[...], lb_ref[...])
        vnb = vn.astype(BF16)
        mask, _ = _sgu_masks()
        for hd in range(SGU_H):
            wm = jnp.where(mask, ws_ref[hd], 0.0).astype(BF16)
            cs = slice(hd * SGU_G, (hd + 1) * SGU_G)
            for n in range(tm // SGU_T):
                rs = slice(n * SGU_T, (n + 1) * SGU_T)
                f_scr[rs, cs] = jnp.dot(wm, vnb[rs, cs], preferred_element_type=F32) + bsx_ref[hd]
        mb = (u * f_scr[...]).astype(BF16)
        z_ref[...] = ALPHA * x_ref[...] + jnp.dot(mb, wout_ref[...], preferred_element_type=F32)

    return _call("fwd_b", body, (S // tm,),
                 [_row(tm, D), _wspec(win), _const((1, 2 * E)), _const((1, E)), _const((1, E)),
                  _const((SGU_H, SGU_T, SGU_T)), _const((SGU_H, SGU_T, SGU_G)), _const(wout.shape)],
                 [_row(tm, D), _row(tm, 2 * E), _row(tm, 2 * E)],
                 [_sds((S, D)), _sds((S, 2 * E), BF16), _sds((S, 2 * E), BF16)],
                 scratch=[pltpu.VMEM((tm, E), F32), pltpu.VMEM((tm, 2 * E), F32)], hosts=hosts
                 )(x0, win, b_in, lg, lb, ws, bsx, wout)


def _fwd_c1(x0, win, hosts=()):
    tm = 512

    def body(x_ref, w_ref, hc_ref):
        xb = x_ref[...].astype(BF16)
        for q in range(NQ):
            hc_ref[:, q * 768:(q + 1) * 768] = jnp.dot(xb, w_ref[q], preferred_element_type=F32)

    return _call("fwd_c1", body, (S // tm,), [_row(tm, D), _wspec(win)], _row(tm, 3 * D),
                 _sds((S, 3 * D)), hosts=hosts)(x0, win)


def _short_conv(buf, hc_ref, hcp_ref, wc_ref, tm, i):
    _fill_halo(buf, 0, HALO_C, lambda: hcp_ref[:, D:2 * D] * hcp_ref[:, 2 * D:], i == 0)
    buf[HALO_C:HALO_C + tm, :] = hc_ref[:, D:2 * D] * hc_ref[:, 2 * D:]
    y = wc_ref[0:1, :] * buf[pl.ds(HALO_C - 2, tm), :]
    for k in range(1, CONV_C):
        y = y + wc_ref[k:k + 1, :] * buf[pl.ds(HALO_C - 2 + k, tm), :]
    return y


def _fwd_c2(hc, x0, wc, wout, hosts=()):
    tm = 256

    def body(hc_ref, hcp_ref, x_ref, wc_ref, wout_ref, z_ref, buf):
        y = _short_conv(buf, hc_ref, hcp_ref, wc_ref, tm, pl.program_id(0))
        mb = (hc_ref[:, :D] * y).astype(BF16)
        z_ref[...] = ALPHA * x_ref[...] + jnp.dot(mb, wout_ref[...], preferred_element_type=F32)

    return _call("fwd_c2", body, (S // tm,),
                 [_row(tm, 3 * D), _prev(tm, HALO_C, 3 * D), _row(tm, D), _const((8, D)), _const(wout.shape)],
                 _row(tm, D), _sds((S, D)), scratch=[pltpu.VMEM((HALO_C + tm, D), F32)], hosts=hosts
                 )(hc, hc, x0, wc, wout)


def _fwd_ffn(z1, lg, lb, wgt, wut, wd, l, hosts=()):
    tm = 256

    def body(z_ref, lg_ref, lb_ref, wg_ref, wu_ref, wd_ref, o_ref, a_ref, u_ref, hm_ref):
        x1, _, _ = _ln(z_ref[...], lg_ref[...], lb_ref[...])
        xb = x1.astype(BF16)
        a = _mmt(xb, wg_ref[...])
        u = _mmt(xb, wu_ref[...])
        hmb = (a * _sig(a) * u).astype(BF16)
        a_ref[...] = a.astype(BF16)
        u_ref[...] = u.astype(BF16)
        hm_ref[...] = hmb
        o_ref[...] = ALPHA * x1 + jnp.dot(hmb, wd_ref[...], preferred_element_type=F32)

    return _call(f"fwd_ffn_{l}", body, (S // tm,),
                 [_row(tm, D), _const((1, D)), _const((1, D)), _const((FF, D)), _const((FF, D)), _const((FF, D))],
                 [_row(tm, D), _row(tm, FF), _row(tm, FF), _row(tm, FF)],
                 [_sds((S, D)), _sds((S, FF), BF16), _sds((S, FF), BF16), _sds((S, FF), BF16)],
                 hosts=hosts)(z1, lg, lb, wgt, wut, wd)


def _ple_parts(z2, p, lg, lb, wg_ref, wp_ref, pg):
    x2, xhat, rstd = _ln(z2, lg, lb)
    xb = x2.astype(BF16)
    gate = _sig(jnp.dot(xb, wg_ref[...], preferred_element_type=F32))
    pb = p.astype(BF16)
    qp = jnp.concatenate([jnp.dot(pb, wp_ref[q], preferred_element_type=F32) for q in range(NQ)], axis=1)
    rs = lax.rsqrt(jnp.mean(qp * qp, axis=-1, keepdims=True) + LN_EPS)
    qn = qp * rs
    return x2, xhat, rstd, xb, gate, qn, rs, qn * pg


def _fwd_ple(z2, p, lg, lb, wg, wp, pg, l, hosts=()):
    tm = 512

    def body(z_ref, p_ref, lg_ref, lb_ref, wg_ref, wp_ref, pg_ref, o_ref):
        x2, _, _, _, gate, _, _, r = _ple_parts(z_ref[...], p_ref[...], lg_ref[...], lb_ref[...], wg_ref, wp_ref,
                                                pg_ref[...])
        o_ref[...] = x2 + gate * r

    return _call(f"fwd_ple_{l}", body, (S // tm,),
                 [_row(tm, D), _row(tm, 256), _const((1, D)), _const((1, D)), _const(wg.shape), _wspec(wp),
                  _const((1, D))],
                 _row(tm, D), _sds((S, D)), hosts=hosts)(z2, p, lg, lb, wg, wp, pg)


def _loss_head(y, target):
    tm = 512

    def body(y_ref, t_ref, dy_ref, acc_ref):
        @pl.when(pl.program_id(0) == 0)
        def _():
            acc_ref[...] = jnp.zeros_like(acc_ref)

        e = y_ref[...] - t_ref[...]
        dy_ref[...] = e * (1.0 / D)
        acc_ref[0:1, :] += _colsum(e * e)

    return _call("loss_head", body, (S // tm,), [_row(tm, D), _row(tm, D)], [_row(tm, D), _acc(8, D)],
                 [_sds((S, D)), _sds((8, D))])(y, target)


def _zero_first(*refs):
    @pl.when(pl.program_id(0) == 0)
    def _():
        for r in refs:
            r[...] = jnp.zeros_like(r)


def _bwd_ple(g, z2, p, lg, lb, wg, wp, pg, l, hosts=()):
    tm = 256

    def body(g_ref, z_ref, p_ref, lg_ref, lb_ref, wg_ref, wp_ref, pg_ref, dz_ref, xb_ref, dgp_ref, dqp_ref, acc_ref):
        _zero_first(acc_ref)
        gin = g_ref[...]
        lgv, pgv = lg_ref[...], pg_ref[...]
        _, xhat, rstd, xb, gate, qn, rs, r = _ple_parts(z_ref[...], p_ref[...], lgv, lb_ref[...], wg_ref, wp_ref, pgv)
        xb_ref[...] = xb
        dgpb = (gin * r * gate * (1.0 - gate)).astype(BF16)
        dgp_ref[...] = dgpb
        dx2 = gin + _mmt(dgpb, wg_ref[...])
        dr = gin * gate
        acc_ref[0:1, :] += _colsum(dr * qn)
        t = dr * pgv
        dqp_ref[...] = (rs * (t - qn * jnp.mean(t * qn, axis=-1, keepdims=True))).astype(BF16)
        acc_ref[1:2, :] += _colsum(dx2 * xhat)
        acc_ref[2:3, :] += _colsum(dx2)
        dz_ref[...] = _ln_bwd(dx2 * lgv, xhat, rstd)

    return _call(f"bwd_ple_{l}", body, (S // tm,),
                 [_row(tm, D), _row(tm, D), _row(tm, 256), _const((1, D)), _const((1, D)), _const(wg.shape),
                  _wspec(wp), _const((1, D))],
                 [_row(tm, D), _row(tm, D), _row(tm, D), _row(tm, D), _acc(8, D)],
                 [_sds((S, D)), _sds((S, D), BF16), _sds((S, D), BF16), _sds((S, D), BF16), _sds((8, D))],
                 hosts=hosts)(g, z2, p, lg, lb, wg, wp, pg)


def _bwd_ffn(dz2, z1, ab, ub, lg, lb, wgt, wut, wd, l, hosts=()):
    tm = 256

    def body(dz2_ref, z_ref, a_ref, u_ref, lg_ref, lb_ref, wg_ref, wu_ref, wd_ref, dz1_ref, xb_ref, da_ref, du_ref,
             acc_ref):
        _zero_first(acc_ref)
        dz2v = dz2_ref[...]
        lgv = lg_ref[...]
        x1, xhat, rstd = _ln(z_ref[...], lgv, lb_ref[...])
        xb_ref[...] = x1.astype(BF16)
        a = a_ref[...].astype(F32)
        u = u_ref[...].astype(F32)
        sg = _sig(a)
        dhm = _mmt(dz2v, wd_ref[...])
        dub = (dhm * (a * sg)).astype(BF16)
        dab = (dhm * u * _silu_grad(a, sg)).astype(BF16)
        da_ref[...] = dab
        du_ref[...] = dub
        dx1 = ALPHA * dz2v + _mm(dab, wg_ref[...]) + _mm(dub, wu_ref[...])
        acc_ref[0:1, :] += _colsum(dx1 * xhat)
        acc_ref[1:2, :] += _colsum(dx1)
        dz1_ref[...] = _ln_bwd(dx1 * lgv, xhat, rstd)

    return _call(f"bwd_ffn_{l}", body, (S // tm,),
                 [_row(tm, D), _row(tm, D), _row(tm, FF), _row(tm, FF), _const((1, D)), _const((1, D)),
                  _const((FF, D)), _const((FF, D)), _const((FF, D))],
                 [_row(tm, D), _row(tm, D), _row(tm, FF), _row(tm, FF), _acc(8, D)],
                 [_sds((S, D)), _sds((S, D), BF16), _sds((S, FF), BF16), _sds((S, FF), BF16), _sds((8, D))],
                 hosts=hosts)(dz2, z1, ab, ub, lg, lb, wgt, wut, wd)


def _bwd_a2(dz1, cv, lg, lb, w2, l, hosts=()):
    tm = 512

    def body(dz_ref, cv_ref, lg_ref, lb_ref, w2_ref, dcv_ref, sb_ref, acc_ref):
        _zero_first(acc_ref)
        lgv = lg_ref[...]
        n, xhat, rstd = _ln(cv_ref[...], lgv, lb_ref[...])
        sg = _sig(n)
        sb_ref[...] = (n * sg).astype(BF16)
        dzb = dz_ref[...].astype(BF16)
        ds = _mmt(dzb, w2_ref[...])
        dn = ds * _silu_grad(n, sg)
        acc_ref[0:1, :] += _colsum(dn * xhat)
        acc_ref[1:2, :] += _colsum(dn)
        dcv = _ln_bwd(dn * lgv, xhat, rstd)
        acc_ref[2:3, :] += _colsum(dcv)
        dcv_ref[...] = dcv

    return _call(f"bwd_a2_{l}", body, (S // tm,),
                 [_row(tm, D), _row(tm, D), _const((1, D)), _const((1, D)), _const(w2.shape)],
                 [_row(tm, D), _row(tm, D), _acc(8, D)],
                 [_sds((S, D)), _sds((S, D), BF16), _sds((8, D))], hosts=hosts)(dz1, cv, lg, lb, w2)


def _bwd_conv_a(dcv, glu, wdw, l, hosts=()):
    tm = 256
    nb = S // tm

    def body(d_ref, dn_ref, g_ref, gp_ref, w_ref, dglu_ref, dw_ref, bufd, bufx, sh):
        i = pl.program_id(0)
        _zero_first(dw_ref)
        bufd[0:tm, :] = d_ref[...]
        _fill_halo(bufd, tm, HALO_A, lambda: dn_ref[...], i == nb - 1)
        _fill_halo(bufx, 0, HALO_A, lambda: gp_ref[...], i == 0)
        bufx[HALO_A:HALO_A + tm, :] = g_ref[...]
        _shift_copies(bufd, sh)
        _conv_rows(dglu_ref, w_ref, None, [CONV_A - 1 - k for k in range(CONV_A)], bufd, sh, tm)
        _shift_copies(bufx, sh)
        _conv_wgrad(dw_ref, d_ref, [HALO_A - (CONV_A - 1) + k for k in range(CONV_A)], bufx, sh, tm)

    return _call(f"bwd_conv_a_{l}", body, (nb,),
                 [_row(tm, D), _next(tm, HALO_A, D), _row(tm, D), _prev(tm, HALO_A, D), _const((32, D))],
                 [_row(tm, D), _acc(32, D)], [_sds((S, D)), _sds((32, D))],
                 scratch=[pltpu.VMEM((tm + HALO_A, D), F32), pltpu.VMEM((HALO_A + tm, D), F32),
                          pltpu.VMEM((SUB - 1, HALO_A + tm - SUB, D), F32)], hosts=hosts)(dcv, dcv, glu, glu, wdw)


def _bwd_a1(dglu, h, dz1, w1, l, hosts=()):
    tm = 256

    def body(dg_ref, h_ref, dz_ref, w_ref, dx_ref, dh_ref, acc_ref):
        _zero_first(acc_ref)
        a, g = h_ref[:, :D], h_ref[:, D:]
        sg = _sig(g)
        dgl = dg_ref[...]
        da = dgl * sg
        dg = dgl * a * sg * (1.0 - sg)
        acc_ref[0:1, 0:D] += _colsum(da)
        acc_ref[0:1, D:2 * D] += _colsum(dg)
        dh_ref[:, 0:D] = da.astype(BF16)
        dh_ref[:, D:2 * D] = dg.astype(BF16)
        dx = ALPHA * dz_ref[...]
        for q in range(NQ):
            dx = dx + _mmt(dh_ref[:, q * 512:(q + 1) * 512], w_ref[q])
        dx_ref[...] = dx

    return _call(f"bwd_a1_{l}", body, (S // tm,),
                 [_row(tm, D), _row(tm, 2 * D), _row(tm, D), _wspec(w1)],
                 [_row(tm, D), _row(tm, 2 * D), _acc(8, 2 * D)],
                 [_sds((S, D)), _sds((S, 2 * D), BF16), _sds((8, 2 * D))], hosts=hosts)(dglu, h, dz1, w1)


def _bwd_c2(dz1, hc, wc, wout):
    tm = 256

    def body(dz_ref, hc_ref, hcp_ref, wc_ref, wout_ref, dy_ref, dbg_ref, mb_ref, buf):
        y = _short_conv(buf, hc_ref, hcp_ref, wc_ref, tm, pl.program_id(0))
        dzb = dz_ref[...].astype(BF16)
        dm = _mmt(dzb, wout_ref[...])
        bg = hc_ref[:, :D]
        mb_ref[...] = (bg * y).astype(BF16)
        dbg_ref[...] = (dm * y).astype(BF16)
        dy_ref[...] = dm * bg

    return _call("bwd_c2", body, (S // tm,),
                 [_row(tm, D), _row(tm, 3 * D), _prev(tm, HALO_C, 3 * D), _const((8, D)), _const(wout.shape)],
                 [_row(tm, D), _row(tm, D), _row(tm, D)],
                 [_sds((S, D)), _sds((S, D), BF16), _sds((S, D), BF16)],
                 scratch=[pltpu.VMEM((HALO_C + tm, D), F32)])(dz1, hc, hc, wc, wout)


def _bwd_c1(dy, hc, dbg, dz1, wc, win):
    tm = 256
    nb = S // tm

    def body(d_ref, dn_ref, hc_ref, hcp_ref, dbg_ref, dz_ref, wc_ref, win_ref, dx_ref, dhc_ref, dwc_ref, bufd, bufq):
        i = pl.program_id(0)
        _zero_first(dwc_ref)
        bufd[0:tm, :] = d_ref[...]
        _fill_halo(bufd, tm, HALO_C, lambda: dn_ref[...], i == nb - 1)
        _fill_halo(bufq, 0, HALO_C, lambda: hcp_ref[:, D:2 * D] * hcp_ref[:, 2 * D:], i == 0)
        bufq[HALO_C:HALO_C + tm, :] = hc_ref[:, D:2 * D] * hc_ref[:, 2 * D:]
        dq = wc_ref[0:1, :] * bufd[pl.ds(CONV_C - 1, tm), :]
        for k in range(1, CONV_C):
            dq = dq + wc_ref[k:k + 1, :] * bufd[pl.ds(CONV_C - 1 - k, tm), :]
        dv = d_ref[...]
        for k in range(CONV_C):
            dwc_ref[k:k + 1, :] += _colsum(dv * bufq[pl.ds(HALO_C - (CONV_C - 1) + k, tm), :])
        dhc_ref[:, 0:D] = dbg_ref[...]
        dhc_ref[:, D:2 * D] = (dq * hc_ref[:, 2 * D:]).astype(BF16)
        dhc_ref[:, 2 * D:3 * D] = (dq * hc_ref[:, D:2 * D]).astype(BF16)
        dx = ALPHA * dz_ref[...]
        for q in range(NQ):
            dx = dx + _mmt(dhc_ref[:, q * 768:(q + 1) * 768], win_ref[q])
        dx_ref[...] = dx

    return _call("bwd_c1", body, (nb,),
                 [_row(tm, D), _next(tm, HALO_C, D), _row(tm, 3 * D), _prev(tm, HALO_C, 3 * D), _row(tm, D),
                  _row(tm, D), _const((8, D)), _wspec(win)],
                 [_row(tm, D), _row(tm, 3 * D), _acc(8, D)],
                 [_sds((S, D)), _sds((S, 3 * D), BF16), _sds((8, D))],
                 scratch=[pltpu.VMEM((tm + HALO_C, D), F32), pltpu.VMEM((HALO_C + tm, D), F32)]
                 )(dy, dy, hc, hc, dbg, dz1, wc, win)


def _bwd_b(dz1, zg, gg, lg, lb, win, wout, ws, wst, bsx, hosts=()):
    tm = 128
    nb = S // tm

    def body(dz_ref, zg_ref, gg_ref, lg_ref, lb_ref, win_ref, wout_ref, ws_ref, wst_ref, bsx_ref,
             dx_ref, dh_ref, mb_ref, acc_ref, dws_ref, dbs_ref, f_scr, dvn_scr):
        _zero_first(acc_ref, dws_ref, dbs_ref)
        lgv = lg_ref[...]
        u = zg_ref[:, :E].astype(F32)
        v = zg_ref[:, E:].astype(F32)
        vn, xhat, rstd = _ln(v, lgv, lb_ref[...])
        vnb = vn.astype(BF16)
        dzb = dz_ref[...].astype(BF16)
        dm = _mmt(dzb, wout_ref[...])
        mask, mask_t = _sgu_masks()
        for hd in range(SGU_H):
            wm = jnp.where(mask, ws_ref[hd], 0.0).astype(BF16)
            cs = slice(hd * SGU_G, (hd + 1) * SGU_G)
            for n in range(tm // SGU_T):
                rs = slice(n * SGU_T, (n + 1) * SGU_T)
                f_scr[rs, cs] = jnp.dot(wm, vnb[rs, cs], preferred_element_type=F32) + bsx_ref[hd]
        f = f_scr[...]
        mb_ref[...] = (u * f).astype(BF16)
        du = dm * f
        df = dm * u
        dfb = df.astype(BF16)
        for hd in range(SGU_H):
            wmt = jnp.where(mask_t, wst_ref[hd], 0.0).astype(BF16)
            cs = slice(hd * SGU_G, (hd + 1) * SGU_G)
            for n in range(tm // SGU_T):
                rs = slice(n * SGU_T, (n + 1) * SGU_T)
                dvn_scr[rs, cs] = jnp.dot(wmt, dfb[rs, cs], preferred_element_type=F32)
                dws_ref[hd] += lax.dot_general(dfb[rs, cs], vnb[rs, cs], (((1,), (1,)), ((), ())),
                                               preferred_element_type=F32)
                dbs_ref[hd] += df[rs, cs]
        dvn = dvn_scr[...]
        acc_ref[1:2, 0:E] += _colsum(dvn * xhat)
        acc_ref[2:3, 0:E] += _colsum(dvn)
        dv = _ln_bwd(dvn * lgv, xhat, rstd)
        dhu = du * gg_ref[:, :E].astype(F32)
        dhv = dv * gg_ref[:, E:].astype(F32)
        acc_ref[0:1, 0:E] += _colsum(dhu)
        acc_ref[0:1, E:2 * E] += _colsum(dhv)
        dh_ref[:, 0:E] = dhu.astype(BF16)
        dh_ref[:, E:2 * E] = dhv.astype(BF16)
        dx = ALPHA * dz_ref[...]
        for q in range(NQ):
            dx = dx + _mmt(dh_ref[:, q * 1024:(q + 1) * 1024], win_ref[q])
        dx_ref[...] = dx

        @pl.when(pl.program_id(0) == nb - 1)
        def _():
            for hd in range(SGU_H):
                dws_ref[hd] = jnp.where(mask, dws_ref[hd], 0.0)

    c3 = lambda a, b, c: pl.BlockSpec((a, b, c), lambda i: (0, 0, 0))
    return _call("bwd_b", body, (nb,),
                 [_row(tm, D), _row(tm, 2 * E), _row(tm, 2 * E), _const((1, E)), _const((1, E)), _wspec(win),
                  _const(wout.shape), _const((SGU_H, SGU_T, SGU_T)), _const((SGU_H, SGU_T, SGU_T)),
                  _const((SGU_H, SGU_T, SGU_G))],
                 [_row(tm, D), _row(tm, 2 * E), _row(tm, E), _acc(8, 2 * E), c3(SGU_H, SGU_T, SGU_T),
                  c3(SGU_H, SGU_T, SGU_G)],
                 [_sds((S, D)), _sds((S, 2 * E), BF16), _sds((S, E), BF16), _sds((8, 2 * E)),
                  _sds((SGU_H, SGU_T, SGU_T)), _sds((SGU_H, SGU_T, SGU_G))],
                 scratch=[pltpu.VMEM((tm, E), F32), pltpu.VMEM((tm, E), F32)], hosts=hosts
                 )(dz1, zg, gg, lg, lb, win, wout, ws, wst, bsx)


def _mm_tn(name, a, amode, b, bmode, k, n, groups=NQ, hosts=()):
    def block_bytes(ts):
        ka = k if amode == "1" else groups * k
        nb = n if bmode == "1" else groups * n
        return 2 * (ts * ka * a.dtype.itemsize + ts * nb * b.dtype.itemsize + groups * k * n * 4)

    ts = min(1024 if block_bytes(1024) <= DW_BLOCK_BUDGET else 512, S)

    def spec(mode, w):
        if mode == "1":
            return pl.BlockSpec((ts, w), lambda s: (s, 0))
        if mode == "c":
            return pl.BlockSpec((ts, groups * w), lambda s: (s, 0))
        return pl.BlockSpec((groups, ts, w), lambda s: (0, s, 0))

    def pick(ref, mode, w, g):
        if mode == "1":
            return ref[...]
        if mode == "c":
            return ref[:, g * w:(g + 1) * w]
        return ref[g]

    def body(a_ref, b_ref, o_ref):
        _zero_first(o_ref)
        a_t = jnp.transpose(a_ref[...].astype(BF16)) if amode == "1" else None
        b_1 = b_ref[...].astype(BF16) if bmode == "1" else None
        for g in range(groups):
            lhs = a_t if amode == "1" else jnp.transpose(pick(a_ref, amode, k, g).astype(BF16))
            rhs = b_1 if bmode == "1" else pick(b_ref, bmode, n, g).astype(BF16)
            o_ref[0, g] += jnp.dot(lhs, rhs, preferred_element_type=F32)

    return _call(name, body, (S // ts,), [spec(amode, k), spec(bmode, n)],
                 pl.BlockSpec((1, groups, k, n), lambda s: (0, 0, 0, 0)), _sds((1, groups, k, n)), hosts=hosts)(a, b)


def _row_block(k, cap=256):
    return max(t for t in range(16, min(k, cap) + 1, 16) if k % t == 0)


def _cast_bf16(w, hosts=()):
    nl, k, n = w.shape
    tb = _row_block(k, 512)
    nb = k // tb

    def body(w_ref, o_ref):
        o_ref[...] = w_ref[...].astype(BF16)

    spec = pl.BlockSpec((None, tb, n), lambda i: (i // nb, i % nb, 0))
    return _call("cast_bf16", body, (nl * nb,), [spec], spec, _sds(w.shape, BF16), hosts=hosts)(w)


def _adam(name, w, m, v, gc, l, prev):
    nl, k, n = w.shape
    nc = gc.shape[0]
    tb = _row_block(k, 512)

    def body(w_ref, m_ref, v_ref, g_ref, *rest):
        go_ref, d_ref, mo_ref, vo_ref = rest[-4:]
        g = g_ref[0].astype(F32)
        for c in range(1, nc):
            g = g + g_ref[c].astype(F32)
        m2 = ADAM_B1 * m_ref[...] + (1.0 - ADAM_B1) * g
        v2 = ADAM_B2 * v_ref[...] + (1.0 - ADAM_B2) * (g * g)
        m_hat = m2 / (1.0 - ADAM_B1 ** ADAM_STEP)
        v_hat = v2 / (1.0 - ADAM_B2 ** ADAM_STEP)
        go_ref[...] = g
        d_ref[...] = -ADAM_LR * (m_hat / (jnp.sqrt(v_hat) + ADAM_EPS) + ADAM_WD * w_ref[...])
        mo_ref[...] = m2
        vo_ref[...] = v2

    spec = pl.BlockSpec((None, tb, n), lambda i: (l, i, 0))
    gspec = pl.BlockSpec((nc, None, tb, n), lambda i: (0, 0, i, 0))
    in_specs, args, aliases = [spec, spec, spec, gspec], [w, m, v, gc], {}
    if prev is not None:
        in_specs += [pl.BlockSpec(memory_space=pl.ANY)] * 4
        args += list(prev)
        aliases = {4 + j: j for j in range(4)}
    return _call(name, body, (k // tb,), in_specs, [spec] * 4, [_sds(w.shape)] * 4, aliases=aliases)(*args)


def _sum8(name, g8):
    r = g8.shape[1]

    def body(g_ref, o_ref):
        acc = g_ref[0]
        for d in range(1, 8):
            acc = acc + g_ref[d]
        o_ref[...] = acc

    return _call(name, body, (1,), [pl.BlockSpec((8, r, 128), lambda i: (0, 0, 0))],
                 pl.BlockSpec((r, 128), lambda i: (0, 0)), _sds((r, 128)))(g8)


def _place():
    x, y, c = lax.axis_index("x"), lax.axis_index("y"), lax.axis_index("c")
    return x, y, c, 2 * x + y, (x, y, 1 - c), [(1 - x, y), (x, 1 - y), (1 - x, 1 - y)]


class _Exchange:
    def __init__(self, arrays, out_shapes):
        self.arrays, self.out_shapes = list(arrays), list(out_shapes)
        n = len(self.arrays)
        self.sems = [pltpu.SemaphoreType.DMA((7 * n,)), pltpu.SemaphoreType.DMA((7 * n,)),
                     pltpu.SemaphoreType.DMA((n,))]

    def _copies(self, ins, outs, sems):
        send, recv, lsem = sems
        local_src, remote_src, dst = self.maps(ins, outs)
        x, y, c, q, sib, chips = _place()

        def rcopy(w, k, qq, cc, to, src=None):
            return pltpu.make_async_remote_copy(
                src_ref=dst(w, qq, cc) if src is None else src, dst_ref=dst(w, qq, cc),
                send_sem=send.at[7 * w + k], recv_sem=recv.at[7 * w + k], device_id=to, device_id_type=MESH)

        def mine(w):
            return pltpu.make_async_copy(local_src(w), dst(w, q, c), lsem.at[w])

        def first(w):
            return [rcopy(w, 0, q, c, sib, local_src(w))] + [
                rcopy(w, 1 + j, q, c, (cx, cy, c), remote_src(w, 2 * cx + cy)) for j, (cx, cy) in enumerate(chips)]

        return rcopy, mine, first, (x, y, c), q, c, sib, chips

    def start(self, ins, outs, sems):
        _, mine, first, *_ = self._copies(ins, outs, sems)
        for w in range(len(self.arrays)):
            mine(w).start()
            for cp in first(w):
                cp.start()

    def forward_steps(self, n_steps):
        sizes = [a.size // a.shape[0] for a in self.arrays]
        plan, moved = {}, 0
        for w, size in enumerate(sizes):
            moved += size
            plan.setdefault(min(n_steps - 1, -(-moved * n_steps // sum(sizes))), []).append(w)
        return plan

    def forward(self, ws, ins, outs, sems):
        rcopy, _, _, me, _, c, sib, chips = self._copies(ins, outs, sems)
        for w in ws:
            for j, (cx, cy) in enumerate(chips):
                rcopy(w, 1 + j, 2 * cx + cy, c, me).wait_recv()
                rcopy(w, 4 + j, 2 * cx + cy, c, sib).start()

    def complete(self, ins, outs, sems):
        rcopy, mine, first, me, q, c, sib, chips = self._copies(ins, outs, sems)
        n = len(self.arrays)
        for w in range(n):
            rcopy(w, 0, q, 1 - c, me).wait_recv()
            for j, (cx, cy) in enumerate(chips):
                rcopy(w, 4 + j, 2 * cx + cy, 1 - c, me).wait_recv()
        for w in range(n):
            for cp in first(w):
                cp.wait_send()
            for j, (cx, cy) in enumerate(chips):
                rcopy(w, 4 + j, 2 * cx + cy, c, sib).wait_send()
            mine(w).wait()


class _GatherWeights(_Exchange):
    def __init__(self, items):
        self.layers = [l for _, l in items]
        self.kh = [s.shape[1] // 2 for s, _ in items]
        super().__init__([s for s, _ in items], [_sds((NQ, 1) + s.shape[1:], BF16) for s, _ in items])

    def maps(self, ins, outs):
        c = lax.axis_index("c")
        src = lambda w: ins[w].at[pl.ds(self.layers[w], 1), pl.ds(c * self.kh[w], self.kh[w]), :]
        return src, lambda w, q: src(w), lambda w, q, cc: outs[w].at[q, :, pl.ds(cc * self.kh[w], self.kh[w]), :]


class _ScatterPartials(_Exchange):
    def __init__(self, parts):
        super().__init__(parts, [_sds((NQ, 1, 2) + p.shape[2:], BF16) for p in parts])

    def maps(self, ins, outs):
        q = 2 * lax.axis_index("x") + lax.axis_index("y")
        return (lambda w: ins[w].at[:, q]), (lambda w, qq: ins[w].at[:, qq]), (lambda w, qq, cc: outs[w].at[qq, :, cc])


class _Gather8(_Exchange):
    def __init__(self, v):
        super().__init__([v], [_sds((8,) + v.shape)])

    def maps(self, ins, outs):
        return (lambda w: ins[0]), (lambda w, q: ins[0]), (lambda w, q, cc: outs[0].at[2 * q + cc])


class _SwapHalves:
    def __init__(self, dws):
        self.arrays = list(dws)
        self.kh = [d.shape[2] // 2 for d in dws]
        self.out_shapes = [_sds(d.shape[:2] + (kh,) + d.shape[3:]) for d, kh in zip(dws, self.kh)]
        self.sems = [pltpu.SemaphoreType.DMA((len(dws),)), pltpu.SemaphoreType.DMA((len(dws),))]

    def _copies(self, ins, outs, sems):
        send, recv = sems
        _, _, c, _, sib, _ = _place()
        return [pltpu.make_async_remote_copy(
            src_ref=ins[w].at[:, :, pl.ds((1 - c) * self.kh[w], self.kh[w]), :], dst_ref=outs[w],
            send_sem=send.at[w], recv_sem=recv.at[w], device_id=sib, device_id_type=MESH)
            for w in range(len(self.arrays))]

    def start(self, ins, outs, sems):
        for cp in self._copies(ins, outs, sems):
            cp.start()

    def forward_steps(self, n_steps):
        return {}

    def complete(self, ins, outs, sems):
        for cp in self._copies(ins, outs, sems):
            cp.wait()


def _comm_only(name, host):
    n_in, n_out = len(host.arrays), len(host.out_shapes)

    def body(*refs):
        ins, outs, sems = refs[:n_in], refs[n_in:n_in + n_out], refs[n_in + n_out:]
        host.start(ins, outs, sems)
        for ws in host.forward_steps(1).values():
            host.forward(ws, ins, outs, sems)
        host.complete(ins, outs, sems)

    any_spec = pl.BlockSpec(memory_space=pl.ANY)
    return pl.pallas_call(body, name=name, in_specs=[any_spec] * n_in, out_specs=[any_spec] * n_out,
                          out_shape=host.out_shapes, scratch_shapes=host.sems)(*host.arrays)


def _add_halves(dw, got, cidx):
    nl, _, k, n = dw.shape
    kh = k // 2
    qb = 2

    def body(c_ref, a_ref, b_ref, o_ref):
        o_ref[...] = (a_ref[...] + b_ref[...]).astype(BF16)

    grid_spec = pltpu.PrefetchScalarGridSpec(
        num_scalar_prefetch=1, grid=(nl, NQ // qb),
        in_specs=[pl.BlockSpec((None, qb, None, kh, n), lambda l, q, c_ref: (l, q, c_ref[0], 0, 0)),
                  pl.BlockSpec((None, qb, kh, n), lambda l, q, c_ref: (l, q, 0, 0))],
        out_specs=pl.BlockSpec((None, qb, kh, n), lambda l, q, c_ref: (l, q, 0, 0)))
    return pl.pallas_call(
        body, name="add_halves", grid_spec=grid_spec, out_shape=_sds((nl, NQ, kh, n), BF16),
        compiler_params=pltpu.CompilerParams(dimension_semantics=("arbitrary", "arbitrary"),
                                             vmem_limit_bytes=VMEM_LIMIT))(cidx, dw.reshape(nl, NQ, 2, kh, n), got)


def _gather8(name, v):
    return _comm_only(name, _Gather8(v))[0]


PACK = 16 * 128


def _pack(arrays):
    parts = []
    for a in arrays:
        flat = a.reshape(-1)
        parts.append(jnp.pad(flat, (0, (-flat.shape[0]) % PACK)))
    return jnp.concatenate(parts).reshape(-1, 128)


def _unpack(packed, shapes):
    flat = packed.reshape(-1)
    out, off = [], 0
    for shp in shapes:
        size = 1
        for d in shp:
            size *= d
        out.append(flat[off:off + size].reshape(shp))
        off += size + (-size) % PACK
    return out


def kernel(x, p, a_w_pw1, a_b_pw1, a_w_dw, a_b_dw, a_ln_g, a_ln_b, a_w_pw2, b_w_in, b_b_in, b_ln_g, b_ln_b, b_w_s, b_b_s, b_w_out, c_w_in, c_w_conv, c_w_out, ln1_g, ln1_b, ln2_g, ln2_b, ffn_w_gate, ffn_w_up, ffn_w_down, ple_w_gate, ple_w_proj, ple_norm_g, loss_target, m_a_w_pw1, m_a_b_pw1, m_a_w_dw, m_a_b_dw, m_a_ln_g, m_a_ln_b, m_a_w_pw2, m_b_w_in, m_b_b_in, m_b_ln_g, m_b_ln_b, m_b_w_s, m_b_b_s, m_b_w_out, m_c_w_in, m_c_w_conv, m_c_w_out, m_ln1_g, m_ln1_b, m_ln2_g, m_ln2_b, m_ffn_w_gate, m_ffn_w_up, m_ffn_w_down, m_ple_w_gate, m_ple_w_proj, m_ple_norm_g, v_a_w_pw1, v_a_b_pw1, v_a_w_dw, v_a_b_dw, v_a_ln_g, v_a_ln_b, v_a_w_pw2, v_b_w_in, v_b_b_in, v_b_ln_g, v_b_ln_b, v_b_w_s, v_b_b_s, v_b_w_out, v_c_w_in, v_c_w_conv, v_c_w_out, v_ln1_g, v_ln1_b, v_ln2_g, v_ln2_b, v_ffn_w_gate, v_ffn_w_up, v_ffn_w_down, v_ple_w_gate, v_ple_w_proj, v_ple_norm_g):
    args = dict(locals())
    wts = {k: args[k] for k in WEIGHTS}
    mom = {k: args["m_" + k] for k in WEIGHTS}
    var = {k: args["v_" + k] for k in WEIGHTS}
    for k in TRANSPOSED:
        wts[k], mom[k], var[k] = (jnp.transpose(t[k], (0, 2, 1)) for t in (wts, mom, var))
    q_idx = 2 * lax.axis_index("x") + lax.axis_index("y")
    c_idx = lax.axis_index("c").astype(jnp.int32).reshape(1)

    wb = {k: _cast_bf16(wts[k]) for k in BIG if k not in ("ffn_w_gate", "ffn_w_up")}
    mixw = [[("a_w_pw1", 0), ("a_w_pw2", 0)], [("b_w_in", 0), ("b_w_out", 0)], [("c_w_in", 0), ("c_w_out", 0)],
            [("a_w_pw1", 1), ("a_w_pw2", 1)]]
    ffnw = [[("ffn_w_gate", l), ("ffn_w_up", l), ("ffn_w_down", l)] for l in range(DEPTH)]
    plew = [[("ple_w_gate", l), ("ple_w_proj", l)] for l in range(DEPTH)]
    fwd_plan = {("a1", 0): mixw[0][1:] + plew[0], ("a2", 0): ffnw[0], ("ffn", 0): mixw[1] + plew[1],
                ("b", 1): ffnw[1], ("ffn", 1): mixw[2] + plew[2] + ffnw[2][:1],
                ("c1", 2): ffnw[2][1:2], ("c2", 2): ffnw[2][2:], ("ffn", 2): mixw[3] + plew[3] + ffnw[3][:1],
                ("a2", 3): ffnw[3][1:]}
    gw = {}

    def gather(keys):
        return _GatherWeights([(wb[name], l) for name, l in keys])

    def hosted(tag, fn, *fargs):
        keys = fwd_plan.get(tag)
        if not keys:
            return fn(*fargs)
        own, (got,) = fn(*fargs, hosts=[gather(keys)])
        store(keys, got)
        return own

    def store(keys, got):
        for (name, l), arr in zip(keys, got):
            gw[name, l] = arr.reshape(NQ * arr.shape[2], arr.shape[3]) if name in ROW_SHARDED else arr

    first_keys = mixw[0][:1]
    wb["ffn_w_gate"], (got,) = _cast_bf16(wts["ffn_w_gate"], hosts=[gather(first_keys)])
    store(first_keys, got)
    shard_shapes = [wts[k].shape for k in SMALL_SHARDED]
    wb["ffn_w_up"], ((small8,),) = _cast_bf16(wts["ffn_w_up"], hosts=[_Gather8(_pack([wts[k] for k in SMALL_SHARDED]))])
    per_chip = [_unpack(small8[2 * qq], shard_shapes) for qq in range(NQ)]
    full = {k: jnp.concatenate([per_chip[qq][i] for qq in range(NQ)], axis=-1) for i, k in enumerate(SMALL_SHARDED)}
    for k in SMALL_REPL:
        full[k] = wts[k]

    def vec(name, l):
        return full[name][l][None, :]

    def conv_w(name, l, rows):
        w = full[name][l]
        return jnp.pad(w, ((0, rows - w.shape[0]), (0, 0)))

    ws = full["b_w_s"][0]
    wst = jnp.transpose(ws, (0, 2, 1))
    bsx = jnp.broadcast_to(full["b_b_s"][0][:, :, None], (SGU_H, SGU_T, SGU_G))

    x0s, z1s, z2s, saved, ffn_saved = [], [], [], [], []
    cur = x[0]
    for i in range(DEPTH):
        mix, j = i % 3, i // 3
        x0s.append(cur)
        if mix == 0:
            h, glu = hosted(("a1", i), _fwd_a1, cur, gw["a_w_pw1", j], vec("a_b_pw1", j), i)
            z1, cv = hosted(("a2", i), _fwd_a2, glu, cur, conv_w("a_w_dw", j, 32), vec("a_b_dw", j), vec("a_ln_g", j),
                            vec("a_ln_b", j), gw["a_w_pw2", j], i)
            saved.append((h, glu, cv))
        elif mix == 1:
            z1, zg, gg = hosted(("b", i), _fwd_b, cur, gw["b_w_in", 0], vec("b_b_in", 0), vec("b_ln_g", 0),
                                vec("b_ln_b", 0), ws, bsx, gw["b_w_out", 0])
            saved.append((zg, gg))
        else:
            hc = hosted(("c1", i), _fwd_c1, cur, gw["c_w_in", 0])
            z1 = hosted(("c2", i), _fwd_c2, hc, cur, conv_w("c_w_conv", 0, 8), gw["c_w_out", 0])
            saved.append((hc,))
        z2, ab, ub, hm = hosted(("ffn", i), _fwd_ffn, z1, vec("ln1_g", i), vec("ln1_b", i), gw["ffn_w_gate", i],
                                gw["ffn_w_up", i], gw["ffn_w_down", i], i)
        ffn_saved.append((ab, ub, hm))
        cur = hosted(("ple", i), _fwd_ple, z2, p[i, 0], vec("ln2_g", i), vec("ln2_b", i), gw["ple_w_gate", i],
                     gw["ple_w_proj", i], vec("ple_norm_g", i), i)
        z1s.append(z1)
        z2s.append(z2)

    g, loss_acc = _loss_head(cur, loss_target[0])
    loss = lax.psum(0.5 / D * jnp.sum(loss_acc[0]), ("x", "y", "c"))

    dws = {}
    sg = {}
    res = {k: None for k in BIG}

    def wgrad(name, l, a, amode, b, bmode, scatter_keys=()):
        _, k, n = wts[name].shape
        hosts = [_ScatterPartials([parts[key] for key in scatter_keys])] if scatter_keys else ()
        if name in ROW_SHARDED:
            out = _mm_tn(f"dw_{name}_{l}", a, "1", b, "1", NQ * k, n, groups=1, hosts=hosts)
        else:
            out = _mm_tn(f"dw_{name}_{l}", a, amode, b, bmode, k, n, hosts=hosts)
        if scatter_keys:
            out, (contribs,) = out
            update(scatter_keys, contribs)
        dws[name, l] = out.reshape(1, NQ, k, n)

    def swap(keys):
        return _SwapHalves([dws[k] for k in keys])

    parts = {}

    def add_halves(keys, got):
        parts.update((k, _add_halves(dws[k], r, c_idx)) for k, r in zip(keys, got))

    def update(keys, contribs):
        for (name, l), gc in zip(keys, contribs):
            _, kq, n = wts[name].shape
            res[name] = _adam(f"adam_{name}_{l}", wts[name], mom[name], var[name], gc.reshape(NQ, 1, kq, n), l,
                              res[name])

    small = SMALL_SHARDED + SMALL_REPL
    late_small = [("a_b_pw1", 0)]
    early_small = [(k, l) for k in small for l in range(full[k].shape[0]) if (k, l) not in late_small]
    pending = None
    for i in reversed(range(DEPTH)):
        mix, j = i % 3, i // 3
        ple_args = (g, z2s[i], p[i, 0], vec("ln2_g", i), vec("ln2_b", i), gw["ple_w_gate", i], gw["ple_w_proj", i],
                    vec("ple_norm_g", i), i)
        if pending:
            (dz2, x2b, dgp, dqp, acc), (got,) = _bwd_ple(*ple_args, hosts=[swap(pending)])
            add_halves(pending, got)
        else:
            dz2, x2b, dgp, dqp, acc = _bwd_ple(*ple_args)
        sg["ple_norm_g", i], sg["ln2_g", i], sg["ln2_b", i] = acc[0], acc[1], acc[2]
        wgrad("ple_w_gate", i, x2b, "c", dgp, "1")
        wgrad("ple_w_proj", i, p[i, 0], "1", dqp, "c")
        ab, ub, hm = ffn_saved[i]
        ffn_args = (dz2, z1s[i], ab, ub, vec("ln1_g", i), vec("ln1_b", i), gw["ffn_w_gate", i], gw["ffn_w_up", i],
                    gw["ffn_w_down", i], i)
        if pending:
            (dz1, x1b, da, du, acc), (contribs,) = _bwd_ffn(
                *ffn_args, hosts=[_ScatterPartials([parts[key] for key in ffnw[i + 1]])])
            update(ffnw[i + 1], contribs)
        else:
            dz1, x1b, da, du, acc = _bwd_ffn(*ffn_args)
        sg["ln1_g", i], sg["ln1_b", i] = acc[0], acc[1]
        behind_mixer = mix == 1
        rest = mixw[i + 1] + plew[i + 1] if pending else []
        wgrad("ffn_w_gate", i, da, "1", x1b, "1", scatter_keys=() if behind_mixer else rest[1:])
        wgrad("ffn_w_up", i, du, "1", x1b, "1")
        wgrad("ffn_w_down", i, hm, "1", dz2, "1", scatter_keys=() if behind_mixer else rest[:1])
        x0 = x0s[i]
        if mix == 0:
            h, glu, cv = saved[i]
            a2_args = (dz1, cv, vec("a_ln_g", j), vec("a_ln_b", j), gw["a_w_pw2", j], i)
            conv_args = (glu, conv_w("a_w_dw", j, 32), i)
            if i == 0:
                early = ffnw[0] + plew[0]
                (dcv, sb, acc), (got,) = _bwd_a2(*a2_args, hosts=[swap(early)])
                sg["a_ln_g", j], sg["a_ln_b", j], sg["a_b_dw", j] = acc[0], acc[1], acc[2]
                add_halves(early, got)
                wgrad("a_w_pw2", j, sb, "c", dz1, "1")
                (dglu, dwdw), (contribs, got) = _bwd_conv_a(
                    dcv, *conv_args, hosts=[_ScatterPartials([parts[key] for key in early]), swap(mixw[0][1:])])
                update(early, contribs)
                add_halves(mixw[0][1:], got)
                sg["a_w_dw", j] = dwdw[:CONV_A]
                (g, dh, acc), (contribs, (g8_early,)) = _bwd_a1(
                    dglu, h, dz1, gw["a_w_pw1", j], i,
                    hosts=[_ScatterPartials([parts[key] for key in mixw[0][1:]]),
                           _Gather8(_pack([sg[pc] for pc in early_small]))])
                update(mixw[0][1:], contribs)
            else:
                dcv, sb, acc = _bwd_a2(*a2_args)
                sg["a_ln_g", j], sg["a_ln_b", j], sg["a_b_dw", j] = acc[0], acc[1], acc[2]
                dglu, dwdw = _bwd_conv_a(dcv, *conv_args)
                wgrad("a_w_pw2", j, sb, "c", dz1, "1")
                sg["a_w_dw", j] = dwdw[:CONV_A]
                g, dh, acc = _bwd_a1(dglu, h, dz1, gw["a_w_pw1", j], i)
            sg["a_b_pw1", j] = acc[0]
            wgrad("a_w_pw1", j, x0, "1", dh, "c")
        elif mix == 1:
            zg, gg = saved[i]
            (g, dh, mb, acc, dw_s, db_s), (contribs,) = _bwd_b(
                dz1, zg, gg, vec("b_ln_g", 0), vec("b_ln_b", 0), gw["b_w_in", 0], gw["b_w_out", 0], ws, wst, bsx,
                hosts=[_ScatterPartials([parts[key] for key in rest])])
            update(rest, contribs)
            sg["b_b_in", 0], sg["b_ln_g", 0], sg["b_ln_b", 0] = acc[0], acc[1, :E], acc[2, :E]
            sg["b_w_s", 0], sg["b_b_s", 0] = dw_s, jnp.sum(db_s, axis=-1)
            wgrad("b_w_out", 0, mb, "c", dz1, "1")
            wgrad("b_w_in", 0, x0, "1", dh, "c")
        else:
            (hc,) = saved[i]
            wc = conv_w("c_w_conv", 0, 8)
            dy, dbg, mb = _bwd_c2(dz1, hc, wc, gw["c_w_out", 0])
            wgrad("c_w_out", 0, mb, "c", dz1, "1")
            g, dhc, dwc = _bwd_c1(dy, hc, dbg, dz1, wc, gw["c_w_in", 0])
            sg["c_w_conv", 0] = dwc[:CONV_C]
            wgrad("c_w_in", 0, x0, "1", dhc, "c")
        pending = mixw[i] + ffnw[i] + plew[i] if i > 0 else mixw[0][:1]
    grad_x = g[None]
    add_halves(pending, _comm_only("swap_last", swap(pending)))
    update(pending, _comm_only("scatter_last", _ScatterPartials([parts[key] for key in pending])))

    g8_late = _gather8("gather_small_late", _pack([sg[pc] for pc in late_small]))
    sums = dict(zip(early_small, _unpack(_sum8("sum8_early", g8_early), [sg[pc].shape for pc in early_small])))
    sums.update(zip(late_small, _unpack(_sum8("sum8_late", g8_late), [sg[pc].shape for pc in late_small])))
    gsum = [jnp.stack([sums[k, l] for l in range(full[k].shape[0])]) for k in small]
    gmine = []
    for k, gs in zip(small, gsum):
        if k in SMALL_SHARDED:
            wdt = wts[k].shape[-1]
            gs = lax.dynamic_slice_in_dim(gs, q_idx * wdt, wdt, axis=gs.ndim - 1)
        gmine.append(gs)
    packed = [_pack(t)[None] for t in ([wts[k] for k in small], [mom[k] for k in small], [var[k] for k in small])]
    outs = _adam("adam_small", packed[0], packed[1], packed[2], _pack(gmine)[None, None], 0, None)
    unpacked = [_unpack(o[0], [wts[k].shape for k in small]) for o in outs]
    for i, k in enumerate(small):
        res[k] = tuple(u[i] for u in unpacked)

    for k in TRANSPOSED:
        res[k] = tuple(jnp.transpose(r, (0, 2, 1)) for r in res[k])
    return (loss, grad_x, *[res[k][0] for k in WEIGHTS], *[res[k][1] for k in WEIGHTS],
            *[res[k][2] for k in WEIGHTS], *[res[k][3] for k in WEIGHTS])
```

```python
import functools

import jax
import jax.numpy as jnp
from jax import lax
from jax.experimental import pallas as pl
from jax.experimental.pallas import tpu as pltpu

F32, BF16 = jnp.float32, jnp.bfloat16
S = 4096
D = 1024
E = 2048
FF = 2816
FQ = FF // 4
NQ = 4
DEPTH = 4
ALPHA = (2 * DEPTH) ** 0.25
LN_EPS = 1e-5
CONV_A, CONV_C = 31, 3
HALO_A, HALO_C = 32, 8
SGU_T, SGU_H, SGU_G, SGU_CHUNK = 128, 8, 256, 64
VMEM_LIMIT = 56 * 1024 * 1024
DW_BLOCK_BUDGET = 40 * 1024 * 1024
MESH = pl.DeviceIdType.MESH
ADAM_LR, ADAM_B1, ADAM_B2, ADAM_EPS, ADAM_WD, ADAM_STEP = 0.001, 0.9, 0.999, 1e-08, 0.01, 10
GELU_C, GELU_A = 0.7978845608028654, 0.044715

BIG = ["a_w_pw1", "a_w_pw2", "b_w_in", "b_w_out", "c_w_in", "c_w_out",
       "ffn_w_gate", "ffn_w_up", "ffn_w_down", "ple_w_gate", "ple_w_proj"]
TRANSPOSED = ["ffn_w_gate", "ffn_w_up"]
ROW_SHARDED = ["a_w_pw2", "b_w_out", "c_w_out", "ffn_w_gate", "ffn_w_up", "ffn_w_down", "ple_w_gate"]
SMALL_SHARDED = ["a_b_pw1", "a_w_dw", "a_b_dw", "a_ln_g", "a_ln_b", "c_w_conv"]
SMALL_REPL = ["b_b_in", "b_ln_g", "b_ln_b", "b_w_s", "b_b_s", "ln1_g", "ln1_b", "ln2_g", "ln2_b", "ple_norm_g"]
WEIGHTS = ["a_w_pw1", "a_b_pw1", "a_w_dw", "a_b_dw", "a_ln_g", "a_ln_b", "a_w_pw2", "b_w_in", "b_b_in", "b_ln_g",
           "b_ln_b", "b_w_s", "b_b_s", "b_w_out", "c_w_in", "c_w_conv", "c_w_out", "ln1_g", "ln1_b", "ln2_g",
           "ln2_b", "ffn_w_gate", "ffn_w_up", "ffn_w_down", "ple_w_gate", "ple_w_proj", "ple_norm_g"]


def _call(name, body, grid, in_specs, out_specs, out_shape, scratch=(), aliases=None, hosts=()):
    params = pltpu.CompilerParams(dimension_semantics=("arbitrary",) * len(grid), vmem_limit_bytes=VMEM_LIMIT)
    if not hosts:
        return pl.pallas_call(
            body, name=name, grid=grid, in_specs=in_specs, out_specs=out_specs, out_shape=out_shape,
            scratch_shapes=list(scratch), input_output_aliases=aliases or {}, compiler_params=params)
    assert len(grid) == 1 and not aliases
    single = not isinstance(out_shape, (list, tuple))
    own_shapes = [out_shape] if single else list(out_shape)
    own_specs = [out_specs] if single else list(out_specs)
    n_in, n_out, n_scr = len(in_specs), len(own_shapes), len(scratch)
    h_in = [len(h.arrays) for h in hosts]
    h_out = [len(h.out_shapes) for h in hosts]
    h_sem = [len(h.sems) for h in hosts]

    def split(refs, counts):
        out, off = [], 0
        for cnt in counts:
            out.append(refs[off:off + cnt])
            off += cnt
        return out

    def wrapped(*refs):
        ins, hin, outs, hout, scr, hsem = split(refs, [n_in, sum(h_in), n_out, sum(h_out), n_scr, sum(h_sem)])
        per_host = list(zip(hosts, split(hin, h_in), split(hout, h_out), split(hsem, h_sem)))

        @pl.when(pl.program_id(0) == 0)
        def _():
            for h, a, o, s in per_host:
                h.start(a, o, s)

        body(*ins, *outs, *scr)

        for h, a, o, s in per_host:
            for step, ws in sorted(h.forward_steps(grid[0]).items()):
                pl.when(pl.program_id(0) == step)(functools.partial(h.forward, ws, a, o, s))

        @pl.when(pl.program_id(0) == grid[0] - 1)
        def _():
            for h, a, o, s in per_host:
                h.complete(a, o, s)

    any_spec = pl.BlockSpec(memory_space=pl.ANY)
    call = pl.pallas_call(
        wrapped, name=name, grid=grid, in_specs=list(in_specs) + [any_spec] * sum(h_in),
        out_specs=own_specs + [any_spec] * sum(h_out),
        out_shape=own_shapes + [s for h in hosts for s in h.out_shapes],
        scratch_shapes=list(scratch) + [s for h in hosts for s in h.sems], compiler_params=params)

    def run(*args):
        res = call(*args, *[a for h in hosts for a in h.arrays])
        own = res[0] if single else list(res[:n_out])
        return own, split(list(res[n_out:]), h_out)

    return run


def _sds(shape, dtype=F32):
    return jax.ShapeDtypeStruct(shape, dtype)


def _row(tm, c):
    return pl.BlockSpec((tm, c), lambda i: (i, 0))


def _grow(g, tm, c):
    return pl.BlockSpec((g, tm, c), lambda i: (0, i, 0))


def _const(shape):
    nd = len(shape)
    return pl.BlockSpec(shape, lambda i: (0,) * nd, pipeline_mode=pl.Buffered(1))


def _wspec(w):
    return pl.BlockSpec((NQ, None, w.shape[2], w.shape[3]), lambda i: (0, 0, 0, 0), pipeline_mode=pl.Buffered(1))


def _prev(tm, hb, c):
    return pl.BlockSpec((hb, c), lambda i: (jnp.maximum(i * (tm // hb) - 1, 0), 0))


def _next(tm, hb, c):
    return pl.BlockSpec((hb, c), lambda i: (jnp.minimum((i + 1) * (tm // hb), S // hb - 1), 0))


def _acc(r, c):
    return pl.BlockSpec((r, c), lambda i: (0, 0))


def _sig(x):
    return 1.0 / (1.0 + jnp.exp(-x))


def _ln(z, g, b):
    mu = jnp.mean(z, axis=-1, keepdims=True)
    zc = z - mu
    rstd = lax.rsqrt(jnp.mean(zc * zc, axis=-1, keepdims=True) + LN_EPS)
    xhat = zc * rstd
    return xhat * g + b, xhat, rstd


def _ln_bwd(dyg, xhat, rstd):
    return rstd * (dyg - jnp.mean(dyg, axis=-1, keepdims=True) - xhat * jnp.mean(dyg * xhat, axis=-1, keepdims=True))


def _mm(a, w):
    return jnp.dot(a.astype(BF16), w, preferred_element_type=F32)


def _mmt(a, w):
    return lax.dot_general(a.astype(BF16), w, (((1,), (1,)), ((), ())), preferred_element_type=F32)


def _colsum(x):
    return jnp.sum(x, axis=0, keepdims=True)


def _gelu_and_grad(x):
    x2 = x * x
    t = jnp.tanh(x * (GELU_C + (GELU_C * GELU_A) * x2))
    hx = 0.5 * x
    return hx + hx * t, 0.5 + 0.5 * t + hx * (1.0 - t * t) * (GELU_C + (3.0 * GELU_C * GELU_A) * x2)


def _silu_grad(a, sg):
    return sg * (1.0 + a * (1.0 - sg))


def _sgu_masks():
    r = lax.broadcasted_iota(jnp.int32, (SGU_T, SGU_T), 0) // SGU_CHUNK
    c = lax.broadcasted_iota(jnp.int32, (SGU_T, SGU_T), 1) // SGU_CHUNK
    return r >= c, c >= r


def _fill_halo(buf, lo, n, halo_val_fn, is_edge):
    @pl.when(is_edge)
    def _():
        buf[lo:lo + n, :] = jnp.zeros((n, buf.shape[1]), F32)

    @pl.when(jnp.logical_not(is_edge))
    def _():
        buf[lo:lo + n, :] = halo_val_fn()


SUB, LANE = 8, 128
ROWS_AT_ONCE = 16


def _shift_copies(buf, sh):
    rows = sh.shape[1]
    for s in range(1, SUB):
        sh[s - 1, :, :] = buf[pl.ds(s, rows), :]


def _tiles(buf, sh, s, first, count, group0, lanes):
    src = buf if s == 0 else sh.at[s - 1]
    return {t: src[pl.ds(pl.multiple_of((group0 + t) * SUB, SUB), SUB), lanes] for t in range(first, first + count)}


def _by_shift(offsets):
    out = []
    for s in range(SUB):
        taps = [(k, o // SUB) for k, o in enumerate(offsets) if o % SUB == s]
        if taps:
            out.append((s, taps))
    return out


def _conv_rows(out_ref, w_ref, bias_ref, offsets, buf, sh, tm):
    n = ROWS_AT_ONCE
    for cb in range(D // LANE):
        lanes = slice(cb * LANE, (cb + 1) * LANE)
        bias = None if bias_ref is None else jnp.broadcast_to(bias_ref[:, lanes], (SUB, LANE))

        def body(jb, carry):
            accs = [bias] * n
            for s, taps in _by_shift(offsets):
                ms = [m for _, m in taps]
                tiles = _tiles(buf, sh, s, min(ms), max(ms) - min(ms) + n, jb * n, lanes)
                for k, m in taps:
                    wk = jnp.broadcast_to(w_ref[k:k + 1, lanes], (SUB, LANE))
                    for jj in range(n):
                        t = wk * tiles[m + jj]
                        accs[jj] = t if accs[jj] is None else accs[jj] + t
            for jj in range(n):
                out_ref[pl.ds(pl.multiple_of((jb * n + jj) * SUB, SUB), SUB), lanes] = accs[jj]
            return carry

        lax.fori_loop(0, tm // (SUB * n), body, 0)


def _conv_wgrad(dw_ref, d_ref, offsets, buf, sh, tm):
    n = 4
    for cb in range(D // LANE):
        lanes = slice(cb * LANE, (cb + 1) * LANE)

        def body(jq, accs):
            accs = list(accs)
            d = [d_ref[pl.ds(pl.multiple_of((jq * n + jj) * SUB, SUB), SUB), lanes] for jj in range(n)]
            for s, taps in _by_shift(offsets):
                ms = [m for _, m in taps]
                tiles = _tiles(buf, sh, s, min(ms), max(ms) - min(ms) + n, jq * n, lanes)
                for k, m in taps:
                    for jj in range(n):
                        accs[k] = accs[k] + d[jj] * tiles[m + jj]
            return tuple(accs)

        accs = lax.fori_loop(0, tm // (SUB * n), body, tuple(jnp.zeros((SUB, LANE), F32) for _ in offsets))
        for k, acc in enumerate(accs):
            dw_ref[k:k + 1, lanes] += jnp.sum(acc, axis=0, keepdims=True)


def _fwd_a1(x0, w1, b1, l, hosts=()):
    tm = 512

    def body(x_ref, w_ref, b_ref, h_ref, glu_ref):
        xb = x_ref[...].astype(BF16)
        for q in range(NQ):
            sl = slice(q * 512, (q + 1) * 512)
            h_ref[:, sl] = jnp.dot(xb, w_ref[q], preferred_element_type=F32) + b_ref[:, sl]
        glu_ref[...] = h_ref[:, :D] * _sig(h_ref[:, D:])

    return _call(f"fwd_a1_{l}", body, (S // tm,), [_row(tm, D), _wspec(w1), _const((1, 2 * D))],
                 [_row(tm, 2 * D), _row(tm, D)], [_sds((S, 2 * D)), _sds((S, D))], hosts=hosts)(x0, w1, b1)


def _fwd_a2(glu, x0, wdw, bdw, lg, lb, w2, l, hosts=()):
    tm = 256

    def body(g_ref, gp_ref, x_ref, wdw_ref, bdw_ref, lg_ref, lb_ref, w2_ref, z_ref, cv_ref, buf, sh):
        i = pl.program_id(0)
        _fill_halo(buf, 0, HALO_A, lambda: gp_ref[...], i == 0)
        buf[HALO_A:HALO_A + tm, :] = g_ref[...]
        _shift_copies(buf, sh)
        _conv_rows(cv_ref, wdw_ref, bdw_ref, [HALO_A - (CONV_A - 1) + k for k in range(CONV_A)], buf, sh, tm)
        n, _, _ = _ln(cv_ref[...], lg_ref[...], lb_ref[...])
        sb = (n * _sig(n)).astype(BF16)
        z_ref[...] = ALPHA * x_ref[...] + jnp.dot(sb, w2_ref[...], preferred_element_type=F32)

    return _call(f"fwd_a2_{l}", body, (S // tm,),
                 [_row(tm, D), _prev(tm, HALO_A, D), _row(tm, D), _const((32, D)), _const((1, D)), _const((1, D)),
                  _const((1, D)), _const(w2.shape)],
                 [_row(tm, D), _row(tm, D)], [_sds((S, D)), _sds((S, D))],
                 scratch=[pltpu.VMEM((HALO_A + tm, D), F32), pltpu.VMEM((SUB - 1, HALO_A + tm - SUB, D), F32)],
                 hosts=hosts)(glu, glu, x0, wdw, bdw, lg, lb, w2)


def _fwd_b(x0, win, b_in, lg, lb, ws, bsx, wout, hosts=()):
    tm = 256

    def body(x_ref, win_ref, bin_ref, lg_ref, lb_ref, ws_ref, bsx_ref, wout_ref, z_ref, zg_ref, gg_ref, f_scr, h_ref):
        xb = x_ref[...].astype(BF16)
        for q in range(NQ):
            sl = slice(q * 1024, (q + 1) * 1024)
            h_ref[:, sl] = jnp.dot(xb, win_ref[q], preferred_element_type=F32) + bin_ref[:, sl]
        u, du = _gelu_and_grad(h_ref[:, :E])
        v, dv = _gelu_and_grad(h_ref[:, E:])
        zg_ref[:, 0:E] = u.astype(BF16)
        zg_ref[:, E:2 * E] = v.astype(BF16)
        gg_ref[:, 0:E] = du.astype(BF16)
        gg_ref[:, E:2 * E] = dv.astype(BF16)
        vn, _, _ = _ln(v, lg_ref[...], lb_ref[...])
        vnb = vn.astype(BF16)
        mask, _ = _sgu_masks()
        for hd in range(SGU_H):
            wm = jnp.where(mask, ws_ref[hd], 0.0).astype(BF16)
            cs = slice(hd * SGU_G, (hd + 1) * SGU_G)
            for n in range(tm // SGU_T):
                rs = slice(n * SGU_T, (n + 1) * SGU_T)
                f_scr[rs, cs] = jnp.dot(wm, vnb[rs, cs], preferred_element_type=F32) + bsx_ref[hd]
        mb = (u * f_scr[...]).astype(BF16)
        z_ref[...] = ALPHA * x_ref[...] + jnp.dot(mb, wout_ref[...], preferred_element_type=F32)

    return _call("fwd_b", body, (S // tm,),
                 [_row(tm, D), _wspec(win), _const((1, 2 * E)), _const((1, E)), _const((1, E)),
                  _const((SGU_H, SGU_T, SGU_T)), _const((SGU_H, SGU_T, SGU_G)), _const(wout.shape)],
                 [_row(tm, D), _row(tm, 2 * E), _row(tm, 2 * E)],
                 [_sds((S, D)), _sds((S, 2 * E), BF16), _sds((S, 2 * E), BF16)],
                 scratch=[pltpu.VMEM((tm, E), F32), pltpu.VMEM((tm, 2 * E), F32)], hosts=hosts
                 )(x0, win, b_in, lg, lb, ws, bsx, wout)


def _fwd_c1(x0, win, hosts=()):
    tm = 512

    def body(x_ref, w_ref, hc_ref):
        xb = x_ref[...].astype(BF16)
        for q in range(NQ):
            hc_ref[:, q * 768:(q + 1) * 768] = jnp.dot(xb, w_ref[q], preferred_element_type=F32)

    return _call("fwd_c1", body, (S // tm,), [_row(tm, D), _wspec(win)], _row(tm, 3 * D),
                 _sds((S, 3 * D)), hosts=hosts)(x0, win)


def _short_conv(buf, hc_ref, hcp_ref, wc_ref, tm, i):
    _fill_halo(buf, 0, HALO_C, lambda: hcp_ref[:, D:2 * D] * hcp_ref[:, 2 * D:], i == 0)
    buf[HALO_C:HALO_C + tm, :] = hc_ref[:, D:2 * D] * hc_ref[:, 2 * D:]
    y = wc_ref[0:1, :] * buf[pl.ds(HALO_C - 2, tm), :]
    for k in range(1, CONV_C):
        y = y + wc_ref[k:k + 1, :] * buf[pl.ds(HALO_C - 2 + k, tm), :]
    return y


def _fwd_c2(hc, x0, wc, wout, hosts=()):
    tm = 256

    def body(hc_ref, hcp_ref, x_ref, wc_ref, wout_ref, z_ref, buf):
        y = _short_conv(buf, hc_ref, hcp_ref, wc_ref, tm, pl.program_id(0))
        mb = (hc_ref[:, :D] * y).astype(BF16)
        z_ref[...] = ALPHA * x_ref[...] + jnp.dot(mb, wout_ref[...], preferred_element_type=F32)

    return _call("fwd_c2", body, (S // tm,),
                 [_row(tm, 3 * D), _prev(tm, HALO_C, 3 * D), _row(tm, D), _const((8, D)), _const(wout.shape)],
                 _row(tm, D), _sds((S, D)), scratch=[pltpu.VMEM((HALO_C + tm, D), F32)], hosts=hosts
                 )(hc, hc, x0, wc, wout)


def _fwd_ffn(z1, lg, lb, wgt, wut, wd, l, hosts=()):
    tm = 256

    def body(z_ref, lg_ref, lb_ref, wg_ref, wu_ref, wd_ref, o_ref, a_ref, u_ref, hm_ref):
        x1, _, _ = _ln(z_ref[...], lg_ref[...], lb_ref[...])
        xb = x1.astype(BF16)
        a = _mmt(xb, wg_ref[...])
        u = _mmt(xb, wu_ref[...])
        hmb = (a * _sig(a) * u).astype(BF16)
        a_ref[...] = a.astype(BF16)
        u_ref[...] = u.astype(BF16)
        hm_ref[...] = hmb
        o_ref[...] = ALPHA * x1 + jnp.dot(hmb, wd_ref[...], preferred_element_type=F32)

    return _call(f"fwd_ffn_{l}", body, (S // tm,),
                 [_row(tm, D), _const((1, D)), _const((1, D)), _const((FF, D)), _const((FF, D)), _const((FF, D))],
                 [_row(tm, D), _row(tm, FF), _row(tm, FF), _row(tm, FF)],
                 [_sds((S, D)), _sds((S, FF), BF16), _sds((S, FF), BF16), _sds((S, FF), BF16)],
                 hosts=hosts)(z1, lg, lb, wgt, wut, wd)


def _ple_parts(z2, p, lg, lb, wg_ref, wp_ref, pg):
    x2, xhat, rstd = _ln(z2, lg, lb)
    xb = x2.astype(BF16)
    gate = _sig(jnp.dot(xb, wg_ref[...], preferred_element_type=F32))
    pb = p.astype(BF16)
    qp = jnp.concatenate([jnp.dot(pb, wp_ref[q], preferred_element_type=F32) for q in range(NQ)], axis=1)
    rs = lax.rsqrt(jnp.mean(qp * qp, axis=-1, keepdims=True) + LN_EPS)
    qn = qp * rs
    return x2, xhat, rstd, xb, gate, qn, rs, qn * pg


def _fwd_ple(z2, p, lg, lb, wg, wp, pg, l, hosts=()):
    tm = 512

    def body(z_ref, p_ref, lg_ref, lb_ref, wg_ref, wp_ref, pg_ref, o_ref):
        x2, _, _, _, gate, _, _, r = _ple_parts(z_ref[...], p_ref[...], lg_ref[...], lb_ref[...], wg_ref, wp_ref,
                                                pg_ref[...])
        o_ref[...] = x2 + gate * r

    return _call(f"fwd_ple_{l}", body, (S // tm,),
                 [_row(tm, D), _row(tm, 256), _const((1, D)), _const((1, D)), _const(wg.shape), _wspec(wp),
                  _const((1, D))],
                 _row(tm, D), _sds((S, D)), hosts=hosts)(z2, p, lg, lb, wg, wp, pg)


def _loss_head(y, target):
    tm = 512

    def body(y_ref, t_ref, dy_ref, acc_ref):
        @pl.when(pl.program_id(0) == 0)
        def _():
            acc_ref[...] = jnp.zeros_like(acc_ref)

        e = y_ref[...] - t_ref[...]
        dy_ref[...] = e * (1.0 / D)
        acc_ref[0:1, :] += _colsum(e * e)

    return _call("loss_head", body, (S // tm,), [_row(tm, D), _row(tm, D)], [_row(tm, D), _acc(8, D)],
                 [_sds((S, D)), _sds((8, D))])(y, target)


def _zero_first(*refs):
    @pl.when(pl.program_id(0) == 0)
    def _():
        for r in refs:
            r[...] = jnp.zeros_like(r)


def _bwd_ple(g, z2, p, lg, lb, wg, wp, pg, l, hosts=()):
    tm = 256

    def body(g_ref, z_ref, p_ref, lg_ref, lb_ref, wg_ref, wp_ref, pg_ref, dz_ref, xb_ref, dgp_ref, dqp_ref, acc_ref):
        _zero_first(acc_ref)
        gin = g_ref[...]
        lgv, pgv = lg_ref[...], pg_ref[...]
        _, xhat, rstd, xb, gate, qn, rs, r = _ple_parts(z_ref[...], p_ref[...], lgv, lb_ref[...], wg_ref, wp_ref, pgv)
        xb_ref[...] = xb
        dgpb = (gin * r * gate * (1.0 - gate)).astype(BF16)
        dgp_ref[...] = dgpb
        dx2 = gin + _mmt(dgpb, wg_ref[...])
        dr = gin * gate
        acc_ref[0:1, :] += _colsum(dr * qn)
        t = dr * pgv
        dqp_ref[...] = (rs * (t - qn * jnp.mean(t * qn, axis=-1, keepdims=True))).astype(BF16)
        acc_ref[1:2, :] += _colsum(dx2 * xhat)
        acc_ref[2:3, :] += _colsum(dx2)
        dz_ref[...] = _ln_bwd(dx2 * lgv, xhat, rstd)

    return _call(f"bwd_ple_{l}", body, (S // tm,),
                 [_row(tm, D), _row(tm, D), _row(tm, 256), _const((1, D)), _const((1, D)), _const(wg.shape),
                  _wspec(wp), _const((1, D))],
                 [_row(tm, D), _row(tm, D), _row(tm, D), _row(tm, D), _acc(8, D)],
                 [_sds((S, D)), _sds((S, D), BF16), _sds((S, D), BF16), _sds((S, D), BF16), _sds((8, D))],
                 hosts=hosts)(g, z2, p, lg, lb, wg, wp, pg)


def _bwd_ffn(dz2, z1, ab, ub, lg, lb, wgt, wut, wd, l, hosts=()):
    tm = 256

    def body(dz2_ref, z_ref, a_ref, u_ref, lg_ref, lb_ref, wg_ref, wu_ref, wd_ref, dz1_ref, xb_ref, da_ref, du_ref,
             acc_ref):
        _zero_first(acc_ref)
        dz2v = dz2_ref[...]
        lgv = lg_ref[...]
        x1, xhat, rstd = _ln(z_ref[...], lgv, lb_ref[...])
        xb_ref[...] = x1.astype(BF16)
        a = a_ref[...].astype(F32)
        u = u_ref[...].astype(F32)
        sg = _sig(a)
        dhm = _mmt(dz2v, wd_ref[...])
        dub = (dhm * (a * sg)).astype(BF16)
        dab = (dhm * u * _silu_grad(a, sg)).astype(BF16)
        da_ref[...] = dab
        du_ref[...] = dub
        dx1 = ALPHA * dz2v + _mm(dab, wg_ref[...]) + _mm(dub, wu_ref[...])
        acc_ref[0:1, :] += _colsum(dx1 * xhat)
        acc_ref[1:2, :] += _colsum(dx1)
        dz1_ref[...] = _ln_bwd(dx1 * lgv, xhat, rstd)

    return _call(f"bwd_ffn_{l}", body, (S // tm,),
                 [_row(tm, D), _row(tm, D), _row(tm, FF), _row(tm, FF), _const((1, D)), _const((1, D)),
                  _const((FF, D)), _const((FF, D)), _const((FF, D))],
                 [_row(tm, D), _row(tm, D), _row(tm, FF), _row(tm, FF), _acc(8, D)],
                 [_sds((S, D)), _sds((S, D), BF16), _sds((S, FF), BF16), _sds((S, FF), BF16), _sds((8, D))],
                 hosts=hosts)(dz2, z1, ab, ub, lg, lb, wgt, wut, wd)


def _bwd_a2(dz1, cv, lg, lb, w2, l, hosts=()):
    tm = 512

    def body(dz_ref, cv_ref, lg_ref, lb_ref, w2_ref, dcv_ref, sb_ref, acc_ref):
        _zero_first(acc_ref)
        lgv = lg_ref[...]
        n, xhat, rstd = _ln(cv_ref[...], lgv, lb_ref[...])
        sg = _sig(n)
        sb_ref[...] = (n * sg).astype(BF16)
        dzb = dz_ref[...].astype(BF16)
        ds = _mmt(dzb, w2_ref[...])
        dn = ds * _silu_grad(n, sg)
        acc_ref[0:1, :] += _colsum(dn * xhat)
        acc_ref[1:2, :] += _colsum(dn)
        dcv = _ln_bwd(dn * lgv, xhat, rstd)
        acc_ref[2:3, :] += _colsum(dcv)
        dcv_ref[...] = dcv

    return _call(f"bwd_a2_{l}", body, (S // tm,),
                 [_row(tm, D), _row(tm, D), _const((1, D)), _const((1, D)), _const(w2.shape)],
                 [_row(tm, D), _row(tm, D), _acc(8, D)],
                 [_sds((S, D)), _sds((S, D), BF16), _sds((8, D))], hosts=hosts)(dz1, cv, lg, lb, w2)


def _bwd_conv_a(dcv, glu, wdw, l, hosts=()):
    tm = 256
    nb = S // tm

    def body(d_ref, dn_ref, g_ref, gp_ref, w_ref, dglu_ref, dw_ref, bufd, bufx, sh):
        i = pl.program_id(0)
        _zero_first(dw_ref)
        bufd[0:tm, :] = d_ref[...]
        _fill_halo(bufd, tm, HALO_A, lambda: dn_ref[...], i == nb - 1)
        _fill_halo(bufx, 0, HALO_A, lambda: gp_ref[...], i == 0)
        bufx[HALO_A:HALO_A + tm, :] = g_ref[...]
        _shift_copies(bufd, sh)
        _conv_rows(dglu_ref, w_ref, None, [CONV_A - 1 - k for k in range(CONV_A)], bufd, sh, tm)
        _shift_copies(bufx, sh)
        _conv_wgrad(dw_ref, d_ref, [HALO_A - (CONV_A - 1) + k for k in range(CONV_A)], bufx, sh, tm)

    return _call(f"bwd_conv_a_{l}", body, (nb,),
                 [_row(tm, D), _next(tm, HALO_A, D), _row(tm, D), _prev(tm, HALO_A, D), _const((32, D))],
                 [_row(tm, D), _acc(32, D)], [_sds((S, D)), _sds((32, D))],
                 scratch=[pltpu.VMEM((tm + HALO_A, D), F32), pltpu.VMEM((HALO_A + tm, D), F32),
                          pltpu.VMEM((SUB - 1, HALO_A + tm - SUB, D), F32)], hosts=hosts)(dcv, dcv, glu, glu, wdw)


def _bwd_a1(dglu, h, dz1, w1, l, hosts=()):
    tm = 256

    def body(dg_ref, h_ref, dz_ref, w_ref, dx_ref, dh_ref, acc_ref):
        _zero_first(acc_ref)
        a, g = h_ref[:, :D], h_ref[:, D:]
        sg = _sig(g)
        dgl = dg_ref[...]
        da = dgl * sg
        dg = dgl * a * sg * (1.0 - sg)
        acc_ref[0:1, 0:D] += _colsum(da)
        acc_ref[0:1, D:2 * D] += _colsum(dg)
        dh_ref[:, 0:D] = da.astype(BF16)
        dh_ref[:, D:2 * D] = dg.astype(BF16)
        dx = ALPHA * dz_ref[...]
        for q in range(NQ):
            dx = dx + _mmt(dh_ref[:, q * 512:(q + 1) * 512], w_ref[q])
        dx_ref[...] = dx

    return _call(f"bwd_a1_{l}", body, (S // tm,),
                 [_row(tm, D), _row(tm, 2 * D), _row(tm, D), _wspec(w1)],
                 [_row(tm, D), _row(tm, 2 * D), _acc(8, 2 * D)],
                 [_sds((S, D)), _sds((S, 2 * D), BF16), _sds((8, 2 * D))], hosts=hosts)(dglu, h, dz1, w1)


def _bwd_c2(dz1, hc, wc, wout):
    tm = 256

    def body(dz_ref, hc_ref, hcp_ref, wc_ref, wout_ref, dy_ref, dbg_ref, mb_ref, buf):
        y = _short_conv(buf, hc_ref, hcp_ref, wc_ref, tm, pl.program_id(0))
        dzb = dz_ref[...].astype(BF16)
        dm = _mmt(dzb, wout_ref[...])
        bg = hc_ref[:, :D]
        mb_ref[...] = (bg * y).astype(BF16)
        dbg_ref[...] = (dm * y).astype(BF16)
        dy_ref[...] = dm * bg

    return _call("bwd_c2", body, (S // tm,),
                 [_row(tm, D), _row(tm, 3 * D), _prev(tm, HALO_C, 3 * D), _const((8, D)), _const(wout.shape)],
                 [_row(tm, D), _row(tm, D), _row(tm, D)],
                 [_sds((S, D)), _sds((S, D), BF16), _sds((S, D), BF16)],
                 scratch=[pltpu.VMEM((HALO_C + tm, D), F32)])(dz1, hc, hc, wc, wout)


def _bwd_c1(dy, hc, dbg, dz1, wc, win):
    tm = 256
    nb = S // tm

    def body(d_ref, dn_ref, hc_ref, hcp_ref, dbg_ref, dz_ref, wc_ref, win_ref, dx_ref, dhc_ref, dwc_ref, bufd, bufq):
        i = pl.program_id(0)
        _zero_first(dwc_ref)
        bufd[0:tm, :] = d_ref[...]
        _fill_halo(bufd, tm, HALO_C, lambda: dn_ref[...], i == nb - 1)
        _fill_halo(bufq, 0, HALO_C, lambda: hcp_ref[:, D:2 * D] * hcp_ref[:, 2 * D:], i == 0)
        bufq[HALO_C:HALO_C + tm, :] = hc_ref[:, D:2 * D] * hc_ref[:, 2 * D:]
        dq = wc_ref[0:1, :] * bufd[pl.ds(CONV_C - 1, tm), :]
        for k in range(1, CONV_C):
            dq = dq + wc_ref[k:k + 1, :] * bufd[pl.ds(CONV_C - 1 - k, tm), :]
        dv = d_ref[...]
        for k in range(CONV_C):
            dwc_ref[k:k + 1, :] += _colsum(dv * bufq[pl.ds(HALO_C - (CONV_C - 1) + k, tm), :])
        dhc_ref[:, 0:D] = dbg_ref[...]
        dhc_ref[:, D:2 * D] = (dq * hc_ref[:, 2 * D:]).astype(BF16)
        dhc_ref[:, 2 * D:3 * D] = (dq * hc_ref[:, D:2 * D]).astype(BF16)
        dx = ALPHA * dz_ref[...]
        for q in range(NQ):
            dx = dx + _mmt(dhc_ref[:, q * 768:(q + 1) * 768], win_ref[q])
        dx_ref[...] = dx

    return _call("bwd_c1", body, (nb,),
                 [_row(tm, D), _next(tm, HALO_C, D), _row(tm, 3 * D), _prev(tm, HALO_C, 3 * D), _row(tm, D),
                  _row(tm, D), _const((8, D)), _wspec(win)],
                 [_row(tm, D), _row(tm, 3 * D), _acc(8, D)],
                 [_sds((S, D)), _sds((S, 3 * D), BF16), _sds((8, D))],
                 scratch=[pltpu.VMEM((tm + HALO_C, D), F32), pltpu.VMEM((HALO_C + tm, D), F32)]
                 )(dy, dy, hc, hc, dbg, dz1, wc, win)


def _bwd_b(dz1, zg, gg, lg, lb, win, wout, ws, wst, bsx, hosts=()):
    tm = 256
    nb = S // tm

    def body(dz_ref, zg_ref, gg_ref, lg_ref, lb_ref, win_ref, wout_ref, ws_ref, wst_ref, bsx_ref,
             dx_ref, dh_ref, mb_ref, acc_ref, dws_ref, dbs_ref, f_scr, dvn_scr):
        _zero_first(acc_ref, dws_ref, dbs_ref)
        lgv = lg_ref[...]
        u = zg_ref[:, :E].astype(F32)
        v = zg_ref[:, E:].astype(F32)
        vn, xhat, rstd = _ln(v, lgv, lb_ref[...])
        vnb = vn.astype(BF16)
        dzb = dz_ref[...].astype(BF16)
        dm = _mmt(dzb, wout_ref[...])
        mask, mask_t = _sgu_masks()
        for hd in range(SGU_H):
            wm = jnp.where(mask, ws_ref[hd], 0.0).astype(BF16)
            cs = slice(hd * SGU_G, (hd + 1) * SGU_G)
            for n in range(tm // SGU_T):
                rs = slice(n * SGU_T, (n + 1) * SGU_T)
                f_scr[rs, cs] = jnp.dot(wm, vnb[rs, cs], preferred_element_type=F32) + bsx_ref[hd]
        f = f_scr[...]
        mb_ref[...] = (u * f).astype(BF16)
        du = dm * f
        df = dm * u
        dfb = df.astype(BF16)
        for hd in range(SGU_H):
            wmt = jnp.where(mask_t, wst_ref[hd], 0.0).astype(BF16)
            cs = slice(hd * SGU_G, (hd + 1) * SGU_G)
            for n in range(tm // SGU_T):
                rs = slice(n * SGU_T, (n + 1) * SGU_T)
                dvn_scr[rs, cs] = jnp.dot(wmt, dfb[rs, cs], preferred_element_type=F32)
                dws_ref[hd] += lax.dot_general(dfb[rs, cs], vnb[rs, cs], (((1,), (1,)), ((), ())),
                                               preferred_element_type=F32)
                dbs_ref[hd] += df[rs, cs]
        dvn = dvn_scr[...]
        acc_ref[1:2, 0:E] += _colsum(dvn * xhat)
        acc_ref[2:3, 0:E] += _colsum(dvn)
        dv = _ln_bwd(dvn * lgv, xhat, rstd)
        dhu = du * gg_ref[:, :E].astype(F32)
        dhv = dv * gg_ref[:, E:].astype(F32)
        acc_ref[0:1, 0:E] += _colsum(dhu)
        acc_ref[0:1, E:2 * E] += _colsum(dhv)
        dh_ref[:, 0:E] = dhu.astype(BF16)
        dh_ref[:, E:2 * E] = dhv.astype(BF16)
        dx = ALPHA * dz_ref[...]
        for q in range(NQ):
            dx = dx + _mmt(dh_ref[:, q * 1024:(q + 1) * 1024], win_ref[q])
        dx_ref[...] = dx

        @pl.when(pl.program_id(0) == nb - 1)
        def _():
            for hd in range(SGU_H):
                dws_ref[hd] = jnp.where(mask, dws_ref[hd], 0.0)

    c3 = lambda a, b, c: pl.BlockSpec((a, b, c), lambda i: (0, 0, 0))
    return _call("bwd_b", body, (nb,),
                 [_row(tm, D), _row(tm, 2 * E), _row(tm, 2 * E), _const((1, E)), _const((1, E)), _wspec(win),
                  _const(wout.shape), _const((SGU_H, SGU_T, SGU_T)), _const((SGU_H, SGU_T, SGU_T)),
                  _const((SGU_H, SGU_T, SGU_G))],
                 [_row(tm, D), _row(tm, 2 * E), _row(tm, E), _acc(8, 2 * E), c3(SGU_H, SGU_T, SGU_T),
                  c3(SGU_H, SGU_T, SGU_G)],
                 [_sds((S, D)), _sds((S, 2 * E), BF16), _sds((S, E), BF16), _sds((8, 2 * E)),
                  _sds((SGU_H, SGU_T, SGU_T)), _sds((SGU_H, SGU_T, SGU_G))],
                 scratch=[pltpu.VMEM((tm, E), F32), pltpu.VMEM((tm, E), F32)], hosts=hosts
                 )(dz1, zg, gg, lg, lb, win, wout, ws, wst, bsx)


def _mm_tn(name, a, amode, b, bmode, k, n, groups=NQ, hosts=()):
    def block_bytes(ts):
        ka = k if amode == "1" else groups * k
        nb = n if bmode == "1" else groups * n
        return 2 * (ts * ka * a.dtype.itemsize + ts * nb * b.dtype.itemsize + groups * k * n * 4)

    ts = min(1024 if block_bytes(1024) <= DW_BLOCK_BUDGET else 512, S)

    def spec(mode, w):
        if mode == "1":
            return pl.BlockSpec((ts, w), lambda s: (s, 0))
        if mode == "c":
            return pl.BlockSpec((ts, groups * w), lambda s: (s, 0))
        return pl.BlockSpec((groups, ts, w), lambda s: (0, s, 0))

    def pick(ref, mode, w, g):
        if mode == "1":
            return ref[...]
        if mode == "c":
            return ref[:, g * w:(g + 1) * w]
        return ref[g]

    def body(a_ref, b_ref, o_ref):
        _zero_first(o_ref)
        a_t = jnp.transpose(a_ref[...].astype(BF16)) if amode == "1" else None
        b_1 = b_ref[...].astype(BF16) if bmode == "1" else None
        for g in range(groups):
            lhs = a_t if amode == "1" else jnp.transpose(pick(a_ref, amode, k, g).astype(BF16))
            rhs = b_1 if bmode == "1" else pick(b_ref, bmode, n, g).astype(BF16)
            o_ref[0, g] += jnp.dot(lhs, rhs, preferred_element_type=F32)

    return _call(name, body, (S // ts,), [spec(amode, k), spec(bmode, n)],
                 pl.BlockSpec((1, groups, k, n), lambda s: (0, 0, 0, 0)), _sds((1, groups, k, n)), hosts=hosts)(a, b)


def _row_block(k, cap=256):
    return max(t for t in range(16, min(k, cap) + 1, 16) if k % t == 0)


def _cast_bf16(w, hosts=()):
    nl, k, n = w.shape
    tb = _row_block(k, 512)
    nb = k // tb

    def body(w_ref, o_ref):
        o_ref[...] = w_ref[...].astype(BF16)

    spec = pl.BlockSpec((None, tb, n), lambda i: (i // nb, i % nb, 0))
    return _call("cast_bf16", body, (nl * nb,), [spec], spec, _sds(w.shape, BF16), hosts=hosts)(w)


def _adam(name, w, m, v, gc, l, prev):
    nl, k, n = w.shape
    nc = gc.shape[0]
    tb = _row_block(k, 512)

    def body(w_ref, m_ref, v_ref, g_ref, *rest):
        go_ref, d_ref, mo_ref, vo_ref = rest[-4:]
        g = g_ref[0].astype(F32)
        for c in range(1, nc):
            g = g + g_ref[c].astype(F32)
        m2 = ADAM_B1 * m_ref[...] + (1.0 - ADAM_B1) * g
        v2 = ADAM_B2 * v_ref[...] + (1.0 - ADAM_B2) * (g * g)
        m_hat = m2 / (1.0 - ADAM_B1 ** ADAM_STEP)
        v_hat = v2 / (1.0 - ADAM_B2 ** ADAM_STEP)
        go_ref[...] = g
        d_ref[...] = -ADAM_LR * (m_hat / (jnp.sqrt(v_hat) + ADAM_EPS) + ADAM_WD * w_ref[...])
        mo_ref[...] = m2
        vo_ref[...] = v2

    spec = pl.BlockSpec((None, tb, n), lambda i: (l, i, 0))
    gspec = pl.BlockSpec((nc, None, tb, n), lambda i: (0, 0, i, 0))
    in_specs, args, aliases = [spec, spec, spec, gspec], [w, m, v, gc], {}
    if prev is not None:
        in_specs += [pl.BlockSpec(memory_space=pl.ANY)] * 4
        args += list(prev)
        aliases = {4 + j: j for j in range(4)}
    return _call(name, body, (k // tb,), in_specs, [spec] * 4, [_sds(w.shape)] * 4, aliases=aliases)(*args)


def _sum8(name, g8):
    r = g8.shape[1]

    def body(g_ref, o_ref):
        acc = g_ref[0]
        for d in range(1, 8):
            acc = acc + g_ref[d]
        o_ref[...] = acc

    return _call(name, body, (1,), [pl.BlockSpec((8, r, 128), lambda i: (0, 0, 0))],
                 pl.BlockSpec((r, 128), lambda i: (0, 0)), _sds((r, 128)))(g8)


def _place():
    x, y, c = lax.axis_index("x"), lax.axis_index("y"), lax.axis_index("c")
    return x, y, c, 2 * x + y, (x, y, 1 - c), [(1 - x, y), (x, 1 - y), (1 - x, 1 - y)]


class _Exchange:
    def __init__(self, arrays, out_shapes):
        self.arrays, self.out_shapes = list(arrays), list(out_shapes)
        n = len(self.arrays)
        self.sems = [pltpu.SemaphoreType.DMA((7 * n,)), pltpu.SemaphoreType.DMA((7 * n,)),
                     pltpu.SemaphoreType.DMA((n,))]

    def _copies(self, ins, outs, sems):
        send, recv, lsem = sems
        local_src, remote_src, dst = self.maps(ins, outs)
        x, y, c, q, sib, chips = _place()

        def rcopy(w, k, qq, cc, to, src=None):
            return pltpu.make_async_remote_copy(
                src_ref=dst(w, qq, cc) if src is None else src, dst_ref=dst(w, qq, cc),
                send_sem=send.at[7 * w + k], recv_sem=recv.at[7 * w + k], device_id=to, device_id_type=MESH)

        def mine(w):
            return pltpu.make_async_copy(local_src(w), dst(w, q, c), lsem.at[w])

        def first(w):
            return [rcopy(w, 0, q, c, sib, local_src(w))] + [
                rcopy(w, 1 + j, q, c, (cx, cy, c), remote_src(w, 2 * cx + cy)) for j, (cx, cy) in enumerate(chips)]

        return rcopy, mine, first, (x, y, c), q, c, sib, chips

    def start(self, ins, outs, sems):
        _, mine, first, *_ = self._copies(ins, outs, sems)
        for w in range(len(self.arrays)):
            mine(w).start()
            for cp in first(w):
                cp.start()

    def forward_steps(self, n_steps):
        sizes = [a.size // a.shape[0] for a in self.arrays]
        plan, moved = {}, 0
        for w, size in enumerate(sizes):
            moved += size
            plan.setdefault(min(n_steps - 1, -(-moved * n_steps // sum(sizes))), []).append(w)
        return plan

    def forward(self, ws, ins, outs, sems):
        rcopy, _, _, me, _, c, sib, chips = self._copies(ins, outs, sems)
        for w in ws:
            for j, (cx, cy) in enumerate(chips):
                rcopy(w, 1 + j, 2 * cx + cy, c, me).wait_recv()
                rcopy(w, 4 + j, 2 * cx + cy, c, sib).start()

    def complete(self, ins, outs, sems):
        rcopy, mine, first, me, q, c, sib, chips = self._copies(ins, outs, sems)
        n = len(self.arrays)
        for w in range(n):
            rcopy(w, 0, q, 1 - c, me).wait_recv()
            for j, (cx, cy) in enumerate(chips):
                rcopy(w, 4 + j, 2 * cx + cy, 1 - c, me).wait_recv()
        for w in range(n):
            for cp in first(w):
                cp.wait_send()
            for j, (cx, cy) in enumerate(chips):
                rcopy(w, 4 + j, 2 * cx + cy, c, sib).wait_send()
            mine(w).wait()


class _GatherWeights(_Exchange):
    def __init__(self, items):
        self.layers = [l for _, l in items]
        self.kh = [s.shape[1] // 2 for s, _ in items]
        super().__init__([s for s, _ in items], [_sds((NQ, 1) + s.shape[1:], BF16) for s, _ in items])

    def maps(self, ins, outs):
        c = lax.axis_index("c")
        src = lambda w: ins[w].at[pl.ds(self.layers[w], 1), pl.ds(c * self.kh[w], self.kh[w]), :]
        return src, lambda w, q: src(w), lambda w, q, cc: outs[w].at[q, :, pl.ds(cc * self.kh[w], self.kh[w]), :]


class _ScatterPartials(_Exchange):
    def __init__(self, parts):
        super().__init__(parts, [_sds((NQ, 1, 2) + p.shape[2:], BF16) for p in parts])

    def maps(self, ins, outs):
        q = 2 * lax.axis_index("x") + lax.axis_index("y")
        return (lambda w: ins[w].at[:, q]), (lambda w, qq: ins[w].at[:, qq]), (lambda w, qq, cc: outs[w].at[qq, :, cc])


class _Gather8(_Exchange):
    def __init__(self, v):
        super().__init__([v], [_sds((8,) + v.shape)])

    def maps(self, ins, outs):
        return (lambda w: ins[0]), (lambda w, q: ins[0]), (lambda w, q, cc: outs[0].at[2 * q + cc])


class _SwapHalves:
    def __init__(self, dws):
        self.arrays = list(dws)
        self.kh = [d.shape[2] // 2 for d in dws]
        self.out_shapes = [_sds(d.shape[:2] + (kh,) + d.shape[3:]) for d, kh in zip(dws, self.kh)]
        self.sems = [pltpu.SemaphoreType.DMA((len(dws),)), pltpu.SemaphoreType.DMA((len(dws),))]

    def _copies(self, ins, outs, sems):
        send, recv = sems
        _, _, c, _, sib, _ = _place()
        return [pltpu.make_async_remote_copy(
            src_ref=ins[w].at[:, :, pl.ds((1 - c) * self.kh[w], self.kh[w]), :], dst_ref=outs[w],
            send_sem=send.at[w], recv_sem=recv.at[w], device_id=sib, device_id_type=MESH)
            for w in range(len(self.arrays))]

    def start(self, ins, outs, sems):
        for cp in self._copies(ins, outs, sems):
            cp.start()

    def forward_steps(self, n_steps):
        return {}

    def complete(self, ins, outs, sems):
        for cp in self._copies(ins, outs, sems):
            cp.wait()


def _comm_only(name, host):
    n_in, n_out = len(host.arrays), len(host.out_shapes)

    def body(*refs):
        ins, outs, sems = refs[:n_in], refs[n_in:n_in + n_out], refs[n_in + n_out:]
        host.start(ins, outs, sems)
        for ws in host.forward_steps(1).values():
            host.forward(ws, ins, outs, sems)
        host.complete(ins, outs, sems)

    any_spec = pl.BlockSpec(memory_space=pl.ANY)
    return pl.pallas_call(body, name=name, in_specs=[any_spec] * n_in, out_specs=[any_spec] * n_out,
                          out_shape=host.out_shapes, scratch_shapes=host.sems)(*host.arrays)


def _add_halves(dw, got, cidx):
    nl, _, k, n = dw.shape
    kh = k // 2
    qb = 2

    def body(c_ref, a_ref, b_ref, o_ref):
        o_ref[...] = (a_ref[...] + b_ref[...]).astype(BF16)

    grid_spec = pltpu.PrefetchScalarGridSpec(
        num_scalar_prefetch=1, grid=(nl, NQ // qb),
        in_specs=[pl.BlockSpec((None, qb, None, kh, n), lambda l, q, c_ref: (l, q, c_ref[0], 0, 0)),
                  pl.BlockSpec((None, qb, kh, n), lambda l, q, c_ref: (l, q, 0, 0))],
        out_specs=pl.BlockSpec((None, qb, kh, n), lambda l, q, c_ref: (l, q, 0, 0)))
    return pl.pallas_call(
        body, name="add_halves", grid_spec=grid_spec, out_shape=_sds((nl, NQ, kh, n), BF16),
        compiler_params=pltpu.CompilerParams(dimension_semantics=("arbitrary", "arbitrary"),
                                             vmem_limit_bytes=VMEM_LIMIT))(cidx, dw.reshape(nl, NQ, 2, kh, n), got)


def _gather8(name, v):
    return _comm_only(name, _Gather8(v))[0]


PACK = 16 * 128


def _pack(arrays):
    parts = []
    for a in arrays:
        flat = a.reshape(-1)
        parts.append(jnp.pad(flat, (0, (-flat.shape[0]) % PACK)))
    return jnp.concatenate(parts).reshape(-1, 128)


def _unpack(packed, shapes):
    flat = packed.reshape(-1)
    out, off = [], 0
    for shp in shapes:
        size = 1
        for d in shp:
            size *= d
        out.append(flat[off:off + size].reshape(shp))
        off += size + (-size) % PACK
    return out


def kernel(x, p, a_w_pw1, a_b_pw1, a_w_dw, a_b_dw, a_ln_g, a_ln_b, a_w_pw2, b_w_in, b_b_in, b_ln_g, b_ln_b, b_w_s, b_b_s, b_w_out, c_w_in, c_w_conv, c_w_out, ln1_g, ln1_b, ln2_g, ln2_b, ffn_w_gate, ffn_w_up, ffn_w_down, ple_w_gate, ple_w_proj, ple_norm_g, loss_target, m_a_w_pw1, m_a_b_pw1, m_a_w_dw, m_a_b_dw, m_a_ln_g, m_a_ln_b, m_a_w_pw2, m_b_w_in, m_b_b_in, m_b_ln_g, m_b_ln_b, m_b_w_s, m_b_b_s, m_b_w_out, m_c_w_in, m_c_w_conv, m_c_w_out, m_ln1_g, m_ln1_b, m_ln2_g, m_ln2_b, m_ffn_w_gate, m_ffn_w_up, m_ffn_w_down, m_ple_w_gate, m_ple_w_proj, m_ple_norm_g, v_a_w_pw1, v_a_b_pw1, v_a_w_dw, v_a_b_dw, v_a_ln_g, v_a_ln_b, v_a_w_pw2, v_b_w_in, v_b_b_in, v_b_ln_g, v_b_ln_b, v_b_w_s, v_b_b_s, v_b_w_out, v_c_w_in, v_c_w_conv, v_c_w_out, v_ln1_g, v_ln1_b, v_ln2_g, v_ln2_b, v_ffn_w_gate, v_ffn_w_up, v_ffn_w_down, v_ple_w_gate, v_ple_w_proj, v_ple_norm_g):
    args = dict(locals())
    wts = {k: args[k] for k in WEIGHTS}
    mom = {k: args["m_" + k] for k in WEIGHTS}
    var = {k: args["v_" + k] for k in WEIGHTS}
    for k in TRANSPOSED:
        wts[k], mom[k], var[k] = (jnp.transpose(t[k], (0, 2, 1)) for t in (wts, mom, var))
    q_idx = 2 * lax.axis_index("x") + lax.axis_index("y")
    c_idx = lax.axis_index("c").astype(jnp.int32).reshape(1)

    wb = {k: _cast_bf16(wts[k]) for k in BIG if k not in ("ffn_w_gate", "ffn_w_up")}
    mixw = [[("a_w_pw1", 0), ("a_w_pw2", 0)], [("b_w_in", 0), ("b_w_out", 0)], [("c_w_in", 0), ("c_w_out", 0)],
            [("a_w_pw1", 1), ("a_w_pw2", 1)]]
    ffnw = [[("ffn_w_gate", l), ("ffn_w_up", l), ("ffn_w_down", l)] for l in range(DEPTH)]
    plew = [[("ple_w_gate", l), ("ple_w_proj", l)] for l in range(DEPTH)]
    fwd_plan = {("a1", 0): mixw[0][1:] + plew[0], ("a2", 0): ffnw[0], ("ffn", 0): mixw[1] + plew[1],
                ("b", 1): ffnw[1], ("ffn", 1): mixw[2] + plew[2] + ffnw[2][:1],
                ("c1", 2): ffnw[2][1:2], ("c2", 2): ffnw[2][2:], ("ffn", 2): mixw[3] + plew[3] + ffnw[3][:1],
                ("a2", 3): ffnw[3][1:]}
    gw = {}

    def gather(keys):
        return _GatherWeights([(wb[name], l) for name, l in keys])

    def hosted(tag, fn, *fargs):
        keys = fwd_plan.get(tag)
        if not keys:
            return fn(*fargs)
        own, (got,) = fn(*fargs, hosts=[gather(keys)])
        store(keys, got)
        return own

    def store(keys, got):
        for (name, l), arr in zip(keys, got):
            gw[name, l] = arr.reshape(NQ * arr.shape[2], arr.shape[3]) if name in ROW_SHARDED else arr

    first_keys = mixw[0][:1]
    wb["ffn_w_gate"], (got,) = _cast_bf16(wts["ffn_w_gate"], hosts=[gather(first_keys)])
    store(first_keys, got)
    shard_shapes = [wts[k].shape for k in SMALL_SHARDED]
    wb["ffn_w_up"], ((small8,),) = _cast_bf16(wts["ffn_w_up"], hosts=[_Gather8(_pack([wts[k] for k in SMALL_SHARDED]))])
    per_chip = [_unpack(small8[2 * qq], shard_shapes) for qq in range(NQ)]
    full = {k: jnp.concatenate([per_chip[qq][i] for qq in range(NQ)], axis=-1) for i, k in enumerate(SMALL_SHARDED)}
    for k in SMALL_REPL:
        full[k] = wts[k]

    def vec(name, l):
        return full[name][l][None, :]

    def conv_w(name, l, rows):
        w = full[name][l]
        return jnp.pad(w, ((0, rows - w.shape[0]), (0, 0)))

    ws = full["b_w_s"][0]
    wst = jnp.transpose(ws, (0, 2, 1))
    bsx = jnp.broadcast_to(full["b_b_s"][0][:, :, None], (SGU_H, SGU_T, SGU_G))

    x0s, z1s, z2s, saved, ffn_saved = [], [], [], [], []
    cur = x[0]
    for i in range(DEPTH):
        mix, j = i % 3, i // 3
        x0s.append(cur)
        if mix == 0:
            h, glu = hosted(("a1", i), _fwd_a1, cur, gw["a_w_pw1", j], vec("a_b_pw1", j), i)
            z1, cv = hosted(("a2", i), _fwd_a2, glu, cur, conv_w("a_w_dw", j, 32), vec("a_b_dw", j), vec("a_ln_g", j),
                            vec("a_ln_b", j), gw["a_w_pw2", j], i)
            saved.append((h, glu, cv))
        elif mix == 1:
            z1, zg, gg = hosted(("b", i), _fwd_b, cur, gw["b_w_in", 0], vec("b_b_in", 0), vec("b_ln_g", 0),
                                vec("b_ln_b", 0), ws, bsx, gw["b_w_out", 0])
            saved.append((zg, gg))
        else:
            hc = hosted(("c1", i), _fwd_c1, cur, gw["c_w_in", 0])
            z1 = hosted(("c2", i), _fwd_c2, hc, cur, conv_w("c_w_conv", 0, 8), gw["c_w_out", 0])
            saved.append((hc,))
        z2, ab, ub, hm = hosted(("ffn", i), _fwd_ffn, z1, vec("ln1_g", i), vec("ln1_b", i), gw["ffn_w_gate", i],
                                gw["ffn_w_up", i], gw["ffn_w_down", i], i)
        ffn_saved.append((ab, ub, hm))
        cur = hosted(("ple", i), _fwd_ple, z2, p[i, 0], vec("ln2_g", i), vec("ln2_b", i), gw["ple_w_gate", i],
                     gw["ple_w_proj", i], vec("ple_norm_g", i), i)
        z1s.append(z1)
        z2s.append(z2)

    g, loss_acc = _loss_head(cur, loss_target[0])
    loss = lax.psum(0.5 / D * jnp.sum(loss_acc[0]), ("x", "y", "c"))

    dws = {}
    sg = {}
    res = {k: None for k in BIG}

    def wgrad(name, l, a, amode, b, bmode, scatter_keys=()):
        _, k, n = wts[name].shape
        hosts = [_ScatterPartials([parts[key] for key in scatter_keys])] if scatter_keys else ()
        if name in ROW_SHARDED:
            out = _mm_tn(f"dw_{name}_{l}", a, "1", b, "1", NQ * k, n, groups=1, hosts=hosts)
        else:
            out = _mm_tn(f"dw_{name}_{l}", a, amode, b, bmode, k, n, hosts=hosts)
        if scatter_keys:
            out, (contribs,) = out
            update(scatter_keys, contribs)
        dws[name, l] = out.reshape(1, NQ, k, n)

    def swap(keys):
        return _SwapHalves([dws[k] for k in keys])

    parts = {}

    def add_halves(keys, got):
        parts.update((k, _add_halves(dws[k], r, c_idx)) for k, r in zip(keys, got))

    def update(keys, contribs):
        for (name, l), gc in zip(keys, contribs):
            _, kq, n = wts[name].shape
            res[name] = _adam(f"adam_{name}_{l}", wts[name], mom[name], var[name], gc.reshape(NQ, 1, kq, n), l,
                              res[name])

    small = SMALL_SHARDED + SMALL_REPL
    late_small = [("a_b_pw1", 0)]
    early_small = [(k, l) for k in small for l in range(full[k].shape[0]) if (k, l) not in late_small]
    pending = None
    for i in reversed(range(DEPTH)):
        mix, j = i % 3, i // 3
        ple_args = (g, z2s[i], p[i, 0], vec("ln2_g", i), vec("ln2_b", i), gw["ple_w_gate", i], gw["ple_w_proj", i],
                    vec("ple_norm_g", i), i)
        if pending:
            (dz2, x2b, dgp, dqp, acc), (got,) = _bwd_ple(*ple_args, hosts=[swap(pending)])
            add_halves(pending, got)
        else:
            dz2, x2b, dgp, dqp, acc = _bwd_ple(*ple_args)
        sg["ple_norm_g", i], sg["ln2_g", i], sg["ln2_b", i] = acc[0], acc[1], acc[2]
        wgrad("ple_w_gate", i, x2b, "c", dgp, "1")
        wgrad("ple_w_proj", i, p[i, 0], "1", dqp, "c")
        ab, ub, hm = ffn_saved[i]
        ffn_args = (dz2, z1s[i], ab, ub, vec("ln1_g", i), vec("ln1_b", i), gw["ffn_w_gate", i], gw["ffn_w_up", i],
                    gw["ffn_w_down", i], i)
        if pending:
            (dz1, x1b, da, du, acc), (contribs,) = _bwd_ffn(
                *ffn_args, hosts=[_ScatterPartials([parts[key] for key in ffnw[i + 1]])])
            update(ffnw[i + 1], contribs)
        else:
            dz1, x1b, da, du, acc = _bwd_ffn(*ffn_args)
        sg["ln1_g", i], sg["ln1_b", i] = acc[0], acc[1]
        behind_mixer = mix == 1
        rest = mixw[i + 1] + plew[i + 1] if pending else []
        wgrad("ffn_w_gate", i, da, "1", x1b, "1", scatter_keys=() if behind_mixer else rest[1:])
        wgrad("ffn_w_up", i, du, "1", x1b, "1")
        wgrad("ffn_w_down", i, hm, "1", dz2, "1", scatter_keys=() if behind_mixer else rest[:1])
        x0 = x0s[i]
        if mix == 0:
            h, glu, cv = saved[i]
            a2_args = (dz1, cv, vec("a_ln_g", j), vec("a_ln_b", j), gw["a_w_pw2", j], i)
            conv_args = (glu, conv_w("a_w_dw", j, 32), i)
            if i == 0:
                early = ffnw[0] + plew[0]
                (dcv, sb, acc), (got,) = _bwd_a2(*a2_args, hosts=[swap(early)])
                sg["a_ln_g", j], sg["a_ln_b", j], sg["a_b_dw", j] = acc[0], acc[1], acc[2]
                add_halves(early, got)
                wgrad("a_w_pw2", j, sb, "c", dz1, "1")
                (dglu, dwdw), (contribs, got) = _bwd_conv_a(
                    dcv, *conv_args, hosts=[_ScatterPartials([parts[key] for key in early]), swap(mixw[0][1:])])
                update(early, contribs)
                add_halves(mixw[0][1:], got)
                sg["a_w_dw", j] = dwdw[:CONV_A]
                (g, dh, acc), (contribs, (g8_early,)) = _bwd_a1(
                    dglu, h, dz1, gw["a_w_pw1", j], i,
                    hosts=[_ScatterPartials([parts[key] for key in mixw[0][1:]]),
                           _Gather8(_pack([sg[pc] for pc in early_small]))])
                update(mixw[0][1:], contribs)
            else:
                dcv, sb, acc = _bwd_a2(*a2_args)
                sg["a_ln_g", j], sg["a_ln_b", j], sg["a_b_dw", j] = acc[0], acc[1], acc[2]
                dglu, dwdw = _bwd_conv_a(dcv, *conv_args)
                wgrad("a_w_pw2", j, sb, "c", dz1, "1")
                sg["a_w_dw", j] = dwdw[:CONV_A]
                g, dh, acc = _bwd_a1(dglu, h, dz1, gw["a_w_pw1", j], i)
            sg["a_b_pw1", j] = acc[0]
            wgrad("a_w_pw1", j, x0, "1", dh, "c")
        elif mix == 1:
            zg, gg = saved[i]
            (g, dh, mb, acc, dw_s, db_s), (contribs,) = _bwd_b(
                dz1, zg, gg, vec("b_ln_g", 0), vec("b_ln_b", 0), gw["b_w_in", 0], gw["b_w_out", 0], ws, wst, bsx,
                hosts=[_ScatterPartials([parts[key] for key in rest])])
            update(rest, contribs)
            sg["b_b_in", 0], sg["b_ln_g", 0], sg["b_ln_b", 0] = acc[0], acc[1, :E], acc[2, :E]
            sg["b_w_s", 0], sg["b_b_s", 0] = dw_s, jnp.sum(db_s, axis=-1)
            wgrad("b_w_out", 0, mb, "c", dz1, "1")
            wgrad("b_w_in", 0, x0, "1", dh, "c")
        else:
            (hc,) = saved[i]
            wc = conv_w("c_w_conv", 0, 8)
            dy, dbg, mb = _bwd_c2(dz1, hc, wc, gw["c_w_out", 0])
            wgrad("c_w_out", 0, mb, "c", dz1, "1")
            g, dhc, dwc = _bwd_c1(dy, hc, dbg, dz1, wc, gw["c_w_in", 0])
            sg["c_w_conv", 0] = dwc[:CONV_C]
            wgrad("c_w_in", 0, x0, "1", dhc, "c")
        pending = mixw[i] + ffnw[i] + plew[i] if i > 0 else mixw[0][:1]
    grad_x = g[None]
    add_halves(pending, _comm_only("swap_last", swap(pending)))
    update(pending, _comm_only("scatter_last", _ScatterPartials([parts[key] for key in pending])))

    g8_late = _gather8("gather_small_late", _pack([sg[pc] for pc in late_small]))
    sums = dict(zip(early_small, _unpack(_sum8("sum8_early", g8_early), [sg[pc].shape for pc in early_small])))
    sums.update(zip(late_small, _unpack(_sum8("sum8_late", g8_late), [sg[pc].shape for pc in late_small])))
    gsum = [jnp.stack([sums[k, l] for l in range(full[k].shape[0])]) for k in small]
    gmine = []
    for k, gs in zip(small, gsum):
        if k in SMALL_SHARDED:
            wdt = wts[k].shape[-1]
            gs = lax.dynamic_slice_in_dim(gs, q_idx * wdt, wdt, axis=gs.ndim - 1)
        gmine.append(gs)
    packed = [_pack(t)[None] for t in ([wts[k] for k in small], [mom[k] for k in small], [var[k] for k in small])]
    outs = _adam("adam_small", packed[0], packed[1], packed[2], _pack(gmine)[None, None], 0, None)
    unpacked = [_unpack(o[0], [wts[k].shape for k in small]) for o in outs]
    for i, k in enumerate(small):
        res[k] = tuple(u[i] for u in unpacked)

    for k in TRANSPOSED:
        res[k] = tuple(jnp.transpose(r, (0, 2, 1)) for r in res[k])
    return (loss, grad_x, *[res[k][0] for k in WEIGHTS], *[res[k][1] for k in WEIGHTS],
            *[res[k][2] for k in WEIGHTS], *[res[k][3] for k in WEIGHTS])
```

```python
import functools

import jax
import jax.numpy as jnp
from jax import lax
from jax.experimental import pallas as pl
from jax.experimental.pallas import tpu as pltpu

F32, BF16 = jnp.float32, jnp.bfloat16
S = 4096
D = 1024
E = 2048
FF = 2816
FQ = FF // 4
NQ = 4
DEPTH = 4
ALPHA = (2 * DEPTH) ** 0.25
LN_EPS = 1e-5
CONV_A, CONV_C = 31, 3
HALO_A, HALO_C = 32, 8
SGU_T, SGU_H, SGU_G, SGU_CHUNK = 128, 8, 256, 64
VMEM_LIMIT = 56 * 1024 * 1024
DW_BLOCK_BUDGET = 40 * 1024 * 1024
MESH = pl.DeviceIdType.MESH
ADAM_LR, ADAM_B1, ADAM_B2, ADAM_EPS, ADAM_WD, ADAM_STEP = 0.001, 0.9, 0.999, 1e-08, 0.01, 10
GELU_C, GELU_A = 0.7978845608028654, 0.044715

BIG = ["a_w_pw1", "a_w_pw2", "b_w_in", "b_w_out", "c_w_in", "c_w_out",
       "ffn_w_gate", "ffn_w_up", "ffn_w_down", "ple_w_gate", "ple_w_proj"]
TRANSPOSED = ["ffn_w_gate", "ffn_w_up"]
ROW_SHARDED = ["a_w_pw2", "b_w_out", "c_w_out", "ffn_w_gate", "ffn_w_up", "ffn_w_down", "ple_w_gate"]
SMALL_SHARDED = ["a_b_pw1", "a_w_dw", "a_b_dw", "a_ln_g", "a_ln_b", "c_w_conv"]
SMALL_REPL = ["b_b_in", "b_ln_g", "b_ln_b", "b_w_s", "b_b_s", "ln1_g", "ln1_b", "ln2_g", "ln2_b", "ple_norm_g"]
WEIGHTS = ["a_w_pw1", "a_b_pw1", "a_w_dw", "a_b_dw", "a_ln_g", "a_ln_b", "a_w_pw2", "b_w_in", "b_b_in", "b_ln_g",
           "b_ln_b", "b_w_s", "b_b_s", "b_w_out", "c_w_in", "c_w_conv", "c_w_out", "ln1_g", "ln1_b", "ln2_g",
           "ln2_b", "ffn_w_gate", "ffn_w_up", "ffn_w_down", "ple_w_gate", "ple_w_proj", "ple_norm_g"]


def _call(name, body, grid, in_specs, out_specs, out_shape, scratch=(), aliases=None, hosts=()):
    params = pltpu.CompilerParams(dimension_semantics=("arbitrary",) * len(grid), vmem_limit_bytes=VMEM_LIMIT)
    if not hosts:
        return pl.pallas_call(
            body, name=name, grid=grid, in_specs=in_specs, out_specs=out_specs, out_shape=out_shape,
            scratch_shapes=list(scratch), input_output_aliases=aliases or {}, compiler_params=params)
    assert len(grid) == 1 and not aliases
    single = not isinstance(out_shape, (list, tuple))
    own_shapes = [out_shape] if single else list(out_shape)
    own_specs = [out_specs] if single else list(out_specs)
    n_in, n_out, n_scr = len(in_specs), len(own_shapes), len(scratch)
    h_in = [len(h.arrays) for h in hosts]
    h_out = [len(h.out_shapes) for h in hosts]
    h_sem = [len(h.sems) for h in hosts]

    def split(refs, counts):
        out, off = [], 0
        for cnt in counts:
            out.append(refs[off:off + cnt])
            off += cnt
        return out

    def wrapped(*refs):
        ins, hin, outs, hout, scr, hsem = split(refs, [n_in, sum(h_in), n_out, sum(h_out), n_scr, sum(h_sem)])
        per_host = list(zip(hosts, split(hin, h_in), split(hout, h_out), split(hsem, h_sem)))

        @pl.when(pl.program_id(0) == 0)
        def _():
            for h, a, o, s in per_host:
                h.start(a, o, s)

        body(*ins, *outs, *scr)

        for h, a, o, s in per_host:
            for step, ws in sorted(h.forward_steps(grid[0]).items()):
                pl.when(pl.program_id(0) == step)(functools.partial(h.forward, ws, a, o, s))

        @pl.when(pl.program_id(0) == grid[0] - 1)
        def _():
            for h, a, o, s in per_host:
                h.complete(a, o, s)

    any_spec = pl.BlockSpec(memory_space=pl.ANY)
    call = pl.pallas_call(
        wrapped, name=name, grid=grid, in_specs=list(in_specs) + [any_spec] * sum(h_in),
        out_specs=own_specs + [any_spec] * sum(h_out),
        out_shape=own_shapes + [s for h in hosts for s in h.out_shapes],
        scratch_shapes=list(scratch) + [s for h in hosts for s in h.sems], compiler_params=params)

    def run(*args):
        res = call(*args, *[a for h in hosts for a in h.arrays])
        own = res[0] if single else list(res[:n_out])
        return own, split(list(res[n_out:]), h_out)

    return run


def _sds(shape, dtype=F32):
    return jax.ShapeDtypeStruct(shape, dtype)


def _row(tm, c):
    return pl.BlockSpec((tm, c), lambda i: (i, 0))


def _grow(g, tm, c):
    return pl.BlockSpec((g, tm, c), lambda i: (0, i, 0))


def _const(shape):
    nd = len(shape)
    return pl.BlockSpec(shape, lambda i: (0,) * nd, pipeline_mode=pl.Buffered(1))


def _wspec(w):
    return pl.BlockSpec((NQ, None, w.shape[2], w.shape[3]), lambda i: (0, 0, 0, 0), pipeline_mode=pl.Buffered(1))


def _prev(tm, hb, c):
    return pl.BlockSpec((hb, c), lambda i: (jnp.maximum(i * (tm // hb) - 1, 0), 0))


def _next(tm, hb, c):
    return pl.BlockSpec((hb, c), lambda i: (jnp.minimum((i + 1) * (tm // hb), S // hb - 1), 0))


def _acc(r, c):
    return pl.BlockSpec((r, c), lambda i: (0, 0))


def _sig(x):
    return 1.0 / (1.0 + jnp.exp(-x))


def _ln(z, g, b):
    mu = jnp.mean(z, axis=-1, keepdims=True)
    zc = z - mu
    rstd = lax.rsqrt(jnp.mean(zc * zc, axis=-1, keepdims=True) + LN_EPS)
    xhat = zc * rstd
    return xhat * g + b, xhat, rstd


def _ln_bwd(dyg, xhat, rstd):
    return rstd * (dyg - jnp.mean(dyg, axis=-1, keepdims=True) - xhat * jnp.mean(dyg * xhat, axis=-1, keepdims=True))


def _mm(a, w):
    return jnp.dot(a.astype(BF16), w, preferred_element_type=F32)


def _mmt(a, w):
    return lax.dot_general(a.astype(BF16), w, (((1,), (1,)), ((), ())), preferred_element_type=F32)


def _colsum(x):
    return jnp.sum(x, axis=0, keepdims=True)


def _gelu_and_grad(x):
    x2 = x * x
    t = jnp.tanh(x * (GELU_C + (GELU_C * GELU_A) * x2))
    hx = 0.5 * x
    return hx + hx * t, 0.5 + 0.5 * t + hx * (1.0 - t * t) * (GELU_C + (3.0 * GELU_C * GELU_A) * x2)


def _silu_grad(a, sg):
    return sg * (1.0 + a * (1.0 - sg))


def _sgu_masks():
    r = lax.broadcasted_iota(jnp.int32, (SGU_T, SGU_T), 0) // SGU_CHUNK
    c = lax.broadcasted_iota(jnp.int32, (SGU_T, SGU_T), 1) // SGU_CHUNK
    return r >= c, c >= r


def _fill_halo(buf, lo, n, halo_val_fn, is_edge):
    @pl.when(is_edge)
    def _():
        buf[lo:lo + n, :] = jnp.zeros((n, buf.shape[1]), F32)

    @pl.when(jnp.logical_not(is_edge))
    def _():
        buf[lo:lo + n, :] = halo_val_fn()


SUB, LANE = 8, 128
ROWS_AT_ONCE = 16


def _shift_copies(buf, sh):
    rows = sh.shape[1]
    for s in range(1, SUB):
        sh[s - 1, :, :] = buf[pl.ds(s, rows), :]


def _tiles(buf, sh, s, first, count, group0, lanes):
    src = buf if s == 0 else sh.at[s - 1]
    return {t: src[pl.ds(pl.multiple_of((group0 + t) * SUB, SUB), SUB), lanes] for t in range(first, first + count)}


def _by_shift(offsets):
    out = []
    for s in range(SUB):
        taps = [(k, o // SUB) for k, o in enumerate(offsets) if o % SUB == s]
        if taps:
            out.append((s, taps))
    return out


def _conv_rows(out_ref, w_ref, bias_ref, offsets, buf, sh, tm):
    n = ROWS_AT_ONCE
    for cb in range(D // LANE):
        lanes = slice(cb * LANE, (cb + 1) * LANE)
        bias = None if bias_ref is None else jnp.broadcast_to(bias_ref[:, lanes], (SUB, LANE))

        def body(jb, carry):
            accs = [bias] * n
            for s, taps in _by_shift(offsets):
                ms = [m for _, m in taps]
                tiles = _tiles(buf, sh, s, min(ms), max(ms) - min(ms) + n, jb * n, lanes)
                for k, m in taps:
                    wk = jnp.broadcast_to(w_ref[k:k + 1, lanes], (SUB, LANE))
                    for jj in range(n):
                        t = wk * tiles[m + jj]
                        accs[jj] = t if accs[jj] is None else accs[jj] + t
            for jj in range(n):
                out_ref[pl.ds(pl.multiple_of((jb * n + jj) * SUB, SUB), SUB), lanes] = accs[jj]
            return carry

        lax.fori_loop(0, tm // (SUB * n), body, 0)


def _conv_wgrad(dw_ref, d_ref, offsets, buf, sh, tm):
    n = 4
    for cb in range(D // LANE):
        lanes = slice(cb * LANE, (cb + 1) * LANE)

        def body(jq, accs):
            accs = list(accs)
            d = [d_ref[pl.ds(pl.multiple_of((jq * n + jj) * SUB, SUB), SUB), lanes] for jj in range(n)]
            for s, taps in _by_shift(offsets):
                ms = [m for _, m in taps]
                tiles = _tiles(buf, sh, s, min(ms), max(ms) - min(ms) + n, jq * n, lanes)
                for k, m in taps:
                    for jj in range(n):
                        accs[k] = accs[k] + d[jj] * tiles[m + jj]
            return tuple(accs)

        accs = lax.fori_loop(0, tm // (SUB * n), body, tuple(jnp.zeros((SUB, LANE), F32) for _ in offsets))
        for k, acc in enumerate(accs):
            dw_ref[k:k + 1, lanes] += jnp.sum(acc, axis=0, keepdims=True)


def _fwd_a1(x0, w1, b1, l, hosts=()):
    tm = 512

    def body(x_ref, w_ref, b_ref, h_ref, glu_ref):
        xb = x_ref[...].astype(BF16)
        for q in range(NQ):
            sl = slice(q * 512, (q + 1) * 512)
            h_ref[:, sl] = jnp.dot(xb, w_ref[q], preferred_element_type=F32) + b_ref[:, sl]
        glu_ref[...] = h_ref[:, :D] * _sig(h_ref[:, D:])

    return _call(f"fwd_a1_{l}", body, (S // tm,), [_row(tm, D), _wspec(w1), _const((1, 2 * D))],
                 [_row(tm, 2 * D), _row(tm, D)], [_sds((S, 2 * D)), _sds((S, D))], hosts=hosts)(x0, w1, b1)


def _fwd_a2(glu, x0, wdw, bdw, lg, lb, w2, l, hosts=()):
    tm = 256

    def body(g_ref, gp_ref, x_ref, wdw_ref, bdw_ref, lg_ref, lb_ref, w2_ref, z_ref, cv_ref, buf, sh):
        i = pl.program_id(0)
        _fill_halo(buf, 0, HALO_A, lambda: gp_ref[...], i == 0)
        buf[HALO_A:HALO_A + tm, :] = g_ref[...]
        _shift_copies(buf, sh)
        _conv_rows(cv_ref, wdw_ref, bdw_ref, [HALO_A - (CONV_A - 1) + k for k in range(CONV_A)], buf, sh, tm)
        n, _, _ = _ln(cv_ref[...], lg_ref[...], lb_ref[...])
        sb = (n * _sig(n)).astype(BF16)
        z_ref[...] = ALPHA * x_ref[...] + jnp.dot(sb, w2_ref[...], preferred_element_type=F32)

    return _call(f"fwd_a2_{l}", body, (S // tm,),
                 [_row(tm, D), _prev(tm, HALO_A, D), _row(tm, D), _const((32, D)), _const((1, D)), _const((1, D)),
                  _const((1, D)), _const(w2.shape)],
                 [_row(tm, D), _row(tm, D)], [_sds((S, D)), _sds((S, D))],
                 scratch=[pltpu.VMEM((HALO_A + tm, D), F32), pltpu.VMEM((SUB - 1, HALO_A + tm - SUB, D), F32)],
                 hosts=hosts)(glu, glu, x0, wdw, bdw, lg, lb, w2)


def _fwd_b(x0, win, b_in, lg, lb, ws, bsx, wout, hosts=()):
    tm = 256

    def body(x_ref, win_ref, bin_ref, lg_ref, lb_ref, ws_ref, bsx_ref, wout_ref, z_ref, zg_ref, gg_ref, f_scr, h_ref):
        xb = x_ref[...].astype(BF16)
        for q in range(NQ):
            sl = slice(q * 1024, (q + 1) * 1024)
            h_ref[:, sl] = jnp.dot(xb, win_ref[q], preferred_element_type=F32) + bin_ref[:, sl]
        u, du = _gelu_and_grad(h_ref[:, :E])
        v, dv = _gelu_and_grad(h_ref[:, E:])
        zg_ref[:, 0:E] = u.astype(BF16)
        zg_ref[:, E:2 * E] = v.astype(BF16)
        gg_ref[:, 0:E] = du.astype(BF16)
        gg_ref[:, E:2 * E] = dv.astype(BF16)
        vn, _, _ = _ln(v, lg_ref[...], lb_ref[...])
        vnb = vn.astype(BF16)
        mask, _ = _sgu_masks()
        for hd in range(SGU_H):
            wm = jnp.where(mask, ws_ref[hd], 0.0).astype(BF16)
            cs = slice(hd * SGU_G, (hd + 1) * SGU_G)
            for n in range(tm // SGU_T):
                rs = slice(n * SGU_T, (n + 1) * SGU_T)
                f_scr[rs, cs] = jnp.dot(wm, vnb[rs, cs], preferred_element_type=F32) + bsx_ref[hd]
        mb = (u * f_scr[...]).astype(BF16)
        z_ref[...] = ALPHA * x_ref[...] + jnp.dot(mb, wout_ref[...], preferred_element_type=F32)

    return _call("fwd_b", body, (S // tm,),
                 [_row(tm, D), _wspec(win), _const((1, 2 * E)), _const((1, E)), _const((1, E)),
                  _const((SGU_H, SGU_T, SGU_T)), _const((SGU_H, SGU_T, SGU_G)), _const(wout.shape)],
                 [_row(tm, D), _row(tm, 2 * E), _row(tm, 2 * E)],
                 [_sds((S, D)), _sds((S, 2 * E), BF16), _sds((S, 2 * E), BF16)],
                 scratch=[pltpu.VMEM((tm, E), F32), pltpu.VMEM((tm, 2 * E), F32)], hosts=hosts
                 )(x0, win, b_in, lg, lb, ws, bsx, wout)


def _fwd_c1(x0, win, hosts=()):
    tm = 512

    def body(x_ref, w_ref, hc_ref):
        xb = x_ref[...].astype(BF16)
        for q in range(NQ):
            hc_ref[:, q * 768:(q + 1) * 768] = jnp.dot(xb, w_ref[q], preferred_element_type=F32)

    return _call("fwd_c1", body, (S // tm,), [_row(tm, D), _wspec(win)], _row(tm, 3 * D),
                 _sds((S, 3 * D)), hosts=hosts)(x0, win)


def _short_conv(buf, hc_ref, hcp_ref, wc_ref, tm, i):
    _fill_halo(buf, 0, HALO_C, lambda: hcp_ref[:, D:2 * D] * hcp_ref[:, 2 * D:], i == 0)
    buf[HALO_C:HALO_C + tm, :] = hc_ref[:, D:2 * D] * hc_ref[:, 2 * D:]
    y = wc_ref[0:1, :] * buf[pl.ds(HALO_C - 2, tm), :]
    for k in range(1, CONV_C):
        y = y + wc_ref[k:k + 1, :] * buf[pl.ds(HALO_C - 2 + k, tm), :]
    return y


def _fwd_c2(hc, x0, wc, wout, hosts=()):
    tm = 256

    def body(hc_ref, hcp_ref, x_ref, wc_ref, wout_ref, z_ref, buf):
        y = _short_conv(buf, hc_ref, hcp_ref, wc_ref, tm, pl.program_id(0))
        mb = (hc_ref[:, :D] * y).astype(BF16)
        z_ref[...] = ALPHA * x_ref[...] + jnp.dot(mb, wout_ref[...], preferred_element_type=F32)

    return _call("fwd_c2", body, (S // tm,),
                 [_row(tm, 3 * D), _prev(tm, HALO_C, 3 * D), _row(tm, D), _const((8, D)), _const(wout.shape)],
                 _row(tm, D), _sds((S, D)), scratch=[pltpu.VMEM((HALO_C + tm, D), F32)], hosts=hosts
                 )(hc, hc, x0, wc, wout)


def _fwd_ffn(z1, lg, lb, wgt, wut, wd, l, hosts=()):
    tm = 256

    def body(z_ref, lg_ref, lb_ref, wg_ref, wu_ref, wd_ref, o_ref, a_ref, u_ref, hm_ref):
        x1, _, _ = _ln(z_ref[...], lg_ref[...], lb_ref[...])
        xb = x1.astype(BF16)
        a = _mmt(xb, wg_ref[...])
        u = _mmt(xb, wu_ref[...])
        hmb = (a * _sig(a) * u).astype(BF16)
        a_ref[...] = a.astype(BF16)
        u_ref[...] = u.astype(BF16)
        hm_ref[...] = hmb
        o_ref[...] = ALPHA * x1 + jnp.dot(hmb, wd_ref[...], preferred_element_type=F32)

    return _call(f"fwd_ffn_{l}", body, (S // tm,),
                 [_row(tm, D), _const((1, D)), _const((1, D)), _const((FF, D)), _const((FF, D)), _const((FF, D))],
                 [_row(tm, D), _row(tm, FF), _row(tm, FF), _row(tm, FF)],
                 [_sds((S, D)), _sds((S, FF), BF16), _sds((S, FF), BF16), _sds((S, FF), BF16)],
                 hosts=hosts)(z1, lg, lb, wgt, wut, wd)


def _ple_parts(z2, p, lg, lb, wg_ref, wp_ref, pg):
    x2, xhat, rstd = _ln(z2, lg, lb)
    xb = x2.astype(BF16)
    gate = _sig(jnp.dot(xb, wg_ref[...], preferred_element_type=F32))
    pb = p.astype(BF16)
    qp = jnp.concatenate([jnp.dot(pb, wp_ref[q], preferred_element_type=F32) for q in range(NQ)], axis=1)
    rs = lax.rsqrt(jnp.mean(qp * qp, axis=-1, keepdims=True) + LN_EPS)
    qn = qp * rs
    return x2, xhat, rstd, xb, gate, qn, rs, qn * pg


def _fwd_ple(z2, p, lg, lb, wg, wp, pg, l, hosts=()):
    tm = 512

    def body(z_ref, p_ref, lg_ref, lb_ref, wg_ref, wp_ref, pg_ref, o_ref):
        x2, _, _, _, gate, _, _, r = _ple_parts(z_ref[...], p_ref[...], lg_ref[...], lb_ref[...], wg_ref, wp_ref,
                                                pg_ref[...])
        o_ref[...] = x2 + gate * r

    return _call(f"fwd_ple_{l}", body, (S // tm,),
                 [_row(tm, D), _row(tm, 256), _const((1, D)), _const((1, D)), _const(wg.shape), _wspec(wp),
                  _const((1, D))],
                 _row(tm, D), _sds((S, D)), hosts=hosts)(z2, p, lg, lb, wg, wp, pg)


def _loss_head(y, target):
    tm = 512

    def body(y_ref, t_ref, dy_ref, acc_ref):
        @pl.when(pl.program_id(0) == 0)
        def _():
            acc_ref[...] = jnp.zeros_like(acc_ref)

        e = y_ref[...] - t_ref[...]
        dy_ref[...] = e * (1.0 / D)
        acc_ref[0:1, :] += _colsum(e * e)

    return _call("loss_head", body, (S // tm,), [_row(tm, D), _row(tm, D)], [_row(tm, D), _acc(8, D)],
                 [_sds((S, D)), _sds((8, D))])(y, target)


def _zero_first(*refs):
    @pl.when(pl.program_id(0) == 0)
    def _():
        for r in refs:
            r[...] = jnp.zeros_like(r)


def _bwd_ple(g, z2, p, lg, lb, wg, wp, pg, l, hosts=()):
    tm = 512

    def body(g_ref, z_ref, p_ref, lg_ref, lb_ref, wg_ref, wp_ref, pg_ref, dz_ref, xb_ref, dgp_ref, dqp_ref, acc_ref):
        _zero_first(acc_ref)
        gin = g_ref[...]
        lgv, pgv = lg_ref[...], pg_ref[...]
        _, xhat, rstd, xb, gate, qn, rs, r = _ple_parts(z_ref[...], p_ref[...], lgv, lb_ref[...], wg_ref, wp_ref, pgv)
        xb_ref[...] = xb
        dgpb = (gin * r * gate * (1.0 - gate)).astype(BF16)
        dgp_ref[...] = dgpb
        dx2 = gin + _mmt(dgpb, wg_ref[...])
        dr = gin * gate
        acc_ref[0:1, :] += _colsum(dr * qn)
        t = dr * pgv
        dqp_ref[...] = (rs * (t - qn * jnp.mean(t * qn, axis=-1, keepdims=True))).astype(BF16)
        acc_ref[1:2, :] += _colsum(dx2 * xhat)
        acc_ref[2:3, :] += _colsum(dx2)
        dz_ref[...] = _ln_bwd(dx2 * lgv, xhat, rstd)

    return _call(f"bwd_ple_{l}", body, (S // tm,),
                 [_row(tm, D), _row(tm, D), _row(tm, 256), _const((1, D)), _const((1, D)), _const(wg.shape),
                  _wspec(wp), _const((1, D))],
                 [_row(tm, D), _row(tm, D), _row(tm, D), _row(tm, D), _acc(8, D)],
                 [_sds((S, D)), _sds((S, D), BF16), _sds((S, D), BF16), _sds((S, D), BF16), _sds((8, D))],
                 hosts=hosts)(g, z2, p, lg, lb, wg, wp, pg)


def _bwd_ffn(dz2, z1, ab, ub, lg, lb, wgt, wut, wd, l, hosts=()):
    tm = 256

    def body(dz2_ref, z_ref, a_ref, u_ref, lg_ref, lb_ref, wg_ref, wu_ref, wd_ref, dz1_ref, xb_ref, da_ref, du_ref,
             acc_ref):
        _zero_first(acc_ref)
        dz2v = dz2_ref[...]
        lgv = lg_ref[...]
        x1, xhat, rstd = _ln(z_ref[...], lgv, lb_ref[...])
        xb_ref[...] = x1.astype(BF16)
        a = a_ref[...].astype(F32)
        u = u_ref[...].astype(F32)
        sg = _sig(a)
        dhm = _mmt(dz2v, wd_ref[...])
        dub = (dhm * (a * sg)).astype(BF16)
        dab = (dhm * u * _silu_grad(a, sg)).astype(BF16)
        da_ref[...] = dab
        du_ref[...] = dub
        dx1 = ALPHA * dz2v + _mm(dab, wg_ref[...]) + _mm(dub, wu_ref[...])
        acc_ref[0:1, :] += _colsum(dx1 * xhat)
        acc_ref[1:2, :] += _colsum(dx1)
        dz1_ref[...] = _ln_bwd(dx1 * lgv, xhat, rstd)

    return _call(f"bwd_ffn_{l}", body, (S // tm,),
                 [_row(tm, D), _row(tm, D), _row(tm, FF), _row(tm, FF), _const((1, D)), _const((1, D)),
                  _const((FF, D)), _const((FF, D)), _const((FF, D))],
                 [_row(tm, D), _row(tm, D), _row(tm, FF), _row(tm, FF), _acc(8, D)],
                 [_sds((S, D)), _sds((S, D), BF16), _sds((S, FF), BF16), _sds((S, FF), BF16), _sds((8, D))],
                 hosts=hosts)(dz2, z1, ab, ub, lg, lb, wgt, wut, wd)


def _bwd_a2(dz1, cv, lg, lb, w2, l, hosts=()):
    tm = 512

    def body(dz_ref, cv_ref, lg_ref, lb_ref, w2_ref, dcv_ref, sb_ref, acc_ref):
        _zero_first(acc_ref)
        lgv = lg_ref[...]
        n, xhat, rstd = _ln(cv_ref[...], lgv, lb_ref[...])
        sg = _sig(n)
        sb_ref[...] = (n * sg).astype(BF16)
        dzb = dz_ref[...].astype(BF16)
        ds = _mmt(dzb, w2_ref[...])
        dn = ds * _silu_grad(n, sg)
        acc_ref[0:1, :] += _colsum(dn * xhat)
        acc_ref[1:2, :] += _colsum(dn)
        dcv = _ln_bwd(dn * lgv, xhat, rstd)
        acc_ref[2:3, :] += _colsum(dcv)
        dcv_ref[...] = dcv

    return _call(f"bwd_a2_{l}", body, (S // tm,),
                 [_row(tm, D), _row(tm, D), _const((1, D)), _const((1, D)), _const(w2.shape)],
                 [_row(tm, D), _row(tm, D), _acc(8, D)],
                 [_sds((S, D)), _sds((S, D), BF16), _sds((8, D))], hosts=hosts)(dz1, cv, lg, lb, w2)


def _bwd_conv_a(dcv, glu, wdw, l, hosts=()):
    tm = 256
    nb = S // tm

    def body(d_ref, dn_ref, g_ref, gp_ref, w_ref, dglu_ref, dw_ref, bufd, bufx, sh):
        i = pl.program_id(0)
        _zero_first(dw_ref)
        bufd[0:tm, :] = d_ref[...]
        _fill_halo(bufd, tm, HALO_A, lambda: dn_ref[...], i == nb - 1)
        _fill_halo(bufx, 0, HALO_A, lambda: gp_ref[...], i == 0)
        bufx[HALO_A:HALO_A + tm, :] = g_ref[...]
        _shift_copies(bufd, sh)
        _conv_rows(dglu_ref, w_ref, None, [CONV_A - 1 - k for k in range(CONV_A)], bufd, sh, tm)
        _shift_copies(bufx, sh)
        _conv_wgrad(dw_ref, d_ref, [HALO_A - (CONV_A - 1) + k for k in range(CONV_A)], bufx, sh, tm)

    return _call(f"bwd_conv_a_{l}", body, (nb,),
                 [_row(tm, D), _next(tm, HALO_A, D), _row(tm, D), _prev(tm, HALO_A, D), _const((32, D))],
                 [_row(tm, D), _acc(32, D)], [_sds((S, D)), _sds((32, D))],
                 scratch=[pltpu.VMEM((tm + HALO_A, D), F32), pltpu.VMEM((HALO_A + tm, D), F32),
                          pltpu.VMEM((SUB - 1, HALO_A + tm - SUB, D), F32)], hosts=hosts)(dcv, dcv, glu, glu, wdw)


def _bwd_a1(dglu, h, dz1, w1, l, hosts=()):
    tm = 256

    def body(dg_ref, h_ref, dz_ref, w_ref, dx_ref, dh_ref, acc_ref):
        _zero_first(acc_ref)
        a, g = h_ref[:, :D], h_ref[:, D:]
        sg = _sig(g)
        dgl = dg_ref[...]
        da = dgl * sg
        dg = dgl * a * sg * (1.0 - sg)
        acc_ref[0:1, 0:D] += _colsum(da)
        acc_ref[0:1, D:2 * D] += _colsum(dg)
        dh_ref[:, 0:D] = da.astype(BF16)
        dh_ref[:, D:2 * D] = dg.astype(BF16)
        dx = ALPHA * dz_ref[...]
        for q in range(NQ):
            dx = dx + _mmt(dh_ref[:, q * 512:(q + 1) * 512], w_ref[q])
        dx_ref[...] = dx

    return _call(f"bwd_a1_{l}", body, (S // tm,),
                 [_row(tm, D), _row(tm, 2 * D), _row(tm, D), _wspec(w1)],
                 [_row(tm, D), _row(tm, 2 * D), _acc(8, 2 * D)],
                 [_sds((S, D)), _sds((S, 2 * D), BF16), _sds((8, 2 * D))], hosts=hosts)(dglu, h, dz1, w1)


def _bwd_c2(dz1, hc, wc, wout):
    tm = 512

    def body(dz_ref, hc_ref, hcp_ref, wc_ref, wout_ref, dy_ref, dbg_ref, mb_ref, buf):
        y = _short_conv(buf, hc_ref, hcp_ref, wc_ref, tm, pl.program_id(0))
        dzb = dz_ref[...].astype(BF16)
        dm = _mmt(dzb, wout_ref[...])
        bg = hc_ref[:, :D]
        mb_ref[...] = (bg * y).astype(BF16)
        dbg_ref[...] = (dm * y).astype(BF16)
        dy_ref[...] = dm * bg

    return _call("bwd_c2", body, (S // tm,),
                 [_row(tm, D), _row(tm, 3 * D), _prev(tm, HALO_C, 3 * D), _const((8, D)), _const(wout.shape)],
                 [_row(tm, D), _row(tm, D), _row(tm, D)],
                 [_sds((S, D)), _sds((S, D), BF16), _sds((S, D), BF16)],
                 scratch=[pltpu.VMEM((HALO_C + tm, D), F32)])(dz1, hc, hc, wc, wout)


def _bwd_c1(dy, hc, dbg, dz1, wc, win):
    tm = 256
    nb = S // tm

    def body(d_ref, dn_ref, hc_ref, hcp_ref, dbg_ref, dz_ref, wc_ref, win_ref, dx_ref, dhc_ref, dwc_ref, bufd, bufq):
        i = pl.program_id(0)
        _zero_first(dwc_ref)
        bufd[0:tm, :] = d_ref[...]
        _fill_halo(bufd, tm, HALO_C, lambda: dn_ref[...], i == nb - 1)
        _fill_halo(bufq, 0, HALO_C, lambda: hcp_ref[:, D:2 * D] * hcp_ref[:, 2 * D:], i == 0)
        bufq[HALO_C:HALO_C + tm, :] = hc_ref[:, D:2 * D] * hc_ref[:, 2 * D:]
        dq = wc_ref[0:1, :] * bufd[pl.ds(CONV_C - 1, tm), :]
        for k in range(1, CONV_C):
            dq = dq + wc_ref[k:k + 1, :] * bufd[pl.ds(CONV_C - 1 - k, tm), :]
        dv = d_ref[...]
        for k in range(CONV_C):
            dwc_ref[k:k + 1, :] += _colsum(dv * bufq[pl.ds(HALO_C - (CONV_C - 1) + k, tm), :])
        dhc_ref[:, 0:D] = dbg_ref[...]
        dhc_ref[:, D:2 * D] = (dq * hc_ref[:, 2 * D:]).astype(BF16)
        dhc_ref[:, 2 * D:3 * D] = (dq * hc_ref[:, D:2 * D]).astype(BF16)
        dx = ALPHA * dz_ref[...]
        for q in range(NQ):
            dx = dx + _mmt(dhc_ref[:, q * 768:(q + 1) * 768], win_ref[q])
        dx_ref[...] = dx

    return _call("bwd_c1", body, (nb,),
                 [_row(tm, D), _next(tm, HALO_C, D), _row(tm, 3 * D), _prev(tm, HALO_C, 3 * D), _row(tm, D),
                  _row(tm, D), _const((8, D)), _wspec(win)],
                 [_row(tm, D), _row(tm, 3 * D), _acc(8, D)],
                 [_sds((S, D)), _sds((S, 3 * D), BF16), _sds((8, D))],
                 scratch=[pltpu.VMEM((tm + HALO_C, D), F32), pltpu.VMEM((HALO_C + tm, D), F32)]
                 )(dy, dy, hc, hc, dbg, dz1, wc, win)


def _bwd_b(dz1, zg, gg, lg, lb, win, wout, ws, wst, bsx, hosts=()):
    tm = 256
    nb = S // tm

    def body(dz_ref, zg_ref, gg_ref, lg_ref, lb_ref, win_ref, wout_ref, ws_ref, wst_ref, bsx_ref,
             dx_ref, dh_ref, mb_ref, acc_ref, dws_ref, dbs_ref, f_scr, dvn_scr):
        _zero_first(acc_ref, dws_ref, dbs_ref)
        lgv = lg_ref[...]
        u = zg_ref[:, :E].astype(F32)
        v = zg_ref[:, E:].astype(F32)
        vn, xhat, rstd = _ln(v, lgv, lb_ref[...])
        vnb = vn.astype(BF16)
        dzb = dz_ref[...].astype(BF16)
        dm = _mmt(dzb, wout_ref[...])
        mask, mask_t = _sgu_masks()
        for hd in range(SGU_H):
            wm = jnp.where(mask, ws_ref[hd], 0.0).astype(BF16)
            cs = slice(hd * SGU_G, (hd + 1) * SGU_G)
            for n in range(tm // SGU_T):
                rs = slice(n * SGU_T, (n + 1) * SGU_T)
                f_scr[rs, cs] = jnp.dot(wm, vnb[rs, cs], preferred_element_type=F32) + bsx_ref[hd]
        f = f_scr[...]
        mb_ref[...] = (u * f).astype(BF16)
        du = dm * f
        df = dm * u
        dfb = df.astype(BF16)
        for hd in range(SGU_H):
            wmt = jnp.where(mask_t, wst_ref[hd], 0.0).astype(BF16)
            cs = slice(hd * SGU_G, (hd + 1) * SGU_G)
            for n in range(tm // SGU_T):
                rs = slice(n * SGU_T, (n + 1) * SGU_T)
                dvn_scr[rs, cs] = jnp.dot(wmt, dfb[rs, cs], preferred_element_type=F32)
                dws_ref[hd] += lax.dot_general(dfb[rs, cs], vnb[rs, cs], (((1,), (1,)), ((), ())),
                                               preferred_element_type=F32)
                dbs_ref[hd] += df[rs, cs]
        dvn = dvn_scr[...]
        acc_ref[1:2, 0:E] += _colsum(dvn * xhat)
        acc_ref[2:3, 0:E] += _colsum(dvn)
        dv = _ln_bwd(dvn * lgv, xhat, rstd)
        dhu = du * gg_ref[:, :E].astype(F32)
        dhv = dv * gg_ref[:, E:].astype(F32)
        acc_ref[0:1, 0:E] += _colsum(dhu)
        acc_ref[0:1, E:2 * E] += _colsum(dhv)
        dh_ref[:, 0:E] = dhu.astype(BF16)
        dh_ref[:, E:2 * E] = dhv.astype(BF16)
        dx = ALPHA * dz_ref[...]
        for q in range(NQ):
            dx = dx + _mmt(dh_ref[:, q * 1024:(q + 1) * 1024], win_ref[q])
        dx_ref[...] = dx

        @pl.when(pl.program_id(0) == nb - 1)
        def _():
            for hd in range(SGU_H):
                dws_ref[hd] = jnp.where(mask, dws_ref[hd], 0.0)

    c3 = lambda a, b, c: pl.BlockSpec((a, b, c), lambda i: (0, 0, 0))
    return _call("bwd_b", body, (nb,),
                 [_row(tm, D), _row(tm, 2 * E), _row(tm, 2 * E), _const((1, E)), _const((1, E)), _wspec(win),
                  _const(wout.shape), _const((SGU_H, SGU_T, SGU_T)), _const((SGU_H, SGU_T, SGU_T)),
                  _const((SGU_H, SGU_T, SGU_G))],
                 [_row(tm, D), _row(tm, 2 * E), _row(tm, E), _acc(8, 2 * E), c3(SGU_H, SGU_T, SGU_T),
                  c3(SGU_H, SGU_T, SGU_G)],
                 [_sds((S, D)), _sds((S, 2 * E), BF16), _sds((S, E), BF16), _sds((8, 2 * E)),
                  _sds((SGU_H, SGU_T, SGU_T)), _sds((SGU_H, SGU_T, SGU_G))],
                 scratch=[pltpu.VMEM((tm, E), F32), pltpu.VMEM((tm, E), F32)], hosts=hosts
                 )(dz1, zg, gg, lg, lb, win, wout, ws, wst, bsx)


def _mm_tn(name, a, amode, b, bmode, k, n, groups=NQ, hosts=()):
    def block_bytes(ts):
        ka = k if amode == "1" else groups * k
        nb = n if bmode == "1" else groups * n
        return 2 * (ts * ka * a.dtype.itemsize + ts * nb * b.dtype.itemsize + groups * k * n * 4)

    ts = min(1024 if block_bytes(1024) <= DW_BLOCK_BUDGET else 512, S)

    def spec(mode, w):
        if mode == "1":
            return pl.BlockSpec((ts, w), lambda s: (s, 0))
        if mode == "c":
            return pl.BlockSpec((ts, groups * w), lambda s: (s, 0))
        return pl.BlockSpec((groups, ts, w), lambda s: (0, s, 0))

    def pick(ref, mode, w, g):
        if mode == "1":
            return ref[...]
        if mode == "c":
            return ref[:, g * w:(g + 1) * w]
        return ref[g]

    def body(a_ref, b_ref, o_ref):
        _zero_first(o_ref)
        a_t = jnp.transpose(a_ref[...].astype(BF16)) if amode == "1" else None
        b_1 = b_ref[...].astype(BF16) if bmode == "1" else None
        for g in range(groups):
            lhs = a_t if amode == "1" else jnp.transpose(pick(a_ref, amode, k, g).astype(BF16))
            rhs = b_1 if bmode == "1" else pick(b_ref, bmode, n, g).astype(BF16)
            o_ref[0, g] += jnp.dot(lhs, rhs, preferred_element_type=F32)

    return _call(name, body, (S // ts,), [spec(amode, k), spec(bmode, n)],
                 pl.BlockSpec((1, groups, k, n), lambda s: (0, 0, 0, 0)), _sds((1, groups, k, n)), hosts=hosts)(a, b)


def _row_block(k, cap=256):
    return max(t for t in range(16, min(k, cap) + 1, 16) if k % t == 0)


def _cast_bf16(w, hosts=()):
    nl, k, n = w.shape
    tb = _row_block(k, 512)
    nb = k // tb

    def body(w_ref, o_ref):
        o_ref[...] = w_ref[...].astype(BF16)

    spec = pl.BlockSpec((None, tb, n), lambda i: (i // nb, i % nb, 0))
    return _call("cast_bf16", body, (nl * nb,), [spec], spec, _sds(w.shape, BF16), hosts=hosts)(w)


def _adam(name, w, m, v, gc, l, prev):
    nl, k, n = w.shape
    nc = gc.shape[0]
    tb = _row_block(k, 512)

    def body(w_ref, m_ref, v_ref, g_ref, *rest):
        go_ref, d_ref, mo_ref, vo_ref = rest[-4:]
        g = g_ref[0].astype(F32)
        for c in range(1, nc):
            g = g + g_ref[c].astype(F32)
        m2 = ADAM_B1 * m_ref[...] + (1.0 - ADAM_B1) * g
        v2 = ADAM_B2 * v_ref[...] + (1.0 - ADAM_B2) * (g * g)
        m_hat = m2 / (1.0 - ADAM_B1 ** ADAM_STEP)
        v_hat = v2 / (1.0 - ADAM_B2 ** ADAM_STEP)
        go_ref[...] = g
        d_ref[...] = -ADAM_LR * (m_hat / (jnp.sqrt(v_hat) + ADAM_EPS) + ADAM_WD * w_ref[...])
        mo_ref[...] = m2
        vo_ref[...] = v2

    spec = pl.BlockSpec((None, tb, n), lambda i: (l, i, 0))
    gspec = pl.BlockSpec((nc, None, tb, n), lambda i: (0, 0, i, 0))
    in_specs, args, aliases = [spec, spec, spec, gspec], [w, m, v, gc], {}
    if prev is not None:
        in_specs += [pl.BlockSpec(memory_space=pl.ANY)] * 4
        args += list(prev)
        aliases = {4 + j: j for j in range(4)}
    return _call(name, body, (k // tb,), in_specs, [spec] * 4, [_sds(w.shape)] * 4, aliases=aliases)(*args)


def _sum8(name, g8):
    r = g8.shape[1]

    def body(g_ref, o_ref):
        acc = g_ref[0]
        for d in range(1, 8):
            acc = acc + g_ref[d]
        o_ref[...] = acc

    return _call(name, body, (1,), [pl.BlockSpec((8, r, 128), lambda i: (0, 0, 0))],
                 pl.BlockSpec((r, 128), lambda i: (0, 0)), _sds((r, 128)))(g8)


def _place():
    x, y, c = lax.axis_index("x"), lax.axis_index("y"), lax.axis_index("c")
    return x, y, c, 2 * x + y, (x, y, 1 - c), [(1 - x, y), (x, 1 - y), (1 - x, 1 - y)]


class _Exchange:
    def __init__(self, arrays, out_shapes):
        self.arrays, self.out_shapes = list(arrays), list(out_shapes)
        n = len(self.arrays)
        self.sems = [pltpu.SemaphoreType.DMA((7 * n,)), pltpu.SemaphoreType.DMA((7 * n,)),
                     pltpu.SemaphoreType.DMA((n,))]

    def _copies(self, ins, outs, sems):
        send, recv, lsem = sems
        local_src, remote_src, dst = self.maps(ins, outs)
        x, y, c, q, sib, chips = _place()

        def rcopy(w, k, qq, cc, to, src=None):
            return pltpu.make_async_remote_copy(
                src_ref=dst(w, qq, cc) if src is None else src, dst_ref=dst(w, qq, cc),
                send_sem=send.at[7 * w + k], recv_sem=recv.at[7 * w + k], device_id=to, device_id_type=MESH)

        def mine(w):
            return pltpu.make_async_copy(local_src(w), dst(w, q, c), lsem.at[w])

        def first(w):
            return [rcopy(w, 0, q, c, sib, local_src(w))] + [
                rcopy(w, 1 + j, q, c, (cx, cy, c), remote_src(w, 2 * cx + cy)) for j, (cx, cy) in enumerate(chips)]

        return rcopy, mine, first, (x, y, c), q, c, sib, chips

    def start(self, ins, outs, sems):
        _, mine, first, *_ = self._copies(ins, outs, sems)
        for w in range(len(self.arrays)):
            mine(w).start()
            for cp in first(w):
                cp.start()

    def forward_steps(self, n_steps):
        sizes = [a.size // a.shape[0] for a in self.arrays]
        plan, moved = {}, 0
        for w, size in enumerate(sizes):
            moved += size
            plan.setdefault(min(n_steps - 1, -(-moved * n_steps // sum(sizes))), []).append(w)
        return plan

    def forward(self, ws, ins, outs, sems):
        rcopy, _, _, me, _, c, sib, chips = self._copies(ins, outs, sems)
        for w in ws:
            for j, (cx, cy) in enumerate(chips):
                rcopy(w, 1 + j, 2 * cx + cy, c, me).wait_recv()
                rcopy(w, 4 + j, 2 * cx + cy, c, sib).start()

    def complete(self, ins, outs, sems):
        rcopy, mine, first, me, q, c, sib, chips = self._copies(ins, outs, sems)
        n = len(self.arrays)
        for w in range(n):
            rcopy(w, 0, q, 1 - c, me).wait_recv()
            for j, (cx, cy) in enumerate(chips):
                rcopy(w, 4 + j, 2 * cx + cy, 1 - c, me).wait_recv()
        for w in range(n):
            for cp in first(w):
                cp.wait_send()
            for j, (cx, cy) in enumerate(chips):
                rcopy(w, 4 + j, 2 * cx + cy, c, sib).wait_send()
            mine(w).wait()


class _GatherWeights(_Exchange):
    def __init__(self, items):
        self.layers = [l for _, l in items]
        self.kh = [s.shape[1] // 2 for s, _ in items]
        super().__init__([s for s, _ in items], [_sds((NQ, 1) + s.shape[1:], BF16) for s, _ in items])

    def maps(self, ins, outs):
        c = lax.axis_index("c")
        src = lambda w: ins[w].at[pl.ds(self.layers[w], 1), pl.ds(c * self.kh[w], self.kh[w]), :]
        return src, lambda w, q: src(w), lambda w, q, cc: outs[w].at[q, :, pl.ds(cc * self.kh[w], self.kh[w]), :]


class _ScatterPartials(_Exchange):
    def __init__(self, parts):
        super().__init__(parts, [_sds((NQ, 1, 2) + p.shape[2:], BF16) for p in parts])

    def maps(self, ins, outs):
        q = 2 * lax.axis_index("x") + lax.axis_index("y")
        return (lambda w: ins[w].at[:, q]), (lambda w, qq: ins[w].at[:, qq]), (lambda w, qq, cc: outs[w].at[qq, :, cc])


class _Gather8(_Exchange):
    def __init__(self, v):
        super().__init__([v], [_sds((8,) + v.shape)])

    def maps(self, ins, outs):
        return (lambda w: ins[0]), (lambda w, q: ins[0]), (lambda w, q, cc: outs[0].at[2 * q + cc])


class _SwapHalves:
    def __init__(self, dws):
        self.arrays = list(dws)
        self.kh = [d.shape[2] // 2 for d in dws]
        self.out_shapes = [_sds(d.shape[:2] + (kh,) + d.shape[3:]) for d, kh in zip(dws, self.kh)]
        self.sems = [pltpu.SemaphoreType.DMA((len(dws),)), pltpu.SemaphoreType.DMA((len(dws),))]

    def _copies(self, ins, outs, sems):
        send, recv = sems
        _, _, c, _, sib, _ = _place()
        return [pltpu.make_async_remote_copy(
            src_ref=ins[w].at[:, :, pl.ds((1 - c) * self.kh[w], self.kh[w]), :], dst_ref=outs[w],
            send_sem=send.at[w], recv_sem=recv.at[w], device_id=sib, device_id_type=MESH)
            for w in range(len(self.arrays))]

    def start(self, ins, outs, sems):
        for cp in self._copies(ins, outs, sems):
            cp.start()

    def forward_steps(self, n_steps):
        return {}

    def complete(self, ins, outs, sems):
        for cp in self._copies(ins, outs, sems):
            cp.wait()


def _comm_only(name, host):
    n_in, n_out = len(host.arrays), len(host.out_shapes)

    def body(*refs):
        ins, outs, sems = refs[:n_in], refs[n_in:n_in + n_out], refs[n_in + n_out:]
        host.start(ins, outs, sems)
        for ws in host.forward_steps(1).values():
            host.forward(ws, ins, outs, sems)
        host.complete(ins, outs, sems)

    any_spec = pl.BlockSpec(memory_space=pl.ANY)
    return pl.pallas_call(body, name=name, in_specs=[any_spec] * n_in, out_specs=[any_spec] * n_out,
                          out_shape=host.out_shapes, scratch_shapes=host.sems)(*host.arrays)


def _add_halves(dw, got, cidx):
    nl, _, k, n = dw.shape
    kh = k // 2
    qb = 2

    def body(c_ref, a_ref, b_ref, o_ref):
        o_ref[...] = (a_ref[...] + b_ref[...]).astype(BF16)

    grid_spec = pltpu.PrefetchScalarGridSpec(
        num_scalar_prefetch=1, grid=(nl, NQ // qb),
        in_specs=[pl.BlockSpec((None, qb, None, kh, n), lambda l, q, c_ref: (l, q, c_ref[0], 0, 0)),
                  pl.BlockSpec((None, qb, kh, n), lambda l, q, c_ref: (l, q, 0, 0))],
        out_specs=pl.BlockSpec((None, qb, kh, n), lambda l, q, c_ref: (l, q, 0, 0)))
    return pl.pallas_call(
        body, name="add_halves", grid_spec=grid_spec, out_shape=_sds((nl, NQ, kh, n), BF16),
        compiler_params=pltpu.CompilerParams(dimension_semantics=("arbitrary", "arbitrary"),
                                             vmem_limit_bytes=VMEM_LIMIT))(cidx, dw.reshape(nl, NQ, 2, kh, n), got)


def _gather8(name, v):
    return _comm_only(name, _Gather8(v))[0]


PACK = 16 * 128


def _pack(arrays):
    parts = []
    for a in arrays:
        flat = a.reshape(-1)
        parts.append(jnp.pad(flat, (0, (-flat.shape[0]) % PACK)))
    return jnp.concatenate(parts).reshape(-1, 128)


def _unpack(packed, shapes):
    flat = packed.reshape(-1)
    out, off = [], 0
    for shp in shapes:
        size = 1
        for d in shp:
            size *= d
        out.append(flat[off:off + size].reshape(shp))
        off += size + (-size) % PACK
    return out


def kernel(x, p, a_w_pw1, a_b_pw1, a_w_dw, a_b_dw, a_ln_g, a_ln_b, a_w_pw2, b_w_in, b_b_in, b_ln_g, b_ln_b, b_w_s, b_b_s, b_w_out, c_w_in, c_w_conv, c_w_out, ln1_g, ln1_b, ln2_g, ln2_b, ffn_w_gate, ffn_w_up, ffn_w_down, ple_w_gate, ple_w_proj, ple_norm_g, loss_target, m_a_w_pw1, m_a_b_pw1, m_a_w_dw, m_a_b_dw, m_a_ln_g, m_a_ln_b, m_a_w_pw2, m_b_w_in, m_b_b_in, m_b_ln_g, m_b_ln_b, m_b_w_s, m_b_b_s, m_b_w_out, m_c_w_in, m_c_w_conv, m_c_w_out, m_ln1_g, m_ln1_b, m_ln2_g, m_ln2_b, m_ffn_w_gate, m_ffn_w_up, m_ffn_w_down, m_ple_w_gate, m_ple_w_proj, m_ple_norm_g, v_a_w_pw1, v_a_b_pw1, v_a_w_dw, v_a_b_dw, v_a_ln_g, v_a_ln_b, v_a_w_pw2, v_b_w_in, v_b_b_in, v_b_ln_g, v_b_ln_b, v_b_w_s, v_b_b_s, v_b_w_out, v_c_w_in, v_c_w_conv, v_c_w_out, v_ln1_g, v_ln1_b, v_ln2_g, v_ln2_b, v_ffn_w_gate, v_ffn_w_up, v_ffn_w_down, v_ple_w_gate, v_ple_w_proj, v_ple_norm_g):
    args = dict(locals())
    wts = {k: args[k] for k in WEIGHTS}
    mom = {k: args["m_" + k] for k in WEIGHTS}
    var = {k: args["v_" + k] for k in WEIGHTS}
    for k in TRANSPOSED:
        wts[k], mom[k], var[k] = (jnp.transpose(t[k], (0, 2, 1)) for t in (wts, mom, var))
    q_idx = 2 * lax.axis_index("x") + lax.axis_index("y")
    c_idx = lax.axis_index("c").astype(jnp.int32).reshape(1)

    wb = {k: _cast_bf16(wts[k]) for k in BIG if k not in ("ffn_w_gate", "ffn_w_up")}
    mixw = [[("a_w_pw1", 0), ("a_w_pw2", 0)], [("b_w_in", 0), ("b_w_out", 0)], [("c_w_in", 0), ("c_w_out", 0)],
            [("a_w_pw1", 1), ("a_w_pw2", 1)]]
    ffnw = [[("ffn_w_gate", l), ("ffn_w_up", l), ("ffn_w_down", l)] for l in range(DEPTH)]
    plew = [[("ple_w_gate", l), ("ple_w_proj", l)] for l in range(DEPTH)]
    fwd_plan = {("a1", 0): mixw[0][1:] + plew[0], ("a2", 0): ffnw[0], ("ffn", 0): mixw[1] + plew[1],
                ("b", 1): ffnw[1], ("ffn", 1): mixw[2] + plew[2] + ffnw[2][:1],
                ("c1", 2): ffnw[2][1:2], ("c2", 2): ffnw[2][2:], ("ffn", 2): mixw[3] + plew[3] + ffnw[3][:1],
                ("a2", 3): ffnw[3][1:]}
    gw = {}

    def gather(keys):
        return _GatherWeights([(wb[name], l) for name, l in keys])

    def hosted(tag, fn, *fargs):
        keys = fwd_plan.get(tag)
        if not keys:
            return fn(*fargs)
        own, (got,) = fn(*fargs, hosts=[gather(keys)])
        store(keys, got)
        return own

    def store(keys, got):
        for (name, l), arr in zip(keys, got):
            gw[name, l] = arr.reshape(NQ * arr.shape[2], arr.shape[3]) if name in ROW_SHARDED else arr

    first_keys = mixw[0][:1]
    wb["ffn_w_gate"], (got,) = _cast_bf16(wts["ffn_w_gate"], hosts=[gather(first_keys)])
    store(first_keys, got)
    shard_shapes = [wts[k].shape for k in SMALL_SHARDED]
    wb["ffn_w_up"], ((small8,),) = _cast_bf16(wts["ffn_w_up"], hosts=[_Gather8(_pack([wts[k] for k in SMALL_SHARDED]))])
    per_chip = [_unpack(small8[2 * qq], shard_shapes) for qq in range(NQ)]
    full = {k: jnp.concatenate([per_chip[qq][i] for qq in range(NQ)], axis=-1) for i, k in enumerate(SMALL_SHARDED)}
    for k in SMALL_REPL:
        full[k] = wts[k]

    def vec(name, l):
        return full[name][l][None, :]

    def conv_w(name, l, rows):
        w = full[name][l]
        return jnp.pad(w, ((0, rows - w.shape[0]), (0, 0)))

    ws = full["b_w_s"][0]
    wst = jnp.transpose(ws, (0, 2, 1))
    bsx = jnp.broadcast_to(full["b_b_s"][0][:, :, None], (SGU_H, SGU_T, SGU_G))

    x0s, z1s, z2s, saved, ffn_saved = [], [], [], [], []
    cur = x[0]
    for i in range(DEPTH):
        mix, j = i % 3, i // 3
        x0s.append(cur)
        if mix == 0:
            h, glu = hosted(("a1", i), _fwd_a1, cur, gw["a_w_pw1", j], vec("a_b_pw1", j), i)
            z1, cv = hosted(("a2", i), _fwd_a2, glu, cur, conv_w("a_w_dw", j, 32), vec("a_b_dw", j), vec("a_ln_g", j),
                            vec("a_ln_b", j), gw["a_w_pw2", j], i)
            saved.append((h, glu, cv))
        elif mix == 1:
            z1, zg, gg = hosted(("b", i), _fwd_b, cur, gw["b_w_in", 0], vec("b_b_in", 0), vec("b_ln_g", 0),
                                vec("b_ln_b", 0), ws, bsx, gw["b_w_out", 0])
            saved.append((zg, gg))
        else:
            hc = hosted(("c1", i), _fwd_c1, cur, gw["c_w_in", 0])
            z1 = hosted(("c2", i), _fwd_c2, hc, cur, conv_w("c_w_conv", 0, 8), gw["c_w_out", 0])
            saved.append((hc,))
        z2, ab, ub, hm = hosted(("ffn", i), _fwd_ffn, z1, vec("ln1_g", i), vec("ln1_b", i), gw["ffn_w_gate", i],
                                gw["ffn_w_up", i], gw["ffn_w_down", i], i)
        ffn_saved.append((ab, ub, hm))
        cur = hosted(("ple", i), _fwd_ple, z2, p[i, 0], vec("ln2_g", i), vec("ln2_b", i), gw["ple_w_gate", i],
                     gw["ple_w_proj", i], vec("ple_norm_g", i), i)
        z1s.append(z1)
        z2s.append(z2)

    g, loss_acc = _loss_head(cur, loss_target[0])
    loss = lax.psum(0.5 / D * jnp.sum(loss_acc[0]), ("x", "y", "c"))

    dws = {}
    sg = {}
    res = {k: None for k in BIG}

    def wgrad(name, l, a, amode, b, bmode, scatter_keys=()):
        _, k, n = wts[name].shape
        hosts = [_ScatterPartials([parts[key] for key in scatter_keys])] if scatter_keys else ()
        if name in ROW_SHARDED:
            out = _mm_tn(f"dw_{name}_{l}", a, "1", b, "1", NQ * k, n, groups=1, hosts=hosts)
        else:
            out = _mm_tn(f"dw_{name}_{l}", a, amode, b, bmode, k, n, hosts=hosts)
        if scatter_keys:
            out, (contribs,) = out
            update(scatter_keys, contribs)
        dws[name, l] = out.reshape(1, NQ, k, n)

    def swap(keys):
        return _SwapHalves([dws[k] for k in keys])

    parts = {}

    def add_halves(keys, got):
        parts.update((k, _add_halves(dws[k], r, c_idx)) for k, r in zip(keys, got))

    def update(keys, contribs):
        for (name, l), gc in zip(keys, contribs):
            _, kq, n = wts[name].shape
            res[name] = _adam(f"adam_{name}_{l}", wts[name], mom[name], var[name], gc.reshape(NQ, 1, kq, n), l,
                              res[name])

    small = SMALL_SHARDED + SMALL_REPL
    late_small = [("a_b_pw1", 0)]
    early_small = [(k, l) for k in small for l in range(full[k].shape[0]) if (k, l) not in late_small]
    pending = None
    for i in reversed(range(DEPTH)):
        mix, j = i % 3, i // 3
        ple_args = (g, z2s[i], p[i, 0], vec("ln2_g", i), vec("ln2_b", i), gw["ple_w_gate", i], gw["ple_w_proj", i],
                    vec("ple_norm_g", i), i)
        if pending:
            (dz2, x2b, dgp, dqp, acc), (got,) = _bwd_ple(*ple_args, hosts=[swap(pending)])
            add_halves(pending, got)
        else:
            dz2, x2b, dgp, dqp, acc = _bwd_ple(*ple_args)
        sg["ple_norm_g", i], sg["ln2_g", i], sg["ln2_b", i] = acc[0], acc[1], acc[2]
        wgrad("ple_w_gate", i, x2b, "c", dgp, "1")
        wgrad("ple_w_proj", i, p[i, 0], "1", dqp, "c")
        ab, ub, hm = ffn_saved[i]
        ffn_args = (dz2, z1s[i], ab, ub, vec("ln1_g", i), vec("ln1_b", i), gw["ffn_w_gate", i], gw["ffn_w_up", i],
                    gw["ffn_w_down", i], i)
        if pending:
            (dz1, x1b, da, du, acc), (contribs,) = _bwd_ffn(
                *ffn_args, hosts=[_ScatterPartials([parts[key] for key in ffnw[i + 1]])])
            update(ffnw[i + 1], contribs)
        else:
            dz1, x1b, da, du, acc = _bwd_ffn(*ffn_args)
        sg["ln1_g", i], sg["ln1_b", i] = acc[0], acc[1]
        behind_mixer = mix == 1
        rest = mixw[i + 1] + plew[i + 1] if pending else []
        wgrad("ffn_w_gate", i, da, "1", x1b, "1", scatter_keys=() if behind_mixer else rest[1:])
        wgrad("ffn_w_up", i, du, "1", x1b, "1")
        wgrad("ffn_w_down", i, hm, "1", dz2, "1", scatter_keys=() if behind_mixer else rest[:1])
        x0 = x0s[i]
        if mix == 0:
            h, glu, cv = saved[i]
            a2_args = (dz1, cv, vec("a_ln_g", j), vec("a_ln_b", j), gw["a_w_pw2", j], i)
            conv_args = (glu, conv_w("a_w_dw", j, 32), i)
            if i == 0:
                early = ffnw[0] + plew[0]
                (dcv, sb, acc), (got,) = _bwd_a2(*a2_args, hosts=[swap(early)])
                sg["a_ln_g", j], sg["a_ln_b", j], sg["a_b_dw", j] = acc[0], acc[1], acc[2]
                add_halves(early, got)
                wgrad("a_w_pw2", j, sb, "c", dz1, "1")
                (dglu, dwdw), (contribs, got) = _bwd_conv_a(
                    dcv, *conv_args, hosts=[_ScatterPartials([parts[key] for key in early]), swap(mixw[0][1:])])
                update(early, contribs)
                add_halves(mixw[0][1:], got)
                sg["a_w_dw", j] = dwdw[:CONV_A]
                (g, dh, acc), (contribs, (g8_early,)) = _bwd_a1(
                    dglu, h, dz1, gw["a_w_pw1", j], i,
                    hosts=[_ScatterPartials([parts[key] for key in mixw[0][1:]]),
                           _Gather8(_pack([sg[pc] for pc in early_small]))])
                update(mixw[0][1:], contribs)
            else:
                dcv, sb, acc = _bwd_a2(*a2_args)
                sg["a_ln_g", j], sg["a_ln_b", j], sg["a_b_dw", j] = acc[0], acc[1], acc[2]
                dglu, dwdw = _bwd_conv_a(dcv, *conv_args)
                wgrad("a_w_pw2", j, sb, "c", dz1, "1")
                sg["a_w_dw", j] = dwdw[:CONV_A]
                g, dh, acc = _bwd_a1(dglu, h, dz1, gw["a_w_pw1", j], i)
            sg["a_b_pw1", j] = acc[0]
            wgrad("a_w_pw1", j, x0, "1", dh, "c")
        elif mix == 1:
            zg, gg = saved[i]
            (g, dh, mb, acc, dw_s, db_s), (contribs,) = _bwd_b(
                dz1, zg, gg, vec("b_ln_g", 0), vec("b_ln_b", 0), gw["b_w_in", 0], gw["b_w_out", 0], ws, wst, bsx,
                hosts=[_ScatterPartials([parts[key] for key in rest])])
            update(rest, contribs)
            sg["b_b_in", 0], sg["b_ln_g", 0], sg["b_ln_b", 0] = acc[0], acc[1, :E], acc[2, :E]
            sg["b_w_s", 0], sg["b_b_s", 0] = dw_s, jnp.sum(db_s, axis=-1)
            wgrad("b_w_out", 0, mb, "c", dz1, "1")
            wgrad("b_w_in", 0, x0, "1", dh, "c")
        else:
            (hc,) = saved[i]
            wc = conv_w("c_w_conv", 0, 8)
            dy, dbg, mb = _bwd_c2(dz1, hc, wc, gw["c_w_out", 0])
            wgrad("c_w_out", 0, mb, "c", dz1, "1")
            g, dhc, dwc = _bwd_c1(dy, hc, dbg, dz1, wc, gw["c_w_in", 0])
            sg["c_w_conv", 0] = dwc[:CONV_C]
            wgrad("c_w_in", 0, x0, "1", dhc, "c")
        pending = mixw[i] + ffnw[i] + plew[i] if i > 0 else mixw[0][:1]
    grad_x = g[None]
    add_halves(pending, _comm_only("swap_last", swap(pending)))
    update(pending, _comm_only("scatter_last", _ScatterPartials([parts[key] for key in pending])))

    g8_late = _gather8("gather_small_late", _pack([sg[pc] for pc in late_small]))
    sums = dict(zip(early_small, _unpack(_sum8("sum8_early", g8_early), [sg[pc].shape for pc in early_small])))
    sums.update(zip(late_small, _unpack(_sum8("sum8_late", g8_late), [sg[pc].shape for pc in late_small])))
    gsum = [jnp.stack([sums[k, l] for l in range(full[k].shape[0])]) for k in small]
    gmine = []
    for k, gs in zip(small, gsum):
        if k in SMALL_SHARDED:
            wdt = wts[k].shape[-1]
            gs = lax.dynamic_slice_in_dim(gs, q_idx * wdt, wdt, axis=gs.ndim - 1)
        gmine.append(gs)
    packed = [_pack(t)[None] for t in ([wts[k] for k in small], [mom[k] for k in small], [var[k] for k in small])]
    outs = _adam("adam_small", packed[0], packed[1], packed[2], _pack(gmine)[None, None], 0, None)
    unpacked = [_unpack(o[0], [wts[k].shape for k in small]) for o in outs]
    for i, k in enumerate(small):
        res[k] = tuple(u[i] for u in unpacked)

    for k in TRANSPOSED:
        res[k] = tuple(jnp.transpose(r, (0, 2, 1)) for r in res[k])
    return (loss, grad_x, *[res[k][0] for k in WEIGHTS], *[res[k][1] for k in WEIGHTS],
            *[res[k][2] for k in WEIGHTS], *[res[k][3] for k in WEIGHTS])
```

```python
import functools

import jax
import jax.numpy as jnp
from jax import lax
from jax.experimental import pallas as pl
from jax.experimental.pallas import tpu as pltpu

F32, BF16 = jnp.float32, jnp.bfloat16
S = 4096
D = 1024
E = 2048
FF = 2816
FQ = FF // 4
NQ = 4
DEPTH = 4
ALPHA = (2 * DEPTH) ** 0.25
LN_EPS = 1e-5
CONV_A, CONV_C = 31, 3
HALO_A, HALO_C = 32, 8
SGU_T, SGU_H, SGU_G, SGU_CHUNK = 128, 8, 256, 64
VMEM_LIMIT = 56 * 1024 * 1024
DW_BLOCK_BUDGET = 40 * 1024 * 1024
MESH = pl.DeviceIdType.MESH
ADAM_LR, ADAM_B1, ADAM_B2, ADAM_EPS, ADAM_WD, ADAM_STEP = 0.001, 0.9, 0.999, 1e-08, 0.01, 10
GELU_C, GELU_A = 0.7978845608028654, 0.044715

BIG = ["a_w_pw1", "a_w_pw2", "b_w_in", "b_w_out", "c_w_in", "c_w_out",
       "ffn_w_gate", "ffn_w_up", "ffn_w_down", "ple_w_gate", "ple_w_proj"]
TRANSPOSED = ["ffn_w_gate", "ffn_w_up"]
ROW_SHARDED = ["a_w_pw2", "b_w_out", "c_w_out", "ffn_w_gate", "ffn_w_up", "ffn_w_down", "ple_w_gate"]
SMALL_SHARDED = ["a_b_pw1", "a_w_dw", "a_b_dw", "a_ln_g", "a_ln_b", "c_w_conv"]
SMALL_REPL = ["b_b_in", "b_ln_g", "b_ln_b", "b_w_s", "b_b_s", "ln1_g", "ln1_b", "ln2_g", "ln2_b", "ple_norm_g"]
WEIGHTS = ["a_w_pw1", "a_b_pw1", "a_w_dw", "a_b_dw", "a_ln_g", "a_ln_b", "a_w_pw2", "b_w_in", "b_b_in", "b_ln_g",
           "b_ln_b", "b_w_s", "b_b_s", "b_w_out", "c_w_in", "c_w_conv", "c_w_out", "ln1_g", "ln1_b", "ln2_g",
           "ln2_b", "ffn_w_gate", "ffn_w_up", "ffn_w_down", "ple_w_gate", "ple_w_proj", "ple_norm_g"]


def _call(name, body, grid, in_specs, out_specs, out_shape, scratch=(), aliases=None, hosts=()):
    params = pltpu.CompilerParams(dimension_semantics=("arbitrary",) * len(grid), vmem_limit_bytes=VMEM_LIMIT)
    if not hosts:
        return pl.pallas_call(
            body, name=name, grid=grid, in_specs=in_specs, out_specs=out_specs, out_shape=out_shape,
            scratch_shapes=list(scratch), input_output_aliases=aliases or {}, compiler_params=params)
    assert len(grid) == 1 and not aliases
    single = not isinstance(out_shape, (list, tuple))
    own_shapes = [out_shape] if single else list(out_shape)
    own_specs = [out_specs] if single else list(out_specs)
    n_in, n_out, n_scr = len(in_specs), len(own_shapes), len(scratch)
    h_in = [len(h.arrays) for h in hosts]
    h_out = [len(h.out_shapes) for h in hosts]
    h_sem = [len(h.sems) for h in hosts]

    def split(refs, counts):
        out, off = [], 0
        for cnt in counts:
            out.append(refs[off:off + cnt])
            off += cnt
        return out

    def wrapped(*refs):
        ins, hin, outs, hout, scr, hsem = split(refs, [n_in, sum(h_in), n_out, sum(h_out), n_scr, sum(h_sem)])
        per_host = list(zip(hosts, split(hin, h_in), split(hout, h_out), split(hsem, h_sem)))

        @pl.when(pl.program_id(0) == 0)
        def _():
            for h, a, o, s in per_host:
                h.start(a, o, s)

        body(*ins, *outs, *scr)

        for h, a, o, s in per_host:
            for step, ws in sorted(h.forward_steps(grid[0]).items()):
                pl.when(pl.program_id(0) == step)(functools.partial(h.forward, ws, a, o, s))

        @pl.when(pl.program_id(0) == grid[0] - 1)
        def _():
            for h, a, o, s in per_host:
                h.complete(a, o, s)

    any_spec = pl.BlockSpec(memory_space=pl.ANY)
    call = pl.pallas_call(
        wrapped, name=name, grid=grid, in_specs=list(in_specs) + [any_spec] * sum(h_in),
        out_specs=own_specs + [any_spec] * sum(h_out),
        out_shape=own_shapes + [s for h in hosts for s in h.out_shapes],
        scratch_shapes=list(scratch) + [s for h in hosts for s in h.sems], compiler_params=params)

    def run(*args):
        res = call(*args, *[a for h in hosts for a in h.arrays])
        own = res[0] if single else list(res[:n_out])
        return own, split(list(res[n_out:]), h_out)

    return run


def _sds(shape, dtype=F32):
    return jax.ShapeDtypeStruct(shape, dtype)


def _row(tm, c):
    return pl.BlockSpec((tm, c), lambda i: (i, 0))


def _grow(g, tm, c):
    return pl.BlockSpec((g, tm, c), lambda i: (0, i, 0))


def _const(shape):
    nd = len(shape)
    return pl.BlockSpec(shape, lambda i: (0,) * nd, pipeline_mode=pl.Buffered(1))


def _wspec(w):
    return pl.BlockSpec((NQ, None, w.shape[2], w.shape[3]), lambda i: (0, 0, 0, 0), pipeline_mode=pl.Buffered(1))


def _prev(tm, hb, c):
    return pl.BlockSpec((hb, c), lambda i: (jnp.maximum(i * (tm // hb) - 1, 0), 0))


def _next(tm, hb, c):
    return pl.BlockSpec((hb, c), lambda i: (jnp.minimum((i + 1) * (tm // hb), S // hb - 1), 0))


def _acc(r, c):
    return pl.BlockSpec((r, c), lambda i: (0, 0))


def _sig(x):
    return 1.0 / (1.0 + jnp.exp(-x))


def _ln(z, g, b):
    mu = jnp.mean(z, axis=-1, keepdims=True)
    zc = z - mu
    rstd = lax.rsqrt(jnp.mean(zc * zc, axis=-1, keepdims=True) + LN_EPS)
    xhat = zc * rstd
    return xhat * g + b, xhat, rstd


def _ln_bwd(dyg, xhat, rstd):
    return rstd * (dyg - jnp.mean(dyg, axis=-1, keepdims=True) - xhat * jnp.mean(dyg * xhat, axis=-1, keepdims=True))


def _mm(a, w):
    return jnp.dot(a.astype(BF16), w, preferred_element_type=F32)


def _mmt(a, w):
    return lax.dot_general(a.astype(BF16), w, (((1,), (1,)), ((), ())), preferred_element_type=F32)


def _colsum(x):
    return jnp.sum(x, axis=0, keepdims=True)


def _gelu_and_grad(x):
    x2 = x * x
    t = jnp.tanh(x * (GELU_C + (GELU_C * GELU_A) * x2))
    hx = 0.5 * x
    return hx + hx * t, 0.5 + 0.5 * t + hx * (1.0 - t * t) * (GELU_C + (3.0 * GELU_C * GELU_A) * x2)


def _silu_grad(a, sg):
    return sg * (1.0 + a * (1.0 - sg))


def _sgu_masks():
    r = lax.broadcasted_iota(jnp.int32, (SGU_T, SGU_T), 0) // SGU_CHUNK
    c = lax.broadcasted_iota(jnp.int32, (SGU_T, SGU_T), 1) // SGU_CHUNK
    return r >= c, c >= r


def _fill_halo(buf, lo, n, halo_val_fn, is_edge):
    @pl.when(is_edge)
    def _():
        buf[lo:lo + n, :] = jnp.zeros((n, buf.shape[1]), F32)

    @pl.when(jnp.logical_not(is_edge))
    def _():
        buf[lo:lo + n, :] = halo_val_fn()


SUB, LANE = 8, 128
ROWS_AT_ONCE = 16


def _shift_copies(buf, sh):
    rows = sh.shape[1]
    for s in range(1, SUB):
        sh[s - 1, :, :] = buf[pl.ds(s, rows), :]


def _tiles(buf, sh, s, first, count, group0, lanes):
    src = buf if s == 0 else sh.at[s - 1]
    return {t: src[pl.ds(pl.multiple_of((group0 + t) * SUB, SUB), SUB), lanes] for t in range(first, first + count)}


def _by_shift(offsets):
    out = []
    for s in range(SUB):
        taps = [(k, o // SUB) for k, o in enumerate(offsets) if o % SUB == s]
        if taps:
            out.append((s, taps))
    return out


def _conv_rows(out_ref, w_ref, bias_ref, offsets, buf, sh, tm):
    n = ROWS_AT_ONCE
    for cb in range(D // LANE):
        lanes = slice(cb * LANE, (cb + 1) * LANE)
        bias = None if bias_ref is None else jnp.broadcast_to(bias_ref[:, lanes], (SUB, LANE))

        def body(jb, carry):
            accs = [bias] * n
            for s, taps in _by_shift(offsets):
                ms = [m for _, m in taps]
                tiles = _tiles(buf, sh, s, min(ms), max(ms) - min(ms) + n, jb * n, lanes)
                for k, m in taps:
                    wk = jnp.broadcast_to(w_ref[k:k + 1, lanes], (SUB, LANE))
                    for jj in range(n):
                        t = wk * tiles[m + jj]
                        accs[jj] = t if accs[jj] is None else accs[jj] + t
            for jj in range(n):
                out_ref[pl.ds(pl.multiple_of((jb * n + jj) * SUB, SUB), SUB), lanes] = accs[jj]
            return carry

        lax.fori_loop(0, tm // (SUB * n), body, 0)


def _conv_wgrad(dw_ref, d_ref, offsets, buf, sh, tm):
    n = 4
    for cb in range(D // LANE):
        lanes = slice(cb * LANE, (cb + 1) * LANE)

        def body(jq, accs):
            accs = list(accs)
            d = [d_ref[pl.ds(pl.multiple_of((jq * n + jj) * SUB, SUB), SUB), lanes] for jj in range(n)]
            for s, taps in _by_shift(offsets):
                ms = [m for _, m in taps]
                tiles = _tiles(buf, sh, s, min(ms), max(ms) - min(ms) + n, jq * n, lanes)
                for k, m in taps:
                    for jj in range(n):
                        accs[k] = accs[k] + d[jj] * tiles[m + jj]
            return tuple(accs)

        accs = lax.fori_loop(0, tm // (SUB * n), body, tuple(jnp.zeros((SUB, LANE), F32) for _ in offsets))
        for k, acc in enumerate(accs):
            dw_ref[k:k + 1, lanes] += jnp.sum(acc, axis=0, keepdims=True)


def _fwd_a1(x0, w1, b1, l, hosts=()):
    tm = 512

    def body(x_ref, w_ref, b_ref, h_ref, glu_ref):
        xb = x_ref[...].astype(BF16)
        for q in range(NQ):
            sl = slice(q * 512, (q + 1) * 512)
            h_ref[:, sl] = jnp.dot(xb, w_ref[q], preferred_element_type=F32) + b_ref[:, sl]
        glu_ref[...] = h_ref[:, :D] * _sig(h_ref[:, D:])

    return _call(f"fwd_a1_{l}", body, (S // tm,), [_row(tm, D), _wspec(w1), _const((1, 2 * D))],
                 [_row(tm, 2 * D), _row(tm, D)], [_sds((S, 2 * D)), _sds((S, D))], hosts=hosts)(x0, w1, b1)


def _fwd_a2(glu, x0, wdw, bdw, lg, lb, w2, l, hosts=()):
    tm = 256

    def body(g_ref, gp_ref, x_ref, wdw_ref, bdw_ref, lg_ref, lb_ref, w2_ref, z_ref, cv_ref, buf, sh):
        i = pl.program_id(0)
        _fill_halo(buf, 0, HALO_A, lambda: gp_ref[...], i == 0)
        buf[HALO_A:HALO_A + tm, :] = g_ref[...]
        _shift_copies(buf, sh)
        _conv_rows(cv_ref, wdw_ref, bdw_ref, [HALO_A - (CONV_A - 1) + k for k in range(CONV_A)], buf, sh, tm)
        n, _, _ = _ln(cv_ref[...], lg_ref[...], lb_ref[...])
        sb = (n * _sig(n)).astype(BF16)
        z_ref[...] = ALPHA * x_ref[...] + jnp.dot(sb, w2_ref[...], preferred_element_type=F32)

    return _call(f"fwd_a2_{l}", body, (S // tm,),
                 [_row(tm, D), _prev(tm, HALO_A, D), _row(tm, D), _const((32, D)), _const((1, D)), _const((1, D)),
                  _const((1, D)), _const(w2.shape)],
                 [_row(tm, D), _row(tm, D)], [_sds((S, D)), _sds((S, D))],
                 scratch=[pltpu.VMEM((HALO_A + tm, D), F32), pltpu.VMEM((SUB - 1, HALO_A + tm - SUB, D), F32)],
                 hosts=hosts)(glu, glu, x0, wdw, bdw, lg, lb, w2)


def _fwd_b(x0, win, b_in, lg, lb, ws, bsx, wout, hosts=()):
    tm = 256

    def body(x_ref, win_ref, bin_ref, lg_ref, lb_ref, ws_ref, bsx_ref, wout_ref, z_ref, zg_ref, gg_ref, f_scr, h_ref):
        xb = x_ref[...].astype(BF16)
        for q in range(NQ):
            sl = slice(q * 1024, (q + 1) * 1024)
            h_ref[:, sl] = jnp.dot(xb, win_ref[q], preferred_element_type=F32) + bin_ref[:, sl]
        u, du = _gelu_and_grad(h_ref[:, :E])
        v, dv = _gelu_and_grad(h_ref[:, E:])
        zg_ref[:, 0:E] = u.astype(BF16)
        zg_ref[:, E:2 * E] = v.astype(BF16)
        gg_ref[:, 0:E] = du.astype(BF16)
        gg_ref[:, E:2 * E] = dv.astype(BF16)
        vn, _, _ = _ln(v, lg_ref[...], lb_ref[...])
        vnb = vn.astype(BF16)
        mask, _ = _sgu_masks()
        for hd in range(SGU_H):
            wm = jnp.where(mask, ws_ref[hd], 0.0).astype(BF16)
            cs = slice(hd * SGU_G, (hd + 1) * SGU_G)
            for n in range(tm // SGU_T):
                rs = slice(n * SGU_T, (n + 1) * SGU_T)
                f_scr[rs, cs] = jnp.dot(wm, vnb[rs, cs], preferred_element_type=F32) + bsx_ref[hd]
        mb = (u * f_scr[...]).astype(BF16)
        z_ref[...] = ALPHA * x_ref[...] + jnp.dot(mb, wout_ref[...], preferred_element_type=F32)

    return _call("fwd_b", body, (S // tm,),
                 [_row(tm, D), _wspec(win), _const((1, 2 * E)), _const((1, E)), _const((1, E)),
                  _const((SGU_H, SGU_T, SGU_T)), _const((SGU_H, SGU_T, SGU_G)), _const(wout.shape)],
                 [_row(tm, D), _row(tm, 2 * E), _row(tm, 2 * E)],
                 [_sds((S, D)), _sds((S, 2 * E), BF16), _sds((S, 2 * E), BF16)],
                 scratch=[pltpu.VMEM((tm, E), F32), pltpu.VMEM((tm, 2 * E), F32)], hosts=hosts
                 )(x0, win, b_in, lg, lb, ws, bsx, wout)


def _fwd_c1(x0, win, hosts=()):
    tm = 512

    def body(x_ref, w_ref, hc_ref):
        xb = x_ref[...].astype(BF16)
        for q in range(NQ):
            hc_ref[:, q * 768:(q + 1) * 768] = jnp.dot(xb, w_ref[q], preferred_element_type=F32)

    return _call("fwd_c1", body, (S // tm,), [_row(tm, D), _wspec(win)], _row(tm, 3 * D),
                 _sds((S, 3 * D)), hosts=hosts)(x0, win)


def _short_conv(buf, hc_ref, hcp_ref, wc_ref, tm, i):
    _fill_halo(buf, 0, HALO_C, lambda: hcp_ref[:, D:2 * D] * hcp_ref[:, 2 * D:], i == 0)
    buf[HALO_C:HALO_C + tm, :] = hc_ref[:, D:2 * D] * hc_ref[:, 2 * D:]
    y = wc_ref[0:1, :] * buf[pl.ds(HALO_C - 2, tm), :]
    for k in range(1, CONV_C):
        y = y + wc_ref[k:k + 1, :] * buf[pl.ds(HALO_C - 2 + k, tm), :]
    return y


def _fwd_c2(hc, x0, wc, wout, hosts=()):
    tm = 512

    def body(hc_ref, hcp_ref, x_ref, wc_ref, wout_ref, z_ref, buf):
        y = _short_conv(buf, hc_ref, hcp_ref, wc_ref, tm, pl.program_id(0))
        mb = (hc_ref[:, :D] * y).astype(BF16)
        z_ref[...] = ALPHA * x_ref[...] + jnp.dot(mb, wout_ref[...], preferred_element_type=F32)

    return _call("fwd_c2", body, (S // tm,),
                 [_row(tm, 3 * D), _prev(tm, HALO_C, 3 * D), _row(tm, D), _const((8, D)), _const(wout.shape)],
                 _row(tm, D), _sds((S, D)), scratch=[pltpu.VMEM((HALO_C + tm, D), F32)], hosts=hosts
                 )(hc, hc, x0, wc, wout)


def _fwd_ffn(z1, lg, lb, wgt, wut, wd, l, hosts=()):
    tm = 256

    def body(z_ref, lg_ref, lb_ref, wg_ref, wu_ref, wd_ref, o_ref, a_ref, u_ref, hm_ref):
        x1, _, _ = _ln(z_ref[...], lg_ref[...], lb_ref[...])
        xb = x1.astype(BF16)
        a = _mmt(xb, wg_ref[...])
        u = _mmt(xb, wu_ref[...])
        hmb = (a * _sig(a) * u).astype(BF16)
        a_ref[...] = a.astype(BF16)
        u_ref[...] = u.astype(BF16)
        hm_ref[...] = hmb
        o_ref[...] = ALPHA * x1 + jnp.dot(hmb, wd_ref[...], preferred_element_type=F32)

    return _call(f"fwd_ffn_{l}", body, (S // tm,),
                 [_row(tm, D), _const((1, D)), _const((1, D)), _const((FF, D)), _const((FF, D)), _const((FF, D))],
                 [_row(tm, D), _row(tm, FF), _row(tm, FF), _row(tm, FF)],
                 [_sds((S, D)), _sds((S, FF), BF16), _sds((S, FF), BF16), _sds((S, FF), BF16)],
                 hosts=hosts)(z1, lg, lb, wgt, wut, wd)


def _ple_parts(z2, p, lg, lb, wg_ref, wp_ref, pg):
    x2, xhat, rstd = _ln(z2, lg, lb)
    xb = x2.astype(BF16)
    gate = _sig(jnp.dot(xb, wg_ref[...], preferred_element_type=F32))
    pb = p.astype(BF16)
    qp = jnp.concatenate([jnp.dot(pb, wp_ref[q], preferred_element_type=F32) for q in range(NQ)], axis=1)
    rs = lax.rsqrt(jnp.mean(qp * qp, axis=-1, keepdims=True) + LN_EPS)
    qn = qp * rs
    return x2, xhat, rstd, xb, gate, qn, rs, qn * pg


def _fwd_ple(z2, p, lg, lb, wg, wp, pg, l, hosts=()):
    tm = 512

    def body(z_ref, p_ref, lg_ref, lb_ref, wg_ref, wp_ref, pg_ref, o_ref):
        x2, _, _, _, gate, _, _, r = _ple_parts(z_ref[...], p_ref[...], lg_ref[...], lb_ref[...], wg_ref, wp_ref,
                                                pg_ref[...])
        o_ref[...] = x2 + gate * r

    return _call(f"fwd_ple_{l}", body, (S // tm,),
                 [_row(tm, D), _row(tm, 256), _const((1, D)), _const((1, D)), _const(wg.shape), _wspec(wp),
                  _const((1, D))],
                 _row(tm, D), _sds((S, D)), hosts=hosts)(z2, p, lg, lb, wg, wp, pg)


def _loss_head(y, target):
    tm = 512

    def body(y_ref, t_ref, dy_ref, acc_ref):
        @pl.when(pl.program_id(0) == 0)
        def _():
            acc_ref[...] = jnp.zeros_like(acc_ref)

        e = y_ref[...] - t_ref[...]
        dy_ref[...] = e * (1.0 / D)
        acc_ref[0:1, :] += _colsum(e * e)

    return _call("loss_head", body, (S // tm,), [_row(tm, D), _row(tm, D)], [_row(tm, D), _acc(8, D)],
                 [_sds((S, D)), _sds((8, D))])(y, target)


def _zero_first(*refs):
    @pl.when(pl.program_id(0) == 0)
    def _():
        for r in refs:
            r[...] = jnp.zeros_like(r)


def _bwd_ple(g, z2, p, lg, lb, wg, wp, pg, l, hosts=()):
    tm = 512

    def body(g_ref, z_ref, p_ref, lg_ref, lb_ref, wg_ref, wp_ref, pg_ref, dz_ref, xb_ref, dgp_ref, dqp_ref, acc_ref):
        _zero_first(acc_ref)
        gin = g_ref[...]
        lgv, pgv = lg_ref[...], pg_ref[...]
        _, xhat, rstd, xb, gate, qn, rs, r = _ple_parts(z_ref[...], p_ref[...], lgv, lb_ref[...], wg_ref, wp_ref, pgv)
        xb_ref[...] = xb
        dgpb = (gin * r * gate * (1.0 - gate)).astype(BF16)
        dgp_ref[...] = dgpb
        dx2 = gin + _mmt(dgpb, wg_ref[...])
        dr = gin * gate
        acc_ref[0:1, :] += _colsum(dr * qn)
        t = dr * pgv
        dqp_ref[...] = (rs * (t - qn * jnp.mean(t * qn, axis=-1, keepdims=True))).astype(BF16)
        acc_ref[1:2, :] += _colsum(dx2 * xhat)
        acc_ref[2:3, :] += _colsum(dx2)
        dz_ref[...] = _ln_bwd(dx2 * lgv, xhat, rstd)

    return _call(f"bwd_ple_{l}", body, (S // tm,),
                 [_row(tm, D), _row(tm, D), _row(tm, 256), _const((1, D)), _const((1, D)), _const(wg.shape),
                  _wspec(wp), _const((1, D))],
                 [_row(tm, D), _row(tm, D), _row(tm, D), _row(tm, D), _acc(8, D)],
                 [_sds((S, D)), _sds((S, D), BF16), _sds((S, D), BF16), _sds((S, D), BF16), _sds((8, D))],
                 hosts=hosts)(g, z2, p, lg, lb, wg, wp, pg)


def _bwd_ffn(dz2, z1, ab, ub, lg, lb, wgt, wut, wd, l, hosts=()):
    tm = 256

    def body(dz2_ref, z_ref, a_ref, u_ref, lg_ref, lb_ref, wg_ref, wu_ref, wd_ref, dz1_ref, xb_ref, da_ref, du_ref,
             acc_ref):
        _zero_first(acc_ref)
        dz2v = dz2_ref[...]
        lgv = lg_ref[...]
        x1, xhat, rstd = _ln(z_ref[...], lgv, lb_ref[...])
        xb_ref[...] = x1.astype(BF16)
        a = a_ref[...].astype(F32)
        u = u_ref[...].astype(F32)
        sg = _sig(a)
        dhm = _mmt(dz2v, wd_ref[...])
        dub = (dhm * (a * sg)).astype(BF16)
        dab = (dhm * u * _silu_grad(a, sg)).astype(BF16)
        da_ref[...] = dab
        du_ref[...] = dub
        dx1 = ALPHA * dz2v + _mm(dab, wg_ref[...]) + _mm(dub, wu_ref[...])
        acc_ref[0:1, :] += _colsum(dx1 * xhat)
        acc_ref[1:2, :] += _colsum(dx1)
        dz1_ref[...] = _ln_bwd(dx1 * lgv, xhat, rstd)

    return _call(f"bwd_ffn_{l}", body, (S // tm,),
                 [_row(tm, D), _row(tm, D), _row(tm, FF), _row(tm, FF), _const((1, D)), _const((1, D)),
                  _const((FF, D)), _const((FF, D)), _const((FF, D))],
                 [_row(tm, D), _row(tm, D), _row(tm, FF), _row(tm, FF), _acc(8, D)],
                 [_sds((S, D)), _sds((S, D), BF16), _sds((S, FF), BF16), _sds((S, FF), BF16), _sds((8, D))],
                 hosts=hosts)(dz2, z1, ab, ub, lg, lb, wgt, wut, wd)


def _bwd_a2(dz1, cv, lg, lb, w2, l, hosts=()):
    tm = 512

    def body(dz_ref, cv_ref, lg_ref, lb_ref, w2_ref, dcv_ref, sb_ref, acc_ref):
        _zero_first(acc_ref)
        lgv = lg_ref[...]
        n, xhat, rstd = _ln(cv_ref[...], lgv, lb_ref[...])
        sg = _sig(n)
        sb_ref[...] = (n * sg).astype(BF16)
        dzb = dz_ref[...].astype(BF16)
        ds = _mmt(dzb, w2_ref[...])
        dn = ds * _silu_grad(n, sg)
        acc_ref[0:1, :] += _colsum(dn * xhat)
        acc_ref[1:2, :] += _colsum(dn)
        dcv = _ln_bwd(dn * lgv, xhat, rstd)
        acc_ref[2:3, :] += _colsum(dcv)
        dcv_ref[...] = dcv

    return _call(f"bwd_a2_{l}", body, (S // tm,),
                 [_row(tm, D), _row(tm, D), _const((1, D)), _const((1, D)), _const(w2.shape)],
                 [_row(tm, D), _row(tm, D), _acc(8, D)],
                 [_sds((S, D)), _sds((S, D), BF16), _sds((8, D))], hosts=hosts)(dz1, cv, lg, lb, w2)


def _bwd_conv_a(dcv, glu, wdw, l, hosts=()):
    tm = 256
    nb = S // tm

    def body(d_ref, dn_ref, g_ref, gp_ref, w_ref, dglu_ref, dw_ref, bufd, bufx, sh):
        i = pl.program_id(0)
        _zero_first(dw_ref)
        bufd[0:tm, :] = d_ref[...]
        _fill_halo(bufd, tm, HALO_A, lambda: dn_ref[...], i == nb - 1)
        _fill_halo(bufx, 0, HALO_A, lambda: gp_ref[...], i == 0)
        bufx[HALO_A:HALO_A + tm, :] = g_ref[...]
        _shift_copies(bufd, sh)
        _conv_rows(dglu_ref, w_ref, None, [CONV_A - 1 - k for k in range(CONV_A)], bufd, sh, tm)
        _shift_copies(bufx, sh)
        _conv_wgrad(dw_ref, d_ref, [HALO_A - (CONV_A - 1) + k for k in range(CONV_A)], bufx, sh, tm)

    return _call(f"bwd_conv_a_{l}", body, (nb,),
                 [_row(tm, D), _next(tm, HALO_A, D), _row(tm, D), _prev(tm, HALO_A, D), _const((32, D))],
                 [_row(tm, D), _acc(32, D)], [_sds((S, D)), _sds((32, D))],
                 scratch=[pltpu.VMEM((tm + HALO_A, D), F32), pltpu.VMEM((HALO_A + tm, D), F32),
                          pltpu.VMEM((SUB - 1, HALO_A + tm - SUB, D), F32)], hosts=hosts)(dcv, dcv, glu, glu, wdw)


def _bwd_a1(dglu, h, dz1, w1, l, hosts=()):
    tm = 512

    def body(dg_ref, h_ref, dz_ref, w_ref, dx_ref, dh_ref, acc_ref):
        _zero_first(acc_ref)
        a, g = h_ref[:, :D], h_ref[:, D:]
        sg = _sig(g)
        dgl = dg_ref[...]
        da = dgl * sg
        dg = dgl * a * sg * (1.0 - sg)
        acc_ref[0:1, 0:D] += _colsum(da)
        acc_ref[0:1, D:2 * D] += _colsum(dg)
        dh_ref[:, 0:D] = da.astype(BF16)
        dh_ref[:, D:2 * D] = dg.astype(BF16)
        dx = ALPHA * dz_ref[...]
        for q in range(NQ):
            dx = dx + _mmt(dh_ref[:, q * 512:(q + 1) * 512], w_ref[q])
        dx_ref[...] = dx

    return _call(f"bwd_a1_{l}", body, (S // tm,),
                 [_row(tm, D), _row(tm, 2 * D), _row(tm, D), _wspec(w1)],
                 [_row(tm, D), _row(tm, 2 * D), _acc(8, 2 * D)],
                 [_sds((S, D)), _sds((S, 2 * D), BF16), _sds((8, 2 * D))], hosts=hosts)(dglu, h, dz1, w1)


def _bwd_c2(dz1, hc, wc, wout):
    tm = 512

    def body(dz_ref, hc_ref, hcp_ref, wc_ref, wout_ref, dy_ref, dbg_ref, mb_ref, buf):
        y = _short_conv(buf, hc_ref, hcp_ref, wc_ref, tm, pl.program_id(0))
        dzb = dz_ref[...].astype(BF16)
        dm = _mmt(dzb, wout_ref[...])
        bg = hc_ref[:, :D]
        mb_ref[...] = (bg * y).astype(BF16)
        dbg_ref[...] = (dm * y).astype(BF16)
        dy_ref[...] = dm * bg

    return _call("bwd_c2", body, (S // tm,),
                 [_row(tm, D), _row(tm, 3 * D), _prev(tm, HALO_C, 3 * D), _const((8, D)), _const(wout.shape)],
                 [_row(tm, D), _row(tm, D), _row(tm, D)],
                 [_sds((S, D)), _sds((S, D), BF16), _sds((S, D), BF16)],
                 scratch=[pltpu.VMEM((HALO_C + tm, D), F32)])(dz1, hc, hc, wc, wout)


def _bwd_c1(dy, hc, dbg, dz1, wc, win):
    tm = 256
    nb = S // tm

    def body(d_ref, dn_ref, hc_ref, hcp_ref, dbg_ref, dz_ref, wc_ref, win_ref, dx_ref, dhc_ref, dwc_ref, bufd, bufq):
        i = pl.program_id(0)
        _zero_first(dwc_ref)
        bufd[0:tm, :] = d_ref[...]
        _fill_halo(bufd, tm, HALO_C, lambda: dn_ref[...], i == nb - 1)
        _fill_halo(bufq, 0, HALO_C, lambda: hcp_ref[:, D:2 * D] * hcp_ref[:, 2 * D:], i == 0)
        bufq[HALO_C:HALO_C + tm, :] = hc_ref[:, D:2 * D] * hc_ref[:, 2 * D:]
        dq = wc_ref[0:1, :] * bufd[pl.ds(CONV_C - 1, tm), :]
        for k in range(1, CONV_C):
            dq = dq + wc_ref[k:k + 1, :] * bufd[pl.ds(CONV_C - 1 - k, tm), :]
        dv = d_ref[...]
        for k in range(CONV_C):
            dwc_ref[k:k + 1, :] += _colsum(dv * bufq[pl.ds(HALO_C - (CONV_C - 1) + k, tm), :])
        dhc_ref[:, 0:D] = dbg_ref[...]
        dhc_ref[:, D:2 * D] = (dq * hc_ref[:, 2 * D:]).astype(BF16)
        dhc_ref[:, 2 * D:3 * D] = (dq * hc_ref[:, D:2 * D]).astype(BF16)
        dx = ALPHA * dz_ref[...]
        for q in range(NQ):
            dx = dx + _mmt(dhc_ref[:, q * 768:(q + 1) * 768], win_ref[q])
        dx_ref[...] = dx

    return _call("bwd_c1", body, (nb,),
                 [_row(tm, D), _next(tm, HALO_C, D), _row(tm, 3 * D), _prev(tm, HALO_C, 3 * D), _row(tm, D),
                  _row(tm, D), _const((8, D)), _wspec(win)],
                 [_row(tm, D), _row(tm, 3 * D), _acc(8, D)],
                 [_sds((S, D)), _sds((S, 3 * D), BF16), _sds((8, D))],
                 scratch=[pltpu.VMEM((tm + HALO_C, D), F32), pltpu.VMEM((HALO_C + tm, D), F32)]
                 )(dy, dy, hc, hc, dbg, dz1, wc, win)


def _bwd_b(dz1, zg, gg, lg, lb, win, wout, ws, wst, bsx, hosts=()):
    tm = 256
    nb = S // tm

    def body(dz_ref, zg_ref, gg_ref, lg_ref, lb_ref, win_ref, wout_ref, ws_ref, wst_ref, bsx_ref,
             dx_ref, dh_ref, mb_ref, acc_ref, dws_ref, dbs_ref, f_scr, dvn_scr):
        _zero_first(acc_ref, dws_ref, dbs_ref)
        lgv = lg_ref[...]
        u = zg_ref[:, :E].astype(F32)
        v = zg_ref[:, E:].astype(F32)
        vn, xhat, rstd = _ln(v, lgv, lb_ref[...])
        vnb = vn.astype(BF16)
        dzb = dz_ref[...].astype(BF16)
        dm = _mmt(dzb, wout_ref[...])
        mask, mask_t = _sgu_masks()
        for hd in range(SGU_H):
            wm = jnp.where(mask, ws_ref[hd], 0.0).astype(BF16)
            cs = slice(hd * SGU_G, (hd + 1) * SGU_G)
            for n in range(tm // SGU_T):
                rs = slice(n * SGU_T, (n + 1) * SGU_T)
                f_scr[rs, cs] = jnp.dot(wm, vnb[rs, cs], preferred_element_type=F32) + bsx_ref[hd]
        f = f_scr[...]
        mb_ref[...] = (u * f).astype(BF16)
        du = dm * f
        df = dm * u
        dfb = df.astype(BF16)
        for hd in range(SGU_H):
            wmt = jnp.where(mask_t, wst_ref[hd], 0.0).astype(BF16)
            cs = slice(hd * SGU_G, (hd + 1) * SGU_G)
            for n in range(tm // SGU_T):
                rs = slice(n * SGU_T, (n + 1) * SGU_T)
                dvn_scr[rs, cs] = jnp.dot(wmt, dfb[rs, cs], preferred_element_type=F32)
                dws_ref[hd] += lax.dot_general(dfb[rs, cs], vnb[rs, cs], (((1,), (1,)), ((), ())),
                                               preferred_element_type=F32)
                dbs_ref[hd] += df[rs, cs]
        dvn = dvn_scr[...]
        acc_ref[1:2, 0:E] += _colsum(dvn * xhat)
        acc_ref[2:3, 0:E] += _colsum(dvn)
        dv = _ln_bwd(dvn * lgv, xhat, rstd)
        dhu = du * gg_ref[:, :E].astype(F32)
        dhv = dv * gg_ref[:, E:].astype(F32)
        acc_ref[0:1, 0:E] += _colsum(dhu)
        acc_ref[0:1, E:2 * E] += _colsum(dhv)
        dh_ref[:, 0:E] = dhu.astype(BF16)
        dh_ref[:, E:2 * E] = dhv.astype(BF16)
        dx = ALPHA * dz_ref[...]
        for q in range(NQ):
            dx = dx + _mmt(dh_ref[:, q * 1024:(q + 1) * 1024], win_ref[q])
        dx_ref[...] = dx

        @pl.when(pl.program_id(0) == nb - 1)
        def _():
            for hd in range(SGU_H):
                dws_ref[hd] = jnp.where(mask, dws_ref[hd], 0.0)

    c3 = lambda a, b, c: pl.BlockSpec((a, b, c), lambda i: (0, 0, 0))
    return _call("bwd_b", body, (nb,),
                 [_row(tm, D), _row(tm, 2 * E), _row(tm, 2 * E), _const((1, E)), _const((1, E)), _wspec(win),
                  _const(wout.shape), _const((SGU_H, SGU_T, SGU_T)), _const((SGU_H, SGU_T, SGU_T)),
                  _const((SGU_H, SGU_T, SGU_G))],
                 [_row(tm, D), _row(tm, 2 * E), _row(tm, E), _acc(8, 2 * E), c3(SGU_H, SGU_T, SGU_T),
                  c3(SGU_H, SGU_T, SGU_G)],
                 [_sds((S, D)), _sds((S, 2 * E), BF16), _sds((S, E), BF16), _sds((8, 2 * E)),
                  _sds((SGU_H, SGU_T, SGU_T)), _sds((SGU_H, SGU_T, SGU_G))],
                 scratch=[pltpu.VMEM((tm, E), F32), pltpu.VMEM((tm, E), F32)], hosts=hosts
                 )(dz1, zg, gg, lg, lb, win, wout, ws, wst, bsx)


def _mm_tn(name, a, amode, b, bmode, k, n, groups=NQ, hosts=()):
    def block_bytes(ts):
        ka = k if amode == "1" else groups * k
        nb = n if bmode == "1" else groups * n
        return 2 * (ts * ka * a.dtype.itemsize + ts * nb * b.dtype.itemsize + groups * k * n * 4)

    ts = min(1024 if block_bytes(1024) <= DW_BLOCK_BUDGET else 512, S)

    def spec(mode, w):
        if mode == "1":
            return pl.BlockSpec((ts, w), lambda s: (s, 0))
        if mode == "c":
            return pl.BlockSpec((ts, groups * w), lambda s: (s, 0))
        return pl.BlockSpec((groups, ts, w), lambda s: (0, s, 0))

    def pick(ref, mode, w, g):
        if mode == "1":
            return ref[...]
        if mode == "c":
            return ref[:, g * w:(g + 1) * w]
        return ref[g]

    def body(a_ref, b_ref, o_ref):
        _zero_first(o_ref)
        a_t = jnp.transpose(a_ref[...].astype(BF16)) if amode == "1" else None
        b_1 = b_ref[...].astype(BF16) if bmode == "1" else None
        for g in range(groups):
            lhs = a_t if amode == "1" else jnp.transpose(pick(a_ref, amode, k, g).astype(BF16))
            rhs = b_1 if bmode == "1" else pick(b_ref, bmode, n, g).astype(BF16)
            o_ref[0, g] += jnp.dot(lhs, rhs, preferred_element_type=F32)

    return _call(name, body, (S // ts,), [spec(amode, k), spec(bmode, n)],
                 pl.BlockSpec((1, groups, k, n), lambda s: (0, 0, 0, 0)), _sds((1, groups, k, n)), hosts=hosts)(a, b)


def _row_block(k, cap=256):
    return max(t for t in range(16, min(k, cap) + 1, 16) if k % t == 0)


def _cast_bf16(w, hosts=()):
    nl, k, n = w.shape
    tb = _row_block(k, 512)
    nb = k // tb

    def body(w_ref, o_ref):
        o_ref[...] = w_ref[...].astype(BF16)

    spec = pl.BlockSpec((None, tb, n), lambda i: (i // nb, i % nb, 0))
    return _call("cast_bf16", body, (nl * nb,), [spec], spec, _sds(w.shape, BF16), hosts=hosts)(w)


def _adam(name, w, m, v, gc, l, prev):
    nl, k, n = w.shape
    nc = gc.shape[0]
    tb = _row_block(k, 512)

    def body(w_ref, m_ref, v_ref, g_ref, *rest):
        go_ref, d_ref, mo_ref, vo_ref = rest[-4:]
        g = g_ref[0].astype(F32)
        for c in range(1, nc):
            g = g + g_ref[c].astype(F32)
        m2 = ADAM_B1 * m_ref[...] + (1.0 - ADAM_B1) * g
        v2 = ADAM_B2 * v_ref[...] + (1.0 - ADAM_B2) * (g * g)
        m_hat = m2 / (1.0 - ADAM_B1 ** ADAM_STEP)
        v_hat = v2 / (1.0 - ADAM_B2 ** ADAM_STEP)
        go_ref[...] = g
        d_ref[...] = -ADAM_LR * (m_hat / (jnp.sqrt(v_hat) + ADAM_EPS) + ADAM_WD * w_ref[...])
        mo_ref[...] = m2
        vo_ref[...] = v2

    spec = pl.BlockSpec((None, tb, n), lambda i: (l, i, 0))
    gspec = pl.BlockSpec((nc, None, tb, n), lambda i: (0, 0, i, 0))
    in_specs, args, aliases = [spec, spec, spec, gspec], [w, m, v, gc], {}
    if prev is not None:
        in_specs += [pl.BlockSpec(memory_space=pl.ANY)] * 4
        args += list(prev)
        aliases = {4 + j: j for j in range(4)}
    return _call(name, body, (k // tb,), in_specs, [spec] * 4, [_sds(w.shape)] * 4, aliases=aliases)(*args)


def _sum8(name, g8):
    r = g8.shape[1]

    def body(g_ref, o_ref):
        acc = g_ref[0]
        for d in range(1, 8):
            acc = acc + g_ref[d]
        o_ref[...] = acc

    return _call(name, body, (1,), [pl.BlockSpec((8, r, 128), lambda i: (0, 0, 0))],
                 pl.BlockSpec((r, 128), lambda i: (0, 0)), _sds((r, 128)))(g8)


def _place():
    x, y, c = lax.axis_index("x"), lax.axis_index("y"), lax.axis_index("c")
    return x, y, c, 2 * x + y, (x, y, 1 - c), [(1 - x, y), (x, 1 - y), (1 - x, 1 - y)]


class _Exchange:
    def __init__(self, arrays, out_shapes):
        self.arrays, self.out_shapes = list(arrays), list(out_shapes)
        n = len(self.arrays)
        self.sems = [pltpu.SemaphoreType.DMA((7 * n,)), pltpu.SemaphoreType.DMA((7 * n,)),
                     pltpu.SemaphoreType.DMA((n,))]

    def _copies(self, ins, outs, sems):
        send, recv, lsem = sems
        local_src, remote_src, dst = self.maps(ins, outs)
        x, y, c, q, sib, chips = _place()

        def rcopy(w, k, qq, cc, to, src=None):
            return pltpu.make_async_remote_copy(
                src_ref=dst(w, qq, cc) if src is None else src, dst_ref=dst(w, qq, cc),
                send_sem=send.at[7 * w + k], recv_sem=recv.at[7 * w + k], device_id=to, device_id_type=MESH)

        def mine(w):
            return pltpu.make_async_copy(local_src(w), dst(w, q, c), lsem.at[w])

        def first(w):
            return [rcopy(w, 0, q, c, sib, local_src(w))] + [
                rcopy(w, 1 + j, q, c, (cx, cy, c), remote_src(w, 2 * cx + cy)) for j, (cx, cy) in enumerate(chips)]

        return rcopy, mine, first, (x, y, c), q, c, sib, chips

    def start(self, ins, outs, sems):
        _, mine, first, *_ = self._copies(ins, outs, sems)
        for w in range(len(self.arrays)):
            mine(w).start()
            for cp in first(w):
                cp.start()

    def forward_steps(self, n_steps):
        sizes = [a.size // a.shape[0] for a in self.arrays]
        plan, moved = {}, 0
        for w, size in enumerate(sizes):
            moved += size
            plan.setdefault(min(n_steps - 1, -(-moved * n_steps // sum(sizes))), []).append(w)
        return plan

    def forward(self, ws, ins, outs, sems):
        rcopy, _, _, me, _, c, sib, chips = self._copies(ins, outs, sems)
        for w in ws:
            for j, (cx, cy) in enumerate(chips):
                rcopy(w, 1 + j, 2 * cx + cy, c, me).wait_recv()
                rcopy(w, 4 + j, 2 * cx + cy, c, sib).start()

    def complete(self, ins, outs, sems):
        rcopy, mine, first, me, q, c, sib, chips = self._copies(ins, outs, sems)
        n = len(self.arrays)
        for w in range(n):
            rcopy(w, 0, q, 1 - c, me).wait_recv()
            for j, (cx, cy) in enumerate(chips):
                rcopy(w, 4 + j, 2 * cx + cy, 1 - c, me).wait_recv()
        for w in range(n):
            for cp in first(w):
                cp.wait_send()
            for j, (cx, cy) in enumerate(chips):
                rcopy(w, 4 + j, 2 * cx + cy, c, sib).wait_send()
            mine(w).wait()


class _GatherWeights(_Exchange):
    def __init__(self, items):
        self.layers = [l for _, l in items]
        self.kh = [s.shape[1] // 2 for s, _ in items]
        super().__init__([s for s, _ in items], [_sds((NQ, 1) + s.shape[1:], BF16) for s, _ in items])

    def maps(self, ins, outs):
        c = lax.axis_index("c")
        src = lambda w: ins[w].at[pl.ds(self.layers[w], 1), pl.ds(c * self.kh[w], self.kh[w]), :]
        return src, lambda w, q: src(w), lambda w, q, cc: outs[w].at[q, :, pl.ds(cc * self.kh[w], self.kh[w]), :]


class _ScatterPartials(_Exchange):
    def __init__(self, parts):
        super().__init__(parts, [_sds((NQ, 1, 2) + p.shape[2:], BF16) for p in parts])

    def maps(self, ins, outs):
        q = 2 * lax.axis_index("x") + lax.axis_index("y")
        return (lambda w: ins[w].at[:, q]), (lambda w, qq: ins[w].at[:, qq]), (lambda w, qq, cc: outs[w].at[qq, :, cc])


class _Gather8(_Exchange):
    def __init__(self, v):
        super().__init__([v], [_sds((8,) + v.shape)])

    def maps(self, ins, outs):
        return (lambda w: ins[0]), (lambda w, q: ins[0]), (lambda w, q, cc: outs[0].at[2 * q + cc])


class _SwapHalves:
    def __init__(self, dws):
        self.arrays = list(dws)
        self.kh = [d.shape[2] // 2 for d in dws]
        self.out_shapes = [_sds(d.shape[:2] + (kh,) + d.shape[3:]) for d, kh in zip(dws, self.kh)]
        self.sems = [pltpu.SemaphoreType.DMA((len(dws),)), pltpu.SemaphoreType.DMA((len(dws),))]

    def _copies(self, ins, outs, sems):
        send, recv = sems
        _, _, c, _, sib, _ = _place()
        return [pltpu.make_async_remote_copy(
            src_ref=ins[w].at[:, :, pl.ds((1 - c) * self.kh[w], self.kh[w]), :], dst_ref=outs[w],
            send_sem=send.at[w], recv_sem=recv.at[w], device_id=sib, device_id_type=MESH)
            for w in range(len(self.arrays))]

    def start(self, ins, outs, sems):
        for cp in self._copies(ins, outs, sems):
            cp.start()

    def forward_steps(self, n_steps):
        return {}

    def complete(self, ins, outs, sems):
        for cp in self._copies(ins, outs, sems):
            cp.wait()


def _comm_only(name, host):
    n_in, n_out = len(host.arrays), len(host.out_shapes)

    def body(*refs):
        ins, outs, sems = refs[:n_in], refs[n_in:n_in + n_out], refs[n_in + n_out:]
        host.start(ins, outs, sems)
        for ws in host.forward_steps(1).values():
            host.forward(ws, ins, outs, sems)
        host.complete(ins, outs, sems)

    any_spec = pl.BlockSpec(memory_space=pl.ANY)
    return pl.pallas_call(body, name=name, in_specs=[any_spec] * n_in, out_specs=[any_spec] * n_out,
                          out_shape=host.out_shapes, scratch_shapes=host.sems)(*host.arrays)


def _add_halves(dw, got, cidx):
    nl, _, k, n = dw.shape
    kh = k // 2
    qb = 2

    def body(c_ref, a_ref, b_ref, o_ref):
        o_ref[...] = (a_ref[...] + b_ref[...]).astype(BF16)

    grid_spec = pltpu.PrefetchScalarGridSpec(
        num_scalar_prefetch=1, grid=(nl, NQ // qb),
        in_specs=[pl.BlockSpec((None, qb, None, kh, n), lambda l, q, c_ref: (l, q, c_ref[0], 0, 0)),
                  pl.BlockSpec((None, qb, kh, n), lambda l, q, c_ref: (l, q, 0, 0))],
        out_specs=pl.BlockSpec((None, qb, kh, n), lambda l, q, c_ref: (l, q, 0, 0)))
    return pl.pallas_call(
        body, name="add_halves", grid_spec=grid_spec, out_shape=_sds((nl, NQ, kh, n), BF16),
        compiler_params=pltpu.CompilerParams(dimension_semantics=("arbitrary", "arbitrary"),
                                             vmem_limit_bytes=VMEM_LIMIT))(cidx, dw.reshape(nl, NQ, 2, kh, n), got)


def _gather8(name, v):
    return _comm_only(name, _Gather8(v))[0]


PACK = 16 * 128


def _pack(arrays):
    parts = []
    for a in arrays:
        flat = a.reshape(-1)
        parts.append(jnp.pad(flat, (0, (-flat.shape[0]) % PACK)))
    return jnp.concatenate(parts).reshape(-1, 128)


def _unpack(packed, shapes):
    flat = packed.reshape(-1)
    out, off = [], 0
    for shp in shapes:
        size = 1
        for d in shp:
            size *= d
        out.append(flat[off:off + size].reshape(shp))
        off += size + (-size) % PACK
    return out


def kernel(x, p, a_w_pw1, a_b_pw1, a_w_dw, a_b_dw, a_ln_g, a_ln_b, a_w_pw2, b_w_in, b_b_in, b_ln_g, b_ln_b, b_w_s, b_b_s, b_w_out, c_w_in, c_w_conv, c_w_out, ln1_g, ln1_b, ln2_g, ln2_b, ffn_w_gate, ffn_w_up, ffn_w_down, ple_w_gate, ple_w_proj, ple_norm_g, loss_target, m_a_w_pw1, m_a_b_pw1, m_a_w_dw, m_a_b_dw, m_a_ln_g, m_a_ln_b, m_a_w_pw2, m_b_w_in, m_b_b_in, m_b_ln_g, m_b_ln_b, m_b_w_s, m_b_b_s, m_b_w_out, m_c_w_in, m_c_w_conv, m_c_w_out, m_ln1_g, m_ln1_b, m_ln2_g, m_ln2_b, m_ffn_w_gate, m_ffn_w_up, m_ffn_w_down, m_ple_w_gate, m_ple_w_proj, m_ple_norm_g, v_a_w_pw1, v_a_b_pw1, v_a_w_dw, v_a_b_dw, v_a_ln_g, v_a_ln_b, v_a_w_pw2, v_b_w_in, v_b_b_in, v_b_ln_g, v_b_ln_b, v_b_w_s, v_b_b_s, v_b_w_out, v_c_w_in, v_c_w_conv, v_c_w_out, v_ln1_g, v_ln1_b, v_ln2_g, v_ln2_b, v_ffn_w_gate, v_ffn_w_up, v_ffn_w_down, v_ple_w_gate, v_ple_w_proj, v_ple_norm_g):
    args = dict(locals())
    wts = {k: args[k] for k in WEIGHTS}
    mom = {k: args["m_" + k] for k in WEIGHTS}
    var = {k: args["v_" + k] for k in WEIGHTS}
    for k in TRANSPOSED:
        wts[k], mom[k], var[k] = (jnp.transpose(t[k], (0, 2, 1)) for t in (wts, mom, var))
    q_idx = 2 * lax.axis_index("x") + lax.axis_index("y")
    c_idx = lax.axis_index("c").astype(jnp.int32).reshape(1)

    wb = {k: _cast_bf16(wts[k]) for k in BIG if k not in ("ffn_w_gate", "ffn_w_up")}
    mixw = [[("a_w_pw1", 0), ("a_w_pw2", 0)], [("b_w_in", 0), ("b_w_out", 0)], [("c_w_in", 0), ("c_w_out", 0)],
            [("a_w_pw1", 1), ("a_w_pw2", 1)]]
    ffnw = [[("ffn_w_gate", l), ("ffn_w_up", l), ("ffn_w_down", l)] for l in range(DEPTH)]
    plew = [[("ple_w_gate", l), ("ple_w_proj", l)] for l in range(DEPTH)]
    fwd_plan = {("a1", 0): mixw[0][1:] + plew[0], ("a2", 0): ffnw[0], ("ffn", 0): mixw[1] + plew[1],
                ("b", 1): ffnw[1], ("ffn", 1): mixw[2] + plew[2] + ffnw[2][:1],
                ("c1", 2): ffnw[2][1:2], ("c2", 2): ffnw[2][2:], ("ffn", 2): mixw[3] + plew[3] + ffnw[3][:1],
                ("a2", 3): ffnw[3][1:]}
    gw = {}

    def gather(keys):
        return _GatherWeights([(wb[name], l) for name, l in keys])

    def hosted(tag, fn, *fargs):
        keys = fwd_plan.get(tag)
        if not keys:
            return fn(*fargs)
        own, (got,) = fn(*fargs, hosts=[gather(keys)])
        store(keys, got)
        return own

    def store(keys, got):
        for (name, l), arr in zip(keys, got):
            gw[name, l] = arr.reshape(NQ * arr.shape[2], arr.shape[3]) if name in ROW_SHARDED else arr

    first_keys = mixw[0][:1]
    wb["ffn_w_gate"], (got,) = _cast_bf16(wts["ffn_w_gate"], hosts=[gather(first_keys)])
    store(first_keys, got)
    shard_shapes = [wts[k].shape for k in SMALL_SHARDED]
    wb["ffn_w_up"], ((small8,),) = _cast_bf16(wts["ffn_w_up"], hosts=[_Gather8(_pack([wts[k] for k in SMALL_SHARDED]))])
    per_chip = [_unpack(small8[2 * qq], shard_shapes) for qq in range(NQ)]
    full = {k: jnp.concatenate([per_chip[qq][i] for qq in range(NQ)], axis=-1) for i, k in enumerate(SMALL_SHARDED)}
    for k in SMALL_REPL:
        full[k] = wts[k]

    def vec(name, l):
        return full[name][l][None, :]

    def conv_w(name, l, rows):
        w = full[name][l]
        return jnp.pad(w, ((0, rows - w.shape[0]), (0, 0)))

    ws = full["b_w_s"][0]
    wst = jnp.transpose(ws, (0, 2, 1))
    bsx = jnp.broadcast_to(full["b_b_s"][0][:, :, None], (SGU_H, SGU_T, SGU_G))

    x0s, z1s, z2s, saved, ffn_saved = [], [], [], [], []
    cur = x[0]
    for i in range(DEPTH):
        mix, j = i % 3, i // 3
        x0s.append(cur)
        if mix == 0:
            h, glu = hosted(("a1", i), _fwd_a1, cur, gw["a_w_pw1", j], vec("a_b_pw1", j), i)
            z1, cv = hosted(("a2", i), _fwd_a2, glu, cur, conv_w("a_w_dw", j, 32), vec("a_b_dw", j), vec("a_ln_g", j),
                            vec("a_ln_b", j), gw["a_w_pw2", j], i)
            saved.append((h, glu, cv))
        elif mix == 1:
            z1, zg, gg = hosted(("b", i), _fwd_b, cur, gw["b_w_in", 0], vec("b_b_in", 0), vec("b_ln_g", 0),
                                vec("b_ln_b", 0), ws, bsx, gw["b_w_out", 0])
            saved.append((zg, gg))
        else:
            hc = hosted(("c1", i), _fwd_c1, cur, gw["c_w_in", 0])
            z1 = hosted(("c2", i), _fwd_c2, hc, cur, conv_w("c_w_conv", 0, 8), gw["c_w_out", 0])
            saved.append((hc,))
        z2, ab, ub, hm = hosted(("ffn", i), _fwd_ffn, z1, vec("ln1_g", i), vec("ln1_b", i), gw["ffn_w_gate", i],
                                gw["ffn_w_up", i], gw["ffn_w_down", i], i)
        ffn_saved.append((ab, ub, hm))
        cur = hosted(("ple", i), _fwd_ple, z2, p[i, 0], vec("ln2_g", i), vec("ln2_b", i), gw["ple_w_gate", i],
                     gw["ple_w_proj", i], vec("ple_norm_g", i), i)
        z1s.append(z1)
        z2s.append(z2)

    g, loss_acc = _loss_head(cur, loss_target[0])
    loss = lax.psum(0.5 / D * jnp.sum(loss_acc[0]), ("x", "y", "c"))

    dws = {}
    sg = {}
    res = {k: None for k in BIG}

    def wgrad(name, l, a, amode, b, bmode, scatter_keys=()):
        _, k, n = wts[name].shape
        hosts = [_ScatterPartials([parts[key] for key in scatter_keys])] if scatter_keys else ()
        if name in ROW_SHARDED:
            out = _mm_tn(f"dw_{name}_{l}", a, "1", b, "1", NQ * k, n, groups=1, hosts=hosts)
        else:
            out = _mm_tn(f"dw_{name}_{l}", a, amode, b, bmode, k, n, hosts=hosts)
        if scatter_keys:
            out, (contribs,) = out
            update(scatter_keys, contribs)
        dws[name, l] = out.reshape(1, NQ, k, n)

    def swap(keys):
        return _SwapHalves([dws[k] for k in keys])

    parts = {}

    def add_halves(keys, got):
        parts.update((k, _add_halves(dws[k], r, c_idx)) for k, r in zip(keys, got))

    def update(keys, contribs):
        for (name, l), gc in zip(keys, contribs):
            _, kq, n = wts[name].shape
            res[name] = _adam(f"adam_{name}_{l}", wts[name], mom[name], var[name], gc.reshape(NQ, 1, kq, n), l,
                              res[name])

    small = SMALL_SHARDED + SMALL_REPL
    late_small = [("a_b_pw1", 0)]
    early_small = [(k, l) for k in small for l in range(full[k].shape[0]) if (k, l) not in late_small]
    pending = None
    for i in reversed(range(DEPTH)):
        mix, j = i % 3, i // 3
        ple_args = (g, z2s[i], p[i, 0], vec("ln2_g", i), vec("ln2_b", i), gw["ple_w_gate", i], gw["ple_w_proj", i],
                    vec("ple_norm_g", i), i)
        if pending:
            (dz2, x2b, dgp, dqp, acc), (got,) = _bwd_ple(*ple_args, hosts=[swap(pending)])
            add_halves(pending, got)
        else:
            dz2, x2b, dgp, dqp, acc = _bwd_ple(*ple_args)
        sg["ple_norm_g", i], sg["ln2_g", i], sg["ln2_b", i] = acc[0], acc[1], acc[2]
        wgrad("ple_w_gate", i, x2b, "c", dgp, "1")
        wgrad("ple_w_proj", i, p[i, 0], "1", dqp, "c")
        ab, ub, hm = ffn_saved[i]
        ffn_args = (dz2, z1s[i], ab, ub, vec("ln1_g", i), vec("ln1_b", i), gw["ffn_w_gate", i], gw["ffn_w_up", i],
                    gw["ffn_w_down", i], i)
        if pending:
            (dz1, x1b, da, du, acc), (contribs,) = _bwd_ffn(
                *ffn_args, hosts=[_ScatterPartials([parts[key] for key in ffnw[i + 1]])])
            update(ffnw[i + 1], contribs)
        else:
            dz1, x1b, da, du, acc = _bwd_ffn(*ffn_args)
        sg["ln1_g", i], sg["ln1_b", i] = acc[0], acc[1]
        behind_mixer = mix == 1
        rest = mixw[i + 1] + plew[i + 1] if pending else []
        wgrad("ffn_w_gate", i, da, "1", x1b, "1", scatter_keys=() if behind_mixer else rest[1:])
        wgrad("ffn_w_up", i, du, "1", x1b, "1")
        wgrad("ffn_w_down", i, hm, "1", dz2, "1", scatter_keys=() if behind_mixer else rest[:1])
        x0 = x0s[i]
        if mix == 0:
            h, glu, cv = saved[i]
            a2_args = (dz1, cv, vec("a_ln_g", j), vec("a_ln_b", j), gw["a_w_pw2", j], i)
            conv_args = (glu, conv_w("a_w_dw", j, 32), i)
            if i == 0:
                early = ffnw[0] + plew[0]
                (dcv, sb, acc), (got,) = _bwd_a2(*a2_args, hosts=[swap(early)])
                sg["a_ln_g", j], sg["a_ln_b", j], sg["a_b_dw", j] = acc[0], acc[1], acc[2]
                add_halves(early, got)
                wgrad("a_w_pw2", j, sb, "c", dz1, "1")
                (dglu, dwdw), (contribs, got) = _bwd_conv_a(
                    dcv, *conv_args, hosts=[_ScatterPartials([parts[key] for key in early]), swap(mixw[0][1:])])
                update(early, contribs)
                add_halves(mixw[0][1:], got)
                sg["a_w_dw", j] = dwdw[:CONV_A]
                (g, dh, acc), (contribs, (g8_early,)) = _bwd_a1(
                    dglu, h, dz1, gw["a_w_pw1", j], i,
                    hosts=[_ScatterPartials([parts[key] for key in mixw[0][1:]]),
                           _Gather8(_pack([sg[pc] for pc in early_small]))])
                update(mixw[0][1:], contribs)
            else:
                dcv, sb, acc = _bwd_a2(*a2_args)
                sg["a_ln_g", j], sg["a_ln_b", j], sg["a_b_dw", j] = acc[0], acc[1], acc[2]
                dglu, dwdw = _bwd_conv_a(dcv, *conv_args)
                wgrad("a_w_pw2", j, sb, "c", dz1, "1")
                sg["a_w_dw", j] = dwdw[:CONV_A]
                g, dh, acc = _bwd_a1(dglu, h, dz1, gw["a_w_pw1", j], i)
            sg["a_b_pw1", j] = acc[0]
            wgrad("a_w_pw1", j, x0, "1", dh, "c")
        elif mix == 1:
            zg, gg = saved[i]
            (g, dh, mb, acc, dw_s, db_s), (contribs,) = _bwd_b(
                dz1, zg, gg, vec("b_ln_g", 0), vec("b_ln_b", 0), gw["b_w_in", 0], gw["b_w_out", 0], ws, wst, bsx,
                hosts=[_ScatterPartials([parts[key] for key in rest])])
            update(rest, contribs)
            sg["b_b_in", 0], sg["b_ln_g", 0], sg["b_ln_b", 0] = acc[0], acc[1, :E], acc[2, :E]
            sg["b_w_s", 0], sg["b_b_s", 0] = dw_s, jnp.sum(db_s, axis=-1)
            wgrad("b_w_out", 0, mb, "c", dz1, "1")
            wgrad("b_w_in", 0, x0, "1", dh, "c")
        else:
            (hc,) = saved[i]
            wc = conv_w("c_w_conv", 0, 8)
            dy, dbg, mb = _bwd_c2(dz1, hc, wc, gw["c_w_out", 0])
            wgrad("c_w_out", 0, mb, "c", dz1, "1")
            g, dhc, dwc = _bwd_c1(dy, hc, dbg, dz1, wc, gw["c_w_in", 0])
            sg["c_w_conv", 0] = dwc[:CONV_C]
            wgrad("c_w_in", 0, x0, "1", dhc, "c")
        pending = mixw[i] + ffnw[i] + plew[i] if i > 0 else mixw[0][:1]
    grad_x = g[None]
    add_halves(pending, _comm_only("swap_last", swap(pending)))
    update(pending, _comm_only("scatter_last", _ScatterPartials([parts[key] for key in pending])))

    g8_late = _gather8("gather_small_late", _pack([sg[pc] for pc in late_small]))
    sums = dict(zip(early_small, _unpack(_sum8("sum8_early", g8_early), [sg[pc].shape for pc in early_small])))
    sums.update(zip(late_small, _unpack(_sum8("sum8_late", g8_late), [sg[pc].shape for pc in late_small])))
    gsum = [jnp.stack([sums[k, l] for l in range(full[k].shape[0])]) for k in small]
    gmine = []
    for k, gs in zip(small, gsum):
        if k in SMALL_SHARDED:
            wdt = wts[k].shape[-1]
            gs = lax.dynamic_slice_in_dim(gs, q_idx * wdt, wdt, axis=gs.ndim - 1)
        gmine.append(gs)
    packed = [_pack(t)[None] for t in ([wts[k] for k in small], [mom[k] for k in small], [var[k] for k in small])]
    outs = _adam("adam_small", packed[0], packed[1], packed[2], _pack(gmine)[None, None], 0, None)
    unpacked = [_unpack(o[0], [wts[k].shape for k in small]) for o in outs]
    for i, k in enumerate(small):
        res[k] = tuple(u[i] for u in unpacked)

    for k in TRANSPOSED:
        res[k] = tuple(jnp.transpose(r, (0, 2, 1)) for r in res[k])
    return (loss, grad_x, *[res[k][0] for k in WEIGHTS], *[res[k][1] for k in WEIGHTS],
            *[res[k][2] for k in WEIGHTS], *[res[k][3] for k in WEIGHTS])
```

```python
import functools

import jax
import jax.numpy as jnp
from jax import lax
from jax.experimental import pallas as pl
from jax.experimental.pallas import tpu as pltpu

F32, BF16 = jnp.float32, jnp.bfloat16
S = 4096
D = 1024
E = 2048
FF = 2816
FQ = FF // 4
NQ = 4
DEPTH = 4
ALPHA = (2 * DEPTH) ** 0.25
LN_EPS = 1e-5
CONV_A, CONV_C = 31, 3
HALO_A, HALO_C = 32, 8
SGU_T, SGU_H, SGU_G, SGU_CHUNK = 128, 8, 256, 64
VMEM_LIMIT = 56 * 1024 * 1024
DW_BLOCK_BUDGET = 40 * 1024 * 1024
MESH = pl.DeviceIdType.MESH
ADAM_LR, ADAM_B1, ADAM_B2, ADAM_EPS, ADAM_WD, ADAM_STEP = 0.001, 0.9, 0.999, 1e-08, 0.01, 10
GELU_C, GELU_A = 0.7978845608028654, 0.044715

BIG = ["a_w_pw1", "a_w_pw2", "b_w_in", "b_w_out", "c_w_in", "c_w_out",
       "ffn_w_gate", "ffn_w_up", "ffn_w_down", "ple_w_gate", "ple_w_proj"]
TRANSPOSED = ["ffn_w_gate", "ffn_w_up"]
ROW_SHARDED = ["a_w_pw2", "b_w_out", "c_w_out", "ffn_w_gate", "ffn_w_up", "ffn_w_down", "ple_w_gate"]
SMALL_SHARDED = ["a_b_pw1", "a_w_dw", "a_b_dw", "a_ln_g", "a_ln_b", "c_w_conv"]
SMALL_REPL = ["b_b_in", "b_ln_g", "b_ln_b", "b_w_s", "b_b_s", "ln1_g", "ln1_b", "ln2_g", "ln2_b", "ple_norm_g"]
WEIGHTS = ["a_w_pw1", "a_b_pw1", "a_w_dw", "a_b_dw", "a_ln_g", "a_ln_b", "a_w_pw2", "b_w_in", "b_b_in", "b_ln_g",
           "b_ln_b", "b_w_s", "b_b_s", "b_w_out", "c_w_in", "c_w_conv", "c_w_out", "ln1_g", "ln1_b", "ln2_g",
           "ln2_b", "ffn_w_gate", "ffn_w_up", "ffn_w_down", "ple_w_gate", "ple_w_proj", "ple_norm_g"]


def _call(name, body, grid, in_specs, out_specs, out_shape, scratch=(), aliases=None, hosts=()):
    params = pltpu.CompilerParams(dimension_semantics=("arbitrary",) * len(grid), vmem_limit_bytes=VMEM_LIMIT)
    if not hosts:
        return pl.pallas_call(
            body, name=name, grid=grid, in_specs=in_specs, out_specs=out_specs, out_shape=out_shape,
            scratch_shapes=list(scratch), input_output_aliases=aliases or {}, compiler_params=params)
    assert len(grid) == 1 and not aliases
    single = not isinstance(out_shape, (list, tuple))
    own_shapes = [out_shape] if single else list(out_shape)
    own_specs = [out_specs] if single else list(out_specs)
    n_in, n_out, n_scr = len(in_specs), len(own_shapes), len(scratch)
    h_in = [len(h.arrays) for h in hosts]
    h_out = [len(h.out_shapes) for h in hosts]
    h_sem = [len(h.sems) for h in hosts]

    def split(refs, counts):
        out, off = [], 0
        for cnt in counts:
            out.append(refs[off:off + cnt])
            off += cnt
        return out

    def wrapped(*refs):
        ins, hin, outs, hout, scr, hsem = split(refs, [n_in, sum(h_in), n_out, sum(h_out), n_scr, sum(h_sem)])
        per_host = list(zip(hosts, split(hin, h_in), split(hout, h_out), split(hsem, h_sem)))

        @pl.when(pl.program_id(0) == 0)
        def _():
            for h, a, o, s in per_host:
                h.start(a, o, s)

        body(*ins, *outs, *scr)

        for h, a, o, s in per_host:
            for step, ws in sorted(h.forward_steps(grid[0]).items()):
                pl.when(pl.program_id(0) == step)(functools.partial(h.forward, ws, a, o, s))

        @pl.when(pl.program_id(0) == grid[0] - 1)
        def _():
            for h, a, o, s in per_host:
                h.complete(a, o, s)

    any_spec = pl.BlockSpec(memory_space=pl.ANY)
    call = pl.pallas_call(
        wrapped, name=name, grid=grid, in_specs=list(in_specs) + [any_spec] * sum(h_in),
        out_specs=own_specs + [any_spec] * sum(h_out),
        out_shape=own_shapes + [s for h in hosts for s in h.out_shapes],
        scratch_shapes=list(scratch) + [s for h in hosts for s in h.sems], compiler_params=params)

    def run(*args):
        res = call(*args, *[a for h in hosts for a in h.arrays])
        own = res[0] if single else list(res[:n_out])
        return own, split(list(res[n_out:]), h_out)

    return run


def _sds(shape, dtype=F32):
    return jax.ShapeDtypeStruct(shape, dtype)


def _row(tm, c):
    return pl.BlockSpec((tm, c), lambda i: (i, 0))


def _grow(g, tm, c):
    return pl.BlockSpec((g, tm, c), lambda i: (0, i, 0))


def _const(shape):
    nd = len(shape)
    return pl.BlockSpec(shape, lambda i: (0,) * nd, pipeline_mode=pl.Buffered(1))


def _wspec(w):
    return pl.BlockSpec((NQ, None, w.shape[2], w.shape[3]), lambda i: (0, 0, 0, 0), pipeline_mode=pl.Buffered(1))


def _prev(tm, hb, c):
    return pl.BlockSpec((hb, c), lambda i: (jnp.maximum(i * (tm // hb) - 1, 0), 0))


def _next(tm, hb, c):
    return pl.BlockSpec((hb, c), lambda i: (jnp.minimum((i + 1) * (tm // hb), S // hb - 1), 0))


def _acc(r, c):
    return pl.BlockSpec((r, c), lambda i: (0, 0))


def _sig(x):
    return 0.5 * jnp.tanh(0.5 * x) + 0.5


def _ln(z, g, b):
    mu = jnp.mean(z, axis=-1, keepdims=True)
    zc = z - mu
    rstd = lax.rsqrt(jnp.mean(zc * zc, axis=-1, keepdims=True) + LN_EPS)
    xhat = zc * rstd
    return xhat * g + b, xhat, rstd


def _ln_bwd(dyg, xhat, rstd):
    return rstd * (dyg - jnp.mean(dyg, axis=-1, keepdims=True) - xhat * jnp.mean(dyg * xhat, axis=-1, keepdims=True))


def _mm(a, w):
    return jnp.dot(a.astype(BF16), w, preferred_element_type=F32)


def _mmt(a, w):
    return lax.dot_general(a.astype(BF16), w, (((1,), (1,)), ((), ())), preferred_element_type=F32)


def _colsum(x):
    return jnp.sum(x, axis=0, keepdims=True)


def _gelu_and_grad(x):
    x2 = x * x
    t = jnp.tanh(x * (GELU_C + (GELU_C * GELU_A) * x2))
    hx = 0.5 * x
    return hx + hx * t, 0.5 + 0.5 * t + hx * (1.0 - t * t) * (GELU_C + (3.0 * GELU_C * GELU_A) * x2)


def _silu_grad(a, sg):
    return sg * (1.0 + a * (1.0 - sg))


def _sgu_masks():
    r = lax.broadcasted_iota(jnp.int32, (SGU_T, SGU_T), 0) // SGU_CHUNK
    c = lax.broadcasted_iota(jnp.int32, (SGU_T, SGU_T), 1) // SGU_CHUNK
    return r >= c, c >= r


def _fill_halo(buf, lo, n, halo_val_fn, is_edge):
    @pl.when(is_edge)
    def _():
        buf[lo:lo + n, :] = jnp.zeros((n, buf.shape[1]), F32)

    @pl.when(jnp.logical_not(is_edge))
    def _():
        buf[lo:lo + n, :] = halo_val_fn()


SUB, LANE = 8, 128
ROWS_AT_ONCE = 16


def _shift_copies(buf, sh):
    rows = sh.shape[1]
    for s in range(1, SUB):
        sh[s - 1, :, :] = buf[pl.ds(s, rows), :]


def _tiles(buf, sh, s, first, count, group0, lanes):
    src = buf if s == 0 else sh.at[s - 1]
    return {t: src[pl.ds(pl.multiple_of((group0 + t) * SUB, SUB), SUB), lanes] for t in range(first, first + count)}


def _by_shift(offsets):
    out = []
    for s in range(SUB):
        taps = [(k, o // SUB) for k, o in enumerate(offsets) if o % SUB == s]
        if taps:
            out.append((s, taps))
    return out


def _conv_rows(out_ref, w_ref, bias_ref, offsets, buf, sh, tm):
    n = ROWS_AT_ONCE
    for cb in range(D // LANE):
        lanes = slice(cb * LANE, (cb + 1) * LANE)
        bias = None if bias_ref is None else jnp.broadcast_to(bias_ref[:, lanes], (SUB, LANE))

        def body(jb, carry):
            accs = [bias] * n
            for s, taps in _by_shift(offsets):
                ms = [m for _, m in taps]
                tiles = _tiles(buf, sh, s, min(ms), max(ms) - min(ms) + n, jb * n, lanes)
                for k, m in taps:
                    wk = jnp.broadcast_to(w_ref[k:k + 1, lanes], (SUB, LANE))
                    for jj in range(n):
                        t = wk * tiles[m + jj]
                        accs[jj] = t if accs[jj] is None else accs[jj] + t
            for jj in range(n):
                out_ref[pl.ds(pl.multiple_of((jb * n + jj) * SUB, SUB), SUB), lanes] = accs[jj]
            return carry

        lax.fori_loop(0, tm // (SUB * n), body, 0)


def _conv_wgrad(dw_ref, d_ref, offsets, buf, sh, tm):
    n = 4
    for cb in range(D // LANE):
        lanes = slice(cb * LANE, (cb + 1) * LANE)

        def body(jq, accs):
            accs = list(accs)
            d = [d_ref[pl.ds(pl.multiple_of((jq * n + jj) * SUB, SUB), SUB), lanes] for jj in range(n)]
            for s, taps in _by_shift(offsets):
                ms = [m for _, m in taps]
                tiles = _tiles(buf, sh, s, min(ms), max(ms) - min(ms) + n, jq * n, lanes)
                for k, m in taps:
                    for jj in range(n):
                        accs[k] = accs[k] + d[jj] * tiles[m + jj]
            return tuple(accs)

        accs = lax.fori_loop(0, tm // (SUB * n), body, tuple(jnp.zeros((SUB, LANE), F32) for _ in offsets))
        for k, acc in enumerate(accs):
            dw_ref[k:k + 1, lanes] += jnp.sum(acc, axis=0, keepdims=True)


def _fwd_a1(x0, w1, b1, l, hosts=()):
    tm = 512

    def body(x_ref, w_ref, b_ref, h_ref, glu_ref):
        xb = x_ref[...].astype(BF16)
        for q in range(NQ):
            sl = slice(q * 512, (q + 1) * 512)
            h_ref[:, sl] = jnp.dot(xb, w_ref[q], preferred_element_type=F32) + b_ref[:, sl]
        glu_ref[...] = h_ref[:, :D] * _sig(h_ref[:, D:])

    return _call(f"fwd_a1_{l}", body, (S // tm,), [_row(tm, D), _wspec(w1), _const((1, 2 * D))],
                 [_row(tm, 2 * D), _row(tm, D)], [_sds((S, 2 * D)), _sds((S, D))], hosts=hosts)(x0, w1, b1)


def _fwd_a2(glu, x0, wdw, bdw, lg, lb, w2, l, hosts=()):
    tm = 256

    def body(g_ref, gp_ref, x_ref, wdw_ref, bdw_ref, lg_ref, lb_ref, w2_ref, z_ref, cv_ref, buf, sh):
        i = pl.program_id(0)
        _fill_halo(buf, 0, HALO_A, lambda: gp_ref[...], i == 0)
        buf[HALO_A:HALO_A + tm, :] = g_ref[...]
        _shift_copies(buf, sh)
        _conv_rows(cv_ref, wdw_ref, bdw_ref, [HALO_A - (CONV_A - 1) + k for k in range(CONV_A)], buf, sh, tm)
        n, _, _ = _ln(cv_ref[...], lg_ref[...], lb_ref[...])
        sb = (n * _sig(n)).astype(BF16)
        z_ref[...] = ALPHA * x_ref[...] + jnp.dot(sb, w2_ref[...], preferred_element_type=F32)

    return _call(f"fwd_a2_{l}", body, (S // tm,),
                 [_row(tm, D), _prev(tm, HALO_A, D), _row(tm, D), _const((32, D)), _const((1, D)), _const((1, D)),
                  _const((1, D)), _const(w2.shape)],
                 [_row(tm, D), _row(tm, D)], [_sds((S, D)), _sds((S, D))],
                 scratch=[pltpu.VMEM((HALO_A + tm, D), F32), pltpu.VMEM((SUB - 1, HALO_A + tm - SUB, D), F32)],
                 hosts=hosts)(glu, glu, x0, wdw, bdw, lg, lb, w2)


def _fwd_b(x0, win, b_in, lg, lb, ws, bsx, wout, hosts=()):
    tm = 256

    def body(x_ref, win_ref, bin_ref, lg_ref, lb_ref, ws_ref, bsx_ref, wout_ref, z_ref, zg_ref, gg_ref, f_scr, h_ref):
        xb = x_ref[...].astype(BF16)
        for q in range(NQ):
            sl = slice(q * 1024, (q + 1) * 1024)
            h_ref[:, sl] = jnp.dot(xb, win_ref[q], preferred_element_type=F32) + bin_ref[:, sl]
        u, du = _gelu_and_grad(h_ref[:, :E])
        v, dv = _gelu_and_grad(h_ref[:, E:])
        zg_ref[:, 0:E] = u.astype(BF16)
        zg_ref[:, E:2 * E] = v.astype(BF16)
        gg_ref[:, 0:E] = du.astype(BF16)
        gg_ref[:, E:2 * E] = dv.astype(BF16)
        vn, _, _ = _ln(v, lg_ref[...], lb_ref[...])
        vnb = vn.astype(BF16)
        mask, _ = _sgu_masks()
        for hd in range(SGU_H):
            wm = jnp.where(mask, ws_ref[hd], 0.0).astype(BF16)
            cs = slice(hd * SGU_G, (hd + 1) * SGU_G)
            for n in range(tm // SGU_T):
                rs = slice(n * SGU_T, (n + 1) * SGU_T)
                f_scr[rs, cs] = jnp.dot(wm, vnb[rs, cs], preferred_element_type=F32) + bsx_ref[hd]
        mb = (u * f_scr[...]).astype(BF16)
        z_ref[...] = ALPHA * x_ref[...] + jnp.dot(mb, wout_ref[...], preferred_element_type=F32)

    return _call("fwd_b", body, (S // tm,),
                 [_row(tm, D), _wspec(win), _const((1, 2 * E)), _const((1, E)), _const((1, E)),
                  _const((SGU_H, SGU_T, SGU_T)), _const((SGU_H, SGU_T, SGU_G)), _const(wout.shape)],
                 [_row(tm, D), _row(tm, 2 * E), _row(tm, 2 * E)],
                 [_sds((S, D)), _sds((S, 2 * E), BF16), _sds((S, 2 * E), BF16)],
                 scratch=[pltpu.VMEM((tm, E), F32), pltpu.VMEM((tm, 2 * E), F32)], hosts=hosts
                 )(x0, win, b_in, lg, lb, ws, bsx, wout)


def _fwd_c1(x0, win, hosts=()):
    tm = 512

    def body(x_ref, w_ref, hc_ref):
        xb = x_ref[...].astype(BF16)
        for q in range(NQ):
            hc_ref[:, q * 768:(q + 1) * 768] = jnp.dot(xb, w_ref[q], preferred_element_type=F32)

    return _call("fwd_c1", body, (S // tm,), [_row(tm, D), _wspec(win)], _row(tm, 3 * D),
                 _sds((S, 3 * D)), hosts=hosts)(x0, win)


def _short_conv(buf, hc_ref, hcp_ref, wc_ref, tm, i):
    _fill_halo(buf, 0, HALO_C, lambda: hcp_ref[:, D:2 * D] * hcp_ref[:, 2 * D:], i == 0)
    buf[HALO_C:HALO_C + tm, :] = hc_ref[:, D:2 * D] * hc_ref[:, 2 * D:]
    y = wc_ref[0:1, :] * buf[pl.ds(HALO_C - 2, tm), :]
    for k in range(1, CONV_C):
        y = y + wc_ref[k:k + 1, :] * buf[pl.ds(HALO_C - 2 + k, tm), :]
    return y


def _fwd_c2(hc, x0, wc, wout, hosts=()):
    tm = 512

    def body(hc_ref, hcp_ref, x_ref, wc_ref, wout_ref, z_ref, buf):
        y = _short_conv(buf, hc_ref, hcp_ref, wc_ref, tm, pl.program_id(0))
        mb = (hc_ref[:, :D] * y).astype(BF16)
        z_ref[...] = ALPHA * x_ref[...] + jnp.dot(mb, wout_ref[...], preferred_element_type=F32)

    return _call("fwd_c2", body, (S // tm,),
                 [_row(tm, 3 * D), _prev(tm, HALO_C, 3 * D), _row(tm, D), _const((8, D)), _const(wout.shape)],
                 _row(tm, D), _sds((S, D)), scratch=[pltpu.VMEM((HALO_C + tm, D), F32)], hosts=hosts
                 )(hc, hc, x0, wc, wout)


def _fwd_ffn(z1, lg, lb, wgt, wut, wd, l, hosts=()):
    tm = 256

    def body(z_ref, lg_ref, lb_ref, wg_ref, wu_ref, wd_ref, o_ref, a_ref, u_ref, hm_ref):
        x1, _, _ = _ln(z_ref[...], lg_ref[...], lb_ref[...])
        xb = x1.astype(BF16)
        a = _mmt(xb, wg_ref[...])
        u = _mmt(xb, wu_ref[...])
        hmb = (a * _sig(a) * u).astype(BF16)
        a_ref[...] = a.astype(BF16)
        u_ref[...] = u.astype(BF16)
        hm_ref[...] = hmb
        o_ref[...] = ALPHA * x1 + jnp.dot(hmb, wd_ref[...], preferred_element_type=F32)

    return _call(f"fwd_ffn_{l}", body, (S // tm,),
                 [_row(tm, D), _const((1, D)), _const((1, D)), _const((FF, D)), _const((FF, D)), _const((FF, D))],
                 [_row(tm, D), _row(tm, FF), _row(tm, FF), _row(tm, FF)],
                 [_sds((S, D)), _sds((S, FF), BF16), _sds((S, FF), BF16), _sds((S, FF), BF16)],
                 hosts=hosts)(z1, lg, lb, wgt, wut, wd)


def _ple_parts(z2, p, lg, lb, wg_ref, wp_ref, pg):
    x2, xhat, rstd = _ln(z2, lg, lb)
    xb = x2.astype(BF16)
    gate = _sig(jnp.dot(xb, wg_ref[...], preferred_element_type=F32))
    pb = p.astype(BF16)
    qp = jnp.concatenate([jnp.dot(pb, wp_ref[q], preferred_element_type=F32) for q in range(NQ)], axis=1)
    rs = lax.rsqrt(jnp.mean(qp * qp, axis=-1, keepdims=True) + LN_EPS)
    qn = qp * rs
    return x2, xhat, rstd, xb, gate, qn, rs, qn * pg


def _fwd_ple(z2, p, lg, lb, wg, wp, pg, l, hosts=()):
    tm = 512

    def body(z_ref, p_ref, lg_ref, lb_ref, wg_ref, wp_ref, pg_ref, o_ref):
        x2, _, _, _, gate, _, _, r = _ple_parts(z_ref[...], p_ref[...], lg_ref[...], lb_ref[...], wg_ref, wp_ref,
                                                pg_ref[...])
        o_ref[...] = x2 + gate * r

    return _call(f"fwd_ple_{l}", body, (S // tm,),
                 [_row(tm, D), _row(tm, 256), _const((1, D)), _const((1, D)), _const(wg.shape), _wspec(wp),
                  _const((1, D))],
                 _row(tm, D), _sds((S, D)), hosts=hosts)(z2, p, lg, lb, wg, wp, pg)


def _loss_head(y, target):
    tm = 512

    def body(y_ref, t_ref, dy_ref, acc_ref):
        @pl.when(pl.program_id(0) == 0)
        def _():
            acc_ref[...] = jnp.zeros_like(acc_ref)

        e = y_ref[...] - t_ref[...]
        dy_ref[...] = e * (1.0 / D)
        acc_ref[0:1, :] += _colsum(e * e)

    return _call("loss_head", body, (S // tm,), [_row(tm, D), _row(tm, D)], [_row(tm, D), _acc(8, D)],
                 [_sds((S, D)), _sds((8, D))])(y, target)


def _zero_first(*refs):
    @pl.when(pl.program_id(0) == 0)
    def _():
        for r in refs:
            r[...] = jnp.zeros_like(r)


def _bwd_ple(g, z2, p, lg, lb, wg, wp, pg, l, hosts=()):
    tm = 512

    def body(g_ref, z_ref, p_ref, lg_ref, lb_ref, wg_ref, wp_ref, pg_ref, dz_ref, xb_ref, dgp_ref, dqp_ref, acc_ref):
        _zero_first(acc_ref)
        gin = g_ref[...]
        lgv, pgv = lg_ref[...], pg_ref[...]
        _, xhat, rstd, xb, gate, qn, rs, r = _ple_parts(z_ref[...], p_ref[...], lgv, lb_ref[...], wg_ref, wp_ref, pgv)
        xb_ref[...] = xb
        dgpb = (gin * r * gate * (1.0 - gate)).astype(BF16)
        dgp_ref[...] = dgpb
        dx2 = gin + _mmt(dgpb, wg_ref[...])
        dr = gin * gate
        acc_ref[0:1, :] += _colsum(dr * qn)
        t = dr * pgv
        dqp_ref[...] = (rs * (t - qn * jnp.mean(t * qn, axis=-1, keepdims=True))).astype(BF16)
        acc_ref[1:2, :] += _colsum(dx2 * xhat)
        acc_ref[2:3, :] += _colsum(dx2)
        dz_ref[...] = _ln_bwd(dx2 * lgv, xhat, rstd)

    return _call(f"bwd_ple_{l}", body, (S // tm,),
                 [_row(tm, D), _row(tm, D), _row(tm, 256), _const((1, D)), _const((1, D)), _const(wg.shape),
                  _wspec(wp), _const((1, D))],
                 [_row(tm, D), _row(tm, D), _row(tm, D), _row(tm, D), _acc(8, D)],
                 [_sds((S, D)), _sds((S, D), BF16), _sds((S, D), BF16), _sds((S, D), BF16), _sds((8, D))],
                 hosts=hosts)(g, z2, p, lg, lb, wg, wp, pg)


def _bwd_ffn(dz2, z1, ab, ub, lg, lb, wgt, wut, wd, l, hosts=()):
    tm = 256

    def body(dz2_ref, z_ref, a_ref, u_ref, lg_ref, lb_ref, wg_ref, wu_ref, wd_ref, dz1_ref, xb_ref, da_ref, du_ref,
             acc_ref):
        _zero_first(acc_ref)
        dz2v = dz2_ref[...]
        lgv = lg_ref[...]
        x1, xhat, rstd = _ln(z_ref[...], lgv, lb_ref[...])
        xb_ref[...] = x1.astype(BF16)
        a = a_ref[...].astype(F32)
        u = u_ref[...].astype(F32)
        sg = _sig(a)
        dhm = _mmt(dz2v, wd_ref[...])
        dub = (dhm * (a * sg)).astype(BF16)
        dab = (dhm * u * _silu_grad(a, sg)).astype(BF16)
        da_ref[...] = dab
        du_ref[...] = dub
        dx1 = ALPHA * dz2v + _mm(dab, wg_ref[...]) + _mm(dub, wu_ref[...])
        acc_ref[0:1, :] += _colsum(dx1 * xhat)
        acc_ref[1:2, :] += _colsum(dx1)
        dz1_ref[...] = _ln_bwd(dx1 * lgv, xhat, rstd)

    return _call(f"bwd_ffn_{l}", body, (S // tm,),
                 [_row(tm, D), _row(tm, D), _row(tm, FF), _row(tm, FF), _const((1, D)), _const((1, D)),
                  _const((FF, D)), _const((FF, D)), _const((FF, D))],
                 [_row(tm, D), _row(tm, D), _row(tm, FF), _row(tm, FF), _acc(8, D)],
                 [_sds((S, D)), _sds((S, D), BF16), _sds((S, FF), BF16), _sds((S, FF), BF16), _sds((8, D))],
                 hosts=hosts)(dz2, z1, ab, ub, lg, lb, wgt, wut, wd)


def _bwd_a2(dz1, cv, lg, lb, w2, l, hosts=()):
    tm = 512

    def body(dz_ref, cv_ref, lg_ref, lb_ref, w2_ref, dcv_ref, sb_ref, acc_ref):
        _zero_first(acc_ref)
        lgv = lg_ref[...]
        n, xhat, rstd = _ln(cv_ref[...], lgv, lb_ref[...])
        sg = _sig(n)
        sb_ref[...] = (n * sg).astype(BF16)
        dzb = dz_ref[...].astype(BF16)
        ds = _mmt(dzb, w2_ref[...])
        dn = ds * _silu_grad(n, sg)
        acc_ref[0:1, :] += _colsum(dn * xhat)
        acc_ref[1:2, :] += _colsum(dn)
        dcv = _ln_bwd(dn * lgv, xhat, rstd)
        acc_ref[2:3, :] += _colsum(dcv)
        dcv_ref[...] = dcv

    return _call(f"bwd_a2_{l}", body, (S // tm,),
                 [_row(tm, D), _row(tm, D), _const((1, D)), _const((1, D)), _const(w2.shape)],
                 [_row(tm, D), _row(tm, D), _acc(8, D)],
                 [_sds((S, D)), _sds((S, D), BF16), _sds((8, D))], hosts=hosts)(dz1, cv, lg, lb, w2)


def _bwd_conv_a(dcv, glu, wdw, l, hosts=()):
    tm = 256
    nb = S // tm

    def body(d_ref, dn_ref, g_ref, gp_ref, w_ref, dglu_ref, dw_ref, bufd, bufx, sh):
        i = pl.program_id(0)
        _zero_first(dw_ref)
        bufd[0:tm, :] = d_ref[...]
        _fill_halo(bufd, tm, HALO_A, lambda: dn_ref[...], i == nb - 1)
        _fill_halo(bufx, 0, HALO_A, lambda: gp_ref[...], i == 0)
        bufx[HALO_A:HALO_A + tm, :] = g_ref[...]
        _shift_copies(bufd, sh)
        _conv_rows(dglu_ref, w_ref, None, [CONV_A - 1 - k for k in range(CONV_A)], bufd, sh, tm)
        _shift_copies(bufx, sh)
        _conv_wgrad(dw_ref, d_ref, [HALO_A - (CONV_A - 1) + k for k in range(CONV_A)], bufx, sh, tm)

    return _call(f"bwd_conv_a_{l}", body, (nb,),
                 [_row(tm, D), _next(tm, HALO_A, D), _row(tm, D), _prev(tm, HALO_A, D), _const((32, D))],
                 [_row(tm, D), _acc(32, D)], [_sds((S, D)), _sds((32, D))],
                 scratch=[pltpu.VMEM((tm + HALO_A, D), F32), pltpu.VMEM((HALO_A + tm, D), F32),
                          pltpu.VMEM((SUB - 1, HALO_A + tm - SUB, D), F32)], hosts=hosts)(dcv, dcv, glu, glu, wdw)


def _bwd_a1(dglu, h, dz1, w1, l, hosts=()):
    tm = 512

    def body(dg_ref, h_ref, dz_ref, w_ref, dx_ref, dh_ref, acc_ref):
        _zero_first(acc_ref)
        a, g = h_ref[:, :D], h_ref[:, D:]
        sg = _sig(g)
        dgl = dg_ref[...]
        da = dgl * sg
        dg = dgl * a * sg * (1.0 - sg)
        acc_ref[0:1, 0:D] += _colsum(da)
        acc_ref[0:1, D:2 * D] += _colsum(dg)
        dh_ref[:, 0:D] = da.astype(BF16)
        dh_ref[:, D:2 * D] = dg.astype(BF16)
        dx = ALPHA * dz_ref[...]
        for q in range(NQ):
            dx = dx + _mmt(dh_ref[:, q * 512:(q + 1) * 512], w_ref[q])
        dx_ref[...] = dx

    return _call(f"bwd_a1_{l}", body, (S // tm,),
                 [_row(tm, D), _row(tm, 2 * D), _row(tm, D), _wspec(w1)],
                 [_row(tm, D), _row(tm, 2 * D), _acc(8, 2 * D)],
                 [_sds((S, D)), _sds((S, 2 * D), BF16), _sds((8, 2 * D))], hosts=hosts)(dglu, h, dz1, w1)


def _bwd_c2(dz1, hc, wc, wout):
    tm = 512

    def body(dz_ref, hc_ref, hcp_ref, wc_ref, wout_ref, dy_ref, dbg_ref, mb_ref, buf):
        y = _short_conv(buf, hc_ref, hcp_ref, wc_ref, tm, pl.program_id(0))
        dzb = dz_ref[...].astype(BF16)
        dm = _mmt(dzb, wout_ref[...])
        bg = hc_ref[:, :D]
        mb_ref[...] = (bg * y).astype(BF16)
        dbg_ref[...] = (dm * y).astype(BF16)
        dy_ref[...] = dm * bg

    return _call("bwd_c2", body, (S // tm,),
                 [_row(tm, D), _row(tm, 3 * D), _prev(tm, HALO_C, 3 * D), _const((8, D)), _const(wout.shape)],
                 [_row(tm, D), _row(tm, D), _row(tm, D)],
                 [_sds((S, D)), _sds((S, D), BF16), _sds((S, D), BF16)],
                 scratch=[pltpu.VMEM((HALO_C + tm, D), F32)])(dz1, hc, hc, wc, wout)


def _bwd_c1(dy, hc, dbg, dz1, wc, win):
    tm = 256
    nb = S // tm

    def body(d_ref, dn_ref, hc_ref, hcp_ref, dbg_ref, dz_ref, wc_ref, win_ref, dx_ref, dhc_ref, dwc_ref, bufd, bufq):
        i = pl.program_id(0)
        _zero_first(dwc_ref)
        bufd[0:tm, :] = d_ref[...]
        _fill_halo(bufd, tm, HALO_C, lambda: dn_ref[...], i == nb - 1)
        _fill_halo(bufq, 0, HALO_C, lambda: hcp_ref[:, D:2 * D] * hcp_ref[:, 2 * D:], i == 0)
        bufq[HALO_C:HALO_C + tm, :] = hc_ref[:, D:2 * D] * hc_ref[:, 2 * D:]
        dq = wc_ref[0:1, :] * bufd[pl.ds(CONV_C - 1, tm), :]
        for k in range(1, CONV_C):
            dq = dq + wc_ref[k:k + 1, :] * bufd[pl.ds(CONV_C - 1 - k, tm), :]
        dv = d_ref[...]
        for k in range(CONV_C):
            dwc_ref[k:k + 1, :] += _colsum(dv * bufq[pl.ds(HALO_C - (CONV_C - 1) + k, tm), :])
        dhc_ref[:, 0:D] = dbg_ref[...]
        dhc_ref[:, D:2 * D] = (dq * hc_ref[:, 2 * D:]).astype(BF16)
        dhc_ref[:, 2 * D:3 * D] = (dq * hc_ref[:, D:2 * D]).astype(BF16)
        dx = ALPHA * dz_ref[...]
        for q in range(NQ):
            dx = dx + _mmt(dhc_ref[:, q * 768:(q + 1) * 768], win_ref[q])
        dx_ref[...] = dx

    return _call("bwd_c1", body, (nb,),
                 [_row(tm, D), _next(tm, HALO_C, D), _row(tm, 3 * D), _prev(tm, HALO_C, 3 * D), _row(tm, D),
                  _row(tm, D), _const((8, D)), _wspec(win)],
                 [_row(tm, D), _row(tm, 3 * D), _acc(8, D)],
                 [_sds((S, D)), _sds((S, 3 * D), BF16), _sds((8, D))],
                 scratch=[pltpu.VMEM((tm + HALO_C, D), F32), pltpu.VMEM((HALO_C + tm, D), F32)]
                 )(dy, dy, hc, hc, dbg, dz1, wc, win)


def _bwd_b(dz1, zg, gg, lg, lb, win, wout, ws, wst, bsx, hosts=()):
    tm = 256
    nb = S // tm

    def body(dz_ref, zg_ref, gg_ref, lg_ref, lb_ref, win_ref, wout_ref, ws_ref, wst_ref, bsx_ref,
             dx_ref, dh_ref, mb_ref, acc_ref, dws_ref, dbs_ref, f_scr, dvn_scr):
        _zero_first(acc_ref, dws_ref, dbs_ref)
        lgv = lg_ref[...]
        u = zg_ref[:, :E].astype(F32)
        v = zg_ref[:, E:].astype(F32)
        vn, xhat, rstd = _ln(v, lgv, lb_ref[...])
        vnb = vn.astype(BF16)
        dzb = dz_ref[...].astype(BF16)
        dm = _mmt(dzb, wout_ref[...])
        mask, mask_t = _sgu_masks()
        for hd in range(SGU_H):
            wm = jnp.where(mask, ws_ref[hd], 0.0).astype(BF16)
            cs = slice(hd * SGU_G, (hd + 1) * SGU_G)
            for n in range(tm // SGU_T):
                rs = slice(n * SGU_T, (n + 1) * SGU_T)
                f_scr[rs, cs] = jnp.dot(wm, vnb[rs, cs], preferred_element_type=F32) + bsx_ref[hd]
        f = f_scr[...]
        mb_ref[...] = (u * f).astype(BF16)
        du = dm * f
        df = dm * u
        dfb = df.astype(BF16)
        for hd in range(SGU_H):
            wmt = jnp.where(mask_t, wst_ref[hd], 0.0).astype(BF16)
            cs = slice(hd * SGU_G, (hd + 1) * SGU_G)
            for n in range(tm // SGU_T):
                rs = slice(n * SGU_T, (n + 1) * SGU_T)
                dvn_scr[rs, cs] = jnp.dot(wmt, dfb[rs, cs], preferred_element_type=F32)
                dws_ref[hd] += lax.dot_general(dfb[rs, cs], vnb[rs, cs], (((1,), (1,)), ((), ())),
                                               preferred_element_type=F32)
                dbs_ref[hd] += df[rs, cs]
        dvn = dvn_scr[...]
        acc_ref[1:2, 0:E] += _colsum(dvn * xhat)
        acc_ref[2:3, 0:E] += _colsum(dvn)
        dv = _ln_bwd(dvn * lgv, xhat, rstd)
        dhu = du * gg_ref[:, :E].astype(F32)
        dhv = dv * gg_ref[:, E:].astype(F32)
        acc_ref[0:1, 0:E] += _colsum(dhu)
        acc_ref[0:1, E:2 * E] += _colsum(dhv)
        dh_ref[:, 0:E] = dhu.astype(BF16)
        dh_ref[:, E:2 * E] = dhv.astype(BF16)
        dx = ALPHA * dz_ref[...]
        for q in range(NQ):
            dx = dx + _mmt(dh_ref[:, q * 1024:(q + 1) * 1024], win_ref[q])
        dx_ref[...] = dx

        @pl.when(pl.program_id(0) == nb - 1)
        def _():
            for hd in range(SGU_H):
                dws_ref[hd] = jnp.where(mask, dws_ref[hd], 0.0)

    c3 = lambda a, b, c: pl.BlockSpec((a, b, c), lambda i: (0, 0, 0))
    return _call("bwd_b", body, (nb,),
                 [_row(tm, D), _row(tm, 2 * E), _row(tm, 2 * E), _const((1, E)), _const((1, E)), _wspec(win),
                  _const(wout.shape), _const((SGU_H, SGU_T, SGU_T)), _const((SGU_H, SGU_T, SGU_T)),
                  _const((SGU_H, SGU_T, SGU_G))],
                 [_row(tm, D), _row(tm, 2 * E), _row(tm, E), _acc(8, 2 * E), c3(SGU_H, SGU_T, SGU_T),
                  c3(SGU_H, SGU_T, SGU_G)],
                 [_sds((S, D)), _sds((S, 2 * E), BF16), _sds((S, E), BF16), _sds((8, 2 * E)),
                  _sds((SGU_H, SGU_T, SGU_T)), _sds((SGU_H, SGU_T, SGU_G))],
                 scratch=[pltpu.VMEM((tm, E), F32), pltpu.VMEM((tm, E), F32)], hosts=hosts
                 )(dz1, zg, gg, lg, lb, win, wout, ws, wst, bsx)


def _mm_tn(name, a, amode, b, bmode, k, n, groups=NQ, hosts=()):
    def block_bytes(ts):
        ka = k if amode == "1" else groups * k
        nb = n if bmode == "1" else groups * n
        return 2 * (ts * ka * a.dtype.itemsize + ts * nb * b.dtype.itemsize + groups * k * n * 4)

    ts = min(1024 if block_bytes(1024) <= DW_BLOCK_BUDGET else 512, S)

    def spec(mode, w):
        if mode == "1":
            return pl.BlockSpec((ts, w), lambda s: (s, 0))
        if mode == "c":
            return pl.BlockSpec((ts, groups * w), lambda s: (s, 0))
        return pl.BlockSpec((groups, ts, w), lambda s: (0, s, 0))

    def pick(ref, mode, w, g):
        if mode == "1":
            return ref[...]
        if mode == "c":
            return ref[:, g * w:(g + 1) * w]
        return ref[g]

    def body(a_ref, b_ref, o_ref):
        _zero_first(o_ref)
        a_t = jnp.transpose(a_ref[...].astype(BF16)) if amode == "1" else None
        b_1 = b_ref[...].astype(BF16) if bmode == "1" else None
        for g in range(groups):
            lhs = a_t if amode == "1" else jnp.transpose(pick(a_ref, amode, k, g).astype(BF16))
            rhs = b_1 if bmode == "1" else pick(b_ref, bmode, n, g).astype(BF16)
            o_ref[0, g] += jnp.dot(lhs, rhs, preferred_element_type=F32)

    return _call(name, body, (S // ts,), [spec(amode, k), spec(bmode, n)],
                 pl.BlockSpec((1, groups, k, n), lambda s: (0, 0, 0, 0)), _sds((1, groups, k, n)), hosts=hosts)(a, b)


def _row_block(k, cap=256):
    return max(t for t in range(16, min(k, cap) + 1, 16) if k % t == 0)


def _cast_bf16(w, hosts=()):
    nl, k, n = w.shape
    tb = _row_block(k, 512)
    nb = k // tb

    def body(w_ref, o_ref):
        o_ref[...] = w_ref[...].astype(BF16)

    spec = pl.BlockSpec((None, tb, n), lambda i: (i // nb, i % nb, 0))
    return _call("cast_bf16", body, (nl * nb,), [spec], spec, _sds(w.shape, BF16), hosts=hosts)(w)


def _adam(name, w, m, v, gc, l, prev):
    nl, k, n = w.shape
    nc = gc.shape[0]
    tb = _row_block(k, 512)

    def body(w_ref, m_ref, v_ref, g_ref, *rest):
        go_ref, d_ref, mo_ref, vo_ref = rest[-4:]
        g = g_ref[0].astype(F32)
        for c in range(1, nc):
            g = g + g_ref[c].astype(F32)
        m2 = ADAM_B1 * m_ref[...] + (1.0 - ADAM_B1) * g
        v2 = ADAM_B2 * v_ref[...] + (1.0 - ADAM_B2) * (g * g)
        m_hat = m2 / (1.0 - ADAM_B1 ** ADAM_STEP)
        v_hat = v2 / (1.0 - ADAM_B2 ** ADAM_STEP)
        go_ref[...] = g
        d_ref[...] = -ADAM_LR * (m_hat / (jnp.sqrt(v_hat) + ADAM_EPS) + ADAM_WD * w_ref[...])
        mo_ref[...] = m2
        vo_ref[...] = v2

    spec = pl.BlockSpec((None, tb, n), lambda i: (l, i, 0))
    gspec = pl.BlockSpec((nc, None, tb, n), lambda i: (0, 0, i, 0))
    in_specs, args, aliases = [spec, spec, spec, gspec], [w, m, v, gc], {}
    if prev is not None:
        in_specs += [pl.BlockSpec(memory_space=pl.ANY)] * 4
        args += list(prev)
        aliases = {4 + j: j for j in range(4)}
    return _call(name, body, (k // tb,), in_specs, [spec] * 4, [_sds(w.shape)] * 4, aliases=aliases)(*args)


def _sum8(name, g8):
    r = g8.shape[1]

    def body(g_ref, o_ref):
        acc = g_ref[0]
        for d in range(1, 8):
            acc = acc + g_ref[d]
        o_ref[...] = acc

    return _call(name, body, (1,), [pl.BlockSpec((8, r, 128), lambda i: (0, 0, 0))],
                 pl.BlockSpec((r, 128), lambda i: (0, 0)), _sds((r, 128)))(g8)


def _place():
    x, y, c = lax.axis_index("x"), lax.axis_index("y"), lax.axis_index("c")
    return x, y, c, 2 * x + y, (x, y, 1 - c), [(1 - x, y), (x, 1 - y), (1 - x, 1 - y)]


class _Exchange:
    def __init__(self, arrays, out_shapes):
        self.arrays, self.out_shapes = list(arrays), list(out_shapes)
        n = len(self.arrays)
        self.sems = [pltpu.SemaphoreType.DMA((7 * n,)), pltpu.SemaphoreType.DMA((7 * n,)),
                     pltpu.SemaphoreType.DMA((n,))]

    def _copies(self, ins, outs, sems):
        send, recv, lsem = sems
        local_src, remote_src, dst = self.maps(ins, outs)
        x, y, c, q, sib, chips = _place()

        def rcopy(w, k, qq, cc, to, src=None):
            return pltpu.make_async_remote_copy(
                src_ref=dst(w, qq, cc) if src is None else src, dst_ref=dst(w, qq, cc),
                send_sem=send.at[7 * w + k], recv_sem=recv.at[7 * w + k], device_id=to, device_id_type=MESH)

        def mine(w):
            return pltpu.make_async_copy(local_src(w), dst(w, q, c), lsem.at[w])

        def first(w):
            return [rcopy(w, 0, q, c, sib, local_src(w))] + [
                rcopy(w, 1 + j, q, c, (cx, cy, c), remote_src(w, 2 * cx + cy)) for j, (cx, cy) in enumerate(chips)]

        return rcopy, mine, first, (x, y, c), q, c, sib, chips

    def start(self, ins, outs, sems):
        _, mine, first, *_ = self._copies(ins, outs, sems)
        for w in range(len(self.arrays)):
            mine(w).start()
            for cp in first(w):
                cp.start()

    def forward_steps(self, n_steps):
        sizes = [a.size // a.shape[0] for a in self.arrays]
        plan, moved = {}, 0
        for w, size in enumerate(sizes):
            moved += size
            plan.setdefault(min(n_steps - 1, -(-moved * n_steps // sum(sizes))), []).append(w)
        return plan

    def forward(self, ws, ins, outs, sems):
        rcopy, _, _, me, _, c, sib, chips = self._copies(ins, outs, sems)
        for w in ws:
            for j, (cx, cy) in enumerate(chips):
                rcopy(w, 1 + j, 2 * cx + cy, c, me).wait_recv()
                rcopy(w, 4 + j, 2 * cx + cy, c, sib).start()

    def complete(self, ins, outs, sems):
        rcopy, mine, first, me, q, c, sib, chips = self._copies(ins, outs, sems)
        n = len(self.arrays)
        for w in range(n):
            rcopy(w, 0, q, 1 - c, me).wait_recv()
            for j, (cx, cy) in enumerate(chips):
                rcopy(w, 4 + j, 2 * cx + cy, 1 - c, me).wait_recv()
        for w in range(n):
            for cp in first(w):
                cp.wait_send()
            for j, (cx, cy) in enumerate(chips):
                rcopy(w, 4 + j, 2 * cx + cy, c, sib).wait_send()
            mine(w).wait()


class _GatherWeights(_Exchange):
    def __init__(self, items):
        self.layers = [l for _, l in items]
        self.kh = [s.shape[1] // 2 for s, _ in items]
        super().__init__([s for s, _ in items], [_sds((NQ, 1) + s.shape[1:], BF16) for s, _ in items])

    def maps(self, ins, outs):
        c = lax.axis_index("c")
        src = lambda w: ins[w].at[pl.ds(self.layers[w], 1), pl.ds(c * self.kh[w], self.kh[w]), :]
        return src, lambda w, q: src(w), lambda w, q, cc: outs[w].at[q, :, pl.ds(cc * self.kh[w], self.kh[w]), :]


class _ScatterPartials(_Exchange):
    def __init__(self, parts):
        super().__init__(parts, [_sds((NQ, 1, 2) + p.shape[2:], BF16) for p in parts])

    def maps(self, ins, outs):
        q = 2 * lax.axis_index("x") + lax.axis_index("y")
        return (lambda w: ins[w].at[:, q]), (lambda w, qq: ins[w].at[:, qq]), (lambda w, qq, cc: outs[w].at[qq, :, cc])


class _Gather8(_Exchange):
    def __init__(self, v):
        super().__init__([v], [_sds((8,) + v.shape)])

    def maps(self, ins, outs):
        return (lambda w: ins[0]), (lambda w, q: ins[0]), (lambda w, q, cc: outs[0].at[2 * q + cc])


class _SwapHalves:
    def __init__(self, dws):
        self.arrays = list(dws)
        self.kh = [d.shape[2] // 2 for d in dws]
        self.out_shapes = [_sds(d.shape[:2] + (kh,) + d.shape[3:]) for d, kh in zip(dws, self.kh)]
        self.sems = [pltpu.SemaphoreType.DMA((len(dws),)), pltpu.SemaphoreType.DMA((len(dws),))]

    def _copies(self, ins, outs, sems):
        send, recv = sems
        _, _, c, _, sib, _ = _place()
        return [pltpu.make_async_remote_copy(
            src_ref=ins[w].at[:, :, pl.ds((1 - c) * self.kh[w], self.kh[w]), :], dst_ref=outs[w],
            send_sem=send.at[w], recv_sem=recv.at[w], device_id=sib, device_id_type=MESH)
            for w in range(len(self.arrays))]

    def start(self, ins, outs, sems):
        for cp in self._copies(ins, outs, sems):
            cp.start()

    def forward_steps(self, n_steps):
        return {}

    def complete(self, ins, outs, sems):
        for cp in self._copies(ins, outs, sems):
            cp.wait()


def _comm_only(name, host):
    n_in, n_out = len(host.arrays), len(host.out_shapes)

    def body(*refs):
        ins, outs, sems = refs[:n_in], refs[n_in:n_in + n_out], refs[n_in + n_out:]
        host.start(ins, outs, sems)
        for ws in host.forward_steps(1).values():
            host.forward(ws, ins, outs, sems)
        host.complete(ins, outs, sems)

    any_spec = pl.BlockSpec(memory_space=pl.ANY)
    return pl.pallas_call(body, name=name, in_specs=[any_spec] * n_in, out_specs=[any_spec] * n_out,
                          out_shape=host.out_shapes, scratch_shapes=host.sems)(*host.arrays)


def _add_halves(dw, got, cidx):
    nl, _, k, n = dw.shape
    kh = k // 2
    qb = 2

    def body(c_ref, a_ref, b_ref, o_ref):
        o_ref[...] = (a_ref[...] + b_ref[...]).astype(BF16)

    grid_spec = pltpu.PrefetchScalarGridSpec(
        num_scalar_prefetch=1, grid=(nl, NQ // qb),
        in_specs=[pl.BlockSpec((None, qb, None, kh, n), lambda l, q, c_ref: (l, q, c_ref[0], 0, 0)),
                  pl.BlockSpec((None, qb, kh, n), lambda l, q, c_ref: (l, q, 0, 0))],
        out_specs=pl.BlockSpec((None, qb, kh, n), lambda l, q, c_ref: (l, q, 0, 0)))
    return pl.pallas_call(
        body, name="add_halves", grid_spec=grid_spec, out_shape=_sds((nl, NQ, kh, n), BF16),
        compiler_params=pltpu.CompilerParams(dimension_semantics=("arbitrary", "arbitrary"),
                                             vmem_limit_bytes=VMEM_LIMIT))(cidx, dw.reshape(nl, NQ, 2, kh, n), got)


def _gather8(name, v):
    return _comm_only(name, _Gather8(v))[0]


PACK = 16 * 128


def _pack(arrays):
    parts = []
    for a in arrays:
        flat = a.reshape(-1)
        parts.append(jnp.pad(flat, (0, (-flat.shape[0]) % PACK)))
    return jnp.concatenate(parts).reshape(-1, 128)


def _unpack(packed, shapes):
    flat = packed.reshape(-1)
    out, off = [], 0
    for shp in shapes:
        size = 1
        for d in shp:
            size *= d
        out.append(flat[off:off + size].reshape(shp))
        off += size + (-size) % PACK
    return out


def kernel(x, p, a_w_pw1, a_b_pw1, a_w_dw, a_b_dw, a_ln_g, a_ln_b, a_w_pw2, b_w_in, b_b_in, b_ln_g, b_ln_b, b_w_s, b_b_s, b_w_out, c_w_in, c_w_conv, c_w_out, ln1_g, ln1_b, ln2_g, ln2_b, ffn_w_gate, ffn_w_up, ffn_w_down, ple_w_gate, ple_w_proj, ple_norm_g, loss_target, m_a_w_pw1, m_a_b_pw1, m_a_w_dw, m_a_b_dw, m_a_ln_g, m_a_ln_b, m_a_w_pw2, m_b_w_in, m_b_b_in, m_b_ln_g, m_b_ln_b, m_b_w_s, m_b_b_s, m_b_w_out, m_c_w_in, m_c_w_conv, m_c_w_out, m_ln1_g, m_ln1_b, m_ln2_g, m_ln2_b, m_ffn_w_gate, m_ffn_w_up, m_ffn_w_down, m_ple_w_gate, m_ple_w_proj, m_ple_norm_g, v_a_w_pw1, v_a_b_pw1, v_a_w_dw, v_a_b_dw, v_a_ln_g, v_a_ln_b, v_a_w_pw2, v_b_w_in, v_b_b_in, v_b_ln_g, v_b_ln_b, v_b_w_s, v_b_b_s, v_b_w_out, v_c_w_in, v_c_w_conv, v_c_w_out, v_ln1_g, v_ln1_b, v_ln2_g, v_ln2_b, v_ffn_w_gate, v_ffn_w_up, v_ffn_w_down, v_ple_w_gate, v_ple_w_proj, v_ple_norm_g):
    args = dict(locals())
    wts = {k: args[k] for k in WEIGHTS}
    mom = {k: args["m_" + k] for k in WEIGHTS}
    var = {k: args["v_" + k] for k in WEIGHTS}
    for k in TRANSPOSED:
        wts[k], mom[k], var[k] = (jnp.transpose(t[k], (0, 2, 1)) for t in (wts, mom, var))
    q_idx = 2 * lax.axis_index("x") + lax.axis_index("y")
    c_idx = lax.axis_index("c").astype(jnp.int32).reshape(1)

    wb = {k: _cast_bf16(wts[k]) for k in BIG if k not in ("ffn_w_gate", "ffn_w_up")}
    mixw = [[("a_w_pw1", 0), ("a_w_pw2", 0)], [("b_w_in", 0), ("b_w_out", 0)], [("c_w_in", 0), ("c_w_out", 0)],
            [("a_w_pw1", 1), ("a_w_pw2", 1)]]
    ffnw = [[("ffn_w_gate", l), ("ffn_w_up", l), ("ffn_w_down", l)] for l in range(DEPTH)]
    plew = [[("ple_w_gate", l), ("ple_w_proj", l)] for l in range(DEPTH)]
    fwd_plan = {("a1", 0): mixw[0][1:] + plew[0], ("a2", 0): ffnw[0], ("ffn", 0): mixw[1] + plew[1],
                ("b", 1): ffnw[1], ("ffn", 1): mixw[2] + plew[2] + ffnw[2][:1],
                ("c1", 2): ffnw[2][1:2], ("c2", 2): ffnw[2][2:], ("ffn", 2): mixw[3] + plew[3] + ffnw[3][:1],
                ("a2", 3): ffnw[3][1:]}
    gw = {}

    def gather(keys):
        return _GatherWeights([(wb[name], l) for name, l in keys])

    def hosted(tag, fn, *fargs):
        keys = fwd_plan.get(tag)
        if not keys:
            return fn(*fargs)
        own, (got,) = fn(*fargs, hosts=[gather(keys)])
        store(keys, got)
        return own

    def store(keys, got):
        for (name, l), arr in zip(keys, got):
            gw[name, l] = arr.reshape(NQ * arr.shape[2], arr.shape[3]) if name in ROW_SHARDED else arr

    first_keys = mixw[0][:1]
    wb["ffn_w_gate"], (got,) = _cast_bf16(wts["ffn_w_gate"], hosts=[gather(first_keys)])
    store(first_keys, got)
    shard_shapes = [wts[k].shape for k in SMALL_SHARDED]
    wb["ffn_w_up"], ((small8,),) = _cast_bf16(wts["ffn_w_up"], hosts=[_Gather8(_pack([wts[k] for k in SMALL_SHARDED]))])
    per_chip = [_unpack(small8[2 * qq], shard_shapes) for qq in range(NQ)]
    full = {k: jnp.concatenate([per_chip[qq][i] for qq in range(NQ)], axis=-1) for i, k in enumerate(SMALL_SHARDED)}
    for k in SMALL_REPL:
        full[k] = wts[k]

    def vec(name, l):
        return full[name][l][None, :]

    def conv_w(name, l, rows):
        w = full[name][l]
        return jnp.pad(w, ((0, rows - w.shape[0]), (0, 0)))

    ws = full["b_w_s"][0]
    wst = jnp.transpose(ws, (0, 2, 1))
    bsx = jnp.broadcast_to(full["b_b_s"][0][:, :, None], (SGU_H, SGU_T, SGU_G))

    x0s, z1s, z2s, saved, ffn_saved = [], [], [], [], []
    cur = x[0]
    for i in range(DEPTH):
        mix, j = i % 3, i // 3
        x0s.append(cur)
        if mix == 0:
            h, glu = hosted(("a1", i), _fwd_a1, cur, gw["a_w_pw1", j], vec("a_b_pw1", j), i)
            z1, cv = hosted(("a2", i), _fwd_a2, glu, cur, conv_w("a_w_dw", j, 32), vec("a_b_dw", j), vec("a_ln_g", j),
                            vec("a_ln_b", j), gw["a_w_pw2", j], i)
            saved.append((h, glu, cv))
        elif mix == 1:
            z1, zg, gg = hosted(("b", i), _fwd_b, cur, gw["b_w_in", 0], vec("b_b_in", 0), vec("b_ln_g", 0),
                                vec("b_ln_b", 0), ws, bsx, gw["b_w_out", 0])
            saved.append((zg, gg))
        else:
            hc = hosted(("c1", i), _fwd_c1, cur, gw["c_w_in", 0])
            z1 = hosted(("c2", i), _fwd_c2, hc, cur, conv_w("c_w_conv", 0, 8), gw["c_w_out", 0])
            saved.append((hc,))
        z2, ab, ub, hm = hosted(("ffn", i), _fwd_ffn, z1, vec("ln1_g", i), vec("ln1_b", i), gw["ffn_w_gate", i],
                                gw["ffn_w_up", i], gw["ffn_w_down", i], i)
        ffn_saved.append((ab, ub, hm))
        cur = hosted(("ple", i), _fwd_ple, z2, p[i, 0], vec("ln2_g", i), vec("ln2_b", i), gw["ple_w_gate", i],
                     gw["ple_w_proj", i], vec("ple_norm_g", i), i)
        z1s.append(z1)
        z2s.append(z2)

    g, loss_acc = _loss_head(cur, loss_target[0])
    loss = lax.psum(0.5 / D * jnp.sum(loss_acc[0]), ("x", "y", "c"))

    dws = {}
    sg = {}
    res = {k: None for k in BIG}

    def wgrad(name, l, a, amode, b, bmode, scatter_keys=()):
        _, k, n = wts[name].shape
        hosts = [_ScatterPartials([parts[key] for key in scatter_keys])] if scatter_keys else ()
        if name in ROW_SHARDED:
            out = _mm_tn(f"dw_{name}_{l}", a, "1", b, "1", NQ * k, n, groups=1, hosts=hosts)
        else:
            out = _mm_tn(f"dw_{name}_{l}", a, amode, b, bmode, k, n, hosts=hosts)
        if scatter_keys:
            out, (contribs,) = out
            update(scatter_keys, contribs)
        dws[name, l] = out.reshape(1, NQ, k, n)

    def swap(keys):
        return _SwapHalves([dws[k] for k in keys])

    parts = {}

    def add_halves(keys, got):
        parts.update((k, _add_halves(dws[k], r, c_idx)) for k, r in zip(keys, got))

    def update(keys, contribs):
        for (name, l), gc in zip(keys, contribs):
            _, kq, n = wts[name].shape
            res[name] = _adam(f"adam_{name}_{l}", wts[name], mom[name], var[name], gc.reshape(NQ, 1, kq, n), l,
                              res[name])

    small = SMALL_SHARDED + SMALL_REPL
    late_small = [("a_b_pw1", 0)]
    early_small = [(k, l) for k in small for l in range(full[k].shape[0]) if (k, l) not in late_small]
    pending = None
    for i in reversed(range(DEPTH)):
        mix, j = i % 3, i // 3
        ple_args = (g, z2s[i], p[i, 0], vec("ln2_g", i), vec("ln2_b", i), gw["ple_w_gate", i], gw["ple_w_proj", i],
                    vec("ple_norm_g", i), i)
        if pending:
            (dz2, x2b, dgp, dqp, acc), (got,) = _bwd_ple(*ple_args, hosts=[swap(pending)])
            add_halves(pending, got)
        else:
            dz2, x2b, dgp, dqp, acc = _bwd_ple(*ple_args)
        sg["ple_norm_g", i], sg["ln2_g", i], sg["ln2_b", i] = acc[0], acc[1], acc[2]
        wgrad("ple_w_gate", i, x2b, "c", dgp, "1")
        wgrad("ple_w_proj", i, p[i, 0], "1", dqp, "c")
        ab, ub, hm = ffn_saved[i]
        ffn_args = (dz2, z1s[i], ab, ub, vec("ln1_g", i), vec("ln1_b", i), gw["ffn_w_gate", i], gw["ffn_w_up", i],
                    gw["ffn_w_down", i], i)
        if pending:
            (dz1, x1b, da, du, acc), (contribs,) = _bwd_ffn(
                *ffn_args, hosts=[_ScatterPartials([parts[key] for key in ffnw[i + 1]])])
            update(ffnw[i + 1], contribs)
        else:
            dz1, x1b, da, du, acc = _bwd_ffn(*ffn_args)
        sg["ln1_g", i], sg["ln1_b", i] = acc[0], acc[1]
        behind_mixer = mix == 1
        rest = mixw[i + 1] + plew[i + 1] if pending else []
        wgrad("ffn_w_gate", i, da, "1", x1b, "1", scatter_keys=() if behind_mixer else rest[1:])
        wgrad("ffn_w_up", i, du, "1", x1b, "1")
        wgrad("ffn_w_down", i, hm, "1", dz2, "1", scatter_keys=() if behind_mixer else rest[:1])
        x0 = x0s[i]
        if mix == 0:
            h, glu, cv = saved[i]
            a2_args = (dz1, cv, vec("a_ln_g", j), vec("a_ln_b", j), gw["a_w_pw2", j], i)
            conv_args = (glu, conv_w("a_w_dw", j, 32), i)
            if i == 0:
                early = ffnw[0] + plew[0]
                (dcv, sb, acc), (got,) = _bwd_a2(*a2_args, hosts=[swap(early)])
                sg["a_ln_g", j], sg["a_ln_b", j], sg["a_b_dw", j] = acc[0], acc[1], acc[2]
                add_halves(early, got)
                wgrad("a_w_pw2", j, sb, "c", dz1, "1")
                (dglu, dwdw), (contribs, got) = _bwd_conv_a(
                    dcv, *conv_args, hosts=[_ScatterPartials([parts[key] for key in early]), swap(mixw[0][1:])])
                update(early, contribs)
                add_halves(mixw[0][1:], got)
                sg["a_w_dw", j] = dwdw[:CONV_A]
                (g, dh, acc), (contribs, (g8_early,)) = _bwd_a1(
                    dglu, h, dz1, gw["a_w_pw1", j], i,
                    hosts=[_ScatterPartials([parts[key] for key in mixw[0][1:]]),
                           _Gather8(_pack([sg[pc] for pc in early_small]))])
                update(mixw[0][1:], contribs)
            else:
                dcv, sb, acc = _bwd_a2(*a2_args)
                sg["a_ln_g", j], sg["a_ln_b", j], sg["a_b_dw", j] = acc[0], acc[1], acc[2]
                dglu, dwdw = _bwd_conv_a(dcv, *conv_args)
                wgrad("a_w_pw2", j, sb, "c", dz1, "1")
                sg["a_w_dw", j] = dwdw[:CONV_A]
                g, dh, acc = _bwd_a1(dglu, h, dz1, gw["a_w_pw1", j], i)
            sg["a_b_pw1", j] = acc[0]
            wgrad("a_w_pw1", j, x0, "1", dh, "c")
        elif mix == 1:
            zg, gg = saved[i]
            (g, dh, mb, acc, dw_s, db_s), (contribs,) = _bwd_b(
                dz1, zg, gg, vec("b_ln_g", 0), vec("b_ln_b", 0), gw["b_w_in", 0], gw["b_w_out", 0], ws, wst, bsx,
                hosts=[_ScatterPartials([parts[key] for key in rest])])
            update(rest, contribs)
            sg["b_b_in", 0], sg["b_ln_g", 0], sg["b_ln_b", 0] = acc[0], acc[1, :E], acc[2, :E]
            sg["b_w_s", 0], sg["b_b_s", 0] = dw_s, jnp.sum(db_s, axis=-1)
            wgrad("b_w_out", 0, mb, "c", dz1, "1")
            wgrad("b_w_in", 0, x0, "1", dh, "c")
        else:
            (hc,) = saved[i]
            wc = conv_w("c_w_conv", 0, 8)
            dy, dbg, mb = _bwd_c2(dz1, hc, wc, gw["c_w_out", 0])
            wgrad("c_w_out", 0, mb, "c", dz1, "1")
            g, dhc, dwc = _bwd_c1(dy, hc, dbg, dz1, wc, gw["c_w_in", 0])
            sg["c_w_conv", 0] = dwc[:CONV_C]
            wgrad("c_w_in", 0, x0, "1", dhc, "c")
        pending = mixw[i] + ffnw[i] + plew[i] if i > 0 else mixw[0][:1]
    grad_x = g[None]
    add_halves(pending, _comm_only("swap_last", swap(pending)))
    update(pending, _comm_only("scatter_last", _ScatterPartials([parts[key] for key in pending])))

    g8_late = _gather8("gather_small_late", _pack([sg[pc] for pc in late_small]))
    sums = dict(zip(early_small, _unpack(_sum8("sum8_early", g8_early), [sg[pc].shape for pc in early_small])))
    sums.update(zip(late_small, _unpack(_sum8("sum8_late", g8_late), [sg[pc].shape for pc in late_small])))
    gsum = [jnp.stack([sums[k, l] for l in range(full[k].shape[0])]) for k in small]
    gmine = []
    for k, gs in zip(small, gsum):
        if k in SMALL_SHARDED:
            wdt = wts[k].shape[-1]
            gs = lax.dynamic_slice_in_dim(gs, q_idx * wdt, wdt, axis=gs.ndim - 1)
        gmine.append(gs)
    packed = [_pack(t)[None] for t in ([wts[k] for k in small], [mom[k] for k in small], [var[k] for k in small])]
    outs = _adam("adam_small", packed[0], packed[1], packed[2], _pack(gmine)[None, None], 0, None)
    unpacked = [_unpack(o[0], [wts[k].shape for k in small]) for o in outs]
    for i, k in enumerate(small):
        res[k] = tuple(u[i] for u in unpacked)

    for k in TRANSPOSED:
        res[k] = tuple(jnp.transpose(r, (0, 2, 1)) for r in res[k])
    return (loss, grad_x, *[res[k][0] for k in WEIGHTS], *[res[k][1] for k in WEIGHTS],
            *[res[k][2] for k in WEIGHTS], *[res[k][3] for k in WEIGHTS])
```
